```python
import jax
import jax.numpy as jnp
from jax import lax
import numpy as np

D_MODEL = 1024
BATCH = 8
SEQ = 4096
DEPTH = 4

FFN_DIM = 2816
NORM_EPS = 1e-6
A_HEADS = 8
A_HEAD_DIM = 64
A_WIDTH = A_HEADS * A_HEAD_DIM
ROT_DIM = A_HEAD_DIM // 4
ROPE_THETA = 500000.0
DILATED_PATTERNS = ((128, 1), (512, 4), (2048, 16))
WIN_BLOCK = 128
CONV_CHANNELS = D_MODEL - A_WIDTH
CONV_WIDTH = 31
HYB_IN = 3 * A_WIDTH + 2 * CONV_CHANNELS
GDN_HEADS = 8
GDN_KEY_DIM = 128
GDN_VALUE_DIM = 128
GDN_QK_W = GDN_HEADS * GDN_KEY_DIM
GDN_V_W = GDN_HEADS * GDN_VALUE_DIM
GDN_QKV_W = 2 * GDN_QK_W + GDN_V_W
GDN_SHORT_CONV = 4
GDN_CHUNK = 64
GDN_IN = GDN_QKV_W + GDN_V_W + 2 * GDN_HEADS
N_EVEN = (DEPTH + 1) // 2
N_ODD = DEPTH // 2

kernel_name = "hybrid_dilated_conformer_gdn_trunk"


def rms_norm(x, g):
    xf = x.astype(jnp.float32)
    y = xf * lax.rsqrt(jnp.mean(xf * xf, axis=-1, keepdims=True) + NORM_EPS)
    return (y * g.astype(jnp.float32)).astype(x.dtype)


def layer_norm(x, g, b):
    xf = x.astype(jnp.float32)
    mu = jnp.mean(xf, axis=-1, keepdims=True)
    xc = xf - mu
    y = xc * lax.rsqrt(jnp.mean(xc * xc, axis=-1, keepdims=True) + NORM_EPS)
    return (y * g.astype(jnp.float32) + b.astype(jnp.float32)).astype(x.dtype)


def l2_normalize(x):
    xf = x.astype(jnp.float32)
    return xf * lax.rsqrt(jnp.sum(xf * xf, axis=-1, keepdims=True) + NORM_EPS)


def swiglu_ffn(h, w_in, w_out):
    gate, up = jnp.split(h @ w_in, 2, axis=-1)
    return (jax.nn.silu(gate) * up) @ w_out


def causal_depthwise_conv(x, w):
    width, ch = w.shape
    return lax.conv_general_dilated(
        x, w[:, None, :].astype(x.dtype), window_strides=(1,), padding=[(width - 1, 0)],
        dimension_numbers=("NWC", "WIO", "NWC"), feature_group_count=ch)


def rotary_angles(positions):
    inv_freq = jnp.power(jnp.float32(ROPE_THETA),
                         -jnp.arange(0, ROT_DIM, 2, dtype=jnp.float32) / ROT_DIM)
    ang = positions.astype(jnp.float32)[..., None] * inv_freq
    return jnp.cos(ang)[:, :, None, :], jnp.sin(ang)[:, :, None, :]


def apply_partial_rotary(x, cos, sin):
    half = ROT_DIM // 2
    xr = x[..., :ROT_DIM].astype(jnp.float32)
    x1, x2 = xr[..., :half], xr[..., half:]
    rot = jnp.concatenate([x1 * cos - x2 * sin, x2 * cos + x1 * sin], axis=-1).astype(x.dtype)
    return jnp.concatenate([rot, x[..., ROT_DIM:]], axis=-1)


def dilated_window_branch(q, k, v, window, dilation):
    B, S, H, hd = q.shape
    span = window // dilation
    L = S // dilation
    nb = -(-L // WIN_BLOCK)
    Lp = nb * WIN_BLOCK

    def to_blocks(t):
        t = jnp.moveaxis(t.reshape(B, L, dilation, H, hd), 2, 1)
        t = jnp.pad(t, ((0, 0), (0, 0), (0, Lp - L), (0, 0), (0, 0)))
        return t.reshape(B, dilation, nb, WIN_BLOCK, H, hd)

    def with_prev(t):
        prev = jnp.pad(t, ((0, 0), (0, 0), (1, 0), (0, 0), (0, 0), (0, 0)))[:, :, :nb]
        return jnp.concatenate([prev, t], axis=3)

    qb = to_blocks(q)
    kb = with_prev(to_blocks(k))
    vb = with_prev(to_blocks(v))
    s = jnp.einsum("brnqhd,brnkhd->brnhqk", qb, kb).astype(jnp.float32) * (hd ** -0.5)
    blk = jnp.arange(nb)[:, None, None] * WIN_BLOCK
    q_idx = blk + jnp.arange(WIN_BLOCK)[None, :, None]
    k_idx = blk - WIN_BLOCK + jnp.arange(2 * WIN_BLOCK)[None, None, :]
    dist = q_idx - k_idx
    allowed = (dist >= 0) & (dist <= span) & (k_idx >= 0)
    s = jnp.where(allowed[:, None], s, -jnp.inf)
    m = jnp.max(s, axis=-1, keepdims=True)
    p = jnp.exp(s - m)
    den = jnp.sum(p, axis=-1, keepdims=True)
    o = jnp.einsum("brnhqk,brnkhd->brnqhd", (p / den).astype(v.dtype), vb)
    lse = jnp.swapaxes((m + jnp.log(den))[..., 0], -1, -2)

    def from_blocks(t):
        t = t.reshape((B, dilation, Lp) + t.shape[4:])[:, :, :L]
        return jnp.moveaxis(t, 1, 2).reshape((B, S) + t.shape[3:])

    return from_blocks(o), from_blocks(lse)


def hybrid_mixer(h, cos, sin, w_in, dw_w, dw_b, ln_g, ln_b, w_out):
    B, S, _ = h.shape
    proj = h @ w_in
    heads = lambda t: t.reshape(B, S, A_HEADS, A_HEAD_DIM)
    q = apply_partial_rotary(heads(proj[..., :A_WIDTH]), cos, sin)
    k = apply_partial_rotary(heads(proj[..., A_WIDTH:2 * A_WIDTH]), cos, sin)
    v = heads(proj[..., 2 * A_WIDTH:3 * A_WIDTH])
    outs, lses = [], []
    for window, dilation in DILATED_PATTERNS:
        o_g, lse_g = dilated_window_branch(q, k, v, window, dilation)
        outs.append(o_g)
        lses.append(lse_g)
    mix_w = jax.nn.softmax(jnp.stack(lses), axis=0)
    attn = jnp.einsum("gbsh,gbshd->bshd", mix_w, jnp.stack(outs).astype(jnp.float32))
    attn = attn.reshape(B, S, A_WIDTH).astype(h.dtype)
    u = proj[..., 3 * A_WIDTH:]
    glu = u[..., :CONV_CHANNELS] * jax.nn.sigmoid(u[..., CONV_CHANNELS:])
    c = causal_depthwise_conv(glu, dw_w) + dw_b
    c = jax.nn.silu(layer_norm(c, ln_g, ln_b))
    return jnp.concatenate([attn, c], axis=-1) @ w_out


def gated_delta_rule(q, k, v, g, beta):
    B, S, H, dk = q.shape
    C = GDN_CHUNK
    N = S // C

    def chunked(t):
        return jnp.moveaxis(t.reshape((B, N, C, H) + t.shape[3:]), 3, 1)

    q = chunked(q) * (dk ** -0.5)
    k = chunked(k)
    v = chunked(v)
    beta = chunked(beta)
    g = jnp.cumsum(chunked(g), axis=-1)
    causal = jnp.tril(jnp.ones((C, C), dtype=bool))
    strict = jnp.tril(jnp.ones((C, C), dtype=bool), -1)
    decay = jnp.exp(jnp.where(causal, g[..., :, None] - g[..., None, :], -jnp.inf))
    k_beta = k * beta[..., None]
    l_mat = jnp.where(strict, jnp.einsum("bhnik,bhnjk->bhnij", k_beta, k) * decay, 0.0)
    eye = jnp.eye(C, dtype=q.dtype)
    t_inv = lax.linalg.triangular_solve(l_mat + eye, jnp.broadcast_to(eye, l_mat.shape),
                                        left_side=True, lower=True, unit_diagonal=True)
    u = jnp.einsum("bhnij,bhnjv->bhniv", t_inv, v * beta[..., None])
    w = jnp.einsum("bhnij,bhnjk->bhnik", t_inv, k_beta * jnp.exp(g)[..., None])
    attn = jnp.where(causal, jnp.einsum("bhnik,bhnjk->bhnij", q, k) * decay, 0.0)
    q_dec = q * jnp.exp(g)[..., None]
    g_last = g[..., -1]
    k_dec = k * jnp.exp(g_last[..., None] - g)[..., None]
    xs = tuple(jnp.moveaxis(t, 2, 0) for t in (w, u, q_dec, k_dec, attn, g_last))

    def step(state, inp):
        w_c, u_c, q_c, k_c, a_c, gl_c = inp
        v_new = u_c - jnp.einsum("bhik,bhkv->bhiv", w_c, state)
        o_c = jnp.einsum("bhik,bhkv->bhiv", q_c, state) + jnp.einsum("bhij,bhjv->bhiv", a_c, v_new)
        state = state * jnp.exp(gl_c)[..., None, None] + jnp.einsum("bhik,bhiv->bhkv", k_c, v_new)
        return state, o_c

    state0 = jnp.zeros((B, H, dk, v.shape[-1]), q.dtype)
    _, o = lax.scan(step, state0, xs)
    return jnp.transpose(o, (1, 0, 3, 2, 4)).reshape(B, S, H, -1)


def gated_deltanet_mixer(h, w_in, conv_w, a_log, dt_bias, norm_g, w_out):
    B, S, _ = h.shape
    proj = h @ w_in
    qkv = jax.nn.silu(causal_depthwise_conv(proj[..., :GDN_QKV_W], conv_w))
    z = proj[..., GDN_QKV_W:GDN_QKV_W + GDN_V_W]
    b = proj[..., GDN_QKV_W + GDN_V_W:GDN_QKV_W + GDN_V_W + GDN_HEADS]
    a = proj[..., GDN_QKV_W + GDN_V_W + GDN_HEADS:]
    q = qkv[..., :GDN_QK_W].reshape(B, S, GDN_HEADS, GDN_KEY_DIM)
    k = qkv[..., GDN_QK_W:2 * GDN_QK_W].reshape(B, S, GDN_HEADS, GDN_KEY_DIM)
    v = qkv[..., 2 * GDN_QK_W:].reshape(B, S, GDN_HEADS, GDN_VALUE_DIM)
    beta = jax.nn.sigmoid(b.astype(jnp.float32))
    g = -jnp.exp(a_log.astype(jnp.float32)) * jax.nn.softplus(
        a.astype(jnp.float32) + dt_bias.astype(jnp.float32))
    o = gated_delta_rule(l2_normalize(q), l2_normalize(k), v.astype(jnp.float32), g, beta)
    o = rms_norm(o, norm_g) * jax.nn.silu(
        z.reshape(B, S, GDN_HEADS, GDN_VALUE_DIM).astype(jnp.float32))
    return o.reshape(B, S, GDN_V_W).astype(h.dtype) @ w_out


def _fwd_setup_inputs(seed: int = 0) -> dict:
    key = jax.random.key(seed)
    ks = jax.random.split(key, 24)
    f32 = jnp.float32

    def dense(kk, shape, fan_in):
        return jax.random.normal(kk, shape, f32) * (fan_in ** -0.5)

    def gain(kk, shape):
        return 1.0 + 0.02 * jax.random.normal(kk, shape, f32)

    x = jax.random.normal(ks[0], (BATCH, SEQ, D_MODEL), f32)
    offsets = jax.random.randint(ks[1], (BATCH, 1), 0, 1024, dtype=jnp.int32)
    positions = offsets + jnp.arange(SEQ, dtype=jnp.int32)[None, :]
    dt = jnp.exp(jax.random.uniform(ks[18], (N_ODD, GDN_HEADS), f32,
                                    float(np.log(1e-3)), float(np.log(1e-1))))
    return {
        "x": x,
        "positions": positions,
        "ffn1_norm": gain(ks[2], (DEPTH, D_MODEL)),
        "ffn1_w_in": dense(ks[3], (DEPTH, D_MODEL, 2 * FFN_DIM), D_MODEL),
        "ffn1_w_out": dense(ks[4], (DEPTH, FFN_DIM, D_MODEL), FFN_DIM),
        "mix_norm": gain(ks[5], (DEPTH, D_MODEL)),
        "ffn2_norm": gain(ks[6], (DEPTH, D_MODEL)),
        "ffn2_w_in": dense(ks[7], (DEPTH, D_MODEL, 2 * FFN_DIM), D_MODEL),
        "ffn2_w_out": dense(ks[8], (DEPTH, FFN_DIM, D_MODEL), FFN_DIM),
        "hyb_w_in": dense(ks[9], (N_EVEN, D_MODEL, HYB_IN), D_MODEL),
        "hyb_dw_w": dense(ks[10], (N_EVEN, CONV_WIDTH, CONV_CHANNELS), CONV_WIDTH),
        "hyb_dw_b": 0.02 * jax.random.normal(ks[11], (N_EVEN, CONV_CHANNELS), f32),
        "hyb_ln_g": gain(ks[12], (N_EVEN, CONV_CHANNELS)),
        "hyb_ln_b": 0.02 * jax.random.normal(ks[13], (N_EVEN, CONV_CHANNELS), f32),
        "hyb_w_out": dense(ks[14], (N_EVEN, A_WIDTH + CONV_CHANNELS, D_MODEL), A_WIDTH + CONV_CHANNELS),
        "gdn_w_in": dense(ks[15], (N_ODD, D_MODEL, GDN_IN), D_MODEL),
        "gdn_conv_w": dense(ks[16], (N_ODD, GDN_SHORT_CONV, GDN_QKV_W), GDN_SHORT_CONV),
        "gdn_A_log": jnp.log(jax.random.uniform(ks[17], (N_ODD, GDN_HEADS), f32, 1.0, 16.0)),
        "gdn_dt_bias": dt + jnp.log(-jnp.expm1(-dt)),
        "gdn_norm_g": gain(ks[19], (N_ODD, GDN_VALUE_DIM)),
        "gdn_w_out": dense(ks[20], (N_ODD, GDN_V_W, D_MODEL), GDN_V_W),
        "final_norm": gain(ks[21], (D_MODEL,)),
    }


def _fwd_reference(x, positions, ffn1_norm, ffn1_w_in, ffn1_w_out, mix_norm, ffn2_norm, ffn2_w_in,
              ffn2_w_out, hyb_w_in, hyb_dw_w, hyb_dw_b, hyb_ln_g, hyb_ln_b, hyb_w_out,
              gdn_w_in, gdn_conv_w, gdn_A_log, gdn_dt_bias, gdn_norm_g, gdn_w_out, final_norm):
    cos, sin = rotary_angles(positions)
    h = x
    for layer in range(DEPTH):
        h = h + 0.5 * swiglu_ffn(rms_norm(h, ffn1_norm[layer]), ffn1_w_in[layer], ffn1_w_out[layer])
        hn = rms_norm(h, mix_norm[layer])
        i = layer // 2
        if layer % 2 == 0:
            mix = hybrid_mixer(hn, cos, sin, hyb_w_in[i], hyb_dw_w[i], hyb_dw_b[i],
                               hyb_ln_g[i], hyb_ln_b[i], hyb_w_out[i])
        else:
            mix = gated_deltanet_mixer(hn, gdn_w_in[i], gdn_conv_w[i], gdn_A_log[i],
                                       gdn_dt_bias[i], gdn_norm_g[i], gdn_w_out[i])
        h = h + mix
        h = h + 0.5 * swiglu_ffn(rms_norm(h, ffn2_norm[layer]), ffn2_w_in[layer], ffn2_w_out[layer])
    return rms_norm(h, final_norm)


import jax as _jax
import jax.numpy as _jnp

TWIN_FORMAT = 'train_step'
FWD_PARAMS = ['x', 'positions', 'ffn1_norm', 'ffn1_w_in', 'ffn1_w_out', 'mix_norm', 'ffn2_norm', 'ffn2_w_in', 'ffn2_w_out', 'hyb_w_in', 'hyb_dw_w', 'hyb_dw_b', 'hyb_ln_g', 'hyb_ln_b', 'hyb_w_out', 'gdn_w_in', 'gdn_conv_w', 'gdn_A_log', 'gdn_dt_bias', 'gdn_norm_g', 'gdn_w_out', 'final_norm']
TWIN_WEIGHTS = ['ffn1_norm', 'ffn1_w_in', 'ffn1_w_out', 'mix_norm', 'ffn2_norm', 'ffn2_w_in', 'ffn2_w_out', 'hyb_w_in', 'hyb_dw_w', 'hyb_dw_b', 'hyb_ln_g', 'hyb_ln_b', 'hyb_w_out', 'gdn_w_in', 'gdn_conv_w', 'gdn_A_log', 'gdn_dt_bias', 'gdn_norm_g', 'gdn_w_out', 'final_norm']
TWIN_DIFF_INPUT = 'x'
TWIN_INPUTS = ['x', 'positions', 'ffn1_norm', 'ffn1_w_in', 'ffn1_w_out', 'mix_norm', 'ffn2_norm', 'ffn2_w_in', 'ffn2_w_out', 'hyb_w_in', 'hyb_dw_w', 'hyb_dw_b', 'hyb_ln_g', 'hyb_ln_b', 'hyb_w_out', 'gdn_w_in', 'gdn_conv_w', 'gdn_A_log', 'gdn_dt_bias', 'gdn_norm_g', 'gdn_w_out', 'final_norm', 'loss_target', 'm_ffn1_norm', 'm_ffn1_w_in', 'm_ffn1_w_out', 'm_mix_norm', 'm_ffn2_norm', 'm_ffn2_w_in', 'm_ffn2_w_out', 'm_hyb_w_in', 'm_hyb_dw_w', 'm_hyb_dw_b', 'm_hyb_ln_g', 'm_hyb_ln_b', 'm_hyb_w_out', 'm_gdn_w_in', 'm_gdn_conv_w', 'm_gdn_A_log', 'm_gdn_dt_bias', 'm_gdn_norm_g', 'm_gdn_w_out', 'm_final_norm', 'v_ffn1_norm', 'v_ffn1_w_in', 'v_ffn1_w_out', 'v_mix_norm', 'v_ffn2_norm', 'v_ffn2_w_in', 'v_ffn2_w_out', 'v_hyb_w_in', 'v_hyb_dw_w', 'v_hyb_dw_b', 'v_hyb_ln_g', 'v_hyb_ln_b', 'v_hyb_w_out', 'v_gdn_w_in', 'v_gdn_conv_w', 'v_gdn_A_log', 'v_gdn_dt_bias', 'v_gdn_norm_g', 'v_gdn_w_out', 'v_final_norm']
TWIN_OUTPUTS = ['loss', 'grad_x', 'grad_ffn1_norm', 'grad_ffn1_w_in', 'grad_ffn1_w_out', 'grad_mix_norm', 'grad_ffn2_norm', 'grad_ffn2_w_in', 'grad_ffn2_w_out', 'grad_hyb_w_in', 'grad_hyb_dw_w', 'grad_hyb_dw_b', 'grad_hyb_ln_g', 'grad_hyb_ln_b', 'grad_hyb_w_out', 'grad_gdn_w_in', 'grad_gdn_conv_w', 'grad_gdn_A_log', 'grad_gdn_dt_bias', 'grad_gdn_norm_g', 'grad_gdn_w_out', 'grad_final_norm', 'delta_ffn1_norm', 'delta_ffn1_w_in', 'delta_ffn1_w_out', 'delta_mix_norm', 'delta_ffn2_norm', 'delta_ffn2_w_in', 'delta_ffn2_w_out', 'delta_hyb_w_in', 'delta_hyb_dw_w', 'delta_hyb_dw_b', 'delta_hyb_ln_g', 'delta_hyb_ln_b', 'delta_hyb_w_out', 'delta_gdn_w_in', 'delta_gdn_conv_w', 'delta_gdn_A_log', 'delta_gdn_dt_bias', 'delta_gdn_norm_g', 'delta_gdn_w_out', 'delta_final_norm', 'new_m_ffn1_norm', 'new_m_ffn1_w_in', 'new_m_ffn1_w_out', 'new_m_mix_norm', 'new_m_ffn2_norm', 'new_m_ffn2_w_in', 'new_m_ffn2_w_out', 'new_m_hyb_w_in', 'new_m_hyb_dw_w', 'new_m_hyb_dw_b', 'new_m_hyb_ln_g', 'new_m_hyb_ln_b', 'new_m_hyb_w_out', 'new_m_gdn_w_in', 'new_m_gdn_conv_w', 'new_m_gdn_A_log', 'new_m_gdn_dt_bias', 'new_m_gdn_norm_g', 'new_m_gdn_w_out', 'new_m_final_norm', 'new_v_ffn1_norm', 'new_v_ffn1_w_in', 'new_v_ffn1_w_out', 'new_v_mix_norm', 'new_v_ffn2_norm', 'new_v_ffn2_w_in', 'new_v_ffn2_w_out', 'new_v_hyb_w_in', 'new_v_hyb_dw_w', 'new_v_hyb_dw_b', 'new_v_hyb_ln_g', 'new_v_hyb_ln_b', 'new_v_hyb_w_out', 'new_v_gdn_w_in', 'new_v_gdn_conv_w', 'new_v_gdn_A_log', 'new_v_gdn_dt_bias', 'new_v_gdn_norm_g', 'new_v_gdn_w_out', 'new_v_final_norm']
TWIN_LEAF_KINDS = {'loss': 'loss', 'grad_x': 'grad_x', 'grad_ffn1_norm': 'grad_w', 'grad_ffn1_w_in': 'grad_w', 'grad_ffn1_w_out': 'grad_w', 'grad_mix_norm': 'grad_w', 'grad_ffn2_norm': 'grad_w', 'grad_ffn2_w_in': 'grad_w', 'grad_ffn2_w_out': 'grad_w', 'grad_hyb_w_in': 'grad_w', 'grad_hyb_dw_w': 'grad_w', 'grad_hyb_dw_b': 'grad_w', 'grad_hyb_ln_g': 'grad_w', 'grad_hyb_ln_b': 'grad_w', 'grad_hyb_w_out': 'grad_w', 'grad_gdn_w_in': 'grad_w', 'grad_gdn_conv_w': 'grad_w', 'grad_gdn_A_log': 'grad_w', 'grad_gdn_dt_bias': 'grad_w', 'grad_gdn_norm_g': 'grad_w', 'grad_gdn_w_out': 'grad_w', 'grad_final_norm': 'grad_w', 'delta_ffn1_norm': 'delta_w', 'delta_ffn1_w_in': 'delta_w', 'delta_ffn1_w_out': 'delta_w', 'delta_mix_norm': 'delta_w', 'delta_ffn2_norm': 'delta_w', 'delta_ffn2_w_in': 'delta_w', 'delta_ffn2_w_out': 'delta_w', 'delta_hyb_w_in': 'delta_w', 'delta_hyb_dw_w': 'delta_w', 'delta_hyb_dw_b': 'delta_w', 'delta_hyb_ln_g': 'delta_w', 'delta_hyb_ln_b': 'delta_w', 'delta_hyb_w_out': 'delta_w', 'delta_gdn_w_in': 'delta_w', 'delta_gdn_conv_w': 'delta_w', 'delta_gdn_A_log': 'delta_w', 'delta_gdn_dt_bias': 'delta_w', 'delta_gdn_norm_g': 'delta_w', 'delta_gdn_w_out': 'delta_w', 'delta_final_norm': 'delta_w', 'new_m_ffn1_norm': 'new_m', 'new_m_ffn1_w_in': 'new_m', 'new_m_ffn1_w_out': 'new_m', 'new_m_mix_norm': 'new_m', 'new_m_ffn2_norm': 'new_m', 'new_m_ffn2_w_in': 'new_m', 'new_m_ffn2_w_out': 'new_m', 'new_m_hyb_w_in': 'new_m', 'new_m_hyb_dw_w': 'new_m', 'new_m_hyb_dw_b': 'new_m', 'new_m_hyb_ln_g': 'new_m', 'new_m_hyb_ln_b': 'new_m', 'new_m_hyb_w_out': 'new_m', 'new_m_gdn_w_in': 'new_m', 'new_m_gdn_conv_w': 'new_m', 'new_m_gdn_A_log': 'new_m', 'new_m_gdn_dt_bias': 'new_m', 'new_m_gdn_norm_g': 'new_m', 'new_m_gdn_w_out': 'new_m', 'new_m_final_norm': 'new_m', 'new_v_ffn1_norm': 'new_v', 'new_v_ffn1_w_in': 'new_v', 'new_v_ffn1_w_out': 'new_v', 'new_v_mix_norm': 'new_v', 'new_v_ffn2_norm': 'new_v', 'new_v_ffn2_w_in': 'new_v', 'new_v_ffn2_w_out': 'new_v', 'new_v_hyb_w_in': 'new_v', 'new_v_hyb_dw_w': 'new_v', 'new_v_hyb_dw_b': 'new_v', 'new_v_hyb_ln_g': 'new_v', 'new_v_hyb_ln_b': 'new_v', 'new_v_hyb_w_out': 'new_v', 'new_v_gdn_w_in': 'new_v', 'new_v_gdn_conv_w': 'new_v', 'new_v_gdn_A_log': 'new_v', 'new_v_gdn_dt_bias': 'new_v', 'new_v_gdn_norm_g': 'new_v', 'new_v_gdn_w_out': 'new_v', 'new_v_final_norm': 'new_v'}


def _forward(args):
    return _fwd_reference(*[args[k] for k in FWD_PARAMS])


def _output_shape():
    out = _jax.eval_shape(lambda: _forward(_fwd_setup_inputs(0)))
    return out.shape, out.dtype

N_MICROBATCH = 1
ADAM_LR = 0.001
ADAM_B1 = 0.9
ADAM_B2 = 0.999
ADAM_EPS = 1e-08
ADAM_WD = 0.01
ADAM_STEP = 10
PER_EXAMPLE_BATCH_AXIS = {'x': 0, 'positions': 0, 'loss_target': 0}
SHARED_INPUTS = []
_WEIGHT_DTYPES = {'ffn1_norm': _jnp.float32, 'ffn1_w_in': _jnp.float32, 'ffn1_w_out': _jnp.float32, 'mix_norm': _jnp.float32, 'ffn2_norm': _jnp.float32, 'ffn2_w_in': _jnp.float32, 'ffn2_w_out': _jnp.float32, 'hyb_w_in': _jnp.float32, 'hyb_dw_w': _jnp.float32, 'hyb_dw_b': _jnp.float32, 'hyb_ln_g': _jnp.float32, 'hyb_ln_b': _jnp.float32, 'hyb_w_out': _jnp.float32, 'gdn_w_in': _jnp.float32, 'gdn_conv_w': _jnp.float32, 'gdn_A_log': _jnp.float32, 'gdn_dt_bias': _jnp.float32, 'gdn_norm_g': _jnp.float32, 'gdn_w_out': _jnp.float32, 'final_norm': _jnp.float32}
MOMENT_SCALE = {'ffn1_norm': 8.621850e-02, 'ffn1_w_in': 3.641812e-02, 'ffn1_w_out': 5.940221e-02, 'mix_norm': 1.224861e-01, 'ffn2_norm': 7.111497e-02, 'ffn2_w_in': 3.007440e-02, 'ffn2_w_out': 4.912221e-02, 'hyb_w_in': 6.763805e-02, 'hyb_dw_w': 1.258988e-01, 'hyb_dw_b': 2.794607e-01, 'hyb_ln_g': 1.471103e-01, 'hyb_ln_b': 1.395907e-01, 'hyb_w_out': 9.209158e-02, 'gdn_w_in': 7.044816e-02, 'gdn_conv_w': 6.638546e-02, 'gdn_A_log': 5.560438e-01, 'gdn_dt_bias': 5.518161e-01, 'gdn_norm_g': 2.299333e-01, 'gdn_w_out': 8.376647e-02, 'final_norm': 3.205469e+01}


def _to_microbatches(a, axis):
    t = _jnp.moveaxis(a, axis, 0)
    t = t.reshape((N_MICROBATCH, t.shape[0] // N_MICROBATCH) + t.shape[1:])
    return _jnp.moveaxis(t, 1, axis + 1)


def setup_inputs(seed: int = 0) -> dict:
    inp = _fwd_setup_inputs(seed)
    key = _jax.random.fold_in(_jax.random.key(seed), 7919)
    shape, _ = _output_shape()
    out = dict(inp)
    out["loss_target"] = _jax.random.normal(_jax.random.fold_in(key, 0), shape, _jnp.float32)
    for i, name in enumerate(TWIN_WEIGHTS):
        w = inp[name].astype(_jnp.float32)
        if MOMENT_SCALE is None:
            s = _jnp.sqrt(_jnp.mean(_jnp.square(w)) + 1e-30)
        else:
            s = MOMENT_SCALE[name]
        km, kv = _jax.random.split(_jax.random.fold_in(key, i + 1))
        out[name] = w
        out["m_" + name] = s * _jax.random.normal(km, w.shape, _jnp.float32)
        out["v_" + name] = (s * s) * _jax.random.uniform(kv, w.shape, _jnp.float32, 0.5, 1.5)
    if N_MICROBATCH > 1:
        for name, axis in PER_EXAMPLE_BATCH_AXIS.items():
            out[name] = _to_microbatches(out[name], axis)
    return {'x': out['x'], 'positions': out['positions'], 'ffn1_norm': out['ffn1_norm'], 'ffn1_w_in': out['ffn1_w_in'], 'ffn1_w_out': out['ffn1_w_out'], 'mix_norm': out['mix_norm'], 'ffn2_norm': out['ffn2_norm'], 'ffn2_w_in': out['ffn2_w_in'], 'ffn2_w_out': out['ffn2_w_out'], 'hyb_w_in': out['hyb_w_in'], 'hyb_dw_w': out['hyb_dw_w'], 'hyb_dw_b': out['hyb_dw_b'], 'hyb_ln_g': out['hyb_ln_g'], 'hyb_ln_b': out['hyb_ln_b'], 'hyb_w_out': out['hyb_w_out'], 'gdn_w_in': out['gdn_w_in'], 'gdn_conv_w': out['gdn_conv_w'], 'gdn_A_log': out['gdn_A_log'], 'gdn_dt_bias': out['gdn_dt_bias'], 'gdn_norm_g': out['gdn_norm_g'], 'gdn_w_out': out['gdn_w_out'], 'final_norm': out['final_norm'], 'loss_target': out['loss_target'], 'm_ffn1_norm': out['m_ffn1_norm'], 'm_ffn1_w_in': out['m_ffn1_w_in'], 'm_ffn1_w_out': out['m_ffn1_w_out'], 'm_mix_norm': out['m_mix_norm'], 'm_ffn2_norm': out['m_ffn2_norm'], 'm_ffn2_w_in': out['m_ffn2_w_in'], 'm_ffn2_w_out': out['m_ffn2_w_out'], 'm_hyb_w_in': out['m_hyb_w_in'], 'm_hyb_dw_w': out['m_hyb_dw_w'], 'm_hyb_dw_b': out['m_hyb_dw_b'], 'm_hyb_ln_g': out['m_hyb_ln_g'], 'm_hyb_ln_b': out['m_hyb_ln_b'], 'm_hyb_w_out': out['m_hyb_w_out'], 'm_gdn_w_in': out['m_gdn_w_in'], 'm_gdn_conv_w': out['m_gdn_conv_w'], 'm_gdn_A_log': out['m_gdn_A_log'], 'm_gdn_dt_bias': out['m_gdn_dt_bias'], 'm_gdn_norm_g': out['m_gdn_norm_g'], 'm_gdn_w_out': out['m_gdn_w_out'], 'm_final_norm': out['m_final_norm'], 'v_ffn1_norm': out['v_ffn1_norm'], 'v_ffn1_w_in': out['v_ffn1_w_in'], 'v_ffn1_w_out': out['v_ffn1_w_out'], 'v_mix_norm': out['v_mix_norm'], 'v_ffn2_norm': out['v_ffn2_norm'], 'v_ffn2_w_in': out['v_ffn2_w_in'], 'v_ffn2_w_out': out['v_ffn2_w_out'], 'v_hyb_w_in': out['v_hyb_w_in'], 'v_hyb_dw_w': out['v_hyb_dw_w'], 'v_hyb_dw_b': out['v_hyb_dw_b'], 'v_hyb_ln_g': out['v_hyb_ln_g'], 'v_hyb_ln_b': out['v_hyb_ln_b'], 'v_hyb_w_out': out['v_hyb_w_out'], 'v_gdn_w_in': out['v_gdn_w_in'], 'v_gdn_conv_w': out['v_gdn_conv_w'], 'v_gdn_A_log': out['v_gdn_A_log'], 'v_gdn_dt_bias': out['v_gdn_dt_bias'], 'v_gdn_norm_g': out['v_gdn_norm_g'], 'v_gdn_w_out': out['v_gdn_w_out'], 'v_final_norm': out['v_final_norm']}


def _loss(weights, diff, rest, loss_target):
    with _jax.named_scope("forward"):
        args = {**rest, TWIN_DIFF_INPUT: diff, **{k: w.astype(_WEIGHT_DTYPES[k]) for k, w in weights.items()}}
        y = _forward(args)
    with _jax.named_scope("loss_head"):
        err = _jnp.square(y.astype(_jnp.float32) - loss_target)
        return 0.5 * _jnp.sum(_jnp.mean(err, axis=-1)) if err.ndim else 0.5 * err


def _adamw(w, g, m, v):
    m = ADAM_B1 * m + (1.0 - ADAM_B1) * g
    v = ADAM_B2 * v + (1.0 - ADAM_B2) * _jnp.square(g)
    m_hat = m / (1.0 - ADAM_B1 ** ADAM_STEP)
    v_hat = v / (1.0 - ADAM_B2 ** ADAM_STEP)
    delta = -ADAM_LR * (m_hat / (_jnp.sqrt(v_hat) + ADAM_EPS) + ADAM_WD * w)
    return delta, m, v


def reference(x, positions, ffn1_norm, ffn1_w_in, ffn1_w_out, mix_norm, ffn2_norm, ffn2_w_in, ffn2_w_out, hyb_w_in, hyb_dw_w, hyb_dw_b, hyb_ln_g, hyb_ln_b, hyb_w_out, gdn_w_in, gdn_conv_w, gdn_A_log, gdn_dt_bias, gdn_norm_g, gdn_w_out, final_norm, loss_target, m_ffn1_norm, m_ffn1_w_in, m_ffn1_w_out, m_mix_norm, m_ffn2_norm, m_ffn2_w_in, m_ffn2_w_out, m_hyb_w_in, m_hyb_dw_w, m_hyb_dw_b, m_hyb_ln_g, m_hyb_ln_b, m_hyb_w_out, m_gdn_w_in, m_gdn_conv_w, m_gdn_A_log, m_gdn_dt_bias, m_gdn_norm_g, m_gdn_w_out, m_final_norm, v_ffn1_norm, v_ffn1_w_in, v_ffn1_w_out, v_mix_norm, v_ffn2_norm, v_ffn2_w_in, v_ffn2_w_out, v_hyb_w_in, v_hyb_dw_w, v_hyb_dw_b, v_hyb_ln_g, v_hyb_ln_b, v_hyb_w_out, v_gdn_w_in, v_gdn_conv_w, v_gdn_A_log, v_gdn_dt_bias, v_gdn_norm_g, v_gdn_w_out, v_final_norm):
    given = dict(x=x, positions=positions, ffn1_norm=ffn1_norm, ffn1_w_in=ffn1_w_in, ffn1_w_out=ffn1_w_out, mix_norm=mix_norm, ffn2_norm=ffn2_norm, ffn2_w_in=ffn2_w_in, ffn2_w_out=ffn2_w_out, hyb_w_in=hyb_w_in, hyb_dw_w=hyb_dw_w, hyb_dw_b=hyb_dw_b, hyb_ln_g=hyb_ln_g, hyb_ln_b=hyb_ln_b, hyb_w_out=hyb_w_out, gdn_w_in=gdn_w_in, gdn_conv_w=gdn_conv_w, gdn_A_log=gdn_A_log, gdn_dt_bias=gdn_dt_bias, gdn_norm_g=gdn_norm_g, gdn_w_out=gdn_w_out, final_norm=final_norm, loss_target=loss_target, m_ffn1_norm=m_ffn1_norm, m_ffn1_w_in=m_ffn1_w_in, m_ffn1_w_out=m_ffn1_w_out, m_mix_norm=m_mix_norm, m_ffn2_norm=m_ffn2_norm, m_ffn2_w_in=m_ffn2_w_in, m_ffn2_w_out=m_ffn2_w_out, m_hyb_w_in=m_hyb_w_in, m_hyb_dw_w=m_hyb_dw_w, m_hyb_dw_b=m_hyb_dw_b, m_hyb_ln_g=m_hyb_ln_g, m_hyb_ln_b=m_hyb_ln_b, m_hyb_w_out=m_hyb_w_out, m_gdn_w_in=m_gdn_w_in, m_gdn_conv_w=m_gdn_conv_w, m_gdn_A_log=m_gdn_A_log, m_gdn_dt_bias=m_gdn_dt_bias, m_gdn_norm_g=m_gdn_norm_g, m_gdn_w_out=m_gdn_w_out, m_final_norm=m_final_norm, v_ffn1_norm=v_ffn1_norm, v_ffn1_w_in=v_ffn1_w_in, v_ffn1_w_out=v_ffn1_w_out, v_mix_norm=v_mix_norm, v_ffn2_norm=v_ffn2_norm, v_ffn2_w_in=v_ffn2_w_in, v_ffn2_w_out=v_ffn2_w_out, v_hyb_w_in=v_hyb_w_in, v_hyb_dw_w=v_hyb_dw_w, v_hyb_dw_b=v_hyb_dw_b, v_hyb_ln_g=v_hyb_ln_g, v_hyb_ln_b=v_hyb_ln_b, v_hyb_w_out=v_hyb_w_out, v_gdn_w_in=v_gdn_w_in, v_gdn_conv_w=v_gdn_conv_w, v_gdn_A_log=v_gdn_A_log, v_gdn_dt_bias=v_gdn_dt_bias, v_gdn_norm_g=v_gdn_norm_g, v_gdn_w_out=v_gdn_w_out, v_final_norm=v_final_norm)
    weights = {n: given[n] for n in TWIN_WEIGHTS}
    shared = {n: given[n] for n in SHARED_INPUTS}
    per_example = {n: given[n] for n in ['x', 'positions']}
    grad_fn = _jax.value_and_grad(_loss, argnums=(0, 1))

    def one_microbatch(ex, loss_target):
        ex = dict(ex)
        diff = ex.pop(TWIN_DIFF_INPUT)
        return grad_fn(weights, diff, {**shared, **ex}, loss_target)

    if N_MICROBATCH == 1:
        loss, (grad_w, grad_x) = one_microbatch(per_example, given["loss_target"])
    else:
        def body(carry, xs):
            loss_sum, grad_sum = carry
            l_k, (gw_k, gx_k) = one_microbatch(xs[0], xs[1])
            with _jax.named_scope("update"):
                return (loss_sum + l_k, _jax.tree.map(_jnp.add, grad_sum, gw_k)), gx_k

        init = (_jnp.zeros((), _jnp.float32), _jax.tree.map(_jnp.zeros_like, weights))
        (loss, grad_w), grad_x = _jax.lax.scan(body, init, (per_example, given["loss_target"]))
    with _jax.named_scope("update"):
        delta_w, new_m, new_v = {}, {}, {}
        for n in TWIN_WEIGHTS:
            delta_w[n], new_m[n], new_v[n] = _adamw(weights[n], grad_w[n], given["m_" + n], given["v_" + n])
    return (loss, grad_x, *[grad_w[n] for n in TWIN_WEIGHTS], *[delta_w[n] for n in TWIN_WEIGHTS],
            *[new_m[n] for n in TWIN_WEIGHTS], *[new_v[n] for n in TWIN_WEIGHTS])
```

```python
import functools
import math

import jax
import jax.numpy as jnp
import numpy as np
from jax import lax
from jax.experimental import pallas as pl
from jax.experimental.pallas import tpu as pltpu

F32 = jnp.float32
BF16 = jnp.bfloat16

N_DEV = 8
T = 4096
D = 1024
DEPTH = 4
FFN = 2816
FFN_SHARD = 2 * FFN // N_DEV
FFN_TILES = FFN // FFN_SHARD
EPS = 1e-6

A_HEADS = 8
A_HD = 64
A_W = 512
CONV_C = 512
CONV_K = 31
HYB_IN = 2560
ROPE_THETA = 500000.0
ROT = 16
DILATIONS = (1, 4, 16)
BLK = 128
KPAD = 2048

G_HEADS = 8
G_DK = 128
G_QKV = 3072
G_IN = 4112
G_CONV = 4
CH = 64
GRP = 512
CPG = GRP // CH
N_GRP = T // GRP

ADAM_LR = 0.001
ADAM_B1 = 0.9
ADAM_B2 = 0.999
ADAM_EPS = 1e-08
ADAM_WD = 0.01
ADAM_STEP = 10

VMEM_LIMIT = 56 * 1024 * 1024

HI = lax.Precision.HIGHEST


def _cp(*sem):
    return pltpu.CompilerParams(dimension_semantics=sem, vmem_limit_bytes=VMEM_LIMIT)


def _dot(a, b):
    return jnp.dot(a, b, preferred_element_type=F32)


def _dot_nt(a, b):
    return lax.dot_general(a, b, (((1,), (1,)), ((), ())), preferred_element_type=F32)


def _dot_tn(a, b):
    return lax.dot_general(a, b, (((0,), (0,)), ((), ())), preferred_element_type=F32)


def _sigmoid(x):
    return 1.0 / (1.0 + jnp.exp(-x))


def _dsilu(x, sig):
    return sig * (1.0 + x * (1.0 - sig))


def _rms(x, g):
    rstd = lax.rsqrt(jnp.mean(x * x, axis=-1, keepdims=True) + EPS)
    return x * rstd * g


FFN_TT = 512


def ffn_fwd(h, g_row, w_in, w_out, layer, tag=""):
    def body(h_ref, g_ref, win_ref, wout_ref, hnew_ref, hn_ref, a_ref, b_ref, acc_ref, hns_ref):
        j = pl.program_id(1)

        @pl.when(j == 0)
        def _():
            hn = _rms(h_ref[...], g_ref[...]).astype(BF16)
            hns_ref[...] = hn
            hn_ref[...] = hn
            acc_ref[...] = jnp.zeros_like(acc_ref)

        hn = hns_ref[...]
        a = _dot(hn, win_ref[0])
        b = _dot(hn, win_ref[1])
        act = a * _sigmoid(a) * b
        acc_ref[...] += _dot(act.astype(BF16), wout_ref[...].reshape(FFN_SHARD, D))
        a_ref[...] = a.astype(BF16)
        b_ref[...] = b.astype(BF16)

        @pl.when(j == FFN_TILES - 1)
        def _():
            hnew_ref[...] = h_ref[...] + 0.5 * acc_ref[...]

    tt = FFN_TT
    return pl.pallas_call(
        body,
        grid=(T // tt, FFN_TILES),
        in_specs=[
            pl.BlockSpec((tt, D), lambda i, j: (i, 0)),
            pl.BlockSpec((1, D), lambda i, j: (0, 0)),
            pl.BlockSpec((2, None, None, D, FFN_SHARD), lambda i, j: (0, j, layer, 0, 0)),
            pl.BlockSpec((2, None, FFN_SHARD // 2, D), lambda i, j: (j, layer, 0, 0)),
        ],
        out_specs=[
            pl.BlockSpec((tt, D), lambda i, j: (i, 0)),
            pl.BlockSpec((tt, D), lambda i, j: (i, 0)),
            pl.BlockSpec((None, tt, FFN_SHARD), lambda i, j: (j, i, 0)),
            pl.BlockSpec((None, tt, FFN_SHARD), lambda i, j: (j, i, 0)),
        ],
        out_shape=[
            jax.ShapeDtypeStruct((T, D), F32),
            jax.ShapeDtypeStruct((T, D), BF16),
            jax.ShapeDtypeStruct((FFN_TILES, T, FFN_SHARD), BF16),
            jax.ShapeDtypeStruct((FFN_TILES, T, FFN_SHARD), BF16),
        ],
        scratch_shapes=[pltpu.VMEM((tt, D), F32), pltpu.VMEM((tt, D), BF16)],
        compiler_params=_cp("parallel", "arbitrary"),
        name=f"ffn{tag}_fwd_{layer}",
    )(h, g_row, w_in, w_out)


def ffn_bwd(hn, a, b, dres, w_in, w_out, layer, tag=""):
    tt = FFN_TT
    nt = T // tt

    def body(hn_ref, a_ref, b_ref, dres_ref, win_ref, wout_ref, dhn_ref, dwin_ref, dwout_ref, gin_ref, gout_ref):
        i = pl.program_id(1)
        do = (0.5 * dres_ref[...]).astype(BF16)
        wo = wout_ref[...].reshape(FFN_SHARD, D)
        dact = _dot_nt(do, wo)
        a = a_ref[...].astype(F32)
        b = b_ref[...].astype(F32)
        sig = _sigmoid(a)
        s = a * sig
        act = (s * b).astype(BF16)
        db = (dact * s).astype(BF16)
        da = (dact * b * _dsilu(a, sig)).astype(BF16)
        hn = hn_ref[...]
        gwo = _dot_tn(act, do)
        gwg = _dot_tn(hn, da)
        gwu = _dot_tn(hn, db)

        @pl.when(i == 0)
        def _():
            gout_ref[...] = gwo
            gin_ref[0] = gwg
            gin_ref[1] = gwu

        @pl.when(i > 0)
        def _():
            gout_ref[...] += gwo
            gin_ref[0] += gwg
            gin_ref[1] += gwu

        dhn_ref[...] = _dot_nt(da, win_ref[0]) + _dot_nt(db, win_ref[1])

        @pl.when(i == nt - 1)
        def _():
            dwin_ref[...] = gin_ref[...].astype(BF16)
            dwout_ref[...] = gout_ref[...].astype(BF16)

    return pl.pallas_call(
        body,
        grid=(FFN_TILES, nt),
        in_specs=[
            pl.BlockSpec((tt, D), lambda j, i: (i, 0)),
            pl.BlockSpec((None, tt, FFN_SHARD), lambda j, i: (j, i, 0)),
            pl.BlockSpec((None, tt, FFN_SHARD), lambda j, i: (j, i, 0)),
            pl.BlockSpec((tt, D), lambda j, i: (i, 0)),
            pl.BlockSpec((2, None, None, D, FFN_SHARD), lambda j, i: (0, j, layer, 0, 0)),
            pl.BlockSpec((2, None, FFN_SHARD // 2, D), lambda j, i: (j, layer, 0, 0)),
        ],
        out_specs=[
            pl.BlockSpec((None, tt, D), lambda j, i: (j, i, 0)),
            pl.BlockSpec((2, None, D, FFN_SHARD), lambda j, i: (0, j, 0, 0)),
            pl.BlockSpec((None, FFN_SHARD, D), lambda j, i: (j, 0, 0)),
        ],
        out_shape=[
            jax.ShapeDtypeStruct((FFN_TILES, T, D), F32),
            jax.ShapeDtypeStruct((2, FFN_TILES, D, FFN_SHARD), BF16),
            jax.ShapeDtypeStruct((FFN_TILES, FFN_SHARD, D), BF16),
        ],
        scratch_shapes=[pltpu.VMEM((2, D, FFN_SHARD), F32), pltpu.VMEM((FFN_SHARD, D), F32)],
        compiler_params=_cp("parallel", "arbitrary"),
        name=f"ffn{tag}_bwd_{layer}",
    )(hn, a, b, dres, w_in, w_out)


def _rms_bwd(x, g, dy):
    rstd = lax.rsqrt(jnp.mean(x * x, axis=-1, keepdims=True) + EPS)
    xh = x * rstd
    u = dy * g
    dx = rstd * (u - xh * jnp.mean(u * xh, axis=-1, keepdims=True))
    return dx, jnp.sum(dy * xh, axis=0, keepdims=True)


def norm_bwd(x, g_row, dy_parts, dres, name):
    p = dy_parts.shape[0]
    tt = 512

    def body(x_ref, g_ref, dy_ref, dres_ref, out_ref, dg_ref):
        i = pl.program_id(0)
        dy = dy_ref[0]
        for q in range(1, p):
            dy = dy + dy_ref[q]
        dx, dg = _rms_bwd(x_ref[...], g_ref[...], dy)
        out_ref[...] = dres_ref[...] + dx

        @pl.when(i == 0)
        def _():
            dg_ref[...] = dg

        @pl.when(i > 0)
        def _():
            dg_ref[...] += dg

    return pl.pallas_call(
        body,
        grid=(T // tt,),
        in_specs=[
            pl.BlockSpec((tt, D), lambda i: (i, 0)),
            pl.BlockSpec((1, D), lambda i: (0, 0)),
            pl.BlockSpec((p, tt, D), lambda i: (0, i, 0)),
            pl.BlockSpec((tt, D), lambda i: (i, 0)),
        ],
        out_specs=[pl.BlockSpec((tt, D), lambda i: (i, 0)), pl.BlockSpec((1, D), lambda i: (0, 0))],
        out_shape=[jax.ShapeDtypeStruct((T, D), F32), jax.ShapeDtypeStruct((1, D), F32)],
        compiler_params=_cp("arbitrary"),
        name=name,
    )(x, g_row, dy_parts, dres)


def final_loss(h, g_row, target):
    tt = 512

    def body(h_ref, g_ref, t_ref, dres_ref, dg_ref, loss_ref):
        i = pl.program_id(0)
        x = h_ref[...]
        g = g_ref[...]
        err = _rms(x, g) - t_ref[...]
        part = 0.5 * jnp.sum(jnp.mean(err * err, axis=-1, keepdims=True), axis=0, keepdims=True)
        dx, dg = _rms_bwd(x, g, err * (1.0 / D))
        dres_ref[...] = dx
        part = jnp.broadcast_to(part, loss_ref.shape)

        @pl.when(i == 0)
        def _():
            dg_ref[...] = dg
            loss_ref[...] = part

        @pl.when(i > 0)
        def _():
            dg_ref[...] += dg
            loss_ref[...] += part

    return pl.pallas_call(
        body,
        grid=(T // tt,),
        in_specs=[
            pl.BlockSpec((tt, D), lambda i: (i, 0)),
            pl.BlockSpec((1, D), lambda i: (0, 0)),
            pl.BlockSpec((tt, D), lambda i: (i, 0)),
        ],
        out_specs=[
            pl.BlockSpec((tt, D), lambda i: (i, 0)),
            pl.BlockSpec((1, D), lambda i: (0, 0)),
            pl.BlockSpec((8, 128), lambda i: (0, 0)),
        ],
        out_shape=[
            jax.ShapeDtypeStruct((T, D), F32),
            jax.ShapeDtypeStruct((1, D), F32),
            jax.ShapeDtypeStruct((8, 128), F32),
        ],
        compiler_params=_cp("arbitrary"),
        name="final_loss",
    )(h, g_row, target)


PROJ_TT = 256


def rope_tables(pos_col, invf_row):
    tt = 512

    def body(p_ref, f_ref, c_ref, sm_ref, sp_ref):
        ang = p_ref[...].astype(F32) * f_ref[...]
        lane = lax.broadcasted_iota(jnp.int32, ang.shape, 1) % A_HD
        cs = jnp.cos(ang)
        sn = jnp.sin(ang)
        c_ref[...] = jnp.where(lane < ROT, cs, 1.0)
        sm_ref[...] = jnp.where(lane < ROT // 2, -sn, 0.0)
        sp_ref[...] = jnp.where((lane >= ROT // 2) & (lane < ROT), sn, 0.0)

    spec = pl.BlockSpec((tt, 128), lambda i: (i, 0))
    return pl.pallas_call(
        body,
        grid=(T // tt,),
        in_specs=[pl.BlockSpec((tt, 1), lambda i: (i, 0)), pl.BlockSpec((1, 128), lambda i: (0, 0))],
        out_specs=[spec, spec, spec],
        out_shape=[jax.ShapeDtypeStruct((T, 128), F32)] * 3,
        compiler_params=_cp("parallel"),
        name="rope_tables",
    )(pos_col, invf_row)


def make_rope(positions):
    inv_freq = jnp.power(jnp.float32(ROPE_THETA), -jnp.arange(0, ROT, 2, dtype=F32) / ROT)
    per_head = jnp.concatenate([inv_freq, inv_freq, jnp.zeros((A_HD - ROT,), F32)])
    invf_row = jnp.tile(per_head, 2)[None, :]
    return tuple(rope_tables(positions.reshape(T, 1), invf_row))


def _rope(x, c, sm, sp):
    return x * c + pltpu.roll(x, 128 - ROT // 2, 1) * sm + pltpu.roll(x, ROT // 2, 1) * sp


def _rope_t(dy, c, sm, sp):
    return dy * c + pltpu.roll(dy * sm, ROT // 2, 1) + pltpu.roll(dy * sp, 128 - ROT // 2, 1)


def proj_fwd(h, g_row, w, splits, name, rope=None):
    tt = PROJ_TT
    n = w.shape[1]
    n_rope = 0 if rope is None else 3

    def body(h_ref, g_ref, w_ref, *rest):
        tabs = rest[:n_rope]
        hn_ref = rest[n_rope]
        outs = rest[n_rope + 1:]
        hn = _rms(h_ref[...], g_ref[...]).astype(BF16)
        hn_ref[...] = hn
        for k, ((st, wd), o_ref) in enumerate(zip(splits, outs)):
            if rope is not None and k == 0:
                c, sm, sp = (t[...] for t in tabs)
                for gi in range(wd // 128):
                    r = _dot(hn, w_ref[:, st + 128 * gi:st + 128 * (gi + 1)])
                    if gi < 2 * A_W // 128:
                        r = _rope(r, c, sm, sp)
                    o_ref[:, 128 * gi:128 * (gi + 1)] = r
            else:
                o_ref[...] = _dot(hn, w_ref[:, st:st + wd])

    tab_specs = [pl.BlockSpec((tt, 128), lambda i: (i, 0))] * n_rope
    return pl.pallas_call(
        body,
        grid=(T // tt,),
        in_specs=[
            pl.BlockSpec((tt, D), lambda i: (i, 0)),
            pl.BlockSpec((1, D), lambda i: (0, 0)),
            pl.BlockSpec((D, n), lambda i: (0, 0)),
        ] + tab_specs,
        out_specs=[pl.BlockSpec((tt, D), lambda i: (i, 0))]
        + [pl.BlockSpec((tt, wd), lambda i: (i, 0)) for _, wd in splits],
        out_shape=[jax.ShapeDtypeStruct((T, D), BF16)]
        + [jax.ShapeDtypeStruct((T, wd), F32) for _, wd in splits],
        compiler_params=_cp("parallel"),
        name=name,
    )(h, g_row, w, *(rope or ()))


def proj_bwd_data(x, g_row, w, dparts, splits, dres, name, rope=None, n_rot=0):
    tt = PROJ_TT
    n = w.shape[1]
    n_rope = 0 if rope is None else 3
    k_parts = len(dparts)

    def body(x_ref, g_ref, w_ref, dres_ref, *rest):
        d_refs = rest[:k_parts]
        tabs = rest[k_parts:k_parts + n_rope]
        out_ref, dg_ref = rest[k_parts + n_rope:k_parts + n_rope + 2]
        unrot_refs = rest[k_parts + n_rope + 2:]
        i = pl.program_id(0)
        dhn = jnp.zeros((tt, D), F32)
        for k, ((st, wd), d_ref) in enumerate(zip(splits, d_refs)):
            if k < n_rot:
                c, sm, sp = (t[...] for t in tabs)
                for gi in range(wd // 128):
                    d = _rope_t(d_ref[:, 128 * gi:128 * (gi + 1)], c, sm, sp)
                    unrot_refs[k][:, 128 * gi:128 * (gi + 1)] = d
                    dhn = dhn + _dot_nt(d.astype(BF16), w_ref[:, st + 128 * gi:st + 128 * (gi + 1)])
            else:
                dhn = dhn + _dot_nt(d_ref[...].astype(BF16), w_ref[:, st:st + wd])
        dx, dg = _rms_bwd(x_ref[...], g_ref[...], dhn)
        out_ref[...] = dres_ref[...] + dx

        @pl.when(i == 0)
        def _():
            dg_ref[...] = dg

        @pl.when(i > 0)
        def _():
            dg_ref[...] += dg

    tab_specs = [pl.BlockSpec((tt, 128), lambda i: (i, 0))] * n_rope
    out_specs = [pl.BlockSpec((tt, D), lambda i: (i, 0)), pl.BlockSpec((1, D), lambda i: (0, 0))]
    out_shape = [jax.ShapeDtypeStruct((T, D), F32), jax.ShapeDtypeStruct((1, D), F32)]
    for k in range(n_rot):
        out_specs.append(pl.BlockSpec((tt, splits[k][1]), lambda i: (i, 0)))
        out_shape.append(jax.ShapeDtypeStruct((T, splits[k][1]), F32))
    return pl.pallas_call(
        body,
        grid=(T // tt,),
        in_specs=[
            pl.BlockSpec((tt, D), lambda i: (i, 0)),
            pl.BlockSpec((1, D), lambda i: (0, 0)),
            pl.BlockSpec((D, n), lambda i: (0, 0)),
            pl.BlockSpec((tt, D), lambda i: (i, 0)),
        ] + [pl.BlockSpec((tt, wd), lambda i: (i, 0)) for _, wd in splits] + tab_specs,
        out_specs=out_specs,
        out_shape=out_shape,
        compiler_params=_cp("arbitrary"),
        name=name,
    )(x, g_row, w, dres, *dparts, *(rope or ()))


def mm_tn(x, d, name):
    k = x.shape[1]
    n = d.shape[1]
    wn = n if n <= 512 else 512
    tt = 512

    def body(x_ref, d_ref, o_ref):
        i = pl.program_id(1)
        r = _dot_tn(x_ref[...], d_ref[...].astype(BF16))

        @pl.when(i == 0)
        def _():
            o_ref[...] = r

        @pl.when(i > 0)
        def _():
            o_ref[...] += r

    return pl.pallas_call(
        body,
        grid=(n // wn, T // tt),
        in_specs=[pl.BlockSpec((tt, k), lambda j, i: (i, 0)), pl.BlockSpec((tt, wn), lambda j, i: (i, j))],
        out_specs=pl.BlockSpec((k, wn), lambda j, i: (0, j)),
        out_shape=jax.ShapeDtypeStruct((k, n), F32),
        compiler_params=_cp("parallel", "arbitrary"),
        name=name,
    )(x, d)


CONV_RC = 128
CONV_PAD = 32


def hyb_conv_fwd(u, dw_w, dw_b, name):
    def body(ua_ref, ug_ref, w_ref, b_ref, o_ref, xpad):
        xpad[0:CONV_PAD, :] = jnp.zeros((CONV_PAD, 128), F32)
        xpad[CONV_PAD:, :] = ua_ref[...] * _sigmoid(ug_ref[...])
        for r in range(T // CONV_RC):
            acc = jnp.broadcast_to(b_ref[...], (CONV_RC, 128))
            for j in range(CONV_K):
                acc = acc + w_ref[pl.ds(j, 1), :] * xpad[pl.ds(r * CONV_RC + CONV_PAD - (CONV_K - 1) + j, CONV_RC), :]
            o_ref[r * CONV_RC:(r + 1) * CONV_RC, :] = acc

    nb = CONV_C // 128
    return pl.pallas_call(
        body,
        grid=(nb,),
        in_specs=[
            pl.BlockSpec((T, 128), lambda c: (0, c)),
            pl.BlockSpec((T, 128), lambda c: (0, nb + c)),
            pl.BlockSpec((32, 128), lambda c: (0, c)),
            pl.BlockSpec((1, 128), lambda c: (0, c)),
        ],
        out_specs=pl.BlockSpec((T, 128), lambda c: (0, c)),
        out_shape=jax.ShapeDtypeStruct((T, CONV_C), F32),
        scratch_shapes=[pltpu.VMEM((T + CONV_PAD, 128), F32)],
        compiler_params=_cp("parallel"),
        name=name,
    )(u, u, dw_w, dw_b)


def hyb_conv_bwd(dc, u, dw_w, name):
    def body(dc_ref, ua_ref, ug_ref, w_ref, da_ref, dgate_ref, dw_ref, db_ref, xpad, dcpad, dwacc):
        ua = ua_ref[...]
        sig = _sigmoid(ug_ref[...])
        xpad[0:CONV_PAD, :] = jnp.zeros((CONV_PAD, 128), F32)
        xpad[CONV_PAD:, :] = ua * sig
        dcpad[0:T, :] = dc_ref[...]
        dcpad[T:, :] = jnp.zeros((CONV_PAD, 128), F32)
        dwacc[...] = jnp.zeros_like(dwacc)
        dbacc = jnp.zeros((8, 128), F32)
        for r in range(T // CONV_RC):
            r0 = r * CONV_RC
            dcr = dc_ref[r0:r0 + CONV_RC, :]
            dbacc = dbacc + dcr.reshape(CONV_RC // 8, 8, 128).sum(axis=0)
            dglu = jnp.zeros((CONV_RC, 128), F32)
            for j in range(CONV_K):
                dglu = dglu + w_ref[pl.ds(j, 1), :] * dcpad[pl.ds(r0 + (CONV_K - 1) - j, CONV_RC), :]
                prod = dcr * xpad[pl.ds(r0 + CONV_PAD - (CONV_K - 1) + j, CONV_RC), :]
                dwacc[8 * j:8 * j + 8, :] += prod.reshape(CONV_RC // 8, 8, 128).sum(axis=0)
            sg = sig[r0:r0 + CONV_RC, :]
            da_ref[r0:r0 + CONV_RC, :] = dglu * sg
            dgate_ref[r0:r0 + CONV_RC, :] = dglu * ua[r0:r0 + CONV_RC, :] * sg * (1.0 - sg)
        for j in range(CONV_K):
            dw_ref[pl.ds(j, 1), :] = jnp.sum(dwacc[8 * j:8 * j + 8, :], axis=0, keepdims=True)
        dw_ref[pl.ds(CONV_K, 1), :] = jnp.zeros((1, 128), F32)
        db_ref[...] = jnp.sum(dbacc, axis=0, keepdims=True)

    nb = CONV_C // 128
    col = pl.BlockSpec((T, 128), lambda c: (0, c))
    return pl.pallas_call(
        body,
        grid=(nb,),
        in_specs=[col, col, pl.BlockSpec((T, 128), lambda c: (0, nb + c)), pl.BlockSpec((32, 128), lambda c: (0, c))],
        out_specs=[col, col, pl.BlockSpec((32, 128), lambda c: (0, c)), pl.BlockSpec((1, 128), lambda c: (0, c))],
        out_shape=[
            jax.ShapeDtypeStruct((T, CONV_C), F32),
            jax.ShapeDtypeStruct((T, CONV_C), F32),
            jax.ShapeDtypeStruct((32, CONV_C), F32),
            jax.ShapeDtypeStruct((1, CONV_C), F32),
        ],
        scratch_shapes=[
            pltpu.VMEM((T + CONV_PAD, 128), F32),
            pltpu.VMEM((T + CONV_PAD, 128), F32),
            pltpu.VMEM((8 * 32, 128), F32),
        ],
        compiler_params=_cp("parallel"),
        name=name,
    )(dc, u, u, dw_w)


ATT_SCALE = A_HD ** -0.5
N_BLK = T // BLK


def _att_masks():
    i = lax.broadcasted_iota(jnp.int32, (BLK, 2 * BLK), 0)
    j = lax.broadcasted_iota(jnp.int32, (BLK, 2 * BLK), 1)
    band = (j >= i) & (j <= i + BLK)
    i1 = lax.broadcasted_iota(jnp.int32, (BLK, BLK), 0)
    j1 = lax.broadcasted_iota(jnp.int32, (BLK, BLK), 1)
    return band, j1 <= i1


def _att_rows(d, t, first):
    if first:
        base = t
        return pl.ds(base, BLK, stride=d), pl.ds(base, BLK, stride=d)
    c = t % d
    n = t // d + 1
    base = c + (BLK * d) * n
    return pl.ds(base, BLK, stride=d), pl.ds(base - BLK * d, 2 * BLK, stride=d)


def attn_fwd(qkv, name):
    def body(q_ref, k_ref, v_ref, o_ref, lse_ref, og, lg):
        band, tri = _att_masks()
        for e in range(2):
            ln = slice(A_HD * e, A_HD * (e + 1))
            for g, d in enumerate(DILATIONS):
                def block(t, carry, first, g=g, d=d):
                    rq, rk = _att_rows(d, t, first)
                    qb = q_ref[rq, :][:, ln].astype(BF16)
                    kb = k_ref[rk, :][:, ln].astype(BF16)
                    vb = v_ref[rk, :][:, ln].astype(BF16)
                    s = _dot_nt(qb, kb) * ATT_SCALE
                    s = jnp.where(tri if first else band, s, -jnp.inf)
                    m = jnp.max(s, axis=-1, keepdims=True)
                    p = jnp.exp(s - m)
                    den = jnp.sum(p, axis=-1, keepdims=True)
                    og[g, rq, :] = _dot((p / den).astype(BF16), vb)
                    lg[g, rq, :] = jnp.broadcast_to(m + jnp.log(den), (BLK, A_HD))
                    return carry

                lax.fori_loop(0, d, functools.partial(block, first=True), 0)
                lax.fori_loop(0, N_BLK - d, functools.partial(block, first=False), 0)
            rc = 512
            for r in range(T // rc):
                rows = pl.ds(r * rc, rc)
                l0, l1, l2 = lg[0, rows, :], lg[1, rows, :], lg[2, rows, :]
                m = jnp.maximum(jnp.maximum(l0, l1), l2)
                e0, e1, e2 = jnp.exp(l0 - m), jnp.exp(l1 - m), jnp.exp(l2 - m)
                z = e0 + e1 + e2
                o_ref[rows, ln] = (e0 / z) * og[0, rows, :] + (e1 / z) * og[1, rows, :] + (e2 / z) * og[2, rows, :]
                lse_ref[rows, ln] = m + jnp.log(z)

    npair = A_HEADS // 2
    col = lambda off: pl.BlockSpec((T, 128), lambda p: (0, off + p))
    return pl.pallas_call(
        body,
        grid=(npair,),
        in_specs=[col(0), col(npair), col(2 * npair)],
        out_specs=[col(0), col(0)],
        out_shape=[jax.ShapeDtypeStruct((T, A_W), F32), jax.ShapeDtypeStruct((T, A_W), F32)],
        scratch_shapes=[pltpu.VMEM((3, T, A_HD), F32), pltpu.VMEM((3, T, A_HD), F32)],
        compiler_params=_cp("parallel"),
        name=name,
    )(qkv, qkv, qkv)


def attn_bwd(qkv, o, lse, do, name):
    def body(q_ref, k_ref, v_ref, o_ref, lse_ref, do_ref, dq_ref, dk_ref, dv_ref, dq_s, dk_s, dv_s):
        band, tri = _att_masks()
        for e in range(2):
            ln = slice(A_HD * e, A_HD * (e + 1))
            dq_s[...] = jnp.zeros_like(dq_s)
            dk_s[...] = jnp.zeros_like(dk_s)
            dv_s[...] = jnp.zeros_like(dv_s)
            for d in DILATIONS:
                def block(t, carry, first, d=d):
                    rq, rk = _att_rows(d, t, first)
                    qb = q_ref[rq, :][:, ln].astype(BF16)
                    kb = k_ref[rk, :][:, ln].astype(BF16)
                    vb = v_ref[rk, :][:, ln].astype(BF16)
                    dof = do_ref[rq, :][:, ln]
                    dob = dof.astype(BF16)
                    l = lse_ref[rq, :][:, A_HD * e:A_HD * e + 1]
                    dd = jnp.sum(dof * o_ref[rq, :][:, ln], axis=-1, keepdims=True)
                    s = _dot_nt(qb, kb) * ATT_SCALE
                    p = jnp.where(tri if first else band, jnp.exp(s - l), 0.0)
                    dp = _dot_nt(dob, vb)
                    ds = (p * (dp - dd) * ATT_SCALE).astype(BF16)
                    dq_s[rq, :] += _dot(ds, kb)
                    dk_s[rk, :] += _dot_tn(ds, qb)
                    dv_s[rk, :] += _dot_tn(p.astype(BF16), dob)
                    return carry

                lax.fori_loop(0, d, functools.partial(block, first=True), 0)
                lax.fori_loop(0, N_BLK - d, functools.partial(block, first=False), 0)
            dq_ref[:, ln] = dq_s[...]
            dk_ref[:, ln] = dk_s[...]
            dv_ref[:, ln] = dv_s[...]

    npair = A_HEADS // 2
    col = lambda off: pl.BlockSpec((T, 128), lambda p: (0, off + p))
    return pl.pallas_call(
        body,
        grid=(npair,),
        in_specs=[col(0), col(npair), col(2 * npair), col(0), col(0), col(0)],
        out_specs=[col(0), col(0), col(0)],
        out_shape=[jax.ShapeDtypeStruct((T, A_W), F32)] * 3,
        scratch_shapes=[pltpu.VMEM((T, A_HD), F32)] * 3,
        compiler_params=_cp("parallel"),
        name=name,
    )(qkv, qkv, qkv, o, lse, do)


def _ln_silu(x, g, b):
    mu = jnp.mean(x, axis=-1, keepdims=True)
    xc = x - mu
    rstd = lax.rsqrt(jnp.mean(xc * xc, axis=-1, keepdims=True) + EPS)
    xh = xc * rstd
    y = xh * g + b
    sig = _sigmoid(y)
    return y * sig, (xh, rstd, y, sig)


def hyb_out_fwd(h, attn, cpre, ln_g, ln_b, w_out, name):
    tt = 512

    def body(h_ref, a_ref, c_ref, g_ref, b_ref, w_ref, hnew_ref, cat_ref):
        cn, _ = _ln_silu(c_ref[...], g_ref[...], b_ref[...])
        ab = a_ref[...].astype(BF16)
        cb = cn.astype(BF16)
        cat_ref[:, 0:A_W] = ab
        cat_ref[:, A_W:D] = cb
        hnew_ref[...] = h_ref[...] + _dot(ab, w_ref[0:A_W, :]) + _dot(cb, w_ref[A_W:D, :])

    half = pl.BlockSpec((tt, A_W), lambda i: (i, 0))
    vec = pl.BlockSpec((1, CONV_C), lambda i: (0, 0))
    full = pl.BlockSpec((tt, D), lambda i: (i, 0))
    return pl.pallas_call(
        body,
        grid=(T // tt,),
        in_specs=[full, half, half, vec, vec, pl.BlockSpec((D, D), lambda i: (0, 0))],
        out_specs=[full, full],
        out_shape=[jax.ShapeDtypeStruct((T, D), F32), jax.ShapeDtypeStruct((T, D), BF16)],
        compiler_params=_cp("parallel"),
        name=name,
    )(h, attn, cpre, ln_g, ln_b, w_out)


def hyb_out_bwd(dres, cpre, ln_g, ln_b, w_out, name):
    tt = 512

    def body(d_ref, c_ref, g_ref, b_ref, w_ref, da_ref, dc_ref, dg_ref, db_ref):
        i = pl.program_id(0)
        db16 = d_ref[...].astype(BF16)
        da_ref[...] = _dot_nt(db16, w_ref[0:A_W, :])
        dcn = _dot_nt(db16, w_ref[A_W:D, :])
        g = g_ref[...]
        _, (xh, rstd, y, sig) = _ln_silu(c_ref[...], g, b_ref[...])
        dy = dcn * _dsilu(y, sig)
        dxh = dy * g
        dc_ref[...] = rstd * (dxh - jnp.mean(dxh, axis=-1, keepdims=True)
                              - xh * jnp.mean(dxh * xh, axis=-1, keepdims=True))
        dg = jnp.sum(dy * xh, axis=0, keepdims=True)
        db = jnp.sum(dy, axis=0, keepdims=True)

        @pl.when(i == 0)
        def _():
            dg_ref[...] = dg
            db_ref[...] = db

        @pl.when(i > 0)
        def _():
            dg_ref[...] += dg
            db_ref[...] += db

    half = pl.BlockSpec((tt, A_W), lambda i: (i, 0))
    vec = pl.BlockSpec((1, CONV_C), lambda i: (0, 0))
    return pl.pallas_call(
        body,
        grid=(T // tt,),
        in_specs=[pl.BlockSpec((tt, D), lambda i: (i, 0)), half, vec, vec, pl.BlockSpec((D, D), lambda i: (0, 0))],
        out_specs=[half, half, vec, vec],
        out_shape=[
            jax.ShapeDtypeStruct((T, A_W), F32),
            jax.ShapeDtypeStruct((T, CONV_C), F32),
            jax.ShapeDtypeStruct((1, CONV_C), F32),
            jax.ShapeDtypeStruct((1, CONV_C), F32),
        ],
        compiler_params=_cp("arbitrary"),
        name=name,
    )(dres, cpre, ln_g, ln_b, w_out)


def hybrid_fwd(h, g_row, w_in, dw_w, dw_b, ln_g, ln_b, w_out, rope, tag):
    hn, qkv, u = proj_fwd(h, g_row, w_in, [(0, 3 * A_W), (3 * A_W, 2 * CONV_C)], f"hyb_proj_{tag}", rope=rope)
    cpre = hyb_conv_fwd(u, dw_w, dw_b, f"hyb_conv_{tag}")
    attn, lse = attn_fwd(qkv, f"attn_fwd_{tag}")
    hnew, cat = hyb_out_fwd(h, attn, cpre, ln_g, ln_b, w_out, f"hyb_out_{tag}")
    return hnew, (h, hn, qkv, u, cpre, attn, lse, cat)


def hybrid_bwd(dres, saved, g_row, w_in, dw_w, ln_g, ln_b, w_out, rope, tag):
    h, hn, qkv, u, cpre, attn, lse, cat = saved
    d_attn, d_cpre, d_lng, d_lnb = hyb_out_bwd(dres, cpre, ln_g, ln_b, w_out, f"hyb_out_bwd_{tag}")
    d_wout = mm_tn(cat, dres, f"hyb_wout_grad_{tag}")
    d_a, d_gate, d_dw, d_db = hyb_conv_bwd(d_cpre, u, dw_w, f"hyb_conv_bwd_{tag}")
    dq, dk, dv = attn_bwd(qkv, attn, lse, d_attn, f"attn_bwd_{tag}")
    splits = [(0, A_W), (A_W, A_W), (2 * A_W, A_W), (3 * A_W, CONV_C), (3 * A_W + CONV_C, CONV_C)]
    dres_new, d_norm, dq_u, dk_u = proj_bwd_data(
        h, g_row, w_in, [dq, dk, dv, d_a, d_gate], splits, dres, f"hyb_proj_bwd_{tag}", rope=rope, n_rot=2)
    parts = [dq_u, dk_u, dv, d_a, d_gate]
    d_win = jnp.concatenate([mm_tn(hn, p, f"hyb_win_grad_{tag}_{k}") for k, p in enumerate(parts)], axis=1)
    return dres_new, dict(norm=d_norm, w_in=d_win, dw_w=d_dw[:CONV_K], dw_b=d_db, ln_g=d_lng, ln_b=d_lnb, w_out=d_wout)


G_SCALE = G_DK ** -0.5
GP_RC = 256
GP_PAD = 8


def gdn_prep_fwd(x, conv_w, name):
    def body(x_ref, w_ref, o_ref, xpad):
        cb = pl.program_id(0)
        xpad[0:GP_PAD, :] = jnp.zeros((GP_PAD, 128), F32)
        xpad[GP_PAD:, :] = x_ref[...]
        for r in range(T // GP_RC):
            r0 = r * GP_RC
            y = jnp.zeros((GP_RC, 128), F32)
            for j in range(G_CONV):
                y = y + w_ref[pl.ds(j, 1), :] * xpad[pl.ds(r0 + GP_PAD - (G_CONV - 1) + j, GP_RC), :]
            s = y * _sigmoid(y)
            n = lax.rsqrt(jnp.sum(s * s, axis=-1, keepdims=True) + EPS)
            o_ref[r0:r0 + GP_RC, :] = s * jnp.where(cb < 2 * G_HEADS, n, 1.0)

    nb = G_QKV // 128
    return pl.pallas_call(
        body,
        grid=(nb,),
        in_specs=[pl.BlockSpec((T, 128), lambda c: (0, c)), pl.BlockSpec((G_CONV, 128), lambda c: (0, c))],
        out_specs=pl.BlockSpec((T, 128), lambda c: (0, c)),
        out_shape=jax.ShapeDtypeStruct((T, G_QKV), F32),
        scratch_shapes=[pltpu.VMEM((T + GP_PAD, 128), F32)],
        compiler_params=_cp("parallel"),
        name=name,
    )(x, conv_w)


def gdn_prep_bwd(dout, x, conv_w, part, l2, name):
    def body(d_ref, x_ref, w_ref, dx_ref, dw_ref, xpad, dypad, dwacc):
        xpad[0:GP_PAD, :] = jnp.zeros((GP_PAD, 128), F32)
        xpad[GP_PAD:, :] = x_ref[...]
        dypad[T:, :] = jnp.zeros((GP_PAD, 128), F32)
        dwacc[...] = jnp.zeros_like(dwacc)
        for r in range(T // GP_RC):
            r0 = r * GP_RC
            y = jnp.zeros((GP_RC, 128), F32)
            xs = []
            for j in range(G_CONV):
                xj = xpad[pl.ds(r0 + GP_PAD - (G_CONV - 1) + j, GP_RC), :]
                xs.append(xj)
                y = y + w_ref[pl.ds(j, 1), :] * xj
            sig = _sigmoid(y)
            s = y * sig
            d = d_ref[r0:r0 + GP_RC, :]
            if l2:
                n = lax.rsqrt(jnp.sum(s * s, axis=-1, keepdims=True) + EPS)
                out = s * n
                d = n * (d - out * jnp.sum(d * out, axis=-1, keepdims=True))
            dy = d * _dsilu(y, sig)
            dypad[r0:r0 + GP_RC, :] = dy
            for j in range(G_CONV):
                dwacc[8 * j:8 * j + 8, :] += (dy * xs[j]).reshape(GP_RC // 8, 8, 128).sum(axis=0)
        for r in range(T // GP_RC):
            r0 = r * GP_RC
            dx = jnp.zeros((GP_RC, 128), F32)
            for j in range(G_CONV):
                dx = dx + w_ref[pl.ds(j, 1), :] * dypad[pl.ds(r0 + (G_CONV - 1) - j, GP_RC), :]
            dx_ref[r0:r0 + GP_RC, :] = dx
        for j in range(G_CONV):
            dw_ref[pl.ds(j, 1), :] = jnp.sum(dwacc[8 * j:8 * j + 8, :], axis=0, keepdims=True)

    nb = G_HEADS
    off = part * nb
    col = pl.BlockSpec((T, 128), lambda c: (0, c))
    return pl.pallas_call(
        body,
        grid=(nb,),
        in_specs=[col, pl.BlockSpec((T, 128), lambda c: (0, off + c)), pl.BlockSpec((G_CONV, 128), lambda c: (0, off + c))],
        out_specs=[col, pl.BlockSpec((G_CONV, 128), lambda c: (0, c))],
        out_shape=[jax.ShapeDtypeStruct((T, G_HEADS * G_DK), F32), jax.ShapeDtypeStruct((G_CONV, G_HEADS * G_DK), F32)],
        scratch_shapes=[
            pltpu.VMEM((T + GP_PAD, 128), F32),
            pltpu.VMEM((T + GP_PAD, 128), F32),
            pltpu.VMEM((8 * G_CONV, 128), F32),
        ],
        compiler_params=_cp("parallel"),
        name=name,
    )(dout, x, conv_w)


def _seg_cumsum(x, reverse=False):
    row = lax.broadcasted_iota(jnp.int32, x.shape, 0) % CH
    s = 1
    while s < CH:
        if reverse:
            x = x + jnp.where(row < CH - s, pltpu.roll(x, x.shape[0] - s, 0), 0.0)
        else:
            x = x + jnp.where(row >= s, pltpu.roll(x, s, 0), 0.0)
        s *= 2
    return x


def _gdn_gates(ba_ref, alog_ref, dt_ref, h):
    ba = ba_ref[...]
    lane = lax.broadcasted_iota(jnp.int32, ba.shape, 1)
    b_col = jnp.sum(jnp.where(lane == h, ba, 0.0), axis=1, keepdims=True)
    a_col = jnp.sum(jnp.where(lane == G_HEADS + h, ba, 0.0), axis=1, keepdims=True)
    lane8 = lax.broadcasted_iota(jnp.int32, (1, G_HEADS), 1)
    alog = jnp.sum(jnp.where(lane8 == h, alog_ref[...], 0.0), axis=1, keepdims=True)
    dt = jnp.sum(jnp.where(lane8 == h, dt_ref[...], 0.0), axis=1, keepdims=True)
    beta = _sigmoid(b_col)
    xa = a_col + dt
    softplus = jnp.maximum(xa, 0.0) + jnp.log(1.0 + jnp.exp(-jnp.abs(xa)))
    ea = jnp.exp(alog)
    return beta, -ea * softplus, xa, ea


def _chunk_masks():
    i = lax.broadcasted_iota(jnp.int32, (CH, CH), 0)
    j = lax.broadcasted_iota(jnp.int32, (CH, CH), 1)
    return i >= j, i > j, i, j


def _decay(gcc, causal):
    gm = gcc[:, 0:CH]
    return jnp.where(causal, jnp.exp(jnp.minimum(gm - gm.T, 0.0)), 0.0)


def _unit_lower_inverse(lm, i, j):
    m = jnp.where(i == j, 1.0, 0.0)
    b = 1
    while b < CH:
        pair = ((i // (2 * b)) == (j // (2 * b))) & ((i // b) % 2 == 1) & ((j // b) % 2 == 0)
        lb = jnp.where(pair, lm, 0.0)
        if b == 1:
            m = m - lb
        else:
            m = m - jnp.dot(jnp.dot(m, lb, precision=HI, preferred_element_type=F32), m,
                            precision=HI, preferred_element_type=F32)
        b *= 2
    return m


def gdn_local_fwd(qkv, ba, alog, dtb, name):
    def body(q_ref, k_ref, v_ref, ba_ref, al_ref, dt_ref, u_ref, w_ref, qd_ref, kd_ref, at_ref, el_ref, ti_ref, gcs):
        h = pl.program_id(1)
        beta, g, _, _ = _gdn_gates(ba_ref, al_ref, dt_ref, h)
        gc = _seg_cumsum(jnp.broadcast_to(g, (GRP, 128)))
        gcs[...] = gc
        causal, strict, i, j = _chunk_masks()
        for c in range(CPG):
            r = slice(c * CH, (c + 1) * CH)
            q, k, v = q_ref[r, :], k_ref[r, :], v_ref[r, :]
            bt = beta[r, :]
            gcc = gc[r, :]
            ec = jnp.exp(gcc)
            gl = gcs[pl.ds(c * CH + CH - 1, 1), :]
            dm = _decay(gcc, causal)
            kb = k * bt
            kbf = k.astype(BF16)
            a1 = _dot_nt(kb.astype(BF16), kbf)
            tinv = _unit_lower_inverse(jnp.where(strict, a1 * dm, 0.0), i, j)
            tb = tinv.astype(BF16)
            qs = q * G_SCALE
            u_ref[r, :] = _dot(tb, (v * bt).astype(BF16))
            w_ref[r, :] = _dot(tb, (kb * ec).astype(BF16)).astype(BF16)
            qd_ref[r, :] = (qs * ec).astype(BF16)
            kd_ref[r, :] = (k * jnp.exp(gl - gcc)).astype(BF16)
            at_ref[r, :] = (_dot_nt(qs.astype(BF16), kbf) * dm).astype(BF16)
            el_ref[pl.ds(c, 1), :] = jnp.exp(gl)
            ti_ref[r, :] = tinv

    hd = lambda off: pl.BlockSpec((GRP, 128), lambda i, h: (i, off + h))
    vec = pl.BlockSpec((1, G_HEADS), lambda i, h: (0, 0))
    sq = pl.BlockSpec((None, GRP, CH), lambda i, h: (h, i, 0))
    return pl.pallas_call(
        body,
        grid=(N_GRP, G_HEADS),
        in_specs=[hd(0), hd(G_HEADS), hd(2 * G_HEADS), pl.BlockSpec((GRP, 2 * G_HEADS), lambda i, h: (i, 0)), vec, vec],
        out_specs=[hd(0), hd(0), hd(0), hd(0), sq, pl.BlockSpec((None, CPG, 128), lambda i, h: (h, i, 0)), sq],
        out_shape=[
            jax.ShapeDtypeStruct((T, D), F32),
            jax.ShapeDtypeStruct((T, D), BF16),
            jax.ShapeDtypeStruct((T, D), BF16),
            jax.ShapeDtypeStruct((T, D), BF16),
            jax.ShapeDtypeStruct((G_HEADS, T, CH), BF16),
            jax.ShapeDtypeStruct((G_HEADS, T // CH, 128), F32),
            jax.ShapeDtypeStruct((G_HEADS, T, CH), F32),
        ],
        scratch_shapes=[pltpu.VMEM((GRP, 128), F32)],
        compiler_params=_cp("parallel", "parallel"),
        name=name,
    )(qkv, qkv, qkv, ba, alog, dtb)


def gdn_rec_fwd(u, w, qd, kd, at, el, name):
    def body(u_ref, w_ref, qd_ref, kd_ref, at_ref, el_ref, o_ref, vn_ref, st_ref, s_scr):
        @pl.when(pl.program_id(0) == 0)
        def _():
            s_scr[...] = jnp.zeros_like(s_scr)

        for c in range(CPG):
            r = slice(c * CH, (c + 1) * CH)
            for h in range(G_HEADS):
                ln = slice(h * 128, (h + 1) * 128)
                s = s_scr[h]
                st_ref[h, c] = s
                sb = s.astype(BF16)
                vn = (u_ref[r, ln] - _dot(w_ref[r, ln], sb)).astype(BF16)
                o_ref[r, ln] = _dot(qd_ref[r, ln], sb) + _dot(at_ref[h, r, :], vn)
                s_scr[h] = s * el_ref[h, pl.ds(c, 1), :] + _dot_tn(kd_ref[r, ln], vn)
                vn_ref[r, ln] = vn

    row = pl.BlockSpec((GRP, D), lambda i: (i, 0))
    return pl.pallas_call(
        body,
        grid=(N_GRP,),
        in_specs=[row, row, row, row, pl.BlockSpec((G_HEADS, GRP, CH), lambda i: (0, i, 0)),
                  pl.BlockSpec((G_HEADS, CPG, 128), lambda i: (0, i, 0))],
        out_specs=[row, row, pl.BlockSpec((G_HEADS, CPG, 128, 128), lambda i: (0, i, 0, 0))],
        out_shape=[
            jax.ShapeDtypeStruct((T, D), F32),
            jax.ShapeDtypeStruct((T, D), BF16),
            jax.ShapeDtypeStruct((G_HEADS, T // CH, 128, 128), F32),
        ],
        scratch_shapes=[pltpu.VMEM((G_HEADS, 128, 128), F32)],
        compiler_params=_cp("arbitrary"),
        name=name,
    )(u, w, qd, kd, at, el)


def gdn_rec_bwd(do, w, qd, kd, at, el, vn, st, name):
    def body(do_ref, w_ref, qd_ref, kd_ref, at_ref, el_ref, vn_ref, st_ref,
             du_ref, dw_ref, dqd_ref, dkd_ref, dat_ref, del_ref, ds_scr):
        @pl.when(pl.program_id(0) == 0)
        def _():
            ds_scr[...] = jnp.zeros_like(ds_scr)

        for c in reversed(range(CPG)):
            r = slice(c * CH, (c + 1) * CH)
            for h in range(G_HEADS):
                ln = slice(h * 128, (h + 1) * 128)
                ds = ds_scr[h]
                dsb = ds.astype(BF16)
                sn = st_ref[h, c]
                snb = sn.astype(BF16)
                dob = do_ref[r, ln].astype(BF16)
                vnb = vn_ref[r, ln]
                dvn = (_dot(kd_ref[r, ln], dsb) + _dot_tn(at_ref[h, r, :], dob)).astype(BF16)
                du_ref[r, ln] = dvn
                dkd_ref[r, ln] = _dot_nt(vnb, dsb)
                tot = jnp.sum(jnp.sum(ds * sn, axis=1, keepdims=True), axis=0, keepdims=True)
                del_ref[h, pl.ds(c, 1), :] = jnp.broadcast_to(tot, (1, 128))
                dqd_ref[r, ln] = _dot_nt(dob, snb)
                dat_ref[h, r, :] = _dot_nt(dob, vnb)
                dw_ref[r, ln] = (-_dot_nt(dvn, snb)).astype(BF16)
                ds_scr[h] = ds * el_ref[h, pl.ds(c, 1), :] + _dot_tn(qd_ref[r, ln], dob) - _dot_tn(w_ref[r, ln], dvn)

    last = N_GRP - 1
    row = pl.BlockSpec((GRP, D), lambda i: (last - i, 0))
    sq = pl.BlockSpec((G_HEADS, GRP, CH), lambda i: (0, last - i, 0))
    sc = pl.BlockSpec((G_HEADS, CPG, 128), lambda i: (0, last - i, 0))
    return pl.pallas_call(
        body,
        grid=(N_GRP,),
        in_specs=[row, row, row, row, sq, sc, row, pl.BlockSpec((G_HEADS, CPG, 128, 128), lambda i: (0, last - i, 0, 0))],
        out_specs=[row, row, row, row, sq, sc],
        out_shape=[
            jax.ShapeDtypeStruct((T, D), BF16),
            jax.ShapeDtypeStruct((T, D), BF16),
            jax.ShapeDtypeStruct((T, D), F32),
            jax.ShapeDtypeStruct((T, D), F32),
            jax.ShapeDtypeStruct((G_HEADS, T, CH), F32),
            jax.ShapeDtypeStruct((G_HEADS, T // CH, 128), F32),
        ],
        scratch_shapes=[pltpu.VMEM((G_HEADS, 128, 128), F32)],
        compiler_params=_cp("arbitrary"),
        name=name,
    )(do, w, qd, kd, at, el, vn, st)


def gdn_local_bwd(qkv, ba, alog, dtb, tinv, du, dw, dqd, dkd, dat, dl, name):
    def body(q_ref, k_ref, v_ref, ba_ref, al_ref, dt_ref, ti_ref, du_ref, dw_ref, dqd_ref, dkd_ref, dat_ref, dl_ref,
             dq_ref, dk_ref, dv_ref, dba_ref, dal_ref, ddt_ref, gcs):
        gi = pl.program_id(0)
        h = pl.program_id(1)
        beta, g, xa, ea = _gdn_gates(ba_ref, al_ref, dt_ref, h)
        gc = _seg_cumsum(jnp.broadcast_to(g, (GRP, 128)))
        gcs[...] = gc
        causal, strict, _, _ = _chunk_masks()
        dgc_l, dgl_l, dbeta_l = [], [], []
        for c in range(CPG):
            r = slice(c * CH, (c + 1) * CH)
            q, k, v = q_ref[r, :], k_ref[r, :], v_ref[r, :]
            bt = beta[r, :]
            gcc = gc[r, :]
            ec = jnp.exp(gcc)
            gl = gcs[pl.ds(c * CH + CH - 1, 1), :]
            f2 = jnp.exp(gl - gcc)
            elc = jnp.exp(gl)
            dm = _decay(gcc, causal)
            qs = q * G_SCALE
            kb = k * bt
            vb = v * bt
            kbe = kb * ec
            kbf, kbb, qsb = k.astype(BF16), kb.astype(BF16), qs.astype(BF16)
            a1 = _dot_nt(kbb, kbf)
            qk = _dot_nt(qsb, kbf)
            ti = ti_ref[r, :]
            tb = ti.astype(BF16)
            du_c, dw_c = du_ref[r, :], dw_ref[r, :]
            dqd_c, dkd_c, dat_c = dqd_ref[r, :], dkd_ref[r, :], dat_ref[r, :]

            dqs = dqd_c * ec
            d_e = jnp.sum(dqd_c * qs, axis=1, keepdims=True)
            dk = dkd_c * f2
            tcol = jnp.sum(dkd_c * k, axis=1, keepdims=True) * f2[:, 0:1]
            dgl = jnp.sum(tcol, axis=0, keepdims=True) + dl_ref[pl.ds(c, 1), 0:1] * elc[:, 0:1]
            dgc = -tcol
            dqk = (dat_c * dm).astype(BF16)
            d_d = dat_c * qk
            dqs = dqs + _dot(dqk, kbf)
            dk = dk + _dot_tn(dqk, qsb)
            dtinv = _dot_nt(du_c, vb.astype(BF16)) + _dot_nt(dw_c, kbe.astype(BF16))
            dvb = _dot_tn(tb, du_c)
            dkbe = _dot_tn(tb, dw_c)
            x = lax.dot_general(ti, dtinv, (((0,), (0,)), ((), ())), precision=HI, preferred_element_type=F32)
            dlm = -lax.dot_general(x, ti, (((1,), (1,)), ((), ())), precision=HI, preferred_element_type=F32)
            dlm = jnp.where(strict, dlm, 0.0)
            da1 = (dlm * dm).astype(BF16)
            d_d = d_d + dlm * a1
            dkb = _dot(da1, kbf) + dkbe * ec
            dk = dk + _dot_tn(da1, kbb)
            d_e = d_e + jnp.sum(dkbe * kb, axis=1, keepdims=True)
            dk = dk + dkb * bt
            dbeta_l.append(jnp.sum(dkb * k, axis=1, keepdims=True) + jnp.sum(dvb * v, axis=1, keepdims=True))
            ddiff = d_d * dm
            dgc = dgc + jnp.sum(ddiff, axis=1, keepdims=True) - jnp.sum(ddiff.T, axis=1, keepdims=True)
            dgc = dgc + d_e * ec[:, 0:1]
            dgc_l.append(dgc)
            dgl_l.append(jnp.broadcast_to(dgl, (CH, 1)))
            dq_ref[r, :] = dqs * G_SCALE
            dk_ref[r, :] = dk
            dv_ref[r, :] = dvb * bt

        dgc_all = jnp.broadcast_to(jnp.concatenate(dgc_l, axis=0), (GRP, 128))
        dg = _seg_cumsum(dgc_all, reverse=True)[:, 0:1] + jnp.concatenate(dgl_l, axis=0)
        dbeta = jnp.concatenate(dbeta_l, axis=0)
        da = dg * (-ea) * _sigmoid(xa)
        db = dbeta * beta * (1.0 - beta)
        lane = lax.broadcasted_iota(jnp.int32, (GRP, 2 * G_HEADS), 1)
        dba = jnp.where(lane == h, db, 0.0) + jnp.where(lane == G_HEADS + h, da, 0.0)
        lane8 = lax.broadcasted_iota(jnp.int32, (1, G_HEADS), 1)
        dal = jnp.where(lane8 == h, jnp.sum(dg * g, axis=0, keepdims=True), 0.0)
        ddt = jnp.where(lane8 == h, jnp.sum(da, axis=0, keepdims=True), 0.0)

        @pl.when(h == 0)
        def _():
            dba_ref[...] = dba

        @pl.when(h > 0)
        def _():
            dba_ref[...] += dba

        @pl.when((h == 0) & (gi == 0))
        def _():
            dal_ref[...] = dal
            ddt_ref[...] = ddt

        @pl.when((h > 0) | (gi > 0))
        def _():
            dal_ref[...] += dal
            ddt_ref[...] += ddt

    hd = lambda off: pl.BlockSpec((GRP, 128), lambda i, h: (i, off + h))
    vec = pl.BlockSpec((1, G_HEADS), lambda i, h: (0, 0))
    sq = pl.BlockSpec((None, GRP, CH), lambda i, h: (h, i, 0))
    gates = pl.BlockSpec((GRP, 2 * G_HEADS), lambda i, h: (i, 0))
    return pl.pallas_call(
        body,
        grid=(N_GRP, G_HEADS),
        in_specs=[hd(0), hd(G_HEADS), hd(2 * G_HEADS), gates, vec, vec, sq, hd(0), hd(0), hd(0), hd(0), sq,
                  pl.BlockSpec((None, CPG, 128), lambda i, h: (h, i, 0))],
        out_specs=[hd(0), hd(0), hd(0), gates, vec, vec],
        out_shape=[
            jax.ShapeDtypeStruct((T, D), F32),
            jax.ShapeDtypeStruct((T, D), F32),
            jax.ShapeDtypeStruct((T, D), F32),
            jax.ShapeDtypeStruct((T, 2 * G_HEADS), F32),
            jax.ShapeDtypeStruct((1, G_HEADS), F32),
            jax.ShapeDtypeStruct((1, G_HEADS), F32),
        ],
        scratch_shapes=[pltpu.VMEM((GRP, 128), F32)],
        compiler_params=_cp("arbitrary", "arbitrary"),
        name=name,
    )(qkv, qkv, qkv, ba, alog, dtb, tinv, du, dw, dqd, dkd, dat, dl)


def _gated_norm(o, z, g):
    rstd = lax.rsqrt(jnp.mean(o * o, axis=-1, keepdims=True) + EPS)
    oh = o * rstd
    sig = _sigmoid(z)
    return oh, rstd, sig


def gdn_out_fwd(h, o, z, norm_g, w_out, name):
    tt = 512

    def body(h_ref, o_ref, z_ref, g_ref, w_ref, hnew_ref, cat_ref):
        g = g_ref[...]
        for hh in range(G_HEADS):
            ln = slice(hh * 128, (hh + 1) * 128)
            zz = z_ref[:, ln]
            oh, _, sig = _gated_norm(o_ref[:, ln], zz, g)
            cat_ref[:, ln] = (oh * g * (zz * sig)).astype(BF16)
        hnew_ref[...] = h_ref[...] + _dot(cat_ref[...], w_ref[...])

    full = pl.BlockSpec((tt, D), lambda i: (i, 0))
    return pl.pallas_call(
        body,
        grid=(T // tt,),
        in_specs=[full, full, full, pl.BlockSpec((1, 128), lambda i: (0, 0)), pl.BlockSpec((D, D), lambda i: (0, 0))],
        out_specs=[full, full],
        out_shape=[jax.ShapeDtypeStruct((T, D), F32), jax.ShapeDtypeStruct((T, D), BF16)],
        compiler_params=_cp("parallel"),
        name=name,
    )(h, o, z, norm_g, w_out)


def gdn_out_bwd(dres, o, z, norm_g, w_out, name):
    tt = 512

    def body(d_ref, o_ref, z_ref, g_ref, w_ref, do_ref, dz_ref, dg_ref, dcat):
        i = pl.program_id(0)
        g = g_ref[...]
        dcat[...] = _dot_nt(d_ref[...].astype(BF16), w_ref[...])
        dg = jnp.zeros((1, 128), F32)
        for hh in range(G_HEADS):
            ln = slice(hh * 128, (hh + 1) * 128)
            zz = z_ref[:, ln]
            oh, rstd, sig = _gated_norm(o_ref[:, ln], zz, g)
            dout = dcat[:, ln]
            dy = dout * (zz * sig)
            dz_ref[:, ln] = dout * (oh * g) * _dsilu(zz, sig)
            dg = dg + jnp.sum(dy * oh, axis=0, keepdims=True)
            doh = dy * g
            do_ref[:, ln] = rstd * (doh - oh * jnp.mean(doh * oh, axis=-1, keepdims=True))

        @pl.when(i == 0)
        def _():
            dg_ref[...] = dg

        @pl.when(i > 0)
        def _():
            dg_ref[...] += dg

    full = pl.BlockSpec((tt, D), lambda i: (i, 0))
    vec = pl.BlockSpec((1, 128), lambda i: (0, 0))
    return pl.pallas_call(
        body,
        grid=(T // tt,),
        in_specs=[full, full, full, vec, pl.BlockSpec((D, D), lambda i: (0, 0))],
        out_specs=[full, full, vec],
        out_shape=[jax.ShapeDtypeStruct((T, D), F32), jax.ShapeDtypeStruct((T, D), F32), jax.ShapeDtypeStruct((1, 128), F32)],
        scratch_shapes=[pltpu.VMEM((tt, D), F32)],
        compiler_params=_cp("arbitrary"),
        name=name,
    )(dres, o, z, norm_g, w_out)


GDN_SPLITS = [(0, 1024), (1024, 1024), (2048, 1024), (3072, 1024), (4096, 2 * G_HEADS)]


def gdn_fwd(h, g_row, w_in, conv_w, alog, dtb, norm_g, w_out, tag):
    hn, qkv_pre, z, ba = proj_fwd(h, g_row, w_in, [(0, G_QKV), (G_QKV, 1024), (4096, 2 * G_HEADS)], f"gdn_proj_{tag}")
    qkv = gdn_prep_fwd(qkv_pre, conv_w, f"gdn_prep_{tag}")
    u, w, qd, kd, at, el, tinv = gdn_local_fwd(qkv, ba, alog, dtb, f"gdn_local_{tag}")
    o, vn, st = gdn_rec_fwd(u, w, qd, kd, at, el, f"gdn_rec_{tag}")
    hnew, cat = gdn_out_fwd(h, o, z, norm_g, w_out, f"gdn_out_{tag}")
    return hnew, (h, hn, qkv_pre, z, ba, qkv, w, qd, kd, at, el, tinv, o, vn, st, cat)


def gdn_bwd(dres, saved, g_row, w_in, conv_w, alog, dtb, norm_g, w_out, tag):
    h, hn, qkv_pre, z, ba, qkv, w, qd, kd, at, el, tinv, o, vn, st, cat = saved
    d_o, d_z, d_ng = gdn_out_bwd(dres, o, z, norm_g, w_out, f"gdn_out_bwd_{tag}")
    d_wout = mm_tn(cat, dres, f"gdn_wout_grad_{tag}")
    du, dw, dqd, dkd, dat, dl = gdn_rec_bwd(d_o, w, qd, kd, at, el, vn, st, f"gdn_rec_bwd_{tag}")
    dq, dk, dv, dba, dal, ddt = gdn_local_bwd(qkv, ba, alog, dtb, tinv, du, dw, dqd, dkd, dat, dl, f"gdn_local_bwd_{tag}")
    dpre, dcw = [], []
    for part, d in enumerate((dq, dk, dv)):
        dx, dwc = gdn_prep_bwd(d, qkv_pre, conv_w, part, part < 2, f"gdn_prep_bwd_{tag}_{part}")
        dpre.append(dx)
        dcw.append(dwc)
    parts = dpre + [d_z, dba]
    dres_new, d_norm = proj_bwd_data(h, g_row, w_in, parts, GDN_SPLITS, dres, f"gdn_proj_bwd_{tag}")
    d_win = jnp.concatenate([mm_tn(hn, p, f"gdn_win_grad_{tag}_{k}") for k, p in enumerate(parts)], axis=1)
    return dres_new, dict(norm=d_norm, w_in=d_win, conv_w=jnp.concatenate(dcw, axis=1), A_log=dal, dt_bias=ddt,
                          norm_g=d_ng, w_out=d_wout)


MESH = pl.DeviceIdType.MESH
ANY = pl.BlockSpec(memory_space=pl.ANY)


def _coords():
    return lax.axis_index("x"), lax.axis_index("y"), lax.axis_index("c")


def _slot(p):
    return 4 * p[0] + 2 * p[1] + p[2]


def all_gather(shards, name):
    k_n = len(shards)

    def body(*refs):
        srcs, dsts = refs[:k_n], refs[k_n:2 * k_n]
        send_sems, recv_sems, local_sems = refs[2 * k_n:]
        x, y, c = _coords()
        me, sibling = (x, y, c), (x, y, 1 - c)
        chips = [(1 - x, y), (x, 1 - y), (1 - x, 1 - y)]

        def copy(k, s, block, to, from_src=False):
            rows = dsts[k].at[_slot(block)]
            return pltpu.make_async_remote_copy(
                src_ref=srcs[k] if from_src else rows, dst_ref=rows,
                send_sem=send_sems.at[k, s], recv_sem=recv_sems.at[k, s], device_id=to, device_id_type=MESH)

        local = [pltpu.make_async_copy(srcs[k], dsts[k].at[_slot(me)], local_sems.at[k]) for k in range(k_n)]
        for cp in local:
            cp.start()
        first = []
        for k in range(k_n):
            first.append(copy(k, 0, me, sibling, True))
            first += [copy(k, 1 + j, me, (*chip, c), True) for j, chip in enumerate(chips)]
        for cp in first:
            cp.start()
        passed = []
        for j, chip in enumerate(chips):
            for k in range(k_n):
                copy(k, 1 + j, (*chip, c), me).wait_recv()
                fw = copy(k, 4 + j, (*chip, c), sibling)
                fw.start()
                passed.append(fw)
        for k in range(k_n):
            copy(k, 0, sibling, me).wait_recv()
            for j, chip in enumerate(chips):
                copy(k, 4 + j, (*chip, 1 - c), me).wait_recv()
        for cp in first + passed:
            cp.wait_send()
        for cp in local:
            cp.wait()

    return pl.pallas_call(
        body,
        in_specs=[ANY] * k_n,
        out_specs=[ANY] * k_n,
        out_shape=[jax.ShapeDtypeStruct((N_DEV,) + s.shape, s.dtype) for s in shards],
        scratch_shapes=[pltpu.SemaphoreType.DMA((k_n, 7)), pltpu.SemaphoreType.DMA((k_n, 7)),
                        pltpu.SemaphoreType.DMA((k_n,))],
        name=name,
    )(*shards)


def slab_exchange(items, dst_shapes, name):
    n_it = len(items)
    n_dst = len(dst_shapes)

    def body(*refs):
        srcs, dsts = refs[:n_it], refs[n_it:n_it + n_dst]
        send_sems, recv_sems, local_sems = refs[n_it + n_dst:]
        x, y, c = _coords()
        me = _slot((x, y, c))
        peers = []
        for j in range(1, N_DEV):
            jx, jy, jc = (j >> 2) & 1, (j >> 1) & 1, j & 1
            peers.append((x if jx == 0 else 1 - x, y if jy == 0 else 1 - y, c if jc == 0 else 1 - c))

        def dview(t, s):
            _, k, layer = items[t]
            return dsts[k].at[s] if layer is None else dsts[k].at[s, layer]

        local = [pltpu.make_async_copy(srcs[t].at[me], dview(t, me), local_sems.at[t]) for t in range(n_it)]
        for cp in local:
            cp.start()
        sends = []
        for j, peer in enumerate(peers):
            for t in range(n_it):
                cp = pltpu.make_async_remote_copy(
                    src_ref=srcs[t].at[_slot(peer)], dst_ref=dview(t, me),
                    send_sem=send_sems.at[t, j], recv_sem=recv_sems.at[t, j], device_id=peer, device_id_type=MESH)
                cp.start()
                sends.append(cp)
        for j, peer in enumerate(peers):
            for t in range(n_it):
                pltpu.make_async_remote_copy(
                    src_ref=srcs[t].at[_slot(peer)], dst_ref=dview(t, _slot(peer)),
                    send_sem=send_sems.at[t, j], recv_sem=recv_sems.at[t, j], device_id=peer,
                    device_id_type=MESH).wait_recv()
        for cp in sends:
            cp.wait_send()
        for cp in local:
            cp.wait()

    return pl.pallas_call(
        body,
        in_specs=[ANY] * n_it,
        out_specs=[ANY] * n_dst,
        out_shape=list(dst_shapes),
        scratch_shapes=[pltpu.SemaphoreType.DMA((n_it, 7)), pltpu.SemaphoreType.DMA((n_it, 7)),
                        pltpu.SemaphoreType.DMA((n_it,))],
        name=name,
    )(*[it[0] for it in items])


def sum_slabs(parts, name):
    n, rows, cols = parts.shape

    def body(p_ref, o_ref):
        g = p_ref[0]
        for s in range(1, n):
            g = g + p_ref[s]
        o_ref[...] = g

    return pl.pallas_call(body, out_shape=jax.ShapeDtypeStruct((rows, cols), F32), name=name)(parts)


def _row_tile(rows, cols):
    if rows * cols * 4 <= (1 << 20) or rows % 8:
        return rows
    tr = rows
    while tr % 2 == 0 and (tr // 2) % 8 == 0 and tr * cols * 4 > (1 << 20):
        tr //= 2
    return tr


def adamw(parts, w, m, v, name):
    p_n = parts.shape[0]
    rows, cols = w.shape
    tr = _row_tile(rows, cols)

    def body(p_ref, w_ref, m_ref, v_ref, g_ref, d_ref, nm_ref, nv_ref):
        g = p_ref[0].astype(F32)
        for s in range(1, p_n):
            g = g + p_ref[s].astype(F32)
        m_new = ADAM_B1 * m_ref[...] + (1.0 - ADAM_B1) * g
        v_new = ADAM_B2 * v_ref[...] + (1.0 - ADAM_B2) * (g * g)
        m_hat = m_new / (1.0 - ADAM_B1 ** ADAM_STEP)
        v_hat = v_new / (1.0 - ADAM_B2 ** ADAM_STEP)
        g_ref[...] = g
        d_ref[...] = -ADAM_LR * (m_hat / (jnp.sqrt(v_hat) + ADAM_EPS) + ADAM_WD * w_ref[...])
        nm_ref[...] = m_new
        nv_ref[...] = v_new

    blk = pl.BlockSpec((tr, cols), lambda i: (i, 0))
    return pl.pallas_call(
        body,
        grid=(rows // tr,),
        in_specs=[pl.BlockSpec((p_n, tr, cols), lambda i: (0, i, 0)), blk, blk, blk],
        out_specs=[blk] * 4,
        out_shape=[jax.ShapeDtypeStruct((rows, cols), F32)] * 4,
        compiler_params=_cp("parallel"),
        name=name,
    )(parts, w, m, v)


def _adamw_nd(parts, w, m, v, name):
    shp = w.shape
    cols = shp[-1]
    rows = math.prod(shp[:-1])
    outs = adamw(parts.reshape(parts.shape[0], rows, cols), w.reshape(rows, cols), m.reshape(rows, cols),
                 v.reshape(rows, cols), name)
    return [o.reshape(shp) for o in outs]


REPL = ["ffn1_norm", "mix_norm", "ffn2_norm", "hyb_dw_b", "hyb_ln_g", "hyb_ln_b", "gdn_A_log", "gdn_dt_bias",
        "gdn_norm_g", "final_norm"]
WEIGHTS = ["ffn1_norm", "ffn1_w_in", "ffn1_w_out", "mix_norm", "ffn2_norm", "ffn2_w_in", "ffn2_w_out", "hyb_w_in",
           "hyb_dw_w", "hyb_dw_b", "hyb_ln_g", "hyb_ln_b", "hyb_w_out", "gdn_w_in", "gdn_conv_w", "gdn_A_log",
           "gdn_dt_bias", "gdn_norm_g", "gdn_w_out", "final_norm"]


def _pack(arrs, rows):
    flat = jnp.concatenate([a.reshape(-1) for a in arrs])
    return jnp.pad(flat, (0, rows * 128 - flat.shape[0])).reshape(rows, 128)


def _cols_to_slabs(a):
    d, n = a.shape
    return a.reshape(d, N_DEV, n // N_DEV).transpose(1, 0, 2)


def _slabs_to_cols(a):
    return jnp.moveaxis(a, 0, -2).reshape(a.shape[1:-1] + (N_DEV * a.shape[-1],))


def kernel(x, positions, ffn1_norm, ffn1_w_in, ffn1_w_out, mix_norm, ffn2_norm, ffn2_w_in, ffn2_w_out, hyb_w_in, hyb_dw_w, hyb_dw_b, hyb_ln_g, hyb_ln_b, hyb_w_out, gdn_w_in, gdn_conv_w, gdn_A_log, gdn_dt_bias, gdn_norm_g, gdn_w_out, final_norm, loss_target, m_ffn1_norm, m_ffn1_w_in, m_ffn1_w_out, m_mix_norm, m_ffn2_norm, m_ffn2_w_in, m_ffn2_w_out, m_hyb_w_in, m_hyb_dw_w, m_hyb_dw_b, m_hyb_ln_g, m_hyb_ln_b, m_hyb_w_out, m_gdn_w_in, m_gdn_conv_w, m_gdn_A_log, m_gdn_dt_bias, m_gdn_norm_g, m_gdn_w_out, m_final_norm, v_ffn1_norm, v_ffn1_w_in, v_ffn1_w_out, v_mix_norm, v_ffn2_norm, v_ffn2_w_in, v_ffn2_w_out, v_hyb_w_in, v_hyb_dw_w, v_hyb_dw_b, v_hyb_ln_g, v_hyb_ln_b, v_hyb_w_out, v_gdn_w_in, v_gdn_conv_w, v_gdn_A_log, v_gdn_dt_bias, v_gdn_norm_g, v_gdn_w_out, v_final_norm):
    w = dict(ffn1_norm=ffn1_norm, ffn1_w_in=ffn1_w_in, ffn1_w_out=ffn1_w_out, mix_norm=mix_norm, ffn2_norm=ffn2_norm,
             ffn2_w_in=ffn2_w_in, ffn2_w_out=ffn2_w_out, hyb_w_in=hyb_w_in, hyb_dw_w=hyb_dw_w, hyb_dw_b=hyb_dw_b,
             hyb_ln_g=hyb_ln_g, hyb_ln_b=hyb_ln_b, hyb_w_out=hyb_w_out, gdn_w_in=gdn_w_in, gdn_conv_w=gdn_conv_w,
             gdn_A_log=gdn_A_log, gdn_dt_bias=gdn_dt_bias, gdn_norm_g=gdn_norm_g, gdn_w_out=gdn_w_out,
             final_norm=final_norm)
    mom = dict(ffn1_norm=m_ffn1_norm, ffn1_w_in=m_ffn1_w_in, ffn1_w_out=m_ffn1_w_out, mix_norm=m_mix_norm,
               ffn2_norm=m_ffn2_norm, ffn2_w_in=m_ffn2_w_in, ffn2_w_out=m_ffn2_w_out, hyb_w_in=m_hyb_w_in,
               hyb_dw_w=m_hyb_dw_w, hyb_dw_b=m_hyb_dw_b, hyb_ln_g=m_hyb_ln_g, hyb_ln_b=m_hyb_ln_b,
               hyb_w_out=m_hyb_w_out, gdn_w_in=m_gdn_w_in, gdn_conv_w=m_gdn_conv_w, gdn_A_log=m_gdn_A_log,
               gdn_dt_bias=m_gdn_dt_bias, gdn_norm_g=m_gdn_norm_g, gdn_w_out=m_gdn_w_out, final_norm=m_final_norm)
    var = dict(ffn1_norm=v_ffn1_norm, ffn1_w_in=v_ffn1_w_in, ffn1_w_out=v_ffn1_w_out, mix_norm=v_mix_norm,
               ffn2_norm=v_ffn2_norm, ffn2_w_in=v_ffn2_w_in, ffn2_w_out=v_ffn2_w_out, hyb_w_in=v_hyb_w_in,
               hyb_dw_w=v_hyb_dw_w, hyb_dw_b=v_hyb_dw_b, hyb_ln_g=v_hyb_ln_g, hyb_ln_b=v_hyb_ln_b,
               hyb_w_out=v_hyb_w_out, gdn_w_in=v_gdn_w_in, gdn_conv_w=v_gdn_conv_w, gdn_A_log=v_gdn_A_log,
               gdn_dt_bias=v_gdn_dt_bias, gdn_norm_g=v_gdn_norm_g, gdn_w_out=v_gdn_w_out, final_norm=v_final_norm)
    xi, yi, ci = _coords()
    me = 4 * xi + 2 * yi + ci

    big = ["ffn1_w_in", "ffn1_w_out", "ffn2_w_in", "ffn2_w_out", "hyb_w_in", "hyb_w_out", "gdn_w_in", "gdn_w_out"]
    gathered = all_gather([w[n].astype(BF16) for n in big] + [w["hyb_dw_w"], w["gdn_conv_w"]], "weights_all_gather")
    gw = dict(zip(big + ["hyb_dw_w", "gdn_conv_w"], gathered))
    ffn_in = {t: gw[f"ffn{t}_w_in"].reshape(2, FFN_TILES, DEPTH, D, FFN_SHARD) for t in (1, 2)}
    ffn_out = {t: gw[f"ffn{t}_w_out"] for t in (1, 2)}
    hyb_in = _slabs_to_cols(gw["hyb_w_in"])
    gdn_in = _slabs_to_cols(gw["gdn_w_in"])
    hyb_out = jnp.moveaxis(gw["hyb_w_out"], 0, 1).reshape(2, D, D)
    gdn_out = jnp.moveaxis(gw["gdn_w_out"], 0, 1).reshape(2, D, D)
    dw_full = jnp.pad(_slabs_to_cols(gw["hyb_dw_w"]), ((0, 0), (0, 1), (0, 0)))
    cw_full = _slabs_to_cols(gw["gdn_conv_w"])
    row = lambda a: a.reshape(1, -1)

    rope = make_rope(positions)
    h = x[0]
    saved = []
    for l in range(DEPTH):
        i = l // 2
        rec = {"h1": h}
        h, rec["hn1"], rec["a1"], rec["b1"] = ffn_fwd(h, row(ffn1_norm[l]), ffn_in[1], ffn_out[1], l, "1")
        if l % 2 == 0:
            h, rec["mix"] = hybrid_fwd(h, row(mix_norm[l]), hyb_in[i], dw_full[i], row(hyb_dw_b[i]), row(hyb_ln_g[i]),
                                       row(hyb_ln_b[i]), hyb_out[i], rope, str(i))
        else:
            h, rec["mix"] = gdn_fwd(h, row(mix_norm[l]), gdn_in[i], cw_full[i], row(gdn_A_log[i]), row(gdn_dt_bias[i]),
                                    row(gdn_norm_g[i]), gdn_out[i], str(i))
        rec["h2"] = h
        h, rec["hn2"], rec["a2"], rec["b2"] = ffn_fwd(h, row(ffn2_norm[l]), ffn_in[2], ffn_out[2], l, "2")
        saved.append(rec)
    dres, d_final, loss_acc = final_loss(h, row(final_norm), loss_target[0])
    loss = lax.psum(loss_acc[0, 0], ("x", "y", "c"))

    items = []
    dst_names = big
    gsmall = {n: [None] * (DEPTH if n in ("ffn1_norm", "mix_norm", "ffn2_norm") else 2) for n in REPL[:-1]}
    gsmall["hyb_dw_w"] = [None, None]
    gsmall["gdn_conv_w"] = [None, None]
    for l in reversed(range(DEPTH)):
        i = l // 2
        rec = saved[l]
        dhn, dwin, dwout = ffn_bwd(rec["hn2"], rec["a2"], rec["b2"], dres, ffn_in[2], ffn_out[2], l, "2")
        items.append((dwin.reshape(N_DEV, D, FFN_SHARD), dst_names.index("ffn2_w_in"), l))
        items.append((dwout.reshape(N_DEV, FFN_SHARD // 2, D), dst_names.index("ffn2_w_out"), l))
        dres, dg = norm_bwd(rec["h2"], row(ffn2_norm[l]), dhn, dres, f"ffn2_norm_bwd_{l}")
        gsmall["ffn2_norm"][l] = dg
        if l % 2 == 0:
            dres, gr = hybrid_bwd(dres, rec["mix"], row(mix_norm[l]), hyb_in[i], dw_full[i], row(hyb_ln_g[i]),
                                  row(hyb_ln_b[i]), hyb_out[i], rope, str(i))
            items.append((_cols_to_slabs(gr["w_in"]).astype(BF16), dst_names.index("hyb_w_in"), i))
            items.append((gr["w_out"].reshape(N_DEV, D // N_DEV, D).astype(BF16), dst_names.index("hyb_w_out"), i))
            for n in ("dw_w", "dw_b", "ln_g", "ln_b"):
                gsmall["hyb_" + n][i] = gr[n]
        else:
            dres, gr = gdn_bwd(dres, rec["mix"], row(mix_norm[l]), gdn_in[i], cw_full[i], row(gdn_A_log[i]),
                               row(gdn_dt_bias[i]), row(gdn_norm_g[i]), gdn_out[i], str(i))
            items.append((_cols_to_slabs(gr["w_in"]).astype(BF16), dst_names.index("gdn_w_in"), i))
            items.append((gr["w_out"].reshape(N_DEV, D // N_DEV, D).astype(BF16), dst_names.index("gdn_w_out"), i))
            for n in ("conv_w", "A_log", "dt_bias", "norm_g"):
                gsmall["gdn_" + n][i] = gr[n]
        gsmall["mix_norm"][l] = gr["norm"]
        dhn, dwin, dwout = ffn_bwd(rec["hn1"], rec["a1"], rec["b1"], dres, ffn_in[1], ffn_out[1], l, "1")
        items.append((dwin.reshape(N_DEV, D, FFN_SHARD), dst_names.index("ffn1_w_in"), l))
        items.append((dwout.reshape(N_DEV, FFN_SHARD // 2, D), dst_names.index("ffn1_w_out"), l))
        dres, dg = norm_bwd(rec["h1"], row(ffn1_norm[l]), dhn, dres, f"ffn1_norm_bwd_{l}")
        gsmall["ffn1_norm"][l] = dg
    grad_x = dres[None]

    n_repl_rows = 136
    small_rows = 576
    repl_flat = jnp.concatenate([jnp.concatenate([a.reshape(-1) for a in gsmall[n]]) for n in REPL[:-1]]
                                + [d_final.reshape(-1)])
    repl_pack = jnp.pad(repl_flat, (0, n_repl_rows * 128 - repl_flat.shape[0]))
    small_pack = jnp.concatenate([repl_pack] + [a.reshape(-1) for a in gsmall["hyb_dw_w"]]
                                 + [a.reshape(-1) for a in gsmall["gdn_conv_w"]]).reshape(small_rows, 128)
    small_all, = all_gather([small_pack], "small_grads_all_gather")
    g_small = sum_slabs(small_all, "small_grads_sum")
    received = slab_exchange(items, [jax.ShapeDtypeStruct((N_DEV,) + w[n].shape, BF16) for n in big], "grad_exchange")
    recv = dict(zip(big, received))

    out = {}
    for n in big:
        out[n] = _adamw_nd(recv[n], w[n], mom[n], var[n], f"adamw_{n}")
    pk = lambda d: _pack([d[n] for n in REPL], n_repl_rows)
    res = adamw(g_small[:n_repl_rows][None], pk(w), pk(mom), pk(var), "adamw_replicated")
    off = 0
    for n in REPL:
        sz = w[n].size
        out[n] = [r.reshape(-1)[off:off + sz].reshape(w[n].shape) for r in res]
        off += sz
    g_dw = g_small[n_repl_rows:n_repl_rows + 248].reshape(2, CONV_K, CONV_C)
    g_dw = lax.dynamic_slice_in_dim(g_dw, me * (CONV_C // N_DEV), CONV_C // N_DEV, axis=2)
    out["hyb_dw_w"] = _adamw_nd(g_dw[None], w["hyb_dw_w"], mom["hyb_dw_w"], var["hyb_dw_w"], "adamw_hyb_dw_w")
    g_cw = g_small[n_repl_rows + 248:].reshape(2, G_CONV, G_QKV)
    g_cw = lax.dynamic_slice_in_dim(g_cw, me * (G_QKV // N_DEV), G_QKV // N_DEV, axis=2)
    out["gdn_conv_w"] = _adamw_nd(g_cw[None], w["gdn_conv_w"], mom["gdn_conv_w"], var["gdn_conv_w"], "adamw_gdn_conv_w")

    return (loss, grad_x, *[out[n][0] for n in WEIGHTS], *[out[n][1] for n in WEIGHTS],
            *[out[n][2] for n in WEIGHTS], *[out[n][3] for n in WEIGHTS])
```

```python
import functools
import math

import jax
import jax.numpy as jnp
import numpy as np
from jax import lax
from jax.experimental import pallas as pl
from jax.experimental.pallas import tpu as pltpu

F32 = jnp.float32
BF16 = jnp.bfloat16

N_DEV = 8
T = 4096
D = 1024
DEPTH = 4
FFN = 2816
FFN_SHARD = 2 * FFN // N_DEV
FFN_TILES = FFN // FFN_SHARD
EPS = 1e-6

A_HEADS = 8
A_HD = 64
A_W = 512
CONV_C = 512
CONV_K = 31
HYB_IN = 2560
ROPE_THETA = 500000.0
ROT = 16
DILATIONS = (1, 4, 16)
BLK = 128
KPAD = 2048

G_HEADS = 8
G_DK = 128
G_QKV = 3072
G_IN = 4112
G_CONV = 4
CH = 64
GRP = 512
CPG = GRP // CH
N_GRP = T // GRP

ADAM_LR = 0.001
ADAM_B1 = 0.9
ADAM_B2 = 0.999
ADAM_EPS = 1e-08
ADAM_WD = 0.01
ADAM_STEP = 10

VMEM_LIMIT = 56 * 1024 * 1024

HI = lax.Precision.HIGHEST


def _cp(*sem):
    return pltpu.CompilerParams(dimension_semantics=sem, vmem_limit_bytes=VMEM_LIMIT)


def _dot(a, b):
    return jnp.dot(a, b, preferred_element_type=F32)


def _dot_nt(a, b):
    return lax.dot_general(a, b, (((1,), (1,)), ((), ())), preferred_element_type=F32)


def _dot_tn(a, b):
    return lax.dot_general(a, b, (((0,), (0,)), ((), ())), preferred_element_type=F32)


def _sigmoid(x):
    return 1.0 / (1.0 + jnp.exp(-x))


def _dsilu(x, sig):
    return sig * (1.0 + x * (1.0 - sig))


def _rms(x, g):
    rstd = lax.rsqrt(jnp.mean(x * x, axis=-1, keepdims=True) + EPS)
    return x * rstd * g


FFN_TT = 512


def ffn_fwd(h, g_row, w_in, w_out, layer, tag=""):
    def body(h_ref, g_ref, win_ref, wout_ref, hnew_ref, hn_ref, a_ref, b_ref, acc_ref, hns_ref):
        j = pl.program_id(1)

        @pl.when(j == 0)
        def _():
            hn = _rms(h_ref[...], g_ref[...]).astype(BF16)
            hns_ref[...] = hn
            hn_ref[...] = hn
            acc_ref[...] = jnp.zeros_like(acc_ref)

        hn = hns_ref[...]
        a = _dot(hn, win_ref[0])
        b = _dot(hn, win_ref[1])
        act = a * _sigmoid(a) * b
        acc_ref[...] += _dot(act.astype(BF16), wout_ref[...].reshape(FFN_SHARD, D))
        a_ref[...] = a.astype(BF16)
        b_ref[...] = b.astype(BF16)

        @pl.when(j == FFN_TILES - 1)
        def _():
            hnew_ref[...] = h_ref[...] + 0.5 * acc_ref[...]

    tt = FFN_TT
    return pl.pallas_call(
        body,
        grid=(T // tt, FFN_TILES),
        in_specs=[
            pl.BlockSpec((tt, D), lambda i, j: (i, 0)),
            pl.BlockSpec((1, D), lambda i, j: (0, 0)),
            pl.BlockSpec((2, None, None, D, FFN_SHARD), lambda i, j: (0, j, layer, 0, 0)),
            pl.BlockSpec((2, None, FFN_SHARD // 2, D), lambda i, j: (j, layer, 0, 0)),
        ],
        out_specs=[
            pl.BlockSpec((tt, D), lambda i, j: (i, 0)),
            pl.BlockSpec((tt, D), lambda i, j: (i, 0)),
            pl.BlockSpec((None, tt, FFN_SHARD), lambda i, j: (j, i, 0)),
            pl.BlockSpec((None, tt, FFN_SHARD), lambda i, j: (j, i, 0)),
        ],
        out_shape=[
            jax.ShapeDtypeStruct((T, D), F32),
            jax.ShapeDtypeStruct((T, D), BF16),
            jax.ShapeDtypeStruct((FFN_TILES, T, FFN_SHARD), BF16),
            jax.ShapeDtypeStruct((FFN_TILES, T, FFN_SHARD), BF16),
        ],
        scratch_shapes=[pltpu.VMEM((tt, D), F32), pltpu.VMEM((tt, D), BF16)],
        compiler_params=_cp("parallel", "arbitrary"),
        name=f"ffn{tag}_fwd_{layer}",
    )(h, g_row, w_in, w_out)


def ffn_bwd(hn, a, b, dres, w_in, w_out, layer, tag=""):
    tt = FFN_TT
    nt = T // tt

    def body(hn_ref, a_ref, b_ref, dres_ref, win_ref, wout_ref, dhn_ref, dwin_ref, dwout_ref, gin_ref, gout_ref):
        i = pl.program_id(1)
        do = (0.5 * dres_ref[...]).astype(BF16)
        wo = wout_ref[...].reshape(FFN_SHARD, D)
        dact = _dot_nt(do, wo)
        a = a_ref[...].astype(F32)
        b = b_ref[...].astype(F32)
        sig = _sigmoid(a)
        s = a * sig
        act = (s * b).astype(BF16)
        db = (dact * s).astype(BF16)
        da = (dact * b * _dsilu(a, sig)).astype(BF16)
        hn = hn_ref[...]
        gwo = _dot_tn(act, do)
        gwg = _dot_tn(hn, da)
        gwu = _dot_tn(hn, db)

        @pl.when(i == 0)
        def _():
            gout_ref[...] = gwo
            gin_ref[0] = gwg
            gin_ref[1] = gwu

        @pl.when(i > 0)
        def _():
            gout_ref[...] += gwo
            gin_ref[0] += gwg
            gin_ref[1] += gwu

        dhn_ref[...] = _dot_nt(da, win_ref[0]) + _dot_nt(db, win_ref[1])

        @pl.when(i == nt - 1)
        def _():
            dwin_ref[...] = gin_ref[...].astype(BF16)
            dwout_ref[...] = gout_ref[...].astype(BF16)

    return pl.pallas_call(
        body,
        grid=(FFN_TILES, nt),
        in_specs=[
            pl.BlockSpec((tt, D), lambda j, i: (i, 0)),
            pl.BlockSpec((None, tt, FFN_SHARD), lambda j, i: (j, i, 0)),
            pl.BlockSpec((None, tt, FFN_SHARD), lambda j, i: (j, i, 0)),
            pl.BlockSpec((tt, D), lambda j, i: (i, 0)),
            pl.BlockSpec((2, None, None, D, FFN_SHARD), lambda j, i: (0, j, layer, 0, 0)),
            pl.BlockSpec((2, None, FFN_SHARD // 2, D), lambda j, i: (j, layer, 0, 0)),
        ],
        out_specs=[
            pl.BlockSpec((None, tt, D), lambda j, i: (j, i, 0)),
            pl.BlockSpec((2, None, D, FFN_SHARD), lambda j, i: (0, j, 0, 0)),
            pl.BlockSpec((None, FFN_SHARD, D), lambda j, i: (j, 0, 0)),
        ],
        out_shape=[
            jax.ShapeDtypeStruct((FFN_TILES, T, D), F32),
            jax.ShapeDtypeStruct((2, FFN_TILES, D, FFN_SHARD), BF16),
            jax.ShapeDtypeStruct((FFN_TILES, FFN_SHARD, D), BF16),
        ],
        scratch_shapes=[pltpu.VMEM((2, D, FFN_SHARD), F32), pltpu.VMEM((FFN_SHARD, D), F32)],
        compiler_params=_cp("parallel", "arbitrary"),
        name=f"ffn{tag}_bwd_{layer}",
    )(hn, a, b, dres, w_in, w_out)


def _rms_bwd(x, g, dy):
    rstd = lax.rsqrt(jnp.mean(x * x, axis=-1, keepdims=True) + EPS)
    xh = x * rstd
    u = dy * g
    dx = rstd * (u - xh * jnp.mean(u * xh, axis=-1, keepdims=True))
    return dx, jnp.sum(dy * xh, axis=0, keepdims=True)


def norm_bwd(x, g_row, dy_parts, dres, name):
    p = dy_parts.shape[0]
    tt = 512

    def body(x_ref, g_ref, dy_ref, dres_ref, out_ref, dg_ref):
        i = pl.program_id(0)
        dy = dy_ref[0]
        for q in range(1, p):
            dy = dy + dy_ref[q]
        dx, dg = _rms_bwd(x_ref[...], g_ref[...], dy)
        out_ref[...] = dres_ref[...] + dx

        @pl.when(i == 0)
        def _():
            dg_ref[...] = dg

        @pl.when(i > 0)
        def _():
            dg_ref[...] += dg

    return pl.pallas_call(
        body,
        grid=(T // tt,),
        in_specs=[
            pl.BlockSpec((tt, D), lambda i: (i, 0)),
            pl.BlockSpec((1, D), lambda i: (0, 0)),
            pl.BlockSpec((p, tt, D), lambda i: (0, i, 0)),
            pl.BlockSpec((tt, D), lambda i: (i, 0)),
        ],
        out_specs=[pl.BlockSpec((tt, D), lambda i: (i, 0)), pl.BlockSpec((1, D), lambda i: (0, 0))],
        out_shape=[jax.ShapeDtypeStruct((T, D), F32), jax.ShapeDtypeStruct((1, D), F32)],
        compiler_params=_cp("arbitrary"),
        name=name,
    )(x, g_row, dy_parts, dres)


def final_loss(h, g_row, target):
    tt = 512

    def body(h_ref, g_ref, t_ref, dres_ref, dg_ref, loss_ref):
        i = pl.program_id(0)
        x = h_ref[...]
        g = g_ref[...]
        err = _rms(x, g) - t_ref[...]
        part = 0.5 * jnp.sum(jnp.mean(err * err, axis=-1, keepdims=True), axis=0, keepdims=True)
        dx, dg = _rms_bwd(x, g, err * (1.0 / D))
        dres_ref[...] = dx
        part = jnp.broadcast_to(part, loss_ref.shape)

        @pl.when(i == 0)
        def _():
            dg_ref[...] = dg
            loss_ref[...] = part

        @pl.when(i > 0)
        def _():
            dg_ref[...] += dg
            loss_ref[...] += part

    return pl.pallas_call(
        body,
        grid=(T // tt,),
        in_specs=[
            pl.BlockSpec((tt, D), lambda i: (i, 0)),
            pl.BlockSpec((1, D), lambda i: (0, 0)),
            pl.BlockSpec((tt, D), lambda i: (i, 0)),
        ],
        out_specs=[
            pl.BlockSpec((tt, D), lambda i: (i, 0)),
            pl.BlockSpec((1, D), lambda i: (0, 0)),
            pl.BlockSpec((8, 128), lambda i: (0, 0)),
        ],
        out_shape=[
            jax.ShapeDtypeStruct((T, D), F32),
            jax.ShapeDtypeStruct((1, D), F32),
            jax.ShapeDtypeStruct((8, 128), F32),
        ],
        compiler_params=_cp("arbitrary"),
        name="final_loss",
    )(h, g_row, target)


PROJ_TT = 256


def rope_tables(pos_col, invf_row):
    tt = 512

    def body(p_ref, f_ref, c_ref, sm_ref, sp_ref):
        ang = p_ref[...].astype(F32) * f_ref[...]
        lane = lax.broadcasted_iota(jnp.int32, ang.shape, 1) % A_HD
        cs = jnp.cos(ang)
        sn = jnp.sin(ang)
        c_ref[...] = jnp.where(lane < ROT, cs, 1.0)
        sm_ref[...] = jnp.where(lane < ROT // 2, -sn, 0.0)
        sp_ref[...] = jnp.where((lane >= ROT // 2) & (lane < ROT), sn, 0.0)

    spec = pl.BlockSpec((tt, 128), lambda i: (i, 0))
    return pl.pallas_call(
        body,
        grid=(T // tt,),
        in_specs=[pl.BlockSpec((tt, 1), lambda i: (i, 0)), pl.BlockSpec((1, 128), lambda i: (0, 0))],
        out_specs=[spec, spec, spec],
        out_shape=[jax.ShapeDtypeStruct((T, 128), F32)] * 3,
        compiler_params=_cp("parallel"),
        name="rope_tables",
    )(pos_col, invf_row)


def make_rope(positions):
    inv_freq = jnp.power(jnp.float32(ROPE_THETA), -jnp.arange(0, ROT, 2, dtype=F32) / ROT)
    per_head = jnp.concatenate([inv_freq, inv_freq, jnp.zeros((A_HD - ROT,), F32)])
    invf_row = jnp.tile(per_head, 2)[None, :]
    return tuple(rope_tables(positions.reshape(T, 1), invf_row))


def _rope(x, c, sm, sp):
    return x * c + pltpu.roll(x, 128 - ROT // 2, 1) * sm + pltpu.roll(x, ROT // 2, 1) * sp


def _rope_t(dy, c, sm, sp):
    return dy * c + pltpu.roll(dy * sm, ROT // 2, 1) + pltpu.roll(dy * sp, 128 - ROT // 2, 1)


def proj_fwd(h, g_row, w, splits, name, rope=None):
    tt = PROJ_TT
    n = w.shape[1]
    n_rope = 0 if rope is None else 3

    def body(h_ref, g_ref, w_ref, *rest):
        tabs = rest[:n_rope]
        hn_ref = rest[n_rope]
        outs = rest[n_rope + 1:]
        hn = _rms(h_ref[...], g_ref[...]).astype(BF16)
        hn_ref[...] = hn
        for k, ((st, wd), o_ref) in enumerate(zip(splits, outs)):
            if rope is not None and k == 0:
                c, sm, sp = (t[...] for t in tabs)
                for gi in range(wd // 128):
                    r = _dot(hn, w_ref[:, st + 128 * gi:st + 128 * (gi + 1)])
                    if gi < 2 * A_W // 128:
                        r = _rope(r, c, sm, sp)
                    o_ref[:, 128 * gi:128 * (gi + 1)] = r
            else:
                o_ref[...] = _dot(hn, w_ref[:, st:st + wd])

    tab_specs = [pl.BlockSpec((tt, 128), lambda i: (i, 0))] * n_rope
    return pl.pallas_call(
        body,
        grid=(T // tt,),
        in_specs=[
            pl.BlockSpec((tt, D), lambda i: (i, 0)),
            pl.BlockSpec((1, D), lambda i: (0, 0)),
            pl.BlockSpec((D, n), lambda i: (0, 0)),
        ] + tab_specs,
        out_specs=[pl.BlockSpec((tt, D), lambda i: (i, 0))]
        + [pl.BlockSpec((tt, wd), lambda i: (i, 0)) for _, wd in splits],
        out_shape=[jax.ShapeDtypeStruct((T, D), BF16)]
        + [jax.ShapeDtypeStruct((T, wd), F32) for _, wd in splits],
        compiler_params=_cp("parallel"),
        name=name,
    )(h, g_row, w, *(rope or ()))


def proj_bwd_data(x, g_row, w, dparts, splits, dres, name, rope=None, n_rot=0):
    tt = PROJ_TT
    n = w.shape[1]
    n_rope = 0 if rope is None else 3
    k_parts = len(dparts)

    def body(x_ref, g_ref, w_ref, dres_ref, *rest):
        d_refs = rest[:k_parts]
        tabs = rest[k_parts:k_parts + n_rope]
        out_ref, dg_ref = rest[k_parts + n_rope:k_parts + n_rope + 2]
        unrot_refs = rest[k_parts + n_rope + 2:]
        i = pl.program_id(0)
        dhn = jnp.zeros((tt, D), F32)
        for k, ((st, wd), d_ref) in enumerate(zip(splits, d_refs)):
            if k < n_rot:
                c, sm, sp = (t[...] for t in tabs)
                for gi in range(wd // 128):
                    d = _rope_t(d_ref[:, 128 * gi:128 * (gi + 1)], c, sm, sp)
                    unrot_refs[k][:, 128 * gi:128 * (gi + 1)] = d
                    dhn = dhn + _dot_nt(d.astype(BF16), w_ref[:, st + 128 * gi:st + 128 * (gi + 1)])
            else:
                dhn = dhn + _dot_nt(d_ref[...].astype(BF16), w_ref[:, st:st + wd])
        dx, dg = _rms_bwd(x_ref[...], g_ref[...], dhn)
        out_ref[...] = dres_ref[...] + dx

        @pl.when(i == 0)
        def _():
            dg_ref[...] = dg

        @pl.when(i > 0)
        def _():
            dg_ref[...] += dg

    tab_specs = [pl.BlockSpec((tt, 128), lambda i: (i, 0))] * n_rope
    out_specs = [pl.BlockSpec((tt, D), lambda i: (i, 0)), pl.BlockSpec((1, D), lambda i: (0, 0))]
    out_shape = [jax.ShapeDtypeStruct((T, D), F32), jax.ShapeDtypeStruct((1, D), F32)]
    for k in range(n_rot):
        out_specs.append(pl.BlockSpec((tt, splits[k][1]), lambda i: (i, 0)))
        out_shape.append(jax.ShapeDtypeStruct((T, splits[k][1]), F32))
    return pl.pallas_call(
        body,
        grid=(T // tt,),
        in_specs=[
            pl.BlockSpec((tt, D), lambda i: (i, 0)),
            pl.BlockSpec((1, D), lambda i: (0, 0)),
            pl.BlockSpec((D, n), lambda i: (0, 0)),
            pl.BlockSpec((tt, D), lambda i: (i, 0)),
        ] + [pl.BlockSpec((tt, wd), lambda i: (i, 0)) for _, wd in splits] + tab_specs,
        out_specs=out_specs,
        out_shape=out_shape,
        compiler_params=_cp("arbitrary"),
        name=name,
    )(x, g_row, w, dres, *dparts, *(rope or ()))


def mm_tn(x, d, name):
    k = x.shape[1]
    n = d.shape[1]
    wn = n if n <= 512 else 512
    tt = 512

    def body(x_ref, d_ref, o_ref):
        i = pl.program_id(1)
        r = _dot_tn(x_ref[...], d_ref[...].astype(BF16))

        @pl.when(i == 0)
        def _():
            o_ref[...] = r

        @pl.when(i > 0)
        def _():
            o_ref[...] += r

    return pl.pallas_call(
        body,
        grid=(n // wn, T // tt),
        in_specs=[pl.BlockSpec((tt, k), lambda j, i: (i, 0)), pl.BlockSpec((tt, wn), lambda j, i: (i, j))],
        out_specs=pl.BlockSpec((k, wn), lambda j, i: (0, j)),
        out_shape=jax.ShapeDtypeStruct((k, n), F32),
        compiler_params=_cp("parallel", "arbitrary"),
        name=name,
    )(x, d)


CONV_RC = 128
CONV_PAD = 32


def hyb_conv_fwd(u, dw_w, dw_b, name):
    def body(ua_ref, ug_ref, w_ref, b_ref, o_ref, xpad):
        xpad[0:CONV_PAD, :] = jnp.zeros((CONV_PAD, 128), F32)
        xpad[CONV_PAD:, :] = ua_ref[...] * _sigmoid(ug_ref[...])
        for r in range(T // CONV_RC):
            acc = jnp.broadcast_to(b_ref[...], (CONV_RC, 128))
            for j in range(CONV_K):
                acc = acc + w_ref[pl.ds(j, 1), :] * xpad[pl.ds(r * CONV_RC + CONV_PAD - (CONV_K - 1) + j, CONV_RC), :]
            o_ref[r * CONV_RC:(r + 1) * CONV_RC, :] = acc

    nb = CONV_C // 128
    return pl.pallas_call(
        body,
        grid=(nb,),
        in_specs=[
            pl.BlockSpec((T, 128), lambda c: (0, c)),
            pl.BlockSpec((T, 128), lambda c: (0, nb + c)),
            pl.BlockSpec((32, 128), lambda c: (0, c)),
            pl.BlockSpec((1, 128), lambda c: (0, c)),
        ],
        out_specs=pl.BlockSpec((T, 128), lambda c: (0, c)),
        out_shape=jax.ShapeDtypeStruct((T, CONV_C), F32),
        scratch_shapes=[pltpu.VMEM((T + CONV_PAD, 128), F32)],
        compiler_params=_cp("parallel"),
        name=name,
    )(u, u, dw_w, dw_b)


def hyb_conv_bwd(dc, u, dw_w, name):
    def body(dc_ref, ua_ref, ug_ref, w_ref, da_ref, dgate_ref, dw_ref, db_ref, xpad, dcpad, dwacc):
        ua = ua_ref[...]
        sig = _sigmoid(ug_ref[...])
        xpad[0:CONV_PAD, :] = jnp.zeros((CONV_PAD, 128), F32)
        xpad[CONV_PAD:, :] = ua * sig
        dcpad[0:T, :] = dc_ref[...]
        dcpad[T:, :] = jnp.zeros((CONV_PAD, 128), F32)
        dwacc[...] = jnp.zeros_like(dwacc)
        dbacc = jnp.zeros((8, 128), F32)
        for r in range(T // CONV_RC):
            r0 = r * CONV_RC
            dcr = dc_ref[r0:r0 + CONV_RC, :]
            dbacc = dbacc + dcr.reshape(CONV_RC // 8, 8, 128).sum(axis=0)
            dglu = jnp.zeros((CONV_RC, 128), F32)
            for j in range(CONV_K):
                dglu = dglu + w_ref[pl.ds(j, 1), :] * dcpad[pl.ds(r0 + (CONV_K - 1) - j, CONV_RC), :]
                prod = dcr * xpad[pl.ds(r0 + CONV_PAD - (CONV_K - 1) + j, CONV_RC), :]
                dwacc[8 * j:8 * j + 8, :] += prod.reshape(CONV_RC // 8, 8, 128).sum(axis=0)
            sg = sig[r0:r0 + CONV_RC, :]
            da_ref[r0:r0 + CONV_RC, :] = dglu * sg
            dgate_ref[r0:r0 + CONV_RC, :] = dglu * ua[r0:r0 + CONV_RC, :] * sg * (1.0 - sg)
        for j in range(CONV_K):
            dw_ref[pl.ds(j, 1), :] = jnp.sum(dwacc[8 * j:8 * j + 8, :], axis=0, keepdims=True)
        dw_ref[pl.ds(CONV_K, 1), :] = jnp.zeros((1, 128), F32)
        db_ref[...] = jnp.sum(dbacc, axis=0, keepdims=True)

    nb = CONV_C // 128
    col = pl.BlockSpec((T, 128), lambda c: (0, c))
    return pl.pallas_call(
        body,
        grid=(nb,),
        in_specs=[col, col, pl.BlockSpec((T, 128), lambda c: (0, nb + c)), pl.BlockSpec((32, 128), lambda c: (0, c))],
        out_specs=[col, col, pl.BlockSpec((32, 128), lambda c: (0, c)), pl.BlockSpec((1, 128), lambda c: (0, c))],
        out_shape=[
            jax.ShapeDtypeStruct((T, CONV_C), F32),
            jax.ShapeDtypeStruct((T, CONV_C), F32),
            jax.ShapeDtypeStruct((32, CONV_C), F32),
            jax.ShapeDtypeStruct((1, CONV_C), F32),
        ],
        scratch_shapes=[
            pltpu.VMEM((T + CONV_PAD, 128), F32),
            pltpu.VMEM((T + CONV_PAD, 128), F32),
            pltpu.VMEM((8 * 32, 128), F32),
        ],
        compiler_params=_cp("parallel"),
        name=name,
    )(dc, u, u, dw_w)


ATT_SCALE = A_HD ** -0.5
N_BLK = T // BLK


def _att_masks():
    i = lax.broadcasted_iota(jnp.int32, (BLK, 2 * BLK), 0)
    j = lax.broadcasted_iota(jnp.int32, (BLK, 2 * BLK), 1)
    band = (j >= i) & (j <= i + BLK)
    i1 = lax.broadcasted_iota(jnp.int32, (BLK, BLK), 0)
    j1 = lax.broadcasted_iota(jnp.int32, (BLK, BLK), 1)
    return band, j1 <= i1


def _att_rows(d, t, first):
    if first:
        base = t
        return pl.ds(base, BLK, stride=d), pl.ds(base, BLK, stride=d)
    c = t % d
    n = t // d + 1
    base = c + (BLK * d) * n
    return pl.ds(base, BLK, stride=d), pl.ds(base - BLK * d, 2 * BLK, stride=d)


def attn_fwd(qkv, name):
    def body(q_ref, k_ref, v_ref, o_ref, lse_ref, og, lg):
        band, tri = _att_masks()
        head0 = lax.broadcasted_iota(jnp.int32, (BLK, 128), 1) < A_HD
        for g, d in enumerate(DILATIONS):
            def block(t, carry, first, g=g, d=d):
                rq, rk = _att_rows(d, t, first)
                q2 = q_ref[rq, :]
                k2 = k_ref[rk, :].astype(BF16)
                v2 = v_ref[rk, :].astype(BF16)
                o_e, l_e = [], []
                for e in range(2):
                    qe = jnp.where(head0 if e == 0 else ~head0, q2, 0.0).astype(BF16)
                    s = _dot_nt(qe, k2) * ATT_SCALE
                    s = jnp.where(tri if first else band, s, -jnp.inf)
                    m = jnp.max(s, axis=-1, keepdims=True)
                    p = jnp.exp(s - m)
                    den = jnp.sum(p, axis=-1, keepdims=True)
                    o_e.append(_dot(p.astype(BF16), v2) / den)
                    l_e.append(m + jnp.log(den))
                og[g, rq, :] = jnp.where(head0, o_e[0], o_e[1])
                lg[g, rq, :] = jnp.where(head0, l_e[0], l_e[1])
                return carry

            lax.fori_loop(0, d, functools.partial(block, first=True), 0)
            lax.fori_loop(0, N_BLK - d, functools.partial(block, first=False), 0)
        rc = 256
        for r in range(T // rc):
            rows = pl.ds(r * rc, rc)
            l0, l1, l2 = lg[0, rows, :], lg[1, rows, :], lg[2, rows, :]
            m = jnp.maximum(jnp.maximum(l0, l1), l2)
            e0, e1, e2 = jnp.exp(l0 - m), jnp.exp(l1 - m), jnp.exp(l2 - m)
            z = e0 + e1 + e2
            o_ref[rows, :] = (e0 / z) * og[0, rows, :] + (e1 / z) * og[1, rows, :] + (e2 / z) * og[2, rows, :]
            lse_ref[rows, :] = m + jnp.log(z)

    npair = A_HEADS // 2
    col = lambda off: pl.BlockSpec((T, 128), lambda p: (0, off + p))
    return pl.pallas_call(
        body,
        grid=(npair,),
        in_specs=[col(0), col(npair), col(2 * npair)],
        out_specs=[col(0), col(0)],
        out_shape=[jax.ShapeDtypeStruct((T, A_W), F32), jax.ShapeDtypeStruct((T, A_W), F32)],
        scratch_shapes=[pltpu.VMEM((3, T, 128), F32), pltpu.VMEM((3, T, 128), F32)],
        compiler_params=_cp("parallel"),
        name=name,
    )(qkv, qkv, qkv)


def attn_bwd(qkv, o, lse, do, name):
    def body(q_ref, k_ref, v_ref, o_ref, lse_ref, do_ref, dq_ref, dk_ref, dv_ref):
        band, tri = _att_masks()
        head0 = lax.broadcasted_iota(jnp.int32, (BLK, 128), 1) < A_HD
        head0k = lax.broadcasted_iota(jnp.int32, (2 * BLK, 128), 1) < A_HD
        dq_ref[...] = jnp.zeros_like(dq_ref)
        dk_ref[...] = jnp.zeros_like(dk_ref)
        dv_ref[...] = jnp.zeros_like(dv_ref)
        for d in DILATIONS:
            def block(t, carry, first, d=d):
                rq, rk = _att_rows(d, t, first)
                q2 = q_ref[rq, :]
                k2 = k_ref[rk, :].astype(BF16)
                v2 = v_ref[rk, :].astype(BF16)
                do2 = do_ref[rq, :]
                l2 = lse_ref[rq, :]
                prod = do2 * o_ref[rq, :]
                q2b = q2.astype(BF16)
                do2b = do2.astype(BF16)
                dq_e, dk_e, dv_e = [], [], []
                for e in range(2):
                    he = head0 if e == 0 else ~head0
                    qe = jnp.where(he, q2, 0.0).astype(BF16)
                    doe = jnp.where(he, do2, 0.0).astype(BF16)
                    l = l2[:, A_HD * e:A_HD * e + 1]
                    dd = jnp.sum(jnp.where(he, prod, 0.0), axis=-1, keepdims=True)
                    s = _dot_nt(qe, k2) * ATT_SCALE
                    p = jnp.where(tri if first else band, jnp.exp(s - l), 0.0)
                    dp = _dot_nt(doe, v2)
                    ds = (p * (dp - dd) * ATT_SCALE).astype(BF16)
                    dq_e.append(_dot(ds, k2))
                    dk_e.append(_dot_tn(ds, q2b))
                    dv_e.append(_dot_tn(p.astype(BF16), do2b))
                hk = head0 if first else head0k
                dq_ref[rq, :] += jnp.where(head0, dq_e[0], dq_e[1])
                dk_ref[rk, :] += jnp.where(hk, dk_e[0], dk_e[1])
                dv_ref[rk, :] += jnp.where(hk, dv_e[0], dv_e[1])
                return carry

            lax.fori_loop(0, d, functools.partial(block, first=True), 0)
            lax.fori_loop(0, N_BLK - d, functools.partial(block, first=False), 0)

    npair = A_HEADS // 2
    col = lambda off: pl.BlockSpec((T, 128), lambda p: (0, off + p))
    return pl.pallas_call(
        body,
        grid=(npair,),
        in_specs=[col(0), col(npair), col(2 * npair), col(0), col(0), col(0)],
        out_specs=[col(0), col(0), col(0)],
        out_shape=[jax.ShapeDtypeStruct((T, A_W), F32)] * 3,
        compiler_params=_cp("parallel"),
        name=name,
    )(qkv, qkv, qkv, o, lse, do)


def _ln_silu(x, g, b):
    mu = jnp.mean(x, axis=-1, keepdims=True)
    xc = x - mu
    rstd = lax.rsqrt(jnp.mean(xc * xc, axis=-1, keepdims=True) + EPS)
    xh = xc * rstd
    y = xh * g + b
    sig = _sigmoid(y)
    return y * sig, (xh, rstd, y, sig)


def hyb_out_fwd(h, attn, cpre, ln_g, ln_b, w_out, name):
    tt = 512

    def body(h_ref, a_ref, c_ref, g_ref, b_ref, w_ref, hnew_ref, cat_ref):
        cn, _ = _ln_silu(c_ref[...], g_ref[...], b_ref[...])
        ab = a_ref[...].astype(BF16)
        cb = cn.astype(BF16)
        cat_ref[:, 0:A_W] = ab
        cat_ref[:, A_W:D] = cb
        hnew_ref[...] = h_ref[...] + _dot(ab, w_ref[0:A_W, :]) + _dot(cb, w_ref[A_W:D, :])

    half = pl.BlockSpec((tt, A_W), lambda i: (i, 0))
    vec = pl.BlockSpec((1, CONV_C), lambda i: (0, 0))
    full = pl.BlockSpec((tt, D), lambda i: (i, 0))
    return pl.pallas_call(
        body,
        grid=(T // tt,),
        in_specs=[full, half, half, vec, vec, pl.BlockSpec((D, D), lambda i: (0, 0))],
        out_specs=[full, full],
        out_shape=[jax.ShapeDtypeStruct((T, D), F32), jax.ShapeDtypeStruct((T, D), BF16)],
        compiler_params=_cp("parallel"),
        name=name,
    )(h, attn, cpre, ln_g, ln_b, w_out)


def hyb_out_bwd(dres, cpre, ln_g, ln_b, w_out, name):
    tt = 512

    def body(d_ref, c_ref, g_ref, b_ref, w_ref, da_ref, dc_ref, dg_ref, db_ref):
        i = pl.program_id(0)
        db16 = d_ref[...].astype(BF16)
        da_ref[...] = _dot_nt(db16, w_ref[0:A_W, :])
        dcn = _dot_nt(db16, w_ref[A_W:D, :])
        g = g_ref[...]
        _, (xh, rstd, y, sig) = _ln_silu(c_ref[...], g, b_ref[...])
        dy = dcn * _dsilu(y, sig)
        dxh = dy * g
        dc_ref[...] = rstd * (dxh - jnp.mean(dxh, axis=-1, keepdims=True)
                              - xh * jnp.mean(dxh * xh, axis=-1, keepdims=True))
        dg = jnp.sum(dy * xh, axis=0, keepdims=True)
        db = jnp.sum(dy, axis=0, keepdims=True)

        @pl.when(i == 0)
        def _():
            dg_ref[...] = dg
            db_ref[...] = db

        @pl.when(i > 0)
        def _():
            dg_ref[...] += dg
            db_ref[...] += db

    half = pl.BlockSpec((tt, A_W), lambda i: (i, 0))
    vec = pl.BlockSpec((1, CONV_C), lambda i: (0, 0))
    return pl.pallas_call(
        body,
        grid=(T // tt,),
        in_specs=[pl.BlockSpec((tt, D), lambda i: (i, 0)), half, vec, vec, pl.BlockSpec((D, D), lambda i: (0, 0))],
        out_specs=[half, half, vec, vec],
        out_shape=[
            jax.ShapeDtypeStruct((T, A_W), F32),
            jax.ShapeDtypeStruct((T, CONV_C), F32),
            jax.ShapeDtypeStruct((1, CONV_C), F32),
            jax.ShapeDtypeStruct((1, CONV_C), F32),
        ],
        compiler_params=_cp("arbitrary"),
        name=name,
    )(dres, cpre, ln_g, ln_b, w_out)


def hybrid_fwd(h, g_row, w_in, dw_w, dw_b, ln_g, ln_b, w_out, rope, tag):
    hn, qkv, u = proj_fwd(h, g_row, w_in, [(0, 3 * A_W), (3 * A_W, 2 * CONV_C)], f"hyb_proj_{tag}", rope=rope)
    cpre = hyb_conv_fwd(u, dw_w, dw_b, f"hyb_conv_{tag}")
    attn, lse = attn_fwd(qkv, f"attn_fwd_{tag}")
    hnew, cat = hyb_out_fwd(h, attn, cpre, ln_g, ln_b, w_out, f"hyb_out_{tag}")
    return hnew, (h, hn, qkv, u, cpre, attn, lse, cat)


def hybrid_bwd(dres, saved, g_row, w_in, dw_w, ln_g, ln_b, w_out, rope, tag):
    h, hn, qkv, u, cpre, attn, lse, cat = saved
    d_attn, d_cpre, d_lng, d_lnb = hyb_out_bwd(dres, cpre, ln_g, ln_b, w_out, f"hyb_out_bwd_{tag}")
    d_wout = mm_tn(cat, dres, f"hyb_wout_grad_{tag}")
    d_a, d_gate, d_dw, d_db = hyb_conv_bwd(d_cpre, u, dw_w, f"hyb_conv_bwd_{tag}")
    dq, dk, dv = attn_bwd(qkv, attn, lse, d_attn, f"attn_bwd_{tag}")
    splits = [(0, A_W), (A_W, A_W), (2 * A_W, A_W), (3 * A_W, CONV_C), (3 * A_W + CONV_C, CONV_C)]
    dres_new, d_norm, dq_u, dk_u = proj_bwd_data(
        h, g_row, w_in, [dq, dk, dv, d_a, d_gate], splits, dres, f"hyb_proj_bwd_{tag}", rope=rope, n_rot=2)
    parts = [dq_u, dk_u, dv, d_a, d_gate]
    d_win = jnp.concatenate([mm_tn(hn, p, f"hyb_win_grad_{tag}_{k}") for k, p in enumerate(parts)], axis=1)
    return dres_new, dict(norm=d_norm, w_in=d_win, dw_w=d_dw[:CONV_K], dw_b=d_db, ln_g=d_lng, ln_b=d_lnb, w_out=d_wout)


G_SCALE = G_DK ** -0.5
GP_RC = 256
GP_PAD = 8


def gdn_prep_fwd(x, conv_w, name):
    def body(x_ref, w_ref, o_ref, xpad):
        cb = pl.program_id(0)
        xpad[0:GP_PAD, :] = jnp.zeros((GP_PAD, 128), F32)
        xpad[GP_PAD:, :] = x_ref[...]
        for r in range(T // GP_RC):
            r0 = r * GP_RC
            y = jnp.zeros((GP_RC, 128), F32)
            for j in range(G_CONV):
                y = y + w_ref[pl.ds(j, 1), :] * xpad[pl.ds(r0 + GP_PAD - (G_CONV - 1) + j, GP_RC), :]
            s = y * _sigmoid(y)
            n = lax.rsqrt(jnp.sum(s * s, axis=-1, keepdims=True) + EPS)
            o_ref[r0:r0 + GP_RC, :] = s * jnp.where(cb < 2 * G_HEADS, n, 1.0)

    nb = G_QKV // 128
    return pl.pallas_call(
        body,
        grid=(nb,),
        in_specs=[pl.BlockSpec((T, 128), lambda c: (0, c)), pl.BlockSpec((G_CONV, 128), lambda c: (0, c))],
        out_specs=pl.BlockSpec((T, 128), lambda c: (0, c)),
        out_shape=jax.ShapeDtypeStruct((T, G_QKV), F32),
        scratch_shapes=[pltpu.VMEM((T + GP_PAD, 128), F32)],
        compiler_params=_cp("parallel"),
        name=name,
    )(x, conv_w)


def gdn_prep_bwd(dout, x, conv_w, part, l2, name):
    def body(d_ref, x_ref, w_ref, dx_ref, dw_ref, xpad, dypad, dwacc):
        xpad[0:GP_PAD, :] = jnp.zeros((GP_PAD, 128), F32)
        xpad[GP_PAD:, :] = x_ref[...]
        dypad[T:, :] = jnp.zeros((GP_PAD, 128), F32)
        dwacc[...] = jnp.zeros_like(dwacc)
        for r in range(T // GP_RC):
            r0 = r * GP_RC
            y = jnp.zeros((GP_RC, 128), F32)
            xs = []
            for j in range(G_CONV):
                xj = xpad[pl.ds(r0 + GP_PAD - (G_CONV - 1) + j, GP_RC), :]
                xs.append(xj)
                y = y + w_ref[pl.ds(j, 1), :] * xj
            sig = _sigmoid(y)
            s = y * sig
            d = d_ref[r0:r0 + GP_RC, :]
            if l2:
                n = lax.rsqrt(jnp.sum(s * s, axis=-1, keepdims=True) + EPS)
                out = s * n
                d = n * (d - out * jnp.sum(d * out, axis=-1, keepdims=True))
            dy = d * _dsilu(y, sig)
            dypad[r0:r0 + GP_RC, :] = dy
            for j in range(G_CONV):
                dwacc[8 * j:8 * j + 8, :] += (dy * xs[j]).reshape(GP_RC // 8, 8, 128).sum(axis=0)
        for r in range(T // GP_RC):
            r0 = r * GP_RC
            dx = jnp.zeros((GP_RC, 128), F32)
            for j in range(G_CONV):
                dx = dx + w_ref[pl.ds(j, 1), :] * dypad[pl.ds(r0 + (G_CONV - 1) - j, GP_RC), :]
            dx_ref[r0:r0 + GP_RC, :] = dx
        for j in range(G_CONV):
            dw_ref[pl.ds(j, 1), :] = jnp.sum(dwacc[8 * j:8 * j + 8, :], axis=0, keepdims=True)

    nb = G_HEADS
    off = part * nb
    col = pl.BlockSpec((T, 128), lambda c: (0, c))
    return pl.pallas_call(
        body,
        grid=(nb,),
        in_specs=[col, pl.BlockSpec((T, 128), lambda c: (0, off + c)), pl.BlockSpec((G_CONV, 128), lambda c: (0, off + c))],
        out_specs=[col, pl.BlockSpec((G_CONV, 128), lambda c: (0, c))],
        out_shape=[jax.ShapeDtypeStruct((T, G_HEADS * G_DK), F32), jax.ShapeDtypeStruct((G_CONV, G_HEADS * G_DK), F32)],
        scratch_shapes=[
            pltpu.VMEM((T + GP_PAD, 128), F32),
            pltpu.VMEM((T + GP_PAD, 128), F32),
            pltpu.VMEM((8 * G_CONV, 128), F32),
        ],
        compiler_params=_cp("parallel"),
        name=name,
    )(dout, x, conv_w)


def _seg_cumsum(x, reverse=False):
    row = lax.broadcasted_iota(jnp.int32, x.shape, 0) % CH
    s = 1
    while s < CH:
        if reverse:
            x = x + jnp.where(row < CH - s, pltpu.roll(x, x.shape[0] - s, 0), 0.0)
        else:
            x = x + jnp.where(row >= s, pltpu.roll(x, s, 0), 0.0)
        s *= 2
    return x


def _gdn_gates(ba_ref, alog_ref, dt_ref, h):
    ba = ba_ref[...]
    lane = lax.broadcasted_iota(jnp.int32, ba.shape, 1)
    b_col = jnp.sum(jnp.where(lane == h, ba, 0.0), axis=1, keepdims=True)
    a_col = jnp.sum(jnp.where(lane == G_HEADS + h, ba, 0.0), axis=1, keepdims=True)
    lane8 = lax.broadcasted_iota(jnp.int32, (1, G_HEADS), 1)
    alog = jnp.sum(jnp.where(lane8 == h, alog_ref[...], 0.0), axis=1, keepdims=True)
    dt = jnp.sum(jnp.where(lane8 == h, dt_ref[...], 0.0), axis=1, keepdims=True)
    beta = _sigmoid(b_col)
    xa = a_col + dt
    softplus = jnp.maximum(xa, 0.0) + jnp.log(1.0 + jnp.exp(-jnp.abs(xa)))
    ea = jnp.exp(alog)
    return beta, -ea * softplus, xa, ea


def _chunk_masks():
    i = lax.broadcasted_iota(jnp.int32, (CH, CH), 0)
    j = lax.broadcasted_iota(jnp.int32, (CH, CH), 1)
    return i >= j, i > j, i, j


def _decay(gcc, causal):
    gm = gcc[:, 0:CH]
    return jnp.where(causal, jnp.exp(jnp.minimum(gm - gm.T, 0.0)), 0.0)


def _split(a):
    hi = a.astype(BF16)
    return hi, (a - hi.astype(F32)).astype(BF16)


def _dot3(a, b):
    ah, al = _split(a)
    bh, bl = _split(b)
    return _dot(ah, bh) + (_dot(ah, bl) + _dot(al, bh))


def _unit_lower_inverse(lms, i, j):
    eye = jnp.where(i == j, 1.0, 0.0)
    ms = [None] * len(lms)
    b = 1
    while b < CH:
        pair = ((i // (2 * b)) == (j // (2 * b))) & ((i // b) % 2 == 1) & ((j // b) % 2 == 0)
        lbs = [jnp.where(pair, lm, 0.0) for lm in lms]
        if b == 1:
            ms = [eye - lb for lb in lbs]
        else:
            ts = [_dot3(m, lb) for m, lb in zip(ms, lbs)]
            ms = [m - _dot3(t, m) for m, t in zip(ms, ts)]
        b *= 2
    return ms


def gdn_local_fwd(qkv, ba, alog, dtb, name):
    def body(q_ref, k_ref, v_ref, ba_ref, al_ref, dt_ref, u_ref, w_ref, qd_ref, kd_ref, at_ref, el_ref, ti_ref, gcs):
        h = pl.program_id(1)
        beta, g, _, _ = _gdn_gates(ba_ref, al_ref, dt_ref, h)
        gc = _seg_cumsum(jnp.broadcast_to(g, (GRP, 128)))
        gcs[...] = gc
        causal, strict, i, j = _chunk_masks()
        lms = []
        for c in range(CPG):
            r = slice(c * CH, (c + 1) * CH)
            q, k = q_ref[r, :], k_ref[r, :]
            gcc = gc[r, :]
            ec = jnp.exp(gcc)
            gl = gcs[pl.ds(c * CH + CH - 1, 1), :]
            dm = _decay(gcc, causal)
            kbf = k.astype(BF16)
            a1 = _dot_nt((k * beta[r, :]).astype(BF16), kbf)
            lms.append(jnp.where(strict, a1 * dm, 0.0))
            qs = q * G_SCALE
            qd_ref[r, :] = (qs * ec).astype(BF16)
            kd_ref[r, :] = (k * jnp.exp(gl - gcc)).astype(BF16)
            at_ref[r, :] = (_dot_nt(qs.astype(BF16), kbf) * dm).astype(BF16)
            el_ref[pl.ds(c, 1), :] = jnp.exp(gl)
        tinvs = _unit_lower_inverse(lms, i, j)
        for c in range(CPG):
            r = slice(c * CH, (c + 1) * CH)
            bt = beta[r, :]
            tb = tinvs[c].astype(BF16)
            u_ref[r, :] = _dot(tb, (v_ref[r, :] * bt).astype(BF16))
            w_ref[r, :] = _dot(tb, (k_ref[r, :] * bt * jnp.exp(gc[r, :])).astype(BF16)).astype(BF16)
            ti_ref[r, :] = tinvs[c]

    hd = lambda off: pl.BlockSpec((GRP, 128), lambda i, h: (i, off + h))
    vec = pl.BlockSpec((1, G_HEADS), lambda i, h: (0, 0))
    sq = pl.BlockSpec((None, GRP, CH), lambda i, h: (h, i, 0))
    return pl.pallas_call(
        body,
        grid=(N_GRP, G_HEADS),
        in_specs=[hd(0), hd(G_HEADS), hd(2 * G_HEADS), pl.BlockSpec((GRP, 2 * G_HEADS), lambda i, h: (i, 0)), vec, vec],
        out_specs=[hd(0), hd(0), hd(0), hd(0), sq, pl.BlockSpec((None, CPG, 128), lambda i, h: (h, i, 0)), sq],
        out_shape=[
            jax.ShapeDtypeStruct((T, D), F32),
            jax.ShapeDtypeStruct((T, D), BF16),
            jax.ShapeDtypeStruct((T, D), BF16),
            jax.ShapeDtypeStruct((T, D), BF16),
            jax.ShapeDtypeStruct((G_HEADS, T, CH), BF16),
            jax.ShapeDtypeStruct((G_HEADS, T // CH, 128), F32),
            jax.ShapeDtypeStruct((G_HEADS, T, CH), F32),
        ],
        scratch_shapes=[pltpu.VMEM((GRP, 128), F32)],
        compiler_params=_cp("parallel", "parallel"),
        name=name,
    )(qkv, qkv, qkv, ba, alog, dtb)


def gdn_rec_fwd(u, w, qd, kd, at, el, name):
    def body(u_ref, w_ref, qd_ref, kd_ref, at_ref, el_ref, o_ref, vn_ref, st_ref, s_scr):
        @pl.when(pl.program_id(0) == 0)
        def _():
            s_scr[...] = jnp.zeros_like(s_scr)

        for c in range(CPG):
            r = slice(c * CH, (c + 1) * CH)
            for h in range(G_HEADS):
                ln = slice(h * 128, (h + 1) * 128)
                s = s_scr[h]
                st_ref[h, c] = s
                sb = s.astype(BF16)
                vn = (u_ref[r, ln] - _dot(w_ref[r, ln], sb)).astype(BF16)
                o_ref[r, ln] = _dot(qd_ref[r, ln], sb) + _dot(at_ref[h, r, :], vn)
                s_scr[h] = s * el_ref[h, pl.ds(c, 1), :] + _dot_tn(kd_ref[r, ln], vn)
                vn_ref[r, ln] = vn

    row = pl.BlockSpec((GRP, D), lambda i: (i, 0))
    return pl.pallas_call(
        body,
        grid=(N_GRP,),
        in_specs=[row, row, row, row, pl.BlockSpec((G_HEADS, GRP, CH), lambda i: (0, i, 0)),
                  pl.BlockSpec((G_HEADS, CPG, 128), lambda i: (0, i, 0))],
        out_specs=[row, row, pl.BlockSpec((G_HEADS, CPG, 128, 128), lambda i: (0, i, 0, 0))],
        out_shape=[
            jax.ShapeDtypeStruct((T, D), F32),
            jax.ShapeDtypeStruct((T, D), BF16),
            jax.ShapeDtypeStruct((G_HEADS, T // CH, 128, 128), F32),
        ],
        scratch_shapes=[pltpu.VMEM((G_HEADS, 128, 128), F32)],
        compiler_params=_cp("arbitrary"),
        name=name,
    )(u, w, qd, kd, at, el)


def gdn_rec_bwd(do, w, qd, kd, at, el, vn, st, name):
    def body(do_ref, w_ref, qd_ref, kd_ref, at_ref, el_ref, vn_ref, st_ref,
             du_ref, dw_ref, dqd_ref, dkd_ref, dat_ref, del_ref, ds_scr):
        @pl.when(pl.program_id(0) == 0)
        def _():
            ds_scr[...] = jnp.zeros_like(ds_scr)

        for c in reversed(range(CPG)):
            r = slice(c * CH, (c + 1) * CH)
            for h in range(G_HEADS):
                ln = slice(h * 128, (h + 1) * 128)
                ds = ds_scr[h]
                dsb = ds.astype(BF16)
                sn = st_ref[h, c]
                snb = sn.astype(BF16)
                dob = do_ref[r, ln].astype(BF16)
                vnb = vn_ref[r, ln]
                dvn = (_dot(kd_ref[r, ln], dsb) + _dot_tn(at_ref[h, r, :], dob)).astype(BF16)
                du_ref[r, ln] = dvn
                dkd_ref[r, ln] = _dot_nt(vnb, dsb)
                tot = jnp.sum(jnp.sum(ds * sn, axis=1, keepdims=True), axis=0, keepdims=True)
                del_ref[h, pl.ds(c, 1), :] = jnp.broadcast_to(tot, (1, 128))
                dqd_ref[r, ln] = _dot_nt(dob, snb)
                dat_ref[h, r, :] = _dot_nt(dob, vnb)
                dw_ref[r, ln] = (-_dot_nt(dvn, snb)).astype(BF16)
                ds_scr[h] = ds * el_ref[h, pl.ds(c, 1), :] + _dot_tn(qd_ref[r, ln], dob) - _dot_tn(w_ref[r, ln], dvn)

    last = N_GRP - 1
    row = pl.BlockSpec((GRP, D), lambda i: (last - i, 0))
    sq = pl.BlockSpec((G_HEADS, GRP, CH), lambda i: (0, last - i, 0))
    sc = pl.BlockSpec((G_HEADS, CPG, 128), lambda i: (0, last - i, 0))
    return pl.pallas_call(
        body,
        grid=(N_GRP,),
        in_specs=[row, row, row, row, sq, sc, row, pl.BlockSpec((G_HEADS, CPG, 128, 128), lambda i: (0, last - i, 0, 0))],
        out_specs=[row, row, row, row, sq, sc],
        out_shape=[
            jax.ShapeDtypeStruct((T, D), BF16),
            jax.ShapeDtypeStruct((T, D), BF16),
            jax.ShapeDtypeStruct((T, D), F32),
            jax.ShapeDtypeStruct((T, D), F32),
            jax.ShapeDtypeStruct((G_HEADS, T, CH), F32),
            jax.ShapeDtypeStruct((G_HEADS, T // CH, 128), F32),
        ],
        scratch_shapes=[pltpu.VMEM((G_HEADS, 128, 128), F32)],
        compiler_params=_cp("arbitrary"),
        name=name,
    )(do, w, qd, kd, at, el, vn, st)


def gdn_local_bwd(qkv, ba, alog, dtb, tinv, du, dw, dqd, dkd, dat, dl, name):
    def body(q_ref, k_ref, v_ref, ba_ref, al_ref, dt_ref, ti_ref, du_ref, dw_ref, dqd_ref, dkd_ref, dat_ref, dl_ref,
             dq_ref, dk_ref, dv_ref, dba_ref, dal_ref, ddt_ref, gcs):
        gi = pl.program_id(0)
        h = pl.program_id(1)
        beta, g, xa, ea = _gdn_gates(ba_ref, al_ref, dt_ref, h)
        gc = _seg_cumsum(jnp.broadcast_to(g, (GRP, 128)))
        gcs[...] = gc
        causal, strict, _, _ = _chunk_masks()
        dgc_l, dgl_l, dbeta_l = [], [], []
        for c in range(CPG):
            r = slice(c * CH, (c + 1) * CH)
            q, k, v = q_ref[r, :], k_ref[r, :], v_ref[r, :]
            bt = beta[r, :]
            gcc = gc[r, :]
            ec = jnp.exp(gcc)
            gl = gcs[pl.ds(c * CH + CH - 1, 1), :]
            f2 = jnp.exp(gl - gcc)
            elc = jnp.exp(gl)
            dm = _decay(gcc, causal)
            qs = q * G_SCALE
            kb = k * bt
            vb = v * bt
            kbe = kb * ec
            kbf, kbb, qsb = k.astype(BF16), kb.astype(BF16), qs.astype(BF16)
            a1 = _dot_nt(kbb, kbf)
            qk = _dot_nt(qsb, kbf)
            ti = ti_ref[r, :]
            tb = ti.astype(BF16)
            du_c, dw_c = du_ref[r, :], dw_ref[r, :]
            dqd_c, dkd_c, dat_c = dqd_ref[r, :], dkd_ref[r, :], dat_ref[r, :]

            dqs = dqd_c * ec
            d_e = jnp.sum(dqd_c * qs, axis=1, keepdims=True)
            dk = dkd_c * f2
            tcol = jnp.sum(dkd_c * k, axis=1, keepdims=True) * f2[:, 0:1]
            dgl = jnp.sum(tcol, axis=0, keepdims=True) + dl_ref[pl.ds(c, 1), 0:1] * elc[:, 0:1]
            dgc = -tcol
            dqk = (dat_c * dm).astype(BF16)
            d_d = dat_c * qk
            dqs = dqs + _dot(dqk, kbf)
            dk = dk + _dot_tn(dqk, qsb)
            dtinv = _dot_nt(du_c, vb.astype(BF16)) + _dot_nt(dw_c, kbe.astype(BF16))
            dvb = _dot_tn(tb, du_c)
            dkbe = _dot_tn(tb, dw_c)
            dlm = jnp.where(strict, -_dot3(_dot3(ti.T, dtinv), ti.T), 0.0)
            da1 = (dlm * dm).astype(BF16)
            d_d = d_d + dlm * a1
            dkb = _dot(da1, kbf) + dkbe * ec
            dk = dk + _dot_tn(da1, kbb)
            d_e = d_e + jnp.sum(dkbe * kb, axis=1, keepdims=True)
            dk = dk + dkb * bt
            dbeta_l.append(jnp.sum(dkb * k, axis=1, keepdims=True) + jnp.sum(dvb * v, axis=1, keepdims=True))
            ddiff = d_d * dm
            dgc = dgc + jnp.sum(ddiff, axis=1, keepdims=True) - jnp.sum(ddiff.T, axis=1, keepdims=True)
            dgc = dgc + d_e * ec[:, 0:1]
            dgc_l.append(dgc)
            dgl_l.append(jnp.broadcast_to(dgl, (CH, 1)))
            dq_ref[r, :] = dqs * G_SCALE
            dk_ref[r, :] = dk
            dv_ref[r, :] = dvb * bt

        dgc_all = jnp.broadcast_to(jnp.concatenate(dgc_l, axis=0), (GRP, 128))
        dg = _seg_cumsum(dgc_all, reverse=True)[:, 0:1] + jnp.concatenate(dgl_l, axis=0)
        dbeta = jnp.concatenate(dbeta_l, axis=0)
        da = dg * (-ea) * _sigmoid(xa)
        db = dbeta * beta * (1.0 - beta)
        lane = lax.broadcasted_iota(jnp.int32, (GRP, 2 * G_HEADS), 1)
        dba = jnp.where(lane == h, db, 0.0) + jnp.where(lane == G_HEADS + h, da, 0.0)
        lane8 = lax.broadcasted_iota(jnp.int32, (1, G_HEADS), 1)
        dal = jnp.where(lane8 == h, jnp.sum(dg * g, axis=0, keepdims=True), 0.0)
        ddt = jnp.where(lane8 == h, jnp.sum(da, axis=0, keepdims=True), 0.0)

        @pl.when(h == 0)
        def _():
            dba_ref[...] = dba

        @pl.when(h > 0)
        def _():
            dba_ref[...] += dba

        @pl.when((h == 0) & (gi == 0))
        def _():
            dal_ref[...] = dal
            ddt_ref[...] = ddt

        @pl.when((h > 0) | (gi > 0))
        def _():
            dal_ref[...] += dal
            ddt_ref[...] += ddt

    hd = lambda off: pl.BlockSpec((GRP, 128), lambda i, h: (i, off + h))
    vec = pl.BlockSpec((1, G_HEADS), lambda i, h: (0, 0))
    sq = pl.BlockSpec((None, GRP, CH), lambda i, h: (h, i, 0))
    gates = pl.BlockSpec((GRP, 2 * G_HEADS), lambda i, h: (i, 0))
    return pl.pallas_call(
        body,
        grid=(N_GRP, G_HEADS),
        in_specs=[hd(0), hd(G_HEADS), hd(2 * G_HEADS), gates, vec, vec, sq, hd(0), hd(0), hd(0), hd(0), sq,
                  pl.BlockSpec((None, CPG, 128), lambda i, h: (h, i, 0))],
        out_specs=[hd(0), hd(0), hd(0), gates, vec, vec],
        out_shape=[
            jax.ShapeDtypeStruct((T, D), F32),
            jax.ShapeDtypeStruct((T, D), F32),
            jax.ShapeDtypeStruct((T, D), F32),
            jax.ShapeDtypeStruct((T, 2 * G_HEADS), F32),
            jax.ShapeDtypeStruct((1, G_HEADS), F32),
            jax.ShapeDtypeStruct((1, G_HEADS), F32),
        ],
        scratch_shapes=[pltpu.VMEM((GRP, 128), F32)],
        compiler_params=_cp("arbitrary", "arbitrary"),
        name=name,
    )(qkv, qkv, qkv, ba, alog, dtb, tinv, du, dw, dqd, dkd, dat, dl)


def _gated_norm(o, z, g):
    rstd = lax.rsqrt(jnp.mean(o * o, axis=-1, keepdims=True) + EPS)
    oh = o * rstd
    sig = _sigmoid(z)
    return oh, rstd, sig


def gdn_out_fwd(h, o, z, norm_g, w_out, name):
    tt = 512

    def body(h_ref, o_ref, z_ref, g_ref, w_ref, hnew_ref, cat_ref):
        g = g_ref[...]
        for hh in range(G_HEADS):
            ln = slice(hh * 128, (hh + 1) * 128)
            zz = z_ref[:, ln]
            oh, _, sig = _gated_norm(o_ref[:, ln], zz, g)
            cat_ref[:, ln] = (oh * g * (zz * sig)).astype(BF16)
        hnew_ref[...] = h_ref[...] + _dot(cat_ref[...], w_ref[...])

    full = pl.BlockSpec((tt, D), lambda i: (i, 0))
    return pl.pallas_call(
        body,
        grid=(T // tt,),
        in_specs=[full, full, full, pl.BlockSpec((1, 128), lambda i: (0, 0)), pl.BlockSpec((D, D), lambda i: (0, 0))],
        out_specs=[full, full],
        out_shape=[jax.ShapeDtypeStruct((T, D), F32), jax.ShapeDtypeStruct((T, D), BF16)],
        compiler_params=_cp("parallel"),
        name=name,
    )(h, o, z, norm_g, w_out)


def gdn_out_bwd(dres, o, z, norm_g, w_out, name):
    tt = 512

    def body(d_ref, o_ref, z_ref, g_ref, w_ref, do_ref, dz_ref, dg_ref, dcat):
        i = pl.program_id(0)
        g = g_ref[...]
        dcat[...] = _dot_nt(d_ref[...].astype(BF16), w_ref[...])
        dg = jnp.zeros((1, 128), F32)
        for hh in range(G_HEADS):
            ln = slice(hh * 128, (hh + 1) * 128)
            zz = z_ref[:, ln]
            oh, rstd, sig = _gated_norm(o_ref[:, ln], zz, g)
            dout = dcat[:, ln]
            dy = dout * (zz * sig)
            dz_ref[:, ln] = dout * (oh * g) * _dsilu(zz, sig)
            dg = dg + jnp.sum(dy * oh, axis=0, keepdims=True)
            doh = dy * g
            do_ref[:, ln] = rstd * (doh - oh * jnp.mean(doh * oh, axis=-1, keepdims=True))

        @pl.when(i == 0)
        def _():
            dg_ref[...] = dg

        @pl.when(i > 0)
        def _():
            dg_ref[...] += dg

    full = pl.BlockSpec((tt, D), lambda i: (i, 0))
    vec = pl.BlockSpec((1, 128), lambda i: (0, 0))
    return pl.pallas_call(
        body,
        grid=(T // tt,),
        in_specs=[full, full, full, vec, pl.BlockSpec((D, D), lambda i: (0, 0))],
        out_specs=[full, full, vec],
        out_shape=[jax.ShapeDtypeStruct((T, D), F32), jax.ShapeDtypeStruct((T, D), F32), jax.ShapeDtypeStruct((1, 128), F32)],
        scratch_shapes=[pltpu.VMEM((tt, D), F32)],
        compiler_params=_cp("arbitrary"),
        name=name,
    )(dres, o, z, norm_g, w_out)


GDN_SPLITS = [(0, 1024), (1024, 1024), (2048, 1024), (3072, 1024), (4096, 2 * G_HEADS)]


def gdn_fwd(h, g_row, w_in, conv_w, alog, dtb, norm_g, w_out, tag):
    hn, qkv_pre, z, ba = proj_fwd(h, g_row, w_in, [(0, G_QKV), (G_QKV, 1024), (4096, 2 * G_HEADS)], f"gdn_proj_{tag}")
    qkv = gdn_prep_fwd(qkv_pre, conv_w, f"gdn_prep_{tag}")
    u, w, qd, kd, at, el, tinv = gdn_local_fwd(qkv, ba, alog, dtb, f"gdn_local_{tag}")
    o, vn, st = gdn_rec_fwd(u, w, qd, kd, at, el, f"gdn_rec_{tag}")
    hnew, cat = gdn_out_fwd(h, o, z, norm_g, w_out, f"gdn_out_{tag}")
    return hnew, (h, hn, qkv_pre, z, ba, qkv, w, qd, kd, at, el, tinv, o, vn, st, cat)


def gdn_bwd(dres, saved, g_row, w_in, conv_w, alog, dtb, norm_g, w_out, tag):
    h, hn, qkv_pre, z, ba, qkv, w, qd, kd, at, el, tinv, o, vn, st, cat = saved
    d_o, d_z, d_ng = gdn_out_bwd(dres, o, z, norm_g, w_out, f"gdn_out_bwd_{tag}")
    d_wout = mm_tn(cat, dres, f"gdn_wout_grad_{tag}")
    du, dw, dqd, dkd, dat, dl = gdn_rec_bwd(d_o, w, qd, kd, at, el, vn, st, f"gdn_rec_bwd_{tag}")
    dq, dk, dv, dba, dal, ddt = gdn_local_bwd(qkv, ba, alog, dtb, tinv, du, dw, dqd, dkd, dat, dl, f"gdn_local_bwd_{tag}")
    dpre, dcw = [], []
    for part, d in enumerate((dq, dk, dv)):
        dx, dwc = gdn_prep_bwd(d, qkv_pre, conv_w, part, part < 2, f"gdn_prep_bwd_{tag}_{part}")
        dpre.append(dx)
        dcw.append(dwc)
    parts = dpre + [d_z, dba]
    dres_new, d_norm = proj_bwd_data(h, g_row, w_in, parts, GDN_SPLITS, dres, f"gdn_proj_bwd_{tag}")
    d_win = jnp.concatenate([mm_tn(hn, p, f"gdn_win_grad_{tag}_{k}") for k, p in enumerate(parts)], axis=1)
    return dres_new, dict(norm=d_norm, w_in=d_win, conv_w=jnp.concatenate(dcw, axis=1), A_log=dal, dt_bias=ddt,
                          norm_g=d_ng, w_out=d_wout)


MESH = pl.DeviceIdType.MESH
ANY = pl.BlockSpec(memory_space=pl.ANY)


def _coords():
    return lax.axis_index("x"), lax.axis_index("y"), lax.axis_index("c")


def _slot(p):
    return 4 * p[0] + 2 * p[1] + p[2]


def all_gather(shards, name):
    k_n = len(shards)

    def body(*refs):
        srcs, dsts = refs[:k_n], refs[k_n:2 * k_n]
        send_sems, recv_sems, local_sems = refs[2 * k_n:]
        x, y, c = _coords()
        me, sibling = (x, y, c), (x, y, 1 - c)
        chips = [(1 - x, y), (x, 1 - y), (1 - x, 1 - y)]

        def copy(k, s, block, to, from_src=False):
            rows = dsts[k].at[_slot(block)]
            return pltpu.make_async_remote_copy(
                src_ref=srcs[k] if from_src else rows, dst_ref=rows,
                send_sem=send_sems.at[k, s], recv_sem=recv_sems.at[k, s], device_id=to, device_id_type=MESH)

        local = [pltpu.make_async_copy(srcs[k], dsts[k].at[_slot(me)], local_sems.at[k]) for k in range(k_n)]
        for cp in local:
            cp.start()
        first = []
        for k in range(k_n):
            first.append(copy(k, 0, me, sibling, True))
            first += [copy(k, 1 + j, me, (*chip, c), True) for j, chip in enumerate(chips)]
        for cp in first:
            cp.start()
        passed = []
        for j, chip in enumerate(chips):
            for k in range(k_n):
                copy(k, 1 + j, (*chip, c), me).wait_recv()
                fw = copy(k, 4 + j, (*chip, c), sibling)
                fw.start()
                passed.append(fw)
        for k in range(k_n):
            copy(k, 0, sibling, me).wait_recv()
            for j, chip in enumerate(chips):
                copy(k, 4 + j, (*chip, 1 - c), me).wait_recv()
        for cp in first + passed:
            cp.wait_send()
        for cp in local:
            cp.wait()

    return pl.pallas_call(
        body,
        in_specs=[ANY] * k_n,
        out_specs=[ANY] * k_n,
        out_shape=[jax.ShapeDtypeStruct((N_DEV,) + s.shape, s.dtype) for s in shards],
        scratch_shapes=[pltpu.SemaphoreType.DMA((k_n, 7)), pltpu.SemaphoreType.DMA((k_n, 7)),
                        pltpu.SemaphoreType.DMA((k_n,))],
        name=name,
    )(*shards)


def slab_exchange(items, dst_shapes, name):
    n_it = len(items)
    n_dst = len(dst_shapes)

    def body(*refs):
        srcs, dsts = refs[:n_it], refs[n_it:n_it + n_dst]
        send_sems, recv_sems, local_sems = refs[n_it + n_dst:]
        x, y, c = _coords()
        me = _slot((x, y, c))
        peers = []
        for j in range(1, N_DEV):
            jx, jy, jc = (j >> 2) & 1, (j >> 1) & 1, j & 1
            peers.append((x if jx == 0 else 1 - x, y if jy == 0 else 1 - y, c if jc == 0 else 1 - c))

        def dview(t, s):
            _, k, layer = items[t]
            return dsts[k].at[s] if layer is None else dsts[k].at[s, layer]

        local = [pltpu.make_async_copy(srcs[t].at[me], dview(t, me), local_sems.at[t]) for t in range(n_it)]
        for cp in local:
            cp.start()
        sends = []
        for j, peer in enumerate(peers):
            for t in range(n_it):
                cp = pltpu.make_async_remote_copy(
                    src_ref=srcs[t].at[_slot(peer)], dst_ref=dview(t, me),
                    send_sem=send_sems.at[t, j], recv_sem=recv_sems.at[t, j], device_id=peer, device_id_type=MESH)
                cp.start()
                sends.append(cp)
        for j, peer in enumerate(peers):
            for t in range(n_it):
                pltpu.make_async_remote_copy(
                    src_ref=srcs[t].at[_slot(peer)], dst_ref=dview(t, _slot(peer)),
                    send_sem=send_sems.at[t, j], recv_sem=recv_sems.at[t, j], device_id=peer,
                    device_id_type=MESH).wait_recv()
        for cp in sends:
            cp.wait_send()
        for cp in local:
            cp.wait()

    return pl.pallas_call(
        body,
        in_specs=[ANY] * n_it,
        out_specs=[ANY] * n_dst,
        out_shape=list(dst_shapes),
        scratch_shapes=[pltpu.SemaphoreType.DMA((n_it, 7)), pltpu.SemaphoreType.DMA((n_it, 7)),
                        pltpu.SemaphoreType.DMA((n_it,))],
        name=name,
    )(*[it[0] for it in items])


def sum_slabs(parts, name):
    n, rows, cols = parts.shape

    def body(p_ref, o_ref):
        g = p_ref[0]
        for s in range(1, n):
            g = g + p_ref[s]
        o_ref[...] = g

    return pl.pallas_call(body, out_shape=jax.ShapeDtypeStruct((rows, cols), F32), name=name)(parts)


def _row_tile(rows, cols):
    if rows * cols * 4 <= (1 << 20) or rows % 8:
        return rows
    tr = rows
    while tr % 2 == 0 and (tr // 2) % 8 == 0 and tr * cols * 4 > (1 << 20):
        tr //= 2
    return tr


def adamw(parts, w, m, v, name):
    p_n = parts.shape[0]
    rows, cols = w.shape
    tr = _row_tile(rows, cols)

    def body(p_ref, w_ref, m_ref, v_ref, g_ref, d_ref, nm_ref, nv_ref):
        g = p_ref[0].astype(F32)
        for s in range(1, p_n):
            g = g + p_ref[s].astype(F32)
        m_new = ADAM_B1 * m_ref[...] + (1.0 - ADAM_B1) * g
        v_new = ADAM_B2 * v_ref[...] + (1.0 - ADAM_B2) * (g * g)
        m_hat = m_new / (1.0 - ADAM_B1 ** ADAM_STEP)
        v_hat = v_new / (1.0 - ADAM_B2 ** ADAM_STEP)
        g_ref[...] = g
        d_ref[...] = -ADAM_LR * (m_hat / (jnp.sqrt(v_hat) + ADAM_EPS) + ADAM_WD * w_ref[...])
        nm_ref[...] = m_new
        nv_ref[...] = v_new

    blk = pl.BlockSpec((tr, cols), lambda i: (i, 0))
    return pl.pallas_call(
        body,
        grid=(rows // tr,),
        in_specs=[pl.BlockSpec((p_n, tr, cols), lambda i: (0, i, 0)), blk, blk, blk],
        out_specs=[blk] * 4,
        out_shape=[jax.ShapeDtypeStruct((rows, cols), F32)] * 4,
        compiler_params=_cp("parallel"),
        name=name,
    )(parts, w, m, v)


def _adamw_nd(parts, w, m, v, name):
    shp = w.shape
    cols = shp[-1]
    rows = math.prod(shp[:-1])
    outs = adamw(parts.reshape(parts.shape[0], rows, cols), w.reshape(rows, cols), m.reshape(rows, cols),
                 v.reshape(rows, cols), name)
    return [o.reshape(shp) for o in outs]


REPL = ["ffn1_norm", "mix_norm", "ffn2_norm", "hyb_dw_b", "hyb_ln_g", "hyb_ln_b", "gdn_A_log", "gdn_dt_bias",
        "gdn_norm_g", "final_norm"]
WEIGHTS = ["ffn1_norm", "ffn1_w_in", "ffn1_w_out", "mix_norm", "ffn2_norm", "ffn2_w_in", "ffn2_w_out", "hyb_w_in",
           "hyb_dw_w", "hyb_dw_b", "hyb_ln_g", "hyb_ln_b", "hyb_w_out", "gdn_w_in", "gdn_conv_w", "gdn_A_log",
           "gdn_dt_bias", "gdn_norm_g", "gdn_w_out", "final_norm"]


def _pack(arrs, rows):
    flat = jnp.concatenate([a.reshape(-1) for a in arrs])
    return jnp.pad(flat, (0, rows * 128 - flat.shape[0])).reshape(rows, 128)


def _cols_to_slabs(a):
    d, n = a.shape
    return a.reshape(d, N_DEV, n // N_DEV).transpose(1, 0, 2)


def _slabs_to_cols(a):
    return jnp.moveaxis(a, 0, -2).reshape(a.shape[1:-1] + (N_DEV * a.shape[-1],))


def kernel(x, positions, ffn1_norm, ffn1_w_in, ffn1_w_out, mix_norm, ffn2_norm, ffn2_w_in, ffn2_w_out, hyb_w_in, hyb_dw_w, hyb_dw_b, hyb_ln_g, hyb_ln_b, hyb_w_out, gdn_w_in, gdn_conv_w, gdn_A_log, gdn_dt_bias, gdn_norm_g, gdn_w_out, final_norm, loss_target, m_ffn1_norm, m_ffn1_w_in, m_ffn1_w_out, m_mix_norm, m_ffn2_norm, m_ffn2_w_in, m_ffn2_w_out, m_hyb_w_in, m_hyb_dw_w, m_hyb_dw_b, m_hyb_ln_g, m_hyb_ln_b, m_hyb_w_out, m_gdn_w_in, m_gdn_conv_w, m_gdn_A_log, m_gdn_dt_bias, m_gdn_norm_g, m_gdn_w_out, m_final_norm, v_ffn1_norm, v_ffn1_w_in, v_ffn1_w_out, v_mix_norm, v_ffn2_norm, v_ffn2_w_in, v_ffn2_w_out, v_hyb_w_in, v_hyb_dw_w, v_hyb_dw_b, v_hyb_ln_g, v_hyb_ln_b, v_hyb_w_out, v_gdn_w_in, v_gdn_conv_w, v_gdn_A_log, v_gdn_dt_bias, v_gdn_norm_g, v_gdn_w_out, v_final_norm):
    w = dict(ffn1_norm=ffn1_norm, ffn1_w_in=ffn1_w_in, ffn1_w_out=ffn1_w_out, mix_norm=mix_norm, ffn2_norm=ffn2_norm,
             ffn2_w_in=ffn2_w_in, ffn2_w_out=ffn2_w_out, hyb_w_in=hyb_w_in, hyb_dw_w=hyb_dw_w, hyb_dw_b=hyb_dw_b,
             hyb_ln_g=hyb_ln_g, hyb_ln_b=hyb_ln_b, hyb_w_out=hyb_w_out, gdn_w_in=gdn_w_in, gdn_conv_w=gdn_conv_w,
             gdn_A_log=gdn_A_log, gdn_dt_bias=gdn_dt_bias, gdn_norm_g=gdn_norm_g, gdn_w_out=gdn_w_out,
             final_norm=final_norm)
    mom = dict(ffn1_norm=m_ffn1_norm, ffn1_w_in=m_ffn1_w_in, ffn1_w_out=m_ffn1_w_out, mix_norm=m_mix_norm,
               ffn2_norm=m_ffn2_norm, ffn2_w_in=m_ffn2_w_in, ffn2_w_out=m_ffn2_w_out, hyb_w_in=m_hyb_w_in,
               hyb_dw_w=m_hyb_dw_w, hyb_dw_b=m_hyb_dw_b, hyb_ln_g=m_hyb_ln_g, hyb_ln_b=m_hyb_ln_b,
               hyb_w_out=m_hyb_w_out, gdn_w_in=m_gdn_w_in, gdn_conv_w=m_gdn_conv_w, gdn_A_log=m_gdn_A_log,
               gdn_dt_bias=m_gdn_dt_bias, gdn_norm_g=m_gdn_norm_g, gdn_w_out=m_gdn_w_out, final_norm=m_final_norm)
    var = dict(ffn1_norm=v_ffn1_norm, ffn1_w_in=v_ffn1_w_in, ffn1_w_out=v_ffn1_w_out, mix_norm=v_mix_norm,
               ffn2_norm=v_ffn2_norm, ffn2_w_in=v_ffn2_w_in, ffn2_w_out=v_ffn2_w_out, hyb_w_in=v_hyb_w_in,
               hyb_dw_w=v_hyb_dw_w, hyb_dw_b=v_hyb_dw_b, hyb_ln_g=v_hyb_ln_g, hyb_ln_b=v_hyb_ln_b,
               hyb_w_out=v_hyb_w_out, gdn_w_in=v_gdn_w_in, gdn_conv_w=v_gdn_conv_w, gdn_A_log=v_gdn_A_log,
               gdn_dt_bias=v_gdn_dt_bias, gdn_norm_g=v_gdn_norm_g, gdn_w_out=v_gdn_w_out, final_norm=v_final_norm)
    xi, yi, ci = _coords()
    me = 4 * xi + 2 * yi + ci

    big = ["ffn1_w_in", "ffn1_w_out", "ffn2_w_in", "ffn2_w_out", "hyb_w_in", "hyb_w_out", "gdn_w_in", "gdn_w_out"]
    gathered = all_gather([w[n].astype(BF16) for n in big] + [w["hyb_dw_w"], w["gdn_conv_w"]], "weights_all_gather")
    gw = dict(zip(big + ["hyb_dw_w", "gdn_conv_w"], gathered))
    ffn_in = {t: gw[f"ffn{t}_w_in"].reshape(2, FFN_TILES, DEPTH, D, FFN_SHARD) for t in (1, 2)}
    ffn_out = {t: gw[f"ffn{t}_w_out"] for t in (1, 2)}
    hyb_in = _slabs_to_cols(gw["hyb_w_in"])
    gdn_in = _slabs_to_cols(gw["gdn_w_in"])
    hyb_out = jnp.moveaxis(gw["hyb_w_out"], 0, 1).reshape(2, D, D)
    gdn_out = jnp.moveaxis(gw["gdn_w_out"], 0, 1).reshape(2, D, D)
    dw_full = jnp.pad(_slabs_to_cols(gw["hyb_dw_w"]), ((0, 0), (0, 1), (0, 0)))
    cw_full = _slabs_to_cols(gw["gdn_conv_w"])
    row = lambda a: a.reshape(1, -1)

    rope = make_rope(positions)
    h = x[0]
    saved = []
    for l in range(DEPTH):
        i = l // 2
        rec = {"h1": h}
        h, rec["hn1"], rec["a1"], rec["b1"] = ffn_fwd(h, row(ffn1_norm[l]), ffn_in[1], ffn_out[1], l, "1")
        if l % 2 == 0:
            h, rec["mix"] = hybrid_fwd(h, row(mix_norm[l]), hyb_in[i], dw_full[i], row(hyb_dw_b[i]), row(hyb_ln_g[i]),
                                       row(hyb_ln_b[i]), hyb_out[i], rope, str(i))
        else:
            h, rec["mix"] = gdn_fwd(h, row(mix_norm[l]), gdn_in[i], cw_full[i], row(gdn_A_log[i]), row(gdn_dt_bias[i]),
                                    row(gdn_norm_g[i]), gdn_out[i], str(i))
        rec["h2"] = h
        h, rec["hn2"], rec["a2"], rec["b2"] = ffn_fwd(h, row(ffn2_norm[l]), ffn_in[2], ffn_out[2], l, "2")
        saved.append(rec)
    dres, d_final, loss_acc = final_loss(h, row(final_norm), loss_target[0])
    loss = lax.psum(loss_acc[0, 0], ("x", "y", "c"))

    items = []
    dst_names = big
    gsmall = {n: [None] * (DEPTH if n in ("ffn1_norm", "mix_norm", "ffn2_norm") else 2) for n in REPL[:-1]}
    gsmall["hyb_dw_w"] = [None, None]
    gsmall["gdn_conv_w"] = [None, None]
    for l in reversed(range(DEPTH)):
        i = l // 2
        rec = saved[l]
        dhn, dwin, dwout = ffn_bwd(rec["hn2"], rec["a2"], rec["b2"], dres, ffn_in[2], ffn_out[2], l, "2")
        items.append((dwin.reshape(N_DEV, D, FFN_SHARD), dst_names.index("ffn2_w_in"), l))
        items.append((dwout.reshape(N_DEV, FFN_SHARD // 2, D), dst_names.index("ffn2_w_out"), l))
        dres, dg = norm_bwd(rec["h2"], row(ffn2_norm[l]), dhn, dres, f"ffn2_norm_bwd_{l}")
        gsmall["ffn2_norm"][l] = dg
        if l % 2 == 0:
            dres, gr = hybrid_bwd(dres, rec["mix"], row(mix_norm[l]), hyb_in[i], dw_full[i], row(hyb_ln_g[i]),
                                  row(hyb_ln_b[i]), hyb_out[i], rope, str(i))
            items.append((_cols_to_slabs(gr["w_in"]).astype(BF16), dst_names.index("hyb_w_in"), i))
            items.append((gr["w_out"].reshape(N_DEV, D // N_DEV, D).astype(BF16), dst_names.index("hyb_w_out"), i))
            for n in ("dw_w", "dw_b", "ln_g", "ln_b"):
                gsmall["hyb_" + n][i] = gr[n]
        else:
            dres, gr = gdn_bwd(dres, rec["mix"], row(mix_norm[l]), gdn_in[i], cw_full[i], row(gdn_A_log[i]),
                               row(gdn_dt_bias[i]), row(gdn_norm_g[i]), gdn_out[i], str(i))
            items.append((_cols_to_slabs(gr["w_in"]).astype(BF16), dst_names.index("gdn_w_in"), i))
            items.append((gr["w_out"].reshape(N_DEV, D // N_DEV, D).astype(BF16), dst_names.index("gdn_w_out"), i))
            for n in ("conv_w", "A_log", "dt_bias", "norm_g"):
                gsmall["gdn_" + n][i] = gr[n]
        gsmall["mix_norm"][l] = gr["norm"]
        dhn, dwin, dwout = ffn_bwd(rec["hn1"], rec["a1"], rec["b1"], dres, ffn_in[1], ffn_out[1], l, "1")
        items.append((dwin.reshape(N_DEV, D, FFN_SHARD), dst_names.index("ffn1_w_in"), l))
        items.append((dwout.reshape(N_DEV, FFN_SHARD // 2, D), dst_names.index("ffn1_w_out"), l))
        dres, dg = norm_bwd(rec["h1"], row(ffn1_norm[l]), dhn, dres, f"ffn1_norm_bwd_{l}")
        gsmall["ffn1_norm"][l] = dg
    grad_x = dres[None]

    n_repl_rows = 136
    small_rows = 576
    repl_flat = jnp.concatenate([jnp.concatenate([a.reshape(-1) for a in gsmall[n]]) for n in REPL[:-1]]
                                + [d_final.reshape(-1)])
    repl_pack = jnp.pad(repl_flat, (0, n_repl_rows * 128 - repl_flat.shape[0]))
    small_pack = jnp.concatenate([repl_pack] + [a.reshape(-1) for a in gsmall["hyb_dw_w"]]
                                 + [a.reshape(-1) for a in gsmall["gdn_conv_w"]]).reshape(small_rows, 128)
    small_all, = all_gather([small_pack], "small_grads_all_gather")
    g_small = sum_slabs(small_all, "small_grads_sum")
    received = slab_exchange(items, [jax.ShapeDtypeStruct((N_DEV,) + w[n].shape, BF16) for n in big], "grad_exchange")
    recv = dict(zip(big, received))

    out = {}
    for n in big:
        out[n] = _adamw_nd(recv[n], w[n], mom[n], var[n], f"adamw_{n}")
    pk = lambda d: _pack([d[n] for n in REPL], n_repl_rows)
    res = adamw(g_small[:n_repl_rows][None], pk(w), pk(mom), pk(var), "adamw_replicated")
    off = 0
    for n in REPL:
        sz = w[n].size
        out[n] = [r.reshape(-1)[off:off + sz].reshape(w[n].shape) for r in res]
        off += sz
    g_dw = g_small[n_repl_rows:n_repl_rows + 248].reshape(2, CONV_K, CONV_C)
    g_dw = lax.dynamic_slice_in_dim(g_dw, me * (CONV_C // N_DEV), CONV_C // N_DEV, axis=2)
    out["hyb_dw_w"] = _adamw_nd(g_dw[None], w["hyb_dw_w"], mom["hyb_dw_w"], var["hyb_dw_w"], "adamw_hyb_dw_w")
    g_cw = g_small[n_repl_rows + 248:].reshape(2, G_CONV, G_QKV)
    g_cw = lax.dynamic_slice_in_dim(g_cw, me * (G_QKV // N_DEV), G_QKV // N_DEV, axis=2)
    out["gdn_conv_w"] = _adamw_nd(g_cw[None], w["gdn_conv_w"], mom["gdn_conv_w"], var["gdn_conv_w"], "adamw_gdn_conv_w")

    return (loss, grad_x, *[out[n][0] for n in WEIGHTS], *[out[n][1] for n in WEIGHTS],
            *[out[n][2] for n in WEIGHTS], *[out[n][3] for n in WEIGHTS])
```

```python
import functools
import math

import jax
import jax.numpy as jnp
import numpy as np
from jax import lax
from jax.experimental import pallas as pl
from jax.experimental.pallas import tpu as pltpu

F32 = jnp.float32
BF16 = jnp.bfloat16

N_DEV = 8
T = 4096
D = 1024
DEPTH = 4
FFN = 2816
FFN_SHARD = 2 * FFN // N_DEV
FFN_TILES = FFN // FFN_SHARD
EPS = 1e-6

A_HEADS = 8
A_HD = 64
A_W = 512
CONV_C = 512
CONV_K = 31
HYB_IN = 2560
ROPE_THETA = 500000.0
ROT = 16
DILATIONS = (1, 4, 16)
BLK = 128
KPAD = 2048

G_HEADS = 8
G_DK = 128
G_QKV = 3072
G_IN = 4112
G_CONV = 4
CH = 64
GRP = 512
CPG = GRP // CH
N_GRP = T // GRP

ADAM_LR = 0.001
ADAM_B1 = 0.9
ADAM_B2 = 0.999
ADAM_EPS = 1e-08
ADAM_WD = 0.01
ADAM_STEP = 10

VMEM_LIMIT = 56 * 1024 * 1024

HI = lax.Precision.HIGHEST


def _cp(*sem):
    return pltpu.CompilerParams(dimension_semantics=sem, vmem_limit_bytes=VMEM_LIMIT)


def _dot(a, b):
    return jnp.dot(a, b, preferred_element_type=F32)


def _dot_nt(a, b):
    return lax.dot_general(a, b, (((1,), (1,)), ((), ())), preferred_element_type=F32)


def _dot_tn(a, b):
    return lax.dot_general(a, b, (((0,), (0,)), ((), ())), preferred_element_type=F32)


def _sigmoid(x):
    return 1.0 / (1.0 + jnp.exp(-x))


def _dsilu(x, sig):
    return sig * (1.0 + x * (1.0 - sig))


def _rms(x, g):
    rstd = lax.rsqrt(jnp.mean(x * x, axis=-1, keepdims=True) + EPS)
    return x * rstd * g


FFN_TT = 512


def ffn_fwd(h, g_row, w_in, w_out, layer, tag=""):
    def body(h_ref, g_ref, win_ref, wout_ref, hnew_ref, hn_ref, a_ref, b_ref, acc_ref, hns_ref):
        j = pl.program_id(1)

        @pl.when(j == 0)
        def _():
            hn = _rms(h_ref[...], g_ref[...]).astype(BF16)
            hns_ref[...] = hn
            hn_ref[...] = hn
            acc_ref[...] = jnp.zeros_like(acc_ref)

        hn = hns_ref[...]
        a = _dot(hn, win_ref[0])
        b = _dot(hn, win_ref[1])
        act = a * _sigmoid(a) * b
        acc_ref[...] += _dot(act.astype(BF16), wout_ref[...].reshape(FFN_SHARD, D))
        a_ref[...] = a.astype(BF16)
        b_ref[...] = b.astype(BF16)

        @pl.when(j == FFN_TILES - 1)
        def _():
            hnew_ref[...] = h_ref[...] + 0.5 * acc_ref[...]

    tt = FFN_TT
    return pl.pallas_call(
        body,
        grid=(T // tt, FFN_TILES),
        in_specs=[
            pl.BlockSpec((tt, D), lambda i, j: (i, 0)),
            pl.BlockSpec((1, D), lambda i, j: (0, 0)),
            pl.BlockSpec((2, None, D, FFN_SHARD), lambda i, j: (0, j, 0, 0)),
            pl.BlockSpec((2, FFN_SHARD // 2, D), lambda i, j: (j, 0, 0)),
        ],
        out_specs=[
            pl.BlockSpec((tt, D), lambda i, j: (i, 0)),
            pl.BlockSpec((tt, D), lambda i, j: (i, 0)),
            pl.BlockSpec((None, tt, FFN_SHARD), lambda i, j: (j, i, 0)),
            pl.BlockSpec((None, tt, FFN_SHARD), lambda i, j: (j, i, 0)),
        ],
        out_shape=[
            jax.ShapeDtypeStruct((T, D), F32),
            jax.ShapeDtypeStruct((T, D), BF16),
            jax.ShapeDtypeStruct((FFN_TILES, T, FFN_SHARD), BF16),
            jax.ShapeDtypeStruct((FFN_TILES, T, FFN_SHARD), BF16),
        ],
        scratch_shapes=[pltpu.VMEM((tt, D), F32), pltpu.VMEM((tt, D), BF16)],
        compiler_params=_cp("parallel", "arbitrary"),
        name=f"ffn{tag}_fwd_{layer}",
    )(h, g_row, w_in, w_out)


def ffn_bwd(hn, a, b, dres, w_in, w_out, layer, tag=""):
    tt = FFN_TT
    nt = T // tt

    def body(hn_ref, a_ref, b_ref, dres_ref, win_ref, wout_ref, dhn_ref, dwin_ref, dwout_ref, gin_ref, gout_ref):
        i = pl.program_id(1)
        do = (0.5 * dres_ref[...]).astype(BF16)
        wo = wout_ref[...].reshape(FFN_SHARD, D)
        dact = _dot_nt(do, wo)
        a = a_ref[...].astype(F32)
        b = b_ref[...].astype(F32)
        sig = _sigmoid(a)
        s = a * sig
        act = (s * b).astype(BF16)
        db = (dact * s).astype(BF16)
        da = (dact * b * _dsilu(a, sig)).astype(BF16)
        hn = hn_ref[...]
        gwo = _dot_tn(act, do)
        gwg = _dot_tn(hn, da)
        gwu = _dot_tn(hn, db)

        @pl.when(i == 0)
        def _():
            gout_ref[...] = gwo
            gin_ref[0] = gwg
            gin_ref[1] = gwu

        @pl.when(i > 0)
        def _():
            gout_ref[...] += gwo
            gin_ref[0] += gwg
            gin_ref[1] += gwu

        dhn_ref[...] = _dot_nt(da, win_ref[0]) + _dot_nt(db, win_ref[1])

        @pl.when(i == nt - 1)
        def _():
            dwin_ref[...] = gin_ref[...].astype(BF16)
            dwout_ref[...] = gout_ref[...].astype(BF16)

    return pl.pallas_call(
        body,
        grid=(FFN_TILES, nt),
        in_specs=[
            pl.BlockSpec((tt, D), lambda j, i: (i, 0)),
            pl.BlockSpec((None, tt, FFN_SHARD), lambda j, i: (j, i, 0)),
            pl.BlockSpec((None, tt, FFN_SHARD), lambda j, i: (j, i, 0)),
            pl.BlockSpec((tt, D), lambda j, i: (i, 0)),
            pl.BlockSpec((2, None, D, FFN_SHARD), lambda j, i: (0, j, 0, 0)),
            pl.BlockSpec((2, FFN_SHARD // 2, D), lambda j, i: (j, 0, 0)),
        ],
        out_specs=[
            pl.BlockSpec((None, tt, D), lambda j, i: (j, i, 0)),
            pl.BlockSpec((2, None, D, FFN_SHARD), lambda j, i: (0, j, 0, 0)),
            pl.BlockSpec((None, FFN_SHARD, D), lambda j, i: (j, 0, 0)),
        ],
        out_shape=[
            jax.ShapeDtypeStruct((FFN_TILES, T, D), F32),
            jax.ShapeDtypeStruct((2, FFN_TILES, D, FFN_SHARD), BF16),
            jax.ShapeDtypeStruct((FFN_TILES, FFN_SHARD, D), BF16),
        ],
        scratch_shapes=[pltpu.VMEM((2, D, FFN_SHARD), F32), pltpu.VMEM((FFN_SHARD, D), F32)],
        compiler_params=_cp("parallel", "arbitrary"),
        name=f"ffn{tag}_bwd_{layer}",
    )(hn, a, b, dres, w_in, w_out)


def _rms_bwd(x, g, dy):
    rstd = lax.rsqrt(jnp.mean(x * x, axis=-1, keepdims=True) + EPS)
    xh = x * rstd
    u = dy * g
    dx = rstd * (u - xh * jnp.mean(u * xh, axis=-1, keepdims=True))
    return dx, jnp.sum(dy * xh, axis=0, keepdims=True)


def norm_bwd(x, g_row, dy_parts, dres, name):
    p = dy_parts.shape[0]
    tt = 512

    def body(x_ref, g_ref, dy_ref, dres_ref, out_ref, dg_ref):
        i = pl.program_id(0)
        dy = dy_ref[0]
        for q in range(1, p):
            dy = dy + dy_ref[q]
        dx, dg = _rms_bwd(x_ref[...], g_ref[...], dy)
        out_ref[...] = dres_ref[...] + dx

        @pl.when(i == 0)
        def _():
            dg_ref[...] = dg

        @pl.when(i > 0)
        def _():
            dg_ref[...] += dg

    return pl.pallas_call(
        body,
        grid=(T // tt,),
        in_specs=[
            pl.BlockSpec((tt, D), lambda i: (i, 0)),
            pl.BlockSpec((1, D), lambda i: (0, 0)),
            pl.BlockSpec((p, tt, D), lambda i: (0, i, 0)),
            pl.BlockSpec((tt, D), lambda i: (i, 0)),
        ],
        out_specs=[pl.BlockSpec((tt, D), lambda i: (i, 0)), pl.BlockSpec((1, D), lambda i: (0, 0))],
        out_shape=[jax.ShapeDtypeStruct((T, D), F32), jax.ShapeDtypeStruct((1, D), F32)],
        compiler_params=_cp("arbitrary"),
        name=name,
    )(x, g_row, dy_parts, dres)


def final_loss(h, g_row, target):
    tt = 512

    def body(h_ref, g_ref, t_ref, dres_ref, dg_ref, loss_ref):
        i = pl.program_id(0)
        x = h_ref[...]
        g = g_ref[...]
        err = _rms(x, g) - t_ref[...]
        part = 0.5 * jnp.sum(jnp.mean(err * err, axis=-1, keepdims=True), axis=0, keepdims=True)
        dx, dg = _rms_bwd(x, g, err * (1.0 / D))
        dres_ref[...] = dx
        part = jnp.broadcast_to(part, loss_ref.shape)

        @pl.when(i == 0)
        def _():
            dg_ref[...] = dg
            loss_ref[...] = part

        @pl.when(i > 0)
        def _():
            dg_ref[...] += dg
            loss_ref[...] += part

    return pl.pallas_call(
        body,
        grid=(T // tt,),
        in_specs=[
            pl.BlockSpec((tt, D), lambda i: (i, 0)),
            pl.BlockSpec((1, D), lambda i: (0, 0)),
            pl.BlockSpec((tt, D), lambda i: (i, 0)),
        ],
        out_specs=[
            pl.BlockSpec((tt, D), lambda i: (i, 0)),
            pl.BlockSpec((1, D), lambda i: (0, 0)),
            pl.BlockSpec((8, 128), lambda i: (0, 0)),
        ],
        out_shape=[
            jax.ShapeDtypeStruct((T, D), F32),
            jax.ShapeDtypeStruct((1, D), F32),
            jax.ShapeDtypeStruct((8, 128), F32),
        ],
        compiler_params=_cp("arbitrary"),
        name="final_loss",
    )(h, g_row, target)


PROJ_TT = 256


def rope_tables(pos_col, invf_row):
    tt = 512

    def body(p_ref, f_ref, c_ref, sm_ref, sp_ref):
        ang = p_ref[...].astype(F32) * f_ref[...]
        lane = lax.broadcasted_iota(jnp.int32, ang.shape, 1) % A_HD
        cs = jnp.cos(ang)
        sn = jnp.sin(ang)
        c_ref[...] = jnp.where(lane < ROT, cs, 1.0)
        sm_ref[...] = jnp.where(lane < ROT // 2, -sn, 0.0)
        sp_ref[...] = jnp.where((lane >= ROT // 2) & (lane < ROT), sn, 0.0)

    spec = pl.BlockSpec((tt, 128), lambda i: (i, 0))
    return pl.pallas_call(
        body,
        grid=(T // tt,),
        in_specs=[pl.BlockSpec((tt, 1), lambda i: (i, 0)), pl.BlockSpec((1, 128), lambda i: (0, 0))],
        out_specs=[spec, spec, spec],
        out_shape=[jax.ShapeDtypeStruct((T, 128), F32)] * 3,
        compiler_params=_cp("parallel"),
        name="rope_tables",
    )(pos_col, invf_row)


def make_rope(positions):
    inv_freq = jnp.power(jnp.float32(ROPE_THETA), -jnp.arange(0, ROT, 2, dtype=F32) / ROT)
    per_head = jnp.concatenate([inv_freq, inv_freq, jnp.zeros((A_HD - ROT,), F32)])
    invf_row = jnp.tile(per_head, 2)[None, :]
    return tuple(rope_tables(positions.reshape(T, 1), invf_row))


def _rope(x, c, sm, sp):
    return x * c + pltpu.roll(x, 128 - ROT // 2, 1) * sm + pltpu.roll(x, ROT // 2, 1) * sp


def _rope_t(dy, c, sm, sp):
    return dy * c + pltpu.roll(dy * sm, ROT // 2, 1) + pltpu.roll(dy * sp, 128 - ROT // 2, 1)


def proj_fwd(h, g_row, w, splits, name, rope=None):
    tt = PROJ_TT
    n = w.shape[1]
    n_rope = 0 if rope is None else 3

    def body(h_ref, g_ref, w_ref, *rest):
        tabs = rest[:n_rope]
        hn_ref = rest[n_rope]
        outs = rest[n_rope + 1:]
        hn = _rms(h_ref[...], g_ref[...]).astype(BF16)
        hn_ref[...] = hn
        for k, ((st, wd), o_ref) in enumerate(zip(splits, outs)):
            if rope is not None and k == 0:
                c, sm, sp = (t[...] for t in tabs)
                for gi in range(wd // 128):
                    r = _dot(hn, w_ref[:, st + 128 * gi:st + 128 * (gi + 1)])
                    if gi < 2 * A_W // 128:
                        r = _rope(r, c, sm, sp)
                    o_ref[:, 128 * gi:128 * (gi + 1)] = r
            else:
                o_ref[...] = _dot(hn, w_ref[:, st:st + wd])

    tab_specs = [pl.BlockSpec((tt, 128), lambda i: (i, 0))] * n_rope
    return pl.pallas_call(
        body,
        grid=(T // tt,),
        in_specs=[
            pl.BlockSpec((tt, D), lambda i: (i, 0)),
            pl.BlockSpec((1, D), lambda i: (0, 0)),
            pl.BlockSpec((D, n), lambda i: (0, 0)),
        ] + tab_specs,
        out_specs=[pl.BlockSpec((tt, D), lambda i: (i, 0))]
        + [pl.BlockSpec((tt, wd), lambda i: (i, 0)) for _, wd in splits],
        out_shape=[jax.ShapeDtypeStruct((T, D), BF16)]
        + [jax.ShapeDtypeStruct((T, wd), F32) for _, wd in splits],
        compiler_params=_cp("parallel"),
        name=name,
    )(h, g_row, w, *(rope or ()))


def proj_bwd_data(x, g_row, w, dparts, splits, dres, name, rope=None, n_rot=0):
    tt = PROJ_TT
    n = w.shape[1]
    n_rope = 0 if rope is None else 3
    k_parts = len(dparts)

    def body(x_ref, g_ref, w_ref, dres_ref, *rest):
        d_refs = rest[:k_parts]
        tabs = rest[k_parts:k_parts + n_rope]
        out_ref, dg_ref = rest[k_parts + n_rope:k_parts + n_rope + 2]
        unrot_refs = rest[k_parts + n_rope + 2:]
        i = pl.program_id(0)
        dhn = jnp.zeros((tt, D), F32)
        for k, ((st, wd), d_ref) in enumerate(zip(splits, d_refs)):
            if k < n_rot:
                c, sm, sp = (t[...] for t in tabs)
                for gi in range(wd // 128):
                    d = _rope_t(d_ref[:, 128 * gi:128 * (gi + 1)], c, sm, sp)
                    unrot_refs[k][:, 128 * gi:128 * (gi + 1)] = d
                    dhn = dhn + _dot_nt(d.astype(BF16), w_ref[:, st + 128 * gi:st + 128 * (gi + 1)])
            else:
                dhn = dhn + _dot_nt(d_ref[...].astype(BF16), w_ref[:, st:st + wd])
        dx, dg = _rms_bwd(x_ref[...], g_ref[...], dhn)
        out_ref[...] = dres_ref[...] + dx

        @pl.when(i == 0)
        def _():
            dg_ref[...] = dg

        @pl.when(i > 0)
        def _():
            dg_ref[...] += dg

    tab_specs = [pl.BlockSpec((tt, 128), lambda i: (i, 0))] * n_rope
    out_specs = [pl.BlockSpec((tt, D), lambda i: (i, 0)), pl.BlockSpec((1, D), lambda i: (0, 0))]
    out_shape = [jax.ShapeDtypeStruct((T, D), F32), jax.ShapeDtypeStruct((1, D), F32)]
    for k in range(n_rot):
        out_specs.append(pl.BlockSpec((tt, splits[k][1]), lambda i: (i, 0)))
        out_shape.append(jax.ShapeDtypeStruct((T, splits[k][1]), F32))
    return pl.pallas_call(
        body,
        grid=(T // tt,),
        in_specs=[
            pl.BlockSpec((tt, D), lambda i: (i, 0)),
            pl.BlockSpec((1, D), lambda i: (0, 0)),
            pl.BlockSpec((D, n), lambda i: (0, 0)),
            pl.BlockSpec((tt, D), lambda i: (i, 0)),
        ] + [pl.BlockSpec((tt, wd), lambda i: (i, 0)) for _, wd in splits] + tab_specs,
        out_specs=out_specs,
        out_shape=out_shape,
        compiler_params=_cp("arbitrary"),
        name=name,
    )(x, g_row, w, dres, *dparts, *(rope or ()))


def mm_tn(x, d, name):
    k = x.shape[1]
    n = d.shape[1]
    wn = n if n <= 512 else 512
    tt = 512

    def body(x_ref, d_ref, o_ref):
        i = pl.program_id(1)
        r = _dot_tn(x_ref[...], d_ref[...].astype(BF16))

        @pl.when(i == 0)
        def _():
            o_ref[...] = r

        @pl.when(i > 0)
        def _():
            o_ref[...] += r

    return pl.pallas_call(
        body,
        grid=(n // wn, T // tt),
        in_specs=[pl.BlockSpec((tt, k), lambda j, i: (i, 0)), pl.BlockSpec((tt, wn), lambda j, i: (i, j))],
        out_specs=pl.BlockSpec((k, wn), lambda j, i: (0, j)),
        out_shape=jax.ShapeDtypeStruct((k, n), F32),
        compiler_params=_cp("parallel", "arbitrary"),
        name=name,
    )(x, d)


CONV_RC = 128
CONV_PAD = 32


def hyb_conv_fwd(u, dw_w, dw_b, name):
    def body(ua_ref, ug_ref, w_ref, b_ref, o_ref, xpad):
        xpad[0:CONV_PAD, :] = jnp.zeros((CONV_PAD, 128), F32)
        xpad[CONV_PAD:, :] = ua_ref[...] * _sigmoid(ug_ref[...])
        for r in range(T // CONV_RC):
            acc = jnp.broadcast_to(b_ref[...], (CONV_RC, 128))
            for j in range(CONV_K):
                acc = acc + w_ref[pl.ds(j, 1), :] * xpad[pl.ds(r * CONV_RC + CONV_PAD - (CONV_K - 1) + j, CONV_RC), :]
            o_ref[r * CONV_RC:(r + 1) * CONV_RC, :] = acc

    nb = CONV_C // 128
    return pl.pallas_call(
        body,
        grid=(nb,),
        in_specs=[
            pl.BlockSpec((T, 128), lambda c: (0, c)),
            pl.BlockSpec((T, 128), lambda c: (0, nb + c)),
            pl.BlockSpec((32, 128), lambda c: (0, c)),
            pl.BlockSpec((1, 128), lambda c: (0, c)),
        ],
        out_specs=pl.BlockSpec((T, 128), lambda c: (0, c)),
        out_shape=jax.ShapeDtypeStruct((T, CONV_C), F32),
        scratch_shapes=[pltpu.VMEM((T + CONV_PAD, 128), F32)],
        compiler_params=_cp("parallel"),
        name=name,
    )(u, u, dw_w, dw_b)


def hyb_conv_bwd(dc, u, dw_w, name):
    def body(dc_ref, ua_ref, ug_ref, w_ref, da_ref, dgate_ref, dw_ref, db_ref, xpad, dcpad, dwacc):
        ua = ua_ref[...]
        sig = _sigmoid(ug_ref[...])
        xpad[0:CONV_PAD, :] = jnp.zeros((CONV_PAD, 128), F32)
        xpad[CONV_PAD:, :] = ua * sig
        dcpad[0:T, :] = dc_ref[...]
        dcpad[T:, :] = jnp.zeros((CONV_PAD, 128), F32)
        dwacc[...] = jnp.zeros_like(dwacc)
        dbacc = jnp.zeros((8, 128), F32)
        for r in range(T // CONV_RC):
            r0 = r * CONV_RC
            dcr = dc_ref[r0:r0 + CONV_RC, :]
            dbacc = dbacc + dcr.reshape(CONV_RC // 8, 8, 128).sum(axis=0)
            dglu = jnp.zeros((CONV_RC, 128), F32)
            for j in range(CONV_K):
                dglu = dglu + w_ref[pl.ds(j, 1), :] * dcpad[pl.ds(r0 + (CONV_K - 1) - j, CONV_RC), :]
                prod = dcr * xpad[pl.ds(r0 + CONV_PAD - (CONV_K - 1) + j, CONV_RC), :]
                dwacc[8 * j:8 * j + 8, :] += prod.reshape(CONV_RC // 8, 8, 128).sum(axis=0)
            sg = sig[r0:r0 + CONV_RC, :]
            da_ref[r0:r0 + CONV_RC, :] = dglu * sg
            dgate_ref[r0:r0 + CONV_RC, :] = dglu * ua[r0:r0 + CONV_RC, :] * sg * (1.0 - sg)
        for j in range(CONV_K):
            dw_ref[pl.ds(j, 1), :] = jnp.sum(dwacc[8 * j:8 * j + 8, :], axis=0, keepdims=True)
        dw_ref[pl.ds(CONV_K, 1), :] = jnp.zeros((1, 128), F32)
        db_ref[...] = jnp.sum(dbacc, axis=0, keepdims=True)

    nb = CONV_C // 128
    col = pl.BlockSpec((T, 128), lambda c: (0, c))
    return pl.pallas_call(
        body,
        grid=(nb,),
        in_specs=[col, col, pl.BlockSpec((T, 128), lambda c: (0, nb + c)), pl.BlockSpec((32, 128), lambda c: (0, c))],
        out_specs=[col, col, pl.BlockSpec((32, 128), lambda c: (0, c)), pl.BlockSpec((1, 128), lambda c: (0, c))],
        out_shape=[
            jax.ShapeDtypeStruct((T, CONV_C), F32),
            jax.ShapeDtypeStruct((T, CONV_C), F32),
            jax.ShapeDtypeStruct((32, CONV_C), F32),
            jax.ShapeDtypeStruct((1, CONV_C), F32),
        ],
        scratch_shapes=[
            pltpu.VMEM((T + CONV_PAD, 128), F32),
            pltpu.VMEM((T + CONV_PAD, 128), F32),
            pltpu.VMEM((8 * 32, 128), F32),
        ],
        compiler_params=_cp("parallel"),
        name=name,
    )(dc, u, u, dw_w)


ATT_SCALE = A_HD ** -0.5
N_BLK = T // BLK


def _att_masks():
    i = lax.broadcasted_iota(jnp.int32, (BLK, 2 * BLK), 0)
    j = lax.broadcasted_iota(jnp.int32, (BLK, 2 * BLK), 1)
    band = (j >= i) & (j <= i + BLK)
    i1 = lax.broadcasted_iota(jnp.int32, (BLK, BLK), 0)
    j1 = lax.broadcasted_iota(jnp.int32, (BLK, BLK), 1)
    return band, j1 <= i1


def _att_rows(d, t, first):
    if first:
        base = t
        return pl.ds(base, BLK, stride=d), pl.ds(base, BLK, stride=d)
    c = t % d
    n = t // d + 1
    base = c + (BLK * d) * n
    return pl.ds(base, BLK, stride=d), pl.ds(base - BLK * d, 2 * BLK, stride=d)


def attn_fwd(qkv, name):
    def body(q_ref, k_ref, v_ref, o_ref, lse_ref, og, lg):
        band, tri = _att_masks()
        head0 = lax.broadcasted_iota(jnp.int32, (BLK, 128), 1) < A_HD
        for g, d in enumerate(DILATIONS):
            def block(t, carry, first, g=g, d=d):
                rq, rk = _att_rows(d, t, first)
                q2 = q_ref[rq, :]
                k2 = k_ref[rk, :].astype(BF16)
                v2 = v_ref[rk, :].astype(BF16)
                o_e, l_e = [], []
                for e in range(2):
                    qe = jnp.where(head0 if e == 0 else ~head0, q2, 0.0).astype(BF16)
                    s = _dot_nt(qe, k2) * ATT_SCALE
                    s = jnp.where(tri if first else band, s, -jnp.inf)
                    m = jnp.max(s, axis=-1, keepdims=True)
                    p = jnp.exp(s - m)
                    den = jnp.sum(p, axis=-1, keepdims=True)
                    o_e.append(_dot(p.astype(BF16), v2) / den)
                    l_e.append(m + jnp.log(den))
                og[g, rq, :] = jnp.where(head0, o_e[0], o_e[1])
                lg[g, rq, :] = jnp.where(head0, l_e[0], l_e[1])
                return carry

            lax.fori_loop(0, d, functools.partial(block, first=True), 0)
            lax.fori_loop(0, N_BLK - d, functools.partial(block, first=False), 0)
        rc = 256
        for r in range(T // rc):
            rows = pl.ds(r * rc, rc)
            l0, l1, l2 = lg[0, rows, :], lg[1, rows, :], lg[2, rows, :]
            m = jnp.maximum(jnp.maximum(l0, l1), l2)
            e0, e1, e2 = jnp.exp(l0 - m), jnp.exp(l1 - m), jnp.exp(l2 - m)
            z = e0 + e1 + e2
            o_ref[rows, :] = (e0 / z) * og[0, rows, :] + (e1 / z) * og[1, rows, :] + (e2 / z) * og[2, rows, :]
            lse_ref[rows, :] = m + jnp.log(z)

    npair = A_HEADS // 2
    col = lambda off: pl.BlockSpec((T, 128), lambda p: (0, off + p))
    return pl.pallas_call(
        body,
        grid=(npair,),
        in_specs=[col(0), col(npair), col(2 * npair)],
        out_specs=[col(0), col(0)],
        out_shape=[jax.ShapeDtypeStruct((T, A_W), F32), jax.ShapeDtypeStruct((T, A_W), F32)],
        scratch_shapes=[pltpu.VMEM((3, T, 128), F32), pltpu.VMEM((3, T, 128), F32)],
        compiler_params=_cp("parallel"),
        name=name,
    )(qkv, qkv, qkv)


def attn_bwd(qkv, o, lse, do, name):
    def body(q_ref, k_ref, v_ref, o_ref, lse_ref, do_ref, dq_ref, dk_ref, dv_ref):
        band, tri = _att_masks()
        head0 = lax.broadcasted_iota(jnp.int32, (BLK, 128), 1) < A_HD
        head0k = lax.broadcasted_iota(jnp.int32, (2 * BLK, 128), 1) < A_HD
        dq_ref[...] = jnp.zeros_like(dq_ref)
        dk_ref[...] = jnp.zeros_like(dk_ref)
        dv_ref[...] = jnp.zeros_like(dv_ref)
        for d in DILATIONS:
            def block(t, carry, first, d=d):
                rq, rk = _att_rows(d, t, first)
                q2 = q_ref[rq, :]
                k2 = k_ref[rk, :].astype(BF16)
                v2 = v_ref[rk, :].astype(BF16)
                do2 = do_ref[rq, :]
                l2 = lse_ref[rq, :]
                prod = do2 * o_ref[rq, :]
                q2b = q2.astype(BF16)
                do2b = do2.astype(BF16)
                dq_e, dk_e, dv_e = [], [], []
                for e in range(2):
                    he = head0 if e == 0 else ~head0
                    qe = jnp.where(he, q2, 0.0).astype(BF16)
                    doe = jnp.where(he, do2, 0.0).astype(BF16)
                    l = l2[:, A_HD * e:A_HD * e + 1]
                    dd = jnp.sum(jnp.where(he, prod, 0.0), axis=-1, keepdims=True)
                    s = _dot_nt(qe, k2) * ATT_SCALE
                    p = jnp.where(tri if first else band, jnp.exp(s - l), 0.0)
                    dp = _dot_nt(doe, v2)
                    ds = (p * (dp - dd) * ATT_SCALE).astype(BF16)
                    dq_e.append(_dot(ds, k2))
                    dk_e.append(_dot_tn(ds, q2b))
                    dv_e.append(_dot_tn(p.astype(BF16), do2b))
                hk = head0 if first else head0k
                dq_ref[rq, :] += jnp.where(head0, dq_e[0], dq_e[1])
                dk_ref[rk, :] += jnp.where(hk, dk_e[0], dk_e[1])
                dv_ref[rk, :] += jnp.where(hk, dv_e[0], dv_e[1])
                return carry

            lax.fori_loop(0, d, functools.partial(block, first=True), 0)
            lax.fori_loop(0, N_BLK - d, functools.partial(block, first=False), 0)

    npair = A_HEADS // 2
    col = lambda off: pl.BlockSpec((T, 128), lambda p: (0, off + p))
    return pl.pallas_call(
        body,
        grid=(npair,),
        in_specs=[col(0), col(npair), col(2 * npair), col(0), col(0), col(0)],
        out_specs=[col(0), col(0), col(0)],
        out_shape=[jax.ShapeDtypeStruct((T, A_W), F32)] * 3,
        compiler_params=_cp("parallel"),
        name=name,
    )(qkv, qkv, qkv, o, lse, do)


def _ln_silu(x, g, b):
    mu = jnp.mean(x, axis=-1, keepdims=True)
    xc = x - mu
    rstd = lax.rsqrt(jnp.mean(xc * xc, axis=-1, keepdims=True) + EPS)
    xh = xc * rstd
    y = xh * g + b
    sig = _sigmoid(y)
    return y * sig, (xh, rstd, y, sig)


def hyb_out_fwd(h, attn, cpre, ln_g, ln_b, w_out, name):
    tt = 512

    def body(h_ref, a_ref, c_ref, g_ref, b_ref, w_ref, hnew_ref, cat_ref):
        cn, _ = _ln_silu(c_ref[...], g_ref[...], b_ref[...])
        ab = a_ref[...].astype(BF16)
        cb = cn.astype(BF16)
        cat_ref[:, 0:A_W] = ab
        cat_ref[:, A_W:D] = cb
        hnew_ref[...] = h_ref[...] + _dot(ab, w_ref[0:A_W, :]) + _dot(cb, w_ref[A_W:D, :])

    half = pl.BlockSpec((tt, A_W), lambda i: (i, 0))
    vec = pl.BlockSpec((1, CONV_C), lambda i: (0, 0))
    full = pl.BlockSpec((tt, D), lambda i: (i, 0))
    return pl.pallas_call(
        body,
        grid=(T // tt,),
        in_specs=[full, half, half, vec, vec, pl.BlockSpec((D, D), lambda i: (0, 0))],
        out_specs=[full, full],
        out_shape=[jax.ShapeDtypeStruct((T, D), F32), jax.ShapeDtypeStruct((T, D), BF16)],
        compiler_params=_cp("parallel"),
        name=name,
    )(h, attn, cpre, ln_g, ln_b, w_out)


def hyb_out_bwd(dres, cpre, ln_g, ln_b, w_out, name):
    tt = 512

    def body(d_ref, c_ref, g_ref, b_ref, w_ref, da_ref, dc_ref, dg_ref, db_ref):
        i = pl.program_id(0)
        db16 = d_ref[...].astype(BF16)
        da_ref[...] = _dot_nt(db16, w_ref[0:A_W, :])
        dcn = _dot_nt(db16, w_ref[A_W:D, :])
        g = g_ref[...]
        _, (xh, rstd, y, sig) = _ln_silu(c_ref[...], g, b_ref[...])
        dy = dcn * _dsilu(y, sig)
        dxh = dy * g
        dc_ref[...] = rstd * (dxh - jnp.mean(dxh, axis=-1, keepdims=True)
                              - xh * jnp.mean(dxh * xh, axis=-1, keepdims=True))
        dg = jnp.sum(dy * xh, axis=0, keepdims=True)
        db = jnp.sum(dy, axis=0, keepdims=True)

        @pl.when(i == 0)
        def _():
            dg_ref[...] = dg
            db_ref[...] = db

        @pl.when(i > 0)
        def _():
            dg_ref[...] += dg
            db_ref[...] += db

    half = pl.BlockSpec((tt, A_W), lambda i: (i, 0))
    vec = pl.BlockSpec((1, CONV_C), lambda i: (0, 0))
    return pl.pallas_call(
        body,
        grid=(T // tt,),
        in_specs=[pl.BlockSpec((tt, D), lambda i: (i, 0)), half, vec, vec, pl.BlockSpec((D, D), lambda i: (0, 0))],
        out_specs=[half, half, vec, vec],
        out_shape=[
            jax.ShapeDtypeStruct((T, A_W), F32),
            jax.ShapeDtypeStruct((T, CONV_C), F32),
            jax.ShapeDtypeStruct((1, CONV_C), F32),
            jax.ShapeDtypeStruct((1, CONV_C), F32),
        ],
        compiler_params=_cp("arbitrary"),
        name=name,
    )(dres, cpre, ln_g, ln_b, w_out)


def hybrid_fwd(h, g_row, w_in, dw_w, dw_b, ln_g, ln_b, w_out, rope, tag):
    hn, qkv, u = proj_fwd(h, g_row, w_in, [(0, 3 * A_W), (3 * A_W, 2 * CONV_C)], f"hyb_proj_{tag}", rope=rope)
    cpre = hyb_conv_fwd(u, dw_w, dw_b, f"hyb_conv_{tag}")
    attn, lse = attn_fwd(qkv, f"attn_fwd_{tag}")
    hnew, cat = hyb_out_fwd(h, attn, cpre, ln_g, ln_b, w_out, f"hyb_out_{tag}")
    return hnew, (h, hn, qkv, u, cpre, attn, lse, cat)


def hybrid_bwd(dres, saved, g_row, w_in, dw_w, ln_g, ln_b, w_out, rope, tag):
    h, hn, qkv, u, cpre, attn, lse, cat = saved
    d_attn, d_cpre, d_lng, d_lnb = hyb_out_bwd(dres, cpre, ln_g, ln_b, w_out, f"hyb_out_bwd_{tag}")
    d_wout = mm_tn(cat, dres, f"hyb_wout_grad_{tag}")
    d_a, d_gate, d_dw, d_db = hyb_conv_bwd(d_cpre, u, dw_w, f"hyb_conv_bwd_{tag}")
    dq, dk, dv = attn_bwd(qkv, attn, lse, d_attn, f"attn_bwd_{tag}")
    splits = [(0, A_W), (A_W, A_W), (2 * A_W, A_W), (3 * A_W, CONV_C), (3 * A_W + CONV_C, CONV_C)]
    dres_new, d_norm, dq_u, dk_u = proj_bwd_data(
        h, g_row, w_in, [dq, dk, dv, d_a, d_gate], splits, dres, f"hyb_proj_bwd_{tag}", rope=rope, n_rot=2)
    parts = [dq_u, dk_u, dv, d_a, d_gate]
    d_win = jnp.concatenate([mm_tn(hn, p, f"hyb_win_grad_{tag}_{k}") for k, p in enumerate(parts)], axis=1)
    return dres_new, dict(norm=d_norm, w_in=d_win, dw_w=d_dw[:CONV_K], dw_b=d_db, ln_g=d_lng, ln_b=d_lnb, w_out=d_wout)


G_SCALE = G_DK ** -0.5
GP_RC = 256
GP_PAD = 8


def gdn_prep_fwd(x, conv_w, name):
    def body(x_ref, w_ref, o_ref, xpad):
        cb = pl.program_id(0)
        xpad[0:GP_PAD, :] = jnp.zeros((GP_PAD, 128), F32)
        xpad[GP_PAD:, :] = x_ref[...]
        for r in range(T // GP_RC):
            r0 = r * GP_RC
            y = jnp.zeros((GP_RC, 128), F32)
            for j in range(G_CONV):
                y = y + w_ref[pl.ds(j, 1), :] * xpad[pl.ds(r0 + GP_PAD - (G_CONV - 1) + j, GP_RC), :]
            s = y * _sigmoid(y)
            n = lax.rsqrt(jnp.sum(s * s, axis=-1, keepdims=True) + EPS)
            o_ref[r0:r0 + GP_RC, :] = s * jnp.where(cb < 2 * G_HEADS, n, 1.0)

    nb = G_QKV // 128
    return pl.pallas_call(
        body,
        grid=(nb,),
        in_specs=[pl.BlockSpec((T, 128), lambda c: (0, c)), pl.BlockSpec((G_CONV, 128), lambda c: (0, c))],
        out_specs=pl.BlockSpec((T, 128), lambda c: (0, c)),
        out_shape=jax.ShapeDtypeStruct((T, G_QKV), F32),
        scratch_shapes=[pltpu.VMEM((T + GP_PAD, 128), F32)],
        compiler_params=_cp("parallel"),
        name=name,
    )(x, conv_w)


def gdn_prep_bwd(dout, x, conv_w, part, l2, name):
    def body(d_ref, x_ref, w_ref, dx_ref, dw_ref, xpad, dypad, dwacc):
        xpad[0:GP_PAD, :] = jnp.zeros((GP_PAD, 128), F32)
        xpad[GP_PAD:, :] = x_ref[...]
        dypad[T:, :] = jnp.zeros((GP_PAD, 128), F32)
        dwacc[...] = jnp.zeros_like(dwacc)
        for r in range(T // GP_RC):
            r0 = r * GP_RC
            y = jnp.zeros((GP_RC, 128), F32)
            xs = []
            for j in range(G_CONV):
                xj = xpad[pl.ds(r0 + GP_PAD - (G_CONV - 1) + j, GP_RC), :]
                xs.append(xj)
                y = y + w_ref[pl.ds(j, 1), :] * xj
            sig = _sigmoid(y)
            s = y * sig
            d = d_ref[r0:r0 + GP_RC, :]
            if l2:
                n = lax.rsqrt(jnp.sum(s * s, axis=-1, keepdims=True) + EPS)
                out = s * n
                d = n * (d - out * jnp.sum(d * out, axis=-1, keepdims=True))
            dy = d * _dsilu(y, sig)
            dypad[r0:r0 + GP_RC, :] = dy
            for j in range(G_CONV):
                dwacc[8 * j:8 * j + 8, :] += (dy * xs[j]).reshape(GP_RC // 8, 8, 128).sum(axis=0)
        for r in range(T // GP_RC):
            r0 = r * GP_RC
            dx = jnp.zeros((GP_RC, 128), F32)
            for j in range(G_CONV):
                dx = dx + w_ref[pl.ds(j, 1), :] * dypad[pl.ds(r0 + (G_CONV - 1) - j, GP_RC), :]
            dx_ref[r0:r0 + GP_RC, :] = dx
        for j in range(G_CONV):
            dw_ref[pl.ds(j, 1), :] = jnp.sum(dwacc[8 * j:8 * j + 8, :], axis=0, keepdims=True)

    nb = G_HEADS
    off = part * nb
    col = pl.BlockSpec((T, 128), lambda c: (0, c))
    return pl.pallas_call(
        body,
        grid=(nb,),
        in_specs=[col, pl.BlockSpec((T, 128), lambda c: (0, off + c)), pl.BlockSpec((G_CONV, 128), lambda c: (0, off + c))],
        out_specs=[col, pl.BlockSpec((G_CONV, 128), lambda c: (0, c))],
        out_shape=[jax.ShapeDtypeStruct((T, G_HEADS * G_DK), F32), jax.ShapeDtypeStruct((G_CONV, G_HEADS * G_DK), F32)],
        scratch_shapes=[
            pltpu.VMEM((T + GP_PAD, 128), F32),
            pltpu.VMEM((T + GP_PAD, 128), F32),
            pltpu.VMEM((8 * G_CONV, 128), F32),
        ],
        compiler_params=_cp("parallel"),
        name=name,
    )(dout, x, conv_w)


def _seg_cumsum(x, reverse=False):
    row = lax.broadcasted_iota(jnp.int32, x.shape, 0) % CH
    s = 1
    while s < CH:
        if reverse:
            x = x + jnp.where(row < CH - s, pltpu.roll(x, x.shape[0] - s, 0), 0.0)
        else:
            x = x + jnp.where(row >= s, pltpu.roll(x, s, 0), 0.0)
        s *= 2
    return x


def _gdn_gates(ba_ref, alog_ref, dt_ref, h):
    ba = ba_ref[...]
    lane = lax.broadcasted_iota(jnp.int32, ba.shape, 1)
    b_col = jnp.sum(jnp.where(lane == h, ba, 0.0), axis=1, keepdims=True)
    a_col = jnp.sum(jnp.where(lane == G_HEADS + h, ba, 0.0), axis=1, keepdims=True)
    lane8 = lax.broadcasted_iota(jnp.int32, (1, G_HEADS), 1)
    alog = jnp.sum(jnp.where(lane8 == h, alog_ref[...], 0.0), axis=1, keepdims=True)
    dt = jnp.sum(jnp.where(lane8 == h, dt_ref[...], 0.0), axis=1, keepdims=True)
    beta = _sigmoid(b_col)
    xa = a_col + dt
    softplus = jnp.maximum(xa, 0.0) + jnp.log(1.0 + jnp.exp(-jnp.abs(xa)))
    ea = jnp.exp(alog)
    return beta, -ea * softplus, xa, ea


def _chunk_masks():
    i = lax.broadcasted_iota(jnp.int32, (CH, CH), 0)
    j = lax.broadcasted_iota(jnp.int32, (CH, CH), 1)
    return i >= j, i > j, i, j


def _decay(gcc, causal):
    gm = gcc[:, 0:CH]
    return jnp.where(causal, jnp.exp(jnp.minimum(gm - gm.T, 0.0)), 0.0)


def _split(a):
    hi = a.astype(BF16)
    return hi, (a - hi.astype(F32)).astype(BF16)


def _dot3(a, b):
    ah, al = _split(a)
    bh, bl = _split(b)
    return _dot(ah, bh) + (_dot(ah, bl) + _dot(al, bh))


def _unit_lower_inverse(lms, i, j):
    eye = jnp.where(i == j, 1.0, 0.0)
    ms = [None] * len(lms)
    b = 1
    while b < CH:
        pair = ((i // (2 * b)) == (j // (2 * b))) & ((i // b) % 2 == 1) & ((j // b) % 2 == 0)
        lbs = [jnp.where(pair, lm, 0.0) for lm in lms]
        if b == 1:
            ms = [eye - lb for lb in lbs]
        else:
            ts = [_dot3(m, lb) for m, lb in zip(ms, lbs)]
            ms = [m - _dot3(t, m) for m, t in zip(ms, ts)]
        b *= 2
    return ms


def gdn_local_fwd(qkv, ba, alog, dtb, name):
    def body(q_ref, k_ref, v_ref, ba_ref, al_ref, dt_ref, u_ref, w_ref, qd_ref, kd_ref, at_ref, el_ref, ti_ref, gcs):
        h = pl.program_id(1)
        beta, g, _, _ = _gdn_gates(ba_ref, al_ref, dt_ref, h)
        gc = _seg_cumsum(jnp.broadcast_to(g, (GRP, 128)))
        gcs[...] = gc
        causal, strict, i, j = _chunk_masks()
        lms = []
        for c in range(CPG):
            r = slice(c * CH, (c + 1) * CH)
            q, k = q_ref[r, :], k_ref[r, :]
            gcc = gc[r, :]
            ec = jnp.exp(gcc)
            gl = gcs[pl.ds(c * CH + CH - 1, 1), :]
            dm = _decay(gcc, causal)
            kbf = k.astype(BF16)
            a1 = _dot_nt((k * beta[r, :]).astype(BF16), kbf)
            lms.append(jnp.where(strict, a1 * dm, 0.0))
            qs = q * G_SCALE
            qd_ref[r, :] = (qs * ec).astype(BF16)
            kd_ref[r, :] = (k * jnp.exp(gl - gcc)).astype(BF16)
            at_ref[r, :] = (_dot_nt(qs.astype(BF16), kbf) * dm).astype(BF16)
            el_ref[pl.ds(c, 1), :] = jnp.exp(gl)
        tinvs = _unit_lower_inverse(lms, i, j)
        for c in range(CPG):
            r = slice(c * CH, (c + 1) * CH)
            bt = beta[r, :]
            tb = tinvs[c].astype(BF16)
            u_ref[r, :] = _dot(tb, (v_ref[r, :] * bt).astype(BF16))
            w_ref[r, :] = _dot(tb, (k_ref[r, :] * bt * jnp.exp(gc[r, :])).astype(BF16)).astype(BF16)
            ti_ref[r, :] = tinvs[c]

    hd = lambda off: pl.BlockSpec((GRP, 128), lambda i, h: (i, off + h))
    vec = pl.BlockSpec((1, G_HEADS), lambda i, h: (0, 0))
    sq = pl.BlockSpec((None, GRP, CH), lambda i, h: (h, i, 0))
    return pl.pallas_call(
        body,
        grid=(N_GRP, G_HEADS),
        in_specs=[hd(0), hd(G_HEADS), hd(2 * G_HEADS), pl.BlockSpec((GRP, 2 * G_HEADS), lambda i, h: (i, 0)), vec, vec],
        out_specs=[hd(0), hd(0), hd(0), hd(0), sq, pl.BlockSpec((None, CPG, 128), lambda i, h: (h, i, 0)), sq],
        out_shape=[
            jax.ShapeDtypeStruct((T, D), F32),
            jax.ShapeDtypeStruct((T, D), BF16),
            jax.ShapeDtypeStruct((T, D), BF16),
            jax.ShapeDtypeStruct((T, D), BF16),
            jax.ShapeDtypeStruct((G_HEADS, T, CH), BF16),
            jax.ShapeDtypeStruct((G_HEADS, T // CH, 128), F32),
            jax.ShapeDtypeStruct((G_HEADS, T, CH), F32),
        ],
        scratch_shapes=[pltpu.VMEM((GRP, 128), F32)],
        compiler_params=_cp("parallel", "parallel"),
        name=name,
    )(qkv, qkv, qkv, ba, alog, dtb)


def gdn_rec_fwd(u, w, qd, kd, at, el, name):
    def body(u_ref, w_ref, qd_ref, kd_ref, at_ref, el_ref, o_ref, vn_ref, st_ref, s_scr):
        @pl.when(pl.program_id(0) == 0)
        def _():
            s_scr[...] = jnp.zeros_like(s_scr)

        for c in range(CPG):
            r = slice(c * CH, (c + 1) * CH)
            for h in range(G_HEADS):
                ln = slice(h * 128, (h + 1) * 128)
                s = s_scr[h]
                st_ref[h, c] = s
                sb = s.astype(BF16)
                vn = (u_ref[r, ln] - _dot(w_ref[r, ln], sb)).astype(BF16)
                o_ref[r, ln] = _dot(qd_ref[r, ln], sb) + _dot(at_ref[h, r, :], vn)
                s_scr[h] = s * el_ref[h, pl.ds(c, 1), :] + _dot_tn(kd_ref[r, ln], vn)
                vn_ref[r, ln] = vn

    row = pl.BlockSpec((GRP, D), lambda i: (i, 0))
    return pl.pallas_call(
        body,
        grid=(N_GRP,),
        in_specs=[row, row, row, row, pl.BlockSpec((G_HEADS, GRP, CH), lambda i: (0, i, 0)),
                  pl.BlockSpec((G_HEADS, CPG, 128), lambda i: (0, i, 0))],
        out_specs=[row, row, pl.BlockSpec((G_HEADS, CPG, 128, 128), lambda i: (0, i, 0, 0))],
        out_shape=[
            jax.ShapeDtypeStruct((T, D), F32),
            jax.ShapeDtypeStruct((T, D), BF16),
            jax.ShapeDtypeStruct((G_HEADS, T // CH, 128, 128), F32),
        ],
        scratch_shapes=[pltpu.VMEM((G_HEADS, 128, 128), F32)],
        compiler_params=_cp("arbitrary"),
        name=name,
    )(u, w, qd, kd, at, el)


def gdn_rec_bwd(do, w, qd, kd, at, el, vn, st, name):
    def body(do_ref, w_ref, qd_ref, kd_ref, at_ref, el_ref, vn_ref, st_ref,
             du_ref, dw_ref, dqd_ref, dkd_ref, dat_ref, del_ref, ds_scr):
        @pl.when(pl.program_id(0) == 0)
        def _():
            ds_scr[...] = jnp.zeros_like(ds_scr)

        for c in reversed(range(CPG)):
            r = slice(c * CH, (c + 1) * CH)
            for h in range(G_HEADS):
                ln = slice(h * 128, (h + 1) * 128)
                ds = ds_scr[h]
                dsb = ds.astype(BF16)
                sn = st_ref[h, c]
                snb = sn.astype(BF16)
                dob = do_ref[r, ln].astype(BF16)
                vnb = vn_ref[r, ln]
                dvn = (_dot(kd_ref[r, ln], dsb) + _dot_tn(at_ref[h, r, :], dob)).astype(BF16)
                du_ref[r, ln] = dvn
                dkd_ref[r, ln] = _dot_nt(vnb, dsb)
                tot = jnp.sum(jnp.sum(ds * sn, axis=1, keepdims=True), axis=0, keepdims=True)
                del_ref[h, pl.ds(c, 1), :] = jnp.broadcast_to(tot, (1, 128))
                dqd_ref[r, ln] = _dot_nt(dob, snb)
                dat_ref[h, r, :] = _dot_nt(dob, vnb)
                dw_ref[r, ln] = (-_dot_nt(dvn, snb)).astype(BF16)
                ds_scr[h] = ds * el_ref[h, pl.ds(c, 1), :] + _dot_tn(qd_ref[r, ln], dob) - _dot_tn(w_ref[r, ln], dvn)

    last = N_GRP - 1
    row = pl.BlockSpec((GRP, D), lambda i: (last - i, 0))
    sq = pl.BlockSpec((G_HEADS, GRP, CH), lambda i: (0, last - i, 0))
    sc = pl.BlockSpec((G_HEADS, CPG, 128), lambda i: (0, last - i, 0))
    return pl.pallas_call(
        body,
        grid=(N_GRP,),
        in_specs=[row, row, row, row, sq, sc, row, pl.BlockSpec((G_HEADS, CPG, 128, 128), lambda i: (0, last - i, 0, 0))],
        out_specs=[row, row, row, row, sq, sc],
        out_shape=[
            jax.ShapeDtypeStruct((T, D), BF16),
            jax.ShapeDtypeStruct((T, D), BF16),
            jax.ShapeDtypeStruct((T, D), F32),
            jax.ShapeDtypeStruct((T, D), F32),
            jax.ShapeDtypeStruct((G_HEADS, T, CH), F32),
            jax.ShapeDtypeStruct((G_HEADS, T // CH, 128), F32),
        ],
        scratch_shapes=[pltpu.VMEM((G_HEADS, 128, 128), F32)],
        compiler_params=_cp("arbitrary"),
        name=name,
    )(do, w, qd, kd, at, el, vn, st)


def gdn_local_bwd(qkv, ba, alog, dtb, tinv, du, dw, dqd, dkd, dat, dl, name):
    def body(q_ref, k_ref, v_ref, ba_ref, al_ref, dt_ref, ti_ref, du_ref, dw_ref, dqd_ref, dkd_ref, dat_ref, dl_ref,
             dq_ref, dk_ref, dv_ref, dba_ref, dal_ref, ddt_ref, gcs):
        gi = pl.program_id(0)
        h = pl.program_id(1)
        beta, g, xa, ea = _gdn_gates(ba_ref, al_ref, dt_ref, h)
        gc = _seg_cumsum(jnp.broadcast_to(g, (GRP, 128)))
        gcs[...] = gc
        causal, strict, _, _ = _chunk_masks()
        dgc_l, dgl_l, dbeta_l = [], [], []
        for c in range(CPG):
            r = slice(c * CH, (c + 1) * CH)
            q, k, v = q_ref[r, :], k_ref[r, :], v_ref[r, :]
            bt = beta[r, :]
            gcc = gc[r, :]
            ec = jnp.exp(gcc)
            gl = gcs[pl.ds(c * CH + CH - 1, 1), :]
            f2 = jnp.exp(gl - gcc)
            elc = jnp.exp(gl)
            dm = _decay(gcc, causal)
            qs = q * G_SCALE
            kb = k * bt
            vb = v * bt
            kbe = kb * ec
            kbf, kbb, qsb = k.astype(BF16), kb.astype(BF16), qs.astype(BF16)
            a1 = _dot_nt(kbb, kbf)
            qk = _dot_nt(qsb, kbf)
            ti = ti_ref[r, :]
            tb = ti.astype(BF16)
            du_c, dw_c = du_ref[r, :], dw_ref[r, :]
            dqd_c, dkd_c, dat_c = dqd_ref[r, :], dkd_ref[r, :], dat_ref[r, :]

            dqs = dqd_c * ec
            d_e = jnp.sum(dqd_c * qs, axis=1, keepdims=True)
            dk = dkd_c * f2
            tcol = jnp.sum(dkd_c * k, axis=1, keepdims=True) * f2[:, 0:1]
            dgl = jnp.sum(tcol, axis=0, keepdims=True) + dl_ref[pl.ds(c, 1), 0:1] * elc[:, 0:1]
            dgc = -tcol
            dqk = (dat_c * dm).astype(BF16)
            d_d = dat_c * qk
            dqs = dqs + _dot(dqk, kbf)
            dk = dk + _dot_tn(dqk, qsb)
            dtinv = _dot_nt(du_c, vb.astype(BF16)) + _dot_nt(dw_c, kbe.astype(BF16))
            dvb = _dot_tn(tb, du_c)
            dkbe = _dot_tn(tb, dw_c)
            dlm = jnp.where(strict, -_dot3(_dot3(ti.T, dtinv), ti.T), 0.0)
            da1 = (dlm * dm).astype(BF16)
            d_d = d_d + dlm * a1
            dkb = _dot(da1, kbf) + dkbe * ec
            dk = dk + _dot_tn(da1, kbb)
            d_e = d_e + jnp.sum(dkbe * kb, axis=1, keepdims=True)
            dk = dk + dkb * bt
            dbeta_l.append(jnp.sum(dkb * k, axis=1, keepdims=True) + jnp.sum(dvb * v, axis=1, keepdims=True))
            ddiff = d_d * dm
            dgc = dgc + jnp.sum(ddiff, axis=1, keepdims=True) - jnp.sum(ddiff.T, axis=1, keepdims=True)
            dgc = dgc + d_e * ec[:, 0:1]
            dgc_l.append(dgc)
            dgl_l.append(jnp.broadcast_to(dgl, (CH, 1)))
            dq_ref[r, :] = dqs * G_SCALE
            dk_ref[r, :] = dk
            dv_ref[r, :] = dvb * bt

        dgc_all = jnp.broadcast_to(jnp.concatenate(dgc_l, axis=0), (GRP, 128))
        dg = _seg_cumsum(dgc_all, reverse=True)[:, 0:1] + jnp.concatenate(dgl_l, axis=0)
        dbeta = jnp.concatenate(dbeta_l, axis=0)
        da = dg * (-ea) * _sigmoid(xa)
        db = dbeta * beta * (1.0 - beta)
        lane = lax.broadcasted_iota(jnp.int32, (GRP, 2 * G_HEADS), 1)
        dba = jnp.where(lane == h, db, 0.0) + jnp.where(lane == G_HEADS + h, da, 0.0)
        lane8 = lax.broadcasted_iota(jnp.int32, (1, G_HEADS), 1)
        dal = jnp.where(lane8 == h, jnp.sum(dg * g, axis=0, keepdims=True), 0.0)
        ddt = jnp.where(lane8 == h, jnp.sum(da, axis=0, keepdims=True), 0.0)

        @pl.when(h == 0)
        def _():
            dba_ref[...] = dba

        @pl.when(h > 0)
        def _():
            dba_ref[...] += dba

        @pl.when((h == 0) & (gi == 0))
        def _():
            dal_ref[...] = dal
            ddt_ref[...] = ddt

        @pl.when((h > 0) | (gi > 0))
        def _():
            dal_ref[...] += dal
            ddt_ref[...] += ddt

    hd = lambda off: pl.BlockSpec((GRP, 128), lambda i, h: (i, off + h))
    vec = pl.BlockSpec((1, G_HEADS), lambda i, h: (0, 0))
    sq = pl.BlockSpec((None, GRP, CH), lambda i, h: (h, i, 0))
    gates = pl.BlockSpec((GRP, 2 * G_HEADS), lambda i, h: (i, 0))
    return pl.pallas_call(
        body,
        grid=(N_GRP, G_HEADS),
        in_specs=[hd(0), hd(G_HEADS), hd(2 * G_HEADS), gates, vec, vec, sq, hd(0), hd(0), hd(0), hd(0), sq,
                  pl.BlockSpec((None, CPG, 128), lambda i, h: (h, i, 0))],
        out_specs=[hd(0), hd(0), hd(0), gates, vec, vec],
        out_shape=[
            jax.ShapeDtypeStruct((T, D), F32),
            jax.ShapeDtypeStruct((T, D), F32),
            jax.ShapeDtypeStruct((T, D), F32),
            jax.ShapeDtypeStruct((T, 2 * G_HEADS), F32),
            jax.ShapeDtypeStruct((1, G_HEADS), F32),
            jax.ShapeDtypeStruct((1, G_HEADS), F32),
        ],
        scratch_shapes=[pltpu.VMEM((GRP, 128), F32)],
        compiler_params=_cp("arbitrary", "arbitrary"),
        name=name,
    )(qkv, qkv, qkv, ba, alog, dtb, tinv, du, dw, dqd, dkd, dat, dl)


def _gated_norm(o, z, g):
    rstd = lax.rsqrt(jnp.mean(o * o, axis=-1, keepdims=True) + EPS)
    oh = o * rstd
    sig = _sigmoid(z)
    return oh, rstd, sig


def gdn_out_fwd(h, o, z, norm_g, w_out, name):
    tt = 512

    def body(h_ref, o_ref, z_ref, g_ref, w_ref, hnew_ref, cat_ref):
        g = g_ref[...]
        for hh in range(G_HEADS):
            ln = slice(hh * 128, (hh + 1) * 128)
            zz = z_ref[:, ln]
            oh, _, sig = _gated_norm(o_ref[:, ln], zz, g)
            cat_ref[:, ln] = (oh * g * (zz * sig)).astype(BF16)
        hnew_ref[...] = h_ref[...] + _dot(cat_ref[...], w_ref[...])

    full = pl.BlockSpec((tt, D), lambda i: (i, 0))
    return pl.pallas_call(
        body,
        grid=(T // tt,),
        in_specs=[full, full, full, pl.BlockSpec((1, 128), lambda i: (0, 0)), pl.BlockSpec((D, D), lambda i: (0, 0))],
        out_specs=[full, full],
        out_shape=[jax.ShapeDtypeStruct((T, D), F32), jax.ShapeDtypeStruct((T, D), BF16)],
        compiler_params=_cp("parallel"),
        name=name,
    )(h, o, z, norm_g, w_out)


def gdn_out_bwd(dres, o, z, norm_g, w_out, name):
    tt = 512

    def body(d_ref, o_ref, z_ref, g_ref, w_ref, do_ref, dz_ref, dg_ref, dcat):
        i = pl.program_id(0)
        g = g_ref[...]
        dcat[...] = _dot_nt(d_ref[...].astype(BF16), w_ref[...])
        dg = jnp.zeros((1, 128), F32)
        for hh in range(G_HEADS):
            ln = slice(hh * 128, (hh + 1) * 128)
            zz = z_ref[:, ln]
            oh, rstd, sig = _gated_norm(o_ref[:, ln], zz, g)
            dout = dcat[:, ln]
            dy = dout * (zz * sig)
            dz_ref[:, ln] = dout * (oh * g) * _dsilu(zz, sig)
            dg = dg + jnp.sum(dy * oh, axis=0, keepdims=True)
            doh = dy * g
            do_ref[:, ln] = rstd * (doh - oh * jnp.mean(doh * oh, axis=-1, keepdims=True))

        @pl.when(i == 0)
        def _():
            dg_ref[...] = dg

        @pl.when(i > 0)
        def _():
            dg_ref[...] += dg

    full = pl.BlockSpec((tt, D), lambda i: (i, 0))
    vec = pl.BlockSpec((1, 128), lambda i: (0, 0))
    return pl.pallas_call(
        body,
        grid=(T // tt,),
        in_specs=[full, full, full, vec, pl.BlockSpec((D, D), lambda i: (0, 0))],
        out_specs=[full, full, vec],
        out_shape=[jax.ShapeDtypeStruct((T, D), F32), jax.ShapeDtypeStruct((T, D), F32), jax.ShapeDtypeStruct((1, 128), F32)],
        scratch_shapes=[pltpu.VMEM((tt, D), F32)],
        compiler_params=_cp("arbitrary"),
        name=name,
    )(dres, o, z, norm_g, w_out)


GDN_SPLITS = [(0, 1024), (1024, 1024), (2048, 1024), (3072, 1024), (4096, 2 * G_HEADS)]


def gdn_fwd(h, g_row, w_in, conv_w, alog, dtb, norm_g, w_out, tag):
    hn, qkv_pre, z, ba = proj_fwd(h, g_row, w_in, [(0, G_QKV), (G_QKV, 1024), (4096, 2 * G_HEADS)], f"gdn_proj_{tag}")
    qkv = gdn_prep_fwd(qkv_pre, conv_w, f"gdn_prep_{tag}")
    u, w, qd, kd, at, el, tinv = gdn_local_fwd(qkv, ba, alog, dtb, f"gdn_local_{tag}")
    o, vn, st = gdn_rec_fwd(u, w, qd, kd, at, el, f"gdn_rec_{tag}")
    hnew, cat = gdn_out_fwd(h, o, z, norm_g, w_out, f"gdn_out_{tag}")
    return hnew, (h, hn, qkv_pre, z, ba, qkv, w, qd, kd, at, el, tinv, o, vn, st, cat)


def gdn_bwd(dres, saved, g_row, w_in, conv_w, alog, dtb, norm_g, w_out, tag):
    h, hn, qkv_pre, z, ba, qkv, w, qd, kd, at, el, tinv, o, vn, st, cat = saved
    d_o, d_z, d_ng = gdn_out_bwd(dres, o, z, norm_g, w_out, f"gdn_out_bwd_{tag}")
    d_wout = mm_tn(cat, dres, f"gdn_wout_grad_{tag}")
    du, dw, dqd, dkd, dat, dl = gdn_rec_bwd(d_o, w, qd, kd, at, el, vn, st, f"gdn_rec_bwd_{tag}")
    dq, dk, dv, dba, dal, ddt = gdn_local_bwd(qkv, ba, alog, dtb, tinv, du, dw, dqd, dkd, dat, dl, f"gdn_local_bwd_{tag}")
    dpre, dcw = [], []
    for part, d in enumerate((dq, dk, dv)):
        dx, dwc = gdn_prep_bwd(d, qkv_pre, conv_w, part, part < 2, f"gdn_prep_bwd_{tag}_{part}")
        dpre.append(dx)
        dcw.append(dwc)
    parts = dpre + [d_z, dba]
    dres_new, d_norm = proj_bwd_data(h, g_row, w_in, parts, GDN_SPLITS, dres, f"gdn_proj_bwd_{tag}")
    d_win = jnp.concatenate([mm_tn(hn, p, f"gdn_win_grad_{tag}_{k}") for k, p in enumerate(parts)], axis=1)
    return dres_new, dict(norm=d_norm, w_in=d_win, conv_w=jnp.concatenate(dcw, axis=1), A_log=dal, dt_bias=ddt,
                          norm_g=d_ng, w_out=d_wout)


MESH = pl.DeviceIdType.MESH
ANY = pl.BlockSpec(memory_space=pl.ANY)


def _coords():
    return lax.axis_index("x"), lax.axis_index("y"), lax.axis_index("c")


def _slot(p):
    return 4 * p[0] + 2 * p[1] + p[2]


def all_gather(shards, name):
    k_n = len(shards)

    def body(*refs):
        srcs, dsts = refs[:k_n], refs[k_n:2 * k_n]
        send_sems, recv_sems, local_sems = refs[2 * k_n:]
        x, y, c = _coords()
        me, sibling = (x, y, c), (x, y, 1 - c)
        chips = [(1 - x, y), (x, 1 - y), (1 - x, 1 - y)]

        def copy(k, s, block, to, from_src=False):
            rows = dsts[k].at[_slot(block)]
            return pltpu.make_async_remote_copy(
                src_ref=srcs[k] if from_src else rows, dst_ref=rows,
                send_sem=send_sems.at[k, s], recv_sem=recv_sems.at[k, s], device_id=to, device_id_type=MESH)

        local = [pltpu.make_async_copy(srcs[k], dsts[k].at[_slot(me)], local_sems.at[k]) for k in range(k_n)]
        for cp in local:
            cp.start()
        first = []
        for k in range(k_n):
            first.append(copy(k, 0, me, sibling, True))
            first += [copy(k, 1 + j, me, (*chip, c), True) for j, chip in enumerate(chips)]
        for cp in first:
            cp.start()
        passed = []
        for j, chip in enumerate(chips):
            for k in range(k_n):
                copy(k, 1 + j, (*chip, c), me).wait_recv()
                fw = copy(k, 4 + j, (*chip, c), sibling)
                fw.start()
                passed.append(fw)
        for k in range(k_n):
            copy(k, 0, sibling, me).wait_recv()
            for j, chip in enumerate(chips):
                copy(k, 4 + j, (*chip, 1 - c), me).wait_recv()
        for cp in first + passed:
            cp.wait_send()
        for cp in local:
            cp.wait()

    return pl.pallas_call(
        body,
        in_specs=[ANY] * k_n,
        out_specs=[ANY] * k_n,
        out_shape=[jax.ShapeDtypeStruct((N_DEV,) + s.shape, s.dtype) for s in shards],
        scratch_shapes=[pltpu.SemaphoreType.DMA((k_n, 7)), pltpu.SemaphoreType.DMA((k_n, 7)),
                        pltpu.SemaphoreType.DMA((k_n,))],
        name=name,
    )(*shards)


HBM = pl.BlockSpec(memory_space=pltpu.HBM)
SEM = pl.BlockSpec(memory_space=pltpu.SEMAPHORE)
EFFECT = pltpu.SideEffectType.DATAFLOW_SIDE_EFFECTING


def _hbm(a):
    return pltpu.with_memory_space_constraint(a, pltpu.HBM)


def _peer_list(x, y, c):
    peers = []
    for j in range(1, N_DEV):
        jx, jy, jc = (j >> 2) & 1, (j >> 1) & 1, j & 1
        peers.append((x if jx == 0 else 1 - x, y if jy == 0 else 1 - y, c if jc == 0 else 1 - c))
    return peers


def _push_views(kind, layer, src_ref, land_ref, me, peer_slot):
    if kind == "gather":
        return src_ref, land_ref.at[me], land_ref.at[peer_slot]
    if layer is None:
        return src_ref.at[peer_slot], land_ref.at[me], land_ref.at[peer_slot]
    return src_ref.at[peer_slot], land_ref.at[me, layer], land_ref.at[peer_slot, layer]


def _push_copies(groups, srcs, lands, sems):
    x, y, c = _coords()
    me = _slot((x, y, c))
    peers = _peer_list(x, y, c)
    t = 0
    for gi, group in enumerate(groups):
        for ti, (kind, layer, _, li) in enumerate(group):
            for j, peer in enumerate(peers):
                out, there, here = _push_views(kind, layer, srcs[t], lands[li], me, _slot(peer))
                k = ti * (N_DEV - 1) + j
                yield out, there, here, sems[2 * gi].at[k], sems[2 * gi + 1].at[k], peer
            t += 1


def push_start(groups, lands, name):
    flat = [it for g in groups for it in g]
    n, n_l, n_g = len(flat), len(lands), len(groups)

    def body(*refs):
        srcs, land_refs, sems = refs[:n], refs[n:n + n_l], refs[n + n_l:n + n_l + 2 * n_g]
        for out, there, _, s_sem, r_sem, peer in _push_copies(groups, srcs, land_refs, sems):
            pltpu.make_async_remote_copy(src_ref=out, dst_ref=there, send_sem=s_sem, recv_sem=r_sem,
                                         device_id=peer, device_id_type=MESH).start()

    arrays = [it[2] for it in flat] + list(lands)
    sem_shapes = []
    for g in groups:
        sem_shapes += [pltpu.SemaphoreType.DMA((len(g) * (N_DEV - 1),))] * 2
    outs = pl.pallas_call(
        body,
        name=name,
        in_specs=[HBM] * (n + n_l),
        out_specs=[SEM] * (2 * n_g) + [HBM] * (n + n_l),
        out_shape=sem_shapes + [pltpu.HBM(a.shape, a.dtype) for a in arrays],
        input_output_aliases={i: 2 * n_g + i for i in range(n + n_l)},
        compiler_params=pltpu.CompilerParams(has_side_effects=EFFECT),
    )(*[_hbm(a) for a in arrays])
    return list(outs[:2 * n_g]), list(outs[2 * n_g:2 * n_g + n]), list(outs[2 * n_g + n:])


def push_wait(groups, lands, sems, after, name):
    flat = [it for g in groups for it in g]
    n, n_l, n_g = len(flat), len(lands), len(groups)

    def body(*refs):
        srcs, land_refs, sem_refs = refs[:n], refs[n:n + n_l], refs[n + n_l:n + n_l + 2 * n_g]
        for out, _, here, s_sem, r_sem, peer in _push_copies(groups, srcs, land_refs, sem_refs):
            cp = pltpu.make_async_remote_copy(src_ref=out, dst_ref=here, send_sem=s_sem, recv_sem=r_sem,
                                              device_id=peer, device_id_type=MESH)
            cp.wait_send()
            cp.wait_recv()

    arrays = [it[2] for it in flat] + list(lands)
    outs = pl.pallas_call(
        body,
        name=name,
        in_specs=[HBM] * (n + n_l) + [SEM] * (2 * n_g) + [ANY],
        out_specs=[HBM] * (n + n_l),
        out_shape=[pltpu.HBM(a.shape, a.dtype) for a in arrays],
        input_output_aliases={i: i for i in range(n + n_l)},
        compiler_params=pltpu.CompilerParams(has_side_effects=EFFECT),
    )(*arrays, *sems, after)
    return list(outs[:n]), list(outs[n:])


def sum_slabs(parts, name):
    n, rows, cols = parts.shape

    def body(p_ref, o_ref):
        g = p_ref[0]
        for s in range(1, n):
            g = g + p_ref[s]
        o_ref[...] = g

    return pl.pallas_call(body, out_shape=jax.ShapeDtypeStruct((rows, cols), F32), name=name)(parts)


def _row_tile(rows, cols):
    if rows * cols * 4 <= (1 << 20) or rows % 8:
        return rows
    tr = rows
    while tr % 2 == 0 and (tr // 2) % 8 == 0 and tr * cols * 4 > (1 << 20):
        tr //= 2
    return tr


def adamw(parts, w, m, v, name):
    p_n = parts.shape[0]
    rows, cols = w.shape
    tr = _row_tile(rows, cols)

    def body(p_ref, w_ref, m_ref, v_ref, g_ref, d_ref, nm_ref, nv_ref):
        g = p_ref[0].astype(F32)
        for s in range(1, p_n):
            g = g + p_ref[s].astype(F32)
        m_new = ADAM_B1 * m_ref[...] + (1.0 - ADAM_B1) * g
        v_new = ADAM_B2 * v_ref[...] + (1.0 - ADAM_B2) * (g * g)
        m_hat = m_new / (1.0 - ADAM_B1 ** ADAM_STEP)
        v_hat = v_new / (1.0 - ADAM_B2 ** ADAM_STEP)
        g_ref[...] = g
        d_ref[...] = -ADAM_LR * (m_hat / (jnp.sqrt(v_hat) + ADAM_EPS) + ADAM_WD * w_ref[...])
        nm_ref[...] = m_new
        nv_ref[...] = v_new

    blk = pl.BlockSpec((tr, cols), lambda i: (i, 0))
    return pl.pallas_call(
        body,
        grid=(rows // tr,),
        in_specs=[pl.BlockSpec((p_n, tr, cols), lambda i: (0, i, 0)), blk, blk, blk],
        out_specs=[blk] * 4,
        out_shape=[jax.ShapeDtypeStruct((rows, cols), F32)] * 4,
        compiler_params=_cp("parallel"),
        name=name,
    )(parts, w, m, v)


def _adamw_nd(parts, w, m, v, name):
    shp = w.shape
    cols = shp[-1]
    rows = math.prod(shp[:-1])
    outs = adamw(parts.reshape(parts.shape[0], rows, cols), w.reshape(rows, cols), m.reshape(rows, cols),
                 v.reshape(rows, cols), name)
    return [o.reshape(shp) for o in outs]


REPL = ["ffn1_norm", "mix_norm", "ffn2_norm", "hyb_dw_b", "hyb_ln_g", "hyb_ln_b", "gdn_A_log", "gdn_dt_bias",
        "gdn_norm_g", "final_norm"]
WEIGHTS = ["ffn1_norm", "ffn1_w_in", "ffn1_w_out", "mix_norm", "ffn2_norm", "ffn2_w_in", "ffn2_w_out", "hyb_w_in",
           "hyb_dw_w", "hyb_dw_b", "hyb_ln_g", "hyb_ln_b", "hyb_w_out", "gdn_w_in", "gdn_conv_w", "gdn_A_log",
           "gdn_dt_bias", "gdn_norm_g", "gdn_w_out", "final_norm"]


def _pack(arrs, rows):
    flat = jnp.concatenate([a.reshape(-1) for a in arrs])
    return jnp.pad(flat, (0, rows * 128 - flat.shape[0])).reshape(rows, 128)


def _cols_to_slabs(a):
    d, n = a.shape
    return a.reshape(d, N_DEV, n // N_DEV).transpose(1, 0, 2)


def _slabs_to_cols(a):
    return jnp.moveaxis(a, 0, -2).reshape(a.shape[1:-1] + (N_DEV * a.shape[-1],))


def kernel(x, positions, ffn1_norm, ffn1_w_in, ffn1_w_out, mix_norm, ffn2_norm, ffn2_w_in, ffn2_w_out, hyb_w_in, hyb_dw_w, hyb_dw_b, hyb_ln_g, hyb_ln_b, hyb_w_out, gdn_w_in, gdn_conv_w, gdn_A_log, gdn_dt_bias, gdn_norm_g, gdn_w_out, final_norm, loss_target, m_ffn1_norm, m_ffn1_w_in, m_ffn1_w_out, m_mix_norm, m_ffn2_norm, m_ffn2_w_in, m_ffn2_w_out, m_hyb_w_in, m_hyb_dw_w, m_hyb_dw_b, m_hyb_ln_g, m_hyb_ln_b, m_hyb_w_out, m_gdn_w_in, m_gdn_conv_w, m_gdn_A_log, m_gdn_dt_bias, m_gdn_norm_g, m_gdn_w_out, m_final_norm, v_ffn1_norm, v_ffn1_w_in, v_ffn1_w_out, v_mix_norm, v_ffn2_norm, v_ffn2_w_in, v_ffn2_w_out, v_hyb_w_in, v_hyb_dw_w, v_hyb_dw_b, v_hyb_ln_g, v_hyb_ln_b, v_hyb_w_out, v_gdn_w_in, v_gdn_conv_w, v_gdn_A_log, v_gdn_dt_bias, v_gdn_norm_g, v_gdn_w_out, v_final_norm):
    w = dict(ffn1_norm=ffn1_norm, ffn1_w_in=ffn1_w_in, ffn1_w_out=ffn1_w_out, mix_norm=mix_norm, ffn2_norm=ffn2_norm,
             ffn2_w_in=ffn2_w_in, ffn2_w_out=ffn2_w_out, hyb_w_in=hyb_w_in, hyb_dw_w=hyb_dw_w, hyb_dw_b=hyb_dw_b,
             hyb_ln_g=hyb_ln_g, hyb_ln_b=hyb_ln_b, hyb_w_out=hyb_w_out, gdn_w_in=gdn_w_in, gdn_conv_w=gdn_conv_w,
             gdn_A_log=gdn_A_log, gdn_dt_bias=gdn_dt_bias, gdn_norm_g=gdn_norm_g, gdn_w_out=gdn_w_out,
             final_norm=final_norm)
    mom = dict(ffn1_norm=m_ffn1_norm, ffn1_w_in=m_ffn1_w_in, ffn1_w_out=m_ffn1_w_out, mix_norm=m_mix_norm,
               ffn2_norm=m_ffn2_norm, ffn2_w_in=m_ffn2_w_in, ffn2_w_out=m_ffn2_w_out, hyb_w_in=m_hyb_w_in,
               hyb_dw_w=m_hyb_dw_w, hyb_dw_b=m_hyb_dw_b, hyb_ln_g=m_hyb_ln_g, hyb_ln_b=m_hyb_ln_b,
               hyb_w_out=m_hyb_w_out, gdn_w_in=m_gdn_w_in, gdn_conv_w=m_gdn_conv_w, gdn_A_log=m_gdn_A_log,
               gdn_dt_bias=m_gdn_dt_bias, gdn_norm_g=m_gdn_norm_g, gdn_w_out=m_gdn_w_out, final_norm=m_final_norm)
    var = dict(ffn1_norm=v_ffn1_norm, ffn1_w_in=v_ffn1_w_in, ffn1_w_out=v_ffn1_w_out, mix_norm=v_mix_norm,
               ffn2_norm=v_ffn2_norm, ffn2_w_in=v_ffn2_w_in, ffn2_w_out=v_ffn2_w_out, hyb_w_in=v_hyb_w_in,
               hyb_dw_w=v_hyb_dw_w, hyb_dw_b=v_hyb_dw_b, hyb_ln_g=v_hyb_ln_g, hyb_ln_b=v_hyb_ln_b,
               hyb_w_out=v_hyb_w_out, gdn_w_in=v_gdn_w_in, gdn_conv_w=v_gdn_conv_w, gdn_A_log=v_gdn_A_log,
               gdn_dt_bias=v_gdn_dt_bias, gdn_norm_g=v_gdn_norm_g, gdn_w_out=v_gdn_w_out, final_norm=v_final_norm)
    xi, yi, ci = _coords()
    me = 4 * xi + 2 * yi + ci

    big = ["ffn1_w_in", "ffn1_w_out", "ffn2_w_in", "ffn2_w_out", "hyb_w_in", "hyb_w_out", "gdn_w_in", "gdn_w_out"]
    ag_groups, ag_lands = [], []

    def add_group(shards):
        group = []
        for s in shards:
            land = lax.dynamic_update_slice(lax.empty((N_DEV,) + s.shape, s.dtype), s[None], (me,) + (0,) * s.ndim)
            group.append(("gather", None, s, len(ag_lands)))
            ag_lands.append(land)
        ag_groups.append(group)

    for l in range(DEPTH):
        i = l // 2
        add_group([ffn1_w_in[l].astype(BF16), ffn1_w_out[l].astype(BF16)])
        if l % 2 == 0:
            add_group([hyb_w_in[i].astype(BF16), hyb_w_out[i].astype(BF16), hyb_dw_w[i]])
        else:
            add_group([gdn_w_in[i].astype(BF16), gdn_w_out[i].astype(BF16), gdn_conv_w[i]])
        add_group([ffn2_w_in[l].astype(BF16), ffn2_w_out[l].astype(BF16)])
    ag_sems, ag_srcs, ag_lands = push_start(ag_groups, ag_lands, "weights_gather_start")

    def fetch(gi, after):
        group = ag_groups[gi]
        base = sum(len(g) for g in ag_groups[:gi])
        items = [(kind, layer, ag_srcs[base + t], t) for t, (kind, layer, _, _) in enumerate(group)]
        lands = [ag_lands[li] for _, _, _, li in group]
        return push_wait([items], lands, ag_sems[2 * gi:2 * gi + 2], after, f"weights_gather_wait_{gi}")[1]

    row = lambda a: a.reshape(1, -1)

    rope = make_rope(positions)
    h = x[0]
    saved = []
    for l in range(DEPTH):
        i = l // 2
        rec = {"h1": h}
        wi, wo = fetch(3 * l, h)
        rec["w1"] = (wi.reshape(2, FFN_TILES, D, FFN_SHARD), wo)
        h, rec["hn1"], rec["a1"], rec["b1"] = ffn_fwd(h, row(ffn1_norm[l]), *rec["w1"], l, "1")
        mi, mo, mc = fetch(3 * l + 1, h)
        if l % 2 == 0:
            rec["wm"] = (_slabs_to_cols(mi), jnp.pad(_slabs_to_cols(mc), ((0, 1), (0, 0))), mo.reshape(D, D))
            w_in_f, dw_f, w_out_f = rec["wm"]
            h, rec["mix"] = hybrid_fwd(h, row(mix_norm[l]), w_in_f, dw_f, row(hyb_dw_b[i]), row(hyb_ln_g[i]),
                                       row(hyb_ln_b[i]), w_out_f, rope, str(i))
        else:
            rec["wm"] = (_slabs_to_cols(mi), _slabs_to_cols(mc), mo.reshape(D, D))
            w_in_f, cw_f, w_out_f = rec["wm"]
            h, rec["mix"] = gdn_fwd(h, row(mix_norm[l]), w_in_f, cw_f, row(gdn_A_log[i]), row(gdn_dt_bias[i]),
                                    row(gdn_norm_g[i]), w_out_f, str(i))
        rec["h2"] = h
        wi, wo = fetch(3 * l + 2, h)
        rec["w2"] = (wi.reshape(2, FFN_TILES, D, FFN_SHARD), wo)
        h, rec["hn2"], rec["a2"], rec["b2"] = ffn_fwd(h, row(ffn2_norm[l]), *rec["w2"], l, "2")
        saved.append(rec)
    dres, d_final, loss_acc = final_loss(h, row(final_norm), loss_target[0])
    loss = lax.psum(loss_acc[0, 0], ("x", "y", "c"))

    ge_land = {n: lax.empty((N_DEV,) + w[n].shape, BF16) for n in big}
    ge_pending = []

    def send(named, layer, tag):
        lands = [ge_land[n] for n, _ in named]
        group = [("scatter", layer, s, t) for t, (_, s) in enumerate(named)]
        sems, srcs, lands_out = push_start([group], lands, f"grad_send_{tag}")
        for (n, _), land in zip(named, lands_out):
            ge_land[n] = land
        ge_pending.append(([(n, layer, s) for (n, _), s in zip(named, srcs)], sems))

    gsmall = {n: [None] * (DEPTH if n in ("ffn1_norm", "mix_norm", "ffn2_norm") else 2) for n in REPL[:-1]}
    gsmall["hyb_dw_w"] = [None, None]
    gsmall["gdn_conv_w"] = [None, None]
    for l in reversed(range(DEPTH)):
        i = l // 2
        rec = saved[l]
        dhn, dwin, dwout = ffn_bwd(rec["hn2"], rec["a2"], rec["b2"], dres, *rec["w2"], l, "2")
        send([("ffn2_w_in", dwin.reshape(N_DEV, D, FFN_SHARD)), ("ffn2_w_out", dwout.reshape(N_DEV, FFN_SHARD // 2, D))],
             l, f"ffn2_{l}")
        dres, dg = norm_bwd(rec["h2"], row(ffn2_norm[l]), dhn, dres, f"ffn2_norm_bwd_{l}")
        gsmall["ffn2_norm"][l] = dg
        if l % 2 == 0:
            w_in_f, dw_f, w_out_f = rec["wm"]
            dres, gr = hybrid_bwd(dres, rec["mix"], row(mix_norm[l]), w_in_f, dw_f, row(hyb_ln_g[i]),
                                  row(hyb_ln_b[i]), w_out_f, rope, str(i))
            send([("hyb_w_in", _cols_to_slabs(gr["w_in"]).astype(BF16)),
                  ("hyb_w_out", gr["w_out"].reshape(N_DEV, D // N_DEV, D).astype(BF16))], i, f"hyb_{i}")
            for n in ("dw_w", "dw_b", "ln_g", "ln_b"):
                gsmall["hyb_" + n][i] = gr[n]
        else:
            w_in_f, cw_f, w_out_f = rec["wm"]
            dres, gr = gdn_bwd(dres, rec["mix"], row(mix_norm[l]), w_in_f, cw_f, row(gdn_A_log[i]),
                               row(gdn_dt_bias[i]), row(gdn_norm_g[i]), w_out_f, str(i))
            send([("gdn_w_in", _cols_to_slabs(gr["w_in"]).astype(BF16)),
                  ("gdn_w_out", gr["w_out"].reshape(N_DEV, D // N_DEV, D).astype(BF16))], i, f"gdn_{i}")
            for n in ("conv_w", "A_log", "dt_bias", "norm_g"):
                gsmall["gdn_" + n][i] = gr[n]
        gsmall["mix_norm"][l] = gr["norm"]
        dhn, dwin, dwout = ffn_bwd(rec["hn1"], rec["a1"], rec["b1"], dres, *rec["w1"], l, "1")
        send([("ffn1_w_in", dwin.reshape(N_DEV, D, FFN_SHARD)), ("ffn1_w_out", dwout.reshape(N_DEV, FFN_SHARD // 2, D))],
             l, f"ffn1_{l}")
        dres, dg = norm_bwd(rec["h1"], row(ffn1_norm[l]), dhn, dres, f"ffn1_norm_bwd_{l}")
        gsmall["ffn1_norm"][l] = dg
    grad_x = dres[None]

    n_repl_rows = 136
    small_rows = 576
    repl_flat = jnp.concatenate([jnp.concatenate([a.reshape(-1) for a in gsmall[n]]) for n in REPL[:-1]]
                                + [d_final.reshape(-1)])
    repl_pack = jnp.pad(repl_flat, (0, n_repl_rows * 128 - repl_flat.shape[0]))
    small_pack = jnp.concatenate([repl_pack] + [a.reshape(-1) for a in gsmall["hyb_dw_w"]]
                                 + [a.reshape(-1) for a in gsmall["gdn_conv_w"]]).reshape(small_rows, 128)
    small_all, = all_gather([small_pack], "small_grads_all_gather")
    g_small = sum_slabs(small_all, "small_grads_sum")

    groups = [[("scatter", layer, s, big.index(n)) for n, layer, s in named] for named, _ in ge_pending]
    sems = [s for _, pair in ge_pending for s in pair]
    srcs_out, lands_out = push_wait(groups, [ge_land[n] for n in big], sems, dres, "grad_wait")
    own = {n: {} for n in big}
    flat_named = [it for named, _ in ge_pending for it in named]
    for (n, layer, _), s in zip(flat_named, srcs_out):
        own[n][layer] = lax.dynamic_index_in_dim(s, me, 0, keepdims=False)
    recv = {}
    for n, land in zip(big, lands_out):
        mine = jnp.stack([own[n][k] for k in range(len(own[n]))])
        recv[n] = lax.dynamic_update_slice(land, mine[None], (me,) + (0,) * mine.ndim)

    out = {}
    for n in big:
        out[n] = _adamw_nd(recv[n], w[n], mom[n], var[n], f"adamw_{n}")
    pk = lambda d: _pack([d[n] for n in REPL], n_repl_rows)
    res = adamw(g_small[:n_repl_rows][None], pk(w), pk(mom), pk(var), "adamw_replicated")
    off = 0
    for n in REPL:
        sz = w[n].size
        out[n] = [r.reshape(-1)[off:off + sz].reshape(w[n].shape) for r in res]
        off += sz
    g_dw = g_small[n_repl_rows:n_repl_rows + 248].reshape(2, CONV_K, CONV_C)
    g_dw = lax.dynamic_slice_in_dim(g_dw, me * (CONV_C // N_DEV), CONV_C // N_DEV, axis=2)
    out["hyb_dw_w"] = _adamw_nd(g_dw[None], w["hyb_dw_w"], mom["hyb_dw_w"], var["hyb_dw_w"], "adamw_hyb_dw_w")
    g_cw = g_small[n_repl_rows + 248:].reshape(2, G_CONV, G_QKV)
    g_cw = lax.dynamic_slice_in_dim(g_cw, me * (G_QKV // N_DEV), G_QKV // N_DEV, axis=2)
    out["gdn_conv_w"] = _adamw_nd(g_cw[None], w["gdn_conv_w"], mom["gdn_conv_w"], var["gdn_conv_w"], "adamw_gdn_conv_w")

    return (loss, grad_x, *[out[n][0] for n in WEIGHTS], *[out[n][1] for n in WEIGHTS],
            *[out[n][2] for n in WEIGHTS], *[out[n][3] for n in WEIGHTS])
```

```python
import functools
import math

import jax
import jax.numpy as jnp
import numpy as np
from jax import lax
from jax.experimental import pallas as pl
from jax.experimental.pallas import tpu as pltpu

F32 = jnp.float32
BF16 = jnp.bfloat16

N_DEV = 8
T = 4096
D = 1024
DEPTH = 4
FFN = 2816
FFN_SHARD = 2 * FFN // N_DEV
FFN_TILES = FFN // FFN_SHARD
EPS = 1e-6

A_HEADS = 8
A_HD = 64
A_W = 512
CONV_C = 512
CONV_K = 31
HYB_IN = 2560
ROPE_THETA = 500000.0
ROT = 16
DILATIONS = (1, 4, 16)
BLK = 128
KPAD = 2048

G_HEADS = 8
G_DK = 128
G_QKV = 3072
G_IN = 4112
G_CONV = 4
CH = 64
GRP = 512
CPG = GRP // CH
N_GRP = T // GRP

ADAM_LR = 0.001
ADAM_B1 = 0.9
ADAM_B2 = 0.999
ADAM_EPS = 1e-08
ADAM_WD = 0.01
ADAM_STEP = 10

VMEM_LIMIT = 56 * 1024 * 1024

HI = lax.Precision.HIGHEST


def _cp(*sem):
    return pltpu.CompilerParams(dimension_semantics=sem, vmem_limit_bytes=VMEM_LIMIT)


def _dot(a, b):
    return jnp.dot(a, b, preferred_element_type=F32)


def _dot_nt(a, b):
    return lax.dot_general(a, b, (((1,), (1,)), ((), ())), preferred_element_type=F32)


def _dot_tn(a, b):
    return lax.dot_general(a, b, (((0,), (0,)), ((), ())), preferred_element_type=F32)


def _sigmoid(x):
    return 1.0 / (1.0 + jnp.exp(-x))


def _dsilu(x, sig):
    return sig * (1.0 + x * (1.0 - sig))


def _rms(x, g):
    rstd = lax.rsqrt(jnp.mean(x * x, axis=-1, keepdims=True) + EPS)
    return x * rstd * g


FFN_TT = 512


def ffn_fwd(h, g_row, w_in, w_out, layer, tag=""):
    def body(h_ref, g_ref, win_ref, wout_ref, hnew_ref, hn_ref, a_ref, b_ref, acc_ref, hns_ref):
        j = pl.program_id(1)

        @pl.when(j == 0)
        def _():
            hn = _rms(h_ref[...], g_ref[...]).astype(BF16)
            hns_ref[...] = hn
            hn_ref[...] = hn
            acc_ref[...] = jnp.zeros_like(acc_ref)

        hn = hns_ref[...]
        a = _dot(hn, win_ref[0])
        b = _dot(hn, win_ref[1])
        act = a * _sigmoid(a) * b
        acc_ref[...] += _dot(act.astype(BF16), wout_ref[...].reshape(FFN_SHARD, D))
        a_ref[...] = a.astype(BF16)
        b_ref[...] = b.astype(BF16)

        @pl.when(j == FFN_TILES - 1)
        def _():
            hnew_ref[...] = h_ref[...] + 0.5 * acc_ref[...]

    tt = FFN_TT
    return pl.pallas_call(
        body,
        grid=(T // tt, FFN_TILES),
        in_specs=[
            pl.BlockSpec((tt, D), lambda i, j: (i, 0)),
            pl.BlockSpec((1, D), lambda i, j: (0, 0)),
            pl.BlockSpec((2, None, D, FFN_SHARD), lambda i, j: (0, j, 0, 0)),
            pl.BlockSpec((2, FFN_SHARD // 2, D), lambda i, j: (j, 0, 0)),
        ],
        out_specs=[
            pl.BlockSpec((tt, D), lambda i, j: (i, 0)),
            pl.BlockSpec((tt, D), lambda i, j: (i, 0)),
            pl.BlockSpec((None, tt, FFN_SHARD), lambda i, j: (j, i, 0)),
            pl.BlockSpec((None, tt, FFN_SHARD), lambda i, j: (j, i, 0)),
        ],
        out_shape=[
            jax.ShapeDtypeStruct((T, D), F32),
            jax.ShapeDtypeStruct((T, D), BF16),
            jax.ShapeDtypeStruct((FFN_TILES, T, FFN_SHARD), BF16),
            jax.ShapeDtypeStruct((FFN_TILES, T, FFN_SHARD), BF16),
        ],
        scratch_shapes=[pltpu.VMEM((tt, D), F32), pltpu.VMEM((tt, D), BF16)],
        compiler_params=_cp("parallel", "arbitrary"),
        name=f"ffn{tag}_fwd_{layer}",
    )(h, g_row, w_in, w_out)


def ffn_bwd(hn, a, b, dres, w_in, w_out, layer, tag=""):
    tt = FFN_TT
    nt = T // tt

    def body(hn_ref, a_ref, b_ref, dres_ref, win_ref, wout_ref, dhn_ref, dwin_ref, dwout_ref, gin_ref, gout_ref):
        i = pl.program_id(1)
        do = (0.5 * dres_ref[...]).astype(BF16)
        wo = wout_ref[...].reshape(FFN_SHARD, D)
        dact = _dot_nt(do, wo)
        a = a_ref[...].astype(F32)
        b = b_ref[...].astype(F32)
        sig = _sigmoid(a)
        s = a * sig
        act = (s * b).astype(BF16)
        db = (dact * s).astype(BF16)
        da = (dact * b * _dsilu(a, sig)).astype(BF16)
        hn = hn_ref[...]
        gwo = _dot_tn(act, do)
        gwg = _dot_tn(hn, da)
        gwu = _dot_tn(hn, db)

        @pl.when(i == 0)
        def _():
            gout_ref[...] = gwo
            gin_ref[0] = gwg
            gin_ref[1] = gwu

        @pl.when(i > 0)
        def _():
            gout_ref[...] += gwo
            gin_ref[0] += gwg
            gin_ref[1] += gwu

        dhn_ref[...] = _dot_nt(da, win_ref[0]) + _dot_nt(db, win_ref[1])

        @pl.when(i == nt - 1)
        def _():
            dwin_ref[...] = gin_ref[...].astype(BF16)
            dwout_ref[...] = gout_ref[...].astype(BF16)

    return pl.pallas_call(
        body,
        grid=(FFN_TILES, nt),
        in_specs=[
            pl.BlockSpec((tt, D), lambda j, i: (i, 0)),
            pl.BlockSpec((None, tt, FFN_SHARD), lambda j, i: (j, i, 0)),
            pl.BlockSpec((None, tt, FFN_SHARD), lambda j, i: (j, i, 0)),
            pl.BlockSpec((tt, D), lambda j, i: (i, 0)),
            pl.BlockSpec((2, None, D, FFN_SHARD), lambda j, i: (0, j, 0, 0)),
            pl.BlockSpec((2, FFN_SHARD // 2, D), lambda j, i: (j, 0, 0)),
        ],
        out_specs=[
            pl.BlockSpec((None, tt, D), lambda j, i: (j, i, 0)),
            pl.BlockSpec((2, None, D, FFN_SHARD), lambda j, i: (0, j, 0, 0)),
            pl.BlockSpec((None, FFN_SHARD, D), lambda j, i: (j, 0, 0)),
        ],
        out_shape=[
            jax.ShapeDtypeStruct((FFN_TILES, T, D), F32),
            jax.ShapeDtypeStruct((2, FFN_TILES, D, FFN_SHARD), BF16),
            jax.ShapeDtypeStruct((FFN_TILES, FFN_SHARD, D), BF16),
        ],
        scratch_shapes=[pltpu.VMEM((2, D, FFN_SHARD), F32), pltpu.VMEM((FFN_SHARD, D), F32)],
        compiler_params=_cp("parallel", "arbitrary"),
        name=f"ffn{tag}_bwd_{layer}",
    )(hn, a, b, dres, w_in, w_out)


def _rms_bwd(x, g, dy):
    rstd = lax.rsqrt(jnp.mean(x * x, axis=-1, keepdims=True) + EPS)
    xh = x * rstd
    u = dy * g
    dx = rstd * (u - xh * jnp.mean(u * xh, axis=-1, keepdims=True))
    return dx, jnp.sum(dy * xh, axis=0, keepdims=True)


def norm_bwd(x, g_row, dy_parts, dres, name):
    p = dy_parts.shape[0]
    tt = 512

    def body(x_ref, g_ref, dy_ref, dres_ref, out_ref, dg_ref):
        i = pl.program_id(0)
        dy = dy_ref[0]
        for q in range(1, p):
            dy = dy + dy_ref[q]
        dx, dg = _rms_bwd(x_ref[...], g_ref[...], dy)
        out_ref[...] = dres_ref[...] + dx

        @pl.when(i == 0)
        def _():
            dg_ref[...] = dg

        @pl.when(i > 0)
        def _():
            dg_ref[...] += dg

    return pl.pallas_call(
        body,
        grid=(T // tt,),
        in_specs=[
            pl.BlockSpec((tt, D), lambda i: (i, 0)),
            pl.BlockSpec((1, D), lambda i: (0, 0)),
            pl.BlockSpec((p, tt, D), lambda i: (0, i, 0)),
            pl.BlockSpec((tt, D), lambda i: (i, 0)),
        ],
        out_specs=[pl.BlockSpec((tt, D), lambda i: (i, 0)), pl.BlockSpec((1, D), lambda i: (0, 0))],
        out_shape=[jax.ShapeDtypeStruct((T, D), F32), jax.ShapeDtypeStruct((1, D), F32)],
        compiler_params=_cp("arbitrary"),
        name=name,
    )(x, g_row, dy_parts, dres)


def final_loss(h, g_row, target):
    tt = 512

    def body(h_ref, g_ref, t_ref, dres_ref, dg_ref, loss_ref):
        i = pl.program_id(0)
        x = h_ref[...]
        g = g_ref[...]
        err = _rms(x, g) - t_ref[...]
        part = 0.5 * jnp.sum(jnp.mean(err * err, axis=-1, keepdims=True), axis=0, keepdims=True)
        dx, dg = _rms_bwd(x, g, err * (1.0 / D))
        dres_ref[...] = dx
        part = jnp.broadcast_to(part, loss_ref.shape)

        @pl.when(i == 0)
        def _():
            dg_ref[...] = dg
            loss_ref[...] = part

        @pl.when(i > 0)
        def _():
            dg_ref[...] += dg
            loss_ref[...] += part

    return pl.pallas_call(
        body,
        grid=(T // tt,),
        in_specs=[
            pl.BlockSpec((tt, D), lambda i: (i, 0)),
            pl.BlockSpec((1, D), lambda i: (0, 0)),
            pl.BlockSpec((tt, D), lambda i: (i, 0)),
        ],
        out_specs=[
            pl.BlockSpec((tt, D), lambda i: (i, 0)),
            pl.BlockSpec((1, D), lambda i: (0, 0)),
            pl.BlockSpec((8, 128), lambda i: (0, 0)),
        ],
        out_shape=[
            jax.ShapeDtypeStruct((T, D), F32),
            jax.ShapeDtypeStruct((1, D), F32),
            jax.ShapeDtypeStruct((8, 128), F32),
        ],
        compiler_params=_cp("arbitrary"),
        name="final_loss",
    )(h, g_row, target)


PROJ_TT = 256


def rope_tables(pos_col, invf_row):
    tt = 512

    def body(p_ref, f_ref, c_ref, sm_ref, sp_ref):
        ang = p_ref[...].astype(F32) * f_ref[...]
        lane = lax.broadcasted_iota(jnp.int32, ang.shape, 1) % A_HD
        cs = jnp.cos(ang)
        sn = jnp.sin(ang)
        c_ref[...] = jnp.where(lane < ROT, cs, 1.0)
        sm_ref[...] = jnp.where(lane < ROT // 2, -sn, 0.0)
        sp_ref[...] = jnp.where((lane >= ROT // 2) & (lane < ROT), sn, 0.0)

    spec = pl.BlockSpec((tt, 128), lambda i: (i, 0))
    return pl.pallas_call(
        body,
        grid=(T // tt,),
        in_specs=[pl.BlockSpec((tt, 1), lambda i: (i, 0)), pl.BlockSpec((1, 128), lambda i: (0, 0))],
        out_specs=[spec, spec, spec],
        out_shape=[jax.ShapeDtypeStruct((T, 128), F32)] * 3,
        compiler_params=_cp("parallel"),
        name="rope_tables",
    )(pos_col, invf_row)


def make_rope(positions):
    inv_freq = jnp.power(jnp.float32(ROPE_THETA), -jnp.arange(0, ROT, 2, dtype=F32) / ROT)
    per_head = jnp.concatenate([inv_freq, inv_freq, jnp.zeros((A_HD - ROT,), F32)])
    invf_row = jnp.tile(per_head, 2)[None, :]
    return tuple(rope_tables(positions.reshape(T, 1), invf_row))


def _rope(x, c, sm, sp):
    return x * c + pltpu.roll(x, 128 - ROT // 2, 1) * sm + pltpu.roll(x, ROT // 2, 1) * sp


def _rope_t(dy, c, sm, sp):
    return dy * c + pltpu.roll(dy * sm, ROT // 2, 1) + pltpu.roll(dy * sp, 128 - ROT // 2, 1)


def proj_fwd(h, g_row, w, splits, name, rope=None):
    tt = PROJ_TT
    n = w.shape[1]
    n_rope = 0 if rope is None else 3

    def body(h_ref, g_ref, w_ref, *rest):
        tabs = rest[:n_rope]
        hn_ref = rest[n_rope]
        outs = rest[n_rope + 1:]
        hn = _rms(h_ref[...], g_ref[...]).astype(BF16)
        hn_ref[...] = hn
        for k, ((st, wd), o_ref) in enumerate(zip(splits, outs)):
            if rope is not None and k == 0:
                c, sm, sp = (t[...] for t in tabs)
                for gi in range(wd // 128):
                    r = _dot(hn, w_ref[:, st + 128 * gi:st + 128 * (gi + 1)])
                    if gi < 2 * A_W // 128:
                        r = _rope(r, c, sm, sp)
                    o_ref[:, 128 * gi:128 * (gi + 1)] = r
            else:
                o_ref[...] = _dot(hn, w_ref[:, st:st + wd])

    tab_specs = [pl.BlockSpec((tt, 128), lambda i: (i, 0))] * n_rope
    return pl.pallas_call(
        body,
        grid=(T // tt,),
        in_specs=[
            pl.BlockSpec((tt, D), lambda i: (i, 0)),
            pl.BlockSpec((1, D), lambda i: (0, 0)),
            pl.BlockSpec((D, n), lambda i: (0, 0)),
        ] + tab_specs,
        out_specs=[pl.BlockSpec((tt, D), lambda i: (i, 0))]
        + [pl.BlockSpec((tt, wd), lambda i: (i, 0)) for _, wd in splits],
        out_shape=[jax.ShapeDtypeStruct((T, D), BF16)]
        + [jax.ShapeDtypeStruct((T, wd), F32) for _, wd in splits],
        compiler_params=_cp("parallel"),
        name=name,
    )(h, g_row, w, *(rope or ()))


def proj_bwd_data(x, g_row, w, dparts, splits, dres, name, rope=None, n_rot=0):
    tt = PROJ_TT
    n = w.shape[1]
    n_rope = 0 if rope is None else 3
    k_parts = len(dparts)

    def body(x_ref, g_ref, w_ref, dres_ref, *rest):
        d_refs = rest[:k_parts]
        tabs = rest[k_parts:k_parts + n_rope]
        out_ref, dg_ref = rest[k_parts + n_rope:k_parts + n_rope + 2]
        unrot_refs = rest[k_parts + n_rope + 2:]
        i = pl.program_id(0)
        dhn = jnp.zeros((tt, D), F32)
        for k, ((st, wd), d_ref) in enumerate(zip(splits, d_refs)):
            if k < n_rot:
                c, sm, sp = (t[...] for t in tabs)
                for gi in range(wd // 128):
                    d = _rope_t(d_ref[:, 128 * gi:128 * (gi + 1)], c, sm, sp)
                    unrot_refs[k][:, 128 * gi:128 * (gi + 1)] = d
                    dhn = dhn + _dot_nt(d.astype(BF16), w_ref[:, st + 128 * gi:st + 128 * (gi + 1)])
            else:
                dhn = dhn + _dot_nt(d_ref[...].astype(BF16), w_ref[:, st:st + wd])
        dx, dg = _rms_bwd(x_ref[...], g_ref[...], dhn)
        out_ref[...] = dres_ref[...] + dx

        @pl.when(i == 0)
        def _():
            dg_ref[...] = dg

        @pl.when(i > 0)
        def _():
            dg_ref[...] += dg

    tab_specs = [pl.BlockSpec((tt, 128), lambda i: (i, 0))] * n_rope
    out_specs = [pl.BlockSpec((tt, D), lambda i: (i, 0)), pl.BlockSpec((1, D), lambda i: (0, 0))]
    out_shape = [jax.ShapeDtypeStruct((T, D), F32), jax.ShapeDtypeStruct((1, D), F32)]
    for k in range(n_rot):
        out_specs.append(pl.BlockSpec((tt, splits[k][1]), lambda i: (i, 0)))
        out_shape.append(jax.ShapeDtypeStruct((T, splits[k][1]), F32))
    return pl.pallas_call(
        body,
        grid=(T // tt,),
        in_specs=[
            pl.BlockSpec((tt, D), lambda i: (i, 0)),
            pl.BlockSpec((1, D), lambda i: (0, 0)),
            pl.BlockSpec((D, n), lambda i: (0, 0)),
            pl.BlockSpec((tt, D), lambda i: (i, 0)),
        ] + [pl.BlockSpec((tt, wd), lambda i: (i, 0)) for _, wd in splits] + tab_specs,
        out_specs=out_specs,
        out_shape=out_shape,
        compiler_params=_cp("arbitrary"),
        name=name,
    )(x, g_row, w, dres, *dparts, *(rope or ()))


def mm_tn(x, d, name):
    k = x.shape[1]
    n = d.shape[1]
    wn = n if n <= 512 else 512
    tt = 512

    def body(x_ref, d_ref, o_ref):
        i = pl.program_id(1)
        r = _dot_tn(x_ref[...], d_ref[...].astype(BF16))

        @pl.when(i == 0)
        def _():
            o_ref[...] = r

        @pl.when(i > 0)
        def _():
            o_ref[...] += r

    return pl.pallas_call(
        body,
        grid=(n // wn, T // tt),
        in_specs=[pl.BlockSpec((tt, k), lambda j, i: (i, 0)), pl.BlockSpec((tt, wn), lambda j, i: (i, j))],
        out_specs=pl.BlockSpec((k, wn), lambda j, i: (0, j)),
        out_shape=jax.ShapeDtypeStruct((k, n), F32),
        compiler_params=_cp("parallel", "arbitrary"),
        name=name,
    )(x, d)


CONV_RC = 128
CONV_PAD = 32


def hyb_conv_fwd(u, dw_w, dw_b, name):
    def body(ua_ref, ug_ref, w_ref, b_ref, o_ref, xpad):
        xpad[0:CONV_PAD, :] = jnp.zeros((CONV_PAD, 128), F32)
        xpad[CONV_PAD:, :] = ua_ref[...] * _sigmoid(ug_ref[...])
        for r in range(T // CONV_RC):
            acc = jnp.broadcast_to(b_ref[...], (CONV_RC, 128))
            for j in range(CONV_K):
                acc = acc + w_ref[pl.ds(j, 1), :] * xpad[pl.ds(r * CONV_RC + CONV_PAD - (CONV_K - 1) + j, CONV_RC), :]
            o_ref[r * CONV_RC:(r + 1) * CONV_RC, :] = acc

    nb = CONV_C // 128
    return pl.pallas_call(
        body,
        grid=(nb,),
        in_specs=[
            pl.BlockSpec((T, 128), lambda c: (0, c)),
            pl.BlockSpec((T, 128), lambda c: (0, nb + c)),
            pl.BlockSpec((32, 128), lambda c: (0, c)),
            pl.BlockSpec((1, 128), lambda c: (0, c)),
        ],
        out_specs=pl.BlockSpec((T, 128), lambda c: (0, c)),
        out_shape=jax.ShapeDtypeStruct((T, CONV_C), F32),
        scratch_shapes=[pltpu.VMEM((T + CONV_PAD, 128), F32)],
        compiler_params=_cp("parallel"),
        name=name,
    )(u, u, dw_w, dw_b)


def hyb_conv_bwd(dc, u, dw_w, name):
    def body(dc_ref, ua_ref, ug_ref, w_ref, da_ref, dgate_ref, dw_ref, db_ref, xpad, dcpad, dwacc):
        ua = ua_ref[...]
        sig = _sigmoid(ug_ref[...])
        xpad[0:CONV_PAD, :] = jnp.zeros((CONV_PAD, 128), F32)
        xpad[CONV_PAD:, :] = ua * sig
        dcpad[0:T, :] = dc_ref[...]
        dcpad[T:, :] = jnp.zeros((CONV_PAD, 128), F32)
        dwacc[...] = jnp.zeros_like(dwacc)
        dbacc = jnp.zeros((8, 128), F32)
        for r in range(T // CONV_RC):
            r0 = r * CONV_RC
            dcr = dc_ref[r0:r0 + CONV_RC, :]
            dbacc = dbacc + dcr.reshape(CONV_RC // 8, 8, 128).sum(axis=0)
            dglu = jnp.zeros((CONV_RC, 128), F32)
            for j in range(CONV_K):
                dglu = dglu + w_ref[pl.ds(j, 1), :] * dcpad[pl.ds(r0 + (CONV_K - 1) - j, CONV_RC), :]
                prod = dcr * xpad[pl.ds(r0 + CONV_PAD - (CONV_K - 1) + j, CONV_RC), :]
                dwacc[8 * j:8 * j + 8, :] += prod.reshape(CONV_RC // 8, 8, 128).sum(axis=0)
            sg = sig[r0:r0 + CONV_RC, :]
            da_ref[r0:r0 + CONV_RC, :] = dglu * sg
            dgate_ref[r0:r0 + CONV_RC, :] = dglu * ua[r0:r0 + CONV_RC, :] * sg * (1.0 - sg)
        for j in range(CONV_K):
            dw_ref[pl.ds(j, 1), :] = jnp.sum(dwacc[8 * j:8 * j + 8, :], axis=0, keepdims=True)
        dw_ref[pl.ds(CONV_K, 1), :] = jnp.zeros((1, 128), F32)
        db_ref[...] = jnp.sum(dbacc, axis=0, keepdims=True)

    nb = CONV_C // 128
    col = pl.BlockSpec((T, 128), lambda c: (0, c))
    return pl.pallas_call(
        body,
        grid=(nb,),
        in_specs=[col, col, pl.BlockSpec((T, 128), lambda c: (0, nb + c)), pl.BlockSpec((32, 128), lambda c: (0, c))],
        out_specs=[col, col, pl.BlockSpec((32, 128), lambda c: (0, c)), pl.BlockSpec((1, 128), lambda c: (0, c))],
        out_shape=[
            jax.ShapeDtypeStruct((T, CONV_C), F32),
            jax.ShapeDtypeStruct((T, CONV_C), F32),
            jax.ShapeDtypeStruct((32, CONV_C), F32),
            jax.ShapeDtypeStruct((1, CONV_C), F32),
        ],
        scratch_shapes=[
            pltpu.VMEM((T + CONV_PAD, 128), F32),
            pltpu.VMEM((T + CONV_PAD, 128), F32),
            pltpu.VMEM((8 * 32, 128), F32),
        ],
        compiler_params=_cp("parallel"),
        name=name,
    )(dc, u, u, dw_w)


ATT_SCALE = A_HD ** -0.5
N_BLK = T // BLK


def _att_masks():
    i = lax.broadcasted_iota(jnp.int32, (BLK, 2 * BLK), 0)
    j = lax.broadcasted_iota(jnp.int32, (BLK, 2 * BLK), 1)
    band = (j >= i) & (j <= i + BLK)
    i1 = lax.broadcasted_iota(jnp.int32, (BLK, BLK), 0)
    j1 = lax.broadcasted_iota(jnp.int32, (BLK, BLK), 1)
    return band, j1 <= i1


def _att_rows(d, t, first):
    if first:
        base = t
        return pl.ds(base, BLK, stride=d), pl.ds(base, BLK, stride=d)
    c = t % d
    n = t // d + 1
    base = c + (BLK * d) * n
    return pl.ds(base, BLK, stride=d), pl.ds(base - BLK * d, 2 * BLK, stride=d)


def attn_fwd(qkv, name):
    def body(q_ref, k_ref, v_ref, o_ref, lse_ref, og, lg):
        band, tri = _att_masks()
        head0 = lax.broadcasted_iota(jnp.int32, (BLK, 128), 1) < A_HD
        for g, d in enumerate(DILATIONS):
            def block(t, carry, first, g=g, d=d):
                rq, rk = _att_rows(d, t, first)
                q2 = q_ref[rq, :]
                k2 = k_ref[rk, :].astype(BF16)
                v2 = v_ref[rk, :].astype(BF16)
                o_e, l_e = [], []
                for e in range(2):
                    qe = jnp.where(head0 if e == 0 else ~head0, q2, 0.0).astype(BF16)
                    s = _dot_nt(qe, k2) * ATT_SCALE
                    s = jnp.where(tri if first else band, s, -jnp.inf)
                    m = jnp.max(s, axis=-1, keepdims=True)
                    p = jnp.exp(s - m)
                    den = jnp.sum(p, axis=-1, keepdims=True)
                    o_e.append(_dot(p.astype(BF16), v2) / den)
                    l_e.append(m + jnp.log(den))
                og[g, rq, :] = jnp.where(head0, o_e[0], o_e[1])
                lg[g, rq, :] = jnp.where(head0, l_e[0], l_e[1])
                return carry

            lax.fori_loop(0, d, functools.partial(block, first=True), 0)
            lax.fori_loop(0, N_BLK - d, functools.partial(block, first=False), 0)
        rc = 256
        for r in range(T // rc):
            rows = pl.ds(r * rc, rc)
            l0, l1, l2 = lg[0, rows, :], lg[1, rows, :], lg[2, rows, :]
            m = jnp.maximum(jnp.maximum(l0, l1), l2)
            e0, e1, e2 = jnp.exp(l0 - m), jnp.exp(l1 - m), jnp.exp(l2 - m)
            z = e0 + e1 + e2
            o_ref[rows, :] = (e0 / z) * og[0, rows, :] + (e1 / z) * og[1, rows, :] + (e2 / z) * og[2, rows, :]
            lse_ref[rows, :] = m + jnp.log(z)

    npair = A_HEADS // 2
    col = lambda off: pl.BlockSpec((T, 128), lambda p: (0, off + p))
    return pl.pallas_call(
        body,
        grid=(npair,),
        in_specs=[col(0), col(npair), col(2 * npair)],
        out_specs=[col(0), col(0)],
        out_shape=[jax.ShapeDtypeStruct((T, A_W), F32), jax.ShapeDtypeStruct((T, A_W), F32)],
        scratch_shapes=[pltpu.VMEM((3, T, 128), F32), pltpu.VMEM((3, T, 128), F32)],
        compiler_params=_cp("parallel"),
        name=name,
    )(qkv, qkv, qkv)


def attn_bwd(qkv, o, lse, do, name):
    def body(q_ref, k_ref, v_ref, o_ref, lse_ref, do_ref, dq_ref, dk_ref, dv_ref):
        band, tri = _att_masks()
        head0 = lax.broadcasted_iota(jnp.int32, (BLK, 128), 1) < A_HD
        head0k = lax.broadcasted_iota(jnp.int32, (2 * BLK, 128), 1) < A_HD
        dq_ref[...] = jnp.zeros_like(dq_ref)
        dk_ref[...] = jnp.zeros_like(dk_ref)
        dv_ref[...] = jnp.zeros_like(dv_ref)
        for d in DILATIONS:
            def block(t, carry, first, d=d):
                rq, rk = _att_rows(d, t, first)
                q2 = q_ref[rq, :]
                k2 = k_ref[rk, :].astype(BF16)
                v2 = v_ref[rk, :].astype(BF16)
                do2 = do_ref[rq, :]
                l2 = lse_ref[rq, :]
                prod = do2 * o_ref[rq, :]
                q2b = q2.astype(BF16)
                do2b = do2.astype(BF16)
                dq_e, dk_e, dv_e = [], [], []
                for e in range(2):
                    he = head0 if e == 0 else ~head0
                    qe = jnp.where(he, q2, 0.0).astype(BF16)
                    doe = jnp.where(he, do2, 0.0).astype(BF16)
                    l = l2[:, A_HD * e:A_HD * e + 1]
                    dd = jnp.sum(jnp.where(he, prod, 0.0), axis=-1, keepdims=True)
                    s = _dot_nt(qe, k2) * ATT_SCALE
                    p = jnp.where(tri if first else band, jnp.exp(s - l), 0.0)
                    dp = _dot_nt(doe, v2)
                    ds = (p * (dp - dd) * ATT_SCALE).astype(BF16)
                    dq_e.append(_dot(ds, k2))
                    dk_e.append(_dot_tn(ds, q2b))
                    dv_e.append(_dot_tn(p.astype(BF16), do2b))
                hk = head0 if first else head0k
                dq_ref[rq, :] += jnp.where(head0, dq_e[0], dq_e[1])
                dk_ref[rk, :] += jnp.where(hk, dk_e[0], dk_e[1])
                dv_ref[rk, :] += jnp.where(hk, dv_e[0], dv_e[1])
                return carry

            lax.fori_loop(0, d, functools.partial(block, first=True), 0)
            lax.fori_loop(0, N_BLK - d, functools.partial(block, first=False), 0)

    npair = A_HEADS // 2
    col = lambda off: pl.BlockSpec((T, 128), lambda p: (0, off + p))
    return pl.pallas_call(
        body,
        grid=(npair,),
        in_specs=[col(0), col(npair), col(2 * npair), col(0), col(0), col(0)],
        out_specs=[col(0), col(0), col(0)],
        out_shape=[jax.ShapeDtypeStruct((T, A_W), F32)] * 3,
        compiler_params=_cp("parallel"),
        name=name,
    )(qkv, qkv, qkv, o, lse, do)


def _ln_silu(x, g, b):
    mu = jnp.mean(x, axis=-1, keepdims=True)
    xc = x - mu
    rstd = lax.rsqrt(jnp.mean(xc * xc, axis=-1, keepdims=True) + EPS)
    xh = xc * rstd
    y = xh * g + b
    sig = _sigmoid(y)
    return y * sig, (xh, rstd, y, sig)


def hyb_out_fwd(h, attn, cpre, ln_g, ln_b, w_out, name):
    tt = 512

    def body(h_ref, a_ref, c_ref, g_ref, b_ref, w_ref, hnew_ref, cat_ref):
        cn, _ = _ln_silu(c_ref[...], g_ref[...], b_ref[...])
        ab = a_ref[...].astype(BF16)
        cb = cn.astype(BF16)
        cat_ref[:, 0:A_W] = ab
        cat_ref[:, A_W:D] = cb
        hnew_ref[...] = h_ref[...] + _dot(ab, w_ref[0:A_W, :]) + _dot(cb, w_ref[A_W:D, :])

    half = pl.BlockSpec((tt, A_W), lambda i: (i, 0))
    vec = pl.BlockSpec((1, CONV_C), lambda i: (0, 0))
    full = pl.BlockSpec((tt, D), lambda i: (i, 0))
    return pl.pallas_call(
        body,
        grid=(T // tt,),
        in_specs=[full, half, half, vec, vec, pl.BlockSpec((D, D), lambda i: (0, 0))],
        out_specs=[full, full],
        out_shape=[jax.ShapeDtypeStruct((T, D), F32), jax.ShapeDtypeStruct((T, D), BF16)],
        compiler_params=_cp("parallel"),
        name=name,
    )(h, attn, cpre, ln_g, ln_b, w_out)


def hyb_out_bwd(dres, cpre, ln_g, ln_b, w_out, name):
    tt = 512

    def body(d_ref, c_ref, g_ref, b_ref, w_ref, da_ref, dc_ref, dg_ref, db_ref):
        i = pl.program_id(0)
        db16 = d_ref[...].astype(BF16)
        da_ref[...] = _dot_nt(db16, w_ref[0:A_W, :])
        dcn = _dot_nt(db16, w_ref[A_W:D, :])
        g = g_ref[...]
        _, (xh, rstd, y, sig) = _ln_silu(c_ref[...], g, b_ref[...])
        dy = dcn * _dsilu(y, sig)
        dxh = dy * g
        dc_ref[...] = rstd * (dxh - jnp.mean(dxh, axis=-1, keepdims=True)
                              - xh * jnp.mean(dxh * xh, axis=-1, keepdims=True))
        dg = jnp.sum(dy * xh, axis=0, keepdims=True)
        db = jnp.sum(dy, axis=0, keepdims=True)

        @pl.when(i == 0)
        def _():
            dg_ref[...] = dg
            db_ref[...] = db

        @pl.when(i > 0)
        def _():
            dg_ref[...] += dg
            db_ref[...] += db

    half = pl.BlockSpec((tt, A_W), lambda i: (i, 0))
    vec = pl.BlockSpec((1, CONV_C), lambda i: (0, 0))
    return pl.pallas_call(
        body,
        grid=(T // tt,),
        in_specs=[pl.BlockSpec((tt, D), lambda i: (i, 0)), half, vec, vec, pl.BlockSpec((D, D), lambda i: (0, 0))],
        out_specs=[half, half, vec, vec],
        out_shape=[
            jax.ShapeDtypeStruct((T, A_W), F32),
            jax.ShapeDtypeStruct((T, CONV_C), F32),
            jax.ShapeDtypeStruct((1, CONV_C), F32),
            jax.ShapeDtypeStruct((1, CONV_C), F32),
        ],
        compiler_params=_cp("arbitrary"),
        name=name,
    )(dres, cpre, ln_g, ln_b, w_out)


def hybrid_fwd(h, g_row, w_in, dw_w, dw_b, ln_g, ln_b, w_out, rope, tag):
    hn, qkv, u = proj_fwd(h, g_row, w_in, [(0, 3 * A_W), (3 * A_W, 2 * CONV_C)], f"hyb_proj_{tag}", rope=rope)
    cpre = hyb_conv_fwd(u, dw_w, dw_b, f"hyb_conv_{tag}")
    attn, lse = attn_fwd(qkv, f"attn_fwd_{tag}")
    hnew, cat = hyb_out_fwd(h, attn, cpre, ln_g, ln_b, w_out, f"hyb_out_{tag}")
    return hnew, (h, hn, qkv, u, cpre, attn, lse, cat)


def hybrid_bwd(dres, saved, g_row, w_in, dw_w, ln_g, ln_b, w_out, rope, tag):
    h, hn, qkv, u, cpre, attn, lse, cat = saved
    d_attn, d_cpre, d_lng, d_lnb = hyb_out_bwd(dres, cpre, ln_g, ln_b, w_out, f"hyb_out_bwd_{tag}")
    d_wout = mm_tn(cat, dres, f"hyb_wout_grad_{tag}")
    d_a, d_gate, d_dw, d_db = hyb_conv_bwd(d_cpre, u, dw_w, f"hyb_conv_bwd_{tag}")
    dq, dk, dv = attn_bwd(qkv, attn, lse, d_attn, f"attn_bwd_{tag}")
    splits = [(0, A_W), (A_W, A_W), (2 * A_W, A_W), (3 * A_W, CONV_C), (3 * A_W + CONV_C, CONV_C)]
    dres_new, d_norm, dq_u, dk_u = proj_bwd_data(
        h, g_row, w_in, [dq, dk, dv, d_a, d_gate], splits, dres, f"hyb_proj_bwd_{tag}", rope=rope, n_rot=2)
    parts = [dq_u, dk_u, dv, d_a, d_gate]
    d_win = jnp.concatenate([mm_tn(hn, p, f"hyb_win_grad_{tag}_{k}") for k, p in enumerate(parts)], axis=1)
    return dres_new, dict(norm=d_norm, w_in=d_win, dw_w=d_dw[:CONV_K], dw_b=d_db, ln_g=d_lng, ln_b=d_lnb, w_out=d_wout)


G_SCALE = G_DK ** -0.5
GP_RC = 256
GP_PAD = 8


def gdn_prep_fwd(x, conv_w, name):
    def body(x_ref, w_ref, o_ref, xpad):
        cb = pl.program_id(0)
        xpad[0:GP_PAD, :] = jnp.zeros((GP_PAD, 128), F32)
        xpad[GP_PAD:, :] = x_ref[...]
        for r in range(T // GP_RC):
            r0 = r * GP_RC
            y = jnp.zeros((GP_RC, 128), F32)
            for j in range(G_CONV):
                y = y + w_ref[pl.ds(j, 1), :] * xpad[pl.ds(r0 + GP_PAD - (G_CONV - 1) + j, GP_RC), :]
            s = y * _sigmoid(y)
            n = lax.rsqrt(jnp.sum(s * s, axis=-1, keepdims=True) + EPS)
            o_ref[r0:r0 + GP_RC, :] = s * jnp.where(cb < 2 * G_HEADS, n, 1.0)

    nb = G_QKV // 128
    return pl.pallas_call(
        body,
        grid=(nb,),
        in_specs=[pl.BlockSpec((T, 128), lambda c: (0, c)), pl.BlockSpec((G_CONV, 128), lambda c: (0, c))],
        out_specs=pl.BlockSpec((T, 128), lambda c: (0, c)),
        out_shape=jax.ShapeDtypeStruct((T, G_QKV), F32),
        scratch_shapes=[pltpu.VMEM((T + GP_PAD, 128), F32)],
        compiler_params=_cp("parallel"),
        name=name,
    )(x, conv_w)


def gdn_prep_bwd(dout, x, conv_w, part, l2, name):
    def body(d_ref, x_ref, w_ref, dx_ref, dw_ref, xpad, dypad, dwacc):
        xpad[0:GP_PAD, :] = jnp.zeros((GP_PAD, 128), F32)
        xpad[GP_PAD:, :] = x_ref[...]
        dypad[T:, :] = jnp.zeros((GP_PAD, 128), F32)
        dwacc[...] = jnp.zeros_like(dwacc)
        for r in range(T // GP_RC):
            r0 = r * GP_RC
            y = jnp.zeros((GP_RC, 128), F32)
            xs = []
            for j in range(G_CONV):
                xj = xpad[pl.ds(r0 + GP_PAD - (G_CONV - 1) + j, GP_RC), :]
                xs.append(xj)
                y = y + w_ref[pl.ds(j, 1), :] * xj
            sig = _sigmoid(y)
            s = y * sig
            d = d_ref[r0:r0 + GP_RC, :]
            if l2:
                n = lax.rsqrt(jnp.sum(s * s, axis=-1, keepdims=True) + EPS)
                out = s * n
                d = n * (d - out * jnp.sum(d * out, axis=-1, keepdims=True))
            dy = d * _dsilu(y, sig)
            dypad[r0:r0 + GP_RC, :] = dy
            for j in range(G_CONV):
                dwacc[8 * j:8 * j + 8, :] += (dy * xs[j]).reshape(GP_RC // 8, 8, 128).sum(axis=0)
        for r in range(T // GP_RC):
            r0 = r * GP_RC
            dx = jnp.zeros((GP_RC, 128), F32)
            for j in range(G_CONV):
                dx = dx + w_ref[pl.ds(j, 1), :] * dypad[pl.ds(r0 + (G_CONV - 1) - j, GP_RC), :]
            dx_ref[r0:r0 + GP_RC, :] = dx
        for j in range(G_CONV):
            dw_ref[pl.ds(j, 1), :] = jnp.sum(dwacc[8 * j:8 * j + 8, :], axis=0, keepdims=True)

    nb = G_HEADS
    off = part * nb
    col = pl.BlockSpec((T, 128), lambda c: (0, c))
    return pl.pallas_call(
        body,
        grid=(nb,),
        in_specs=[col, pl.BlockSpec((T, 128), lambda c: (0, off + c)), pl.BlockSpec((G_CONV, 128), lambda c: (0, off + c))],
        out_specs=[col, pl.BlockSpec((G_CONV, 128), lambda c: (0, c))],
        out_shape=[jax.ShapeDtypeStruct((T, G_HEADS * G_DK), F32), jax.ShapeDtypeStruct((G_CONV, G_HEADS * G_DK), F32)],
        scratch_shapes=[
            pltpu.VMEM((T + GP_PAD, 128), F32),
            pltpu.VMEM((T + GP_PAD, 128), F32),
            pltpu.VMEM((8 * G_CONV, 128), F32),
        ],
        compiler_params=_cp("parallel"),
        name=name,
    )(dout, x, conv_w)


def _seg_cumsum(x, reverse=False):
    row = lax.broadcasted_iota(jnp.int32, x.shape, 0) % CH
    s = 1
    while s < CH:
        if reverse:
            x = x + jnp.where(row < CH - s, pltpu.roll(x, x.shape[0] - s, 0), 0.0)
        else:
            x = x + jnp.where(row >= s, pltpu.roll(x, s, 0), 0.0)
        s *= 2
    return x


def _gdn_gates(ba_ref, alog_ref, dt_ref, h):
    ba = ba_ref[...]
    lane = lax.broadcasted_iota(jnp.int32, ba.shape, 1)
    b_col = jnp.sum(jnp.where(lane == h, ba, 0.0), axis=1, keepdims=True)
    a_col = jnp.sum(jnp.where(lane == G_HEADS + h, ba, 0.0), axis=1, keepdims=True)
    lane8 = lax.broadcasted_iota(jnp.int32, (1, G_HEADS), 1)
    alog = jnp.sum(jnp.where(lane8 == h, alog_ref[...], 0.0), axis=1, keepdims=True)
    dt = jnp.sum(jnp.where(lane8 == h, dt_ref[...], 0.0), axis=1, keepdims=True)
    beta = _sigmoid(b_col)
    xa = a_col + dt
    softplus = jnp.maximum(xa, 0.0) + jnp.log(1.0 + jnp.exp(-jnp.abs(xa)))
    ea = jnp.exp(alog)
    return beta, -ea * softplus, xa, ea


def _chunk_masks():
    i = lax.broadcasted_iota(jnp.int32, (CH, CH), 0)
    j = lax.broadcasted_iota(jnp.int32, (CH, CH), 1)
    return i >= j, i > j, i, j


def _decay(gcc, causal):
    gm = gcc[:, 0:CH]
    return jnp.where(causal, jnp.exp(jnp.minimum(gm - gm.T, 0.0)), 0.0)


def _split(a):
    hi = a.astype(BF16)
    return hi, (a - hi.astype(F32)).astype(BF16)


def _dot3(a, b):
    ah, al = _split(a)
    bh, bl = _split(b)
    return _dot(ah, bh) + (_dot(ah, bl) + _dot(al, bh))


def _unit_lower_inverse(lms, i, j):
    eye = jnp.where(i == j, 1.0, 0.0)
    ms = [None] * len(lms)
    b = 1
    while b < CH:
        pair = ((i // (2 * b)) == (j // (2 * b))) & ((i // b) % 2 == 1) & ((j // b) % 2 == 0)
        lbs = [jnp.where(pair, lm, 0.0) for lm in lms]
        if b == 1:
            ms = [eye - lb for lb in lbs]
        else:
            ts = [_dot3(m, lb) for m, lb in zip(ms, lbs)]
            ms = [m - _dot3(t, m) for m, t in zip(ms, ts)]
        b *= 2
    return ms


def gdn_local_fwd(qkv, ba, alog, dtb, name):
    def body(q_ref, k_ref, v_ref, ba_ref, al_ref, dt_ref, u_ref, w_ref, qd_ref, kd_ref, at_ref, el_ref, ti_ref, gcs):
        h = pl.program_id(1)
        beta, g, _, _ = _gdn_gates(ba_ref, al_ref, dt_ref, h)
        gc = _seg_cumsum(jnp.broadcast_to(g, (GRP, 128)))
        gcs[...] = gc
        causal, strict, i, j = _chunk_masks()
        lms = []
        for c in range(CPG):
            r = slice(c * CH, (c + 1) * CH)
            q, k = q_ref[r, :], k_ref[r, :]
            gcc = gc[r, :]
            ec = jnp.exp(gcc)
            gl = gcs[pl.ds(c * CH + CH - 1, 1), :]
            dm = _decay(gcc, causal)
            kbf = k.astype(BF16)
            a1 = _dot_nt((k * beta[r, :]).astype(BF16), kbf)
            lms.append(jnp.where(strict, a1 * dm, 0.0))
            qs = q * G_SCALE
            qd_ref[r, :] = (qs * ec).astype(BF16)
            kd_ref[r, :] = (k * jnp.exp(gl - gcc)).astype(BF16)
            at_ref[r, :] = (_dot_nt(qs.astype(BF16), kbf) * dm).astype(BF16)
            el_ref[pl.ds(c, 1), :] = jnp.exp(gl)
        tinvs = _unit_lower_inverse(lms, i, j)
        for c in range(CPG):
            r = slice(c * CH, (c + 1) * CH)
            bt = beta[r, :]
            tb = tinvs[c].astype(BF16)
            u_ref[r, :] = _dot(tb, (v_ref[r, :] * bt).astype(BF16))
            w_ref[r, :] = _dot(tb, (k_ref[r, :] * bt * jnp.exp(gc[r, :])).astype(BF16)).astype(BF16)
            ti_ref[r, :] = tinvs[c]

    hd = lambda off: pl.BlockSpec((GRP, 128), lambda i, h: (i, off + h))
    vec = pl.BlockSpec((1, G_HEADS), lambda i, h: (0, 0))
    sq = pl.BlockSpec((None, GRP, CH), lambda i, h: (h, i, 0))
    return pl.pallas_call(
        body,
        grid=(N_GRP, G_HEADS),
        in_specs=[hd(0), hd(G_HEADS), hd(2 * G_HEADS), pl.BlockSpec((GRP, 2 * G_HEADS), lambda i, h: (i, 0)), vec, vec],
        out_specs=[hd(0), hd(0), hd(0), hd(0), sq, pl.BlockSpec((None, CPG, 128), lambda i, h: (h, i, 0)), sq],
        out_shape=[
            jax.ShapeDtypeStruct((T, D), F32),
            jax.ShapeDtypeStruct((T, D), BF16),
            jax.ShapeDtypeStruct((T, D), BF16),
            jax.ShapeDtypeStruct((T, D), BF16),
            jax.ShapeDtypeStruct((G_HEADS, T, CH), BF16),
            jax.ShapeDtypeStruct((G_HEADS, T // CH, 128), F32),
            jax.ShapeDtypeStruct((G_HEADS, T, CH), F32),
        ],
        scratch_shapes=[pltpu.VMEM((GRP, 128), F32)],
        compiler_params=_cp("parallel", "parallel"),
        name=name,
    )(qkv, qkv, qkv, ba, alog, dtb)


def gdn_rec_fwd(u, w, qd, kd, at, el, name):
    def body(u_ref, w_ref, qd_ref, kd_ref, at_ref, el_ref, o_ref, vn_ref, st_ref, s_scr):
        @pl.when(pl.program_id(0) == 0)
        def _():
            s_scr[...] = jnp.zeros_like(s_scr)

        for c in range(CPG):
            r = slice(c * CH, (c + 1) * CH)
            for h in range(G_HEADS):
                ln = slice(h * 128, (h + 1) * 128)
                s = s_scr[h]
                st_ref[h, c] = s
                sb = s.astype(BF16)
                vn = (u_ref[r, ln] - _dot(w_ref[r, ln], sb)).astype(BF16)
                o_ref[r, ln] = _dot(qd_ref[r, ln], sb) + _dot(at_ref[h, r, :], vn)
                s_scr[h] = s * el_ref[h, pl.ds(c, 1), :] + _dot_tn(kd_ref[r, ln], vn)
                vn_ref[r, ln] = vn

    row = pl.BlockSpec((GRP, D), lambda i: (i, 0))
    return pl.pallas_call(
        body,
        grid=(N_GRP,),
        in_specs=[row, row, row, row, pl.BlockSpec((G_HEADS, GRP, CH), lambda i: (0, i, 0)),
                  pl.BlockSpec((G_HEADS, CPG, 128), lambda i: (0, i, 0))],
        out_specs=[row, row, pl.BlockSpec((G_HEADS, CPG, 128, 128), lambda i: (0, i, 0, 0))],
        out_shape=[
            jax.ShapeDtypeStruct((T, D), F32),
            jax.ShapeDtypeStruct((T, D), BF16),
            jax.ShapeDtypeStruct((G_HEADS, T // CH, 128, 128), F32),
        ],
        scratch_shapes=[pltpu.VMEM((G_HEADS, 128, 128), F32)],
        compiler_params=_cp("arbitrary"),
        name=name,
    )(u, w, qd, kd, at, el)


def gdn_rec_bwd(do, w, qd, kd, at, el, vn, st, name):
    def body(do_ref, w_ref, qd_ref, kd_ref, at_ref, el_ref, vn_ref, st_ref,
             du_ref, dw_ref, dqd_ref, dkd_ref, dat_ref, del_ref, ds_scr):
        @pl.when(pl.program_id(0) == 0)
        def _():
            ds_scr[...] = jnp.zeros_like(ds_scr)

        for c in reversed(range(CPG)):
            r = slice(c * CH, (c + 1) * CH)
            for h in range(G_HEADS):
                ln = slice(h * 128, (h + 1) * 128)
                ds = ds_scr[h]
                dsb = ds.astype(BF16)
                sn = st_ref[h, c]
                snb = sn.astype(BF16)
                dob = do_ref[r, ln].astype(BF16)
                vnb = vn_ref[r, ln]
                dvn = (_dot(kd_ref[r, ln], dsb) + _dot_tn(at_ref[h, r, :], dob)).astype(BF16)
                du_ref[r, ln] = dvn
                dkd_ref[r, ln] = _dot_nt(vnb, dsb)
                tot = jnp.sum(jnp.sum(ds * sn, axis=1, keepdims=True), axis=0, keepdims=True)
                del_ref[h, pl.ds(c, 1), :] = jnp.broadcast_to(tot, (1, 128))
                dqd_ref[r, ln] = _dot_nt(dob, snb)
                dat_ref[h, r, :] = _dot_nt(dob, vnb)
                dw_ref[r, ln] = (-_dot_nt(dvn, snb)).astype(BF16)
                ds_scr[h] = ds * el_ref[h, pl.ds(c, 1), :] + _dot_tn(qd_ref[r, ln], dob) - _dot_tn(w_ref[r, ln], dvn)

    last = N_GRP - 1
    row = pl.BlockSpec((GRP, D), lambda i: (last - i, 0))
    sq = pl.BlockSpec((G_HEADS, GRP, CH), lambda i: (0, last - i, 0))
    sc = pl.BlockSpec((G_HEADS, CPG, 128), lambda i: (0, last - i, 0))
    return pl.pallas_call(
        body,
        grid=(N_GRP,),
        in_specs=[row, row, row, row, sq, sc, row, pl.BlockSpec((G_HEADS, CPG, 128, 128), lambda i: (0, last - i, 0, 0))],
        out_specs=[row, row, row, row, sq, sc],
        out_shape=[
            jax.ShapeDtypeStruct((T, D), BF16),
            jax.ShapeDtypeStruct((T, D), BF16),
            jax.ShapeDtypeStruct((T, D), F32),
            jax.ShapeDtypeStruct((T, D), F32),
            jax.ShapeDtypeStruct((G_HEADS, T, CH), F32),
            jax.ShapeDtypeStruct((G_HEADS, T // CH, 128), F32),
        ],
        scratch_shapes=[pltpu.VMEM((G_HEADS, 128, 128), F32)],
        compiler_params=_cp("arbitrary"),
        name=name,
    )(do, w, qd, kd, at, el, vn, st)


def gdn_local_bwd(qkv, ba, alog, dtb, tinv, du, dw, dqd, dkd, dat, dl, name):
    def body(q_ref, k_ref, v_ref, ba_ref, al_ref, dt_ref, ti_ref, du_ref, dw_ref, dqd_ref, dkd_ref, dat_ref, dl_ref,
             dq_ref, dk_ref, dv_ref, dba_ref, dal_ref, ddt_ref, gcs):
        gi = pl.program_id(0)
        h = pl.program_id(1)
        beta, g, xa, ea = _gdn_gates(ba_ref, al_ref, dt_ref, h)
        gc = _seg_cumsum(jnp.broadcast_to(g, (GRP, 128)))
        gcs[...] = gc
        causal, strict, _, _ = _chunk_masks()
        dgc_l, dgl_l, dbeta_l = [], [], []
        for c in range(CPG):
            r = slice(c * CH, (c + 1) * CH)
            q, k, v = q_ref[r, :], k_ref[r, :], v_ref[r, :]
            bt = beta[r, :]
            gcc = gc[r, :]
            ec = jnp.exp(gcc)
            gl = gcs[pl.ds(c * CH + CH - 1, 1), :]
            f2 = jnp.exp(gl - gcc)
            elc = jnp.exp(gl)
            dm = _decay(gcc, causal)
            qs = q * G_SCALE
            kb = k * bt
            vb = v * bt
            kbe = kb * ec
            kbf, kbb, qsb = k.astype(BF16), kb.astype(BF16), qs.astype(BF16)
            a1 = _dot_nt(kbb, kbf)
            qk = _dot_nt(qsb, kbf)
            ti = ti_ref[r, :]
            tb = ti.astype(BF16)
            du_c, dw_c = du_ref[r, :], dw_ref[r, :]
            dqd_c, dkd_c, dat_c = dqd_ref[r, :], dkd_ref[r, :], dat_ref[r, :]

            dqs = dqd_c * ec
            d_e = jnp.sum(dqd_c * qs, axis=1, keepdims=True)
            dk = dkd_c * f2
            tcol = jnp.sum(dkd_c * k, axis=1, keepdims=True) * f2[:, 0:1]
            dgl = jnp.sum(tcol, axis=0, keepdims=True) + dl_ref[pl.ds(c, 1), 0:1] * elc[:, 0:1]
            dgc = -tcol
            dqk = (dat_c * dm).astype(BF16)
            d_d = dat_c * qk
            dqs = dqs + _dot(dqk, kbf)
            dk = dk + _dot_tn(dqk, qsb)
            dtinv = _dot_nt(du_c, vb.astype(BF16)) + _dot_nt(dw_c, kbe.astype(BF16))
            dvb = _dot_tn(tb, du_c)
            dkbe = _dot_tn(tb, dw_c)
            dlm = jnp.where(strict, -_dot3(_dot3(ti.T, dtinv), ti.T), 0.0)
            da1 = (dlm * dm).astype(BF16)
            d_d = d_d + dlm * a1
            dkb = _dot(da1, kbf) + dkbe * ec
            dk = dk + _dot_tn(da1, kbb)
            d_e = d_e + jnp.sum(dkbe * kb, axis=1, keepdims=True)
            dk = dk + dkb * bt
            dbeta_l.append(jnp.sum(dkb * k, axis=1, keepdims=True) + jnp.sum(dvb * v, axis=1, keepdims=True))
            ddiff = d_d * dm
            dgc = dgc + jnp.sum(ddiff, axis=1, keepdims=True) - jnp.sum(ddiff.T, axis=1, keepdims=True)
            dgc = dgc + d_e * ec[:, 0:1]
            dgc_l.append(dgc)
            dgl_l.append(jnp.broadcast_to(dgl, (CH, 1)))
            dq_ref[r, :] = dqs * G_SCALE
            dk_ref[r, :] = dk
            dv_ref[r, :] = dvb * bt

        dgc_all = jnp.broadcast_to(jnp.concatenate(dgc_l, axis=0), (GRP, 128))
        dg = _seg_cumsum(dgc_all, reverse=True)[:, 0:1] + jnp.concatenate(dgl_l, axis=0)
        dbeta = jnp.concatenate(dbeta_l, axis=0)
        da = dg * (-ea) * _sigmoid(xa)
        db = dbeta * beta * (1.0 - beta)
        lane = lax.broadcasted_iota(jnp.int32, (GRP, 2 * G_HEADS), 1)
        dba = jnp.where(lane == h, db, 0.0) + jnp.where(lane == G_HEADS + h, da, 0.0)
        lane8 = lax.broadcasted_iota(jnp.int32, (1, G_HEADS), 1)
        dal = jnp.where(lane8 == h, jnp.sum(dg * g, axis=0, keepdims=True), 0.0)
        ddt = jnp.where(lane8 == h, jnp.sum(da, axis=0, keepdims=True), 0.0)

        @pl.when(h == 0)
        def _():
            dba_ref[...] = dba

        @pl.when(h > 0)
        def _():
            dba_ref[...] += dba

        @pl.when((h == 0) & (gi == 0))
        def _():
            dal_ref[...] = dal
            ddt_ref[...] = ddt

        @pl.when((h > 0) | (gi > 0))
        def _():
            dal_ref[...] += dal
            ddt_ref[...] += ddt

    hd = lambda off: pl.BlockSpec((GRP, 128), lambda i, h: (i, off + h))
    vec = pl.BlockSpec((1, G_HEADS), lambda i, h: (0, 0))
    sq = pl.BlockSpec((None, GRP, CH), lambda i, h: (h, i, 0))
    gates = pl.BlockSpec((GRP, 2 * G_HEADS), lambda i, h: (i, 0))
    return pl.pallas_call(
        body,
        grid=(N_GRP, G_HEADS),
        in_specs=[hd(0), hd(G_HEADS), hd(2 * G_HEADS), gates, vec, vec, sq, hd(0), hd(0), hd(0), hd(0), sq,
                  pl.BlockSpec((None, CPG, 128), lambda i, h: (h, i, 0))],
        out_specs=[hd(0), hd(0), hd(0), gates, vec, vec],
        out_shape=[
            jax.ShapeDtypeStruct((T, D), F32),
            jax.ShapeDtypeStruct((T, D), F32),
            jax.ShapeDtypeStruct((T, D), F32),
            jax.ShapeDtypeStruct((T, 2 * G_HEADS), F32),
            jax.ShapeDtypeStruct((1, G_HEADS), F32),
            jax.ShapeDtypeStruct((1, G_HEADS), F32),
        ],
        scratch_shapes=[pltpu.VMEM((GRP, 128), F32)],
        compiler_params=_cp("arbitrary", "arbitrary"),
        name=name,
    )(qkv, qkv, qkv, ba, alog, dtb, tinv, du, dw, dqd, dkd, dat, dl)


def _gated_norm(o, z, g):
    rstd = lax.rsqrt(jnp.mean(o * o, axis=-1, keepdims=True) + EPS)
    oh = o * rstd
    sig = _sigmoid(z)
    return oh, rstd, sig


def gdn_out_fwd(h, o, z, norm_g, w_out, name):
    tt = 512

    def body(h_ref, o_ref, z_ref, g_ref, w_ref, hnew_ref, cat_ref):
        g = g_ref[...]
        for hh in range(G_HEADS):
            ln = slice(hh * 128, (hh + 1) * 128)
            zz = z_ref[:, ln]
            oh, _, sig = _gated_norm(o_ref[:, ln], zz, g)
            cat_ref[:, ln] = (oh * g * (zz * sig)).astype(BF16)
        hnew_ref[...] = h_ref[...] + _dot(cat_ref[...], w_ref[...])

    full = pl.BlockSpec((tt, D), lambda i: (i, 0))
    return pl.pallas_call(
        body,
        grid=(T // tt,),
        in_specs=[full, full, full, pl.BlockSpec((1, 128), lambda i: (0, 0)), pl.BlockSpec((D, D), lambda i: (0, 0))],
        out_specs=[full, full],
        out_shape=[jax.ShapeDtypeStruct((T, D), F32), jax.ShapeDtypeStruct((T, D), BF16)],
        compiler_params=_cp("parallel"),
        name=name,
    )(h, o, z, norm_g, w_out)


def gdn_out_bwd(dres, o, z, norm_g, w_out, name):
    tt = 512

    def body(d_ref, o_ref, z_ref, g_ref, w_ref, do_ref, dz_ref, dg_ref, dcat):
        i = pl.program_id(0)
        g = g_ref[...]
        dcat[...] = _dot_nt(d_ref[...].astype(BF16), w_ref[...])
        dg = jnp.zeros((1, 128), F32)
        for hh in range(G_HEADS):
            ln = slice(hh * 128, (hh + 1) * 128)
            zz = z_ref[:, ln]
            oh, rstd, sig = _gated_norm(o_ref[:, ln], zz, g)
            dout = dcat[:, ln]
            dy = dout * (zz * sig)
            dz_ref[:, ln] = dout * (oh * g) * _dsilu(zz, sig)
            dg = dg + jnp.sum(dy * oh, axis=0, keepdims=True)
            doh = dy * g
            do_ref[:, ln] = rstd * (doh - oh * jnp.mean(doh * oh, axis=-1, keepdims=True))

        @pl.when(i == 0)
        def _():
            dg_ref[...] = dg

        @pl.when(i > 0)
        def _():
            dg_ref[...] += dg

    full = pl.BlockSpec((tt, D), lambda i: (i, 0))
    vec = pl.BlockSpec((1, 128), lambda i: (0, 0))
    return pl.pallas_call(
        body,
        grid=(T // tt,),
        in_specs=[full, full, full, vec, pl.BlockSpec((D, D), lambda i: (0, 0))],
        out_specs=[full, full, vec],
        out_shape=[jax.ShapeDtypeStruct((T, D), F32), jax.ShapeDtypeStruct((T, D), F32), jax.ShapeDtypeStruct((1, 128), F32)],
        scratch_shapes=[pltpu.VMEM((tt, D), F32)],
        compiler_params=_cp("arbitrary"),
        name=name,
    )(dres, o, z, norm_g, w_out)


GDN_SPLITS = [(0, 1024), (1024, 1024), (2048, 1024), (3072, 1024), (4096, 2 * G_HEADS)]


def gdn_fwd(h, g_row, w_in, conv_w, alog, dtb, norm_g, w_out, tag):
    hn, qkv_pre, z, ba = proj_fwd(h, g_row, w_in, [(0, G_QKV), (G_QKV, 1024), (4096, 2 * G_HEADS)], f"gdn_proj_{tag}")
    qkv = gdn_prep_fwd(qkv_pre, conv_w, f"gdn_prep_{tag}")
    u, w, qd, kd, at, el, tinv = gdn_local_fwd(qkv, ba, alog, dtb, f"gdn_local_{tag}")
    o, vn, st = gdn_rec_fwd(u, w, qd, kd, at, el, f"gdn_rec_{tag}")
    hnew, cat = gdn_out_fwd(h, o, z, norm_g, w_out, f"gdn_out_{tag}")
    return hnew, (h, hn, qkv_pre, z, ba, qkv, w, qd, kd, at, el, tinv, o, vn, st, cat)


def gdn_bwd(dres, saved, g_row, w_in, conv_w, alog, dtb, norm_g, w_out, tag):
    h, hn, qkv_pre, z, ba, qkv, w, qd, kd, at, el, tinv, o, vn, st, cat = saved
    d_o, d_z, d_ng = gdn_out_bwd(dres, o, z, norm_g, w_out, f"gdn_out_bwd_{tag}")
    d_wout = mm_tn(cat, dres, f"gdn_wout_grad_{tag}")
    du, dw, dqd, dkd, dat, dl = gdn_rec_bwd(d_o, w, qd, kd, at, el, vn, st, f"gdn_rec_bwd_{tag}")
    dq, dk, dv, dba, dal, ddt = gdn_local_bwd(qkv, ba, alog, dtb, tinv, du, dw, dqd, dkd, dat, dl, f"gdn_local_bwd_{tag}")
    dpre, dcw = [], []
    for part, d in enumerate((dq, dk, dv)):
        dx, dwc = gdn_prep_bwd(d, qkv_pre, conv_w, part, part < 2, f"gdn_prep_bwd_{tag}_{part}")
        dpre.append(dx)
        dcw.append(dwc)
    parts = dpre + [d_z, dba]
    dres_new, d_norm = proj_bwd_data(h, g_row, w_in, parts, GDN_SPLITS, dres, f"gdn_proj_bwd_{tag}")
    d_win = jnp.concatenate([mm_tn(hn, p, f"gdn_win_grad_{tag}_{k}") for k, p in enumerate(parts)], axis=1)
    return dres_new, dict(norm=d_norm, w_in=d_win, conv_w=jnp.concatenate(dcw, axis=1), A_log=dal, dt_bias=ddt,
                          norm_g=d_ng, w_out=d_wout)


MESH = pl.DeviceIdType.MESH
ANY = pl.BlockSpec(memory_space=pl.ANY)


def _coords():
    return lax.axis_index("x"), lax.axis_index("y"), lax.axis_index("c")


def _slot(p):
    return 4 * p[0] + 2 * p[1] + p[2]


def all_gather(shards, name):
    k_n = len(shards)

    def body(*refs):
        srcs, dsts = refs[:k_n], refs[k_n:2 * k_n]
        send_sems, recv_sems, local_sems = refs[2 * k_n:]
        x, y, c = _coords()
        me, sibling = (x, y, c), (x, y, 1 - c)
        chips = [(1 - x, y), (x, 1 - y), (1 - x, 1 - y)]

        def copy(k, s, block, to, from_src=False):
            rows = dsts[k].at[_slot(block)]
            return pltpu.make_async_remote_copy(
                src_ref=srcs[k] if from_src else rows, dst_ref=rows,
                send_sem=send_sems.at[k, s], recv_sem=recv_sems.at[k, s], device_id=to, device_id_type=MESH)

        local = [pltpu.make_async_copy(srcs[k], dsts[k].at[_slot(me)], local_sems.at[k]) for k in range(k_n)]
        for cp in local:
            cp.start()
        first = []
        for k in range(k_n):
            first.append(copy(k, 0, me, sibling, True))
            first += [copy(k, 1 + j, me, (*chip, c), True) for j, chip in enumerate(chips)]
        for cp in first:
            cp.start()
        passed = []
        for j, chip in enumerate(chips):
            for k in range(k_n):
                copy(k, 1 + j, (*chip, c), me).wait_recv()
                fw = copy(k, 4 + j, (*chip, c), sibling)
                fw.start()
                passed.append(fw)
        for k in range(k_n):
            copy(k, 0, sibling, me).wait_recv()
            for j, chip in enumerate(chips):
                copy(k, 4 + j, (*chip, 1 - c), me).wait_recv()
        for cp in first + passed:
            cp.wait_send()
        for cp in local:
            cp.wait()

    return pl.pallas_call(
        body,
        in_specs=[ANY] * k_n,
        out_specs=[ANY] * k_n,
        out_shape=[jax.ShapeDtypeStruct((N_DEV,) + s.shape, s.dtype) for s in shards],
        scratch_shapes=[pltpu.SemaphoreType.DMA((k_n, 7)), pltpu.SemaphoreType.DMA((k_n, 7)),
                        pltpu.SemaphoreType.DMA((k_n,))],
        name=name,
    )(*shards)


HBM = pl.BlockSpec(memory_space=pltpu.HBM)
SEM = pl.BlockSpec(memory_space=pltpu.SEMAPHORE)
EFFECT = pltpu.SideEffectType.DATAFLOW_SIDE_EFFECTING


def _hbm(a):
    return pltpu.with_memory_space_constraint(a, pltpu.HBM)


def _peer_list(x, y, c):
    peers = []
    for j in range(1, N_DEV):
        jx, jy, jc = (j >> 2) & 1, (j >> 1) & 1, j & 1
        peers.append((x if jx == 0 else 1 - x, y if jy == 0 else 1 - y, c if jc == 0 else 1 - c))
    return peers


def _push_views(kind, layer, src_ref, land_ref, me, peer_slot):
    if kind == "gather":
        return src_ref, land_ref.at[me], land_ref.at[peer_slot]
    if layer is None:
        return src_ref.at[peer_slot], land_ref.at[me], land_ref.at[peer_slot]
    return src_ref.at[peer_slot], land_ref.at[me, layer], land_ref.at[peer_slot, layer]


def _push_copies(groups, srcs, lands, sems):
    x, y, c = _coords()
    me = _slot((x, y, c))
    peers = _peer_list(x, y, c)
    t = 0
    for gi, group in enumerate(groups):
        for ti, (kind, layer, _, li) in enumerate(group):
            for j, peer in enumerate(peers):
                out, there, here = _push_views(kind, layer, srcs[t], lands[li], me, _slot(peer))
                k = ti * (N_DEV - 1) + j
                yield out, there, here, sems[2 * gi].at[k], sems[2 * gi + 1].at[k], peer
            t += 1


def push_start(groups, lands, name, carry=()):
    flat = [it for g in groups for it in g]
    n, n_l, n_g, n_c = len(flat), len(lands), len(groups), len(carry)
    n_in = n + n_l + n_c

    def body(*refs):
        srcs, land_refs, sems = refs[:n], refs[n:n + n_l], refs[n_in:n_in + 2 * n_g]
        for out, there, _, s_sem, r_sem, peer in _push_copies(groups, srcs, land_refs, sems):
            pltpu.make_async_remote_copy(src_ref=out, dst_ref=there, send_sem=s_sem, recv_sem=r_sem,
                                         device_id=peer, device_id_type=MESH).start()

    arrays = [it[2] for it in flat] + list(lands) + list(carry)
    sem_shapes = []
    for g in groups:
        sem_shapes += [pltpu.SemaphoreType.DMA((len(g) * (N_DEV - 1),))] * 2
    outs = pl.pallas_call(
        body,
        name=name,
        in_specs=[HBM] * n_in,
        out_specs=[SEM] * (2 * n_g) + [HBM] * n_in,
        out_shape=sem_shapes + [pltpu.HBM(a.shape, a.dtype) for a in arrays],
        input_output_aliases={i: 2 * n_g + i for i in range(n_in)},
        compiler_params=pltpu.CompilerParams(has_side_effects=EFFECT),
    )(*[_hbm(a) for a in arrays])
    sems, thru = list(outs[:2 * n_g]), list(outs[2 * n_g:])
    return sems, thru[:n], thru[n:n + n_l], thru[n + n_l:]


def push_wait(groups, lands, sems, after, name):
    flat = [it for g in groups for it in g]
    n, n_l, n_g = len(flat), len(lands), len(groups)

    def body(*refs):
        srcs, land_refs, sem_refs = refs[:n], refs[n:n + n_l], refs[n + n_l:n + n_l + 2 * n_g]
        for out, _, here, s_sem, r_sem, peer in _push_copies(groups, srcs, land_refs, sem_refs):
            cp = pltpu.make_async_remote_copy(src_ref=out, dst_ref=here, send_sem=s_sem, recv_sem=r_sem,
                                              device_id=peer, device_id_type=MESH)
            cp.wait_send()
            cp.wait_recv()

    arrays = [it[2] for it in flat] + list(lands)
    outs = pl.pallas_call(
        body,
        name=name,
        in_specs=[HBM] * (n + n_l) + [SEM] * (2 * n_g) + [ANY],
        out_specs=[HBM] * (n + n_l),
        out_shape=[pltpu.HBM(a.shape, a.dtype) for a in arrays],
        input_output_aliases={i: i for i in range(n + n_l)},
        compiler_params=pltpu.CompilerParams(has_side_effects=EFFECT),
    )(*arrays, *sems, after)
    return list(outs[:n]), list(outs[n:])


def sum_slabs(parts, name):
    n, rows, cols = parts.shape

    def body(p_ref, o_ref):
        g = p_ref[0]
        for s in range(1, n):
            g = g + p_ref[s]
        o_ref[...] = g

    return pl.pallas_call(body, out_shape=jax.ShapeDtypeStruct((rows, cols), F32), name=name)(parts)


def _row_tile(rows, cols):
    if rows * cols * 4 <= (1 << 20) or rows % 8:
        return rows
    tr = rows
    while tr % 2 == 0 and (tr // 2) % 8 == 0 and tr * cols * 4 > (1 << 20):
        tr //= 2
    return tr


def adamw(parts, w, m, v, name):
    p_n = parts.shape[0]
    rows, cols = w.shape
    tr = _row_tile(rows, cols)

    def body(p_ref, w_ref, m_ref, v_ref, g_ref, d_ref, nm_ref, nv_ref):
        g = p_ref[0].astype(F32)
        for s in range(1, p_n):
            g = g + p_ref[s].astype(F32)
        m_new = ADAM_B1 * m_ref[...] + (1.0 - ADAM_B1) * g
        v_new = ADAM_B2 * v_ref[...] + (1.0 - ADAM_B2) * (g * g)
        m_hat = m_new / (1.0 - ADAM_B1 ** ADAM_STEP)
        v_hat = v_new / (1.0 - ADAM_B2 ** ADAM_STEP)
        g_ref[...] = g
        d_ref[...] = -ADAM_LR * (m_hat / (jnp.sqrt(v_hat) + ADAM_EPS) + ADAM_WD * w_ref[...])
        nm_ref[...] = m_new
        nv_ref[...] = v_new

    blk = pl.BlockSpec((tr, cols), lambda i: (i, 0))
    return pl.pallas_call(
        body,
        grid=(rows // tr,),
        in_specs=[pl.BlockSpec((p_n, tr, cols), lambda i: (0, i, 0)), blk, blk, blk],
        out_specs=[blk] * 4,
        out_shape=[jax.ShapeDtypeStruct((rows, cols), F32)] * 4,
        compiler_params=_cp("parallel"),
        name=name,
    )(parts, w, m, v)


def _adamw_nd(parts, w, m, v, name):
    shp = w.shape
    cols = shp[-1]
    rows = math.prod(shp[:-1])
    outs = adamw(parts.reshape(parts.shape[0], rows, cols), w.reshape(rows, cols), m.reshape(rows, cols),
                 v.reshape(rows, cols), name)
    return [o.reshape(shp) for o in outs]


REPL = ["ffn1_norm", "mix_norm", "ffn2_norm", "hyb_dw_b", "hyb_ln_g", "hyb_ln_b", "gdn_A_log", "gdn_dt_bias",
        "gdn_norm_g", "final_norm"]
WEIGHTS = ["ffn1_norm", "ffn1_w_in", "ffn1_w_out", "mix_norm", "ffn2_norm", "ffn2_w_in", "ffn2_w_out", "hyb_w_in",
           "hyb_dw_w", "hyb_dw_b", "hyb_ln_g", "hyb_ln_b", "hyb_w_out", "gdn_w_in", "gdn_conv_w", "gdn_A_log",
           "gdn_dt_bias", "gdn_norm_g", "gdn_w_out", "final_norm"]


def _pack(arrs, rows):
    flat = jnp.concatenate([a.reshape(-1) for a in arrs])
    return jnp.pad(flat, (0, rows * 128 - flat.shape[0])).reshape(rows, 128)


def _cols_to_slabs(a):
    d, n = a.shape
    return a.reshape(d, N_DEV, n // N_DEV).transpose(1, 0, 2)


def _slabs_to_cols(a):
    return jnp.moveaxis(a, 0, -2).reshape(a.shape[1:-1] + (N_DEV * a.shape[-1],))


def kernel(x, positions, ffn1_norm, ffn1_w_in, ffn1_w_out, mix_norm, ffn2_norm, ffn2_w_in, ffn2_w_out, hyb_w_in, hyb_dw_w, hyb_dw_b, hyb_ln_g, hyb_ln_b, hyb_w_out, gdn_w_in, gdn_conv_w, gdn_A_log, gdn_dt_bias, gdn_norm_g, gdn_w_out, final_norm, loss_target, m_ffn1_norm, m_ffn1_w_in, m_ffn1_w_out, m_mix_norm, m_ffn2_norm, m_ffn2_w_in, m_ffn2_w_out, m_hyb_w_in, m_hyb_dw_w, m_hyb_dw_b, m_hyb_ln_g, m_hyb_ln_b, m_hyb_w_out, m_gdn_w_in, m_gdn_conv_w, m_gdn_A_log, m_gdn_dt_bias, m_gdn_norm_g, m_gdn_w_out, m_final_norm, v_ffn1_norm, v_ffn1_w_in, v_ffn1_w_out, v_mix_norm, v_ffn2_norm, v_ffn2_w_in, v_ffn2_w_out, v_hyb_w_in, v_hyb_dw_w, v_hyb_dw_b, v_hyb_ln_g, v_hyb_ln_b, v_hyb_w_out, v_gdn_w_in, v_gdn_conv_w, v_gdn_A_log, v_gdn_dt_bias, v_gdn_norm_g, v_gdn_w_out, v_final_norm):
    w = dict(ffn1_norm=ffn1_norm, ffn1_w_in=ffn1_w_in, ffn1_w_out=ffn1_w_out, mix_norm=mix_norm, ffn2_norm=ffn2_norm,
             ffn2_w_in=ffn2_w_in, ffn2_w_out=ffn2_w_out, hyb_w_in=hyb_w_in, hyb_dw_w=hyb_dw_w, hyb_dw_b=hyb_dw_b,
             hyb_ln_g=hyb_ln_g, hyb_ln_b=hyb_ln_b, hyb_w_out=hyb_w_out, gdn_w_in=gdn_w_in, gdn_conv_w=gdn_conv_w,
             gdn_A_log=gdn_A_log, gdn_dt_bias=gdn_dt_bias, gdn_norm_g=gdn_norm_g, gdn_w_out=gdn_w_out,
             final_norm=final_norm)
    mom = dict(ffn1_norm=m_ffn1_norm, ffn1_w_in=m_ffn1_w_in, ffn1_w_out=m_ffn1_w_out, mix_norm=m_mix_norm,
               ffn2_norm=m_ffn2_norm, ffn2_w_in=m_ffn2_w_in, ffn2_w_out=m_ffn2_w_out, hyb_w_in=m_hyb_w_in,
               hyb_dw_w=m_hyb_dw_w, hyb_dw_b=m_hyb_dw_b, hyb_ln_g=m_hyb_ln_g, hyb_ln_b=m_hyb_ln_b,
               hyb_w_out=m_hyb_w_out, gdn_w_in=m_gdn_w_in, gdn_conv_w=m_gdn_conv_w, gdn_A_log=m_gdn_A_log,
               gdn_dt_bias=m_gdn_dt_bias, gdn_norm_g=m_gdn_norm_g, gdn_w_out=m_gdn_w_out, final_norm=m_final_norm)
    var = dict(ffn1_norm=v_ffn1_norm, ffn1_w_in=v_ffn1_w_in, ffn1_w_out=v_ffn1_w_out, mix_norm=v_mix_norm,
               ffn2_norm=v_ffn2_norm, ffn2_w_in=v_ffn2_w_in, ffn2_w_out=v_ffn2_w_out, hyb_w_in=v_hyb_w_in,
               hyb_dw_w=v_hyb_dw_w, hyb_dw_b=v_hyb_dw_b, hyb_ln_g=v_hyb_ln_g, hyb_ln_b=v_hyb_ln_b,
               hyb_w_out=v_hyb_w_out, gdn_w_in=v_gdn_w_in, gdn_conv_w=v_gdn_conv_w, gdn_A_log=v_gdn_A_log,
               gdn_dt_bias=v_gdn_dt_bias, gdn_norm_g=v_gdn_norm_g, gdn_w_out=v_gdn_w_out, final_norm=v_final_norm)
    xi, yi, ci = _coords()
    me = 4 * xi + 2 * yi + ci

    big = ["ffn1_w_in", "ffn1_w_out", "ffn2_w_in", "ffn2_w_out", "hyb_w_in", "hyb_w_out", "gdn_w_in", "gdn_w_out"]
    ag_groups, ag_lands = [], []

    def add_group(shards):
        group = []
        for s in shards:
            land = lax.dynamic_update_slice(lax.empty((N_DEV,) + s.shape, s.dtype), s[None], (me,) + (0,) * s.ndim)
            group.append(("gather", None, s, len(ag_lands)))
            ag_lands.append(land)
        ag_groups.append(group)

    for l in range(DEPTH):
        i = l // 2
        add_group([ffn1_w_in[l].astype(BF16), ffn1_w_out[l].astype(BF16)])
        if l % 2 == 0:
            add_group([hyb_w_in[i].astype(BF16), hyb_w_out[i].astype(BF16), hyb_dw_w[i]])
        else:
            add_group([gdn_w_in[i].astype(BF16), gdn_w_out[i].astype(BF16), gdn_conv_w[i]])
        add_group([ffn2_w_in[l].astype(BF16), ffn2_w_out[l].astype(BF16)])
    ag_sems, ag_srcs, ag_lands, _ = push_start(ag_groups, ag_lands, "weights_gather_start")

    def fetch(gi, after):
        group = ag_groups[gi]
        base = sum(len(g) for g in ag_groups[:gi])
        items = [(kind, layer, ag_srcs[base + t], t) for t, (kind, layer, _, _) in enumerate(group)]
        lands = [ag_lands[li] for _, _, _, li in group]
        return push_wait([items], lands, ag_sems[2 * gi:2 * gi + 2], after, f"weights_gather_wait_{gi}")[1]

    row = lambda a: a.reshape(1, -1)

    rope = make_rope(positions)
    h = x[0]
    saved = []
    for l in range(DEPTH):
        i = l // 2
        rec = {"h1": h}
        wi, wo = fetch(3 * l, h)
        rec["w1"] = (wi.reshape(2, FFN_TILES, D, FFN_SHARD), wo)
        h, rec["hn1"], rec["a1"], rec["b1"] = ffn_fwd(h, row(ffn1_norm[l]), *rec["w1"], l, "1")
        mi, mo, mc = fetch(3 * l + 1, h)
        if l % 2 == 0:
            rec["wm"] = (_slabs_to_cols(mi), jnp.pad(_slabs_to_cols(mc), ((0, 1), (0, 0))), mo.reshape(D, D))
            w_in_f, dw_f, w_out_f = rec["wm"]
            h, rec["mix"] = hybrid_fwd(h, row(mix_norm[l]), w_in_f, dw_f, row(hyb_dw_b[i]), row(hyb_ln_g[i]),
                                       row(hyb_ln_b[i]), w_out_f, rope, str(i))
        else:
            rec["wm"] = (_slabs_to_cols(mi), _slabs_to_cols(mc), mo.reshape(D, D))
            w_in_f, cw_f, w_out_f = rec["wm"]
            h, rec["mix"] = gdn_fwd(h, row(mix_norm[l]), w_in_f, cw_f, row(gdn_A_log[i]), row(gdn_dt_bias[i]),
                                    row(gdn_norm_g[i]), w_out_f, str(i))
        rec["h2"] = h
        wi, wo = fetch(3 * l + 2, h)
        rec["w2"] = (wi.reshape(2, FFN_TILES, D, FFN_SHARD), wo)
        h, rec["hn2"], rec["a2"], rec["b2"] = ffn_fwd(h, row(ffn2_norm[l]), *rec["w2"], l, "2")
        saved.append(rec)
    dres, d_final, loss_acc = final_loss(h, row(final_norm), loss_target[0])
    loss = lax.psum(loss_acc[0, 0], ("x", "y", "c"))

    ge_land = {n: lax.empty((N_DEV,) + w[n].shape, BF16) for n in big}
    ge_pending = []

    def send(named, layer, tag, carry):
        lands = [ge_land[n] for n, _ in named]
        group = [("scatter", layer, s, t) for t, (_, s) in enumerate(named)]
        sems, srcs, lands_out, carried = push_start([group], lands, f"grad_send_{tag}", carry=[carry])
        for (n, _), land in zip(named, lands_out):
            ge_land[n] = land
        ge_pending.append(([(n, layer, s) for (n, _), s in zip(named, srcs)], sems))
        return carried[0]

    gsmall = {n: [None] * (DEPTH if n in ("ffn1_norm", "mix_norm", "ffn2_norm") else 2) for n in REPL[:-1]}
    gsmall["hyb_dw_w"] = [None, None]
    gsmall["gdn_conv_w"] = [None, None]
    for l in reversed(range(DEPTH)):
        i = l // 2
        rec = saved[l]
        dhn, dwin, dwout = ffn_bwd(rec["hn2"], rec["a2"], rec["b2"], dres, *rec["w2"], l, "2")
        dhn = send([("ffn2_w_in", dwin.reshape(N_DEV, D, FFN_SHARD)),
                    ("ffn2_w_out", dwout.reshape(N_DEV, FFN_SHARD // 2, D))], l, f"ffn2_{l}", dhn)
        dres, dg = norm_bwd(rec["h2"], row(ffn2_norm[l]), dhn, dres, f"ffn2_norm_bwd_{l}")
        gsmall["ffn2_norm"][l] = dg
        if l % 2 == 0:
            w_in_f, dw_f, w_out_f = rec["wm"]
            dres, gr = hybrid_bwd(dres, rec["mix"], row(mix_norm[l]), w_in_f, dw_f, row(hyb_ln_g[i]),
                                  row(hyb_ln_b[i]), w_out_f, rope, str(i))
            dres = send([("hyb_w_in", _cols_to_slabs(gr["w_in"]).astype(BF16)),
                         ("hyb_w_out", gr["w_out"].reshape(N_DEV, D // N_DEV, D).astype(BF16))], i, f"hyb_{i}", dres)
            for n in ("dw_w", "dw_b", "ln_g", "ln_b"):
                gsmall["hyb_" + n][i] = gr[n]
        else:
            w_in_f, cw_f, w_out_f = rec["wm"]
            dres, gr = gdn_bwd(dres, rec["mix"], row(mix_norm[l]), w_in_f, cw_f, row(gdn_A_log[i]),
                               row(gdn_dt_bias[i]), row(gdn_norm_g[i]), w_out_f, str(i))
            dres = send([("gdn_w_in", _cols_to_slabs(gr["w_in"]).astype(BF16)),
                         ("gdn_w_out", gr["w_out"].reshape(N_DEV, D // N_DEV, D).astype(BF16))], i, f"gdn_{i}", dres)
            for n in ("conv_w", "A_log", "dt_bias", "norm_g"):
                gsmall["gdn_" + n][i] = gr[n]
        gsmall["mix_norm"][l] = gr["norm"]
        dhn, dwin, dwout = ffn_bwd(rec["hn1"], rec["a1"], rec["b1"], dres, *rec["w1"], l, "1")
        dhn = send([("ffn1_w_in", dwin.reshape(N_DEV, D, FFN_SHARD)),
                    ("ffn1_w_out", dwout.reshape(N_DEV, FFN_SHARD // 2, D))], l, f"ffn1_{l}", dhn)
        dres, dg = norm_bwd(rec["h1"], row(ffn1_norm[l]), dhn, dres, f"ffn1_norm_bwd_{l}")
        gsmall["ffn1_norm"][l] = dg
    grad_x = dres[None]

    n_repl_rows = 136
    small_rows = 576
    repl_flat = jnp.concatenate([jnp.concatenate([a.reshape(-1) for a in gsmall[n]]) for n in REPL[:-1]]
                                + [d_final.reshape(-1)])
    repl_pack = jnp.pad(repl_flat, (0, n_repl_rows * 128 - repl_flat.shape[0]))
    small_pack = jnp.concatenate([repl_pack] + [a.reshape(-1) for a in gsmall["hyb_dw_w"]]
                                 + [a.reshape(-1) for a in gsmall["gdn_conv_w"]]).reshape(small_rows, 128)
    small_all, = all_gather([small_pack], "small_grads_all_gather")
    g_small = sum_slabs(small_all, "small_grads_sum")

    groups = [[("scatter", layer, s, big.index(n)) for n, layer, s in named] for named, _ in ge_pending]
    sems = [s for _, pair in ge_pending for s in pair]
    srcs_out, lands_out = push_wait(groups, [ge_land[n] for n in big], sems, dres, "grad_wait")
    own = {n: {} for n in big}
    flat_named = [it for named, _ in ge_pending for it in named]
    for (n, layer, _), s in zip(flat_named, srcs_out):
        own[n][layer] = lax.dynamic_index_in_dim(s, me, 0, keepdims=False)
    recv = {}
    for n, land in zip(big, lands_out):
        mine = jnp.stack([own[n][k] for k in range(len(own[n]))])
        recv[n] = lax.dynamic_update_slice(land, mine[None], (me,) + (0,) * mine.ndim)

    out = {}
    for n in big:
        out[n] = _adamw_nd(recv[n], w[n], mom[n], var[n], f"adamw_{n}")
    pk = lambda d: _pack([d[n] for n in REPL], n_repl_rows)
    res = adamw(g_small[:n_repl_rows][None], pk(w), pk(mom), pk(var), "adamw_replicated")
    off = 0
    for n in REPL:
        sz = w[n].size
        out[n] = [r.reshape(-1)[off:off + sz].reshape(w[n].shape) for r in res]
        off += sz
    g_dw = g_small[n_repl_rows:n_repl_rows + 248].reshape(2, CONV_K, CONV_C)
    g_dw = lax.dynamic_slice_in_dim(g_dw, me * (CONV_C // N_DEV), CONV_C // N_DEV, axis=2)
    out["hyb_dw_w"] = _adamw_nd(g_dw[None], w["hyb_dw_w"], mom["hyb_dw_w"], var["hyb_dw_w"], "adamw_hyb_dw_w")
    g_cw = g_small[n_repl_rows + 248:].reshape(2, G_CONV, G_QKV)
    g_cw = lax.dynamic_slice_in_dim(g_cw, me * (G_QKV // N_DEV), G_QKV // N_DEV, axis=2)
    out["gdn_conv_w"] = _adamw_nd(g_cw[None], w["gdn_conv_w"], mom["gdn_conv_w"], var["gdn_conv_w"], "adamw_gdn_conv_w")

    return (loss, grad_x, *[out[n][0] for n in WEIGHTS], *[out[n][1] for n in WEIGHTS],
            *[out[n][2] for n in WEIGHTS], *[out[n][3] for n in WEIGHTS])
```

```python
import functools
import math

import jax
import jax.numpy as jnp
import numpy as np
from jax import lax
from jax.experimental import pallas as pl
from jax.experimental.pallas import tpu as pltpu

F32 = jnp.float32
BF16 = jnp.bfloat16

N_DEV = 8
T = 4096
D = 1024
DEPTH = 4
FFN = 2816
FFN_SHARD = 2 * FFN // N_DEV
FFN_TILES = FFN // FFN_SHARD
EPS = 1e-6

A_HEADS = 8
A_HD = 64
A_W = 512
CONV_C = 512
CONV_K = 31
HYB_IN = 2560
ROPE_THETA = 500000.0
ROT = 16
DILATIONS = (1, 4, 16)
BLK = 128
KPAD = 2048

G_HEADS = 8
G_DK = 128
G_QKV = 3072
G_IN = 4112
G_CONV = 4
CH = 64
GRP = 512
CPG = GRP // CH
N_GRP = T // GRP

ADAM_LR = 0.001
ADAM_B1 = 0.9
ADAM_B2 = 0.999
ADAM_EPS = 1e-08
ADAM_WD = 0.01
ADAM_STEP = 10

VMEM_LIMIT = 56 * 1024 * 1024

HI = lax.Precision.HIGHEST


def _cp(*sem):
    return pltpu.CompilerParams(dimension_semantics=sem, vmem_limit_bytes=VMEM_LIMIT)


def _dot(a, b):
    return jnp.dot(a, b, preferred_element_type=F32)


def _dot_nt(a, b):
    return lax.dot_general(a, b, (((1,), (1,)), ((), ())), preferred_element_type=F32)


def _dot_tn(a, b):
    return lax.dot_general(a, b, (((0,), (0,)), ((), ())), preferred_element_type=F32)


def _sigmoid(x):
    return 1.0 / (1.0 + jnp.exp(-x))


def _dsilu(x, sig):
    return sig * (1.0 + x * (1.0 - sig))


def _rms(x, g):
    rstd = lax.rsqrt(jnp.mean(x * x, axis=-1, keepdims=True) + EPS)
    return x * rstd * g


FFN_TT = 512


def ffn_fwd(h, g_row, w_in, w_out, layer, tag=""):
    def body(h_ref, g_ref, win_ref, wout_ref, hnew_ref, hn_ref, a_ref, b_ref):
        x = h_ref[...]
        hn = _rms(x, g_ref[...]).astype(BF16)
        hn_ref[...] = hn
        acc = None
        for j in range(FFN_TILES):
            a = _dot(hn, win_ref[0, j])
            b = _dot(hn, win_ref[1, j])
            act = a * _sigmoid(a) * b
            a_ref[j] = a.astype(BF16)
            b_ref[j] = b.astype(BF16)
            part = _dot(act.astype(BF16), wout_ref[2 * j:2 * j + 2].reshape(FFN_SHARD, D))
            acc = part if acc is None else acc + part
        hnew_ref[...] = x + 0.5 * acc

    tt = FFN_TT
    resident = pl.Buffered(1)
    return pl.pallas_call(
        body,
        grid=(T // tt,),
        in_specs=[
            pl.BlockSpec((tt, D), lambda i: (i, 0)),
            pl.BlockSpec((1, D), lambda i: (0, 0)),
            pl.BlockSpec((2, FFN_TILES, D, FFN_SHARD), lambda i: (0, 0, 0, 0), pipeline_mode=resident),
            pl.BlockSpec((N_DEV, FFN_SHARD // 2, D), lambda i: (0, 0, 0), pipeline_mode=resident),
        ],
        out_specs=[
            pl.BlockSpec((tt, D), lambda i: (i, 0)),
            pl.BlockSpec((tt, D), lambda i: (i, 0)),
            pl.BlockSpec((FFN_TILES, tt, FFN_SHARD), lambda i: (0, i, 0)),
            pl.BlockSpec((FFN_TILES, tt, FFN_SHARD), lambda i: (0, i, 0)),
        ],
        out_shape=[
            jax.ShapeDtypeStruct((T, D), F32),
            jax.ShapeDtypeStruct((T, D), BF16),
            jax.ShapeDtypeStruct((FFN_TILES, T, FFN_SHARD), BF16),
            jax.ShapeDtypeStruct((FFN_TILES, T, FFN_SHARD), BF16),
        ],
        compiler_params=_cp("parallel"),
        name=f"ffn{tag}_fwd_{layer}",
    )(h, g_row, w_in, w_out)


def ffn_bwd(hn, a, b, dres, w_in, w_out, layer, tag=""):
    tt = FFN_TT
    nt = T // tt

    def body(hn_ref, a_ref, b_ref, dres_ref, win_ref, wout_ref, dhn_ref, dwin_ref, dwout_ref, gin_ref, gout_ref):
        i = pl.program_id(1)
        do = (0.5 * dres_ref[...]).astype(BF16)
        wo = wout_ref[...].reshape(FFN_SHARD, D)
        dact = _dot_nt(do, wo)
        a = a_ref[...].astype(F32)
        b = b_ref[...].astype(F32)
        sig = _sigmoid(a)
        s = a * sig
        act = (s * b).astype(BF16)
        db = (dact * s).astype(BF16)
        da = (dact * b * _dsilu(a, sig)).astype(BF16)
        hn = hn_ref[...]
        gwo = _dot_tn(act, do)
        gwg = _dot_tn(hn, da)
        gwu = _dot_tn(hn, db)

        @pl.when(i == 0)
        def _():
            gout_ref[...] = gwo
            gin_ref[0] = gwg
            gin_ref[1] = gwu

        @pl.when(i > 0)
        def _():
            gout_ref[...] += gwo
            gin_ref[0] += gwg
            gin_ref[1] += gwu

        dhn_ref[...] = (_dot_nt(da, win_ref[0]) + _dot_nt(db, win_ref[1])).astype(BF16)

        @pl.when(i == nt - 1)
        def _():
            dwin_ref[...] = gin_ref[...].astype(BF16)
            dwout_ref[...] = gout_ref[...].astype(BF16)

    return pl.pallas_call(
        body,
        grid=(FFN_TILES, nt),
        in_specs=[
            pl.BlockSpec((tt, D), lambda j, i: (i, 0)),
            pl.BlockSpec((None, tt, FFN_SHARD), lambda j, i: (j, i, 0)),
            pl.BlockSpec((None, tt, FFN_SHARD), lambda j, i: (j, i, 0)),
            pl.BlockSpec((tt, D), lambda j, i: (i, 0)),
            pl.BlockSpec((2, None, D, FFN_SHARD), lambda j, i: (0, j, 0, 0)),
            pl.BlockSpec((2, FFN_SHARD // 2, D), lambda j, i: (j, 0, 0)),
        ],
        out_specs=[
            pl.BlockSpec((None, tt, D), lambda j, i: (j, i, 0)),
            pl.BlockSpec((2, None, D, FFN_SHARD), lambda j, i: (0, j, 0, 0)),
            pl.BlockSpec((None, FFN_SHARD, D), lambda j, i: (j, 0, 0)),
        ],
        out_shape=[
            jax.ShapeDtypeStruct((FFN_TILES, T, D), BF16),
            jax.ShapeDtypeStruct((2, FFN_TILES, D, FFN_SHARD), BF16),
            jax.ShapeDtypeStruct((FFN_TILES, FFN_SHARD, D), BF16),
        ],
        scratch_shapes=[pltpu.VMEM((2, D, FFN_SHARD), F32), pltpu.VMEM((FFN_SHARD, D), F32)],
        compiler_params=_cp("parallel", "arbitrary"),
        name=f"ffn{tag}_bwd_{layer}",
    )(hn, a, b, dres, w_in, w_out)


def _rms_bwd(x, g, dy):
    rstd = lax.rsqrt(jnp.mean(x * x, axis=-1, keepdims=True) + EPS)
    xh = x * rstd
    u = dy * g
    dx = rstd * (u - xh * jnp.mean(u * xh, axis=-1, keepdims=True))
    return dx, jnp.sum(dy * xh, axis=0, keepdims=True)


def norm_bwd(x, g_row, dy_parts, dres, name):
    p = dy_parts.shape[0]
    tt = 512

    def body(x_ref, g_ref, dy_ref, dres_ref, out_ref, dg_ref):
        i = pl.program_id(0)
        dy = dy_ref[0].astype(F32)
        for q in range(1, p):
            dy = dy + dy_ref[q].astype(F32)
        dx, dg = _rms_bwd(x_ref[...], g_ref[...], dy)
        out_ref[...] = dres_ref[...] + dx

        @pl.when(i == 0)
        def _():
            dg_ref[...] = dg

        @pl.when(i > 0)
        def _():
            dg_ref[...] += dg

    return pl.pallas_call(
        body,
        grid=(T // tt,),
        in_specs=[
            pl.BlockSpec((tt, D), lambda i: (i, 0)),
            pl.BlockSpec((1, D), lambda i: (0, 0)),
            pl.BlockSpec((p, tt, D), lambda i: (0, i, 0)),
            pl.BlockSpec((tt, D), lambda i: (i, 0)),
        ],
        out_specs=[pl.BlockSpec((tt, D), lambda i: (i, 0)), pl.BlockSpec((1, D), lambda i: (0, 0))],
        out_shape=[jax.ShapeDtypeStruct((T, D), F32), jax.ShapeDtypeStruct((1, D), F32)],
        compiler_params=_cp("arbitrary"),
        name=name,
    )(x, g_row, dy_parts, dres)


def final_loss(h, g_row, target):
    tt = 512

    def body(h_ref, g_ref, t_ref, dres_ref, dg_ref, loss_ref):
        i = pl.program_id(0)
        x = h_ref[...]
        g = g_ref[...]
        err = _rms(x, g) - t_ref[...]
        part = 0.5 * jnp.sum(jnp.mean(err * err, axis=-1, keepdims=True), axis=0, keepdims=True)
        dx, dg = _rms_bwd(x, g, err * (1.0 / D))
        dres_ref[...] = dx
        part = jnp.broadcast_to(part, loss_ref.shape)

        @pl.when(i == 0)
        def _():
            dg_ref[...] = dg
            loss_ref[...] = part

        @pl.when(i > 0)
        def _():
            dg_ref[...] += dg
            loss_ref[...] += part

    return pl.pallas_call(
        body,
        grid=(T // tt,),
        in_specs=[
            pl.BlockSpec((tt, D), lambda i: (i, 0)),
            pl.BlockSpec((1, D), lambda i: (0, 0)),
            pl.BlockSpec((tt, D), lambda i: (i, 0)),
        ],
        out_specs=[
            pl.BlockSpec((tt, D), lambda i: (i, 0)),
            pl.BlockSpec((1, D), lambda i: (0, 0)),
            pl.BlockSpec((8, 128), lambda i: (0, 0)),
        ],
        out_shape=[
            jax.ShapeDtypeStruct((T, D), F32),
            jax.ShapeDtypeStruct((1, D), F32),
            jax.ShapeDtypeStruct((8, 128), F32),
        ],
        compiler_params=_cp("arbitrary"),
        name="final_loss",
    )(h, g_row, target)


PROJ_TT = 256


def rope_tables(pos_col, invf_row):
    tt = 512

    def body(p_ref, f_ref, c_ref, sm_ref, sp_ref):
        ang = p_ref[...].astype(F32) * f_ref[...]
        lane = lax.broadcasted_iota(jnp.int32, ang.shape, 1) % A_HD
        cs = jnp.cos(ang)
        sn = jnp.sin(ang)
        c_ref[...] = jnp.where(lane < ROT, cs, 1.0)
        sm_ref[...] = jnp.where(lane < ROT // 2, -sn, 0.0)
        sp_ref[...] = jnp.where((lane >= ROT // 2) & (lane < ROT), sn, 0.0)

    spec = pl.BlockSpec((tt, 128), lambda i: (i, 0))
    return pl.pallas_call(
        body,
        grid=(T // tt,),
        in_specs=[pl.BlockSpec((tt, 1), lambda i: (i, 0)), pl.BlockSpec((1, 128), lambda i: (0, 0))],
        out_specs=[spec, spec, spec],
        out_shape=[jax.ShapeDtypeStruct((T, 128), F32)] * 3,
        compiler_params=_cp("parallel"),
        name="rope_tables",
    )(pos_col, invf_row)


def make_rope(positions):
    inv_freq = jnp.power(jnp.float32(ROPE_THETA), -jnp.arange(0, ROT, 2, dtype=F32) / ROT)
    per_head = jnp.concatenate([inv_freq, inv_freq, jnp.zeros((A_HD - ROT,), F32)])
    invf_row = jnp.tile(per_head, 2)[None, :]
    return tuple(rope_tables(positions.reshape(T, 1), invf_row))


def _rope(x, c, sm, sp):
    return x * c + pltpu.roll(x, 128 - ROT // 2, 1) * sm + pltpu.roll(x, ROT // 2, 1) * sp


def _rope_t(dy, c, sm, sp):
    return dy * c + pltpu.roll(dy * sm, ROT // 2, 1) + pltpu.roll(dy * sp, 128 - ROT // 2, 1)


def proj_fwd(h, g_row, w, splits, name, rope=None):
    tt = PROJ_TT
    n = w.shape[1]
    n_rope = 0 if rope is None else 3

    def body(h_ref, g_ref, w_ref, *rest):
        tabs = rest[:n_rope]
        hn_ref = rest[n_rope]
        outs = rest[n_rope + 1:]
        hn = _rms(h_ref[...], g_ref[...]).astype(BF16)
        hn_ref[...] = hn
        for k, ((st, wd), o_ref) in enumerate(zip(splits, outs)):
            if rope is not None and k == 0:
                c, sm, sp = (t[...] for t in tabs)
                for gi in range(wd // 128):
                    r = _dot(hn, w_ref[:, st + 128 * gi:st + 128 * (gi + 1)])
                    if gi < 2 * A_W // 128:
                        r = _rope(r, c, sm, sp)
                    o_ref[:, 128 * gi:128 * (gi + 1)] = r
            else:
                o_ref[...] = _dot(hn, w_ref[:, st:st + wd])

    tab_specs = [pl.BlockSpec((tt, 128), lambda i: (i, 0))] * n_rope
    return pl.pallas_call(
        body,
        grid=(T // tt,),
        in_specs=[
            pl.BlockSpec((tt, D), lambda i: (i, 0)),
            pl.BlockSpec((1, D), lambda i: (0, 0)),
            pl.BlockSpec((D, n), lambda i: (0, 0)),
        ] + tab_specs,
        out_specs=[pl.BlockSpec((tt, D), lambda i: (i, 0))]
        + [pl.BlockSpec((tt, wd), lambda i: (i, 0)) for _, wd in splits],
        out_shape=[jax.ShapeDtypeStruct((T, D), BF16)]
        + [jax.ShapeDtypeStruct((T, wd), F32) for _, wd in splits],
        compiler_params=_cp("parallel"),
        name=name,
    )(h, g_row, w, *(rope or ()))


def proj_bwd_data(x, g_row, w, dparts, splits, dres, name, rope=None, n_rot=0):
    tt = PROJ_TT
    n = w.shape[1]
    n_rope = 0 if rope is None else 3
    k_parts = len(dparts)

    def body(x_ref, g_ref, w_ref, dres_ref, *rest):
        d_refs = rest[:k_parts]
        tabs = rest[k_parts:k_parts + n_rope]
        out_ref, dg_ref = rest[k_parts + n_rope:k_parts + n_rope + 2]
        unrot_refs = rest[k_parts + n_rope + 2:]
        i = pl.program_id(0)
        dhn = jnp.zeros((tt, D), F32)
        for k, ((st, wd), d_ref) in enumerate(zip(splits, d_refs)):
            if k < n_rot:
                c, sm, sp = (t[...] for t in tabs)
                for gi in range(wd // 128):
                    d = _rope_t(d_ref[:, 128 * gi:128 * (gi + 1)], c, sm, sp)
                    unrot_refs[k][:, 128 * gi:128 * (gi + 1)] = d
                    dhn = dhn + _dot_nt(d.astype(BF16), w_ref[:, st + 128 * gi:st + 128 * (gi + 1)])
            else:
                dhn = dhn + _dot_nt(d_ref[...].astype(BF16), w_ref[:, st:st + wd])
        dx, dg = _rms_bwd(x_ref[...], g_ref[...], dhn)
        out_ref[...] = dres_ref[...] + dx

        @pl.when(i == 0)
        def _():
            dg_ref[...] = dg

        @pl.when(i > 0)
        def _():
            dg_ref[...] += dg

    tab_specs = [pl.BlockSpec((tt, 128), lambda i: (i, 0))] * n_rope
    out_specs = [pl.BlockSpec((tt, D), lambda i: (i, 0)), pl.BlockSpec((1, D), lambda i: (0, 0))]
    out_shape = [jax.ShapeDtypeStruct((T, D), F32), jax.ShapeDtypeStruct((1, D), F32)]
    for k in range(n_rot):
        out_specs.append(pl.BlockSpec((tt, splits[k][1]), lambda i: (i, 0)))
        out_shape.append(jax.ShapeDtypeStruct((T, splits[k][1]), F32))
    return pl.pallas_call(
        body,
        grid=(T // tt,),
        in_specs=[
            pl.BlockSpec((tt, D), lambda i: (i, 0)),
            pl.BlockSpec((1, D), lambda i: (0, 0)),
            pl.BlockSpec((D, n), lambda i: (0, 0)),
            pl.BlockSpec((tt, D), lambda i: (i, 0)),
        ] + [pl.BlockSpec((tt, wd), lambda i: (i, 0)) for _, wd in splits] + tab_specs,
        out_specs=out_specs,
        out_shape=out_shape,
        compiler_params=_cp("arbitrary"),
        name=name,
    )(x, g_row, w, dres, *dparts, *(rope or ()))


def mm_tn(x, d, name):
    k = x.shape[1]
    n = d.shape[1]
    wn = n if n <= 512 else 512
    tt = 512

    def body(x_ref, d_ref, o_ref):
        i = pl.program_id(1)
        r = _dot_tn(x_ref[...], d_ref[...].astype(BF16))

        @pl.when(i == 0)
        def _():
            o_ref[...] = r

        @pl.when(i > 0)
        def _():
            o_ref[...] += r

    return pl.pallas_call(
        body,
        grid=(n // wn, T // tt),
        in_specs=[pl.BlockSpec((tt, k), lambda j, i: (i, 0)), pl.BlockSpec((tt, wn), lambda j, i: (i, j))],
        out_specs=pl.BlockSpec((k, wn), lambda j, i: (0, j)),
        out_shape=jax.ShapeDtypeStruct((k, n), F32),
        compiler_params=_cp("parallel", "arbitrary"),
        name=name,
    )(x, d)


CONV_RC = 128
CONV_PAD = 32


def hyb_conv_fwd(u, dw_w, dw_b, name):
    def body(ua_ref, ug_ref, w_ref, b_ref, o_ref, xpad):
        xpad[0:CONV_PAD, :] = jnp.zeros((CONV_PAD, 128), F32)
        xpad[CONV_PAD:, :] = ua_ref[...] * _sigmoid(ug_ref[...])
        for r in range(T // CONV_RC):
            acc = jnp.broadcast_to(b_ref[...], (CONV_RC, 128))
            for j in range(CONV_K):
                acc = acc + w_ref[pl.ds(j, 1), :] * xpad[pl.ds(r * CONV_RC + CONV_PAD - (CONV_K - 1) + j, CONV_RC), :]
            o_ref[r * CONV_RC:(r + 1) * CONV_RC, :] = acc

    nb = CONV_C // 128
    return pl.pallas_call(
        body,
        grid=(nb,),
        in_specs=[
            pl.BlockSpec((T, 128), lambda c: (0, c)),
            pl.BlockSpec((T, 128), lambda c: (0, nb + c)),
            pl.BlockSpec((32, 128), lambda c: (0, c)),
            pl.BlockSpec((1, 128), lambda c: (0, c)),
        ],
        out_specs=pl.BlockSpec((T, 128), lambda c: (0, c)),
        out_shape=jax.ShapeDtypeStruct((T, CONV_C), F32),
        scratch_shapes=[pltpu.VMEM((T + CONV_PAD, 128), F32)],
        compiler_params=_cp("parallel"),
        name=name,
    )(u, u, dw_w, dw_b)


def hyb_conv_bwd(dc, u, dw_w, name):
    def body(dc_ref, ua_ref, ug_ref, w_ref, da_ref, dgate_ref, dw_ref, db_ref, xpad, dcpad, dwacc):
        ua = ua_ref[...]
        sig = _sigmoid(ug_ref[...])
        xpad[0:CONV_PAD, :] = jnp.zeros((CONV_PAD, 128), F32)
        xpad[CONV_PAD:, :] = ua * sig
        dcpad[0:T, :] = dc_ref[...]
        dcpad[T:, :] = jnp.zeros((CONV_PAD, 128), F32)
        dwacc[...] = jnp.zeros_like(dwacc)
        dbacc = jnp.zeros((8, 128), F32)
        for r in range(T // CONV_RC):
            r0 = r * CONV_RC
            dcr = dc_ref[r0:r0 + CONV_RC, :]
            dbacc = dbacc + dcr.reshape(CONV_RC // 8, 8, 128).sum(axis=0)
            dglu = jnp.zeros((CONV_RC, 128), F32)
            for j in range(CONV_K):
                dglu = dglu + w_ref[pl.ds(j, 1), :] * dcpad[pl.ds(r0 + (CONV_K - 1) - j, CONV_RC), :]
                prod = dcr * xpad[pl.ds(r0 + CONV_PAD - (CONV_K - 1) + j, CONV_RC), :]
                dwacc[8 * j:8 * j + 8, :] += prod.reshape(CONV_RC // 8, 8, 128).sum(axis=0)
            sg = sig[r0:r0 + CONV_RC, :]
            da_ref[r0:r0 + CONV_RC, :] = dglu * sg
            dgate_ref[r0:r0 + CONV_RC, :] = dglu * ua[r0:r0 + CONV_RC, :] * sg * (1.0 - sg)
        for j in range(CONV_K):
            dw_ref[pl.ds(j, 1), :] = jnp.sum(dwacc[8 * j:8 * j + 8, :], axis=0, keepdims=True)
        dw_ref[pl.ds(CONV_K, 1), :] = jnp.zeros((1, 128), F32)
        db_ref[...] = jnp.sum(dbacc, axis=0, keepdims=True)

    nb = CONV_C // 128
    col = pl.BlockSpec((T, 128), lambda c: (0, c))
    return pl.pallas_call(
        body,
        grid=(nb,),
        in_specs=[col, col, pl.BlockSpec((T, 128), lambda c: (0, nb + c)), pl.BlockSpec((32, 128), lambda c: (0, c))],
        out_specs=[col, col, pl.BlockSpec((32, 128), lambda c: (0, c)), pl.BlockSpec((1, 128), lambda c: (0, c))],
        out_shape=[
            jax.ShapeDtypeStruct((T, CONV_C), F32),
            jax.ShapeDtypeStruct((T, CONV_C), F32),
            jax.ShapeDtypeStruct((32, CONV_C), F32),
            jax.ShapeDtypeStruct((1, CONV_C), F32),
        ],
        scratch_shapes=[
            pltpu.VMEM((T + CONV_PAD, 128), F32),
            pltpu.VMEM((T + CONV_PAD, 128), F32),
            pltpu.VMEM((8 * 32, 128), F32),
        ],
        compiler_params=_cp("parallel"),
        name=name,
    )(dc, u, u, dw_w)


ATT_SCALE = A_HD ** -0.5
N_BLK = T // BLK


def _att_masks():
    i = lax.broadcasted_iota(jnp.int32, (BLK, 2 * BLK), 0)
    j = lax.broadcasted_iota(jnp.int32, (BLK, 2 * BLK), 1)
    band = (j >= i) & (j <= i + BLK)
    i1 = lax.broadcasted_iota(jnp.int32, (BLK, BLK), 0)
    j1 = lax.broadcasted_iota(jnp.int32, (BLK, BLK), 1)
    return band, j1 <= i1


def _att_rows(d, t, first):
    if first:
        base = t
        return pl.ds(base, BLK, stride=d), pl.ds(base, BLK, stride=d)
    c = t % d
    n = t // d + 1
    base = c + (BLK * d) * n
    return pl.ds(base, BLK, stride=d), pl.ds(base - BLK * d, 2 * BLK, stride=d)


def _loop_pairs(n, block, per=2):
    def several(i, carry):
        for k in range(per):
            block(per * i + k, carry)
        return carry

    if n >= per:
        lax.fori_loop(0, n // per, several, 0)
    for t in range(n - n % per, n):
        block(t, 0)


def attn_fwd(qkv, name):
    def body(q_ref, k_ref, v_ref, o_ref, lse_ref, og, lg):
        band, tri = _att_masks()
        head0 = lax.broadcasted_iota(jnp.int32, (BLK, 128), 1) < A_HD
        for g, d in enumerate(DILATIONS):
            def block(t, carry, first, g=g, d=d):
                rq, rk = _att_rows(d, t, first)
                q2 = q_ref[rq, :]
                k2 = k_ref[rk, :].astype(BF16)
                v2 = v_ref[rk, :].astype(BF16)
                o_e, l_e = [], []
                for e in range(2):
                    qe = jnp.where(head0 if e == 0 else ~head0, q2, 0.0).astype(BF16)
                    s = _dot_nt(qe, k2) * ATT_SCALE
                    s = jnp.where(tri if first else band, s, -jnp.inf)
                    m = jnp.max(s, axis=-1, keepdims=True)
                    p = jnp.exp(s - m)
                    den = jnp.sum(p, axis=-1, keepdims=True)
                    o_e.append(_dot(p.astype(BF16), v2) / den)
                    l_e.append(m + jnp.log(den))
                og[g, rq, :] = jnp.where(head0, o_e[0], o_e[1])
                lg[g, rq, :] = jnp.where(head0, l_e[0], l_e[1])
                return carry

            _loop_pairs(d, functools.partial(block, first=True), per=4)
            _loop_pairs(N_BLK - d, functools.partial(block, first=False), per=4)
        rc = 256
        for r in range(T // rc):
            rows = pl.ds(r * rc, rc)
            l0, l1, l2 = lg[0, rows, :], lg[1, rows, :], lg[2, rows, :]
            m = jnp.maximum(jnp.maximum(l0, l1), l2)
            e0, e1, e2 = jnp.exp(l0 - m), jnp.exp(l1 - m), jnp.exp(l2 - m)
            z = e0 + e1 + e2
            o_ref[rows, :] = (e0 / z) * og[0, rows, :] + (e1 / z) * og[1, rows, :] + (e2 / z) * og[2, rows, :]
            lse_ref[rows, :] = m + jnp.log(z)

    npair = A_HEADS // 2
    col = lambda off: pl.BlockSpec((T, 128), lambda p: (0, off + p))
    return pl.pallas_call(
        body,
        grid=(npair,),
        in_specs=[col(0), col(npair), col(2 * npair)],
        out_specs=[col(0), col(0)],
        out_shape=[jax.ShapeDtypeStruct((T, A_W), F32), jax.ShapeDtypeStruct((T, A_W), F32)],
        scratch_shapes=[pltpu.VMEM((3, T, 128), F32), pltpu.VMEM((3, T, 128), F32)],
        compiler_params=_cp("parallel"),
        name=name,
    )(qkv, qkv, qkv)


def attn_bwd(qkv, o, lse, do, name):
    def body(q_ref, k_ref, v_ref, o_ref, lse_ref, do_ref, dq_ref, dk_ref, dv_ref):
        band, tri = _att_masks()
        head0 = lax.broadcasted_iota(jnp.int32, (BLK, 128), 1) < A_HD
        head0k = lax.broadcasted_iota(jnp.int32, (2 * BLK, 128), 1) < A_HD
        dq_ref[...] = jnp.zeros_like(dq_ref)
        dk_ref[...] = jnp.zeros_like(dk_ref)
        dv_ref[...] = jnp.zeros_like(dv_ref)
        for d in DILATIONS:
            def block(t, carry, first, d=d):
                rq, rk = _att_rows(d, t, first)
                q2 = q_ref[rq, :]
                k2 = k_ref[rk, :].astype(BF16)
                v2 = v_ref[rk, :].astype(BF16)
                do2 = do_ref[rq, :]
                l2 = lse_ref[rq, :]
                prod = do2 * o_ref[rq, :]
                q2b = q2.astype(BF16)
                do2b = do2.astype(BF16)
                dq_e, dk_e, dv_e = [], [], []
                for e in range(2):
                    he = head0 if e == 0 else ~head0
                    qe = jnp.where(he, q2, 0.0).astype(BF16)
                    doe = jnp.where(he, do2, 0.0).astype(BF16)
                    l = l2[:, A_HD * e:A_HD * e + 1]
                    dd = jnp.sum(jnp.where(he, prod, 0.0), axis=-1, keepdims=True)
                    s = _dot_nt(qe, k2) * ATT_SCALE
                    p = jnp.where(tri if first else band, jnp.exp(s - l), 0.0)
                    dp = _dot_nt(doe, v2)
                    ds = (p * (dp - dd) * ATT_SCALE).astype(BF16)
                    dq_e.append(_dot(ds, k2))
                    dk_e.append(_dot_tn(ds, q2b))
                    dv_e.append(_dot_tn(p.astype(BF16), do2b))
                hk = head0 if first else head0k
                dq_ref[rq, :] += jnp.where(head0, dq_e[0], dq_e[1])
                dk_ref[rk, :] += jnp.where(hk, dk_e[0], dk_e[1])
                dv_ref[rk, :] += jnp.where(hk, dv_e[0], dv_e[1])
                return carry

            _loop_pairs(d, functools.partial(block, first=True))
            _loop_pairs(N_BLK - d, functools.partial(block, first=False))

    npair = A_HEADS // 2
    col = lambda off: pl.BlockSpec((T, 128), lambda p: (0, off + p))
    return pl.pallas_call(
        body,
        grid=(npair,),
        in_specs=[col(0), col(npair), col(2 * npair), col(0), col(0), col(0)],
        out_specs=[col(0), col(0), col(0)],
        out_shape=[jax.ShapeDtypeStruct((T, A_W), F32)] * 3,
        compiler_params=_cp("parallel"),
        name=name,
    )(qkv, qkv, qkv, o, lse, do)


def _ln_silu(x, g, b):
    mu = jnp.mean(x, axis=-1, keepdims=True)
    xc = x - mu
    rstd = lax.rsqrt(jnp.mean(xc * xc, axis=-1, keepdims=True) + EPS)
    xh = xc * rstd
    y = xh * g + b
    sig = _sigmoid(y)
    return y * sig, (xh, rstd, y, sig)


def hyb_out_fwd(h, attn, cpre, ln_g, ln_b, w_out, name):
    tt = 512

    def body(h_ref, a_ref, c_ref, g_ref, b_ref, w_ref, hnew_ref, cat_ref):
        cn, _ = _ln_silu(c_ref[...], g_ref[...], b_ref[...])
        ab = a_ref[...].astype(BF16)
        cb = cn.astype(BF16)
        cat_ref[:, 0:A_W] = ab
        cat_ref[:, A_W:D] = cb
        hnew_ref[...] = h_ref[...] + _dot(ab, w_ref[0:A_W, :]) + _dot(cb, w_ref[A_W:D, :])

    half = pl.BlockSpec((tt, A_W), lambda i: (i, 0))
    vec = pl.BlockSpec((1, CONV_C), lambda i: (0, 0))
    full = pl.BlockSpec((tt, D), lambda i: (i, 0))
    return pl.pallas_call(
        body,
        grid=(T // tt,),
        in_specs=[full, half, half, vec, vec, pl.BlockSpec((D, D), lambda i: (0, 0))],
        out_specs=[full, full],
        out_shape=[jax.ShapeDtypeStruct((T, D), F32), jax.ShapeDtypeStruct((T, D), BF16)],
        compiler_params=_cp("parallel"),
        name=name,
    )(h, attn, cpre, ln_g, ln_b, w_out)


def hyb_out_bwd(dres, cpre, ln_g, ln_b, w_out, name):
    tt = 512

    def body(d_ref, c_ref, g_ref, b_ref, w_ref, da_ref, dc_ref, dg_ref, db_ref):
        i = pl.program_id(0)
        db16 = d_ref[...].astype(BF16)
        da_ref[...] = _dot_nt(db16, w_ref[0:A_W, :])
        dcn = _dot_nt(db16, w_ref[A_W:D, :])
        g = g_ref[...]
        _, (xh, rstd, y, sig) = _ln_silu(c_ref[...], g, b_ref[...])
        dy = dcn * _dsilu(y, sig)
        dxh = dy * g
        dc_ref[...] = rstd * (dxh - jnp.mean(dxh, axis=-1, keepdims=True)
                              - xh * jnp.mean(dxh * xh, axis=-1, keepdims=True))
        dg = jnp.sum(dy * xh, axis=0, keepdims=True)
        db = jnp.sum(dy, axis=0, keepdims=True)

        @pl.when(i == 0)
        def _():
            dg_ref[...] = dg
            db_ref[...] = db

        @pl.when(i > 0)
        def _():
            dg_ref[...] += dg
            db_ref[...] += db

    half = pl.BlockSpec((tt, A_W), lambda i: (i, 0))
    vec = pl.BlockSpec((1, CONV_C), lambda i: (0, 0))
    return pl.pallas_call(
        body,
        grid=(T // tt,),
        in_specs=[pl.BlockSpec((tt, D), lambda i: (i, 0)), half, vec, vec, pl.BlockSpec((D, D), lambda i: (0, 0))],
        out_specs=[half, half, vec, vec],
        out_shape=[
            jax.ShapeDtypeStruct((T, A_W), F32),
            jax.ShapeDtypeStruct((T, CONV_C), F32),
            jax.ShapeDtypeStruct((1, CONV_C), F32),
            jax.ShapeDtypeStruct((1, CONV_C), F32),
        ],
        compiler_params=_cp("arbitrary"),
        name=name,
    )(dres, cpre, ln_g, ln_b, w_out)


def hybrid_fwd(h, g_row, w_in, dw_w, dw_b, ln_g, ln_b, w_out, rope, tag):
    hn, qkv, u = proj_fwd(h, g_row, w_in, [(0, 3 * A_W), (3 * A_W, 2 * CONV_C)], f"hyb_proj_{tag}", rope=rope)
    cpre = hyb_conv_fwd(u, dw_w, dw_b, f"hyb_conv_{tag}")
    attn, lse = attn_fwd(qkv, f"attn_fwd_{tag}")
    hnew, cat = hyb_out_fwd(h, attn, cpre, ln_g, ln_b, w_out, f"hyb_out_{tag}")
    return hnew, (h, hn, qkv, u, cpre, attn, lse, cat)


def hybrid_bwd(dres, saved, g_row, w_in, dw_w, ln_g, ln_b, w_out, rope, tag):
    h, hn, qkv, u, cpre, attn, lse, cat = saved
    d_attn, d_cpre, d_lng, d_lnb = hyb_out_bwd(dres, cpre, ln_g, ln_b, w_out, f"hyb_out_bwd_{tag}")
    d_wout = mm_tn(cat, dres, f"hyb_wout_grad_{tag}")
    d_a, d_gate, d_dw, d_db = hyb_conv_bwd(d_cpre, u, dw_w, f"hyb_conv_bwd_{tag}")
    dq, dk, dv = attn_bwd(qkv, attn, lse, d_attn, f"attn_bwd_{tag}")
    splits = [(0, A_W), (A_W, A_W), (2 * A_W, A_W), (3 * A_W, CONV_C), (3 * A_W + CONV_C, CONV_C)]
    dres_new, d_norm, dq_u, dk_u = proj_bwd_data(
        h, g_row, w_in, [dq, dk, dv, d_a, d_gate], splits, dres, f"hyb_proj_bwd_{tag}", rope=rope, n_rot=2)
    parts = [dq_u, dk_u, dv, d_a, d_gate]
    d_win = jnp.concatenate([mm_tn(hn, p, f"hyb_win_grad_{tag}_{k}") for k, p in enumerate(parts)], axis=1)
    return dres_new, dict(norm=d_norm, w_in=d_win, dw_w=d_dw[:CONV_K], dw_b=d_db, ln_g=d_lng, ln_b=d_lnb, w_out=d_wout)


G_SCALE = G_DK ** -0.5
GP_RC = 256
GP_PAD = 8


def gdn_prep_fwd(x, conv_w, name):
    def body(x_ref, w_ref, o_ref, xpad):
        cb = pl.program_id(0)
        xpad[0:GP_PAD, :] = jnp.zeros((GP_PAD, 128), F32)
        xpad[GP_PAD:, :] = x_ref[...]
        for r in range(T // GP_RC):
            r0 = r * GP_RC
            y = jnp.zeros((GP_RC, 128), F32)
            for j in range(G_CONV):
                y = y + w_ref[pl.ds(j, 1), :] * xpad[pl.ds(r0 + GP_PAD - (G_CONV - 1) + j, GP_RC), :]
            s = y * _sigmoid(y)
            n = lax.rsqrt(jnp.sum(s * s, axis=-1, keepdims=True) + EPS)
            o_ref[r0:r0 + GP_RC, :] = s * jnp.where(cb < 2 * G_HEADS, n, 1.0)

    nb = G_QKV // 128
    return pl.pallas_call(
        body,
        grid=(nb,),
        in_specs=[pl.BlockSpec((T, 128), lambda c: (0, c)), pl.BlockSpec((G_CONV, 128), lambda c: (0, c))],
        out_specs=pl.BlockSpec((T, 128), lambda c: (0, c)),
        out_shape=jax.ShapeDtypeStruct((T, G_QKV), F32),
        scratch_shapes=[pltpu.VMEM((T + GP_PAD, 128), F32)],
        compiler_params=_cp("parallel"),
        name=name,
    )(x, conv_w)


def gdn_prep_bwd(dout, x, conv_w, part, l2, name):
    def body(d_ref, x_ref, w_ref, dx_ref, dw_ref, xpad, dypad, dwacc):
        xpad[0:GP_PAD, :] = jnp.zeros((GP_PAD, 128), F32)
        xpad[GP_PAD:, :] = x_ref[...]
        dypad[T:, :] = jnp.zeros((GP_PAD, 128), F32)
        dwacc[...] = jnp.zeros_like(dwacc)
        for r in range(T // GP_RC):
            r0 = r * GP_RC
            y = jnp.zeros((GP_RC, 128), F32)
            xs = []
            for j in range(G_CONV):
                xj = xpad[pl.ds(r0 + GP_PAD - (G_CONV - 1) + j, GP_RC), :]
                xs.append(xj)
                y = y + w_ref[pl.ds(j, 1), :] * xj
            sig = _sigmoid(y)
            s = y * sig
            d = d_ref[r0:r0 + GP_RC, :]
            if l2:
                n = lax.rsqrt(jnp.sum(s * s, axis=-1, keepdims=True) + EPS)
                out = s * n
                d = n * (d - out * jnp.sum(d * out, axis=-1, keepdims=True))
            dy = d * _dsilu(y, sig)
            dypad[r0:r0 + GP_RC, :] = dy
            for j in range(G_CONV):
                dwacc[8 * j:8 * j + 8, :] += (dy * xs[j]).reshape(GP_RC // 8, 8, 128).sum(axis=0)
        for r in range(T // GP_RC):
            r0 = r * GP_RC
            dx = jnp.zeros((GP_RC, 128), F32)
            for j in range(G_CONV):
                dx = dx + w_ref[pl.ds(j, 1), :] * dypad[pl.ds(r0 + (G_CONV - 1) - j, GP_RC), :]
            dx_ref[r0:r0 + GP_RC, :] = dx
        for j in range(G_CONV):
            dw_ref[pl.ds(j, 1), :] = jnp.sum(dwacc[8 * j:8 * j + 8, :], axis=0, keepdims=True)

    nb = G_HEADS
    off = part * nb
    col = pl.BlockSpec((T, 128), lambda c: (0, c))
    return pl.pallas_call(
        body,
        grid=(nb,),
        in_specs=[col, pl.BlockSpec((T, 128), lambda c: (0, off + c)), pl.BlockSpec((G_CONV, 128), lambda c: (0, off + c))],
        out_specs=[col, pl.BlockSpec((G_CONV, 128), lambda c: (0, c))],
        out_shape=[jax.ShapeDtypeStruct((T, G_HEADS * G_DK), F32), jax.ShapeDtypeStruct((G_CONV, G_HEADS * G_DK), F32)],
        scratch_shapes=[
            pltpu.VMEM((T + GP_PAD, 128), F32),
            pltpu.VMEM((T + GP_PAD, 128), F32),
            pltpu.VMEM((8 * G_CONV, 128), F32),
        ],
        compiler_params=_cp("parallel"),
        name=name,
    )(dout, x, conv_w)


def _seg_cumsum(x, reverse=False):
    row = lax.broadcasted_iota(jnp.int32, x.shape, 0) % CH
    s = 1
    while s < CH:
        if reverse:
            x = x + jnp.where(row < CH - s, pltpu.roll(x, x.shape[0] - s, 0), 0.0)
        else:
            x = x + jnp.where(row >= s, pltpu.roll(x, s, 0), 0.0)
        s *= 2
    return x


def _gdn_gates(ba_ref, alog_ref, dt_ref, h):
    ba = ba_ref[...]
    lane = lax.broadcasted_iota(jnp.int32, ba.shape, 1)
    b_col = jnp.sum(jnp.where(lane == h, ba, 0.0), axis=1, keepdims=True)
    a_col = jnp.sum(jnp.where(lane == G_HEADS + h, ba, 0.0), axis=1, keepdims=True)
    lane8 = lax.broadcasted_iota(jnp.int32, (1, G_HEADS), 1)
    alog = jnp.sum(jnp.where(lane8 == h, alog_ref[...], 0.0), axis=1, keepdims=True)
    dt = jnp.sum(jnp.where(lane8 == h, dt_ref[...], 0.0), axis=1, keepdims=True)
    beta = _sigmoid(b_col)
    xa = a_col + dt
    softplus = jnp.maximum(xa, 0.0) + jnp.log(1.0 + jnp.exp(-jnp.abs(xa)))
    ea = jnp.exp(alog)
    return beta, -ea * softplus, xa, ea


def _chunk_masks():
    i = lax.broadcasted_iota(jnp.int32, (CH, CH), 0)
    j = lax.broadcasted_iota(jnp.int32, (CH, CH), 1)
    return i >= j, i > j, i, j


def _decay(gcc, causal):
    gm = gcc[:, 0:CH]
    return jnp.where(causal, jnp.exp(jnp.minimum(gm - gm.T, 0.0)), 0.0)


def _split(a):
    hi = a.astype(BF16)
    return hi, (a - hi.astype(F32)).astype(BF16)


def _dot3(a, b):
    ah, al = _split(a)
    bh, bl = _split(b)
    return _dot(ah, bh) + (_dot(ah, bl) + _dot(al, bh))


def _unit_lower_inverse(lms, i, j):
    eye = jnp.where(i == j, 1.0, 0.0)
    ms = [None] * len(lms)
    b = 1
    while b < CH:
        pair = ((i // (2 * b)) == (j // (2 * b))) & ((i // b) % 2 == 1) & ((j // b) % 2 == 0)
        lbs = [jnp.where(pair, lm, 0.0) for lm in lms]
        if b == 1:
            ms = [eye - lb for lb in lbs]
        else:
            ts = [_dot3(m, lb) for m, lb in zip(ms, lbs)]
            ms = [m - _dot3(t, m) for m, t in zip(ms, ts)]
        b *= 2
    return ms


def gdn_local_fwd(qkv, ba, alog, dtb, name):
    def body(q_ref, k_ref, v_ref, ba_ref, al_ref, dt_ref, u_ref, w_ref, qd_ref, kd_ref, at_ref, el_ref, ti_ref, gcs):
        h = pl.program_id(1)
        beta, g, _, _ = _gdn_gates(ba_ref, al_ref, dt_ref, h)
        gc = _seg_cumsum(jnp.broadcast_to(g, (GRP, 128)))
        gcs[...] = gc
        causal, strict, i, j = _chunk_masks()
        lms = []
        for c in range(CPG):
            r = slice(c * CH, (c + 1) * CH)
            q, k = q_ref[r, :], k_ref[r, :]
            gcc = gc[r, :]
            ec = jnp.exp(gcc)
            gl = gcs[pl.ds(c * CH + CH - 1, 1), :]
            dm = _decay(gcc, causal)
            kbf = k.astype(BF16)
            a1 = _dot_nt((k * beta[r, :]).astype(BF16), kbf)
            lms.append(jnp.where(strict, a1 * dm, 0.0))
            qs = q * G_SCALE
            qd_ref[r, :] = (qs * ec).astype(BF16)
            kd_ref[r, :] = (k * jnp.exp(gl - gcc)).astype(BF16)
            at_ref[r, :] = (_dot_nt(qs.astype(BF16), kbf) * dm).astype(BF16)
            el_ref[pl.ds(c, 1), :] = jnp.exp(gl)
        tinvs = _unit_lower_inverse(lms, i, j)
        for c in range(CPG):
            r = slice(c * CH, (c + 1) * CH)
            bt = beta[r, :]
            tb = tinvs[c].astype(BF16)
            u_ref[r, :] = _dot(tb, (v_ref[r, :] * bt).astype(BF16))
            w_ref[r, :] = _dot(tb, (k_ref[r, :] * bt * jnp.exp(gc[r, :])).astype(BF16)).astype(BF16)
            ti_ref[r, :] = tinvs[c]

    hd = lambda off: pl.BlockSpec((GRP, 128), lambda i, h: (i, off + h))
    vec = pl.BlockSpec((1, G_HEADS), lambda i, h: (0, 0))
    sq = pl.BlockSpec((None, GRP, CH), lambda i, h: (h, i, 0))
    return pl.pallas_call(
        body,
        grid=(N_GRP, G_HEADS),
        in_specs=[hd(0), hd(G_HEADS), hd(2 * G_HEADS), pl.BlockSpec((GRP, 2 * G_HEADS), lambda i, h: (i, 0)), vec, vec],
        out_specs=[hd(0), hd(0), hd(0), hd(0), sq, pl.BlockSpec((None, CPG, 128), lambda i, h: (h, i, 0)), sq],
        out_shape=[
            jax.ShapeDtypeStruct((T, D), F32),
            jax.ShapeDtypeStruct((T, D), BF16),
            jax.ShapeDtypeStruct((T, D), BF16),
            jax.ShapeDtypeStruct((T, D), BF16),
            jax.ShapeDtypeStruct((G_HEADS, T, CH), BF16),
            jax.ShapeDtypeStruct((G_HEADS, T // CH, 128), F32),
            jax.ShapeDtypeStruct((G_HEADS, T, CH), F32),
        ],
        scratch_shapes=[pltpu.VMEM((GRP, 128), F32)],
        compiler_params=_cp("parallel", "parallel"),
        name=name,
    )(qkv, qkv, qkv, ba, alog, dtb)


def gdn_rec_fwd(u, w, qd, kd, at, el, name):
    def body(u_ref, w_ref, qd_ref, kd_ref, at_ref, el_ref, o_ref, vn_ref, st_ref, s_scr):
        @pl.when(pl.program_id(0) == 0)
        def _():
            s_scr[...] = jnp.zeros_like(s_scr)

        for c in range(CPG):
            r = slice(c * CH, (c + 1) * CH)
            for h in range(G_HEADS):
                ln = slice(h * 128, (h + 1) * 128)
                s = s_scr[h]
                st_ref[h, c] = s
                sb = s.astype(BF16)
                vn = (u_ref[r, ln] - _dot(w_ref[r, ln], sb)).astype(BF16)
                o_ref[r, ln] = _dot(qd_ref[r, ln], sb) + _dot(at_ref[h, r, :], vn)
                s_scr[h] = s * el_ref[h, pl.ds(c, 1), :] + _dot_tn(kd_ref[r, ln], vn)
                vn_ref[r, ln] = vn

    row = pl.BlockSpec((GRP, D), lambda i: (i, 0))
    return pl.pallas_call(
        body,
        grid=(N_GRP,),
        in_specs=[row, row, row, row, pl.BlockSpec((G_HEADS, GRP, CH), lambda i: (0, i, 0)),
                  pl.BlockSpec((G_HEADS, CPG, 128), lambda i: (0, i, 0))],
        out_specs=[row, row, pl.BlockSpec((G_HEADS, CPG, 128, 128), lambda i: (0, i, 0, 0))],
        out_shape=[
            jax.ShapeDtypeStruct((T, D), F32),
            jax.ShapeDtypeStruct((T, D), BF16),
            jax.ShapeDtypeStruct((G_HEADS, T // CH, 128, 128), F32),
        ],
        scratch_shapes=[pltpu.VMEM((G_HEADS, 128, 128), F32)],
        compiler_params=_cp("arbitrary"),
        name=name,
    )(u, w, qd, kd, at, el)


def gdn_rec_bwd(do, w, qd, kd, at, el, vn, st, name):
    def body(do_ref, w_ref, qd_ref, kd_ref, at_ref, el_ref, vn_ref, st_ref,
             du_ref, dw_ref, dqd_ref, dkd_ref, dat_ref, del_ref, ds_scr):
        @pl.when(pl.program_id(0) == 0)
        def _():
            ds_scr[...] = jnp.zeros_like(ds_scr)

        for c in reversed(range(CPG)):
            r = slice(c * CH, (c + 1) * CH)
            for h in range(G_HEADS):
                ln = slice(h * 128, (h + 1) * 128)
                ds = ds_scr[h]
                dsb = ds.astype(BF16)
                sn = st_ref[h, c]
                snb = sn.astype(BF16)
                dob = do_ref[r, ln].astype(BF16)
                vnb = vn_ref[r, ln]
                dvn = (_dot(kd_ref[r, ln], dsb) + _dot_tn(at_ref[h, r, :], dob)).astype(BF16)
                du_ref[r, ln] = dvn
                dkd_ref[r, ln] = _dot_nt(vnb, dsb)
                tot = jnp.sum(jnp.sum(ds * sn, axis=1, keepdims=True), axis=0, keepdims=True)
                del_ref[h, pl.ds(c, 1), :] = jnp.broadcast_to(tot, (1, 128))
                dqd_ref[r, ln] = _dot_nt(dob, snb)
                dat_ref[h, r, :] = _dot_nt(dob, vnb)
                dw_ref[r, ln] = (-_dot_nt(dvn, snb)).astype(BF16)
                ds_scr[h] = ds * el_ref[h, pl.ds(c, 1), :] + _dot_tn(qd_ref[r, ln], dob) - _dot_tn(w_ref[r, ln], dvn)

    last = N_GRP - 1
    row = pl.BlockSpec((GRP, D), lambda i: (last - i, 0))
    sq = pl.BlockSpec((G_HEADS, GRP, CH), lambda i: (0, last - i, 0))
    sc = pl.BlockSpec((G_HEADS, CPG, 128), lambda i: (0, last - i, 0))
    return pl.pallas_call(
        body,
        grid=(N_GRP,),
        in_specs=[row, row, row, row, sq, sc, row, pl.BlockSpec((G_HEADS, CPG, 128, 128), lambda i: (0, last - i, 0, 0))],
        out_specs=[row, row, row, row, sq, sc],
        out_shape=[
            jax.ShapeDtypeStruct((T, D), BF16),
            jax.ShapeDtypeStruct((T, D), BF16),
            jax.ShapeDtypeStruct((T, D), F32),
            jax.ShapeDtypeStruct((T, D), F32),
            jax.ShapeDtypeStruct((G_HEADS, T, CH), F32),
            jax.ShapeDtypeStruct((G_HEADS, T // CH, 128), F32),
        ],
        scratch_shapes=[pltpu.VMEM((G_HEADS, 128, 128), F32)],
        compiler_params=_cp("arbitrary"),
        name=name,
    )(do, w, qd, kd, at, el, vn, st)


def gdn_local_bwd(qkv, ba, alog, dtb, tinv, du, dw, dqd, dkd, dat, dl, name):
    def body(q_ref, k_ref, v_ref, ba_ref, al_ref, dt_ref, ti_ref, du_ref, dw_ref, dqd_ref, dkd_ref, dat_ref, dl_ref,
             dq_ref, dk_ref, dv_ref, dba_ref, dal_ref, ddt_ref, gcs):
        gi = pl.program_id(0)
        h = pl.program_id(1)
        beta, g, xa, ea = _gdn_gates(ba_ref, al_ref, dt_ref, h)
        gc = _seg_cumsum(jnp.broadcast_to(g, (GRP, 128)))
        gcs[...] = gc
        causal, strict, _, _ = _chunk_masks()
        dgc_l, dgl_l, dbeta_l, state = [], [], [], []
        for c in range(CPG):
            r = slice(c * CH, (c + 1) * CH)
            q, k, v = q_ref[r, :], k_ref[r, :], v_ref[r, :]
            bt = beta[r, :]
            gcc = gc[r, :]
            ec = jnp.exp(gcc)
            gl = gcs[pl.ds(c * CH + CH - 1, 1), :]
            f2 = jnp.exp(gl - gcc)
            elc = jnp.exp(gl)
            dm = _decay(gcc, causal)
            qs = q * G_SCALE
            kb = k * bt
            vb = v * bt
            kbe = kb * ec
            kbf, kbb, qsb = k.astype(BF16), kb.astype(BF16), qs.astype(BF16)
            a1 = _dot_nt(kbb, kbf)
            qk = _dot_nt(qsb, kbf)
            ti = ti_ref[r, :]
            tb = ti.astype(BF16)
            du_c, dw_c = du_ref[r, :], dw_ref[r, :]
            dqd_c, dkd_c, dat_c = dqd_ref[r, :], dkd_ref[r, :], dat_ref[r, :]

            dqs = dqd_c * ec
            d_e = jnp.sum(dqd_c * qs, axis=1, keepdims=True)
            dk = dkd_c * f2
            tcol = jnp.sum(dkd_c * k, axis=1, keepdims=True) * f2[:, 0:1]
            dgl = jnp.sum(tcol, axis=0, keepdims=True) + dl_ref[pl.ds(c, 1), 0:1] * elc[:, 0:1]
            dgc = -tcol
            dqk = (dat_c * dm).astype(BF16)
            d_d = dat_c * qk
            dqs = dqs + _dot(dqk, kbf)
            dk = dk + _dot_tn(dqk, qsb)
            dtinv = _dot_nt(du_c, vb.astype(BF16)) + _dot_nt(dw_c, kbe.astype(BF16))
            dvb = _dot_tn(tb, du_c)
            dkbe = _dot_tn(tb, dw_c)
            dq_ref[r, :] = dqs * G_SCALE
            state.append((ti.T, dtinv, dm, a1, dkbe, dvb, d_d, dk, d_e, dgc, dgl))

        xs = [_dot3(st[0], st[1]) for st in state]
        dlms = [jnp.where(strict, -_dot3(x, st[0]), 0.0) for x, st in zip(xs, state)]

        for c in range(CPG):
            r = slice(c * CH, (c + 1) * CH)
            _, _, dm, a1, dkbe, dvb, d_d, dk, d_e, dgc, dgl = state[c]
            dlm = dlms[c]
            k, v = k_ref[r, :], v_ref[r, :]
            bt = beta[r, :]
            ec = jnp.exp(gc[r, :])
            kb = k * bt
            kbf, kbb = k.astype(BF16), kb.astype(BF16)
            da1 = (dlm * dm).astype(BF16)
            d_d = d_d + dlm * a1
            dkb = _dot(da1, kbf) + dkbe * ec
            dk = dk + _dot_tn(da1, kbb)
            d_e = d_e + jnp.sum(dkbe * kb, axis=1, keepdims=True)
            dk = dk + dkb * bt
            dbeta_l.append(jnp.sum(dkb * k, axis=1, keepdims=True) + jnp.sum(dvb * v, axis=1, keepdims=True))
            ddiff = d_d * dm
            dgc = dgc + jnp.sum(ddiff, axis=1, keepdims=True) - jnp.sum(ddiff.T, axis=1, keepdims=True)
            dgc = dgc + d_e * ec[:, 0:1]
            dgc_l.append(dgc)
            dgl_l.append(jnp.broadcast_to(dgl, (CH, 1)))
            dk_ref[r, :] = dk
            dv_ref[r, :] = dvb * bt

        dgc_all = jnp.broadcast_to(jnp.concatenate(dgc_l, axis=0), (GRP, 128))
        dg = _seg_cumsum(dgc_all, reverse=True)[:, 0:1] + jnp.concatenate(dgl_l, axis=0)
        dbeta = jnp.concatenate(dbeta_l, axis=0)
        da = dg * (-ea) * _sigmoid(xa)
        db = dbeta * beta * (1.0 - beta)
        lane = lax.broadcasted_iota(jnp.int32, (GRP, 2 * G_HEADS), 1)
        dba = jnp.where(lane == h, db, 0.0) + jnp.where(lane == G_HEADS + h, da, 0.0)
        lane8 = lax.broadcasted_iota(jnp.int32, (1, G_HEADS), 1)
        dal = jnp.where(lane8 == h, jnp.sum(dg * g, axis=0, keepdims=True), 0.0)
        ddt = jnp.where(lane8 == h, jnp.sum(da, axis=0, keepdims=True), 0.0)

        @pl.when(h == 0)
        def _():
            dba_ref[...] = dba

        @pl.when(h > 0)
        def _():
            dba_ref[...] += dba

        @pl.when((h == 0) & (gi == 0))
        def _():
            dal_ref[...] = dal
            ddt_ref[...] = ddt

        @pl.when((h > 0) | (gi > 0))
        def _():
            dal_ref[...] += dal
            ddt_ref[...] += ddt

    hd = lambda off: pl.BlockSpec((GRP, 128), lambda i, h: (i, off + h))
    vec = pl.BlockSpec((1, G_HEADS), lambda i, h: (0, 0))
    sq = pl.BlockSpec((None, GRP, CH), lambda i, h: (h, i, 0))
    gates = pl.BlockSpec((GRP, 2 * G_HEADS), lambda i, h: (i, 0))
    return pl.pallas_call(
        body,
        grid=(N_GRP, G_HEADS),
        in_specs=[hd(0), hd(G_HEADS), hd(2 * G_HEADS), gates, vec, vec, sq, hd(0), hd(0), hd(0), hd(0), sq,
                  pl.BlockSpec((None, CPG, 128), lambda i, h: (h, i, 0))],
        out_specs=[hd(0), hd(0), hd(0), gates, vec, vec],
        out_shape=[
            jax.ShapeDtypeStruct((T, D), F32),
            jax.ShapeDtypeStruct((T, D), F32),
            jax.ShapeDtypeStruct((T, D), F32),
            jax.ShapeDtypeStruct((T, 2 * G_HEADS), F32),
            jax.ShapeDtypeStruct((1, G_HEADS), F32),
            jax.ShapeDtypeStruct((1, G_HEADS), F32),
        ],
        scratch_shapes=[pltpu.VMEM((GRP, 128), F32)],
        compiler_params=_cp("arbitrary", "arbitrary"),
        name=name,
    )(qkv, qkv, qkv, ba, alog, dtb, tinv, du, dw, dqd, dkd, dat, dl)


def _gated_norm(o, z, g):
    rstd = lax.rsqrt(jnp.mean(o * o, axis=-1, keepdims=True) + EPS)
    oh = o * rstd
    sig = _sigmoid(z)
    return oh, rstd, sig


def gdn_out_fwd(h, o, z, norm_g, w_out, name):
    tt = 512

    def body(h_ref, o_ref, z_ref, g_ref, w_ref, hnew_ref, cat_ref):
        g = g_ref[...]
        for hh in range(G_HEADS):
            ln = slice(hh * 128, (hh + 1) * 128)
            zz = z_ref[:, ln]
            oh, _, sig = _gated_norm(o_ref[:, ln], zz, g)
            cat_ref[:, ln] = (oh * g * (zz * sig)).astype(BF16)
        hnew_ref[...] = h_ref[...] + _dot(cat_ref[...], w_ref[...])

    full = pl.BlockSpec((tt, D), lambda i: (i, 0))
    return pl.pallas_call(
        body,
        grid=(T // tt,),
        in_specs=[full, full, full, pl.BlockSpec((1, 128), lambda i: (0, 0)), pl.BlockSpec((D, D), lambda i: (0, 0))],
        out_specs=[full, full],
        out_shape=[jax.ShapeDtypeStruct((T, D), F32), jax.ShapeDtypeStruct((T, D), BF16)],
        compiler_params=_cp("parallel"),
        name=name,
    )(h, o, z, norm_g, w_out)


def gdn_out_bwd(dres, o, z, norm_g, w_out, name):
    tt = 512

    def body(d_ref, o_ref, z_ref, g_ref, w_ref, do_ref, dz_ref, dg_ref, dcat):
        i = pl.program_id(0)
        g = g_ref[...]
        dcat[...] = _dot_nt(d_ref[...].astype(BF16), w_ref[...])
        dg = jnp.zeros((1, 128), F32)
        for hh in range(G_HEADS):
            ln = slice(hh * 128, (hh + 1) * 128)
            zz = z_ref[:, ln]
            oh, rstd, sig = _gated_norm(o_ref[:, ln], zz, g)
            dout = dcat[:, ln]
            dy = dout * (zz * sig)
            dz_ref[:, ln] = dout * (oh * g) * _dsilu(zz, sig)
            dg = dg + jnp.sum(dy * oh, axis=0, keepdims=True)
            doh = dy * g
            do_ref[:, ln] = rstd * (doh - oh * jnp.mean(doh * oh, axis=-1, keepdims=True))

        @pl.when(i == 0)
        def _():
            dg_ref[...] = dg

        @pl.when(i > 0)
        def _():
            dg_ref[...] += dg

    full = pl.BlockSpec((tt, D), lambda i: (i, 0))
    vec = pl.BlockSpec((1, 128), lambda i: (0, 0))
    return pl.pallas_call(
        body,
        grid=(T // tt,),
        in_specs=[full, full, full, vec, pl.BlockSpec((D, D), lambda i: (0, 0))],
        out_specs=[full, full, vec],
        out_shape=[jax.ShapeDtypeStruct((T, D), F32), jax.ShapeDtypeStruct((T, D), F32), jax.ShapeDtypeStruct((1, 128), F32)],
        scratch_shapes=[pltpu.VMEM((tt, D), F32)],
        compiler_params=_cp("arbitrary"),
        name=name,
    )(dres, o, z, norm_g, w_out)


GDN_SPLITS = [(0, 1024), (1024, 1024), (2048, 1024), (3072, 1024), (4096, 2 * G_HEADS)]


def gdn_fwd(h, g_row, w_in, conv_w, alog, dtb, norm_g, w_out, tag):
    hn, qkv_pre, z, ba = proj_fwd(h, g_row, w_in, [(0, G_QKV), (G_QKV, 1024), (4096, 2 * G_HEADS)], f"gdn_proj_{tag}")
    qkv = gdn_prep_fwd(qkv_pre, conv_w, f"gdn_prep_{tag}")
    u, w, qd, kd, at, el, tinv = gdn_local_fwd(qkv, ba, alog, dtb, f"gdn_local_{tag}")
    o, vn, st = gdn_rec_fwd(u, w, qd, kd, at, el, f"gdn_rec_{tag}")
    hnew, cat = gdn_out_fwd(h, o, z, norm_g, w_out, f"gdn_out_{tag}")
    return hnew, (h, hn, qkv_pre, z, ba, qkv, w, qd, kd, at, el, tinv, o, vn, st, cat)


def gdn_bwd(dres, saved, g_row, w_in, conv_w, alog, dtb, norm_g, w_out, tag):
    h, hn, qkv_pre, z, ba, qkv, w, qd, kd, at, el, tinv, o, vn, st, cat = saved
    d_o, d_z, d_ng = gdn_out_bwd(dres, o, z, norm_g, w_out, f"gdn_out_bwd_{tag}")
    d_wout = mm_tn(cat, dres, f"gdn_wout_grad_{tag}")
    du, dw, dqd, dkd, dat, dl = gdn_rec_bwd(d_o, w, qd, kd, at, el, vn, st, f"gdn_rec_bwd_{tag}")
    dq, dk, dv, dba, dal, ddt = gdn_local_bwd(qkv, ba, alog, dtb, tinv, du, dw, dqd, dkd, dat, dl, f"gdn_local_bwd_{tag}")
    dpre, dcw = [], []
    for part, d in enumerate((dq, dk, dv)):
        dx, dwc = gdn_prep_bwd(d, qkv_pre, conv_w, part, part < 2, f"gdn_prep_bwd_{tag}_{part}")
        dpre.append(dx)
        dcw.append(dwc)
    parts = dpre + [d_z, dba]
    dres_new, d_norm = proj_bwd_data(h, g_row, w_in, parts, GDN_SPLITS, dres, f"gdn_proj_bwd_{tag}")
    d_win = jnp.concatenate([mm_tn(hn, p, f"gdn_win_grad_{tag}_{k}") for k, p in enumerate(parts)], axis=1)
    return dres_new, dict(norm=d_norm, w_in=d_win, conv_w=jnp.concatenate(dcw, axis=1), A_log=dal, dt_bias=ddt,
                          norm_g=d_ng, w_out=d_wout)


MESH = pl.DeviceIdType.MESH
ANY = pl.BlockSpec(memory_space=pl.ANY)


def _coords():
    return lax.axis_index("x"), lax.axis_index("y"), lax.axis_index("c")


def _slot(p):
    return 4 * p[0] + 2 * p[1] + p[2]


def all_gather(shards, name):
    k_n = len(shards)

    def body(*refs):
        srcs, dsts = refs[:k_n], refs[k_n:2 * k_n]
        send_sems, recv_sems, local_sems = refs[2 * k_n:]
        x, y, c = _coords()
        me, sibling = (x, y, c), (x, y, 1 - c)
        chips = [(1 - x, y), (x, 1 - y), (1 - x, 1 - y)]

        def copy(k, s, block, to, from_src=False):
            rows = dsts[k].at[_slot(block)]
            return pltpu.make_async_remote_copy(
                src_ref=srcs[k] if from_src else rows, dst_ref=rows,
                send_sem=send_sems.at[k, s], recv_sem=recv_sems.at[k, s], device_id=to, device_id_type=MESH)

        local = [pltpu.make_async_copy(srcs[k], dsts[k].at[_slot(me)], local_sems.at[k]) for k in range(k_n)]
        for cp in local:
            cp.start()
        first = []
        for k in range(k_n):
            first.append(copy(k, 0, me, sibling, True))
            first += [copy(k, 1 + j, me, (*chip, c), True) for j, chip in enumerate(chips)]
        for cp in first:
            cp.start()
        passed = []
        for j, chip in enumerate(chips):
            for k in range(k_n):
                copy(k, 1 + j, (*chip, c), me).wait_recv()
                fw = copy(k, 4 + j, (*chip, c), sibling)
                fw.start()
                passed.append(fw)
        for k in range(k_n):
            copy(k, 0, sibling, me).wait_recv()
            for j, chip in enumerate(chips):
                copy(k, 4 + j, (*chip, 1 - c), me).wait_recv()
        for cp in first + passed:
            cp.wait_send()
        for cp in local:
            cp.wait()

    return pl.pallas_call(
        body,
        in_specs=[ANY] * k_n,
        out_specs=[ANY] * k_n,
        out_shape=[jax.ShapeDtypeStruct((N_DEV,) + s.shape, s.dtype) for s in shards],
        scratch_shapes=[pltpu.SemaphoreType.DMA((k_n, 7)), pltpu.SemaphoreType.DMA((k_n, 7)),
                        pltpu.SemaphoreType.DMA((k_n,))],
        name=name,
    )(*shards)


HBM = pl.BlockSpec(memory_space=pltpu.HBM)
SEM = pl.BlockSpec(memory_space=pltpu.SEMAPHORE)
EFFECT = pltpu.SideEffectType.DATAFLOW_SIDE_EFFECTING


def _hbm(a):
    return pltpu.with_memory_space_constraint(a, pltpu.HBM)


def _peer_list(x, y, c):
    peers = []
    for j in range(1, N_DEV):
        jx, jy, jc = (j >> 2) & 1, (j >> 1) & 1, j & 1
        peers.append((x if jx == 0 else 1 - x, y if jy == 0 else 1 - y, c if jc == 0 else 1 - c))
    return peers


def _push_views(kind, layer, src_ref, land_ref, me, peer_slot):
    if kind == "gather":
        return src_ref, land_ref.at[me], land_ref.at[peer_slot]
    if layer is None:
        return src_ref.at[peer_slot], land_ref.at[me], land_ref.at[peer_slot]
    return src_ref.at[peer_slot], land_ref.at[me, layer], land_ref.at[peer_slot, layer]


def _push_copies(groups, srcs, lands, sems):
    x, y, c = _coords()
    me = _slot((x, y, c))
    peers = _peer_list(x, y, c)
    t = 0
    for gi, group in enumerate(groups):
        for ti, (kind, layer, _, li) in enumerate(group):
            for j, peer in enumerate(peers):
                out, there, here = _push_views(kind, layer, srcs[t], lands[li], me, _slot(peer))
                k = ti * (N_DEV - 1) + j
                yield out, there, here, sems[2 * gi].at[k], sems[2 * gi + 1].at[k], peer
            t += 1


def push_start(groups, lands, name, carry=()):
    flat = [it for g in groups for it in g]
    n, n_l, n_g, n_c = len(flat), len(lands), len(groups), len(carry)
    n_in = n + n_l + n_c

    def body(*refs):
        srcs, land_refs, sems = refs[:n], refs[n:n + n_l], refs[n_in:n_in + 2 * n_g]
        for out, there, _, s_sem, r_sem, peer in _push_copies(groups, srcs, land_refs, sems):
            pltpu.make_async_remote_copy(src_ref=out, dst_ref=there, send_sem=s_sem, recv_sem=r_sem,
                                         device_id=peer, device_id_type=MESH).start()

    arrays = [it[2] for it in flat] + list(lands) + list(carry)
    sem_shapes = []
    for g in groups:
        sem_shapes += [pltpu.SemaphoreType.DMA((len(g) * (N_DEV - 1),))] * 2
    outs = pl.pallas_call(
        body,
        name=name,
        in_specs=[HBM] * n_in,
        out_specs=[SEM] * (2 * n_g) + [HBM] * n_in,
        out_shape=sem_shapes + [pltpu.HBM(a.shape, a.dtype) for a in arrays],
        input_output_aliases={i: 2 * n_g + i for i in range(n_in)},
        compiler_params=pltpu.CompilerParams(has_side_effects=EFFECT),
    )(*[_hbm(a) for a in arrays])
    sems, thru = list(outs[:2 * n_g]), list(outs[2 * n_g:])
    return sems, thru[:n], thru[n:n + n_l], thru[n + n_l:]


def push_wait(groups, lands, sems, after, name):
    flat = [it for g in groups for it in g]
    n, n_l, n_g = len(flat), len(lands), len(groups)

    def body(*refs):
        srcs, land_refs, sem_refs = refs[:n], refs[n:n + n_l], refs[n + n_l:n + n_l + 2 * n_g]
        for out, _, here, s_sem, r_sem, peer in _push_copies(groups, srcs, land_refs, sem_refs):
            cp = pltpu.make_async_remote_copy(src_ref=out, dst_ref=here, send_sem=s_sem, recv_sem=r_sem,
                                              device_id=peer, device_id_type=MESH)
            cp.wait_send()
            cp.wait_recv()

    arrays = [it[2] for it in flat] + list(lands)
    outs = pl.pallas_call(
        body,
        name=name,
        in_specs=[HBM] * (n + n_l) + [SEM] * (2 * n_g) + [ANY],
        out_specs=[HBM] * (n + n_l),
        out_shape=[pltpu.HBM(a.shape, a.dtype) for a in arrays],
        input_output_aliases={i: i for i in range(n + n_l)},
        compiler_params=pltpu.CompilerParams(has_side_effects=EFFECT),
    )(*arrays, *sems, after)
    return list(outs[:n]), list(outs[n:])


def sum_slabs(parts, name):
    n, rows, cols = parts.shape

    def body(p_ref, o_ref):
        g = p_ref[0]
        for s in range(1, n):
            g = g + p_ref[s]
        o_ref[...] = g

    return pl.pallas_call(body, out_shape=jax.ShapeDtypeStruct((rows, cols), F32), name=name)(parts)


def _row_tile(rows, cols):
    if rows * cols * 4 <= (1 << 20) or rows % 8:
        return rows
    tr = rows
    while tr % 2 == 0 and (tr // 2) % 8 == 0 and tr * cols * 4 > (1 << 20):
        tr //= 2
    return tr


def adamw(parts, w, m, v, name):
    p_n = parts.shape[0]
    rows, cols = w.shape
    tr = _row_tile(rows, cols)

    def body(p_ref, w_ref, m_ref, v_ref, g_ref, d_ref, nm_ref, nv_ref):
        g = p_ref[0].astype(F32)
        for s in range(1, p_n):
            g = g + p_ref[s].astype(F32)
        m_new = ADAM_B1 * m_ref[...] + (1.0 - ADAM_B1) * g
        v_new = ADAM_B2 * v_ref[...] + (1.0 - ADAM_B2) * (g * g)
        m_hat = m_new / (1.0 - ADAM_B1 ** ADAM_STEP)
        v_hat = v_new / (1.0 - ADAM_B2 ** ADAM_STEP)
        g_ref[...] = g
        d_ref[...] = -ADAM_LR * (m_hat / (jnp.sqrt(v_hat) + ADAM_EPS) + ADAM_WD * w_ref[...])
        nm_ref[...] = m_new
        nv_ref[...] = v_new

    blk = pl.BlockSpec((tr, cols), lambda i: (i, 0))
    return pl.pallas_call(
        body,
        grid=(rows // tr,),
        in_specs=[pl.BlockSpec((p_n, tr, cols), lambda i: (0, i, 0)), blk, blk, blk],
        out_specs=[blk] * 4,
        out_shape=[jax.ShapeDtypeStruct((rows, cols), F32)] * 4,
        compiler_params=_cp("parallel"),
        name=name,
    )(parts, w, m, v)


def _adamw_nd(parts, w, m, v, name):
    shp = w.shape
    cols = shp[-1]
    rows = math.prod(shp[:-1])
    outs = adamw(parts.reshape(parts.shape[0], rows, cols), w.reshape(rows, cols), m.reshape(rows, cols),
                 v.reshape(rows, cols), name)
    return [o.reshape(shp) for o in outs]


REPL = ["ffn1_norm", "mix_norm", "ffn2_norm", "hyb_dw_b", "hyb_ln_g", "hyb_ln_b", "gdn_A_log", "gdn_dt_bias",
        "gdn_norm_g", "final_norm"]
WEIGHTS = ["ffn1_norm", "ffn1_w_in", "ffn1_w_out", "mix_norm", "ffn2_norm", "ffn2_w_in", "ffn2_w_out", "hyb_w_in",
           "hyb_dw_w", "hyb_dw_b", "hyb_ln_g", "hyb_ln_b", "hyb_w_out", "gdn_w_in", "gdn_conv_w", "gdn_A_log",
           "gdn_dt_bias", "gdn_norm_g", "gdn_w_out", "final_norm"]


def _pack(arrs, rows):
    flat = jnp.concatenate([a.reshape(-1) for a in arrs])
    return jnp.pad(flat, (0, rows * 128 - flat.shape[0])).reshape(rows, 128)


def _cols_to_slabs(a):
    d, n = a.shape
    return a.reshape(d, N_DEV, n // N_DEV).transpose(1, 0, 2)


def _slabs_to_cols(a):
    return jnp.moveaxis(a, 0, -2).reshape(a.shape[1:-1] + (N_DEV * a.shape[-1],))


def kernel(x, positions, ffn1_norm, ffn1_w_in, ffn1_w_out, mix_norm, ffn2_norm, ffn2_w_in, ffn2_w_out, hyb_w_in, hyb_dw_w, hyb_dw_b, hyb_ln_g, hyb_ln_b, hyb_w_out, gdn_w_in, gdn_conv_w, gdn_A_log, gdn_dt_bias, gdn_norm_g, gdn_w_out, final_norm, loss_target, m_ffn1_norm, m_ffn1_w_in, m_ffn1_w_out, m_mix_norm, m_ffn2_norm, m_ffn2_w_in, m_ffn2_w_out, m_hyb_w_in, m_hyb_dw_w, m_hyb_dw_b, m_hyb_ln_g, m_hyb_ln_b, m_hyb_w_out, m_gdn_w_in, m_gdn_conv_w, m_gdn_A_log, m_gdn_dt_bias, m_gdn_norm_g, m_gdn_w_out, m_final_norm, v_ffn1_norm, v_ffn1_w_in, v_ffn1_w_out, v_mix_norm, v_ffn2_norm, v_ffn2_w_in, v_ffn2_w_out, v_hyb_w_in, v_hyb_dw_w, v_hyb_dw_b, v_hyb_ln_g, v_hyb_ln_b, v_hyb_w_out, v_gdn_w_in, v_gdn_conv_w, v_gdn_A_log, v_gdn_dt_bias, v_gdn_norm_g, v_gdn_w_out, v_final_norm):
    w = dict(ffn1_norm=ffn1_norm, ffn1_w_in=ffn1_w_in, ffn1_w_out=ffn1_w_out, mix_norm=mix_norm, ffn2_norm=ffn2_norm,
             ffn2_w_in=ffn2_w_in, ffn2_w_out=ffn2_w_out, hyb_w_in=hyb_w_in, hyb_dw_w=hyb_dw_w, hyb_dw_b=hyb_dw_b,
             hyb_ln_g=hyb_ln_g, hyb_ln_b=hyb_ln_b, hyb_w_out=hyb_w_out, gdn_w_in=gdn_w_in, gdn_conv_w=gdn_conv_w,
             gdn_A_log=gdn_A_log, gdn_dt_bias=gdn_dt_bias, gdn_norm_g=gdn_norm_g, gdn_w_out=gdn_w_out,
             final_norm=final_norm)
    mom = dict(ffn1_norm=m_ffn1_norm, ffn1_w_in=m_ffn1_w_in, ffn1_w_out=m_ffn1_w_out, mix_norm=m_mix_norm,
               ffn2_norm=m_ffn2_norm, ffn2_w_in=m_ffn2_w_in, ffn2_w_out=m_ffn2_w_out, hyb_w_in=m_hyb_w_in,
               hyb_dw_w=m_hyb_dw_w, hyb_dw_b=m_hyb_dw_b, hyb_ln_g=m_hyb_ln_g, hyb_ln_b=m_hyb_ln_b,
               hyb_w_out=m_hyb_w_out, gdn_w_in=m_gdn_w_in, gdn_conv_w=m_gdn_conv_w, gdn_A_log=m_gdn_A_log,
               gdn_dt_bias=m_gdn_dt_bias, gdn_norm_g=m_gdn_norm_g, gdn_w_out=m_gdn_w_out, final_norm=m_final_norm)
    var = dict(ffn1_norm=v_ffn1_norm, ffn1_w_in=v_ffn1_w_in, ffn1_w_out=v_ffn1_w_out, mix_norm=v_mix_norm,
               ffn2_norm=v_ffn2_norm, ffn2_w_in=v_ffn2_w_in, ffn2_w_out=v_ffn2_w_out, hyb_w_in=v_hyb_w_in,
               hyb_dw_w=v_hyb_dw_w, hyb_dw_b=v_hyb_dw_b, hyb_ln_g=v_hyb_ln_g, hyb_ln_b=v_hyb_ln_b,
               hyb_w_out=v_hyb_w_out, gdn_w_in=v_gdn_w_in, gdn_conv_w=v_gdn_conv_w, gdn_A_log=v_gdn_A_log,
               gdn_dt_bias=v_gdn_dt_bias, gdn_norm_g=v_gdn_norm_g, gdn_w_out=v_gdn_w_out, final_norm=v_final_norm)
    xi, yi, ci = _coords()
    me = 4 * xi + 2 * yi + ci

    big = ["ffn1_w_in", "ffn1_w_out", "ffn2_w_in", "ffn2_w_out", "hyb_w_in", "hyb_w_out", "gdn_w_in", "gdn_w_out"]
    ag_groups, ag_lands = [], []

    def add_group(shards):
        group = []
        for s in shards:
            land = lax.dynamic_update_slice(lax.empty((N_DEV,) + s.shape, s.dtype), s[None], (me,) + (0,) * s.ndim)
            group.append(("gather", None, s, len(ag_lands)))
            ag_lands.append(land)
        ag_groups.append(group)

    for l in range(DEPTH):
        i = l // 2
        add_group([ffn1_w_in[l].astype(BF16), ffn1_w_out[l].astype(BF16)])
        if l % 2 == 0:
            add_group([hyb_w_in[i].astype(BF16), hyb_w_out[i].astype(BF16), hyb_dw_w[i]])
        else:
            add_group([gdn_w_in[i].astype(BF16), gdn_w_out[i].astype(BF16), gdn_conv_w[i]])
        add_group([ffn2_w_in[l].astype(BF16), ffn2_w_out[l].astype(BF16)])
    ag_sems, ag_srcs, ag_lands, _ = push_start(ag_groups, ag_lands, "weights_gather_start")

    def fetch(gi, after):
        group = ag_groups[gi]
        base = sum(len(g) for g in ag_groups[:gi])
        items = [(kind, layer, ag_srcs[base + t], t) for t, (kind, layer, _, _) in enumerate(group)]
        lands = [ag_lands[li] for _, _, _, li in group]
        return push_wait([items], lands, ag_sems[2 * gi:2 * gi + 2], after, f"weights_gather_wait_{gi}")[1]

    row = lambda a: a.reshape(1, -1)

    rope = make_rope(positions)
    h = x[0]
    saved = []
    for l in range(DEPTH):
        i = l // 2
        rec = {"h1": h}
        wi, wo = fetch(3 * l, h)
        rec["w1"] = (wi.reshape(2, FFN_TILES, D, FFN_SHARD), wo)
        h, rec["hn1"], rec["a1"], rec["b1"] = ffn_fwd(h, row(ffn1_norm[l]), *rec["w1"], l, "1")
        mi, mo, mc = fetch(3 * l + 1, h)
        if l % 2 == 0:
            rec["wm"] = (_slabs_to_cols(mi), jnp.pad(_slabs_to_cols(mc), ((0, 1), (0, 0))), mo.reshape(D, D))
            w_in_f, dw_f, w_out_f = rec["wm"]
            h, rec["mix"] = hybrid_fwd(h, row(mix_norm[l]), w_in_f, dw_f, row(hyb_dw_b[i]), row(hyb_ln_g[i]),
                                       row(hyb_ln_b[i]), w_out_f, rope, str(i))
        else:
            rec["wm"] = (_slabs_to_cols(mi), _slabs_to_cols(mc), mo.reshape(D, D))
            w_in_f, cw_f, w_out_f = rec["wm"]
            h, rec["mix"] = gdn_fwd(h, row(mix_norm[l]), w_in_f, cw_f, row(gdn_A_log[i]), row(gdn_dt_bias[i]),
                                    row(gdn_norm_g[i]), w_out_f, str(i))
        rec["h2"] = h
        wi, wo = fetch(3 * l + 2, h)
        rec["w2"] = (wi.reshape(2, FFN_TILES, D, FFN_SHARD), wo)
        h, rec["hn2"], rec["a2"], rec["b2"] = ffn_fwd(h, row(ffn2_norm[l]), *rec["w2"], l, "2")
        saved.append(rec)
    dres, d_final, loss_acc = final_loss(h, row(final_norm), loss_target[0])
    loss = lax.psum(loss_acc[0, 0], ("x", "y", "c"))

    ge_land = {n: lax.empty((N_DEV,) + w[n].shape, BF16) for n in big}
    ge_pending = []

    def send(named, layer, tag, carry):
        lands = [ge_land[n] for n, _ in named]
        group = [("scatter", layer, s, t) for t, (_, s) in enumerate(named)]
        sems, srcs, lands_out, carried = push_start([group], lands, f"grad_send_{tag}", carry=[carry])
        for (n, _), land in zip(named, lands_out):
            ge_land[n] = land
        ge_pending.append(([(n, layer, s) for (n, _), s in zip(named, srcs)], sems))
        return carried[0]

    gsmall = {n: [None] * (DEPTH if n in ("ffn1_norm", "mix_norm", "ffn2_norm") else 2) for n in REPL[:-1]}
    gsmall["hyb_dw_w"] = [None, None]
    gsmall["gdn_conv_w"] = [None, None]
    for l in reversed(range(DEPTH)):
        i = l // 2
        rec = saved[l]
        dhn, dwin, dwout = ffn_bwd(rec["hn2"], rec["a2"], rec["b2"], dres, *rec["w2"], l, "2")
        dhn = send([("ffn2_w_in", dwin.reshape(N_DEV, D, FFN_SHARD)),
                    ("ffn2_w_out", dwout.reshape(N_DEV, FFN_SHARD // 2, D))], l, f"ffn2_{l}", dhn)
        dres, dg = norm_bwd(rec["h2"], row(ffn2_norm[l]), dhn, dres, f"ffn2_norm_bwd_{l}")
        gsmall["ffn2_norm"][l] = dg
        if l % 2 == 0:
            w_in_f, dw_f, w_out_f = rec["wm"]
            dres, gr = hybrid_bwd(dres, rec["mix"], row(mix_norm[l]), w_in_f, dw_f, row(hyb_ln_g[i]),
                                  row(hyb_ln_b[i]), w_out_f, rope, str(i))
            dres = send([("hyb_w_in", _cols_to_slabs(gr["w_in"]).astype(BF16)),
                         ("hyb_w_out", gr["w_out"].reshape(N_DEV, D // N_DEV, D).astype(BF16))], i, f"hyb_{i}", dres)
            for n in ("dw_w", "dw_b", "ln_g", "ln_b"):
                gsmall["hyb_" + n][i] = gr[n]
        else:
            w_in_f, cw_f, w_out_f = rec["wm"]
            dres, gr = gdn_bwd(dres, rec["mix"], row(mix_norm[l]), w_in_f, cw_f, row(gdn_A_log[i]),
                               row(gdn_dt_bias[i]), row(gdn_norm_g[i]), w_out_f, str(i))
            dres = send([("gdn_w_in", _cols_to_slabs(gr["w_in"]).astype(BF16)),
                         ("gdn_w_out", gr["w_out"].reshape(N_DEV, D // N_DEV, D).astype(BF16))], i, f"gdn_{i}", dres)
            for n in ("conv_w", "A_log", "dt_bias", "norm_g"):
                gsmall["gdn_" + n][i] = gr[n]
        gsmall["mix_norm"][l] = gr["norm"]
        dhn, dwin, dwout = ffn_bwd(rec["hn1"], rec["a1"], rec["b1"], dres, *rec["w1"], l, "1")
        dhn = send([("ffn1_w_in", dwin.reshape(N_DEV, D, FFN_SHARD)),
                    ("ffn1_w_out", dwout.reshape(N_DEV, FFN_SHARD // 2, D))], l, f"ffn1_{l}", dhn)
        dres, dg = norm_bwd(rec["h1"], row(ffn1_norm[l]), dhn, dres, f"ffn1_norm_bwd_{l}")
        gsmall["ffn1_norm"][l] = dg
    grad_x = dres[None]

    n_repl_rows = 136
    small_rows = 576
    repl_flat = jnp.concatenate([jnp.concatenate([a.reshape(-1) for a in gsmall[n]]) for n in REPL[:-1]]
                                + [d_final.reshape(-1)])
    repl_pack = jnp.pad(repl_flat, (0, n_repl_rows * 128 - repl_flat.shape[0]))
    small_pack = jnp.concatenate([repl_pack] + [a.reshape(-1) for a in gsmall["hyb_dw_w"]]
                                 + [a.reshape(-1) for a in gsmall["gdn_conv_w"]]).reshape(small_rows, 128)
    small_all, = all_gather([small_pack], "small_grads_all_gather")
    g_small = sum_slabs(small_all, "small_grads_sum")

    groups = [[("scatter", layer, s, big.index(n)) for n, layer, s in named] for named, _ in ge_pending]
    sems = [s for _, pair in ge_pending for s in pair]
    srcs_out, lands_out = push_wait(groups, [ge_land[n] for n in big], sems, dres, "grad_wait")
    own = {n: {} for n in big}
    flat_named = [it for named, _ in ge_pending for it in named]
    for (n, layer, _), s in zip(flat_named, srcs_out):
        own[n][layer] = lax.dynamic_index_in_dim(s, me, 0, keepdims=False)
    recv = {}
    for n, land in zip(big, lands_out):
        mine = jnp.stack([own[n][k] for k in range(len(own[n]))])
        recv[n] = lax.dynamic_update_slice(land, mine[None], (me,) + (0,) * mine.ndim)

    out = {}
    for n in big:
        out[n] = _adamw_nd(recv[n], w[n], mom[n], var[n], f"adamw_{n}")
    pk = lambda d: _pack([d[n] for n in REPL], n_repl_rows)
    res = adamw(g_small[:n_repl_rows][None], pk(w), pk(mom), pk(var), "adamw_replicated")
    off = 0
    for n in REPL:
        sz = w[n].size
        out[n] = [r.reshape(-1)[off:off + sz].reshape(w[n].shape) for r in res]
        off += sz
    g_dw = g_small[n_repl_rows:n_repl_rows + 248].reshape(2, CONV_K, CONV_C)
    g_dw = lax.dynamic_slice_in_dim(g_dw, me * (CONV_C // N_DEV), CONV_C // N_DEV, axis=2)
    out["hyb_dw_w"] = _adamw_nd(g_dw[None], w["hyb_dw_w"], mom["hyb_dw_w"], var["hyb_dw_w"], "adamw_hyb_dw_w")
    g_cw = g_small[n_repl_rows + 248:].reshape(2, G_CONV, G_QKV)
    g_cw = lax.dynamic_slice_in_dim(g_cw, me * (G_QKV // N_DEV), G_QKV // N_DEV, axis=2)
    out["gdn_conv_w"] = _adamw_nd(g_cw[None], w["gdn_conv_w"], mom["gdn_conv_w"], var["gdn_conv_w"], "adamw_gdn_conv_w")

    return (loss, grad_x, *[out[n][0] for n in WEIGHTS], *[out[n][1] for n in WEIGHTS],
            *[out[n][2] for n in WEIGHTS], *[out[n][3] for n in WEIGHTS])
```

```python
import functools
import math

import jax
import jax.numpy as jnp
import numpy as np
from jax import lax
from jax.experimental import pallas as pl
from jax.experimental.pallas import tpu as pltpu

F32 = jnp.float32
BF16 = jnp.bfloat16

N_DEV = 8
T = 4096
D = 1024
DEPTH = 4
FFN = 2816
FFN_SHARD = 2 * FFN // N_DEV
FFN_TILES = FFN // FFN_SHARD
EPS = 1e-6

A_HEADS = 8
A_HD = 64
A_W = 512
CONV_C = 512
CONV_K = 31
HYB_IN = 2560
ROPE_THETA = 500000.0
ROT = 16
DILATIONS = (1, 4, 16)
BLK = 128
KPAD = 2048

G_HEADS = 8
G_DK = 128
G_QKV = 3072
G_IN = 4112
G_CONV = 4
CH = 64
GRP = 512
CPG = GRP // CH
N_GRP = T // GRP

ADAM_LR = 0.001
ADAM_B1 = 0.9
ADAM_B2 = 0.999
ADAM_EPS = 1e-08
ADAM_WD = 0.01
ADAM_STEP = 10

VMEM_LIMIT = 56 * 1024 * 1024

HI = lax.Precision.HIGHEST


def _cp(*sem):
    return pltpu.CompilerParams(dimension_semantics=sem, vmem_limit_bytes=VMEM_LIMIT)


def _dot(a, b):
    return jnp.dot(a, b, preferred_element_type=F32)


def _dot_nt(a, b):
    return lax.dot_general(a, b, (((1,), (1,)), ((), ())), preferred_element_type=F32)


def _dot_tn(a, b):
    return lax.dot_general(a, b, (((0,), (0,)), ((), ())), preferred_element_type=F32)


def _sigmoid(x):
    return 1.0 / (1.0 + jnp.exp(-x))


def _dsilu(x, sig):
    return sig * (1.0 + x * (1.0 - sig))


def _rms(x, g):
    rstd = lax.rsqrt(jnp.mean(x * x, axis=-1, keepdims=True) + EPS)
    return x * rstd * g


FFN_TT = 512


def ffn_fwd(h, g_row, w_in, w_out, layer, tag=""):
    def body(h_ref, g_ref, win_ref, wout_ref, hnew_ref, hn_ref, a_ref, b_ref):
        x = h_ref[...]
        hn = _rms(x, g_ref[...]).astype(BF16)
        hn_ref[...] = hn
        acc = None
        for j in range(FFN_TILES):
            a = _dot(hn, win_ref[0, j])
            b = _dot(hn, win_ref[1, j])
            act = a * _sigmoid(a) * b
            a_ref[j] = a.astype(BF16)
            b_ref[j] = b.astype(BF16)
            part = _dot(act.astype(BF16), wout_ref[2 * j:2 * j + 2].reshape(FFN_SHARD, D))
            acc = part if acc is None else acc + part
        hnew_ref[...] = x + 0.5 * acc

    tt = FFN_TT
    resident = pl.Buffered(1)
    return pl.pallas_call(
        body,
        grid=(T // tt,),
        in_specs=[
            pl.BlockSpec((tt, D), lambda i: (i, 0)),
            pl.BlockSpec((1, D), lambda i: (0, 0)),
            pl.BlockSpec((2, FFN_TILES, D, FFN_SHARD), lambda i: (0, 0, 0, 0), pipeline_mode=resident),
            pl.BlockSpec((N_DEV, FFN_SHARD // 2, D), lambda i: (0, 0, 0), pipeline_mode=resident),
        ],
        out_specs=[
            pl.BlockSpec((tt, D), lambda i: (i, 0)),
            pl.BlockSpec((tt, D), lambda i: (i, 0)),
            pl.BlockSpec((FFN_TILES, tt, FFN_SHARD), lambda i: (0, i, 0)),
            pl.BlockSpec((FFN_TILES, tt, FFN_SHARD), lambda i: (0, i, 0)),
        ],
        out_shape=[
            jax.ShapeDtypeStruct((T, D), F32),
            jax.ShapeDtypeStruct((T, D), BF16),
            jax.ShapeDtypeStruct((FFN_TILES, T, FFN_SHARD), BF16),
            jax.ShapeDtypeStruct((FFN_TILES, T, FFN_SHARD), BF16),
        ],
        compiler_params=_cp("parallel"),
        name=f"ffn{tag}_fwd_{layer}",
    )(h, g_row, w_in, w_out)


def ffn_bwd(hn, a, b, dres, w_in, w_out, layer, tag=""):
    tt = FFN_TT
    nt = T // tt

    def body(hn_ref, a_ref, b_ref, dres_ref, win_ref, wout_ref, dhn_ref, dwin_ref, dwout_ref, gin_ref, gout_ref,
             do_s, act_s, da_s, db_s):
        i = pl.program_id(1)
        wo = wout_ref[...].reshape(FFN_SHARD, D)
        half = tt // 2
        for r0 in (0, half):
            rows = slice(r0, r0 + half)
            do_h = (0.5 * dres_ref[rows, :]).astype(BF16)
            do_s[rows, :] = do_h
            dact = _dot_nt(do_h, wo)
            a = a_ref[rows, :].astype(F32)
            b = b_ref[rows, :].astype(F32)
            sig = _sigmoid(a)
            s = a * sig
            da_h = (dact * b * _dsilu(a, sig)).astype(BF16)
            db_h = (dact * s).astype(BF16)
            act_s[rows, :] = (s * b).astype(BF16)
            da_s[rows, :] = da_h
            db_s[rows, :] = db_h
            dhn_ref[rows, :] = (_dot_nt(da_h, win_ref[0]) + _dot_nt(db_h, win_ref[1])).astype(BF16)
        do, act, da, db = do_s[...], act_s[...], da_s[...], db_s[...]
        hn = hn_ref[...]
        gwo = _dot_tn(act, do)
        gwg = _dot_tn(hn, da)
        gwu = _dot_tn(hn, db)

        @pl.when(i == 0)
        def _():
            gout_ref[...] = gwo
            gin_ref[0] = gwg
            gin_ref[1] = gwu

        @pl.when(i > 0)
        def _():
            gout_ref[...] += gwo
            gin_ref[0] += gwg
            gin_ref[1] += gwu

        @pl.when(i == nt - 1)
        def _():
            dwin_ref[...] = gin_ref[...].astype(BF16)
            dwout_ref[...] = gout_ref[...].astype(BF16)

    return pl.pallas_call(
        body,
        grid=(FFN_TILES, nt),
        in_specs=[
            pl.BlockSpec((tt, D), lambda j, i: (i, 0)),
            pl.BlockSpec((None, tt, FFN_SHARD), lambda j, i: (j, i, 0)),
            pl.BlockSpec((None, tt, FFN_SHARD), lambda j, i: (j, i, 0)),
            pl.BlockSpec((tt, D), lambda j, i: (i, 0)),
            pl.BlockSpec((2, None, D, FFN_SHARD), lambda j, i: (0, j, 0, 0)),
            pl.BlockSpec((2, FFN_SHARD // 2, D), lambda j, i: (j, 0, 0)),
        ],
        out_specs=[
            pl.BlockSpec((None, tt, D), lambda j, i: (j, i, 0)),
            pl.BlockSpec((2, None, D, FFN_SHARD), lambda j, i: (0, j, 0, 0)),
            pl.BlockSpec((None, FFN_SHARD, D), lambda j, i: (j, 0, 0)),
        ],
        out_shape=[
            jax.ShapeDtypeStruct((FFN_TILES, T, D), BF16),
            jax.ShapeDtypeStruct((2, FFN_TILES, D, FFN_SHARD), BF16),
            jax.ShapeDtypeStruct((FFN_TILES, FFN_SHARD, D), BF16),
        ],
        scratch_shapes=[pltpu.VMEM((2, D, FFN_SHARD), F32), pltpu.VMEM((FFN_SHARD, D), F32),
                        pltpu.VMEM((tt, D), BF16),
                        pltpu.VMEM((tt, FFN_SHARD), BF16), pltpu.VMEM((tt, FFN_SHARD), BF16),
                        pltpu.VMEM((tt, FFN_SHARD), BF16)],
        compiler_params=_cp("parallel", "arbitrary"),
        name=f"ffn{tag}_bwd_{layer}",
    )(hn, a, b, dres, w_in, w_out)


def _rms_bwd(x, g, dy):
    rstd = lax.rsqrt(jnp.mean(x * x, axis=-1, keepdims=True) + EPS)
    xh = x * rstd
    u = dy * g
    dx = rstd * (u - xh * jnp.mean(u * xh, axis=-1, keepdims=True))
    return dx, jnp.sum(dy * xh, axis=0, keepdims=True)


def norm_bwd(x, g_row, dy_parts, dres, name):
    p = dy_parts.shape[0]
    tt = 512

    def body(x_ref, g_ref, dy_ref, dres_ref, out_ref, dg_ref):
        i = pl.program_id(0)
        dy = dy_ref[0].astype(F32)
        for q in range(1, p):
            dy = dy + dy_ref[q].astype(F32)
        dx, dg = _rms_bwd(x_ref[...], g_ref[...], dy)
        out_ref[...] = dres_ref[...] + dx

        @pl.when(i == 0)
        def _():
            dg_ref[...] = dg

        @pl.when(i > 0)
        def _():
            dg_ref[...] += dg

    return pl.pallas_call(
        body,
        grid=(T // tt,),
        in_specs=[
            pl.BlockSpec((tt, D), lambda i: (i, 0)),
            pl.BlockSpec((1, D), lambda i: (0, 0)),
            pl.BlockSpec((p, tt, D), lambda i: (0, i, 0)),
            pl.BlockSpec((tt, D), lambda i: (i, 0)),
        ],
        out_specs=[pl.BlockSpec((tt, D), lambda i: (i, 0)), pl.BlockSpec((1, D), lambda i: (0, 0))],
        out_shape=[jax.ShapeDtypeStruct((T, D), F32), jax.ShapeDtypeStruct((1, D), F32)],
        compiler_params=_cp("arbitrary"),
        name=name,
    )(x, g_row, dy_parts, dres)


def final_loss(h, g_row, target):
    tt = 512

    def body(h_ref, g_ref, t_ref, dres_ref, dg_ref, loss_ref):
        i = pl.program_id(0)
        x = h_ref[...]
        g = g_ref[...]
        err = _rms(x, g) - t_ref[...]
        part = 0.5 * jnp.sum(jnp.mean(err * err, axis=-1, keepdims=True), axis=0, keepdims=True)
        dx, dg = _rms_bwd(x, g, err * (1.0 / D))
        dres_ref[...] = dx
        part = jnp.broadcast_to(part, loss_ref.shape)

        @pl.when(i == 0)
        def _():
            dg_ref[...] = dg
            loss_ref[...] = part

        @pl.when(i > 0)
        def _():
            dg_ref[...] += dg
            loss_ref[...] += part

    return pl.pallas_call(
        body,
        grid=(T // tt,),
        in_specs=[
            pl.BlockSpec((tt, D), lambda i: (i, 0)),
            pl.BlockSpec((1, D), lambda i: (0, 0)),
            pl.BlockSpec((tt, D), lambda i: (i, 0)),
        ],
        out_specs=[
            pl.BlockSpec((tt, D), lambda i: (i, 0)),
            pl.BlockSpec((1, D), lambda i: (0, 0)),
            pl.BlockSpec((8, 128), lambda i: (0, 0)),
        ],
        out_shape=[
            jax.ShapeDtypeStruct((T, D), F32),
            jax.ShapeDtypeStruct((1, D), F32),
            jax.ShapeDtypeStruct((8, 128), F32),
        ],
        compiler_params=_cp("arbitrary"),
        name="final_loss",
    )(h, g_row, target)


PROJ_TT = 256


def rope_tables(pos_col, invf_row):
    tt = 512

    def body(p_ref, f_ref, c_ref, sm_ref, sp_ref):
        ang = p_ref[...].astype(F32) * f_ref[...]
        lane = lax.broadcasted_iota(jnp.int32, ang.shape, 1) % A_HD
        cs = jnp.cos(ang)
        sn = jnp.sin(ang)
        c_ref[...] = jnp.where(lane < ROT, cs, 1.0)
        sm_ref[...] = jnp.where(lane < ROT // 2, -sn, 0.0)
        sp_ref[...] = jnp.where((lane >= ROT // 2) & (lane < ROT), sn, 0.0)

    spec = pl.BlockSpec((tt, 128), lambda i: (i, 0))
    return pl.pallas_call(
        body,
        grid=(T // tt,),
        in_specs=[pl.BlockSpec((tt, 1), lambda i: (i, 0)), pl.BlockSpec((1, 128), lambda i: (0, 0))],
        out_specs=[spec, spec, spec],
        out_shape=[jax.ShapeDtypeStruct((T, 128), F32)] * 3,
        compiler_params=_cp("parallel"),
        name="rope_tables",
    )(pos_col, invf_row)


def make_rope(positions):
    inv_freq = jnp.power(jnp.float32(ROPE_THETA), -jnp.arange(0, ROT, 2, dtype=F32) / ROT)
    per_head = jnp.concatenate([inv_freq, inv_freq, jnp.zeros((A_HD - ROT,), F32)])
    invf_row = jnp.tile(per_head, 2)[None, :]
    return tuple(rope_tables(positions.reshape(T, 1), invf_row))


def _rope(x, c, sm, sp):
    return x * c + pltpu.roll(x, 128 - ROT // 2, 1) * sm + pltpu.roll(x, ROT // 2, 1) * sp


def _rope_t(dy, c, sm, sp):
    return dy * c + pltpu.roll(dy * sm, ROT // 2, 1) + pltpu.roll(dy * sp, 128 - ROT // 2, 1)


def proj_fwd(h, g_row, w, splits, name, rope=None):
    tt = PROJ_TT
    n = w.shape[1]
    n_rope = 0 if rope is None else 3

    def body(h_ref, g_ref, w_ref, *rest):
        tabs = rest[:n_rope]
        hn_ref = rest[n_rope]
        outs = rest[n_rope + 1:]
        hn = _rms(h_ref[...], g_ref[...]).astype(BF16)
        hn_ref[...] = hn
        for k, ((st, wd), o_ref) in enumerate(zip(splits, outs)):
            if rope is not None and k == 0:
                c, sm, sp = (t[...] for t in tabs)
                for gi in range(wd // 128):
                    r = _dot(hn, w_ref[:, st + 128 * gi:st + 128 * (gi + 1)])
                    if gi < 2 * A_W // 128:
                        r = _rope(r, c, sm, sp)
                    o_ref[:, 128 * gi:128 * (gi + 1)] = r
            else:
                o_ref[...] = _dot(hn, w_ref[:, st:st + wd])

    tab_specs = [pl.BlockSpec((tt, 128), lambda i: (i, 0))] * n_rope
    return pl.pallas_call(
        body,
        grid=(T // tt,),
        in_specs=[
            pl.BlockSpec((tt, D), lambda i: (i, 0)),
            pl.BlockSpec((1, D), lambda i: (0, 0)),
            pl.BlockSpec((D, n), lambda i: (0, 0)),
        ] + tab_specs,
        out_specs=[pl.BlockSpec((tt, D), lambda i: (i, 0))]
        + [pl.BlockSpec((tt, wd), lambda i: (i, 0)) for _, wd in splits],
        out_shape=[jax.ShapeDtypeStruct((T, D), BF16)]
        + [jax.ShapeDtypeStruct((T, wd), F32) for _, wd in splits],
        compiler_params=_cp("parallel"),
        name=name,
    )(h, g_row, w, *(rope or ()))


def proj_bwd_data(x, g_row, w, dparts, splits, dres, name, rope=None, n_rot=0):
    tt = PROJ_TT
    n = w.shape[1]
    n_rope = 0 if rope is None else 3
    k_parts = len(dparts)
    n_main = sum(wd for _, wd in splits if wd % 128 == 0)

    def body(x_ref, g_ref, w_ref, dres_ref, *rest):
        d_refs = rest[:k_parts]
        tabs = rest[k_parts:k_parts + n_rope]
        out_ref, dg_ref, dproj_ref = rest[k_parts + n_rope:k_parts + n_rope + 3]
        i = pl.program_id(0)
        dhn = jnp.zeros((tt, D), F32)
        for k, ((st, wd), d_ref) in enumerate(zip(splits, d_refs)):
            if k < n_rot:
                c, sm, sp = (t[...] for t in tabs)
                for gi in range(wd // 128):
                    cols = slice(st + 128 * gi, st + 128 * (gi + 1))
                    d = _rope_t(d_ref[:, 128 * gi:128 * (gi + 1)], c, sm, sp).astype(BF16)
                    dproj_ref[:, cols] = d
                    dhn = dhn + _dot_nt(d, w_ref[:, cols])
            else:
                d = d_ref[...].astype(BF16)
                if wd % 128 == 0:
                    dproj_ref[:, st:st + wd] = d
                dhn = dhn + _dot_nt(d, w_ref[:, st:st + wd])
        dx, dg = _rms_bwd(x_ref[...], g_ref[...], dhn)
        out_ref[...] = dres_ref[...] + dx

        @pl.when(i == 0)
        def _():
            dg_ref[...] = dg

        @pl.when(i > 0)
        def _():
            dg_ref[...] += dg

    tab_specs = [pl.BlockSpec((tt, 128), lambda i: (i, 0))] * n_rope
    out_specs = [pl.BlockSpec((tt, D), lambda i: (i, 0)), pl.BlockSpec((1, D), lambda i: (0, 0))]
    out_shape = [jax.ShapeDtypeStruct((T, D), F32), jax.ShapeDtypeStruct((1, D), F32)]
    out_specs.append(pl.BlockSpec((tt, n_main), lambda i: (i, 0)))
    out_shape.append(jax.ShapeDtypeStruct((T, n_main), BF16))
    return pl.pallas_call(
        body,
        grid=(T // tt,),
        in_specs=[
            pl.BlockSpec((tt, D), lambda i: (i, 0)),
            pl.BlockSpec((1, D), lambda i: (0, 0)),
            pl.BlockSpec((D, n), lambda i: (0, 0)),
            pl.BlockSpec((tt, D), lambda i: (i, 0)),
        ] + [pl.BlockSpec((tt, wd), lambda i: (i, 0)) for _, wd in splits] + tab_specs,
        out_specs=out_specs,
        out_shape=out_shape,
        compiler_params=_cp("arbitrary"),
        name=name,
    )(x, g_row, w, dres, *dparts, *(rope or ()))


def mm_tn_full(x, d, out_dtype, name):
    k = x.shape[1]
    n = d.shape[1]
    wn = 512

    def body(x_ref, d_ref, o_ref):
        o_ref[...] = _dot_tn(x_ref[...], d_ref[...].astype(BF16)).astype(out_dtype)

    return pl.pallas_call(
        body,
        grid=(n // wn,),
        in_specs=[pl.BlockSpec((T, k), lambda j: (0, 0), pipeline_mode=pl.Buffered(1)),
                  pl.BlockSpec((T, wn), lambda j: (0, j))],
        out_specs=pl.BlockSpec((k, wn), lambda j: (0, j)),
        out_shape=jax.ShapeDtypeStruct((k, n), out_dtype),
        compiler_params=_cp("parallel"),
        name=name,
    )(x, d)


def cols_to_slabs(main, tail, name):
    nm = main.shape[1]
    n = nm + (0 if tail is None else tail.shape[1])
    w = n // N_DEV
    tr = 256

    def body(*refs):
        m_ref, o_ref = refs[0], refs[-1]
        for s in range(N_DEV):
            a, b = w * s, w * (s + 1)
            if b <= nm:
                o_ref[s] = m_ref[:, a:b].astype(BF16)
            else:
                o_ref[s, :, 0:nm - a] = m_ref[:, a:nm].astype(BF16)
                o_ref[s, :, nm - a:w] = refs[1][:, 0:b - nm].astype(BF16)

    arrays = [main] + ([] if tail is None else [tail])
    return pl.pallas_call(
        body,
        grid=(D // tr,),
        in_specs=[pl.BlockSpec((tr, a.shape[1]), lambda i: (i, 0)) for a in arrays],
        out_specs=pl.BlockSpec((N_DEV, tr, w), lambda i: (0, i, 0)),
        out_shape=jax.ShapeDtypeStruct((N_DEV, D, w), BF16),
        compiler_params=_cp("parallel"),
        name=name,
    )(*arrays)


def mm_tn(x, d, name):
    k = x.shape[1]
    n = d.shape[1]
    wn = n if n <= 512 else 512
    tt = 512

    def body(x_ref, d_ref, o_ref):
        i = pl.program_id(1)
        r = _dot_tn(x_ref[...], d_ref[...].astype(BF16))

        @pl.when(i == 0)
        def _():
            o_ref[...] = r

        @pl.when(i > 0)
        def _():
            o_ref[...] += r

    return pl.pallas_call(
        body,
        grid=(n // wn, T // tt),
        in_specs=[pl.BlockSpec((tt, k), lambda j, i: (i, 0)), pl.BlockSpec((tt, wn), lambda j, i: (i, j))],
        out_specs=pl.BlockSpec((k, wn), lambda j, i: (0, j)),
        out_shape=jax.ShapeDtypeStruct((k, n), F32),
        compiler_params=_cp("parallel", "arbitrary"),
        name=name,
    )(x, d)


CONV_RC = 128
CONV_PAD = 32


def hyb_conv_fwd(u, dw_w, dw_b, name):
    def body(ua_ref, ug_ref, w_ref, b_ref, o_ref, xpad):
        xpad[0:CONV_PAD, :] = jnp.zeros((CONV_PAD, 128), F32)
        xpad[CONV_PAD:, :] = ua_ref[...] * _sigmoid(ug_ref[...])
        for r in range(T // CONV_RC):
            acc = jnp.broadcast_to(b_ref[...], (CONV_RC, 128))
            for j in range(CONV_K):
                acc = acc + w_ref[pl.ds(j, 1), :] * xpad[pl.ds(r * CONV_RC + CONV_PAD - (CONV_K - 1) + j, CONV_RC), :]
            o_ref[r * CONV_RC:(r + 1) * CONV_RC, :] = acc

    nb = CONV_C // 128
    return pl.pallas_call(
        body,
        grid=(nb,),
        in_specs=[
            pl.BlockSpec((T, 128), lambda c: (0, c)),
            pl.BlockSpec((T, 128), lambda c: (0, nb + c)),
            pl.BlockSpec((32, 128), lambda c: (0, c)),
            pl.BlockSpec((1, 128), lambda c: (0, c)),
        ],
        out_specs=pl.BlockSpec((T, 128), lambda c: (0, c)),
        out_shape=jax.ShapeDtypeStruct((T, CONV_C), F32),
        scratch_shapes=[pltpu.VMEM((T + CONV_PAD, 128), F32)],
        compiler_params=_cp("parallel"),
        name=name,
    )(u, u, dw_w, dw_b)


def hyb_conv_bwd(dc, u, dw_w, name):
    def body(dc_ref, ua_ref, ug_ref, w_ref, da_ref, dgate_ref, dw_ref, db_ref, xpad, dcpad, dwacc):
        ua = ua_ref[...]
        sig = _sigmoid(ug_ref[...])
        xpad[0:CONV_PAD, :] = jnp.zeros((CONV_PAD, 128), F32)
        xpad[CONV_PAD:, :] = ua * sig
        dcpad[0:T, :] = dc_ref[...]
        dcpad[T:, :] = jnp.zeros((CONV_PAD, 128), F32)
        dwacc[...] = jnp.zeros_like(dwacc)
        dbacc = jnp.zeros((8, 128), F32)
        for r in range(T // CONV_RC):
            r0 = r * CONV_RC
            dcr = dc_ref[r0:r0 + CONV_RC, :]
            dbacc = dbacc + dcr.reshape(CONV_RC // 8, 8, 128).sum(axis=0)
            dglu = jnp.zeros((CONV_RC, 128), F32)
            for j in range(CONV_K):
                dglu = dglu + w_ref[pl.ds(j, 1), :] * dcpad[pl.ds(r0 + (CONV_K - 1) - j, CONV_RC), :]
                prod = dcr * xpad[pl.ds(r0 + CONV_PAD - (CONV_K - 1) + j, CONV_RC), :]
                dwacc[8 * j:8 * j + 8, :] += prod.reshape(CONV_RC // 8, 8, 128).sum(axis=0)
            sg = sig[r0:r0 + CONV_RC, :]
            da_ref[r0:r0 + CONV_RC, :] = dglu * sg
            dgate_ref[r0:r0 + CONV_RC, :] = dglu * ua[r0:r0 + CONV_RC, :] * sg * (1.0 - sg)
        for j in range(CONV_K):
            dw_ref[pl.ds(j, 1), :] = jnp.sum(dwacc[8 * j:8 * j + 8, :], axis=0, keepdims=True)
        dw_ref[pl.ds(CONV_K, 1), :] = jnp.zeros((1, 128), F32)
        db_ref[...] = jnp.sum(dbacc, axis=0, keepdims=True)

    nb = CONV_C // 128
    col = pl.BlockSpec((T, 128), lambda c: (0, c))
    return pl.pallas_call(
        body,
        grid=(nb,),
        in_specs=[col, col, pl.BlockSpec((T, 128), lambda c: (0, nb + c)), pl.BlockSpec((32, 128), lambda c: (0, c))],
        out_specs=[col, col, pl.BlockSpec((32, 128), lambda c: (0, c)), pl.BlockSpec((1, 128), lambda c: (0, c))],
        out_shape=[
            jax.ShapeDtypeStruct((T, CONV_C), F32),
            jax.ShapeDtypeStruct((T, CONV_C), F32),
            jax.ShapeDtypeStruct((32, CONV_C), F32),
            jax.ShapeDtypeStruct((1, CONV_C), F32),
        ],
        scratch_shapes=[
            pltpu.VMEM((T + CONV_PAD, 128), F32),
            pltpu.VMEM((T + CONV_PAD, 128), F32),
            pltpu.VMEM((8 * 32, 128), F32),
        ],
        compiler_params=_cp("parallel"),
        name=name,
    )(dc, u, u, dw_w)


ATT_SCALE = A_HD ** -0.5
N_BLK = T // BLK


def _att_masks():
    i = lax.broadcasted_iota(jnp.int32, (BLK, 2 * BLK), 0)
    j = lax.broadcasted_iota(jnp.int32, (BLK, 2 * BLK), 1)
    band = (j >= i) & (j <= i + BLK)
    i1 = lax.broadcasted_iota(jnp.int32, (BLK, BLK), 0)
    j1 = lax.broadcasted_iota(jnp.int32, (BLK, BLK), 1)
    return band, j1 <= i1


def _att_rows(d, t, first):
    if first:
        base = t
        return pl.ds(base, BLK, stride=d), pl.ds(base, BLK, stride=d)
    c = t % d
    n = t // d + 1
    base = c + (BLK * d) * n
    return pl.ds(base, BLK, stride=d), pl.ds(base - BLK * d, 2 * BLK, stride=d)


def _loop_pairs(n, block, per=2):
    def several(i, carry):
        for k in range(per):
            block(per * i + k, carry)
        return carry

    if n >= per:
        lax.fori_loop(0, n // per, several, 0)
    for t in range(n - n % per, n):
        block(t, 0)


def attn_fwd(qkv, name):
    def body(q_ref, k_ref, v_ref, o_ref, lse_ref, og, lg):
        band, tri = _att_masks()
        head0 = lax.broadcasted_iota(jnp.int32, (BLK, 128), 1) < A_HD
        for g, d in enumerate(DILATIONS):
            def block(t, carry, first, g=g, d=d):
                rq, rk = _att_rows(d, t, first)
                q2 = q_ref[rq, :]
                k2 = k_ref[rk, :].astype(BF16)
                v2 = v_ref[rk, :].astype(BF16)
                o_e, l_e = [], []
                for e in range(2):
                    qe = jnp.where(head0 if e == 0 else ~head0, q2, 0.0).astype(BF16)
                    s = _dot_nt(qe, k2) * ATT_SCALE
                    s = jnp.where(tri if first else band, s, -jnp.inf)
                    m = jnp.max(s, axis=-1, keepdims=True)
                    p = jnp.exp(s - m)
                    den = jnp.sum(p, axis=-1, keepdims=True)
                    o_e.append(_dot(p.astype(BF16), v2) / den)
                    l_e.append(m + jnp.log(den))
                og[g, rq, :] = jnp.where(head0, o_e[0], o_e[1])
                lg[g, rq, :] = jnp.where(head0, l_e[0], l_e[1])
                return carry

            _loop_pairs(d, functools.partial(block, first=True), per=4)
            _loop_pairs(N_BLK - d, functools.partial(block, first=False), per=4)
        rc = 256
        for r in range(T // rc):
            rows = pl.ds(r * rc, rc)
            l0, l1, l2 = lg[0, rows, :], lg[1, rows, :], lg[2, rows, :]
            m = jnp.maximum(jnp.maximum(l0, l1), l2)
            e0, e1, e2 = jnp.exp(l0 - m), jnp.exp(l1 - m), jnp.exp(l2 - m)
            z = e0 + e1 + e2
            o_ref[rows, :] = (e0 / z) * og[0, rows, :] + (e1 / z) * og[1, rows, :] + (e2 / z) * og[2, rows, :]
            lse_ref[rows, :] = m + jnp.log(z)

    npair = A_HEADS // 2
    col = lambda off: pl.BlockSpec((T, 128), lambda p: (0, off + p))
    return pl.pallas_call(
        body,
        grid=(npair,),
        in_specs=[col(0), col(npair), col(2 * npair)],
        out_specs=[col(0), col(0)],
        out_shape=[jax.ShapeDtypeStruct((T, A_W), F32), jax.ShapeDtypeStruct((T, A_W), F32)],
        scratch_shapes=[pltpu.VMEM((3, T, 128), F32), pltpu.VMEM((3, T, 128), F32)],
        compiler_params=_cp("parallel"),
        name=name,
    )(qkv, qkv, qkv)


def attn_bwd(qkv, o, lse, do, name):
    def body(q_ref, k_ref, v_ref, o_ref, lse_ref, do_ref, dq_ref, dk_ref, dv_ref):
        band, tri = _att_masks()
        head0 = lax.broadcasted_iota(jnp.int32, (BLK, 128), 1) < A_HD
        head0k = lax.broadcasted_iota(jnp.int32, (2 * BLK, 128), 1) < A_HD
        dq_ref[...] = jnp.zeros_like(dq_ref)
        dk_ref[...] = jnp.zeros_like(dk_ref)
        dv_ref[...] = jnp.zeros_like(dv_ref)
        for d in DILATIONS:
            def block(t, carry, first, d=d):
                rq, rk = _att_rows(d, t, first)
                q2 = q_ref[rq, :]
                k2 = k_ref[rk, :].astype(BF16)
                v2 = v_ref[rk, :].astype(BF16)
                do2 = do_ref[rq, :]
                l2 = lse_ref[rq, :]
                prod = do2 * o_ref[rq, :]
                q2b = q2.astype(BF16)
                do2b = do2.astype(BF16)
                dq_e, dk_e, dv_e = [], [], []
                for e in range(2):
                    he = head0 if e == 0 else ~head0
                    qe = jnp.where(he, q2, 0.0).astype(BF16)
                    doe = jnp.where(he, do2, 0.0).astype(BF16)
                    l = l2[:, A_HD * e:A_HD * e + 1]
                    dd = jnp.sum(jnp.where(he, prod, 0.0), axis=-1, keepdims=True)
                    s = _dot_nt(qe, k2) * ATT_SCALE
                    p = jnp.where(tri if first else band, jnp.exp(s - l), 0.0)
                    dp = _dot_nt(doe, v2)
                    ds = (p * (dp - dd) * ATT_SCALE).astype(BF16)
                    dq_e.append(_dot(ds, k2))
                    dk_e.append(_dot_tn(ds, q2b))
                    dv_e.append(_dot_tn(p.astype(BF16), do2b))
                hk = head0 if first else head0k
                dq_ref[rq, :] += jnp.where(head0, dq_e[0], dq_e[1])
                dk_ref[rk, :] += jnp.where(hk, dk_e[0], dk_e[1])
                dv_ref[rk, :] += jnp.where(hk, dv_e[0], dv_e[1])
                return carry

            _loop_pairs(d, functools.partial(block, first=True), per=4)
            _loop_pairs(N_BLK - d, functools.partial(block, first=False), per=4)

    npair = A_HEADS // 2
    col = lambda off: pl.BlockSpec((T, 128), lambda p: (0, off + p))
    return pl.pallas_call(
        body,
        grid=(npair,),
        in_specs=[col(0), col(npair), col(2 * npair), col(0), col(0), col(0)],
        out_specs=[col(0), col(0), col(0)],
        out_shape=[jax.ShapeDtypeStruct((T, A_W), F32)] * 3,
        compiler_params=_cp("parallel"),
        name=name,
    )(qkv, qkv, qkv, o, lse, do)


def _ln_silu(x, g, b):
    mu = jnp.mean(x, axis=-1, keepdims=True)
    xc = x - mu
    rstd = lax.rsqrt(jnp.mean(xc * xc, axis=-1, keepdims=True) + EPS)
    xh = xc * rstd
    y = xh * g + b
    sig = _sigmoid(y)
    return y * sig, (xh, rstd, y, sig)


def hyb_out_fwd(h, attn, cpre, ln_g, ln_b, w_out, name):
    tt = 512

    def body(h_ref, a_ref, c_ref, g_ref, b_ref, w_ref, hnew_ref, cat_ref):
        cn, _ = _ln_silu(c_ref[...], g_ref[...], b_ref[...])
        ab = a_ref[...].astype(BF16)
        cb = cn.astype(BF16)
        cat_ref[:, 0:A_W] = ab
        cat_ref[:, A_W:D] = cb
        hnew_ref[...] = h_ref[...] + _dot(ab, w_ref[0:A_W, :]) + _dot(cb, w_ref[A_W:D, :])

    half = pl.BlockSpec((tt, A_W), lambda i: (i, 0))
    vec = pl.BlockSpec((1, CONV_C), lambda i: (0, 0))
    full = pl.BlockSpec((tt, D), lambda i: (i, 0))
    return pl.pallas_call(
        body,
        grid=(T // tt,),
        in_specs=[full, half, half, vec, vec, pl.BlockSpec((D, D), lambda i: (0, 0))],
        out_specs=[full, full],
        out_shape=[jax.ShapeDtypeStruct((T, D), F32), jax.ShapeDtypeStruct((T, D), BF16)],
        compiler_params=_cp("parallel"),
        name=name,
    )(h, attn, cpre, ln_g, ln_b, w_out)


def hyb_out_bwd(dres, cpre, ln_g, ln_b, w_out, name):
    tt = 512

    def body(d_ref, c_ref, g_ref, b_ref, w_ref, da_ref, dc_ref, dg_ref, db_ref):
        i = pl.program_id(0)
        db16 = d_ref[...].astype(BF16)
        da_ref[...] = _dot_nt(db16, w_ref[0:A_W, :])
        dcn = _dot_nt(db16, w_ref[A_W:D, :])
        g = g_ref[...]
        _, (xh, rstd, y, sig) = _ln_silu(c_ref[...], g, b_ref[...])
        dy = dcn * _dsilu(y, sig)
        dxh = dy * g
        dc_ref[...] = rstd * (dxh - jnp.mean(dxh, axis=-1, keepdims=True)
                              - xh * jnp.mean(dxh * xh, axis=-1, keepdims=True))
        dg = jnp.sum(dy * xh, axis=0, keepdims=True)
        db = jnp.sum(dy, axis=0, keepdims=True)

        @pl.when(i == 0)
        def _():
            dg_ref[...] = dg
            db_ref[...] = db

        @pl.when(i > 0)
        def _():
            dg_ref[...] += dg
            db_ref[...] += db

    half = pl.BlockSpec((tt, A_W), lambda i: (i, 0))
    vec = pl.BlockSpec((1, CONV_C), lambda i: (0, 0))
    return pl.pallas_call(
        body,
        grid=(T // tt,),
        in_specs=[pl.BlockSpec((tt, D), lambda i: (i, 0)), half, vec, vec, pl.BlockSpec((D, D), lambda i: (0, 0))],
        out_specs=[half, half, vec, vec],
        out_shape=[
            jax.ShapeDtypeStruct((T, A_W), F32),
            jax.ShapeDtypeStruct((T, CONV_C), F32),
            jax.ShapeDtypeStruct((1, CONV_C), F32),
            jax.ShapeDtypeStruct((1, CONV_C), F32),
        ],
        compiler_params=_cp("arbitrary"),
        name=name,
    )(dres, cpre, ln_g, ln_b, w_out)


def hybrid_fwd(h, g_row, w_in, dw_w, dw_b, ln_g, ln_b, w_out, rope, tag):
    hn, qkv, u = proj_fwd(h, g_row, w_in, [(0, 3 * A_W), (3 * A_W, 2 * CONV_C)], f"hyb_proj_{tag}", rope=rope)
    cpre = hyb_conv_fwd(u, dw_w, dw_b, f"hyb_conv_{tag}")
    attn, lse = attn_fwd(qkv, f"attn_fwd_{tag}")
    hnew, cat = hyb_out_fwd(h, attn, cpre, ln_g, ln_b, w_out, f"hyb_out_{tag}")
    return hnew, (h, hn, qkv, u, cpre, attn, lse, cat)


def hybrid_bwd(dres, saved, g_row, w_in, dw_w, ln_g, ln_b, w_out, rope, tag):
    h, hn, qkv, u, cpre, attn, lse, cat = saved
    d_attn, d_cpre, d_lng, d_lnb = hyb_out_bwd(dres, cpre, ln_g, ln_b, w_out, f"hyb_out_bwd_{tag}")
    d_wout = mm_tn_full(cat, dres, BF16, f"hyb_wout_grad_{tag}")
    d_a, d_gate, d_dw, d_db = hyb_conv_bwd(d_cpre, u, dw_w, f"hyb_conv_bwd_{tag}")
    dq, dk, dv = attn_bwd(qkv, attn, lse, d_attn, f"attn_bwd_{tag}")
    splits = [(0, A_W), (A_W, A_W), (2 * A_W, A_W), (3 * A_W, CONV_C), (3 * A_W + CONV_C, CONV_C)]
    dres_new, d_norm, dproj = proj_bwd_data(
        h, g_row, w_in, [dq, dk, dv, d_a, d_gate], splits, dres, f"hyb_proj_bwd_{tag}", rope=rope, n_rot=2)
    d_win = cols_to_slabs(mm_tn_full(hn, dproj, F32, f"hyb_win_grad_{tag}"), None, f"hyb_win_slabs_{tag}")
    return dres_new, dict(norm=d_norm, w_in=d_win, dw_w=d_dw[:CONV_K], dw_b=d_db, ln_g=d_lng, ln_b=d_lnb, w_out=d_wout)


G_SCALE = G_DK ** -0.5
GP_RC = 256
GP_PAD = 8


def gdn_prep_fwd(x, conv_w, name):
    def body(x_ref, w_ref, o_ref, xpad):
        cb = pl.program_id(0)
        xpad[0:GP_PAD, :] = jnp.zeros((GP_PAD, 128), F32)
        xpad[GP_PAD:, :] = x_ref[...]
        for r in range(T // GP_RC):
            r0 = r * GP_RC
            y = jnp.zeros((GP_RC, 128), F32)
            for j in range(G_CONV):
                y = y + w_ref[pl.ds(j, 1), :] * xpad[pl.ds(r0 + GP_PAD - (G_CONV - 1) + j, GP_RC), :]
            s = y * _sigmoid(y)
            n = lax.rsqrt(jnp.sum(s * s, axis=-1, keepdims=True) + EPS)
            o_ref[r0:r0 + GP_RC, :] = s * jnp.where(cb < 2 * G_HEADS, n, 1.0)

    nb = G_QKV // 128
    return pl.pallas_call(
        body,
        grid=(nb,),
        in_specs=[pl.BlockSpec((T, 128), lambda c: (0, c)), pl.BlockSpec((G_CONV, 128), lambda c: (0, c))],
        out_specs=pl.BlockSpec((T, 128), lambda c: (0, c)),
        out_shape=jax.ShapeDtypeStruct((T, G_QKV), F32),
        scratch_shapes=[pltpu.VMEM((T + GP_PAD, 128), F32)],
        compiler_params=_cp("parallel"),
        name=name,
    )(x, conv_w)


def gdn_prep_bwd(dout, x, conv_w, part, l2, name):
    def body(d_ref, x_ref, w_ref, dx_ref, dw_ref, xpad, dypad, dwacc):
        xpad[0:GP_PAD, :] = jnp.zeros((GP_PAD, 128), F32)
        xpad[GP_PAD:, :] = x_ref[...]
        dypad[T:, :] = jnp.zeros((GP_PAD, 128), F32)
        dwacc[...] = jnp.zeros_like(dwacc)
        for r in range(T // GP_RC):
            r0 = r * GP_RC
            y = jnp.zeros((GP_RC, 128), F32)
            xs = []
            for j in range(G_CONV):
                xj = xpad[pl.ds(r0 + GP_PAD - (G_CONV - 1) + j, GP_RC), :]
                xs.append(xj)
                y = y + w_ref[pl.ds(j, 1), :] * xj
            sig = _sigmoid(y)
            s = y * sig
            d = d_ref[r0:r0 + GP_RC, :]
            if l2:
                n = lax.rsqrt(jnp.sum(s * s, axis=-1, keepdims=True) + EPS)
                out = s * n
                d = n * (d - out * jnp.sum(d * out, axis=-1, keepdims=True))
            dy = d * _dsilu(y, sig)
            dypad[r0:r0 + GP_RC, :] = dy
            for j in range(G_CONV):
                dwacc[8 * j:8 * j + 8, :] += (dy * xs[j]).reshape(GP_RC // 8, 8, 128).sum(axis=0)
        for r in range(T // GP_RC):
            r0 = r * GP_RC
            dx = jnp.zeros((GP_RC, 128), F32)
            for j in range(G_CONV):
                dx = dx + w_ref[pl.ds(j, 1), :] * dypad[pl.ds(r0 + (G_CONV - 1) - j, GP_RC), :]
            dx_ref[r0:r0 + GP_RC, :] = dx
        for j in range(G_CONV):
            dw_ref[pl.ds(j, 1), :] = jnp.sum(dwacc[8 * j:8 * j + 8, :], axis=0, keepdims=True)

    nb = G_HEADS
    off = part * nb
    col = pl.BlockSpec((T, 128), lambda c: (0, c))
    return pl.pallas_call(
        body,
        grid=(nb,),
        in_specs=[col, pl.BlockSpec((T, 128), lambda c: (0, off + c)), pl.BlockSpec((G_CONV, 128), lambda c: (0, off + c))],
        out_specs=[col, pl.BlockSpec((G_CONV, 128), lambda c: (0, c))],
        out_shape=[jax.ShapeDtypeStruct((T, G_HEADS * G_DK), F32), jax.ShapeDtypeStruct((G_CONV, G_HEADS * G_DK), F32)],
        scratch_shapes=[
            pltpu.VMEM((T + GP_PAD, 128), F32),
            pltpu.VMEM((T + GP_PAD, 128), F32),
            pltpu.VMEM((8 * G_CONV, 128), F32),
        ],
        compiler_params=_cp("parallel"),
        name=name,
    )(dout, x, conv_w)


def _seg_cumsum(x, reverse=False):
    row = lax.broadcasted_iota(jnp.int32, x.shape, 0) % CH
    s = 1
    while s < CH:
        if reverse:
            x = x + jnp.where(row < CH - s, pltpu.roll(x, x.shape[0] - s, 0), 0.0)
        else:
            x = x + jnp.where(row >= s, pltpu.roll(x, s, 0), 0.0)
        s *= 2
    return x


def _gdn_gates(ba_ref, alog_ref, dt_ref, h):
    ba = ba_ref[...]
    lane = lax.broadcasted_iota(jnp.int32, ba.shape, 1)
    b_col = jnp.sum(jnp.where(lane == h, ba, 0.0), axis=1, keepdims=True)
    a_col = jnp.sum(jnp.where(lane == G_HEADS + h, ba, 0.0), axis=1, keepdims=True)
    lane8 = lax.broadcasted_iota(jnp.int32, (1, G_HEADS), 1)
    alog = jnp.sum(jnp.where(lane8 == h, alog_ref[...], 0.0), axis=1, keepdims=True)
    dt = jnp.sum(jnp.where(lane8 == h, dt_ref[...], 0.0), axis=1, keepdims=True)
    beta = _sigmoid(b_col)
    xa = a_col + dt
    softplus = jnp.maximum(xa, 0.0) + jnp.log(1.0 + jnp.exp(-jnp.abs(xa)))
    ea = jnp.exp(alog)
    return beta, -ea * softplus, xa, ea


def _chunk_masks():
    i = lax.broadcasted_iota(jnp.int32, (CH, CH), 0)
    j = lax.broadcasted_iota(jnp.int32, (CH, CH), 1)
    return i >= j, i > j, i, j


def _decay(gcc, causal):
    gm = gcc[:, 0:CH]
    return jnp.where(causal, jnp.exp(jnp.minimum(gm - gm.T, 0.0)), 0.0)


def _split(a):
    hi = a.astype(BF16)
    return hi, (a - hi.astype(F32)).astype(BF16)


def _dot3(a, b):
    ah, al = _split(a)
    bh, bl = _split(b)
    return _dot(ah, bh) + (_dot(ah, bl) + _dot(al, bh))


def _unit_lower_inverse(lms, i, j):
    eye = jnp.where(i == j, 1.0, 0.0)
    ms = [None] * len(lms)
    b = 1
    while b < CH:
        pair = ((i // (2 * b)) == (j // (2 * b))) & ((i // b) % 2 == 1) & ((j // b) % 2 == 0)
        lbs = [jnp.where(pair, lm, 0.0) for lm in lms]
        if b == 1:
            ms = [eye - lb for lb in lbs]
        else:
            ts = [_dot3(m, lb) for m, lb in zip(ms, lbs)]
            ms = [m - _dot3(t, m) for m, t in zip(ms, ts)]
        b *= 2
    return ms


def gdn_local_fwd(qkv, ba, alog, dtb, name):
    def body(q_ref, k_ref, v_ref, ba_ref, al_ref, dt_ref, u_ref, w_ref, qd_ref, kd_ref, at_ref, el_ref, ti_ref, gcs):
        h = pl.program_id(1)
        beta, g, _, _ = _gdn_gates(ba_ref, al_ref, dt_ref, h)
        gc = _seg_cumsum(jnp.broadcast_to(g, (GRP, 128)))
        gcs[...] = gc
        causal, strict, i, j = _chunk_masks()
        lms = []
        for c in range(CPG):
            r = slice(c * CH, (c + 1) * CH)
            q, k = q_ref[r, :], k_ref[r, :]
            gcc = gc[r, :]
            ec = jnp.exp(gcc)
            gl = gcs[pl.ds(c * CH + CH - 1, 1), :]
            dm = _decay(gcc, causal)
            kbf = k.astype(BF16)
            a1 = _dot_nt((k * beta[r, :]).astype(BF16), kbf)
            lms.append(jnp.where(strict, a1 * dm, 0.0))
            qs = q * G_SCALE
            qd_ref[r, :] = (qs * ec).astype(BF16)
            kd_ref[r, :] = (k * jnp.exp(gl - gcc)).astype(BF16)
            at_ref[r, :] = (_dot_nt(qs.astype(BF16), kbf) * dm).astype(BF16)
            el_ref[pl.ds(c, 1), :] = jnp.exp(gl)
        tinvs = _unit_lower_inverse(lms, i, j)
        for c in range(CPG):
            r = slice(c * CH, (c + 1) * CH)
            bt = beta[r, :]
            tb = tinvs[c].astype(BF16)
            u_ref[r, :] = _dot(tb, (v_ref[r, :] * bt).astype(BF16))
            w_ref[r, :] = _dot(tb, (k_ref[r, :] * bt * jnp.exp(gc[r, :])).astype(BF16)).astype(BF16)
            ti_ref[r, :] = tinvs[c]

    hd = lambda off: pl.BlockSpec((GRP, 128), lambda i, h: (i, off + h))
    vec = pl.BlockSpec((1, G_HEADS), lambda i, h: (0, 0))
    sq = pl.BlockSpec((None, GRP, CH), lambda i, h: (h, i, 0))
    return pl.pallas_call(
        body,
        grid=(N_GRP, G_HEADS),
        in_specs=[hd(0), hd(G_HEADS), hd(2 * G_HEADS), pl.BlockSpec((GRP, 2 * G_HEADS), lambda i, h: (i, 0)), vec, vec],
        out_specs=[hd(0), hd(0), hd(0), hd(0), sq, pl.BlockSpec((None, CPG, 128), lambda i, h: (h, i, 0)), sq],
        out_shape=[
            jax.ShapeDtypeStruct((T, D), F32),
            jax.ShapeDtypeStruct((T, D), BF16),
            jax.ShapeDtypeStruct((T, D), BF16),
            jax.ShapeDtypeStruct((T, D), BF16),
            jax.ShapeDtypeStruct((G_HEADS, T, CH), BF16),
            jax.ShapeDtypeStruct((G_HEADS, T // CH, 128), F32),
            jax.ShapeDtypeStruct((G_HEADS, T, CH), F32),
        ],
        scratch_shapes=[pltpu.VMEM((GRP, 128), F32)],
        compiler_params=_cp("parallel", "parallel"),
        name=name,
    )(qkv, qkv, qkv, ba, alog, dtb)


def gdn_rec_fwd(u, w, qd, kd, at, el, name):
    def body(u_ref, w_ref, qd_ref, kd_ref, at_ref, el_ref, o_ref, vn_ref, st_ref, s_scr):
        @pl.when(pl.program_id(0) == 0)
        def _():
            s_scr[...] = jnp.zeros_like(s_scr)

        states = [s_scr[h] for h in range(G_HEADS)]
        for c in range(CPG):
            r = slice(c * CH, (c + 1) * CH)
            for h in range(G_HEADS):
                ln = slice(h * 128, (h + 1) * 128)
                s = states[h]
                st_ref[h, c] = s
                sb = s.astype(BF16)
                vn = (u_ref[r, ln] - _dot(w_ref[r, ln], sb)).astype(BF16)
                o_ref[r, ln] = _dot(qd_ref[r, ln], sb) + _dot(at_ref[h, r, :], vn)
                states[h] = s * el_ref[h, pl.ds(c, 1), :] + _dot_tn(kd_ref[r, ln], vn)
                vn_ref[r, ln] = vn
        for h in range(G_HEADS):
            s_scr[h] = states[h]

    row = pl.BlockSpec((GRP, D), lambda i: (i, 0))
    return pl.pallas_call(
        body,
        grid=(N_GRP,),
        in_specs=[row, row, row, row, pl.BlockSpec((G_HEADS, GRP, CH), lambda i: (0, i, 0)),
                  pl.BlockSpec((G_HEADS, CPG, 128), lambda i: (0, i, 0))],
        out_specs=[row, row, pl.BlockSpec((G_HEADS, CPG, 128, 128), lambda i: (0, i, 0, 0))],
        out_shape=[
            jax.ShapeDtypeStruct((T, D), F32),
            jax.ShapeDtypeStruct((T, D), BF16),
            jax.ShapeDtypeStruct((G_HEADS, T // CH, 128, 128), F32),
        ],
        scratch_shapes=[pltpu.VMEM((G_HEADS, 128, 128), F32)],
        compiler_params=_cp("arbitrary"),
        name=name,
    )(u, w, qd, kd, at, el)


def gdn_rec_bwd(do, w, qd, kd, at, el, vn, st, name):
    def body(do_ref, w_ref, qd_ref, kd_ref, at_ref, el_ref, vn_ref, st_ref,
             du_ref, dw_ref, dqd_ref, dkd_ref, dat_ref, del_ref, ds_scr):
        @pl.when(pl.program_id(0) == 0)
        def _():
            ds_scr[...] = jnp.zeros_like(ds_scr)

        dstates = [ds_scr[h] for h in range(G_HEADS)]
        for c in reversed(range(CPG)):
            r = slice(c * CH, (c + 1) * CH)
            for h in range(G_HEADS):
                ln = slice(h * 128, (h + 1) * 128)
                ds = dstates[h]
                dsb = ds.astype(BF16)
                sn = st_ref[h, c]
                snb = sn.astype(BF16)
                dob = do_ref[r, ln].astype(BF16)
                vnb = vn_ref[r, ln]
                dvn = (_dot(kd_ref[r, ln], dsb) + _dot_tn(at_ref[h, r, :], dob)).astype(BF16)
                du_ref[r, ln] = dvn
                dkd_ref[r, ln] = _dot_nt(vnb, dsb)
                tot = jnp.sum(jnp.sum(ds * sn, axis=1, keepdims=True), axis=0, keepdims=True)
                del_ref[h, pl.ds(c, 1), :] = jnp.broadcast_to(tot, (1, 128))
                dqd_ref[r, ln] = _dot_nt(dob, snb)
                dat_ref[h, r, :] = _dot_nt(dob, vnb)
                dw_ref[r, ln] = (-_dot_nt(dvn, snb)).astype(BF16)
                dstates[h] = ds * el_ref[h, pl.ds(c, 1), :] + _dot_tn(qd_ref[r, ln], dob) - _dot_tn(w_ref[r, ln], dvn)
        for h in range(G_HEADS):
            ds_scr[h] = dstates[h]

    last = N_GRP - 1
    row = pl.BlockSpec((GRP, D), lambda i: (last - i, 0))
    sq = pl.BlockSpec((G_HEADS, GRP, CH), lambda i: (0, last - i, 0))
    sc = pl.BlockSpec((G_HEADS, CPG, 128), lambda i: (0, last - i, 0))
    return pl.pallas_call(
        body,
        grid=(N_GRP,),
        in_specs=[row, row, row, row, sq, sc, row, pl.BlockSpec((G_HEADS, CPG, 128, 128), lambda i: (0, last - i, 0, 0))],
        out_specs=[row, row, row, row, sq, sc],
        out_shape=[
            jax.ShapeDtypeStruct((T, D), BF16),
            jax.ShapeDtypeStruct((T, D), BF16),
            jax.ShapeDtypeStruct((T, D), F32),
            jax.ShapeDtypeStruct((T, D), F32),
            jax.ShapeDtypeStruct((G_HEADS, T, CH), F32),
            jax.ShapeDtypeStruct((G_HEADS, T // CH, 128), F32),
        ],
        scratch_shapes=[pltpu.VMEM((G_HEADS, 128, 128), F32)],
        compiler_params=_cp("arbitrary"),
        name=name,
    )(do, w, qd, kd, at, el, vn, st)


def gdn_local_bwd(qkv, ba, alog, dtb, tinv, du, dw, dqd, dkd, dat, dl, name):
    def body(q_ref, k_ref, v_ref, ba_ref, al_ref, dt_ref, ti_ref, du_ref, dw_ref, dqd_ref, dkd_ref, dat_ref, dl_ref,
             dq_ref, dk_ref, dv_ref, dba_ref, dal_ref, ddt_ref, gcs):
        gi = pl.program_id(0)
        h = pl.program_id(1)
        beta, g, xa, ea = _gdn_gates(ba_ref, al_ref, dt_ref, h)
        gc = _seg_cumsum(jnp.broadcast_to(g, (GRP, 128)))
        gcs[...] = gc
        causal, strict, _, _ = _chunk_masks()
        dgc_l, dgl_l, dbeta_l, state = [], [], [], []
        for c in range(CPG):
            r = slice(c * CH, (c + 1) * CH)
            q, k, v = q_ref[r, :], k_ref[r, :], v_ref[r, :]
            bt = beta[r, :]
            gcc = gc[r, :]
            ec = jnp.exp(gcc)
            gl = gcs[pl.ds(c * CH + CH - 1, 1), :]
            f2 = jnp.exp(gl - gcc)
            elc = jnp.exp(gl)
            dm = _decay(gcc, causal)
            qs = q * G_SCALE
            kb = k * bt
            vb = v * bt
            kbe = kb * ec
            kbf, kbb, qsb = k.astype(BF16), kb.astype(BF16), qs.astype(BF16)
            a1 = _dot_nt(kbb, kbf)
            qk = _dot_nt(qsb, kbf)
            ti = ti_ref[r, :]
            tb = ti.astype(BF16)
            du_c, dw_c = du_ref[r, :], dw_ref[r, :]
            dqd_c, dkd_c, dat_c = dqd_ref[r, :], dkd_ref[r, :], dat_ref[r, :]

            dqs = dqd_c * ec
            d_e = jnp.sum(dqd_c * qs, axis=1, keepdims=True)
            dk = dkd_c * f2
            tcol = jnp.sum(dkd_c * k, axis=1, keepdims=True) * f2[:, 0:1]
            dgl = jnp.sum(tcol, axis=0, keepdims=True) + dl_ref[pl.ds(c, 1), 0:1] * elc[:, 0:1]
            dgc = -tcol
            dqk = (dat_c * dm).astype(BF16)
            d_d = dat_c * qk
            dqs = dqs + _dot(dqk, kbf)
            dk = dk + _dot_tn(dqk, qsb)
            dtinv = _dot_nt(du_c, vb.astype(BF16)) + _dot_nt(dw_c, kbe.astype(BF16))
            dvb = _dot_tn(tb, du_c)
            dkbe = _dot_tn(tb, dw_c)
            dq_ref[r, :] = dqs * G_SCALE
            state.append((ti.T, dtinv, dm, a1, dkbe, dvb, d_d, dk, d_e, dgc, dgl))

        xs = [_dot3(st[0], st[1]) for st in state]
        dlms = [jnp.where(strict, -_dot3(x, st[0]), 0.0) for x, st in zip(xs, state)]

        for c in range(CPG):
            r = slice(c * CH, (c + 1) * CH)
            _, _, dm, a1, dkbe, dvb, d_d, dk, d_e, dgc, dgl = state[c]
            dlm = dlms[c]
            k, v = k_ref[r, :], v_ref[r, :]
            bt = beta[r, :]
            ec = jnp.exp(gc[r, :])
            kb = k * bt
            kbf, kbb = k.astype(BF16), kb.astype(BF16)
            da1 = (dlm * dm).astype(BF16)
            d_d = d_d + dlm * a1
            dkb = _dot(da1, kbf) + dkbe * ec
            dk = dk + _dot_tn(da1, kbb)
            d_e = d_e + jnp.sum(dkbe * kb, axis=1, keepdims=True)
            dk = dk + dkb * bt
            dbeta_l.append(jnp.sum(dkb * k, axis=1, keepdims=True) + jnp.sum(dvb * v, axis=1, keepdims=True))
            ddiff = d_d * dm
            dgc = dgc + jnp.sum(ddiff, axis=1, keepdims=True) - jnp.sum(ddiff.T, axis=1, keepdims=True)
            dgc = dgc + d_e * ec[:, 0:1]
            dgc_l.append(dgc)
            dgl_l.append(jnp.broadcast_to(dgl, (CH, 1)))
            dk_ref[r, :] = dk
            dv_ref[r, :] = dvb * bt

        dgc_all = jnp.broadcast_to(jnp.concatenate(dgc_l, axis=0), (GRP, 128))
        dg = _seg_cumsum(dgc_all, reverse=True)[:, 0:1] + jnp.concatenate(dgl_l, axis=0)
        dbeta = jnp.concatenate(dbeta_l, axis=0)
        da = dg * (-ea) * _sigmoid(xa)
        db = dbeta * beta * (1.0 - beta)
        lane = lax.broadcasted_iota(jnp.int32, (GRP, 2 * G_HEADS), 1)
        dba = jnp.where(lane == h, db, 0.0) + jnp.where(lane == G_HEADS + h, da, 0.0)
        lane8 = lax.broadcasted_iota(jnp.int32, (1, G_HEADS), 1)
        dal = jnp.where(lane8 == h, jnp.sum(dg * g, axis=0, keepdims=True), 0.0)
        ddt = jnp.where(lane8 == h, jnp.sum(da, axis=0, keepdims=True), 0.0)

        @pl.when(h == 0)
        def _():
            dba_ref[...] = dba

        @pl.when(h > 0)
        def _():
            dba_ref[...] += dba

        @pl.when((h == 0) & (gi == 0))
        def _():
            dal_ref[...] = dal
            ddt_ref[...] = ddt

        @pl.when((h > 0) | (gi > 0))
        def _():
            dal_ref[...] += dal
            ddt_ref[...] += ddt

    hd = lambda off: pl.BlockSpec((GRP, 128), lambda i, h: (i, off + h))
    vec = pl.BlockSpec((1, G_HEADS), lambda i, h: (0, 0))
    sq = pl.BlockSpec((None, GRP, CH), lambda i, h: (h, i, 0))
    gates = pl.BlockSpec((GRP, 2 * G_HEADS), lambda i, h: (i, 0))
    return pl.pallas_call(
        body,
        grid=(N_GRP, G_HEADS),
        in_specs=[hd(0), hd(G_HEADS), hd(2 * G_HEADS), gates, vec, vec, sq, hd(0), hd(0), hd(0), hd(0), sq,
                  pl.BlockSpec((None, CPG, 128), lambda i, h: (h, i, 0))],
        out_specs=[hd(0), hd(0), hd(0), gates, vec, vec],
        out_shape=[
            jax.ShapeDtypeStruct((T, D), F32),
            jax.ShapeDtypeStruct((T, D), F32),
            jax.ShapeDtypeStruct((T, D), F32),
            jax.ShapeDtypeStruct((T, 2 * G_HEADS), F32),
            jax.ShapeDtypeStruct((1, G_HEADS), F32),
            jax.ShapeDtypeStruct((1, G_HEADS), F32),
        ],
        scratch_shapes=[pltpu.VMEM((GRP, 128), F32)],
        compiler_params=_cp("arbitrary", "arbitrary"),
        name=name,
    )(qkv, qkv, qkv, ba, alog, dtb, tinv, du, dw, dqd, dkd, dat, dl)


def _gated_norm(o, z, g):
    rstd = lax.rsqrt(jnp.mean(o * o, axis=-1, keepdims=True) + EPS)
    oh = o * rstd
    sig = _sigmoid(z)
    return oh, rstd, sig


def gdn_out_fwd(h, o, z, norm_g, w_out, name):
    tt = 512

    def body(h_ref, o_ref, z_ref, g_ref, w_ref, hnew_ref, cat_ref):
        g = g_ref[...]
        for hh in range(G_HEADS):
            ln = slice(hh * 128, (hh + 1) * 128)
            zz = z_ref[:, ln]
            oh, _, sig = _gated_norm(o_ref[:, ln], zz, g)
            cat_ref[:, ln] = (oh * g * (zz * sig)).astype(BF16)
        hnew_ref[...] = h_ref[...] + _dot(cat_ref[...], w_ref[...])

    full = pl.BlockSpec((tt, D), lambda i: (i, 0))
    return pl.pallas_call(
        body,
        grid=(T // tt,),
        in_specs=[full, full, full, pl.BlockSpec((1, 128), lambda i: (0, 0)), pl.BlockSpec((D, D), lambda i: (0, 0))],
        out_specs=[full, full],
        out_shape=[jax.ShapeDtypeStruct((T, D), F32), jax.ShapeDtypeStruct((T, D), BF16)],
        compiler_params=_cp("parallel"),
        name=name,
    )(h, o, z, norm_g, w_out)


def gdn_out_bwd(dres, o, z, norm_g, w_out, name):
    tt = 512

    def body(d_ref, o_ref, z_ref, g_ref, w_ref, do_ref, dz_ref, dg_ref, dcat):
        i = pl.program_id(0)
        g = g_ref[...]
        dcat[...] = _dot_nt(d_ref[...].astype(BF16), w_ref[...])
        dg = jnp.zeros((1, 128), F32)
        for hh in range(G_HEADS):
            ln = slice(hh * 128, (hh + 1) * 128)
            zz = z_ref[:, ln]
            oh, rstd, sig = _gated_norm(o_ref[:, ln], zz, g)
            dout = dcat[:, ln]
            dy = dout * (zz * sig)
            dz_ref[:, ln] = dout * (oh * g) * _dsilu(zz, sig)
            dg = dg + jnp.sum(dy * oh, axis=0, keepdims=True)
            doh = dy * g
            do_ref[:, ln] = rstd * (doh - oh * jnp.mean(doh * oh, axis=-1, keepdims=True))

        @pl.when(i == 0)
        def _():
            dg_ref[...] = dg

        @pl.when(i > 0)
        def _():
            dg_ref[...] += dg

    full = pl.BlockSpec((tt, D), lambda i: (i, 0))
    vec = pl.BlockSpec((1, 128), lambda i: (0, 0))
    return pl.pallas_call(
        body,
        grid=(T // tt,),
        in_specs=[full, full, full, vec, pl.BlockSpec((D, D), lambda i: (0, 0))],
        out_specs=[full, full, vec],
        out_shape=[jax.ShapeDtypeStruct((T, D), F32), jax.ShapeDtypeStruct((T, D), F32), jax.ShapeDtypeStruct((1, 128), F32)],
        scratch_shapes=[pltpu.VMEM((tt, D), F32)],
        compiler_params=_cp("arbitrary"),
        name=name,
    )(dres, o, z, norm_g, w_out)


GDN_SPLITS = [(0, 1024), (1024, 1024), (2048, 1024), (3072, 1024), (4096, 2 * G_HEADS)]


def gdn_fwd(h, g_row, w_in, conv_w, alog, dtb, norm_g, w_out, tag):
    hn, qkv_pre, z, ba = proj_fwd(h, g_row, w_in, [(0, G_QKV), (G_QKV, 1024), (4096, 2 * G_HEADS)], f"gdn_proj_{tag}")
    qkv = gdn_prep_fwd(qkv_pre, conv_w, f"gdn_prep_{tag}")
    u, w, qd, kd, at, el, tinv = gdn_local_fwd(qkv, ba, alog, dtb, f"gdn_local_{tag}")
    o, vn, st = gdn_rec_fwd(u, w, qd, kd, at, el, f"gdn_rec_{tag}")
    hnew, cat = gdn_out_fwd(h, o, z, norm_g, w_out, f"gdn_out_{tag}")
    return hnew, (h, hn, qkv_pre, z, ba, qkv, w, qd, kd, at, el, tinv, o, vn, st, cat)


def gdn_bwd(dres, saved, g_row, w_in, conv_w, alog, dtb, norm_g, w_out, tag):
    h, hn, qkv_pre, z, ba, qkv, w, qd, kd, at, el, tinv, o, vn, st, cat = saved
    d_o, d_z, d_ng = gdn_out_bwd(dres, o, z, norm_g, w_out, f"gdn_out_bwd_{tag}")
    d_wout = mm_tn_full(cat, dres, BF16, f"gdn_wout_grad_{tag}")
    du, dw, dqd, dkd, dat, dl = gdn_rec_bwd(d_o, w, qd, kd, at, el, vn, st, f"gdn_rec_bwd_{tag}")
    dq, dk, dv, dba, dal, ddt = gdn_local_bwd(qkv, ba, alog, dtb, tinv, du, dw, dqd, dkd, dat, dl, f"gdn_local_bwd_{tag}")
    dpre, dcw = [], []
    for part, d in enumerate((dq, dk, dv)):
        dx, dwc = gdn_prep_bwd(d, qkv_pre, conv_w, part, part < 2, f"gdn_prep_bwd_{tag}_{part}")
        dpre.append(dx)
        dcw.append(dwc)
    parts = dpre + [d_z, dba]
    dres_new, d_norm, dproj = proj_bwd_data(h, g_row, w_in, parts, GDN_SPLITS, dres, f"gdn_proj_bwd_{tag}")
    d_win = cols_to_slabs(mm_tn_full(hn, dproj, F32, f"gdn_win_grad_{tag}"), mm_tn(hn, dba, f"gdn_win_grad_ba_{tag}"),
                          f"gdn_win_slabs_{tag}")
    return dres_new, dict(norm=d_norm, w_in=d_win, conv_w=jnp.concatenate(dcw, axis=1), A_log=dal, dt_bias=ddt,
                          norm_g=d_ng, w_out=d_wout)


MESH = pl.DeviceIdType.MESH
ANY = pl.BlockSpec(memory_space=pl.ANY)


def _coords():
    return lax.axis_index("x"), lax.axis_index("y"), lax.axis_index("c")


def _slot(p):
    return 4 * p[0] + 2 * p[1] + p[2]


def all_gather(shards, name):
    k_n = len(shards)

    def body(*refs):
        srcs, dsts = refs[:k_n], refs[k_n:2 * k_n]
        send_sems, recv_sems, local_sems = refs[2 * k_n:]
        x, y, c = _coords()
        me, sibling = (x, y, c), (x, y, 1 - c)
        chips = [(1 - x, y), (x, 1 - y), (1 - x, 1 - y)]

        def copy(k, s, block, to, from_src=False):
            rows = dsts[k].at[_slot(block)]
            return pltpu.make_async_remote_copy(
                src_ref=srcs[k] if from_src else rows, dst_ref=rows,
                send_sem=send_sems.at[k, s], recv_sem=recv_sems.at[k, s], device_id=to, device_id_type=MESH)

        local = [pltpu.make_async_copy(srcs[k], dsts[k].at[_slot(me)], local_sems.at[k]) for k in range(k_n)]
        for cp in local:
            cp.start()
        first = []
        for k in range(k_n):
            first.append(copy(k, 0, me, sibling, True))
            first += [copy(k, 1 + j, me, (*chip, c), True) for j, chip in enumerate(chips)]
        for cp in first:
            cp.start()
        passed = []
        for j, chip in enumerate(chips):
            for k in range(k_n):
                copy(k, 1 + j, (*chip, c), me).wait_recv()
                fw = copy(k, 4 + j, (*chip, c), sibling)
                fw.start()
                passed.append(fw)
        for k in range(k_n):
            copy(k, 0, sibling, me).wait_recv()
            for j, chip in enumerate(chips):
                copy(k, 4 + j, (*chip, 1 - c), me).wait_recv()
        for cp in first + passed:
            cp.wait_send()
        for cp in local:
            cp.wait()

    return pl.pallas_call(
        body,
        in_specs=[ANY] * k_n,
        out_specs=[ANY] * k_n,
        out_shape=[jax.ShapeDtypeStruct((N_DEV,) + s.shape, s.dtype) for s in shards],
        scratch_shapes=[pltpu.SemaphoreType.DMA((k_n, 7)), pltpu.SemaphoreType.DMA((k_n, 7)),
                        pltpu.SemaphoreType.DMA((k_n,))],
        name=name,
    )(*shards)


HBM = pl.BlockSpec(memory_space=pltpu.HBM)
SEM = pl.BlockSpec(memory_space=pltpu.SEMAPHORE)
EFFECT = pltpu.SideEffectType.DATAFLOW_SIDE_EFFECTING


def _hbm(a):
    return pltpu.with_memory_space_constraint(a, pltpu.HBM)


def _peer_list(x, y, c):
    peers = []
    for j in range(1, N_DEV):
        jx, jy, jc = (j >> 2) & 1, (j >> 1) & 1, j & 1
        peers.append((x if jx == 0 else 1 - x, y if jy == 0 else 1 - y, c if jc == 0 else 1 - c))
    return peers


def _push_views(kind, layer, src_ref, land_ref, me, peer_slot):
    if kind == "gather":
        return src_ref, land_ref.at[me], land_ref.at[peer_slot]
    if layer is None:
        return src_ref.at[peer_slot], land_ref.at[me], land_ref.at[peer_slot]
    return src_ref.at[peer_slot], land_ref.at[me, layer], land_ref.at[peer_slot, layer]


def _push_copies(groups, srcs, lands, sems):
    x, y, c = _coords()
    me = _slot((x, y, c))
    peers = _peer_list(x, y, c)
    t = 0
    for gi, group in enumerate(groups):
        for ti, (kind, layer, _, li) in enumerate(group):
            for j, peer in enumerate(peers):
                out, there, here = _push_views(kind, layer, srcs[t], lands[li], me, _slot(peer))
                k = ti * (N_DEV - 1) + j
                yield out, there, here, sems[2 * gi].at[k], sems[2 * gi + 1].at[k], peer
            t += 1


def push_start(groups, lands, name, carry=()):
    flat = [it for g in groups for it in g]
    n, n_l, n_g, n_c = len(flat), len(lands), len(groups), len(carry)
    n_in = n + n_l + n_c

    def body(*refs):
        srcs, land_refs, sems = refs[:n], refs[n:n + n_l], refs[n_in:n_in + 2 * n_g]
        for out, there, _, s_sem, r_sem, peer in _push_copies(groups, srcs, land_refs, sems):
            pltpu.make_async_remote_copy(src_ref=out, dst_ref=there, send_sem=s_sem, recv_sem=r_sem,
                                         device_id=peer, device_id_type=MESH).start()

    arrays = [it[2] for it in flat] + list(lands) + list(carry)
    sem_shapes = []
    for g in groups:
        sem_shapes += [pltpu.SemaphoreType.DMA((len(g) * (N_DEV - 1),))] * 2
    outs = pl.pallas_call(
        body,
        name=name,
        in_specs=[HBM] * n_in,
        out_specs=[SEM] * (2 * n_g) + [HBM] * n_in,
        out_shape=sem_shapes + [pltpu.HBM(a.shape, a.dtype) for a in arrays],
        input_output_aliases={i: 2 * n_g + i for i in range(n_in)},
        compiler_params=pltpu.CompilerParams(has_side_effects=EFFECT),
    )(*[_hbm(a) for a in arrays])
    sems, thru = list(outs[:2 * n_g]), list(outs[2 * n_g:])
    return sems, thru[:n], thru[n:n + n_l], thru[n + n_l:]


def push_wait(groups, lands, sems, after, name):
    flat = [it for g in groups for it in g]
    n, n_l, n_g = len(flat), len(lands), len(groups)

    def body(*refs):
        srcs, land_refs, sem_refs = refs[:n], refs[n:n + n_l], refs[n + n_l:n + n_l + 2 * n_g]
        for out, _, here, s_sem, r_sem, peer in _push_copies(groups, srcs, land_refs, sem_refs):
            cp = pltpu.make_async_remote_copy(src_ref=out, dst_ref=here, send_sem=s_sem, recv_sem=r_sem,
                                              device_id=peer, device_id_type=MESH)
            cp.wait_send()
            cp.wait_recv()

    arrays = [it[2] for it in flat] + list(lands)
    outs = pl.pallas_call(
        body,
        name=name,
        in_specs=[HBM] * (n + n_l) + [SEM] * (2 * n_g) + [ANY],
        out_specs=[HBM] * (n + n_l),
        out_shape=[pltpu.HBM(a.shape, a.dtype) for a in arrays],
        input_output_aliases={i: i for i in range(n + n_l)},
        compiler_params=pltpu.CompilerParams(has_side_effects=EFFECT),
    )(*arrays, *sems, after)
    return list(outs[:n]), list(outs[n:])


def sum_slabs(parts, name):
    n, rows, cols = parts.shape

    def body(p_ref, o_ref):
        g = p_ref[0]
        for s in range(1, n):
            g = g + p_ref[s]
        o_ref[...] = g

    return pl.pallas_call(body, out_shape=jax.ShapeDtypeStruct((rows, cols), F32), name=name)(parts)


def _row_tile(rows, cols):
    if rows * cols * 4 <= (1 << 20) or rows % 8:
        return rows
    tr = rows
    while tr % 2 == 0 and (tr // 2) % 8 == 0 and tr * cols * 4 > (1 << 20):
        tr //= 2
    return tr


def adamw(parts, w, m, v, name):
    p_n = parts.shape[0]
    rows, cols = w.shape
    tr = _row_tile(rows, cols)

    def body(p_ref, w_ref, m_ref, v_ref, g_ref, d_ref, nm_ref, nv_ref):
        g = p_ref[0].astype(F32)
        for s in range(1, p_n):
            g = g + p_ref[s].astype(F32)
        m_new = ADAM_B1 * m_ref[...] + (1.0 - ADAM_B1) * g
        v_new = ADAM_B2 * v_ref[...] + (1.0 - ADAM_B2) * (g * g)
        m_hat = m_new / (1.0 - ADAM_B1 ** ADAM_STEP)
        v_hat = v_new / (1.0 - ADAM_B2 ** ADAM_STEP)
        g_ref[...] = g
        d_ref[...] = -ADAM_LR * (m_hat / (jnp.sqrt(v_hat) + ADAM_EPS) + ADAM_WD * w_ref[...])
        nm_ref[...] = m_new
        nv_ref[...] = v_new

    blk = pl.BlockSpec((tr, cols), lambda i: (i, 0))
    return pl.pallas_call(
        body,
        grid=(rows // tr,),
        in_specs=[pl.BlockSpec((p_n, tr, cols), lambda i: (0, i, 0)), blk, blk, blk],
        out_specs=[blk] * 4,
        out_shape=[jax.ShapeDtypeStruct((rows, cols), F32)] * 4,
        compiler_params=_cp("parallel"),
        name=name,
    )(parts, w, m, v)


def _adamw_nd(parts, w, m, v, name):
    shp = w.shape
    cols = shp[-1]
    rows = math.prod(shp[:-1])
    outs = adamw(parts.reshape(parts.shape[0], rows, cols), w.reshape(rows, cols), m.reshape(rows, cols),
                 v.reshape(rows, cols), name)
    return [o.reshape(shp) for o in outs]


REPL = ["ffn1_norm", "mix_norm", "ffn2_norm", "hyb_dw_b", "hyb_ln_g", "hyb_ln_b", "gdn_A_log", "gdn_dt_bias",
        "gdn_norm_g", "final_norm"]
WEIGHTS = ["ffn1_norm", "ffn1_w_in", "ffn1_w_out", "mix_norm", "ffn2_norm", "ffn2_w_in", "ffn2_w_out", "hyb_w_in",
           "hyb_dw_w", "hyb_dw_b", "hyb_ln_g", "hyb_ln_b", "hyb_w_out", "gdn_w_in", "gdn_conv_w", "gdn_A_log",
           "gdn_dt_bias", "gdn_norm_g", "gdn_w_out", "final_norm"]


def _pack(arrs, rows):
    flat = jnp.concatenate([a.reshape(-1) for a in arrs])
    return jnp.pad(flat, (0, rows * 128 - flat.shape[0])).reshape(rows, 128)


def _slabs_to_cols(a):
    return jnp.moveaxis(a, 0, -2).reshape(a.shape[1:-1] + (N_DEV * a.shape[-1],))


def kernel(x, positions, ffn1_norm, ffn1_w_in, ffn1_w_out, mix_norm, ffn2_norm, ffn2_w_in, ffn2_w_out, hyb_w_in, hyb_dw_w, hyb_dw_b, hyb_ln_g, hyb_ln_b, hyb_w_out, gdn_w_in, gdn_conv_w, gdn_A_log, gdn_dt_bias, gdn_norm_g, gdn_w_out, final_norm, loss_target, m_ffn1_norm, m_ffn1_w_in, m_ffn1_w_out, m_mix_norm, m_ffn2_norm, m_ffn2_w_in, m_ffn2_w_out, m_hyb_w_in, m_hyb_dw_w, m_hyb_dw_b, m_hyb_ln_g, m_hyb_ln_b, m_hyb_w_out, m_gdn_w_in, m_gdn_conv_w, m_gdn_A_log, m_gdn_dt_bias, m_gdn_norm_g, m_gdn_w_out, m_final_norm, v_ffn1_norm, v_ffn1_w_in, v_ffn1_w_out, v_mix_norm, v_ffn2_norm, v_ffn2_w_in, v_ffn2_w_out, v_hyb_w_in, v_hyb_dw_w, v_hyb_dw_b, v_hyb_ln_g, v_hyb_ln_b, v_hyb_w_out, v_gdn_w_in, v_gdn_conv_w, v_gdn_A_log, v_gdn_dt_bias, v_gdn_norm_g, v_gdn_w_out, v_final_norm):
    w = dict(ffn1_norm=ffn1_norm, ffn1_w_in=ffn1_w_in, ffn1_w_out=ffn1_w_out, mix_norm=mix_norm, ffn2_norm=ffn2_norm,
             ffn2_w_in=ffn2_w_in, ffn2_w_out=ffn2_w_out, hyb_w_in=hyb_w_in, hyb_dw_w=hyb_dw_w, hyb_dw_b=hyb_dw_b,
             hyb_ln_g=hyb_ln_g, hyb_ln_b=hyb_ln_b, hyb_w_out=hyb_w_out, gdn_w_in=gdn_w_in, gdn_conv_w=gdn_conv_w,
             gdn_A_log=gdn_A_log, gdn_dt_bias=gdn_dt_bias, gdn_norm_g=gdn_norm_g, gdn_w_out=gdn_w_out,
             final_norm=final_norm)
    mom = dict(ffn1_norm=m_ffn1_norm, ffn1_w_in=m_ffn1_w_in, ffn1_w_out=m_ffn1_w_out, mix_norm=m_mix_norm,
               ffn2_norm=m_ffn2_norm, ffn2_w_in=m_ffn2_w_in, ffn2_w_out=m_ffn2_w_out, hyb_w_in=m_hyb_w_in,
               hyb_dw_w=m_hyb_dw_w, hyb_dw_b=m_hyb_dw_b, hyb_ln_g=m_hyb_ln_g, hyb_ln_b=m_hyb_ln_b,
               hyb_w_out=m_hyb_w_out, gdn_w_in=m_gdn_w_in, gdn_conv_w=m_gdn_conv_w, gdn_A_log=m_gdn_A_log,
               gdn_dt_bias=m_gdn_dt_bias, gdn_norm_g=m_gdn_norm_g, gdn_w_out=m_gdn_w_out, final_norm=m_final_norm)
    var = dict(ffn1_norm=v_ffn1_norm, ffn1_w_in=v_ffn1_w_in, ffn1_w_out=v_ffn1_w_out, mix_norm=v_mix_norm,
               ffn2_norm=v_ffn2_norm, ffn2_w_in=v_ffn2_w_in, ffn2_w_out=v_ffn2_w_out, hyb_w_in=v_hyb_w_in,
               hyb_dw_w=v_hyb_dw_w, hyb_dw_b=v_hyb_dw_b, hyb_ln_g=v_hyb_ln_g, hyb_ln_b=v_hyb_ln_b,
               hyb_w_out=v_hyb_w_out, gdn_w_in=v_gdn_w_in, gdn_conv_w=v_gdn_conv_w, gdn_A_log=v_gdn_A_log,
               gdn_dt_bias=v_gdn_dt_bias, gdn_norm_g=v_gdn_norm_g, gdn_w_out=v_gdn_w_out, final_norm=v_final_norm)
    xi, yi, ci = _coords()
    me = 4 * xi + 2 * yi + ci

    big = ["ffn1_w_in", "ffn1_w_out", "ffn2_w_in", "ffn2_w_out", "hyb_w_in", "hyb_w_out", "gdn_w_in", "gdn_w_out"]
    ag_groups, ag_lands = [], []

    def add_group(shards):
        group = []
        for s in shards:
            land = lax.dynamic_update_slice(lax.empty((N_DEV,) + s.shape, s.dtype), s[None], (me,) + (0,) * s.ndim)
            group.append(("gather", None, s, len(ag_lands)))
            ag_lands.append(land)
        ag_groups.append(group)

    for l in range(DEPTH):
        i = l // 2
        add_group([ffn1_w_in[l].astype(BF16), ffn1_w_out[l].astype(BF16)])
        if l % 2 == 0:
            add_group([hyb_w_in[i].astype(BF16), hyb_w_out[i].astype(BF16), hyb_dw_w[i]])
        else:
            add_group([gdn_w_in[i].astype(BF16), gdn_w_out[i].astype(BF16), gdn_conv_w[i]])
        add_group([ffn2_w_in[l].astype(BF16), ffn2_w_out[l].astype(BF16)])
    ag_sems, ag_srcs, ag_lands, _ = push_start(ag_groups, ag_lands, "weights_gather_start")

    def fetch(gi, after):
        group = ag_groups[gi]
        base = sum(len(g) for g in ag_groups[:gi])
        items = [(kind, layer, ag_srcs[base + t], t) for t, (kind, layer, _, _) in enumerate(group)]
        lands = [ag_lands[li] for _, _, _, li in group]
        return push_wait([items], lands, ag_sems[2 * gi:2 * gi + 2], after, f"weights_gather_wait_{gi}")[1]

    row = lambda a: a.reshape(1, -1)

    rope = make_rope(positions)
    h = x[0]
    saved = []
    for l in range(DEPTH):
        i = l // 2
        rec = {"h1": h}
        wi, wo = fetch(3 * l, h)
        rec["w1"] = (wi.reshape(2, FFN_TILES, D, FFN_SHARD), wo)
        h, rec["hn1"], rec["a1"], rec["b1"] = ffn_fwd(h, row(ffn1_norm[l]), *rec["w1"], l, "1")
        mi, mo, mc = fetch(3 * l + 1, h)
        if l % 2 == 0:
            rec["wm"] = (_slabs_to_cols(mi), jnp.pad(_slabs_to_cols(mc), ((0, 1), (0, 0))), mo.reshape(D, D))
            w_in_f, dw_f, w_out_f = rec["wm"]
            h, rec["mix"] = hybrid_fwd(h, row(mix_norm[l]), w_in_f, dw_f, row(hyb_dw_b[i]), row(hyb_ln_g[i]),
                                       row(hyb_ln_b[i]), w_out_f, rope, str(i))
        else:
            rec["wm"] = (_slabs_to_cols(mi), _slabs_to_cols(mc), mo.reshape(D, D))
            w_in_f, cw_f, w_out_f = rec["wm"]
            h, rec["mix"] = gdn_fwd(h, row(mix_norm[l]), w_in_f, cw_f, row(gdn_A_log[i]), row(gdn_dt_bias[i]),
                                    row(gdn_norm_g[i]), w_out_f, str(i))
        rec["h2"] = h
        wi, wo = fetch(3 * l + 2, h)
        rec["w2"] = (wi.reshape(2, FFN_TILES, D, FFN_SHARD), wo)
        h, rec["hn2"], rec["a2"], rec["b2"] = ffn_fwd(h, row(ffn2_norm[l]), *rec["w2"], l, "2")
        saved.append(rec)
    dres, d_final, loss_acc = final_loss(h, row(final_norm), loss_target[0])
    loss = lax.psum(loss_acc[0, 0], ("x", "y", "c"))

    ge_land = {n: lax.empty((N_DEV,) + w[n].shape, BF16) for n in big}
    ge_pending = []

    def send(named, layer, tag, carry):
        lands = [ge_land[n] for n, _ in named]
        group = [("scatter", layer, s, t) for t, (_, s) in enumerate(named)]
        sems, srcs, lands_out, carried = push_start([group], lands, f"grad_send_{tag}", carry=[carry])
        for (n, _), land in zip(named, lands_out):
            ge_land[n] = land
        ge_pending.append(([(n, layer, s) for (n, _), s in zip(named, srcs)], sems))
        return carried[0]

    gsmall = {n: [None] * (DEPTH if n in ("ffn1_norm", "mix_norm", "ffn2_norm") else 2) for n in REPL[:-1]}
    gsmall["hyb_dw_w"] = [None, None]
    gsmall["gdn_conv_w"] = [None, None]
    for l in reversed(range(DEPTH)):
        i = l // 2
        rec = saved[l]
        dhn, dwin, dwout = ffn_bwd(rec["hn2"], rec["a2"], rec["b2"], dres, *rec["w2"], l, "2")
        dhn = send([("ffn2_w_in", dwin.reshape(N_DEV, D, FFN_SHARD)),
                    ("ffn2_w_out", dwout.reshape(N_DEV, FFN_SHARD // 2, D))], l, f"ffn2_{l}", dhn)
        dres, dg = norm_bwd(rec["h2"], row(ffn2_norm[l]), dhn, dres, f"ffn2_norm_bwd_{l}")
        gsmall["ffn2_norm"][l] = dg
        if l % 2 == 0:
            w_in_f, dw_f, w_out_f = rec["wm"]
            dres, gr = hybrid_bwd(dres, rec["mix"], row(mix_norm[l]), w_in_f, dw_f, row(hyb_ln_g[i]),
                                  row(hyb_ln_b[i]), w_out_f, rope, str(i))
            dres = send([("hyb_w_in", gr["w_in"]), ("hyb_w_out", gr["w_out"].reshape(N_DEV, D // N_DEV, D))],
                        i, f"hyb_{i}", dres)
            for n in ("dw_w", "dw_b", "ln_g", "ln_b"):
                gsmall["hyb_" + n][i] = gr[n]
        else:
            w_in_f, cw_f, w_out_f = rec["wm"]
            dres, gr = gdn_bwd(dres, rec["mix"], row(mix_norm[l]), w_in_f, cw_f, row(gdn_A_log[i]),
                               row(gdn_dt_bias[i]), row(gdn_norm_g[i]), w_out_f, str(i))
            dres = send([("gdn_w_in", gr["w_in"]), ("gdn_w_out", gr["w_out"].reshape(N_DEV, D // N_DEV, D))],
                        i, f"gdn_{i}", dres)
            for n in ("conv_w", "A_log", "dt_bias", "norm_g"):
                gsmall["gdn_" + n][i] = gr[n]
        gsmall["mix_norm"][l] = gr["norm"]
        dhn, dwin, dwout = ffn_bwd(rec["hn1"], rec["a1"], rec["b1"], dres, *rec["w1"], l, "1")
        dhn = send([("ffn1_w_in", dwin.reshape(N_DEV, D, FFN_SHARD)),
                    ("ffn1_w_out", dwout.reshape(N_DEV, FFN_SHARD // 2, D))], l, f"ffn1_{l}", dhn)
        dres, dg = norm_bwd(rec["h1"], row(ffn1_norm[l]), dhn, dres, f"ffn1_norm_bwd_{l}")
        gsmall["ffn1_norm"][l] = dg
    grad_x = dres[None]

    n_repl_rows = 136
    small_rows = 576
    repl_flat = jnp.concatenate([jnp.concatenate([a.reshape(-1) for a in gsmall[n]]) for n in REPL[:-1]]
                                + [d_final.reshape(-1)])
    repl_pack = jnp.pad(repl_flat, (0, n_repl_rows * 128 - repl_flat.shape[0]))
    small_pack = jnp.concatenate([repl_pack] + [a.reshape(-1) for a in gsmall["hyb_dw_w"]]
                                 + [a.reshape(-1) for a in gsmall["gdn_conv_w"]]).reshape(small_rows, 128)
    small_all, = all_gather([small_pack], "small_grads_all_gather")
    g_small = sum_slabs(small_all, "small_grads_sum")

    groups = [[("scatter", layer, s, big.index(n)) for n, layer, s in named] for named, _ in ge_pending]
    sems = [s for _, pair in ge_pending for s in pair]
    srcs_out, lands_out = push_wait(groups, [ge_land[n] for n in big], sems, dres, "grad_wait")
    own = {n: {} for n in big}
    flat_named = [it for named, _ in ge_pending for it in named]
    for (n, layer, _), s in zip(flat_named, srcs_out):
        own[n][layer] = lax.dynamic_index_in_dim(s, me, 0, keepdims=False)
    recv = {}
    for n, land in zip(big, lands_out):
        mine = jnp.stack([own[n][k] for k in range(len(own[n]))])
        recv[n] = lax.dynamic_update_slice(land, mine[None], (me,) + (0,) * mine.ndim)

    out = {}
    for n in big:
        out[n] = _adamw_nd(recv[n], w[n], mom[n], var[n], f"adamw_{n}")
    pk = lambda d: _pack([d[n] for n in REPL], n_repl_rows)
    res = adamw(g_small[:n_repl_rows][None], pk(w), pk(mom), pk(var), "adamw_replicated")
    off = 0
    for n in REPL:
        sz = w[n].size
        out[n] = [r.reshape(-1)[off:off + sz].reshape(w[n].shape) for r in res]
        off += sz
    g_dw = g_small[n_repl_rows:n_repl_rows + 248].reshape(2, CONV_K, CONV_C)
    g_dw = lax.dynamic_slice_in_dim(g_dw, me * (CONV_C // N_DEV), CONV_C // N_DEV, axis=2)
    out["hyb_dw_w"] = _adamw_nd(g_dw[None], w["hyb_dw_w"], mom["hyb_dw_w"], var["hyb_dw_w"], "adamw_hyb_dw_w")
    g_cw = g_small[n_repl_rows + 248:].reshape(2, G_CONV, G_QKV)
    g_cw = lax.dynamic_slice_in_dim(g_cw, me * (G_QKV // N_DEV), G_QKV // N_DEV, axis=2)
    out["gdn_conv_w"] = _adamw_nd(g_cw[None], w["gdn_conv_w"], mom["gdn_conv_w"], var["gdn_conv_w"], "adamw_gdn_conv_w")

    return (loss, grad_x, *[out[n][0] for n in WEIGHTS], *[out[n][1] for n in WEIGHTS],
            *[out[n][2] for n in WEIGHTS], *[out[n][3] for n in WEIGHTS])
```

```python
import functools
import math

import jax
import jax.numpy as jnp
import numpy as np
from jax import lax
from jax.experimental import pallas as pl
from jax.experimental.pallas import tpu as pltpu

F32 = jnp.float32
BF16 = jnp.bfloat16

N_DEV = 8
T = 4096
D = 1024
DEPTH = 4
FFN = 2816
FFN_SHARD = 2 * FFN // N_DEV
FFN_TILES = FFN // FFN_SHARD
EPS = 1e-6

A_HEADS = 8
A_HD = 64
A_W = 512
CONV_C = 512
CONV_K = 31
HYB_IN = 2560
ROPE_THETA = 500000.0
ROT = 16
DILATIONS = (1, 4, 16)
BLK = 128
KPAD = 2048

G_HEADS = 8
G_DK = 128
G_QKV = 3072
G_IN = 4112
G_CONV = 4
CH = 64
GRP = 512
CPG = GRP // CH
N_GRP = T // GRP

ADAM_LR = 0.001
ADAM_B1 = 0.9
ADAM_B2 = 0.999
ADAM_EPS = 1e-08
ADAM_WD = 0.01
ADAM_STEP = 10

VMEM_LIMIT = 56 * 1024 * 1024

HI = lax.Precision.HIGHEST


def _cp(*sem):
    return pltpu.CompilerParams(dimension_semantics=sem, vmem_limit_bytes=VMEM_LIMIT)


def _dot(a, b):
    return jnp.dot(a, b, preferred_element_type=F32)


def _dot_nt(a, b):
    return lax.dot_general(a, b, (((1,), (1,)), ((), ())), preferred_element_type=F32)


def _dot_tn(a, b):
    return lax.dot_general(a, b, (((0,), (0,)), ((), ())), preferred_element_type=F32)


def _sigmoid(x):
    return 1.0 / (1.0 + jnp.exp(-x))


def _dsilu(x, sig):
    return sig * (1.0 + x * (1.0 - sig))


def _rms(x, g):
    rstd = lax.rsqrt(jnp.mean(x * x, axis=-1, keepdims=True) + EPS)
    return x * rstd * g


FFN_TT = 512


def ffn_fwd(h, g_row, w_in, w_out, layer, tag=""):
    def body(h_ref, g_ref, win_ref, wout_ref, hnew_ref, hn_ref, a_ref, b_ref):
        x = h_ref[...]
        hn = _rms(x, g_ref[...]).astype(BF16)
        hn_ref[...] = hn
        acc = None
        for j in range(FFN_TILES):
            a = _dot(hn, win_ref[0, j])
            b = _dot(hn, win_ref[1, j])
            act = a * _sigmoid(a) * b
            a_ref[j] = a.astype(BF16)
            b_ref[j] = b.astype(BF16)
            part = _dot(act.astype(BF16), wout_ref[2 * j:2 * j + 2].reshape(FFN_SHARD, D))
            acc = part if acc is None else acc + part
        hnew_ref[...] = x + 0.5 * acc

    tt = FFN_TT
    resident = pl.Buffered(1)
    return pl.pallas_call(
        body,
        grid=(T // tt,),
        in_specs=[
            pl.BlockSpec((tt, D), lambda i: (i, 0)),
            pl.BlockSpec((1, D), lambda i: (0, 0)),
            pl.BlockSpec((2, FFN_TILES, D, FFN_SHARD), lambda i: (0, 0, 0, 0), pipeline_mode=resident),
            pl.BlockSpec((N_DEV, FFN_SHARD // 2, D), lambda i: (0, 0, 0), pipeline_mode=resident),
        ],
        out_specs=[
            pl.BlockSpec((tt, D), lambda i: (i, 0)),
            pl.BlockSpec((tt, D), lambda i: (i, 0)),
            pl.BlockSpec((FFN_TILES, tt, FFN_SHARD), lambda i: (0, i, 0)),
            pl.BlockSpec((FFN_TILES, tt, FFN_SHARD), lambda i: (0, i, 0)),
        ],
        out_shape=[
            jax.ShapeDtypeStruct((T, D), F32),
            jax.ShapeDtypeStruct((T, D), BF16),
            jax.ShapeDtypeStruct((FFN_TILES, T, FFN_SHARD), BF16),
            jax.ShapeDtypeStruct((FFN_TILES, T, FFN_SHARD), BF16),
        ],
        compiler_params=_cp("parallel"),
        name=f"ffn{tag}_fwd_{layer}",
    )(h, g_row, w_in, w_out)


def ffn_bwd(hn, a, b, dres, w_in, w_out, layer, tag=""):
    tt = FFN_TT
    nt = T // tt

    def body(hn_ref, a_ref, b_ref, dres_ref, win_ref, wout_ref, dhn_ref, dwin_ref, dwout_ref, gin_ref, gout_ref,
             do_s, act_s, da_s, db_s):
        i = pl.program_id(1)
        wo = wout_ref[...].reshape(FFN_SHARD, D)
        half = tt // 2
        for r0 in (0, half):
            rows = slice(r0, r0 + half)
            do_h = (0.5 * dres_ref[rows, :]).astype(BF16)
            do_s[rows, :] = do_h
            dact = _dot_nt(do_h, wo)
            a = a_ref[rows, :].astype(F32)
            b = b_ref[rows, :].astype(F32)
            sig = _sigmoid(a)
            s = a * sig
            da_h = (dact * b * _dsilu(a, sig)).astype(BF16)
            db_h = (dact * s).astype(BF16)
            act_s[rows, :] = (s * b).astype(BF16)
            da_s[rows, :] = da_h
            db_s[rows, :] = db_h
            dhn_ref[rows, :] = (_dot_nt(da_h, win_ref[0]) + _dot_nt(db_h, win_ref[1])).astype(BF16)
        do, act, da, db = do_s[...], act_s[...], da_s[...], db_s[...]
        hn = hn_ref[...]
        gwo = _dot_tn(act, do)
        gwg = _dot_tn(hn, da)
        gwu = _dot_tn(hn, db)

        @pl.when(i == 0)
        def _():
            gout_ref[...] = gwo
            gin_ref[0] = gwg
            gin_ref[1] = gwu

        @pl.when(i > 0)
        def _():
            gout_ref[...] += gwo
            gin_ref[0] += gwg
            gin_ref[1] += gwu

        @pl.when(i == nt - 1)
        def _():
            dwin_ref[...] = gin_ref[...].astype(BF16)
            dwout_ref[...] = gout_ref[...].astype(BF16)

    return pl.pallas_call(
        body,
        grid=(FFN_TILES, nt),
        in_specs=[
            pl.BlockSpec((tt, D), lambda j, i: (i, 0)),
            pl.BlockSpec((None, tt, FFN_SHARD), lambda j, i: (j, i, 0)),
            pl.BlockSpec((None, tt, FFN_SHARD), lambda j, i: (j, i, 0)),
            pl.BlockSpec((tt, D), lambda j, i: (i, 0)),
            pl.BlockSpec((2, None, D, FFN_SHARD), lambda j, i: (0, j, 0, 0)),
            pl.BlockSpec((2, FFN_SHARD // 2, D), lambda j, i: (j, 0, 0)),
        ],
        out_specs=[
            pl.BlockSpec((None, tt, D), lambda j, i: (j, i, 0)),
            pl.BlockSpec((2, None, D, FFN_SHARD), lambda j, i: (0, j, 0, 0)),
            pl.BlockSpec((None, FFN_SHARD, D), lambda j, i: (j, 0, 0)),
        ],
        out_shape=[
            jax.ShapeDtypeStruct((FFN_TILES, T, D), BF16),
            jax.ShapeDtypeStruct((2, FFN_TILES, D, FFN_SHARD), BF16),
            jax.ShapeDtypeStruct((FFN_TILES, FFN_SHARD, D), BF16),
        ],
        scratch_shapes=[pltpu.VMEM((2, D, FFN_SHARD), F32), pltpu.VMEM((FFN_SHARD, D), F32),
                        pltpu.VMEM((tt, D), BF16),
                        pltpu.VMEM((tt, FFN_SHARD), BF16), pltpu.VMEM((tt, FFN_SHARD), BF16),
                        pltpu.VMEM((tt, FFN_SHARD), BF16)],
        compiler_params=_cp("parallel", "arbitrary"),
        name=f"ffn{tag}_bwd_{layer}",
    )(hn, a, b, dres, w_in, w_out)


def _rms_bwd(x, g, dy):
    rstd = lax.rsqrt(jnp.mean(x * x, axis=-1, keepdims=True) + EPS)
    xh = x * rstd
    u = dy * g
    dx = rstd * (u - xh * jnp.mean(u * xh, axis=-1, keepdims=True))
    return dx, jnp.sum(dy * xh, axis=0, keepdims=True)


def norm_bwd(x, g_row, dy_parts, dres, name):
    p = dy_parts.shape[0]
    tt = 512

    def body(x_ref, g_ref, dy_ref, dres_ref, out_ref, dg_ref):
        i = pl.program_id(0)
        dy = dy_ref[0].astype(F32)
        for q in range(1, p):
            dy = dy + dy_ref[q].astype(F32)
        dx, dg = _rms_bwd(x_ref[...], g_ref[...], dy)
        out_ref[...] = dres_ref[...] + dx

        @pl.when(i == 0)
        def _():
            dg_ref[...] = dg

        @pl.when(i > 0)
        def _():
            dg_ref[...] += dg

    return pl.pallas_call(
        body,
        grid=(T // tt,),
        in_specs=[
            pl.BlockSpec((tt, D), lambda i: (i, 0)),
            pl.BlockSpec((1, D), lambda i: (0, 0)),
            pl.BlockSpec((p, tt, D), lambda i: (0, i, 0)),
            pl.BlockSpec((tt, D), lambda i: (i, 0)),
        ],
        out_specs=[pl.BlockSpec((tt, D), lambda i: (i, 0)), pl.BlockSpec((1, D), lambda i: (0, 0))],
        out_shape=[jax.ShapeDtypeStruct((T, D), F32), jax.ShapeDtypeStruct((1, D), F32)],
        compiler_params=_cp("arbitrary"),
        name=name,
    )(x, g_row, dy_parts, dres)


def final_loss(h, g_row, target):
    tt = 512

    def body(h_ref, g_ref, t_ref, dres_ref, dg_ref, loss_ref):
        i = pl.program_id(0)
        x = h_ref[...]
        g = g_ref[...]
        err = _rms(x, g) - t_ref[...]
        part = 0.5 * jnp.sum(jnp.mean(err * err, axis=-1, keepdims=True), axis=0, keepdims=True)
        dx, dg = _rms_bwd(x, g, err * (1.0 / D))
        dres_ref[...] = dx
        part = jnp.broadcast_to(part, loss_ref.shape)

        @pl.when(i == 0)
        def _():
            dg_ref[...] = dg
            loss_ref[...] = part

        @pl.when(i > 0)
        def _():
            dg_ref[...] += dg
            loss_ref[...] += part

    return pl.pallas_call(
        body,
        grid=(T // tt,),
        in_specs=[
            pl.BlockSpec((tt, D), lambda i: (i, 0)),
            pl.BlockSpec((1, D), lambda i: (0, 0)),
            pl.BlockSpec((tt, D), lambda i: (i, 0)),
        ],
        out_specs=[
            pl.BlockSpec((tt, D), lambda i: (i, 0)),
            pl.BlockSpec((1, D), lambda i: (0, 0)),
            pl.BlockSpec((8, 128), lambda i: (0, 0)),
        ],
        out_shape=[
            jax.ShapeDtypeStruct((T, D), F32),
            jax.ShapeDtypeStruct((1, D), F32),
            jax.ShapeDtypeStruct((8, 128), F32),
        ],
        compiler_params=_cp("arbitrary"),
        name="final_loss",
    )(h, g_row, target)


PROJ_TT = 256


def rope_tables(pos_col, invf_row):
    tt = 512

    def body(p_ref, f_ref, c_ref, sm_ref, sp_ref):
        ang = p_ref[...].astype(F32) * f_ref[...]
        lane = lax.broadcasted_iota(jnp.int32, ang.shape, 1) % A_HD
        cs = jnp.cos(ang)
        sn = jnp.sin(ang)
        c_ref[...] = jnp.where(lane < ROT, cs, 1.0)
        sm_ref[...] = jnp.where(lane < ROT // 2, -sn, 0.0)
        sp_ref[...] = jnp.where((lane >= ROT // 2) & (lane < ROT), sn, 0.0)

    spec = pl.BlockSpec((tt, 128), lambda i: (i, 0))
    return pl.pallas_call(
        body,
        grid=(T // tt,),
        in_specs=[pl.BlockSpec((tt, 1), lambda i: (i, 0)), pl.BlockSpec((1, 128), lambda i: (0, 0))],
        out_specs=[spec, spec, spec],
        out_shape=[jax.ShapeDtypeStruct((T, 128), F32)] * 3,
        compiler_params=_cp("parallel"),
        name="rope_tables",
    )(pos_col, invf_row)


def make_rope(positions):
    inv_freq = jnp.power(jnp.float32(ROPE_THETA), -jnp.arange(0, ROT, 2, dtype=F32) / ROT)
    per_head = jnp.concatenate([inv_freq, inv_freq, jnp.zeros((A_HD - ROT,), F32)])
    invf_row = jnp.tile(per_head, 2)[None, :]
    return tuple(rope_tables(positions.reshape(T, 1), invf_row))


def _rope(x, c, sm, sp):
    return x * c + pltpu.roll(x, 128 - ROT // 2, 1) * sm + pltpu.roll(x, ROT // 2, 1) * sp


def _rope_t(dy, c, sm, sp):
    return dy * c + pltpu.roll(dy * sm, ROT // 2, 1) + pltpu.roll(dy * sp, 128 - ROT // 2, 1)


def proj_fwd(h, g_row, w, splits, name, rope=None):
    tt = PROJ_TT
    n = w.shape[1]
    n_rope = 0 if rope is None else 3

    def body(h_ref, g_ref, w_ref, *rest):
        tabs = rest[:n_rope]
        hn_ref = rest[n_rope]
        outs = rest[n_rope + 1:]
        hn = _rms(h_ref[...], g_ref[...]).astype(BF16)
        hn_ref[...] = hn
        for k, ((st, wd), o_ref) in enumerate(zip(splits, outs)):
            if rope is not None and k == 0:
                c, sm, sp = (t[...] for t in tabs)
                for gi in range(wd // 128):
                    r = _dot(hn, w_ref[:, st + 128 * gi:st + 128 * (gi + 1)])
                    if gi < 2 * A_W // 128:
                        r = _rope(r, c, sm, sp)
                    o_ref[:, 128 * gi:128 * (gi + 1)] = r
            else:
                o_ref[...] = _dot(hn, w_ref[:, st:st + wd])

    tab_specs = [pl.BlockSpec((tt, 128), lambda i: (i, 0))] * n_rope
    return pl.pallas_call(
        body,
        grid=(T // tt,),
        in_specs=[
            pl.BlockSpec((tt, D), lambda i: (i, 0)),
            pl.BlockSpec((1, D), lambda i: (0, 0)),
            pl.BlockSpec((D, n), lambda i: (0, 0)),
        ] + tab_specs,
        out_specs=[pl.BlockSpec((tt, D), lambda i: (i, 0))]
        + [pl.BlockSpec((tt, wd), lambda i: (i, 0)) for _, wd in splits],
        out_shape=[jax.ShapeDtypeStruct((T, D), BF16)]
        + [jax.ShapeDtypeStruct((T, wd), F32) for _, wd in splits],
        compiler_params=_cp("parallel"),
        name=name,
    )(h, g_row, w, *(rope or ()))


def proj_bwd_data(x, g_row, w, dparts, splits, dres, name, rope=None, n_rot=0):
    tt = PROJ_TT
    n = w.shape[1]
    n_rope = 0 if rope is None else 3
    k_parts = len(dparts)
    n_main = sum(wd for _, wd in splits if wd % 128 == 0)

    def body(x_ref, g_ref, w_ref, dres_ref, *rest):
        d_refs = rest[:k_parts]
        tabs = rest[k_parts:k_parts + n_rope]
        out_ref, dg_ref, dproj_ref = rest[k_parts + n_rope:k_parts + n_rope + 3]
        i = pl.program_id(0)
        dhn = jnp.zeros((tt, D), F32)
        for k, ((st, wd), d_ref) in enumerate(zip(splits, d_refs)):
            if k < n_rot:
                c, sm, sp = (t[...] for t in tabs)
                for gi in range(wd // 128):
                    cols = slice(st + 128 * gi, st + 128 * (gi + 1))
                    d = _rope_t(d_ref[:, 128 * gi:128 * (gi + 1)], c, sm, sp).astype(BF16)
                    dproj_ref[:, cols] = d
                    dhn = dhn + _dot_nt(d, w_ref[:, cols])
            else:
                d = d_ref[...].astype(BF16)
                if wd % 128 == 0:
                    dproj_ref[:, st:st + wd] = d
                dhn = dhn + _dot_nt(d, w_ref[:, st:st + wd])
        dx, dg = _rms_bwd(x_ref[...], g_ref[...], dhn)
        out_ref[...] = dres_ref[...] + dx

        @pl.when(i == 0)
        def _():
            dg_ref[...] = dg

        @pl.when(i > 0)
        def _():
            dg_ref[...] += dg

    tab_specs = [pl.BlockSpec((tt, 128), lambda i: (i, 0))] * n_rope
    out_specs = [pl.BlockSpec((tt, D), lambda i: (i, 0)), pl.BlockSpec((1, D), lambda i: (0, 0))]
    out_shape = [jax.ShapeDtypeStruct((T, D), F32), jax.ShapeDtypeStruct((1, D), F32)]
    out_specs.append(pl.BlockSpec((tt, n_main), lambda i: (i, 0)))
    out_shape.append(jax.ShapeDtypeStruct((T, n_main), BF16))
    return pl.pallas_call(
        body,
        grid=(T // tt,),
        in_specs=[
            pl.BlockSpec((tt, D), lambda i: (i, 0)),
            pl.BlockSpec((1, D), lambda i: (0, 0)),
            pl.BlockSpec((D, n), lambda i: (0, 0)),
            pl.BlockSpec((tt, D), lambda i: (i, 0)),
        ] + [pl.BlockSpec((tt, wd), lambda i: (i, 0)) for _, wd in splits] + tab_specs,
        out_specs=out_specs,
        out_shape=out_shape,
        compiler_params=_cp("arbitrary"),
        name=name,
    )(x, g_row, w, dres, *dparts, *(rope or ()))


def mm_tn_full(x, d, out_dtype, name):
    k = x.shape[1]
    n = d.shape[1]
    wn = 512

    def body(x_ref, d_ref, o_ref):
        o_ref[...] = _dot_tn(x_ref[...], d_ref[...].astype(BF16)).astype(out_dtype)

    return pl.pallas_call(
        body,
        grid=(n // wn,),
        in_specs=[pl.BlockSpec((T, k), lambda j: (0, 0), pipeline_mode=pl.Buffered(1)),
                  pl.BlockSpec((T, wn), lambda j: (0, j))],
        out_specs=pl.BlockSpec((k, wn), lambda j: (0, j)),
        out_shape=jax.ShapeDtypeStruct((k, n), out_dtype),
        compiler_params=_cp("parallel"),
        name=name,
    )(x, d)


def cols_to_slabs(main, tail, name):
    nm = main.shape[1]
    n = nm + (0 if tail is None else tail.shape[1])
    w = n // N_DEV
    tr = 256

    def body(*refs):
        m_ref, o_ref = refs[0], refs[-1]
        for s in range(N_DEV):
            a, b = w * s, w * (s + 1)
            if b <= nm:
                o_ref[s] = m_ref[:, a:b].astype(BF16)
            else:
                o_ref[s, :, 0:nm - a] = m_ref[:, a:nm].astype(BF16)
                o_ref[s, :, nm - a:w] = refs[1][:, 0:b - nm].astype(BF16)

    arrays = [main] + ([] if tail is None else [tail])
    return pl.pallas_call(
        body,
        grid=(D // tr,),
        in_specs=[pl.BlockSpec((tr, a.shape[1]), lambda i: (i, 0)) for a in arrays],
        out_specs=pl.BlockSpec((N_DEV, tr, w), lambda i: (0, i, 0)),
        out_shape=jax.ShapeDtypeStruct((N_DEV, D, w), BF16),
        compiler_params=_cp("parallel"),
        name=name,
    )(*arrays)


def mm_tn(x, d, name):
    k = x.shape[1]
    n = d.shape[1]
    wn = n if n <= 512 else 512
    tt = 512

    def body(x_ref, d_ref, o_ref):
        i = pl.program_id(1)
        r = _dot_tn(x_ref[...], d_ref[...].astype(BF16))

        @pl.when(i == 0)
        def _():
            o_ref[...] = r

        @pl.when(i > 0)
        def _():
            o_ref[...] += r

    return pl.pallas_call(
        body,
        grid=(n // wn, T // tt),
        in_specs=[pl.BlockSpec((tt, k), lambda j, i: (i, 0)), pl.BlockSpec((tt, wn), lambda j, i: (i, j))],
        out_specs=pl.BlockSpec((k, wn), lambda j, i: (0, j)),
        out_shape=jax.ShapeDtypeStruct((k, n), F32),
        compiler_params=_cp("parallel", "arbitrary"),
        name=name,
    )(x, d)


CONV_RC = 128
CONV_PAD = 32


def hyb_conv_fwd(u, dw_w, dw_b, name):
    def body(ua_ref, ug_ref, w_ref, b_ref, o_ref, xpad):
        xpad[0:CONV_PAD, :] = jnp.zeros((CONV_PAD, 128), F32)
        xpad[CONV_PAD:, :] = ua_ref[...] * _sigmoid(ug_ref[...])
        for r in range(T // CONV_RC):
            acc = jnp.broadcast_to(b_ref[...], (CONV_RC, 128))
            for j in range(CONV_K):
                acc = acc + w_ref[pl.ds(j, 1), :] * xpad[pl.ds(r * CONV_RC + CONV_PAD - (CONV_K - 1) + j, CONV_RC), :]
            o_ref[r * CONV_RC:(r + 1) * CONV_RC, :] = acc

    nb = CONV_C // 128
    return pl.pallas_call(
        body,
        grid=(nb,),
        in_specs=[
            pl.BlockSpec((T, 128), lambda c: (0, c)),
            pl.BlockSpec((T, 128), lambda c: (0, nb + c)),
            pl.BlockSpec((32, 128), lambda c: (0, c)),
            pl.BlockSpec((1, 128), lambda c: (0, c)),
        ],
        out_specs=pl.BlockSpec((T, 128), lambda c: (0, c)),
        out_shape=jax.ShapeDtypeStruct((T, CONV_C), F32),
        scratch_shapes=[pltpu.VMEM((T + CONV_PAD, 128), F32)],
        compiler_params=_cp("parallel"),
        name=name,
    )(u, u, dw_w, dw_b)


def hyb_conv_bwd(dc, u, dw_w, name):
    def body(dc_ref, ua_ref, ug_ref, w_ref, da_ref, dgate_ref, dw_ref, db_ref, xpad, dcpad, dwacc):
        ua = ua_ref[...]
        sig = _sigmoid(ug_ref[...])
        xpad[0:CONV_PAD, :] = jnp.zeros((CONV_PAD, 128), F32)
        xpad[CONV_PAD:, :] = ua * sig
        dcpad[0:T, :] = dc_ref[...]
        dcpad[T:, :] = jnp.zeros((CONV_PAD, 128), F32)
        dwacc[...] = jnp.zeros_like(dwacc)
        dbacc = jnp.zeros((8, 128), F32)
        for r in range(T // CONV_RC):
            r0 = r * CONV_RC
            dcr = dc_ref[r0:r0 + CONV_RC, :]
            dbacc = dbacc + dcr.reshape(CONV_RC // 8, 8, 128).sum(axis=0)
            dglu = jnp.zeros((CONV_RC, 128), F32)
            for j in range(CONV_K):
                dglu = dglu + w_ref[pl.ds(j, 1), :] * dcpad[pl.ds(r0 + (CONV_K - 1) - j, CONV_RC), :]
                prod = dcr * xpad[pl.ds(r0 + CONV_PAD - (CONV_K - 1) + j, CONV_RC), :]
                dwacc[8 * j:8 * j + 8, :] += prod.reshape(CONV_RC // 8, 8, 128).sum(axis=0)
            sg = sig[r0:r0 + CONV_RC, :]
            da_ref[r0:r0 + CONV_RC, :] = dglu * sg
            dgate_ref[r0:r0 + CONV_RC, :] = dglu * ua[r0:r0 + CONV_RC, :] * sg * (1.0 - sg)
        for j in range(CONV_K):
            dw_ref[pl.ds(j, 1), :] = jnp.sum(dwacc[8 * j:8 * j + 8, :], axis=0, keepdims=True)
        dw_ref[pl.ds(CONV_K, 1), :] = jnp.zeros((1, 128), F32)
        db_ref[...] = jnp.sum(dbacc, axis=0, keepdims=True)

    nb = CONV_C // 128
    col = pl.BlockSpec((T, 128), lambda c: (0, c))
    return pl.pallas_call(
        body,
        grid=(nb,),
        in_specs=[col, col, pl.BlockSpec((T, 128), lambda c: (0, nb + c)), pl.BlockSpec((32, 128), lambda c: (0, c))],
        out_specs=[col, col, pl.BlockSpec((32, 128), lambda c: (0, c)), pl.BlockSpec((1, 128), lambda c: (0, c))],
        out_shape=[
            jax.ShapeDtypeStruct((T, CONV_C), F32),
            jax.ShapeDtypeStruct((T, CONV_C), F32),
            jax.ShapeDtypeStruct((32, CONV_C), F32),
            jax.ShapeDtypeStruct((1, CONV_C), F32),
        ],
        scratch_shapes=[
            pltpu.VMEM((T + CONV_PAD, 128), F32),
            pltpu.VMEM((T + CONV_PAD, 128), F32),
            pltpu.VMEM((8 * 32, 128), F32),
        ],
        compiler_params=_cp("parallel"),
        name=name,
    )(dc, u, u, dw_w)


ATT_SCALE = A_HD ** -0.5
N_BLK = T // BLK


def _att_masks():
    i = lax.broadcasted_iota(jnp.int32, (BLK, 2 * BLK), 0)
    j = lax.broadcasted_iota(jnp.int32, (BLK, 2 * BLK), 1)
    band = (j >= i) & (j <= i + BLK)
    i1 = lax.broadcasted_iota(jnp.int32, (BLK, BLK), 0)
    j1 = lax.broadcasted_iota(jnp.int32, (BLK, BLK), 1)
    return band, j1 <= i1


def _att_rows(d, t, first):
    if first:
        base = t
        return pl.ds(base, BLK, stride=d), pl.ds(base, BLK, stride=d)
    c = t % d
    n = t // d + 1
    base = c + (BLK * d) * n
    return pl.ds(base, BLK, stride=d), pl.ds(base - BLK * d, 2 * BLK, stride=d)


def _loop_pairs(n, block, per=2):
    def several(i, carry):
        for k in range(per):
            block(per * i + k, carry)
        return carry

    if n >= per:
        lax.fori_loop(0, n // per, several, 0)
    for t in range(n - n % per, n):
        block(t, 0)


def attn_fwd(qkv, name):
    def body(q_ref, k_ref, v_ref, o_ref, lse_ref, og, lg):
        band, tri = _att_masks()
        head0 = lax.broadcasted_iota(jnp.int32, (BLK, 128), 1) < A_HD
        for g, d in enumerate(DILATIONS):
            def block(t, carry, first, g=g, d=d):
                rq, rk = _att_rows(d, t, first)
                q2 = q_ref[rq, :]
                k2 = k_ref[rk, :].astype(BF16)
                v2 = v_ref[rk, :].astype(BF16)
                o_e, l_e = [], []
                for e in range(2):
                    qe = jnp.where(head0 if e == 0 else ~head0, q2, 0.0).astype(BF16)
                    s = _dot_nt(qe, k2) * ATT_SCALE
                    s = jnp.where(tri if first else band, s, -jnp.inf)
                    m = jnp.max(s, axis=-1, keepdims=True)
                    p = jnp.exp(s - m)
                    den = jnp.sum(p, axis=-1, keepdims=True)
                    o_e.append(_dot(p.astype(BF16), v2) / den)
                    l_e.append(m + jnp.log(den))
                og[g, rq, :] = jnp.where(head0, o_e[0], o_e[1])
                lg[g, rq, :] = jnp.where(head0, l_e[0], l_e[1])
                return carry

            _loop_pairs(d, functools.partial(block, first=True), per=4)
            _loop_pairs(N_BLK - d, functools.partial(block, first=False), per=4)
        rc = 256
        for r in range(T // rc):
            rows = pl.ds(r * rc, rc)
            l0, l1, l2 = lg[0, rows, :], lg[1, rows, :], lg[2, rows, :]
            m = jnp.maximum(jnp.maximum(l0, l1), l2)
            e0, e1, e2 = jnp.exp(l0 - m), jnp.exp(l1 - m), jnp.exp(l2 - m)
            z = e0 + e1 + e2
            o_ref[rows, :] = (e0 / z) * og[0, rows, :] + (e1 / z) * og[1, rows, :] + (e2 / z) * og[2, rows, :]
            lse_ref[rows, :] = m + jnp.log(z)

    npair = A_HEADS // 2
    col = lambda off: pl.BlockSpec((T, 128), lambda p: (0, off + p))
    return pl.pallas_call(
        body,
        grid=(npair,),
        in_specs=[col(0), col(npair), col(2 * npair)],
        out_specs=[col(0), col(0)],
        out_shape=[jax.ShapeDtypeStruct((T, A_W), F32), jax.ShapeDtypeStruct((T, A_W), F32)],
        scratch_shapes=[pltpu.VMEM((3, T, 128), F32), pltpu.VMEM((3, T, 128), F32)],
        compiler_params=_cp("parallel"),
        name=name,
    )(qkv, qkv, qkv)


def attn_bwd(qkv, o, lse, do, name):
    def body(q_ref, k_ref, v_ref, o_ref, lse_ref, do_ref, dq_ref, dk_ref, dv_ref):
        band, tri = _att_masks()
        head0 = lax.broadcasted_iota(jnp.int32, (BLK, 128), 1) < A_HD
        head0k = lax.broadcasted_iota(jnp.int32, (2 * BLK, 128), 1) < A_HD
        dq_ref[...] = jnp.zeros_like(dq_ref)
        dk_ref[...] = jnp.zeros_like(dk_ref)
        dv_ref[...] = jnp.zeros_like(dv_ref)
        for d in DILATIONS:
            def block(t, carry, first, d=d):
                rq, rk = _att_rows(d, t, first)
                q2 = q_ref[rq, :]
                k2 = k_ref[rk, :].astype(BF16)
                v2 = v_ref[rk, :].astype(BF16)
                do2 = do_ref[rq, :]
                l2 = lse_ref[rq, :]
                prod = do2 * o_ref[rq, :]
                q2b = q2.astype(BF16)
                do2b = do2.astype(BF16)
                dq_e, dk_e, dv_e = [], [], []
                for e in range(2):
                    he = head0 if e == 0 else ~head0
                    qe = jnp.where(he, q2, 0.0).astype(BF16)
                    doe = jnp.where(he, do2, 0.0).astype(BF16)
                    l = l2[:, A_HD * e:A_HD * e + 1]
                    dd = jnp.sum(jnp.where(he, prod, 0.0), axis=-1, keepdims=True)
                    s = _dot_nt(qe, k2) * ATT_SCALE
                    p = jnp.where(tri if first else band, jnp.exp(s - l), 0.0)
                    dp = _dot_nt(doe, v2)
                    ds = (p * (dp - dd) * ATT_SCALE).astype(BF16)
                    dq_e.append(_dot(ds, k2))
                    dk_e.append(_dot_tn(ds, q2b))
                    dv_e.append(_dot_tn(p.astype(BF16), do2b))
                hk = head0 if first else head0k
                dq_ref[rq, :] += jnp.where(head0, dq_e[0], dq_e[1])
                dk_ref[rk, :] += jnp.where(hk, dk_e[0], dk_e[1])
                dv_ref[rk, :] += jnp.where(hk, dv_e[0], dv_e[1])
                return carry

            _loop_pairs(d, functools.partial(block, first=True), per=4)
            _loop_pairs(N_BLK - d, functools.partial(block, first=False), per=4)

    npair = A_HEADS // 2
    col = lambda off: pl.BlockSpec((T, 128), lambda p: (0, off + p))
    return pl.pallas_call(
        body,
        grid=(npair,),
        in_specs=[col(0), col(npair), col(2 * npair), col(0), col(0), col(0)],
        out_specs=[col(0), col(0), col(0)],
        out_shape=[jax.ShapeDtypeStruct((T, A_W), F32)] * 3,
        compiler_params=_cp("parallel"),
        name=name,
    )(qkv, qkv, qkv, o, lse, do)


def _ln_silu(x, g, b):
    mu = jnp.mean(x, axis=-1, keepdims=True)
    xc = x - mu
    rstd = lax.rsqrt(jnp.mean(xc * xc, axis=-1, keepdims=True) + EPS)
    xh = xc * rstd
    y = xh * g + b
    sig = _sigmoid(y)
    return y * sig, (xh, rstd, y, sig)


def hyb_out_fwd(h, attn, cpre, ln_g, ln_b, w_out, name):
    tt = 512

    def body(h_ref, a_ref, c_ref, g_ref, b_ref, w_ref, hnew_ref, cat_ref):
        cn, _ = _ln_silu(c_ref[...], g_ref[...], b_ref[...])
        ab = a_ref[...].astype(BF16)
        cb = cn.astype(BF16)
        cat_ref[:, 0:A_W] = ab
        cat_ref[:, A_W:D] = cb
        hnew_ref[...] = h_ref[...] + _dot(ab, w_ref[0:A_W, :]) + _dot(cb, w_ref[A_W:D, :])

    half = pl.BlockSpec((tt, A_W), lambda i: (i, 0))
    vec = pl.BlockSpec((1, CONV_C), lambda i: (0, 0))
    full = pl.BlockSpec((tt, D), lambda i: (i, 0))
    return pl.pallas_call(
        body,
        grid=(T // tt,),
        in_specs=[full, half, half, vec, vec, pl.BlockSpec((D, D), lambda i: (0, 0))],
        out_specs=[full, full],
        out_shape=[jax.ShapeDtypeStruct((T, D), F32), jax.ShapeDtypeStruct((T, D), BF16)],
        compiler_params=_cp("parallel"),
        name=name,
    )(h, attn, cpre, ln_g, ln_b, w_out)


def hyb_out_bwd(dres, cpre, ln_g, ln_b, w_out, name):
    tt = 512

    def body(d_ref, c_ref, g_ref, b_ref, w_ref, da_ref, dc_ref, dg_ref, db_ref):
        i = pl.program_id(0)
        db16 = d_ref[...].astype(BF16)
        da_ref[...] = _dot_nt(db16, w_ref[0:A_W, :])
        dcn = _dot_nt(db16, w_ref[A_W:D, :])
        g = g_ref[...]
        _, (xh, rstd, y, sig) = _ln_silu(c_ref[...], g, b_ref[...])
        dy = dcn * _dsilu(y, sig)
        dxh = dy * g
        dc_ref[...] = rstd * (dxh - jnp.mean(dxh, axis=-1, keepdims=True)
                              - xh * jnp.mean(dxh * xh, axis=-1, keepdims=True))
        dg = jnp.sum(dy * xh, axis=0, keepdims=True)
        db = jnp.sum(dy, axis=0, keepdims=True)

        @pl.when(i == 0)
        def _():
            dg_ref[...] = dg
            db_ref[...] = db

        @pl.when(i > 0)
        def _():
            dg_ref[...] += dg
            db_ref[...] += db

    half = pl.BlockSpec((tt, A_W), lambda i: (i, 0))
    vec = pl.BlockSpec((1, CONV_C), lambda i: (0, 0))
    return pl.pallas_call(
        body,
        grid=(T // tt,),
        in_specs=[pl.BlockSpec((tt, D), lambda i: (i, 0)), half, vec, vec, pl.BlockSpec((D, D), lambda i: (0, 0))],
        out_specs=[half, half, vec, vec],
        out_shape=[
            jax.ShapeDtypeStruct((T, A_W), F32),
            jax.ShapeDtypeStruct((T, CONV_C), F32),
            jax.ShapeDtypeStruct((1, CONV_C), F32),
            jax.ShapeDtypeStruct((1, CONV_C), F32),
        ],
        compiler_params=_cp("arbitrary"),
        name=name,
    )(dres, cpre, ln_g, ln_b, w_out)


def hybrid_fwd(h, g_row, w_in, dw_w, dw_b, ln_g, ln_b, w_out, rope, tag):
    hn, qkv, u = proj_fwd(h, g_row, w_in, [(0, 3 * A_W), (3 * A_W, 2 * CONV_C)], f"hyb_proj_{tag}", rope=rope)
    cpre = hyb_conv_fwd(u, dw_w, dw_b, f"hyb_conv_{tag}")
    attn, lse = attn_fwd(qkv, f"attn_fwd_{tag}")
    hnew, cat = hyb_out_fwd(h, attn, cpre, ln_g, ln_b, w_out, f"hyb_out_{tag}")
    return hnew, (h, hn, qkv, u, cpre, attn, lse, cat)


def hybrid_bwd(dres, saved, g_row, w_in, dw_w, ln_g, ln_b, w_out, rope, tag):
    h, hn, qkv, u, cpre, attn, lse, cat = saved
    d_attn, d_cpre, d_lng, d_lnb = hyb_out_bwd(dres, cpre, ln_g, ln_b, w_out, f"hyb_out_bwd_{tag}")
    d_wout = mm_tn_full(cat, dres, BF16, f"hyb_wout_grad_{tag}")
    d_a, d_gate, d_dw, d_db = hyb_conv_bwd(d_cpre, u, dw_w, f"hyb_conv_bwd_{tag}")
    dq, dk, dv = attn_bwd(qkv, attn, lse, d_attn, f"attn_bwd_{tag}")
    splits = [(0, A_W), (A_W, A_W), (2 * A_W, A_W), (3 * A_W, CONV_C), (3 * A_W + CONV_C, CONV_C)]
    dres_new, d_norm, dproj = proj_bwd_data(
        h, g_row, w_in, [dq, dk, dv, d_a, d_gate], splits, dres, f"hyb_proj_bwd_{tag}", rope=rope, n_rot=2)
    d_win = cols_to_slabs(mm_tn_full(hn, dproj, F32, f"hyb_win_grad_{tag}"), None, f"hyb_win_slabs_{tag}")
    return dres_new, dict(norm=d_norm, w_in=d_win, dw_w=d_dw[:CONV_K], dw_b=d_db, ln_g=d_lng, ln_b=d_lnb, w_out=d_wout)


G_SCALE = G_DK ** -0.5
GP_RC = 256
GP_PAD = 8


def gdn_prep_fwd(x, conv_w, name):
    def body(x_ref, w_ref, o_ref, xpad):
        cb = pl.program_id(0)
        xpad[0:GP_PAD, :] = jnp.zeros((GP_PAD, 128), F32)
        xpad[GP_PAD:, :] = x_ref[...]
        for r in range(T // GP_RC):
            r0 = r * GP_RC
            y = jnp.zeros((GP_RC, 128), F32)
            for j in range(G_CONV):
                y = y + w_ref[pl.ds(j, 1), :] * xpad[pl.ds(r0 + GP_PAD - (G_CONV - 1) + j, GP_RC), :]
            s = y * _sigmoid(y)
            n = lax.rsqrt(jnp.sum(s * s, axis=-1, keepdims=True) + EPS)
            o_ref[r0:r0 + GP_RC, :] = s * jnp.where(cb < 2 * G_HEADS, n, 1.0)

    nb = G_QKV // 128
    return pl.pallas_call(
        body,
        grid=(nb,),
        in_specs=[pl.BlockSpec((T, 128), lambda c: (0, c)), pl.BlockSpec((G_CONV, 128), lambda c: (0, c))],
        out_specs=pl.BlockSpec((T, 128), lambda c: (0, c)),
        out_shape=jax.ShapeDtypeStruct((T, G_QKV), F32),
        scratch_shapes=[pltpu.VMEM((T + GP_PAD, 128), F32)],
        compiler_params=_cp("parallel"),
        name=name,
    )(x, conv_w)


def gdn_prep_bwd(dout, x, conv_w, part, l2, name):
    def body(d_ref, x_ref, w_ref, dx_ref, dw_ref, xpad, dypad, dwacc):
        xpad[0:GP_PAD, :] = jnp.zeros((GP_PAD, 128), F32)
        xpad[GP_PAD:, :] = x_ref[...]
        dypad[T:, :] = jnp.zeros((GP_PAD, 128), F32)
        dwacc[...] = jnp.zeros_like(dwacc)
        for r in range(T // GP_RC):
            r0 = r * GP_RC
            y = jnp.zeros((GP_RC, 128), F32)
            xs = []
            for j in range(G_CONV):
                xj = xpad[pl.ds(r0 + GP_PAD - (G_CONV - 1) + j, GP_RC), :]
                xs.append(xj)
                y = y + w_ref[pl.ds(j, 1), :] * xj
            sig = _sigmoid(y)
            s = y * sig
            d = d_ref[r0:r0 + GP_RC, :]
            if l2:
                n = lax.rsqrt(jnp.sum(s * s, axis=-1, keepdims=True) + EPS)
                out = s * n
                d = n * (d - out * jnp.sum(d * out, axis=-1, keepdims=True))
            dy = d * _dsilu(y, sig)
            dypad[r0:r0 + GP_RC, :] = dy
            for j in range(G_CONV):
                dwacc[8 * j:8 * j + 8, :] += (dy * xs[j]).reshape(GP_RC // 8, 8, 128).sum(axis=0)
        for r in range(T // GP_RC):
            r0 = r * GP_RC
            dx = jnp.zeros((GP_RC, 128), F32)
            for j in range(G_CONV):
                dx = dx + w_ref[pl.ds(j, 1), :] * dypad[pl.ds(r0 + (G_CONV - 1) - j, GP_RC), :]
            dx_ref[r0:r0 + GP_RC, :] = dx
        for j in range(G_CONV):
            dw_ref[pl.ds(j, 1), :] = jnp.sum(dwacc[8 * j:8 * j + 8, :], axis=0, keepdims=True)

    nb = G_HEADS
    off = part * nb
    col = pl.BlockSpec((T, 128), lambda c: (0, c))
    return pl.pallas_call(
        body,
        grid=(nb,),
        in_specs=[col, pl.BlockSpec((T, 128), lambda c: (0, off + c)), pl.BlockSpec((G_CONV, 128), lambda c: (0, off + c))],
        out_specs=[col, pl.BlockSpec((G_CONV, 128), lambda c: (0, c))],
        out_shape=[jax.ShapeDtypeStruct((T, G_HEADS * G_DK), F32), jax.ShapeDtypeStruct((G_CONV, G_HEADS * G_DK), F32)],
        scratch_shapes=[
            pltpu.VMEM((T + GP_PAD, 128), F32),
            pltpu.VMEM((T + GP_PAD, 128), F32),
            pltpu.VMEM((8 * G_CONV, 128), F32),
        ],
        compiler_params=_cp("parallel"),
        name=name,
    )(dout, x, conv_w)


def _seg_cumsum(x, reverse=False):
    row = lax.broadcasted_iota(jnp.int32, x.shape, 0) % CH
    s = 1
    while s < CH:
        if reverse:
            x = x + jnp.where(row < CH - s, pltpu.roll(x, x.shape[0] - s, 0), 0.0)
        else:
            x = x + jnp.where(row >= s, pltpu.roll(x, s, 0), 0.0)
        s *= 2
    return x


def _gdn_gates(ba_ref, alog_ref, dt_ref, h):
    ba = ba_ref[...]
    lane = lax.broadcasted_iota(jnp.int32, ba.shape, 1)
    b_col = jnp.sum(jnp.where(lane == h, ba, 0.0), axis=1, keepdims=True)
    a_col = jnp.sum(jnp.where(lane == G_HEADS + h, ba, 0.0), axis=1, keepdims=True)
    lane8 = lax.broadcasted_iota(jnp.int32, (1, G_HEADS), 1)
    alog = jnp.sum(jnp.where(lane8 == h, alog_ref[...], 0.0), axis=1, keepdims=True)
    dt = jnp.sum(jnp.where(lane8 == h, dt_ref[...], 0.0), axis=1, keepdims=True)
    beta = _sigmoid(b_col)
    xa = a_col + dt
    softplus = jnp.maximum(xa, 0.0) + jnp.log(1.0 + jnp.exp(-jnp.abs(xa)))
    ea = jnp.exp(alog)
    return beta, -ea * softplus, xa, ea


def _chunk_masks():
    i = lax.broadcasted_iota(jnp.int32, (CH, CH), 0)
    j = lax.broadcasted_iota(jnp.int32, (CH, CH), 1)
    return i >= j, i > j, i, j


def _decay(gcc, causal):
    gm = gcc[:, 0:CH]
    return jnp.where(causal, jnp.exp(jnp.minimum(gm - gm.T, 0.0)), 0.0)


def _split(a):
    hi = a.astype(BF16)
    return hi, (a - hi.astype(F32)).astype(BF16)


def _dot3(a, b):
    ah, al = _split(a)
    bh, bl = _split(b)
    return _dot(ah, bh) + (_dot(ah, bl) + _dot(al, bh))


def _unit_lower_inverse(lms, i, j):
    eye = jnp.where(i == j, 1.0, 0.0)
    ms = [None] * len(lms)
    b = 1
    while b < CH:
        pair = ((i // (2 * b)) == (j // (2 * b))) & ((i // b) % 2 == 1) & ((j // b) % 2 == 0)
        lbs = [jnp.where(pair, lm, 0.0) for lm in lms]
        if b == 1:
            ms = [eye - lb for lb in lbs]
        else:
            ts = [_dot3(m, lb) for m, lb in zip(ms, lbs)]
            ms = [m - _dot3(t, m) for m, t in zip(ms, ts)]
        b *= 2
    return ms


def gdn_local_fwd(qkv, ba, alog, dtb, name):
    def body(q_ref, k_ref, v_ref, ba_ref, al_ref, dt_ref, u_ref, w_ref, qd_ref, kd_ref, at_ref, el_ref, ti_ref, gcs):
        h = pl.program_id(1)
        beta, g, _, _ = _gdn_gates(ba_ref, al_ref, dt_ref, h)
        gc = _seg_cumsum(jnp.broadcast_to(g, (GRP, 128)))
        gcs[...] = gc
        causal, strict, i, j = _chunk_masks()
        lms = []
        for c in range(CPG):
            r = slice(c * CH, (c + 1) * CH)
            q, k = q_ref[r, :], k_ref[r, :]
            gcc = gc[r, :]
            ec = jnp.exp(gcc)
            gl = gcs[pl.ds(c * CH + CH - 1, 1), :]
            dm = _decay(gcc, causal)
            kbf = k.astype(BF16)
            a1 = _dot_nt((k * beta[r, :]).astype(BF16), kbf)
            lms.append(jnp.where(strict, a1 * dm, 0.0))
            qs = q * G_SCALE
            qd_ref[r, :] = (qs * ec).astype(BF16)
            kd_ref[r, :] = (k * jnp.exp(gl - gcc)).astype(BF16)
            at_ref[r, :] = (_dot_nt(qs.astype(BF16), kbf) * dm).astype(BF16)
            el_ref[pl.ds(c, 1), :] = jnp.exp(gl)
        tinvs = _unit_lower_inverse(lms, i, j)
        for c in range(CPG):
            r = slice(c * CH, (c + 1) * CH)
            bt = beta[r, :]
            tb = tinvs[c].astype(BF16)
            u_ref[r, :] = _dot(tb, (v_ref[r, :] * bt).astype(BF16))
            w_ref[r, :] = _dot(tb, (k_ref[r, :] * bt * jnp.exp(gc[r, :])).astype(BF16)).astype(BF16)
            ti_ref[r, :] = tinvs[c]

    hd = lambda off: pl.BlockSpec((GRP, 128), lambda i, h: (i, off + h))
    vec = pl.BlockSpec((1, G_HEADS), lambda i, h: (0, 0))
    sq = pl.BlockSpec((None, GRP, CH), lambda i, h: (h, i, 0))
    return pl.pallas_call(
        body,
        grid=(N_GRP, G_HEADS),
        in_specs=[hd(0), hd(G_HEADS), hd(2 * G_HEADS), pl.BlockSpec((GRP, 2 * G_HEADS), lambda i, h: (i, 0)), vec, vec],
        out_specs=[hd(0), hd(0), hd(0), hd(0), sq, pl.BlockSpec((None, CPG, 128), lambda i, h: (h, i, 0)), sq],
        out_shape=[
            jax.ShapeDtypeStruct((T, D), F32),
            jax.ShapeDtypeStruct((T, D), BF16),
            jax.ShapeDtypeStruct((T, D), BF16),
            jax.ShapeDtypeStruct((T, D), BF16),
            jax.ShapeDtypeStruct((G_HEADS, T, CH), BF16),
            jax.ShapeDtypeStruct((G_HEADS, T // CH, 128), F32),
            jax.ShapeDtypeStruct((G_HEADS, T, CH), F32),
        ],
        scratch_shapes=[pltpu.VMEM((GRP, 128), F32)],
        compiler_params=_cp("parallel", "parallel"),
        name=name,
    )(qkv, qkv, qkv, ba, alog, dtb)


def gdn_rec_fwd(u, w, qd, kd, at, el, name):
    def body(u_ref, w_ref, qd_ref, kd_ref, at_ref, el_ref, o_ref, vn_ref, st_ref, s_scr):
        @pl.when(pl.program_id(0) == 0)
        def _():
            s_scr[...] = jnp.zeros_like(s_scr)

        states = [s_scr[h] for h in range(G_HEADS)]
        for c in range(CPG):
            r = slice(c * CH, (c + 1) * CH)
            for h in range(G_HEADS):
                ln = slice(h * 128, (h + 1) * 128)
                s = states[h]
                st_ref[h, c] = s
                sb = s.astype(BF16)
                vn = (u_ref[r, ln] - _dot(w_ref[r, ln], sb)).astype(BF16)
                o_ref[r, ln] = _dot(qd_ref[r, ln], sb) + _dot(at_ref[h, r, :], vn)
                states[h] = s * el_ref[h, pl.ds(c, 1), :] + _dot_tn(kd_ref[r, ln], vn)
                vn_ref[r, ln] = vn
        for h in range(G_HEADS):
            s_scr[h] = states[h]

    row = pl.BlockSpec((GRP, D), lambda i: (i, 0))
    return pl.pallas_call(
        body,
        grid=(N_GRP,),
        in_specs=[row, row, row, row, pl.BlockSpec((G_HEADS, GRP, CH), lambda i: (0, i, 0)),
                  pl.BlockSpec((G_HEADS, CPG, 128), lambda i: (0, i, 0))],
        out_specs=[row, row, pl.BlockSpec((G_HEADS, CPG, 128, 128), lambda i: (0, i, 0, 0))],
        out_shape=[
            jax.ShapeDtypeStruct((T, D), F32),
            jax.ShapeDtypeStruct((T, D), BF16),
            jax.ShapeDtypeStruct((G_HEADS, T // CH, 128, 128), F32),
        ],
        scratch_shapes=[pltpu.VMEM((G_HEADS, 128, 128), F32)],
        compiler_params=_cp("arbitrary"),
        name=name,
    )(u, w, qd, kd, at, el)


def gdn_rec_bwd(do, w, qd, kd, at, el, vn, st, name):
    def body(do_ref, w_ref, qd_ref, kd_ref, at_ref, el_ref, vn_ref, st_ref,
             du_ref, dw_ref, dqd_ref, dkd_ref, dat_ref, del_ref, ds_scr):
        @pl.when(pl.program_id(0) == 0)
        def _():
            ds_scr[...] = jnp.zeros_like(ds_scr)

        dstates = [ds_scr[h] for h in range(G_HEADS)]
        for c in reversed(range(CPG)):
            r = slice(c * CH, (c + 1) * CH)
            for h in range(G_HEADS):
                ln = slice(h * 128, (h + 1) * 128)
                ds = dstates[h]
                dsb = ds.astype(BF16)
                sn = st_ref[h, c]
                snb = sn.astype(BF16)
                dob = do_ref[r, ln].astype(BF16)
                vnb = vn_ref[r, ln]
                dvn = (_dot(kd_ref[r, ln], dsb) + _dot_tn(at_ref[h, r, :], dob)).astype(BF16)
                du_ref[r, ln] = dvn
                dkd_ref[r, ln] = _dot_nt(vnb, dsb)
                tot = jnp.sum(jnp.sum(ds * sn, axis=1, keepdims=True), axis=0, keepdims=True)
                del_ref[h, pl.ds(c, 1), :] = jnp.broadcast_to(tot, (1, 128))
                dqd_ref[r, ln] = _dot_nt(dob, snb)
                dat_ref[h, r, :] = _dot_nt(dob, vnb)
                dw_ref[r, ln] = (-_dot_nt(dvn, snb)).astype(BF16)
                dstates[h] = ds * el_ref[h, pl.ds(c, 1), :] + _dot_tn(qd_ref[r, ln], dob) - _dot_tn(w_ref[r, ln], dvn)
        for h in range(G_HEADS):
            ds_scr[h] = dstates[h]

    last = N_GRP - 1
    row = pl.BlockSpec((GRP, D), lambda i: (last - i, 0))
    sq = pl.BlockSpec((G_HEADS, GRP, CH), lambda i: (0, last - i, 0))
    sc = pl.BlockSpec((G_HEADS, CPG, 128), lambda i: (0, last - i, 0))
    return pl.pallas_call(
        body,
        grid=(N_GRP,),
        in_specs=[row, row, row, row, sq, sc, row, pl.BlockSpec((G_HEADS, CPG, 128, 128), lambda i: (0, last - i, 0, 0))],
        out_specs=[row, row, row, row, sq, sc],
        out_shape=[
            jax.ShapeDtypeStruct((T, D), BF16),
            jax.ShapeDtypeStruct((T, D), BF16),
            jax.ShapeDtypeStruct((T, D), F32),
            jax.ShapeDtypeStruct((T, D), F32),
            jax.ShapeDtypeStruct((G_HEADS, T, CH), F32),
            jax.ShapeDtypeStruct((G_HEADS, T // CH, 128), F32),
        ],
        scratch_shapes=[pltpu.VMEM((G_HEADS, 128, 128), F32)],
        compiler_params=_cp("arbitrary"),
        name=name,
    )(do, w, qd, kd, at, el, vn, st)


def gdn_local_bwd(qkv, ba, alog, dtb, tinv, du, dw, dqd, dkd, dat, dl, name):
    def body(q_ref, k_ref, v_ref, ba_ref, al_ref, dt_ref, ti_ref, du_ref, dw_ref, dqd_ref, dkd_ref, dat_ref, dl_ref,
             dq_ref, dk_ref, dv_ref, dba_ref, dal_ref, ddt_ref, gcs):
        gi = pl.program_id(0)
        h = pl.program_id(1)
        beta, g, xa, ea = _gdn_gates(ba_ref, al_ref, dt_ref, h)
        gc = _seg_cumsum(jnp.broadcast_to(g, (GRP, 128)))
        gcs[...] = gc
        causal, strict, _, _ = _chunk_masks()
        dgc_l, dgl_l, dbeta_l, state = [], [], [], []
        for c in range(CPG):
            r = slice(c * CH, (c + 1) * CH)
            q, k, v = q_ref[r, :], k_ref[r, :], v_ref[r, :]
            bt = beta[r, :]
            gcc = gc[r, :]
            ec = jnp.exp(gcc)
            gl = gcs[pl.ds(c * CH + CH - 1, 1), :]
            f2 = jnp.exp(gl - gcc)
            elc = jnp.exp(gl)
            dm = _decay(gcc, causal)
            qs = q * G_SCALE
            kb = k * bt
            vb = v * bt
            kbe = kb * ec
            kbf, kbb, qsb = k.astype(BF16), kb.astype(BF16), qs.astype(BF16)
            a1 = _dot_nt(kbb, kbf)
            qk = _dot_nt(qsb, kbf)
            ti = ti_ref[r, :]
            tb = ti.astype(BF16)
            du_c, dw_c = du_ref[r, :], dw_ref[r, :]
            dqd_c, dkd_c, dat_c = dqd_ref[r, :], dkd_ref[r, :], dat_ref[r, :]

            dqs = dqd_c * ec
            d_e = jnp.sum(dqd_c * qs, axis=1, keepdims=True)
            dk = dkd_c * f2
            tcol = jnp.sum(dkd_c * k, axis=1, keepdims=True) * f2[:, 0:1]
            dgl = jnp.sum(tcol, axis=0, keepdims=True) + dl_ref[pl.ds(c, 1), 0:1] * elc[:, 0:1]
            dgc = -tcol
            dqk = (dat_c * dm).astype(BF16)
            d_d = dat_c * qk
            dqs = dqs + _dot(dqk, kbf)
            dk = dk + _dot_tn(dqk, qsb)
            dtinv = _dot_nt(du_c, vb.astype(BF16)) + _dot_nt(dw_c, kbe.astype(BF16))
            dvb = _dot_tn(tb, du_c)
            dkbe = _dot_tn(tb, dw_c)
            dq_ref[r, :] = dqs * G_SCALE
            state.append((ti.T, dtinv, dm, a1, dkbe, dvb, d_d, dk, d_e, dgc, dgl))

        xs = [_dot3(st[0], st[1]) for st in state]
        dlms = [jnp.where(strict, -_dot3(x, st[0]), 0.0) for x, st in zip(xs, state)]

        for c in range(CPG):
            r = slice(c * CH, (c + 1) * CH)
            _, _, dm, a1, dkbe, dvb, d_d, dk, d_e, dgc, dgl = state[c]
            dlm = dlms[c]
            k, v = k_ref[r, :], v_ref[r, :]
            bt = beta[r, :]
            ec = jnp.exp(gc[r, :])
            kb = k * bt
            kbf, kbb = k.astype(BF16), kb.astype(BF16)
            da1 = (dlm * dm).astype(BF16)
            d_d = d_d + dlm * a1
            dkb = _dot(da1, kbf) + dkbe * ec
            dk = dk + _dot_tn(da1, kbb)
            d_e = d_e + jnp.sum(dkbe * kb, axis=1, keepdims=True)
            dk = dk + dkb * bt
            dbeta_l.append(jnp.sum(dkb * k, axis=1, keepdims=True) + jnp.sum(dvb * v, axis=1, keepdims=True))
            ddiff = d_d * dm
            dgc = dgc + jnp.sum(ddiff, axis=1, keepdims=True) - jnp.sum(ddiff.T, axis=1, keepdims=True)
            dgc = dgc + d_e * ec[:, 0:1]
            dgc_l.append(dgc)
            dgl_l.append(jnp.broadcast_to(dgl, (CH, 1)))
            dk_ref[r, :] = dk
            dv_ref[r, :] = dvb * bt

        dgc_all = jnp.broadcast_to(jnp.concatenate(dgc_l, axis=0), (GRP, 128))
        dg = _seg_cumsum(dgc_all, reverse=True)[:, 0:1] + jnp.concatenate(dgl_l, axis=0)
        dbeta = jnp.concatenate(dbeta_l, axis=0)
        da = dg * (-ea) * _sigmoid(xa)
        db = dbeta * beta * (1.0 - beta)
        lane = lax.broadcasted_iota(jnp.int32, (GRP, 2 * G_HEADS), 1)
        dba = jnp.where(lane == h, db, 0.0) + jnp.where(lane == G_HEADS + h, da, 0.0)
        lane8 = lax.broadcasted_iota(jnp.int32, (1, G_HEADS), 1)
        dal = jnp.where(lane8 == h, jnp.sum(dg * g, axis=0, keepdims=True), 0.0)
        ddt = jnp.where(lane8 == h, jnp.sum(da, axis=0, keepdims=True), 0.0)

        @pl.when(h == 0)
        def _():
            dba_ref[...] = dba

        @pl.when(h > 0)
        def _():
            dba_ref[...] += dba

        @pl.when((h == 0) & (gi == 0))
        def _():
            dal_ref[...] = dal
            ddt_ref[...] = ddt

        @pl.when((h > 0) | (gi > 0))
        def _():
            dal_ref[...] += dal
            ddt_ref[...] += ddt

    hd = lambda off: pl.BlockSpec((GRP, 128), lambda i, h: (i, off + h))
    vec = pl.BlockSpec((1, G_HEADS), lambda i, h: (0, 0))
    sq = pl.BlockSpec((None, GRP, CH), lambda i, h: (h, i, 0))
    gates = pl.BlockSpec((GRP, 2 * G_HEADS), lambda i, h: (i, 0))
    return pl.pallas_call(
        body,
        grid=(N_GRP, G_HEADS),
        in_specs=[hd(0), hd(G_HEADS), hd(2 * G_HEADS), gates, vec, vec, sq, hd(0), hd(0), hd(0), hd(0), sq,
                  pl.BlockSpec((None, CPG, 128), lambda i, h: (h, i, 0))],
        out_specs=[hd(0), hd(0), hd(0), gates, vec, vec],
        out_shape=[
            jax.ShapeDtypeStruct((T, D), F32),
            jax.ShapeDtypeStruct((T, D), F32),
            jax.ShapeDtypeStruct((T, D), F32),
            jax.ShapeDtypeStruct((T, 2 * G_HEADS), F32),
            jax.ShapeDtypeStruct((1, G_HEADS), F32),
            jax.ShapeDtypeStruct((1, G_HEADS), F32),
        ],
        scratch_shapes=[pltpu.VMEM((GRP, 128), F32)],
        compiler_params=_cp("arbitrary", "arbitrary"),
        name=name,
    )(qkv, qkv, qkv, ba, alog, dtb, tinv, du, dw, dqd, dkd, dat, dl)


def _gated_norm(o, z, g):
    rstd = lax.rsqrt(jnp.mean(o * o, axis=-1, keepdims=True) + EPS)
    oh = o * rstd
    sig = _sigmoid(z)
    return oh, rstd, sig


def gdn_out_fwd(h, o, z, norm_g, w_out, name):
    tt = 512

    def body(h_ref, o_ref, z_ref, g_ref, w_ref, hnew_ref, cat_ref):
        g = g_ref[...]
        for hh in range(G_HEADS):
            ln = slice(hh * 128, (hh + 1) * 128)
            zz = z_ref[:, ln]
            oh, _, sig = _gated_norm(o_ref[:, ln], zz, g)
            cat_ref[:, ln] = (oh * g * (zz * sig)).astype(BF16)
        hnew_ref[...] = h_ref[...] + _dot(cat_ref[...], w_ref[...])

    full = pl.BlockSpec((tt, D), lambda i: (i, 0))
    return pl.pallas_call(
        body,
        grid=(T // tt,),
        in_specs=[full, full, full, pl.BlockSpec((1, 128), lambda i: (0, 0)), pl.BlockSpec((D, D), lambda i: (0, 0))],
        out_specs=[full, full],
        out_shape=[jax.ShapeDtypeStruct((T, D), F32), jax.ShapeDtypeStruct((T, D), BF16)],
        compiler_params=_cp("parallel"),
        name=name,
    )(h, o, z, norm_g, w_out)


def gdn_out_bwd(dres, o, z, norm_g, w_out, name):
    tt = 512

    def body(d_ref, o_ref, z_ref, g_ref, w_ref, do_ref, dz_ref, dg_ref, dcat):
        i = pl.program_id(0)
        g = g_ref[...]
        dcat[...] = _dot_nt(d_ref[...].astype(BF16), w_ref[...])
        dg = jnp.zeros((1, 128), F32)
        for hh in range(G_HEADS):
            ln = slice(hh * 128, (hh + 1) * 128)
            zz = z_ref[:, ln]
            oh, rstd, sig = _gated_norm(o_ref[:, ln], zz, g)
            dout = dcat[:, ln]
            dy = dout * (zz * sig)
            dz_ref[:, ln] = dout * (oh * g) * _dsilu(zz, sig)
            dg = dg + jnp.sum(dy * oh, axis=0, keepdims=True)
            doh = dy * g
            do_ref[:, ln] = rstd * (doh - oh * jnp.mean(doh * oh, axis=-1, keepdims=True))

        @pl.when(i == 0)
        def _():
            dg_ref[...] = dg

        @pl.when(i > 0)
        def _():
            dg_ref[...] += dg

    full = pl.BlockSpec((tt, D), lambda i: (i, 0))
    vec = pl.BlockSpec((1, 128), lambda i: (0, 0))
    return pl.pallas_call(
        body,
        grid=(T // tt,),
        in_specs=[full, full, full, vec, pl.BlockSpec((D, D), lambda i: (0, 0))],
        out_specs=[full, full, vec],
        out_shape=[jax.ShapeDtypeStruct((T, D), F32), jax.ShapeDtypeStruct((T, D), F32), jax.ShapeDtypeStruct((1, 128), F32)],
        scratch_shapes=[pltpu.VMEM((tt, D), F32)],
        compiler_params=_cp("arbitrary"),
        name=name,
    )(dres, o, z, norm_g, w_out)


GDN_SPLITS = [(0, 1024), (1024, 1024), (2048, 1024), (3072, 1024), (4096, 2 * G_HEADS)]


def gdn_fwd(h, g_row, w_in, conv_w, alog, dtb, norm_g, w_out, tag):
    hn, qkv_pre, z, ba = proj_fwd(h, g_row, w_in, [(0, G_QKV), (G_QKV, 1024), (4096, 2 * G_HEADS)], f"gdn_proj_{tag}")
    qkv = gdn_prep_fwd(qkv_pre, conv_w, f"gdn_prep_{tag}")
    u, w, qd, kd, at, el, tinv = gdn_local_fwd(qkv, ba, alog, dtb, f"gdn_local_{tag}")
    o, vn, st = gdn_rec_fwd(u, w, qd, kd, at, el, f"gdn_rec_{tag}")
    hnew, cat = gdn_out_fwd(h, o, z, norm_g, w_out, f"gdn_out_{tag}")
    return hnew, (h, hn, qkv_pre, z, ba, qkv, w, qd, kd, at, el, tinv, o, vn, st, cat)


def gdn_bwd(dres, saved, g_row, w_in, conv_w, alog, dtb, norm_g, w_out, tag):
    h, hn, qkv_pre, z, ba, qkv, w, qd, kd, at, el, tinv, o, vn, st, cat = saved
    d_o, d_z, d_ng = gdn_out_bwd(dres, o, z, norm_g, w_out, f"gdn_out_bwd_{tag}")
    d_wout = mm_tn_full(cat, dres, BF16, f"gdn_wout_grad_{tag}")
    du, dw, dqd, dkd, dat, dl = gdn_rec_bwd(d_o, w, qd, kd, at, el, vn, st, f"gdn_rec_bwd_{tag}")
    dq, dk, dv, dba, dal, ddt = gdn_local_bwd(qkv, ba, alog, dtb, tinv, du, dw, dqd, dkd, dat, dl, f"gdn_local_bwd_{tag}")
    dpre, dcw = [], []
    for part, d in enumerate((dq, dk, dv)):
        dx, dwc = gdn_prep_bwd(d, qkv_pre, conv_w, part, part < 2, f"gdn_prep_bwd_{tag}_{part}")
        dpre.append(dx)
        dcw.append(dwc)
    parts = dpre + [d_z, dba]
    dres_new, d_norm, dproj = proj_bwd_data(h, g_row, w_in, parts, GDN_SPLITS, dres, f"gdn_proj_bwd_{tag}")
    d_win = cols_to_slabs(mm_tn_full(hn, dproj, F32, f"gdn_win_grad_{tag}"), mm_tn(hn, dba, f"gdn_win_grad_ba_{tag}"),
                          f"gdn_win_slabs_{tag}")
    return dres_new, dict(norm=d_norm, w_in=d_win, conv_w=jnp.concatenate(dcw, axis=1), A_log=dal, dt_bias=ddt,
                          norm_g=d_ng, w_out=d_wout)


MESH = pl.DeviceIdType.MESH
ANY = pl.BlockSpec(memory_space=pl.ANY)


def _coords():
    return lax.axis_index("x"), lax.axis_index("y"), lax.axis_index("c")


def _slot(p):
    return 4 * p[0] + 2 * p[1] + p[2]


def all_gather(shards, name):
    k_n = len(shards)

    def body(*refs):
        srcs, dsts = refs[:k_n], refs[k_n:2 * k_n]
        send_sems, recv_sems, local_sems = refs[2 * k_n:]
        x, y, c = _coords()
        me, sibling = (x, y, c), (x, y, 1 - c)
        chips = [(1 - x, y), (x, 1 - y), (1 - x, 1 - y)]

        def copy(k, s, block, to, from_src=False):
            rows = dsts[k].at[_slot(block)]
            return pltpu.make_async_remote_copy(
                src_ref=srcs[k] if from_src else rows, dst_ref=rows,
                send_sem=send_sems.at[k, s], recv_sem=recv_sems.at[k, s], device_id=to, device_id_type=MESH)

        local = [pltpu.make_async_copy(srcs[k], dsts[k].at[_slot(me)], local_sems.at[k]) for k in range(k_n)]
        for cp in local:
            cp.start()
        first = []
        for k in range(k_n):
            first.append(copy(k, 0, me, sibling, True))
            first += [copy(k, 1 + j, me, (*chip, c), True) for j, chip in enumerate(chips)]
        for cp in first:
            cp.start()
        passed = []
        for j, chip in enumerate(chips):
            for k in range(k_n):
                copy(k, 1 + j, (*chip, c), me).wait_recv()
                fw = copy(k, 4 + j, (*chip, c), sibling)
                fw.start()
                passed.append(fw)
        for k in range(k_n):
            copy(k, 0, sibling, me).wait_recv()
            for j, chip in enumerate(chips):
                copy(k, 4 + j, (*chip, 1 - c), me).wait_recv()
        for cp in first + passed:
            cp.wait_send()
        for cp in local:
            cp.wait()

    return pl.pallas_call(
        body,
        in_specs=[ANY] * k_n,
        out_specs=[ANY] * k_n,
        out_shape=[jax.ShapeDtypeStruct((N_DEV,) + s.shape, s.dtype) for s in shards],
        scratch_shapes=[pltpu.SemaphoreType.DMA((k_n, 7)), pltpu.SemaphoreType.DMA((k_n, 7)),
                        pltpu.SemaphoreType.DMA((k_n,))],
        name=name,
    )(*shards)


HBM = pl.BlockSpec(memory_space=pltpu.HBM)
SEM = pl.BlockSpec(memory_space=pltpu.SEMAPHORE)
EFFECT = pltpu.SideEffectType.DATAFLOW_SIDE_EFFECTING


def _hbm(a):
    return pltpu.with_memory_space_constraint(a, pltpu.HBM)


def _peer_list(x, y, c):
    peers = []
    for j in range(1, N_DEV):
        jx, jy, jc = (j >> 2) & 1, (j >> 1) & 1, j & 1
        peers.append((x if jx == 0 else 1 - x, y if jy == 0 else 1 - y, c if jc == 0 else 1 - c))
    return peers


def _push_views(kind, layer, src_ref, land_ref, me, peer_slot):
    if kind == "gather":
        return src_ref, land_ref.at[me], land_ref.at[peer_slot]
    if layer is None:
        return src_ref.at[peer_slot], land_ref.at[me], land_ref.at[peer_slot]
    return src_ref.at[peer_slot], land_ref.at[me, layer], land_ref.at[peer_slot, layer]


def _push_copies(groups, srcs, lands, sems):
    x, y, c = _coords()
    me = _slot((x, y, c))
    peers = _peer_list(x, y, c)
    t = 0
    for gi, group in enumerate(groups):
        for ti, (kind, layer, _, li) in enumerate(group):
            for j, peer in enumerate(peers):
                out, there, here = _push_views(kind, layer, srcs[t], lands[li], me, _slot(peer))
                k = ti * (N_DEV - 1) + j
                yield out, there, here, sems[2 * gi].at[k], sems[2 * gi + 1].at[k], peer
            t += 1


def push_start(groups, lands, name, carry=()):
    flat = [it for g in groups for it in g]
    n, n_l, n_g, n_c = len(flat), len(lands), len(groups), len(carry)
    n_in = n + n_l + n_c

    def body(*refs):
        srcs, land_refs, sems = refs[:n], refs[n:n + n_l], refs[n_in:n_in + 2 * n_g]
        for out, there, _, s_sem, r_sem, peer in _push_copies(groups, srcs, land_refs, sems):
            pltpu.make_async_remote_copy(src_ref=out, dst_ref=there, send_sem=s_sem, recv_sem=r_sem,
                                         device_id=peer, device_id_type=MESH).start()

    arrays = [it[2] for it in flat] + list(lands) + list(carry)
    sem_shapes = []
    for g in groups:
        sem_shapes += [pltpu.SemaphoreType.DMA((len(g) * (N_DEV - 1),))] * 2
    outs = pl.pallas_call(
        body,
        name=name,
        in_specs=[HBM] * n_in,
        out_specs=[SEM] * (2 * n_g) + [HBM] * n_in,
        out_shape=sem_shapes + [pltpu.HBM(a.shape, a.dtype) for a in arrays],
        input_output_aliases={i: 2 * n_g + i for i in range(n_in)},
        compiler_params=pltpu.CompilerParams(has_side_effects=EFFECT),
    )(*[_hbm(a) for a in arrays])
    sems, thru = list(outs[:2 * n_g]), list(outs[2 * n_g:])
    return sems, thru[:n], thru[n:n + n_l], thru[n + n_l:]


def push_wait(groups, lands, sems, after, name):
    flat = [it for g in groups for it in g]
    n, n_l, n_g = len(flat), len(lands), len(groups)

    def body(*refs):
        srcs, land_refs, sem_refs = refs[:n], refs[n:n + n_l], refs[n + n_l:n + n_l + 2 * n_g]
        for out, _, here, s_sem, r_sem, peer in _push_copies(groups, srcs, land_refs, sem_refs):
            cp = pltpu.make_async_remote_copy(src_ref=out, dst_ref=here, send_sem=s_sem, recv_sem=r_sem,
                                              device_id=peer, device_id_type=MESH)
            cp.wait_send()
            cp.wait_recv()

    arrays = [it[2] for it in flat] + list(lands)
    outs = pl.pallas_call(
        body,
        name=name,
        in_specs=[HBM] * (n + n_l) + [SEM] * (2 * n_g) + [ANY],
        out_specs=[HBM] * (n + n_l),
        out_shape=[pltpu.HBM(a.shape, a.dtype) for a in arrays],
        input_output_aliases={i: i for i in range(n + n_l)},
        compiler_params=pltpu.CompilerParams(has_side_effects=EFFECT),
    )(*arrays, *sems, after)
    return list(outs[:n]), list(outs[n:])


def sum_slabs(parts, name):
    n, rows, cols = parts.shape

    def body(p_ref, o_ref):
        g = p_ref[0]
        for s in range(1, n):
            g = g + p_ref[s]
        o_ref[...] = g

    return pl.pallas_call(body, out_shape=jax.ShapeDtypeStruct((rows, cols), F32), name=name)(parts)


def _row_tile(rows, cols):
    if rows * cols * 4 <= (1 << 20) or rows % 8:
        return rows
    tr = rows
    while tr % 2 == 0 and (tr // 2) % 8 == 0 and tr * cols * 4 > (1 << 20):
        tr //= 2
    return tr


def adamw(parts, w, m, v, name):
    p_n = parts.shape[0]
    rows, cols = w.shape
    tr = _row_tile(rows, cols)

    def body(p_ref, w_ref, m_ref, v_ref, g_ref, d_ref, nm_ref, nv_ref):
        g = p_ref[0].astype(F32)
        for s in range(1, p_n):
            g = g + p_ref[s].astype(F32)
        m_new = ADAM_B1 * m_ref[...] + (1.0 - ADAM_B1) * g
        v_new = ADAM_B2 * v_ref[...] + (1.0 - ADAM_B2) * (g * g)
        m_hat = m_new / (1.0 - ADAM_B1 ** ADAM_STEP)
        v_hat = v_new / (1.0 - ADAM_B2 ** ADAM_STEP)
        g_ref[...] = g
        d_ref[...] = -ADAM_LR * (m_hat / (jnp.sqrt(v_hat) + ADAM_EPS) + ADAM_WD * w_ref[...])
        nm_ref[...] = m_new
        nv_ref[...] = v_new

    blk = pl.BlockSpec((tr, cols), lambda i: (i, 0))
    return pl.pallas_call(
        body,
        grid=(rows // tr,),
        in_specs=[pl.BlockSpec((p_n, tr, cols), lambda i: (0, i, 0)), blk, blk, blk],
        out_specs=[blk] * 4,
        out_shape=[jax.ShapeDtypeStruct((rows, cols), F32)] * 4,
        compiler_params=_cp("parallel"),
        name=name,
    )(parts, w, m, v)


def _adamw_nd(parts, w, m, v, name):
    shp = w.shape
    cols = shp[-1]
    rows = math.prod(shp[:-1])
    outs = adamw(parts.reshape(parts.shape[0], rows, cols), w.reshape(rows, cols), m.reshape(rows, cols),
                 v.reshape(rows, cols), name)
    return [o.reshape(shp) for o in outs]


REPL = ["ffn1_norm", "mix_norm", "ffn2_norm", "hyb_dw_b", "hyb_ln_g", "hyb_ln_b", "gdn_A_log", "gdn_dt_bias",
        "gdn_norm_g", "final_norm"]
WEIGHTS = ["ffn1_norm", "ffn1_w_in", "ffn1_w_out", "mix_norm", "ffn2_norm", "ffn2_w_in", "ffn2_w_out", "hyb_w_in",
           "hyb_dw_w", "hyb_dw_b", "hyb_ln_g", "hyb_ln_b", "hyb_w_out", "gdn_w_in", "gdn_conv_w", "gdn_A_log",
           "gdn_dt_bias", "gdn_norm_g", "gdn_w_out", "final_norm"]


def _pack(arrs, rows):
    flat = jnp.concatenate([a.reshape(-1) for a in arrs])
    return jnp.pad(flat, (0, rows * 128 - flat.shape[0])).reshape(rows, 128)


def _slabs_to_cols(a):
    return jnp.moveaxis(a, 0, -2).reshape(a.shape[1:-1] + (N_DEV * a.shape[-1],))


def kernel(x, positions, ffn1_norm, ffn1_w_in, ffn1_w_out, mix_norm, ffn2_norm, ffn2_w_in, ffn2_w_out, hyb_w_in, hyb_dw_w, hyb_dw_b, hyb_ln_g, hyb_ln_b, hyb_w_out, gdn_w_in, gdn_conv_w, gdn_A_log, gdn_dt_bias, gdn_norm_g, gdn_w_out, final_norm, loss_target, m_ffn1_norm, m_ffn1_w_in, m_ffn1_w_out, m_mix_norm, m_ffn2_norm, m_ffn2_w_in, m_ffn2_w_out, m_hyb_w_in, m_hyb_dw_w, m_hyb_dw_b, m_hyb_ln_g, m_hyb_ln_b, m_hyb_w_out, m_gdn_w_in, m_gdn_conv_w, m_gdn_A_log, m_gdn_dt_bias, m_gdn_norm_g, m_gdn_w_out, m_final_norm, v_ffn1_norm, v_ffn1_w_in, v_ffn1_w_out, v_mix_norm, v_ffn2_norm, v_ffn2_w_in, v_ffn2_w_out, v_hyb_w_in, v_hyb_dw_w, v_hyb_dw_b, v_hyb_ln_g, v_hyb_ln_b, v_hyb_w_out, v_gdn_w_in, v_gdn_conv_w, v_gdn_A_log, v_gdn_dt_bias, v_gdn_norm_g, v_gdn_w_out, v_final_norm):
    w = dict(ffn1_norm=ffn1_norm, ffn1_w_in=ffn1_w_in, ffn1_w_out=ffn1_w_out, mix_norm=mix_norm, ffn2_norm=ffn2_norm,
             ffn2_w_in=ffn2_w_in, ffn2_w_out=ffn2_w_out, hyb_w_in=hyb_w_in, hyb_dw_w=hyb_dw_w, hyb_dw_b=hyb_dw_b,
             hyb_ln_g=hyb_ln_g, hyb_ln_b=hyb_ln_b, hyb_w_out=hyb_w_out, gdn_w_in=gdn_w_in, gdn_conv_w=gdn_conv_w,
             gdn_A_log=gdn_A_log, gdn_dt_bias=gdn_dt_bias, gdn_norm_g=gdn_norm_g, gdn_w_out=gdn_w_out,
             final_norm=final_norm)
    mom = dict(ffn1_norm=m_ffn1_norm, ffn1_w_in=m_ffn1_w_in, ffn1_w_out=m_ffn1_w_out, mix_norm=m_mix_norm,
               ffn2_norm=m_ffn2_norm, ffn2_w_in=m_ffn2_w_in, ffn2_w_out=m_ffn2_w_out, hyb_w_in=m_hyb_w_in,
               hyb_dw_w=m_hyb_dw_w, hyb_dw_b=m_hyb_dw_b, hyb_ln_g=m_hyb_ln_g, hyb_ln_b=m_hyb_ln_b,
               hyb_w_out=m_hyb_w_out, gdn_w_in=m_gdn_w_in, gdn_conv_w=m_gdn_conv_w, gdn_A_log=m_gdn_A_log,
               gdn_dt_bias=m_gdn_dt_bias, gdn_norm_g=m_gdn_norm_g, gdn_w_out=m_gdn_w_out, final_norm=m_final_norm)
    var = dict(ffn1_norm=v_ffn1_norm, ffn1_w_in=v_ffn1_w_in, ffn1_w_out=v_ffn1_w_out, mix_norm=v_mix_norm,
               ffn2_norm=v_ffn2_norm, ffn2_w_in=v_ffn2_w_in, ffn2_w_out=v_ffn2_w_out, hyb_w_in=v_hyb_w_in,
               hyb_dw_w=v_hyb_dw_w, hyb_dw_b=v_hyb_dw_b, hyb_ln_g=v_hyb_ln_g, hyb_ln_b=v_hyb_ln_b,
               hyb_w_out=v_hyb_w_out, gdn_w_in=v_gdn_w_in, gdn_conv_w=v_gdn_conv_w, gdn_A_log=v_gdn_A_log,
               gdn_dt_bias=v_gdn_dt_bias, gdn_norm_g=v_gdn_norm_g, gdn_w_out=v_gdn_w_out, final_norm=v_final_norm)
    xi, yi, ci = _coords()
    me = 4 * xi + 2 * yi + ci

    big = ["ffn1_w_in", "ffn1_w_out", "ffn2_w_in", "ffn2_w_out", "hyb_w_in", "hyb_w_out", "gdn_w_in", "gdn_w_out"]
    ag_groups, ag_lands = [], []

    def add_group(shards):
        group = []
        for s in shards:
            land = lax.dynamic_update_slice(lax.empty((N_DEV,) + s.shape, s.dtype), s[None], (me,) + (0,) * s.ndim)
            group.append(("gather", None, s, len(ag_lands)))
            ag_lands.append(land)
        ag_groups.append(group)

    first = all_gather([ffn1_w_in[0].astype(BF16), ffn1_w_out[0].astype(BF16)], "weights_gather_first")
    for l in range(DEPTH):
        i = l // 2
        if l == 0:
            ag_groups.append([])
        else:
            add_group([ffn1_w_in[l].astype(BF16), ffn1_w_out[l].astype(BF16)])
        if l % 2 == 0:
            add_group([hyb_w_in[i].astype(BF16), hyb_w_out[i].astype(BF16), hyb_dw_w[i]])
        else:
            add_group([gdn_w_in[i].astype(BF16), gdn_w_out[i].astype(BF16), gdn_conv_w[i]])
        add_group([ffn2_w_in[l].astype(BF16), ffn2_w_out[l].astype(BF16)])
    ag_sems, ag_srcs, ag_lands, first = push_start(ag_groups[1:], ag_lands, "weights_gather_start", carry=first)
    ag_sems = [None, None] + ag_sems

    def fetch(gi, after):
        if gi == 0:
            return first
        group = ag_groups[gi]
        base = sum(len(g) for g in ag_groups[:gi])
        items = [(kind, layer, ag_srcs[base + t], t) for t, (kind, layer, _, _) in enumerate(group)]
        lands = [ag_lands[li] for _, _, _, li in group]
        return push_wait([items], lands, ag_sems[2 * gi:2 * gi + 2], after, f"weights_gather_wait_{gi}")[1]

    row = lambda a: a.reshape(1, -1)

    rope = make_rope(positions)
    h = x[0]
    saved = []
    for l in range(DEPTH):
        i = l // 2
        rec = {"h1": h}
        wi, wo = fetch(3 * l, h)
        rec["w1"] = (wi.reshape(2, FFN_TILES, D, FFN_SHARD), wo)
        h, rec["hn1"], rec["a1"], rec["b1"] = ffn_fwd(h, row(ffn1_norm[l]), *rec["w1"], l, "1")
        mi, mo, mc = fetch(3 * l + 1, h)
        if l % 2 == 0:
            rec["wm"] = (_slabs_to_cols(mi), jnp.pad(_slabs_to_cols(mc), ((0, 1), (0, 0))), mo.reshape(D, D))
            w_in_f, dw_f, w_out_f = rec["wm"]
            h, rec["mix"] = hybrid_fwd(h, row(mix_norm[l]), w_in_f, dw_f, row(hyb_dw_b[i]), row(hyb_ln_g[i]),
                                       row(hyb_ln_b[i]), w_out_f, rope, str(i))
        else:
            rec["wm"] = (_slabs_to_cols(mi), _slabs_to_cols(mc), mo.reshape(D, D))
            w_in_f, cw_f, w_out_f = rec["wm"]
            h, rec["mix"] = gdn_fwd(h, row(mix_norm[l]), w_in_f, cw_f, row(gdn_A_log[i]), row(gdn_dt_bias[i]),
                                    row(gdn_norm_g[i]), w_out_f, str(i))
        rec["h2"] = h
        wi, wo = fetch(3 * l + 2, h)
        rec["w2"] = (wi.reshape(2, FFN_TILES, D, FFN_SHARD), wo)
        h, rec["hn2"], rec["a2"], rec["b2"] = ffn_fwd(h, row(ffn2_norm[l]), *rec["w2"], l, "2")
        saved.append(rec)
    dres, d_final, loss_acc = final_loss(h, row(final_norm), loss_target[0])
    loss = lax.psum(loss_acc[0, 0], ("x", "y", "c"))

    ge_land = {n: lax.empty((N_DEV,) + w[n].shape, BF16) for n in big}
    ge_pending = []

    def send(named, layer, tag, carry):
        lands = [ge_land[n] for n, _ in named]
        group = [("scatter", layer, s, t) for t, (_, s) in enumerate(named)]
        sems, srcs, lands_out, carried = push_start([group], lands, f"grad_send_{tag}", carry=[carry])
        for (n, _), land in zip(named, lands_out):
            ge_land[n] = land
        ge_pending.append(([(n, layer, s) for (n, _), s in zip(named, srcs)], sems))
        return carried[0]

    gsmall = {n: [None] * (DEPTH if n in ("ffn1_norm", "mix_norm", "ffn2_norm") else 2) for n in REPL[:-1]}
    gsmall["hyb_dw_w"] = [None, None]
    gsmall["gdn_conv_w"] = [None, None]
    for l in reversed(range(DEPTH)):
        i = l // 2
        rec = saved[l]
        dhn, dwin, dwout = ffn_bwd(rec["hn2"], rec["a2"], rec["b2"], dres, *rec["w2"], l, "2")
        dhn = send([("ffn2_w_in", dwin.reshape(N_DEV, D, FFN_SHARD)),
                    ("ffn2_w_out", dwout.reshape(N_DEV, FFN_SHARD // 2, D))], l, f"ffn2_{l}", dhn)
        dres, dg = norm_bwd(rec["h2"], row(ffn2_norm[l]), dhn, dres, f"ffn2_norm_bwd_{l}")
        gsmall["ffn2_norm"][l] = dg
        if l % 2 == 0:
            w_in_f, dw_f, w_out_f = rec["wm"]
            dres, gr = hybrid_bwd(dres, rec["mix"], row(mix_norm[l]), w_in_f, dw_f, row(hyb_ln_g[i]),
                                  row(hyb_ln_b[i]), w_out_f, rope, str(i))
            dres = send([("hyb_w_in", gr["w_in"]), ("hyb_w_out", gr["w_out"].reshape(N_DEV, D // N_DEV, D))],
                        i, f"hyb_{i}", dres)
            for n in ("dw_w", "dw_b", "ln_g", "ln_b"):
                gsmall["hyb_" + n][i] = gr[n]
        else:
            w_in_f, cw_f, w_out_f = rec["wm"]
            dres, gr = gdn_bwd(dres, rec["mix"], row(mix_norm[l]), w_in_f, cw_f, row(gdn_A_log[i]),
                               row(gdn_dt_bias[i]), row(gdn_norm_g[i]), w_out_f, str(i))
            dres = send([("gdn_w_in", gr["w_in"]), ("gdn_w_out", gr["w_out"].reshape(N_DEV, D // N_DEV, D))],
                        i, f"gdn_{i}", dres)
            for n in ("conv_w", "A_log", "dt_bias", "norm_g"):
                gsmall["gdn_" + n][i] = gr[n]
        gsmall["mix_norm"][l] = gr["norm"]
        dhn, dwin, dwout = ffn_bwd(rec["hn1"], rec["a1"], rec["b1"], dres, *rec["w1"], l, "1")
        dhn = send([("ffn1_w_in", dwin.reshape(N_DEV, D, FFN_SHARD)),
                    ("ffn1_w_out", dwout.reshape(N_DEV, FFN_SHARD // 2, D))], l, f"ffn1_{l}", dhn)
        dres, dg = norm_bwd(rec["h1"], row(ffn1_norm[l]), dhn, dres, f"ffn1_norm_bwd_{l}")
        gsmall["ffn1_norm"][l] = dg
    grad_x = dres[None]

    n_repl_rows = 136
    small_rows = 576
    repl_flat = jnp.concatenate([jnp.concatenate([a.reshape(-1) for a in gsmall[n]]) for n in REPL[:-1]]
                                + [d_final.reshape(-1)])
    repl_pack = jnp.pad(repl_flat, (0, n_repl_rows * 128 - repl_flat.shape[0]))
    small_pack = jnp.concatenate([repl_pack] + [a.reshape(-1) for a in gsmall["hyb_dw_w"]]
                                 + [a.reshape(-1) for a in gsmall["gdn_conv_w"]]).reshape(small_rows, 128)

    own = {n: {} for n in big}

    def wait_for(pending, names, after, name):
        groups = [[("scatter", layer, s, names.index(n)) for n, layer, s in named] for named, _ in pending]
        sems = [s for _, pair in pending for s in pair]
        srcs_out, lands_out = push_wait(groups, [ge_land[n] for n in names], sems, after, name)
        flat_named = [it for named, _ in pending for it in named]
        for (n, layer, _), s in zip(flat_named, srcs_out):
            own[n][layer] = lax.dynamic_index_in_dim(s, me, 0, keepdims=False)
        for n, land in zip(names, lands_out):
            ge_land[n] = land

    def with_own(n, land):
        mine = jnp.stack([own[n][k] for k in range(len(own[n]))])
        return lax.dynamic_update_slice(land, mine[None], (me,) + (0,) * mine.ndim)

    out = {}
    last = ["ffn1_w_in", "ffn1_w_out"]
    wait_for(ge_pending[:-1], big, dres, "grad_wait_a")
    for n in big:
        if n not in last:
            out[n] = _adamw_nd(with_own(n, ge_land[n]), w[n], mom[n], var[n], f"adamw_{n}")
    pin = sum(out[n][1].reshape(-1)[0] for n in big if n not in last) * 0.0
    small_all, = all_gather([small_pack + pin], "small_grads_all_gather")
    g_small = sum_slabs(small_all, "small_grads_sum")
    wait_for(ge_pending[-1:], last, g_small, "grad_wait_b")
    for n in last:
        out[n] = _adamw_nd(with_own(n, ge_land[n]), w[n], mom[n], var[n], f"adamw_{n}")

    pk = lambda d: _pack([d[n] for n in REPL], n_repl_rows)
    res = adamw(g_small[:n_repl_rows][None], pk(w), pk(mom), pk(var), "adamw_replicated")
    off = 0
    for n in REPL:
        sz = w[n].size
        out[n] = [r.reshape(-1)[off:off + sz].reshape(w[n].shape) for r in res]
        off += sz
    g_dw = g_small[n_repl_rows:n_repl_rows + 248].reshape(2, CONV_K, CONV_C)
    g_dw = lax.dynamic_slice_in_dim(g_dw, me * (CONV_C // N_DEV), CONV_C // N_DEV, axis=2)
    out["hyb_dw_w"] = _adamw_nd(g_dw[None], w["hyb_dw_w"], mom["hyb_dw_w"], var["hyb_dw_w"], "adamw_hyb_dw_w")
    g_cw = g_small[n_repl_rows + 248:].reshape(2, G_CONV, G_QKV)
    g_cw = lax.dynamic_slice_in_dim(g_cw, me * (G_QKV // N_DEV), G_QKV // N_DEV, axis=2)
    out["gdn_conv_w"] = _adamw_nd(g_cw[None], w["gdn_conv_w"], mom["gdn_conv_w"], var["gdn_conv_w"], "adamw_gdn_conv_w")

    return (loss, grad_x, *[out[n][0] for n in WEIGHTS], *[out[n][1] for n in WEIGHTS],
            *[out[n][2] for n in WEIGHTS], *[out[n][3] for n in WEIGHTS])
```

```python
import functools
import math

import jax
import jax.numpy as jnp
import numpy as np
from jax import lax
from jax.experimental import pallas as pl
from jax.experimental.pallas import tpu as pltpu

F32 = jnp.float32
BF16 = jnp.bfloat16

N_DEV = 8
T = 4096
D = 1024
DEPTH = 4
FFN = 2816
FFN_SHARD = 2 * FFN // N_DEV
FFN_TILES = FFN // FFN_SHARD
EPS = 1e-6

A_HEADS = 8
A_HD = 64
A_W = 512
CONV_C = 512
CONV_K = 31
HYB_IN = 2560
ROPE_THETA = 500000.0
ROT = 16
DILATIONS = (1, 4, 16)
BLK = 128
KPAD = 2048

G_HEADS = 8
G_DK = 128
G_QKV = 3072
G_IN = 4112
G_CONV = 4
CH = 64
GRP = 512
CPG = GRP // CH
N_GRP = T // GRP

ADAM_LR = 0.001
ADAM_B1 = 0.9
ADAM_B2 = 0.999
ADAM_EPS = 1e-08
ADAM_WD = 0.01
ADAM_STEP = 10

VMEM_LIMIT = 56 * 1024 * 1024

HI = lax.Precision.HIGHEST


def _cp(*sem):
    return pltpu.CompilerParams(dimension_semantics=sem, vmem_limit_bytes=VMEM_LIMIT)


def _dot(a, b):
    return jnp.dot(a, b, preferred_element_type=F32)


def _dot_nt(a, b):
    return lax.dot_general(a, b, (((1,), (1,)), ((), ())), preferred_element_type=F32)


def _dot_tn(a, b):
    return lax.dot_general(a, b, (((0,), (0,)), ((), ())), preferred_element_type=F32)


def _sigmoid(x):
    return 1.0 / (1.0 + jnp.exp(-x))


def _dsilu(x, sig):
    return sig * (1.0 + x * (1.0 - sig))


def _rms(x, g):
    rstd = lax.rsqrt(jnp.mean(x * x, axis=-1, keepdims=True) + EPS)
    return x * rstd * g


FFN_TT = 512


def ffn_fwd(h, g_row, w_in, w_out, layer, tag=""):
    def body(h_ref, g_ref, win_ref, wout_ref, hnew_ref, hn_ref, a_ref, b_ref):
        x = h_ref[...]
        hn = _rms(x, g_ref[...]).astype(BF16)
        hn_ref[...] = hn
        acc = None
        for j in range(FFN_TILES):
            a = _dot(hn, win_ref[0, j])
            b = _dot(hn, win_ref[1, j])
            act = a * _sigmoid(a) * b
            a_ref[j] = a.astype(BF16)
            b_ref[j] = b.astype(BF16)
            part = _dot(act.astype(BF16), wout_ref[2 * j:2 * j + 2].reshape(FFN_SHARD, D))
            acc = part if acc is None else acc + part
        hnew_ref[...] = x + 0.5 * acc

    tt = FFN_TT
    resident = pl.Buffered(1)
    return pl.pallas_call(
        body,
        grid=(T // tt,),
        in_specs=[
            pl.BlockSpec((tt, D), lambda i: (i, 0)),
            pl.BlockSpec((1, D), lambda i: (0, 0)),
            pl.BlockSpec((2, FFN_TILES, D, FFN_SHARD), lambda i: (0, 0, 0, 0), pipeline_mode=resident),
            pl.BlockSpec((N_DEV, FFN_SHARD // 2, D), lambda i: (0, 0, 0), pipeline_mode=resident),
        ],
        out_specs=[
            pl.BlockSpec((tt, D), lambda i: (i, 0)),
            pl.BlockSpec((tt, D), lambda i: (i, 0)),
            pl.BlockSpec((FFN_TILES, tt, FFN_SHARD), lambda i: (0, i, 0)),
            pl.BlockSpec((FFN_TILES, tt, FFN_SHARD), lambda i: (0, i, 0)),
        ],
        out_shape=[
            jax.ShapeDtypeStruct((T, D), F32),
            jax.ShapeDtypeStruct((T, D), BF16),
            jax.ShapeDtypeStruct((FFN_TILES, T, FFN_SHARD), BF16),
            jax.ShapeDtypeStruct((FFN_TILES, T, FFN_SHARD), BF16),
        ],
        compiler_params=_cp("parallel"),
        name=f"ffn{tag}_fwd_{layer}",
    )(h, g_row, w_in, w_out)


def ffn_bwd(hn, a, b, dres, w_in, w_out, layer, tag=""):
    tt = FFN_TT
    nt = T // tt

    def body(hn_ref, a_ref, b_ref, dres_ref, win_ref, wout_ref, dhn_ref, dwin_ref, dwout_ref, gin_ref, gout_ref,
             do_s, act_s, da_s, db_s):
        i = pl.program_id(1)
        wo = wout_ref[...].reshape(FFN_SHARD, D)
        half = tt // 2
        for r0 in (0, half):
            rows = slice(r0, r0 + half)
            do_h = (0.5 * dres_ref[rows, :]).astype(BF16)
            do_s[rows, :] = do_h
            dact = _dot_nt(do_h, wo)
            a = a_ref[rows, :].astype(F32)
            b = b_ref[rows, :].astype(F32)
            sig = _sigmoid(a)
            s = a * sig
            da_h = (dact * b * _dsilu(a, sig)).astype(BF16)
            db_h = (dact * s).astype(BF16)
            act_s[rows, :] = (s * b).astype(BF16)
            da_s[rows, :] = da_h
            db_s[rows, :] = db_h
            dhn_ref[rows, :] = (_dot_nt(da_h, win_ref[0]) + _dot_nt(db_h, win_ref[1])).astype(BF16)
        do, act, da, db = do_s[...], act_s[...], da_s[...], db_s[...]
        hn = hn_ref[...]
        gwo = _dot_tn(act, do)
        gwg = _dot_tn(hn, da)
        gwu = _dot_tn(hn, db)

        @pl.when(i == 0)
        def _():
            gout_ref[...] = gwo
            gin_ref[0] = gwg
            gin_ref[1] = gwu

        @pl.when(i > 0)
        def _():
            gout_ref[...] += gwo
            gin_ref[0] += gwg
            gin_ref[1] += gwu

        @pl.when(i == nt - 1)
        def _():
            dwin_ref[...] = gin_ref[...].astype(BF16)
            dwout_ref[...] = gout_ref[...].astype(BF16)

    return pl.pallas_call(
        body,
        grid=(FFN_TILES, nt),
        in_specs=[
            pl.BlockSpec((tt, D), lambda j, i: (i, 0)),
            pl.BlockSpec((None, tt, FFN_SHARD), lambda j, i: (j, i, 0)),
            pl.BlockSpec((None, tt, FFN_SHARD), lambda j, i: (j, i, 0)),
            pl.BlockSpec((tt, D), lambda j, i: (i, 0)),
            pl.BlockSpec((2, None, D, FFN_SHARD), lambda j, i: (0, j, 0, 0)),
            pl.BlockSpec((2, FFN_SHARD // 2, D), lambda j, i: (j, 0, 0)),
        ],
        out_specs=[
            pl.BlockSpec((None, tt, D), lambda j, i: (j, i, 0)),
            pl.BlockSpec((2, None, D, FFN_SHARD), lambda j, i: (0, j, 0, 0)),
            pl.BlockSpec((None, FFN_SHARD, D), lambda j, i: (j, 0, 0)),
        ],
        out_shape=[
            jax.ShapeDtypeStruct((FFN_TILES, T, D), BF16),
            jax.ShapeDtypeStruct((2, FFN_TILES, D, FFN_SHARD), BF16),
            jax.ShapeDtypeStruct((FFN_TILES, FFN_SHARD, D), BF16),
        ],
        scratch_shapes=[pltpu.VMEM((2, D, FFN_SHARD), F32), pltpu.VMEM((FFN_SHARD, D), F32),
                        pltpu.VMEM((tt, D), BF16),
                        pltpu.VMEM((tt, FFN_SHARD), BF16), pltpu.VMEM((tt, FFN_SHARD), BF16),
                        pltpu.VMEM((tt, FFN_SHARD), BF16)],
        compiler_params=_cp("parallel", "arbitrary"),
        name=f"ffn{tag}_bwd_{layer}",
    )(hn, a, b, dres, w_in, w_out)


def _rms_bwd(x, g, dy):
    rstd = lax.rsqrt(jnp.mean(x * x, axis=-1, keepdims=True) + EPS)
    xh = x * rstd
    u = dy * g
    dx = rstd * (u - xh * jnp.mean(u * xh, axis=-1, keepdims=True))
    return dx, jnp.sum(dy * xh, axis=0, keepdims=True)


def norm_bwd(x, g_row, dy_parts, dres, name):
    p = dy_parts.shape[0]
    tt = 512

    def body(x_ref, g_ref, dy_ref, dres_ref, out_ref, dg_ref):
        i = pl.program_id(0)
        dy = dy_ref[0].astype(F32)
        for q in range(1, p):
            dy = dy + dy_ref[q].astype(F32)
        dx, dg = _rms_bwd(x_ref[...], g_ref[...], dy)
        out_ref[...] = dres_ref[...] + dx

        @pl.when(i == 0)
        def _():
            dg_ref[...] = dg

        @pl.when(i > 0)
        def _():
            dg_ref[...] += dg

    return pl.pallas_call(
        body,
        grid=(T // tt,),
        in_specs=[
            pl.BlockSpec((tt, D), lambda i: (i, 0)),
            pl.BlockSpec((1, D), lambda i: (0, 0)),
            pl.BlockSpec((p, tt, D), lambda i: (0, i, 0)),
            pl.BlockSpec((tt, D), lambda i: (i, 0)),
        ],
        out_specs=[pl.BlockSpec((tt, D), lambda i: (i, 0)), pl.BlockSpec((1, D), lambda i: (0, 0))],
        out_shape=[jax.ShapeDtypeStruct((T, D), F32), jax.ShapeDtypeStruct((1, D), F32)],
        compiler_params=_cp("arbitrary"),
        name=name,
    )(x, g_row, dy_parts, dres)


def final_loss(h, g_row, target):
    tt = 512

    def body(h_ref, g_ref, t_ref, dres_ref, dg_ref, loss_ref):
        i = pl.program_id(0)
        x = h_ref[...]
        g = g_ref[...]
        err = _rms(x, g) - t_ref[...]
        part = 0.5 * jnp.sum(jnp.mean(err * err, axis=-1, keepdims=True), axis=0, keepdims=True)
        dx, dg = _rms_bwd(x, g, err * (1.0 / D))
        dres_ref[...] = dx
        part = jnp.broadcast_to(part, loss_ref.shape)

        @pl.when(i == 0)
        def _():
            dg_ref[...] = dg
            loss_ref[...] = part

        @pl.when(i > 0)
        def _():
            dg_ref[...] += dg
            loss_ref[...] += part

    return pl.pallas_call(
        body,
        grid=(T // tt,),
        in_specs=[
            pl.BlockSpec((tt, D), lambda i: (i, 0)),
            pl.BlockSpec((1, D), lambda i: (0, 0)),
            pl.BlockSpec((tt, D), lambda i: (i, 0)),
        ],
        out_specs=[
            pl.BlockSpec((tt, D), lambda i: (i, 0)),
            pl.BlockSpec((1, D), lambda i: (0, 0)),
            pl.BlockSpec((8, 128), lambda i: (0, 0)),
        ],
        out_shape=[
            jax.ShapeDtypeStruct((T, D), F32),
            jax.ShapeDtypeStruct((1, D), F32),
            jax.ShapeDtypeStruct((8, 128), F32),
        ],
        compiler_params=_cp("arbitrary"),
        name="final_loss",
    )(h, g_row, target)


PROJ_TT = 256


def rope_tables(pos_col, invf_row):
    tt = 512

    def body(p_ref, f_ref, c_ref, sm_ref, sp_ref):
        ang = p_ref[...].astype(F32) * f_ref[...]
        lane = lax.broadcasted_iota(jnp.int32, ang.shape, 1) % A_HD
        cs = jnp.cos(ang)
        sn = jnp.sin(ang)
        c_ref[...] = jnp.where(lane < ROT, cs, 1.0)
        sm_ref[...] = jnp.where(lane < ROT // 2, -sn, 0.0)
        sp_ref[...] = jnp.where((lane >= ROT // 2) & (lane < ROT), sn, 0.0)

    spec = pl.BlockSpec((tt, 128), lambda i: (i, 0))
    return pl.pallas_call(
        body,
        grid=(T // tt,),
        in_specs=[pl.BlockSpec((tt, 1), lambda i: (i, 0)), pl.BlockSpec((1, 128), lambda i: (0, 0))],
        out_specs=[spec, spec, spec],
        out_shape=[jax.ShapeDtypeStruct((T, 128), F32)] * 3,
        compiler_params=_cp("parallel"),
        name="rope_tables",
    )(pos_col, invf_row)


def make_rope(positions):
    inv_freq = jnp.power(jnp.float32(ROPE_THETA), -jnp.arange(0, ROT, 2, dtype=F32) / ROT)
    per_head = jnp.concatenate([inv_freq, inv_freq, jnp.zeros((A_HD - ROT,), F32)])
    invf_row = jnp.tile(per_head, 2)[None, :]
    return tuple(rope_tables(positions.reshape(T, 1), invf_row))


def _rope(x, c, sm, sp):
    return x * c + pltpu.roll(x, 128 - ROT // 2, 1) * sm + pltpu.roll(x, ROT // 2, 1) * sp


def _rope_t(dy, c, sm, sp):
    return dy * c + pltpu.roll(dy * sm, ROT // 2, 1) + pltpu.roll(dy * sp, 128 - ROT // 2, 1)


def proj_fwd(h, g_row, w, splits, name, rope=None):
    tt = PROJ_TT
    n = w.shape[1]
    n_rope = 0 if rope is None else 3

    def body(h_ref, g_ref, w_ref, *rest):
        tabs = rest[:n_rope]
        hn_ref = rest[n_rope]
        outs = rest[n_rope + 1:]
        hn = _rms(h_ref[...], g_ref[...]).astype(BF16)
        hn_ref[...] = hn
        for k, ((st, wd), o_ref) in enumerate(zip(splits, outs)):
            if rope is not None and k == 0:
                c, sm, sp = (t[...] for t in tabs)
                for gi in range(wd // 128):
                    r = _dot(hn, w_ref[:, st + 128 * gi:st + 128 * (gi + 1)])
                    if gi < 2 * A_W // 128:
                        r = _rope(r, c, sm, sp)
                    o_ref[:, 128 * gi:128 * (gi + 1)] = r
            else:
                o_ref[...] = _dot(hn, w_ref[:, st:st + wd])

    tab_specs = [pl.BlockSpec((tt, 128), lambda i: (i, 0))] * n_rope
    return pl.pallas_call(
        body,
        grid=(T // tt,),
        in_specs=[
            pl.BlockSpec((tt, D), lambda i: (i, 0)),
            pl.BlockSpec((1, D), lambda i: (0, 0)),
            pl.BlockSpec((D, n), lambda i: (0, 0)),
        ] + tab_specs,
        out_specs=[pl.BlockSpec((tt, D), lambda i: (i, 0))]
        + [pl.BlockSpec((tt, wd), lambda i: (i, 0)) for _, wd in splits],
        out_shape=[jax.ShapeDtypeStruct((T, D), BF16)]
        + [jax.ShapeDtypeStruct((T, wd), F32) for _, wd in splits],
        compiler_params=_cp("parallel"),
        name=name,
    )(h, g_row, w, *(rope or ()))


def proj_bwd_data(x, g_row, w, dparts, splits, dres, name, rope=None, n_rot=0):
    tt = PROJ_TT
    n = w.shape[1]
    n_rope = 0 if rope is None else 3
    k_parts = len(dparts)
    n_main = sum(wd for _, wd in splits if wd % 128 == 0)

    def body(x_ref, g_ref, w_ref, dres_ref, *rest):
        d_refs = rest[:k_parts]
        tabs = rest[k_parts:k_parts + n_rope]
        out_ref, dg_ref, dproj_ref = rest[k_parts + n_rope:k_parts + n_rope + 3]
        i = pl.program_id(0)
        dhn = jnp.zeros((tt, D), F32)
        for k, ((st, wd), d_ref) in enumerate(zip(splits, d_refs)):
            if k < n_rot:
                c, sm, sp = (t[...] for t in tabs)
                for gi in range(wd // 128):
                    cols = slice(st + 128 * gi, st + 128 * (gi + 1))
                    d = _rope_t(d_ref[:, 128 * gi:128 * (gi + 1)], c, sm, sp).astype(BF16)
                    dproj_ref[:, cols] = d
                    dhn = dhn + _dot_nt(d, w_ref[:, cols])
            else:
                d = d_ref[...].astype(BF16)
                if wd % 128 == 0:
                    dproj_ref[:, st:st + wd] = d
                dhn = dhn + _dot_nt(d, w_ref[:, st:st + wd])
        dx, dg = _rms_bwd(x_ref[...], g_ref[...], dhn)
        out_ref[...] = dres_ref[...] + dx

        @pl.when(i == 0)
        def _():
            dg_ref[...] = dg

        @pl.when(i > 0)
        def _():
            dg_ref[...] += dg

    tab_specs = [pl.BlockSpec((tt, 128), lambda i: (i, 0))] * n_rope
    out_specs = [pl.BlockSpec((tt, D), lambda i: (i, 0)), pl.BlockSpec((1, D), lambda i: (0, 0))]
    out_shape = [jax.ShapeDtypeStruct((T, D), F32), jax.ShapeDtypeStruct((1, D), F32)]
    out_specs.append(pl.BlockSpec((tt, n_main), lambda i: (i, 0)))
    out_shape.append(jax.ShapeDtypeStruct((T, n_main), BF16))
    return pl.pallas_call(
        body,
        grid=(T // tt,),
        in_specs=[
            pl.BlockSpec((tt, D), lambda i: (i, 0)),
            pl.BlockSpec((1, D), lambda i: (0, 0)),
            pl.BlockSpec((D, n), lambda i: (0, 0)),
            pl.BlockSpec((tt, D), lambda i: (i, 0)),
        ] + [pl.BlockSpec((tt, wd), lambda i: (i, 0)) for _, wd in splits] + tab_specs,
        out_specs=out_specs,
        out_shape=out_shape,
        compiler_params=_cp("arbitrary"),
        name=name,
    )(x, g_row, w, dres, *dparts, *(rope or ()))


def mm_tn_full(x, d, out_dtype, name):
    k = x.shape[1]
    n = d.shape[1]
    wn = 512

    def body(x_ref, d_ref, o_ref):
        o_ref[...] = _dot_tn(x_ref[...], d_ref[...].astype(BF16)).astype(out_dtype)

    return pl.pallas_call(
        body,
        grid=(n // wn,),
        in_specs=[pl.BlockSpec((T, k), lambda j: (0, 0), pipeline_mode=pl.Buffered(1)),
                  pl.BlockSpec((T, wn), lambda j: (0, j))],
        out_specs=pl.BlockSpec((k, wn), lambda j: (0, j)),
        out_shape=jax.ShapeDtypeStruct((k, n), out_dtype),
        compiler_params=_cp("parallel"),
        name=name,
    )(x, d)


def cols_to_slabs(main, tail, name):
    nm = main.shape[1]
    n = nm + (0 if tail is None else tail.shape[1])
    w = n // N_DEV
    tr = 256

    def body(*refs):
        m_ref, o_ref = refs[0], refs[-1]
        for s in range(N_DEV):
            a, b = w * s, w * (s + 1)
            if b <= nm:
                o_ref[s] = m_ref[:, a:b].astype(BF16)
            else:
                o_ref[s, :, 0:nm - a] = m_ref[:, a:nm].astype(BF16)
                o_ref[s, :, nm - a:w] = refs[1][:, 0:b - nm].astype(BF16)

    arrays = [main] + ([] if tail is None else [tail])
    return pl.pallas_call(
        body,
        grid=(D // tr,),
        in_specs=[pl.BlockSpec((tr, a.shape[1]), lambda i: (i, 0)) for a in arrays],
        out_specs=pl.BlockSpec((N_DEV, tr, w), lambda i: (0, i, 0)),
        out_shape=jax.ShapeDtypeStruct((N_DEV, D, w), BF16),
        compiler_params=_cp("parallel"),
        name=name,
    )(*arrays)


def mm_tn(x, d, name):
    k = x.shape[1]
    n = d.shape[1]
    wn = n if n <= 512 else 512
    tt = 512

    def body(x_ref, d_ref, o_ref):
        i = pl.program_id(1)
        r = _dot_tn(x_ref[...], d_ref[...].astype(BF16))

        @pl.when(i == 0)
        def _():
            o_ref[...] = r

        @pl.when(i > 0)
        def _():
            o_ref[...] += r

    return pl.pallas_call(
        body,
        grid=(n // wn, T // tt),
        in_specs=[pl.BlockSpec((tt, k), lambda j, i: (i, 0)), pl.BlockSpec((tt, wn), lambda j, i: (i, j))],
        out_specs=pl.BlockSpec((k, wn), lambda j, i: (0, j)),
        out_shape=jax.ShapeDtypeStruct((k, n), F32),
        compiler_params=_cp("parallel", "arbitrary"),
        name=name,
    )(x, d)


CONV_RC = 128
CONV_PAD = 32


def hyb_conv_fwd(u, dw_w, dw_b, name):
    def body(ua_ref, ug_ref, w_ref, b_ref, o_ref, xpad):
        xpad[0:CONV_PAD, :] = jnp.zeros((CONV_PAD, 128), F32)
        xpad[CONV_PAD:, :] = ua_ref[...] * _sigmoid(ug_ref[...])
        for r in range(T // CONV_RC):
            acc = jnp.broadcast_to(b_ref[...], (CONV_RC, 128))
            for j in range(CONV_K):
                acc = acc + w_ref[pl.ds(j, 1), :] * xpad[pl.ds(r * CONV_RC + CONV_PAD - (CONV_K - 1) + j, CONV_RC), :]
            o_ref[r * CONV_RC:(r + 1) * CONV_RC, :] = acc

    nb = CONV_C // 128
    return pl.pallas_call(
        body,
        grid=(nb,),
        in_specs=[
            pl.BlockSpec((T, 128), lambda c: (0, c)),
            pl.BlockSpec((T, 128), lambda c: (0, nb + c)),
            pl.BlockSpec((32, 128), lambda c: (0, c)),
            pl.BlockSpec((1, 128), lambda c: (0, c)),
        ],
        out_specs=pl.BlockSpec((T, 128), lambda c: (0, c)),
        out_shape=jax.ShapeDtypeStruct((T, CONV_C), F32),
        scratch_shapes=[pltpu.VMEM((T + CONV_PAD, 128), F32)],
        compiler_params=_cp("parallel"),
        name=name,
    )(u, u, dw_w, dw_b)


def hyb_conv_bwd(dc, u, dw_w, name):
    def body(dc_ref, ua_ref, ug_ref, w_ref, da_ref, dgate_ref, dw_ref, db_ref, xpad, dcpad, dwacc):
        ua = ua_ref[...]
        sig = _sigmoid(ug_ref[...])
        xpad[0:CONV_PAD, :] = jnp.zeros((CONV_PAD, 128), F32)
        xpad[CONV_PAD:, :] = ua * sig
        dcpad[0:T, :] = dc_ref[...]
        dcpad[T:, :] = jnp.zeros((CONV_PAD, 128), F32)
        dwacc[...] = jnp.zeros_like(dwacc)
        dbacc = jnp.zeros((8, 128), F32)
        for r in range(T // CONV_RC):
            r0 = r * CONV_RC
            dcr = dc_ref[r0:r0 + CONV_RC, :]
            dbacc = dbacc + dcr.reshape(CONV_RC // 8, 8, 128).sum(axis=0)
            dglu = jnp.zeros((CONV_RC, 128), F32)
            for j in range(CONV_K):
                dglu = dglu + w_ref[pl.ds(j, 1), :] * dcpad[pl.ds(r0 + (CONV_K - 1) - j, CONV_RC), :]
                prod = dcr * xpad[pl.ds(r0 + CONV_PAD - (CONV_K - 1) + j, CONV_RC), :]
                dwacc[8 * j:8 * j + 8, :] += prod.reshape(CONV_RC // 8, 8, 128).sum(axis=0)
            sg = sig[r0:r0 + CONV_RC, :]
            da_ref[r0:r0 + CONV_RC, :] = dglu * sg
            dgate_ref[r0:r0 + CONV_RC, :] = dglu * ua[r0:r0 + CONV_RC, :] * sg * (1.0 - sg)
        for j in range(CONV_K):
            dw_ref[pl.ds(j, 1), :] = jnp.sum(dwacc[8 * j:8 * j + 8, :], axis=0, keepdims=True)
        dw_ref[pl.ds(CONV_K, 1), :] = jnp.zeros((1, 128), F32)
        db_ref[...] = jnp.sum(dbacc, axis=0, keepdims=True)

    nb = CONV_C // 128
    col = pl.BlockSpec((T, 128), lambda c: (0, c))
    return pl.pallas_call(
        body,
        grid=(nb,),
        in_specs=[col, col, pl.BlockSpec((T, 128), lambda c: (0, nb + c)), pl.BlockSpec((32, 128), lambda c: (0, c))],
        out_specs=[col, col, pl.BlockSpec((32, 128), lambda c: (0, c)), pl.BlockSpec((1, 128), lambda c: (0, c))],
        out_shape=[
            jax.ShapeDtypeStruct((T, CONV_C), F32),
            jax.ShapeDtypeStruct((T, CONV_C), F32),
            jax.ShapeDtypeStruct((32, CONV_C), F32),
            jax.ShapeDtypeStruct((1, CONV_C), F32),
        ],
        scratch_shapes=[
            pltpu.VMEM((T + CONV_PAD, 128), F32),
            pltpu.VMEM((T + CONV_PAD, 128), F32),
            pltpu.VMEM((8 * 32, 128), F32),
        ],
        compiler_params=_cp("parallel"),
        name=name,
    )(dc, u, u, dw_w)


ATT_SCALE = A_HD ** -0.5
N_BLK = T // BLK


def _att_masks():
    i = lax.broadcasted_iota(jnp.int32, (BLK, 2 * BLK), 0)
    j = lax.broadcasted_iota(jnp.int32, (BLK, 2 * BLK), 1)
    band = (j >= i) & (j <= i + BLK)
    i1 = lax.broadcasted_iota(jnp.int32, (BLK, BLK), 0)
    j1 = lax.broadcasted_iota(jnp.int32, (BLK, BLK), 1)
    return band, j1 <= i1


def _att_rows(d, t, first):
    if first:
        base = t
        return pl.ds(base, BLK, stride=d), pl.ds(base, BLK, stride=d)
    c = t % d
    n = t // d + 1
    base = c + (BLK * d) * n
    return pl.ds(base, BLK, stride=d), pl.ds(base - BLK * d, 2 * BLK, stride=d)


def _loop_pairs(n, block, per=2):
    def several(i, carry):
        for k in range(per):
            block(per * i + k, carry)
        return carry

    if n >= per:
        lax.fori_loop(0, n // per, several, 0)
    for t in range(n - n % per, n):
        block(t, 0)


def attn_fwd(qkv, name):
    def body(q_ref, k_ref, v_ref, o_ref, lse_ref, og, lg):
        band, tri = _att_masks()
        head0 = lax.broadcasted_iota(jnp.int32, (BLK, 128), 1) < A_HD
        for g, d in enumerate(DILATIONS):
            def block(t, carry, first, g=g, d=d):
                rq, rk = _att_rows(d, t, first)
                q2 = q_ref[rq, :]
                k2 = k_ref[rk, :].astype(BF16)
                v2 = v_ref[rk, :].astype(BF16)
                o_e, l_e = [], []
                for e in range(2):
                    qe = jnp.where(head0 if e == 0 else ~head0, q2, 0.0).astype(BF16)
                    s = _dot_nt(qe, k2) * ATT_SCALE
                    s = jnp.where(tri if first else band, s, -jnp.inf)
                    m = jnp.max(s, axis=-1, keepdims=True)
                    p = jnp.exp(s - m)
                    den = jnp.sum(p, axis=-1, keepdims=True)
                    o_e.append(_dot(p.astype(BF16), v2) / den)
                    l_e.append(m + jnp.log(den))
                og[g, rq, :] = jnp.where(head0, o_e[0], o_e[1])
                lg[g, rq, :] = jnp.where(head0, l_e[0], l_e[1])
                return carry

            _loop_pairs(d, functools.partial(block, first=True), per=4)
            _loop_pairs(N_BLK - d, functools.partial(block, first=False), per=4)
        rc = 256
        for r in range(T // rc):
            rows = pl.ds(r * rc, rc)
            l0, l1, l2 = lg[0, rows, :], lg[1, rows, :], lg[2, rows, :]
            m = jnp.maximum(jnp.maximum(l0, l1), l2)
            e0, e1, e2 = jnp.exp(l0 - m), jnp.exp(l1 - m), jnp.exp(l2 - m)
            z = e0 + e1 + e2
            o_ref[rows, :] = (e0 / z) * og[0, rows, :] + (e1 / z) * og[1, rows, :] + (e2 / z) * og[2, rows, :]
            lse_ref[rows, :] = m + jnp.log(z)

    npair = A_HEADS // 2
    col = lambda off: pl.BlockSpec((T, 128), lambda p: (0, off + p))
    return pl.pallas_call(
        body,
        grid=(npair,),
        in_specs=[col(0), col(npair), col(2 * npair)],
        out_specs=[col(0), col(0)],
        out_shape=[jax.ShapeDtypeStruct((T, A_W), F32), jax.ShapeDtypeStruct((T, A_W), F32)],
        scratch_shapes=[pltpu.VMEM((3, T, 128), F32), pltpu.VMEM((3, T, 128), F32)],
        compiler_params=_cp("parallel"),
        name=name,
    )(qkv, qkv, qkv)


def attn_bwd(qkv, o, lse, do, name):
    def body(q_ref, k_ref, v_ref, o_ref, lse_ref, do_ref, dq_ref, dk_ref, dv_ref):
        band, tri = _att_masks()
        head0 = lax.broadcasted_iota(jnp.int32, (BLK, 128), 1) < A_HD
        head0k = lax.broadcasted_iota(jnp.int32, (2 * BLK, 128), 1) < A_HD
        dq_ref[...] = jnp.zeros_like(dq_ref)
        dk_ref[...] = jnp.zeros_like(dk_ref)
        dv_ref[...] = jnp.zeros_like(dv_ref)
        for d in DILATIONS:
            def block(t, carry, first, d=d):
                rq, rk = _att_rows(d, t, first)
                q2 = q_ref[rq, :]
                k2 = k_ref[rk, :].astype(BF16)
                v2 = v_ref[rk, :].astype(BF16)
                do2 = do_ref[rq, :]
                l2 = lse_ref[rq, :]
                prod = do2 * o_ref[rq, :]
                q2b = q2.astype(BF16)
                do2b = do2.astype(BF16)
                dq_e, dk_e, dv_e = [], [], []
                for e in range(2):
                    he = head0 if e == 0 else ~head0
                    qe = jnp.where(he, q2, 0.0).astype(BF16)
                    doe = jnp.where(he, do2, 0.0).astype(BF16)
                    l = l2[:, A_HD * e:A_HD * e + 1]
                    dd = jnp.sum(jnp.where(he, prod, 0.0), axis=-1, keepdims=True)
                    s = _dot_nt(qe, k2) * ATT_SCALE
                    p = jnp.where(tri if first else band, jnp.exp(s - l), 0.0)
                    dp = _dot_nt(doe, v2)
                    ds = (p * (dp - dd) * ATT_SCALE).astype(BF16)
                    dq_e.append(_dot(ds, k2))
                    dk_e.append(_dot_tn(ds, q2b))
                    dv_e.append(_dot_tn(p.astype(BF16), do2b))
                hk = head0 if first else head0k
                dq_ref[rq, :] += jnp.where(head0, dq_e[0], dq_e[1])
                dk_ref[rk, :] += jnp.where(hk, dk_e[0], dk_e[1])
                dv_ref[rk, :] += jnp.where(hk, dv_e[0], dv_e[1])
                return carry

            _loop_pairs(d, functools.partial(block, first=True), per=4)
            _loop_pairs(N_BLK - d, functools.partial(block, first=False), per=4)

    npair = A_HEADS // 2
    col = lambda off: pl.BlockSpec((T, 128), lambda p: (0, off + p))
    return pl.pallas_call(
        body,
        grid=(npair,),
        in_specs=[col(0), col(npair), col(2 * npair), col(0), col(0), col(0)],
        out_specs=[col(0), col(0), col(0)],
        out_shape=[jax.ShapeDtypeStruct((T, A_W), F32)] * 3,
        compiler_params=_cp("parallel"),
        name=name,
    )(qkv, qkv, qkv, o, lse, do)


def _ln_silu(x, g, b):
    mu = jnp.mean(x, axis=-1, keepdims=True)
    xc = x - mu
    rstd = lax.rsqrt(jnp.mean(xc * xc, axis=-1, keepdims=True) + EPS)
    xh = xc * rstd
    y = xh * g + b
    sig = _sigmoid(y)
    return y * sig, (xh, rstd, y, sig)


def hyb_out_fwd(h, attn, cpre, ln_g, ln_b, w_out, name):
    tt = 512

    def body(h_ref, a_ref, c_ref, g_ref, b_ref, w_ref, hnew_ref, cat_ref):
        cn, _ = _ln_silu(c_ref[...], g_ref[...], b_ref[...])
        ab = a_ref[...].astype(BF16)
        cb = cn.astype(BF16)
        cat_ref[:, 0:A_W] = ab
        cat_ref[:, A_W:D] = cb
        hnew_ref[...] = h_ref[...] + _dot(ab, w_ref[0:A_W, :]) + _dot(cb, w_ref[A_W:D, :])

    half = pl.BlockSpec((tt, A_W), lambda i: (i, 0))
    vec = pl.BlockSpec((1, CONV_C), lambda i: (0, 0))
    full = pl.BlockSpec((tt, D), lambda i: (i, 0))
    return pl.pallas_call(
        body,
        grid=(T // tt,),
        in_specs=[full, half, half, vec, vec, pl.BlockSpec((D, D), lambda i: (0, 0))],
        out_specs=[full, full],
        out_shape=[jax.ShapeDtypeStruct((T, D), F32), jax.ShapeDtypeStruct((T, D), BF16)],
        compiler_params=_cp("parallel"),
        name=name,
    )(h, attn, cpre, ln_g, ln_b, w_out)


def hyb_out_bwd(dres, cpre, ln_g, ln_b, w_out, name):
    tt = 512

    def body(d_ref, c_ref, g_ref, b_ref, w_ref, da_ref, dc_ref, dg_ref, db_ref):
        i = pl.program_id(0)
        db16 = d_ref[...].astype(BF16)
        da_ref[...] = _dot_nt(db16, w_ref[0:A_W, :])
        dcn = _dot_nt(db16, w_ref[A_W:D, :])
        g = g_ref[...]
        _, (xh, rstd, y, sig) = _ln_silu(c_ref[...], g, b_ref[...])
        dy = dcn * _dsilu(y, sig)
        dxh = dy * g
        dc_ref[...] = rstd * (dxh - jnp.mean(dxh, axis=-1, keepdims=True)
                              - xh * jnp.mean(dxh * xh, axis=-1, keepdims=True))
        dg = jnp.sum(dy * xh, axis=0, keepdims=True)
        db = jnp.sum(dy, axis=0, keepdims=True)

        @pl.when(i == 0)
        def _():
            dg_ref[...] = dg
            db_ref[...] = db

        @pl.when(i > 0)
        def _():
            dg_ref[...] += dg
            db_ref[...] += db

    half = pl.BlockSpec((tt, A_W), lambda i: (i, 0))
    vec = pl.BlockSpec((1, CONV_C), lambda i: (0, 0))
    return pl.pallas_call(
        body,
        grid=(T // tt,),
        in_specs=[pl.BlockSpec((tt, D), lambda i: (i, 0)), half, vec, vec, pl.BlockSpec((D, D), lambda i: (0, 0))],
        out_specs=[half, half, vec, vec],
        out_shape=[
            jax.ShapeDtypeStruct((T, A_W), F32),
            jax.ShapeDtypeStruct((T, CONV_C), F32),
            jax.ShapeDtypeStruct((1, CONV_C), F32),
            jax.ShapeDtypeStruct((1, CONV_C), F32),
        ],
        compiler_params=_cp("arbitrary"),
        name=name,
    )(dres, cpre, ln_g, ln_b, w_out)


def hybrid_fwd(h, g_row, w_in, dw_w, dw_b, ln_g, ln_b, w_out, rope, tag):
    hn, qkv, u = proj_fwd(h, g_row, w_in, [(0, 3 * A_W), (3 * A_W, 2 * CONV_C)], f"hyb_proj_{tag}", rope=rope)
    cpre = hyb_conv_fwd(u, dw_w, dw_b, f"hyb_conv_{tag}")
    attn, lse = attn_fwd(qkv, f"attn_fwd_{tag}")
    hnew, cat = hyb_out_fwd(h, attn, cpre, ln_g, ln_b, w_out, f"hyb_out_{tag}")
    return hnew, (h, hn, qkv, u, cpre, attn, lse, cat)


def hybrid_bwd(dres, saved, g_row, w_in, dw_w, ln_g, ln_b, w_out, rope, tag):
    h, hn, qkv, u, cpre, attn, lse, cat = saved
    d_attn, d_cpre, d_lng, d_lnb = hyb_out_bwd(dres, cpre, ln_g, ln_b, w_out, f"hyb_out_bwd_{tag}")
    d_wout = mm_tn_full(cat, dres, BF16, f"hyb_wout_grad_{tag}")
    d_a, d_gate, d_dw, d_db = hyb_conv_bwd(d_cpre, u, dw_w, f"hyb_conv_bwd_{tag}")
    dq, dk, dv = attn_bwd(qkv, attn, lse, d_attn, f"attn_bwd_{tag}")
    splits = [(0, A_W), (A_W, A_W), (2 * A_W, A_W), (3 * A_W, CONV_C), (3 * A_W + CONV_C, CONV_C)]
    dres_new, d_norm, dproj = proj_bwd_data(
        h, g_row, w_in, [dq, dk, dv, d_a, d_gate], splits, dres, f"hyb_proj_bwd_{tag}", rope=rope, n_rot=2)
    d_win = cols_to_slabs(mm_tn_full(hn, dproj, F32, f"hyb_win_grad_{tag}"), None, f"hyb_win_slabs_{tag}")
    return dres_new, dict(norm=d_norm, w_in=d_win, dw_w=d_dw[:CONV_K], dw_b=d_db, ln_g=d_lng, ln_b=d_lnb, w_out=d_wout)


G_SCALE = G_DK ** -0.5
GP_RC = 256
GP_PAD = 8


def gdn_prep_fwd(x, conv_w, name):
    def body(x_ref, w_ref, o_ref, xpad):
        cb = pl.program_id(0)
        xpad[0:GP_PAD, :] = jnp.zeros((GP_PAD, 128), F32)
        xpad[GP_PAD:, :] = x_ref[...]
        for r in range(T // GP_RC):
            r0 = r * GP_RC
            y = jnp.zeros((GP_RC, 128), F32)
            for j in range(G_CONV):
                y = y + w_ref[pl.ds(j, 1), :] * xpad[pl.ds(r0 + GP_PAD - (G_CONV - 1) + j, GP_RC), :]
            s = y * _sigmoid(y)
            n = lax.rsqrt(jnp.sum(s * s, axis=-1, keepdims=True) + EPS)
            o_ref[r0:r0 + GP_RC, :] = s * jnp.where(cb < 2 * G_HEADS, n, 1.0)

    nb = G_QKV // 128
    return pl.pallas_call(
        body,
        grid=(nb,),
        in_specs=[pl.BlockSpec((T, 128), lambda c: (0, c)), pl.BlockSpec((G_CONV, 128), lambda c: (0, c))],
        out_specs=pl.BlockSpec((T, 128), lambda c: (0, c)),
        out_shape=jax.ShapeDtypeStruct((T, G_QKV), F32),
        scratch_shapes=[pltpu.VMEM((T + GP_PAD, 128), F32)],
        compiler_params=_cp("parallel"),
        name=name,
    )(x, conv_w)


def gdn_prep_bwd(dout, x, conv_w, part, l2, name):
    def body(d_ref, x_ref, w_ref, dx_ref, dw_ref, xpad, dypad, dwacc):
        xpad[0:GP_PAD, :] = jnp.zeros((GP_PAD, 128), F32)
        xpad[GP_PAD:, :] = x_ref[...]
        dypad[T:, :] = jnp.zeros((GP_PAD, 128), F32)
        dwacc[...] = jnp.zeros_like(dwacc)
        for r in range(T // GP_RC):
            r0 = r * GP_RC
            y = jnp.zeros((GP_RC, 128), F32)
            xs = []
            for j in range(G_CONV):
                xj = xpad[pl.ds(r0 + GP_PAD - (G_CONV - 1) + j, GP_RC), :]
                xs.append(xj)
                y = y + w_ref[pl.ds(j, 1), :] * xj
            sig = _sigmoid(y)
            s = y * sig
            d = d_ref[r0:r0 + GP_RC, :]
            if l2:
                n = lax.rsqrt(jnp.sum(s * s, axis=-1, keepdims=True) + EPS)
                out = s * n
                d = n * (d - out * jnp.sum(d * out, axis=-1, keepdims=True))
            dy = d * _dsilu(y, sig)
            dypad[r0:r0 + GP_RC, :] = dy
            for j in range(G_CONV):
                dwacc[8 * j:8 * j + 8, :] += (dy * xs[j]).reshape(GP_RC // 8, 8, 128).sum(axis=0)
        for r in range(T // GP_RC):
            r0 = r * GP_RC
            dx = jnp.zeros((GP_RC, 128), F32)
            for j in range(G_CONV):
                dx = dx + w_ref[pl.ds(j, 1), :] * dypad[pl.ds(r0 + (G_CONV - 1) - j, GP_RC), :]
            dx_ref[r0:r0 + GP_RC, :] = dx
        for j in range(G_CONV):
            dw_ref[pl.ds(j, 1), :] = jnp.sum(dwacc[8 * j:8 * j + 8, :], axis=0, keepdims=True)

    nb = G_HEADS
    off = part * nb
    col = pl.BlockSpec((T, 128), lambda c: (0, c))
    return pl.pallas_call(
        body,
        grid=(nb,),
        in_specs=[col, pl.BlockSpec((T, 128), lambda c: (0, off + c)), pl.BlockSpec((G_CONV, 128), lambda c: (0, off + c))],
        out_specs=[col, pl.BlockSpec((G_CONV, 128), lambda c: (0, c))],
        out_shape=[jax.ShapeDtypeStruct((T, G_HEADS * G_DK), F32), jax.ShapeDtypeStruct((G_CONV, G_HEADS * G_DK), F32)],
        scratch_shapes=[
            pltpu.VMEM((T + GP_PAD, 128), F32),
            pltpu.VMEM((T + GP_PAD, 128), F32),
            pltpu.VMEM((8 * G_CONV, 128), F32),
        ],
        compiler_params=_cp("parallel"),
        name=name,
    )(dout, x, conv_w)


def _seg_cumsum(x, reverse=False):
    row = lax.broadcasted_iota(jnp.int32, x.shape, 0) % CH
    s = 1
    while s < CH:
        if reverse:
            x = x + jnp.where(row < CH - s, pltpu.roll(x, x.shape[0] - s, 0), 0.0)
        else:
            x = x + jnp.where(row >= s, pltpu.roll(x, s, 0), 0.0)
        s *= 2
    return x


def _gdn_gates(ba_ref, alog_ref, dt_ref, h):
    ba = ba_ref[...]
    lane = lax.broadcasted_iota(jnp.int32, ba.shape, 1)
    b_col = jnp.sum(jnp.where(lane == h, ba, 0.0), axis=1, keepdims=True)
    a_col = jnp.sum(jnp.where(lane == G_HEADS + h, ba, 0.0), axis=1, keepdims=True)
    lane8 = lax.broadcasted_iota(jnp.int32, (1, G_HEADS), 1)
    alog = jnp.sum(jnp.where(lane8 == h, alog_ref[...], 0.0), axis=1, keepdims=True)
    dt = jnp.sum(jnp.where(lane8 == h, dt_ref[...], 0.0), axis=1, keepdims=True)
    beta = _sigmoid(b_col)
    xa = a_col + dt
    softplus = jnp.maximum(xa, 0.0) + jnp.log(1.0 + jnp.exp(-jnp.abs(xa)))
    ea = jnp.exp(alog)
    return beta, -ea * softplus, xa, ea


def _chunk_masks():
    i = lax.broadcasted_iota(jnp.int32, (CH, CH), 0)
    j = lax.broadcasted_iota(jnp.int32, (CH, CH), 1)
    return i >= j, i > j, i, j


def _decay(gcc, causal):
    gm = gcc[:, 0:CH]
    return jnp.where(causal, jnp.exp(jnp.minimum(gm - gm.T, 0.0)), 0.0)


def _split(a):
    hi = a.astype(BF16)
    return hi, (a - hi.astype(F32)).astype(BF16)


def _dot3(a, b):
    ah, al = _split(a)
    bh, bl = _split(b)
    return _dot(ah, bh) + (_dot(ah, bl) + _dot(al, bh))


def _unit_lower_inverse(lms, i, j):
    eye = jnp.where(i == j, 1.0, 0.0)
    ms = [None] * len(lms)
    b = 1
    while b < CH:
        pair = ((i // (2 * b)) == (j // (2 * b))) & ((i // b) % 2 == 1) & ((j // b) % 2 == 0)
        lbs = [jnp.where(pair, lm, 0.0) for lm in lms]
        if b == 1:
            ms = [eye - lb for lb in lbs]
        else:
            ts = [_dot3(m, lb) for m, lb in zip(ms, lbs)]
            ms = [m - _dot3(t, m) for m, t in zip(ms, ts)]
        b *= 2
    return ms


def gdn_local_fwd(qkv, ba, alog, dtb, name):
    def body(q_ref, k_ref, v_ref, ba_ref, al_ref, dt_ref, u_ref, w_ref, qd_ref, kd_ref, at_ref, el_ref, ti_ref, gcs):
        h = pl.program_id(1)
        beta, g, _, _ = _gdn_gates(ba_ref, al_ref, dt_ref, h)
        gc = _seg_cumsum(jnp.broadcast_to(g, (GRP, 128)))
        gcs[...] = gc
        causal, strict, i, j = _chunk_masks()
        lms = []
        for c in range(CPG):
            r = slice(c * CH, (c + 1) * CH)
            q, k = q_ref[r, :], k_ref[r, :]
            gcc = gc[r, :]
            ec = jnp.exp(gcc)
            gl = gcs[pl.ds(c * CH + CH - 1, 1), :]
            dm = _decay(gcc, causal)
            kbf = k.astype(BF16)
            a1 = _dot_nt((k * beta[r, :]).astype(BF16), kbf)
            lms.append(jnp.where(strict, a1 * dm, 0.0))
            qs = q * G_SCALE
            qd_ref[r, :] = (qs * ec).astype(BF16)
            kd_ref[r, :] = (k * jnp.exp(gl - gcc)).astype(BF16)
            at_ref[r, :] = (_dot_nt(qs.astype(BF16), kbf) * dm).astype(BF16)
            el_ref[pl.ds(c, 1), :] = jnp.exp(gl)
        tinvs = _unit_lower_inverse(lms, i, j)
        for c in range(CPG):
            r = slice(c * CH, (c + 1) * CH)
            bt = beta[r, :]
            tb = tinvs[c].astype(BF16)
            u_ref[r, :] = _dot(tb, (v_ref[r, :] * bt).astype(BF16))
            w_ref[r, :] = _dot(tb, (k_ref[r, :] * bt * jnp.exp(gc[r, :])).astype(BF16)).astype(BF16)
            ti_ref[r, :] = tinvs[c]

    hd = lambda off: pl.BlockSpec((GRP, 128), lambda i, h: (i, off + h))
    vec = pl.BlockSpec((1, G_HEADS), lambda i, h: (0, 0))
    sq = pl.BlockSpec((None, GRP, CH), lambda i, h: (h, i, 0))
    return pl.pallas_call(
        body,
        grid=(N_GRP, G_HEADS),
        in_specs=[hd(0), hd(G_HEADS), hd(2 * G_HEADS), pl.BlockSpec((GRP, 2 * G_HEADS), lambda i, h: (i, 0)), vec, vec],
        out_specs=[hd(0), hd(0), hd(0), hd(0), sq, pl.BlockSpec((None, CPG, 128), lambda i, h: (h, i, 0)), sq],
        out_shape=[
            jax.ShapeDtypeStruct((T, D), F32),
            jax.ShapeDtypeStruct((T, D), BF16),
            jax.ShapeDtypeStruct((T, D), BF16),
            jax.ShapeDtypeStruct((T, D), BF16),
            jax.ShapeDtypeStruct((G_HEADS, T, CH), BF16),
            jax.ShapeDtypeStruct((G_HEADS, T // CH, 128), F32),
            jax.ShapeDtypeStruct((G_HEADS, T, CH), F32),
        ],
        scratch_shapes=[pltpu.VMEM((GRP, 128), F32)],
        compiler_params=_cp("parallel", "parallel"),
        name=name,
    )(qkv, qkv, qkv, ba, alog, dtb)


def gdn_rec_fwd(u, w, qd, kd, at, el, name):
    def body(u_ref, w_ref, qd_ref, kd_ref, at_ref, el_ref, o_ref, vn_ref, st_ref, s_scr):
        @pl.when(pl.program_id(0) == 0)
        def _():
            s_scr[...] = jnp.zeros_like(s_scr)

        states = [s_scr[h] for h in range(G_HEADS)]
        for c in range(CPG):
            r = slice(c * CH, (c + 1) * CH)
            for h in range(G_HEADS):
                ln = slice(h * 128, (h + 1) * 128)
                s = states[h]
                st_ref[h, c] = s
                sb = s.astype(BF16)
                vn = (u_ref[r, ln] - _dot(w_ref[r, ln], sb)).astype(BF16)
                o_ref[r, ln] = _dot(qd_ref[r, ln], sb) + _dot(at_ref[h, r, :], vn)
                states[h] = s * el_ref[h, pl.ds(c, 1), :] + _dot_tn(kd_ref[r, ln], vn)
                vn_ref[r, ln] = vn
        for h in range(G_HEADS):
            s_scr[h] = states[h]

    row = pl.BlockSpec((GRP, D), lambda i: (i, 0))
    return pl.pallas_call(
        body,
        grid=(N_GRP,),
        in_specs=[row, row, row, row, pl.BlockSpec((G_HEADS, GRP, CH), lambda i: (0, i, 0)),
                  pl.BlockSpec((G_HEADS, CPG, 128), lambda i: (0, i, 0))],
        out_specs=[row, row, pl.BlockSpec((G_HEADS, CPG, 128, 128), lambda i: (0, i, 0, 0))],
        out_shape=[
            jax.ShapeDtypeStruct((T, D), F32),
            jax.ShapeDtypeStruct((T, D), BF16),
            jax.ShapeDtypeStruct((G_HEADS, T // CH, 128, 128), F32),
        ],
        scratch_shapes=[pltpu.VMEM((G_HEADS, 128, 128), F32)],
        compiler_params=_cp("arbitrary"),
        name=name,
    )(u, w, qd, kd, at, el)


def gdn_rec_bwd(do, w, qd, kd, at, el, vn, st, name):
    def body(do_ref, w_ref, qd_ref, kd_ref, at_ref, el_ref, vn_ref, st_ref,
             du_ref, dw_ref, dqd_ref, dkd_ref, dat_ref, del_ref, ds_scr):
        @pl.when(pl.program_id(0) == 0)
        def _():
            ds_scr[...] = jnp.zeros_like(ds_scr)

        dstates = [ds_scr[h] for h in range(G_HEADS)]
        for c in reversed(range(CPG)):
            r = slice(c * CH, (c + 1) * CH)
            for h in range(G_HEADS):
                ln = slice(h * 128, (h + 1) * 128)
                ds = dstates[h]
                dsb = ds.astype(BF16)
                sn = st_ref[h, c]
                snb = sn.astype(BF16)
                dob = do_ref[r, ln].astype(BF16)
                vnb = vn_ref[r, ln]
                dvn = (_dot(kd_ref[r, ln], dsb) + _dot_tn(at_ref[h, r, :], dob)).astype(BF16)
                du_ref[r, ln] = dvn
                dkd_ref[r, ln] = _dot_nt(vnb, dsb)
                tot = jnp.sum(jnp.sum(ds * sn, axis=1, keepdims=True), axis=0, keepdims=True)
                del_ref[h, pl.ds(c, 1), :] = jnp.broadcast_to(tot, (1, 128))
                dqd_ref[r, ln] = _dot_nt(dob, snb)
                dat_ref[h, r, :] = _dot_nt(dob, vnb)
                dw_ref[r, ln] = (-_dot_nt(dvn, snb)).astype(BF16)
                dstates[h] = ds * el_ref[h, pl.ds(c, 1), :] + _dot_tn(qd_ref[r, ln], dob) - _dot_tn(w_ref[r, ln], dvn)
        for h in range(G_HEADS):
            ds_scr[h] = dstates[h]

    last = N_GRP - 1
    row = pl.BlockSpec((GRP, D), lambda i: (last - i, 0))
    sq = pl.BlockSpec((G_HEADS, GRP, CH), lambda i: (0, last - i, 0))
    sc = pl.BlockSpec((G_HEADS, CPG, 128), lambda i: (0, last - i, 0))
    return pl.pallas_call(
        body,
        grid=(N_GRP,),
        in_specs=[row, row, row, row, sq, sc, row, pl.BlockSpec((G_HEADS, CPG, 128, 128), lambda i: (0, last - i, 0, 0))],
        out_specs=[row, row, row, row, sq, sc],
        out_shape=[
            jax.ShapeDtypeStruct((T, D), BF16),
            jax.ShapeDtypeStruct((T, D), BF16),
            jax.ShapeDtypeStruct((T, D), F32),
            jax.ShapeDtypeStruct((T, D), F32),
            jax.ShapeDtypeStruct((G_HEADS, T, CH), F32),
            jax.ShapeDtypeStruct((G_HEADS, T // CH, 128), F32),
        ],
        scratch_shapes=[pltpu.VMEM((G_HEADS, 128, 128), F32)],
        compiler_params=_cp("arbitrary"),
        name=name,
    )(do, w, qd, kd, at, el, vn, st)


def gdn_local_bwd(qkv, ba, alog, dtb, tinv, du, dw, dqd, dkd, dat, dl, name):
    def body(q_ref, k_ref, v_ref, ba_ref, al_ref, dt_ref, ti_ref, du_ref, dw_ref, dqd_ref, dkd_ref, dat_ref, dl_ref,
             dq_ref, dk_ref, dv_ref, dba_ref, dal_ref, ddt_ref, gcs):
        gi = pl.program_id(0)
        h = pl.program_id(1)
        beta, g, xa, ea = _gdn_gates(ba_ref, al_ref, dt_ref, h)
        gc = _seg_cumsum(jnp.broadcast_to(g, (GRP, 128)))
        gcs[...] = gc
        causal, strict, _, _ = _chunk_masks()
        dgc_l, dgl_l, dbeta_l, state = [], [], [], []
        for c in range(CPG):
            r = slice(c * CH, (c + 1) * CH)
            q, k, v = q_ref[r, :], k_ref[r, :], v_ref[r, :]
            bt = beta[r, :]
            gcc = gc[r, :]
            ec = jnp.exp(gcc)
            gl = gcs[pl.ds(c * CH + CH - 1, 1), :]
            f2 = jnp.exp(gl - gcc)
            elc = jnp.exp(gl)
            dm = _decay(gcc, causal)
            qs = q * G_SCALE
            kb = k * bt
            vb = v * bt
            kbe = kb * ec
            kbf, kbb, qsb = k.astype(BF16), kb.astype(BF16), qs.astype(BF16)
            a1 = _dot_nt(kbb, kbf)
            qk = _dot_nt(qsb, kbf)
            ti = ti_ref[r, :]
            tb = ti.astype(BF16)
            du_c, dw_c = du_ref[r, :], dw_ref[r, :]
            dqd_c, dkd_c, dat_c = dqd_ref[r, :], dkd_ref[r, :], dat_ref[r, :]

            dqs = dqd_c * ec
            d_e = jnp.sum(dqd_c * qs, axis=1, keepdims=True)
            dk = dkd_c * f2
            tcol = jnp.sum(dkd_c * k, axis=1, keepdims=True) * f2[:, 0:1]
            dgl = jnp.sum(tcol, axis=0, keepdims=True) + dl_ref[pl.ds(c, 1), 0:1] * elc[:, 0:1]
            dgc = -tcol
            dqk = (dat_c * dm).astype(BF16)
            d_d = dat_c * qk
            dqs = dqs + _dot(dqk, kbf)
            dk = dk + _dot_tn(dqk, qsb)
            dtinv = _dot_nt(du_c, vb.astype(BF16)) + _dot_nt(dw_c, kbe.astype(BF16))
            dvb = _dot_tn(tb, du_c)
            dkbe = _dot_tn(tb, dw_c)
            dq_ref[r, :] = dqs * G_SCALE
            state.append((ti.T, dtinv, dm, a1, dkbe, dvb, d_d, dk, d_e, dgc, dgl))

        xs = [_dot3(st[0], st[1]) for st in state]
        dlms = [jnp.where(strict, -_dot3(x, st[0]), 0.0) for x, st in zip(xs, state)]

        for c in range(CPG):
            r = slice(c * CH, (c + 1) * CH)
            _, _, dm, a1, dkbe, dvb, d_d, dk, d_e, dgc, dgl = state[c]
            dlm = dlms[c]
            k, v = k_ref[r, :], v_ref[r, :]
            bt = beta[r, :]
            ec = jnp.exp(gc[r, :])
            kb = k * bt
            kbf, kbb = k.astype(BF16), kb.astype(BF16)
            da1 = (dlm * dm).astype(BF16)
            d_d = d_d + dlm * a1
            dkb = _dot(da1, kbf) + dkbe * ec
            dk = dk + _dot_tn(da1, kbb)
            d_e = d_e + jnp.sum(dkbe * kb, axis=1, keepdims=True)
            dk = dk + dkb * bt
            dbeta_l.append(jnp.sum(dkb * k, axis=1, keepdims=True) + jnp.sum(dvb * v, axis=1, keepdims=True))
            ddiff = d_d * dm
            dgc = dgc + jnp.sum(ddiff, axis=1, keepdims=True) - jnp.sum(ddiff.T, axis=1, keepdims=True)
            dgc = dgc + d_e * ec[:, 0:1]
            dgc_l.append(dgc)
            dgl_l.append(jnp.broadcast_to(dgl, (CH, 1)))
            dk_ref[r, :] = dk
            dv_ref[r, :] = dvb * bt

        dgc_all = jnp.broadcast_to(jnp.concatenate(dgc_l, axis=0), (GRP, 128))
        dg = _seg_cumsum(dgc_all, reverse=True)[:, 0:1] + jnp.concatenate(dgl_l, axis=0)
        dbeta = jnp.concatenate(dbeta_l, axis=0)
        da = dg * (-ea) * _sigmoid(xa)
        db = dbeta * beta * (1.0 - beta)
        lane = lax.broadcasted_iota(jnp.int32, (GRP, 2 * G_HEADS), 1)
        dba = jnp.where(lane == h, db, 0.0) + jnp.where(lane == G_HEADS + h, da, 0.0)
        lane8 = lax.broadcasted_iota(jnp.int32, (1, G_HEADS), 1)
        dal = jnp.where(lane8 == h, jnp.sum(dg * g, axis=0, keepdims=True), 0.0)
        ddt = jnp.where(lane8 == h, jnp.sum(da, axis=0, keepdims=True), 0.0)

        @pl.when(h == 0)
        def _():
            dba_ref[...] = dba

        @pl.when(h > 0)
        def _():
            dba_ref[...] += dba

        @pl.when((h == 0) & (gi == 0))
        def _():
            dal_ref[...] = dal
            ddt_ref[...] = ddt

        @pl.when((h > 0) | (gi > 0))
        def _():
            dal_ref[...] += dal
            ddt_ref[...] += ddt

    hd = lambda off: pl.BlockSpec((GRP, 128), lambda i, h: (i, off + h))
    vec = pl.BlockSpec((1, G_HEADS), lambda i, h: (0, 0))
    sq = pl.BlockSpec((None, GRP, CH), lambda i, h: (h, i, 0))
    gates = pl.BlockSpec((GRP, 2 * G_HEADS), lambda i, h: (i, 0))
    return pl.pallas_call(
        body,
        grid=(N_GRP, G_HEADS),
        in_specs=[hd(0), hd(G_HEADS), hd(2 * G_HEADS), gates, vec, vec, sq, hd(0), hd(0), hd(0), hd(0), sq,
                  pl.BlockSpec((None, CPG, 128), lambda i, h: (h, i, 0))],
        out_specs=[hd(0), hd(0), hd(0), gates, vec, vec],
        out_shape=[
            jax.ShapeDtypeStruct((T, D), F32),
            jax.ShapeDtypeStruct((T, D), F32),
            jax.ShapeDtypeStruct((T, D), F32),
            jax.ShapeDtypeStruct((T, 2 * G_HEADS), F32),
            jax.ShapeDtypeStruct((1, G_HEADS), F32),
            jax.ShapeDtypeStruct((1, G_HEADS), F32),
        ],
        scratch_shapes=[pltpu.VMEM((GRP, 128), F32)],
        compiler_params=_cp("arbitrary", "arbitrary"),
        name=name,
    )(qkv, qkv, qkv, ba, alog, dtb, tinv, du, dw, dqd, dkd, dat, dl)


def _gated_norm(o, z, g):
    rstd = lax.rsqrt(jnp.mean(o * o, axis=-1, keepdims=True) + EPS)
    oh = o * rstd
    sig = _sigmoid(z)
    return oh, rstd, sig


def gdn_out_fwd(h, o, z, norm_g, w_out, name):
    tt = 512

    def body(h_ref, o_ref, z_ref, g_ref, w_ref, hnew_ref, cat_ref):
        g = g_ref[...]
        for hh in range(G_HEADS):
            ln = slice(hh * 128, (hh + 1) * 128)
            zz = z_ref[:, ln]
            oh, _, sig = _gated_norm(o_ref[:, ln], zz, g)
            cat_ref[:, ln] = (oh * g * (zz * sig)).astype(BF16)
        hnew_ref[...] = h_ref[...] + _dot(cat_ref[...], w_ref[...])

    full = pl.BlockSpec((tt, D), lambda i: (i, 0))
    return pl.pallas_call(
        body,
        grid=(T // tt,),
        in_specs=[full, full, full, pl.BlockSpec((1, 128), lambda i: (0, 0)), pl.BlockSpec((D, D), lambda i: (0, 0))],
        out_specs=[full, full],
        out_shape=[jax.ShapeDtypeStruct((T, D), F32), jax.ShapeDtypeStruct((T, D), BF16)],
        compiler_params=_cp("parallel"),
        name=name,
    )(h, o, z, norm_g, w_out)


def gdn_out_bwd(dres, o, z, norm_g, w_out, name):
    tt = 512

    def body(d_ref, o_ref, z_ref, g_ref, w_ref, do_ref, dz_ref, dg_ref, dcat):
        i = pl.program_id(0)
        g = g_ref[...]
        dcat[...] = _dot_nt(d_ref[...].astype(BF16), w_ref[...])
        dg = jnp.zeros((1, 128), F32)
        for hh in range(G_HEADS):
            ln = slice(hh * 128, (hh + 1) * 128)
            zz = z_ref[:, ln]
            oh, rstd, sig = _gated_norm(o_ref[:, ln], zz, g)
            dout = dcat[:, ln]
            dy = dout * (zz * sig)
            dz_ref[:, ln] = dout * (oh * g) * _dsilu(zz, sig)
            dg = dg + jnp.sum(dy * oh, axis=0, keepdims=True)
            doh = dy * g
            do_ref[:, ln] = rstd * (doh - oh * jnp.mean(doh * oh, axis=-1, keepdims=True))

        @pl.when(i == 0)
        def _():
            dg_ref[...] = dg

        @pl.when(i > 0)
        def _():
            dg_ref[...] += dg

    full = pl.BlockSpec((tt, D), lambda i: (i, 0))
    vec = pl.BlockSpec((1, 128), lambda i: (0, 0))
    return pl.pallas_call(
        body,
        grid=(T // tt,),
        in_specs=[full, full, full, vec, pl.BlockSpec((D, D), lambda i: (0, 0))],
        out_specs=[full, full, vec],
        out_shape=[jax.ShapeDtypeStruct((T, D), F32), jax.ShapeDtypeStruct((T, D), F32), jax.ShapeDtypeStruct((1, 128), F32)],
        scratch_shapes=[pltpu.VMEM((tt, D), F32)],
        compiler_params=_cp("arbitrary"),
        name=name,
    )(dres, o, z, norm_g, w_out)


GDN_SPLITS = [(0, 1024), (1024, 1024), (2048, 1024), (3072, 1024), (4096, 2 * G_HEADS)]


def gdn_fwd(h, g_row, w_in, conv_w, alog, dtb, norm_g, w_out, tag):
    hn, qkv_pre, z, ba = proj_fwd(h, g_row, w_in, [(0, G_QKV), (G_QKV, 1024), (4096, 2 * G_HEADS)], f"gdn_proj_{tag}")
    qkv = gdn_prep_fwd(qkv_pre, conv_w, f"gdn_prep_{tag}")
    u, w, qd, kd, at, el, tinv = gdn_local_fwd(qkv, ba, alog, dtb, f"gdn_local_{tag}")
    o, vn, st = gdn_rec_fwd(u, w, qd, kd, at, el, f"gdn_rec_{tag}")
    hnew, cat = gdn_out_fwd(h, o, z, norm_g, w_out, f"gdn_out_{tag}")
    return hnew, (h, hn, qkv_pre, z, ba, qkv, w, qd, kd, at, el, tinv, o, vn, st, cat)


def gdn_bwd(dres, saved, g_row, w_in, conv_w, alog, dtb, norm_g, w_out, tag):
    h, hn, qkv_pre, z, ba, qkv, w, qd, kd, at, el, tinv, o, vn, st, cat = saved
    d_o, d_z, d_ng = gdn_out_bwd(dres, o, z, norm_g, w_out, f"gdn_out_bwd_{tag}")
    d_wout = mm_tn_full(cat, dres, BF16, f"gdn_wout_grad_{tag}")
    du, dw, dqd, dkd, dat, dl = gdn_rec_bwd(d_o, w, qd, kd, at, el, vn, st, f"gdn_rec_bwd_{tag}")
    dq, dk, dv, dba, dal, ddt = gdn_local_bwd(qkv, ba, alog, dtb, tinv, du, dw, dqd, dkd, dat, dl, f"gdn_local_bwd_{tag}")
    dpre, dcw = [], []
    for part, d in enumerate((dq, dk, dv)):
        dx, dwc = gdn_prep_bwd(d, qkv_pre, conv_w, part, part < 2, f"gdn_prep_bwd_{tag}_{part}")
        dpre.append(dx)
        dcw.append(dwc)
    parts = dpre + [d_z, dba]
    dres_new, d_norm, dproj = proj_bwd_data(h, g_row, w_in, parts, GDN_SPLITS, dres, f"gdn_proj_bwd_{tag}")
    d_win = cols_to_slabs(mm_tn_full(hn, dproj, F32, f"gdn_win_grad_{tag}"), mm_tn(hn, dba, f"gdn_win_grad_ba_{tag}"),
                          f"gdn_win_slabs_{tag}")
    return dres_new, dict(norm=d_norm, w_in=d_win, conv_w=jnp.concatenate(dcw, axis=1), A_log=dal, dt_bias=ddt,
                          norm_g=d_ng, w_out=d_wout)


MESH = pl.DeviceIdType.MESH
ANY = pl.BlockSpec(memory_space=pl.ANY)


def _coords():
    return lax.axis_index("x"), lax.axis_index("y"), lax.axis_index("c")


def _slot(p):
    return 4 * p[0] + 2 * p[1] + p[2]


def all_gather(shards, name):
    k_n = len(shards)

    def body(*refs):
        srcs, dsts = refs[:k_n], refs[k_n:2 * k_n]
        send_sems, recv_sems, local_sems = refs[2 * k_n:]
        x, y, c = _coords()
        me, sibling = (x, y, c), (x, y, 1 - c)
        chips = [(1 - x, y), (x, 1 - y), (1 - x, 1 - y)]

        def copy(k, s, block, to, from_src=False):
            rows = dsts[k].at[_slot(block)]
            return pltpu.make_async_remote_copy(
                src_ref=srcs[k] if from_src else rows, dst_ref=rows,
                send_sem=send_sems.at[k, s], recv_sem=recv_sems.at[k, s], device_id=to, device_id_type=MESH)

        local = [pltpu.make_async_copy(srcs[k], dsts[k].at[_slot(me)], local_sems.at[k]) for k in range(k_n)]
        for cp in local:
            cp.start()
        first = []
        for k in range(k_n):
            first.append(copy(k, 0, me, sibling, True))
            first += [copy(k, 1 + j, me, (*chip, c), True) for j, chip in enumerate(chips)]
        for cp in first:
            cp.start()
        passed = []
        for j, chip in enumerate(chips):
            for k in range(k_n):
                copy(k, 1 + j, (*chip, c), me).wait_recv()
                fw = copy(k, 4 + j, (*chip, c), sibling)
                fw.start()
                passed.append(fw)
        for k in range(k_n):
            copy(k, 0, sibling, me).wait_recv()
            for j, chip in enumerate(chips):
                copy(k, 4 + j, (*chip, 1 - c), me).wait_recv()
        for cp in first + passed:
            cp.wait_send()
        for cp in local:
            cp.wait()

    return pl.pallas_call(
        body,
        in_specs=[ANY] * k_n,
        out_specs=[ANY] * k_n,
        out_shape=[jax.ShapeDtypeStruct((N_DEV,) + s.shape, s.dtype) for s in shards],
        scratch_shapes=[pltpu.SemaphoreType.DMA((k_n, 7)), pltpu.SemaphoreType.DMA((k_n, 7)),
                        pltpu.SemaphoreType.DMA((k_n,))],
        name=name,
    )(*shards)


HBM = pl.BlockSpec(memory_space=pltpu.HBM)
SEM = pl.BlockSpec(memory_space=pltpu.SEMAPHORE)
EFFECT = pltpu.SideEffectType.DATAFLOW_SIDE_EFFECTING


def _hbm(a):
    return pltpu.with_memory_space_constraint(a, pltpu.HBM)


def _peer_list(x, y, c):
    peers = []
    for j in range(1, N_DEV):
        jx, jy, jc = (j >> 2) & 1, (j >> 1) & 1, j & 1
        peers.append((x if jx == 0 else 1 - x, y if jy == 0 else 1 - y, c if jc == 0 else 1 - c))
    return peers


def _push_views(kind, layer, src_ref, land_ref, me, peer_slot):
    if kind == "gather":
        return src_ref, land_ref.at[me], land_ref.at[peer_slot]
    if layer is None:
        return src_ref.at[peer_slot], land_ref.at[me], land_ref.at[peer_slot]
    return src_ref.at[peer_slot], land_ref.at[me, layer], land_ref.at[peer_slot, layer]


def _push_copies(groups, srcs, lands, sems):
    x, y, c = _coords()
    me = _slot((x, y, c))
    peers = _peer_list(x, y, c)
    t = 0
    for gi, group in enumerate(groups):
        for ti, (kind, layer, _, li) in enumerate(group):
            for j, peer in enumerate(peers):
                out, there, here = _push_views(kind, layer, srcs[t], lands[li], me, _slot(peer))
                k = ti * (N_DEV - 1) + j
                yield out, there, here, sems[2 * gi].at[k], sems[2 * gi + 1].at[k], peer
            t += 1


def push_start(groups, lands, name, carry=()):
    flat = [it for g in groups for it in g]
    n, n_l, n_g, n_c = len(flat), len(lands), len(groups), len(carry)
    n_in = n + n_l + n_c

    def body(*refs):
        srcs, land_refs, sems = refs[:n], refs[n:n + n_l], refs[n_in:n_in + 2 * n_g]
        for out, there, _, s_sem, r_sem, peer in _push_copies(groups, srcs, land_refs, sems):
            pltpu.make_async_remote_copy(src_ref=out, dst_ref=there, send_sem=s_sem, recv_sem=r_sem,
                                         device_id=peer, device_id_type=MESH).start()

    arrays = [it[2] for it in flat] + list(lands) + list(carry)
    sem_shapes = []
    for g in groups:
        sem_shapes += [pltpu.SemaphoreType.DMA((len(g) * (N_DEV - 1),))] * 2
    outs = pl.pallas_call(
        body,
        name=name,
        in_specs=[HBM] * n_in,
        out_specs=[SEM] * (2 * n_g) + [HBM] * n_in,
        out_shape=sem_shapes + [pltpu.HBM(a.shape, a.dtype) for a in arrays],
        input_output_aliases={i: 2 * n_g + i for i in range(n_in)},
        compiler_params=pltpu.CompilerParams(has_side_effects=EFFECT),
    )(*[_hbm(a) for a in arrays])
    sems, thru = list(outs[:2 * n_g]), list(outs[2 * n_g:])
    return sems, thru[:n], thru[n:n + n_l], thru[n + n_l:]


def push_wait(groups, lands, sems, after, name):
    flat = [it for g in groups for it in g]
    n, n_l, n_g = len(flat), len(lands), len(groups)

    def body(*refs):
        srcs, land_refs, sem_refs = refs[:n], refs[n:n + n_l], refs[n + n_l:n + n_l + 2 * n_g]
        for out, _, here, s_sem, r_sem, peer in _push_copies(groups, srcs, land_refs, sem_refs):
            cp = pltpu.make_async_remote_copy(src_ref=out, dst_ref=here, send_sem=s_sem, recv_sem=r_sem,
                                              device_id=peer, device_id_type=MESH)
            cp.wait_send()
            cp.wait_recv()

    arrays = [it[2] for it in flat] + list(lands)
    outs = pl.pallas_call(
        body,
        name=name,
        in_specs=[HBM] * (n + n_l) + [SEM] * (2 * n_g) + [ANY],
        out_specs=[HBM] * (n + n_l),
        out_shape=[pltpu.HBM(a.shape, a.dtype) for a in arrays],
        input_output_aliases={i: i for i in range(n + n_l)},
        compiler_params=pltpu.CompilerParams(has_side_effects=EFFECT),
    )(*arrays, *sems, after)
    return list(outs[:n]), list(outs[n:])


def sum_slabs(parts, name):
    n, rows, cols = parts.shape

    def body(p_ref, o_ref):
        g = p_ref[0]
        for s in range(1, n):
            g = g + p_ref[s]
        o_ref[...] = g

    return pl.pallas_call(body, out_shape=jax.ShapeDtypeStruct((rows, cols), F32), name=name)(parts)


def _row_tile(rows, cols):
    if rows * cols * 4 <= (1 << 20) or rows % 8:
        return rows
    tr = rows
    while tr % 2 == 0 and (tr // 2) % 8 == 0 and tr * cols * 4 > (1 << 20):
        tr //= 2
    return tr


def adamw(parts, w, m, v, name):
    p_n = parts.shape[0]
    rows, cols = w.shape
    tr = _row_tile(rows, cols)

    def body(p_ref, w_ref, m_ref, v_ref, g_ref, d_ref, nm_ref, nv_ref):
        g = p_ref[0].astype(F32)
        for s in range(1, p_n):
            g = g + p_ref[s].astype(F32)
        g_ref[...] = g
        d_ref[...], nm_ref[...], nv_ref[...] = _adam_update(g, w_ref[...], m_ref[...], v_ref[...])

    blk = pl.BlockSpec((tr, cols), lambda i: (i, 0))
    return pl.pallas_call(
        body,
        grid=(rows // tr,),
        in_specs=[pl.BlockSpec((p_n, tr, cols), lambda i: (0, i, 0)), blk, blk, blk],
        out_specs=[blk] * 4,
        out_shape=[jax.ShapeDtypeStruct((rows, cols), F32)] * 4,
        compiler_params=_cp("parallel"),
        name=name,
    )(parts, w, m, v)


def _adam_update(g, w, m, v):
    m_new = ADAM_B1 * m + (1.0 - ADAM_B1) * g
    v_new = ADAM_B2 * v + (1.0 - ADAM_B2) * (g * g)
    m_hat = m_new / (1.0 - ADAM_B1 ** ADAM_STEP)
    v_hat = v_new / (1.0 - ADAM_B2 ** ADAM_STEP)
    return -ADAM_LR * (m_hat / (jnp.sqrt(v_hat) + ADAM_EPS) + ADAM_WD * w), m_new, v_new


def _adamw_nd(parts, w, m, v, name):
    p_n = parts.shape[0]
    n_l, rows, cols = w.shape
    tr = _row_tile(rows, cols)

    def body(p_ref, w_ref, m_ref, v_ref, g_ref, d_ref, nm_ref, nv_ref):
        g = p_ref[0].astype(F32)
        for s in range(1, p_n):
            g = g + p_ref[s].astype(F32)
        g_ref[...] = g
        d_ref[...], nm_ref[...], nv_ref[...] = _adam_update(g, w_ref[...], m_ref[...], v_ref[...])

    blk = pl.BlockSpec((None, tr, cols), lambda l, i: (l, i, 0))
    return pl.pallas_call(
        body,
        grid=(n_l, rows // tr),
        in_specs=[pl.BlockSpec((p_n, None, tr, cols), lambda l, i: (0, l, i, 0)), blk, blk, blk],
        out_specs=[blk] * 4,
        out_shape=[jax.ShapeDtypeStruct(w.shape, F32)] * 4,
        compiler_params=_cp("parallel", "parallel"),
        name=name,
    )(parts, w, m, v)


def slabs_to_cols(slabs, name):
    n, r, w = slabs.shape
    tr = 256 if r % 256 == 0 else r

    def body(s_ref, o_ref):
        for s in range(n):
            o_ref[:, w * s:w * (s + 1)] = s_ref[s]

    return pl.pallas_call(
        body,
        grid=(r // tr,),
        in_specs=[pl.BlockSpec((n, tr, w), lambda i: (0, i, 0))],
        out_specs=pl.BlockSpec((tr, n * w), lambda i: (i, 0)),
        out_shape=jax.ShapeDtypeStruct((r, n * w), slabs.dtype),
        compiler_params=_cp("parallel"),
        name=name,
    )(slabs)


REPL = ["ffn1_norm", "mix_norm", "ffn2_norm", "hyb_dw_b", "hyb_ln_g", "hyb_ln_b", "gdn_A_log", "gdn_dt_bias",
        "gdn_norm_g", "final_norm"]
WEIGHTS = ["ffn1_norm", "ffn1_w_in", "ffn1_w_out", "mix_norm", "ffn2_norm", "ffn2_w_in", "ffn2_w_out", "hyb_w_in",
           "hyb_dw_w", "hyb_dw_b", "hyb_ln_g", "hyb_ln_b", "hyb_w_out", "gdn_w_in", "gdn_conv_w", "gdn_A_log",
           "gdn_dt_bias", "gdn_norm_g", "gdn_w_out", "final_norm"]


def _pack(arrs, rows):
    flat = jnp.concatenate([a.reshape(-1) for a in arrs])
    return jnp.pad(flat, (0, rows * 128 - flat.shape[0])).reshape(rows, 128)


def kernel(x, positions, ffn1_norm, ffn1_w_in, ffn1_w_out, mix_norm, ffn2_norm, ffn2_w_in, ffn2_w_out, hyb_w_in, hyb_dw_w, hyb_dw_b, hyb_ln_g, hyb_ln_b, hyb_w_out, gdn_w_in, gdn_conv_w, gdn_A_log, gdn_dt_bias, gdn_norm_g, gdn_w_out, final_norm, loss_target, m_ffn1_norm, m_ffn1_w_in, m_ffn1_w_out, m_mix_norm, m_ffn2_norm, m_ffn2_w_in, m_ffn2_w_out, m_hyb_w_in, m_hyb_dw_w, m_hyb_dw_b, m_hyb_ln_g, m_hyb_ln_b, m_hyb_w_out, m_gdn_w_in, m_gdn_conv_w, m_gdn_A_log, m_gdn_dt_bias, m_gdn_norm_g, m_gdn_w_out, m_final_norm, v_ffn1_norm, v_ffn1_w_in, v_ffn1_w_out, v_mix_norm, v_ffn2_norm, v_ffn2_w_in, v_ffn2_w_out, v_hyb_w_in, v_hyb_dw_w, v_hyb_dw_b, v_hyb_ln_g, v_hyb_ln_b, v_hyb_w_out, v_gdn_w_in, v_gdn_conv_w, v_gdn_A_log, v_gdn_dt_bias, v_gdn_norm_g, v_gdn_w_out, v_final_norm):
    w = dict(ffn1_norm=ffn1_norm, ffn1_w_in=ffn1_w_in, ffn1_w_out=ffn1_w_out, mix_norm=mix_norm, ffn2_norm=ffn2_norm,
             ffn2_w_in=ffn2_w_in, ffn2_w_out=ffn2_w_out, hyb_w_in=hyb_w_in, hyb_dw_w=hyb_dw_w, hyb_dw_b=hyb_dw_b,
             hyb_ln_g=hyb_ln_g, hyb_ln_b=hyb_ln_b, hyb_w_out=hyb_w_out, gdn_w_in=gdn_w_in, gdn_conv_w=gdn_conv_w,
             gdn_A_log=gdn_A_log, gdn_dt_bias=gdn_dt_bias, gdn_norm_g=gdn_norm_g, gdn_w_out=gdn_w_out,
             final_norm=final_norm)
    mom = dict(ffn1_norm=m_ffn1_norm, ffn1_w_in=m_ffn1_w_in, ffn1_w_out=m_ffn1_w_out, mix_norm=m_mix_norm,
               ffn2_norm=m_ffn2_norm, ffn2_w_in=m_ffn2_w_in, ffn2_w_out=m_ffn2_w_out, hyb_w_in=m_hyb_w_in,
               hyb_dw_w=m_hyb_dw_w, hyb_dw_b=m_hyb_dw_b, hyb_ln_g=m_hyb_ln_g, hyb_ln_b=m_hyb_ln_b,
               hyb_w_out=m_hyb_w_out, gdn_w_in=m_gdn_w_in, gdn_conv_w=m_gdn_conv_w, gdn_A_log=m_gdn_A_log,
               gdn_dt_bias=m_gdn_dt_bias, gdn_norm_g=m_gdn_norm_g, gdn_w_out=m_gdn_w_out, final_norm=m_final_norm)
    var = dict(ffn1_norm=v_ffn1_norm, ffn1_w_in=v_ffn1_w_in, ffn1_w_out=v_ffn1_w_out, mix_norm=v_mix_norm,
               ffn2_norm=v_ffn2_norm, ffn2_w_in=v_ffn2_w_in, ffn2_w_out=v_ffn2_w_out, hyb_w_in=v_hyb_w_in,
               hyb_dw_w=v_hyb_dw_w, hyb_dw_b=v_hyb_dw_b, hyb_ln_g=v_hyb_ln_g, hyb_ln_b=v_hyb_ln_b,
               hyb_w_out=v_hyb_w_out, gdn_w_in=v_gdn_w_in, gdn_conv_w=v_gdn_conv_w, gdn_A_log=v_gdn_A_log,
               gdn_dt_bias=v_gdn_dt_bias, gdn_norm_g=v_gdn_norm_g, gdn_w_out=v_gdn_w_out, final_norm=v_final_norm)
    xi, yi, ci = _coords()
    me = 4 * xi + 2 * yi + ci

    big = ["ffn1_w_in", "ffn1_w_out", "ffn2_w_in", "ffn2_w_out", "hyb_w_in", "hyb_w_out", "gdn_w_in", "gdn_w_out"]
    ag_groups, ag_lands = [], []

    def add_group(shards):
        group = []
        for s in shards:
            land = lax.dynamic_update_slice(lax.empty((N_DEV,) + s.shape, s.dtype), s[None], (me,) + (0,) * s.ndim)
            group.append(("gather", None, s, len(ag_lands)))
            ag_lands.append(land)
        ag_groups.append(group)

    first = all_gather([ffn1_w_in[0].astype(BF16), ffn1_w_out[0].astype(BF16)], "weights_gather_first")
    for l in range(DEPTH):
        i = l // 2
        if l == 0:
            ag_groups.append([])
        else:
            add_group([ffn1_w_in[l].astype(BF16), ffn1_w_out[l].astype(BF16)])
        if l % 2 == 0:
            add_group([hyb_w_in[i].astype(BF16), hyb_w_out[i].astype(BF16), hyb_dw_w[i]])
        else:
            add_group([gdn_w_in[i].astype(BF16), gdn_w_out[i].astype(BF16), gdn_conv_w[i]])
        add_group([ffn2_w_in[l].astype(BF16), ffn2_w_out[l].astype(BF16)])
    ag_sems, ag_srcs, ag_lands, first = push_start(ag_groups[1:], ag_lands, "weights_gather_start", carry=first)
    ag_sems = [None, None] + ag_sems

    def fetch(gi, after):
        if gi == 0:
            return first
        group = ag_groups[gi]
        base = sum(len(g) for g in ag_groups[:gi])
        items = [(kind, layer, ag_srcs[base + t], t) for t, (kind, layer, _, _) in enumerate(group)]
        lands = [ag_lands[li] for _, _, _, li in group]
        return push_wait([items], lands, ag_sems[2 * gi:2 * gi + 2], after, f"weights_gather_wait_{gi}")[1]

    row = lambda a: a.reshape(1, -1)

    rope = make_rope(positions)
    h = x[0]
    saved = []
    for l in range(DEPTH):
        i = l // 2
        rec = {"h1": h}
        wi, wo = fetch(3 * l, h)
        rec["w1"] = (wi.reshape(2, FFN_TILES, D, FFN_SHARD), wo)
        h, rec["hn1"], rec["a1"], rec["b1"] = ffn_fwd(h, row(ffn1_norm[l]), *rec["w1"], l, "1")
        mi, mo, mc = fetch(3 * l + 1, h)
        if l % 2 == 0:
            rec["wm"] = (slabs_to_cols(mi, f"hyb_w_in_cols_{i}"),
                         jnp.pad(slabs_to_cols(mc, f"hyb_dw_w_cols_{i}"), ((0, 1), (0, 0))), mo.reshape(D, D))
            w_in_f, dw_f, w_out_f = rec["wm"]
            h, rec["mix"] = hybrid_fwd(h, row(mix_norm[l]), w_in_f, dw_f, row(hyb_dw_b[i]), row(hyb_ln_g[i]),
                                       row(hyb_ln_b[i]), w_out_f, rope, str(i))
        else:
            rec["wm"] = (slabs_to_cols(mi, f"gdn_w_in_cols_{i}"), slabs_to_cols(mc, f"gdn_conv_w_cols_{i}"),
                         mo.reshape(D, D))
            w_in_f, cw_f, w_out_f = rec["wm"]
            h, rec["mix"] = gdn_fwd(h, row(mix_norm[l]), w_in_f, cw_f, row(gdn_A_log[i]), row(gdn_dt_bias[i]),
                                    row(gdn_norm_g[i]), w_out_f, str(i))
        rec["h2"] = h
        wi, wo = fetch(3 * l + 2, h)
        rec["w2"] = (wi.reshape(2, FFN_TILES, D, FFN_SHARD), wo)
        h, rec["hn2"], rec["a2"], rec["b2"] = ffn_fwd(h, row(ffn2_norm[l]), *rec["w2"], l, "2")
        saved.append(rec)
    dres, d_final, loss_acc = final_loss(h, row(final_norm), loss_target[0])

    ge_land = {n: lax.empty((N_DEV,) + w[n].shape, BF16) for n in big}
    ge_pending = []

    def send(named, layer, tag, carry):
        lands = [ge_land[n] for n, _ in named]
        group = [("scatter", layer, s, t) for t, (_, s) in enumerate(named)]
        sems, srcs, lands_out, carried = push_start([group], lands, f"grad_send_{tag}", carry=[carry])
        for (n, _), land in zip(named, lands_out):
            ge_land[n] = land
        ge_pending.append(([(n, layer, s) for (n, _), s in zip(named, srcs)], sems))
        return carried[0]

    gsmall = {n: [None] * (DEPTH if n in ("ffn1_norm", "mix_norm", "ffn2_norm") else 2) for n in REPL[:-1]}
    gsmall["hyb_dw_w"] = [None, None]
    gsmall["gdn_conv_w"] = [None, None]
    for l in reversed(range(DEPTH)):
        i = l // 2
        rec = saved[l]
        dhn, dwin, dwout = ffn_bwd(rec["hn2"], rec["a2"], rec["b2"], dres, *rec["w2"], l, "2")
        dhn = send([("ffn2_w_in", dwin.reshape(N_DEV, D, FFN_SHARD)),
                    ("ffn2_w_out", dwout.reshape(N_DEV, FFN_SHARD // 2, D))], l, f"ffn2_{l}", dhn)
        dres, dg = norm_bwd(rec["h2"], row(ffn2_norm[l]), dhn, dres, f"ffn2_norm_bwd_{l}")
        gsmall["ffn2_norm"][l] = dg
        if l % 2 == 0:
            w_in_f, dw_f, w_out_f = rec["wm"]
            dres, gr = hybrid_bwd(dres, rec["mix"], row(mix_norm[l]), w_in_f, dw_f, row(hyb_ln_g[i]),
                                  row(hyb_ln_b[i]), w_out_f, rope, str(i))
            dres = send([("hyb_w_in", gr["w_in"]), ("hyb_w_out", gr["w_out"].reshape(N_DEV, D // N_DEV, D))],
                        i, f"hyb_{i}", dres)
            for n in ("dw_w", "dw_b", "ln_g", "ln_b"):
                gsmall["hyb_" + n][i] = gr[n]
        else:
            w_in_f, cw_f, w_out_f = rec["wm"]
            dres, gr = gdn_bwd(dres, rec["mix"], row(mix_norm[l]), w_in_f, cw_f, row(gdn_A_log[i]),
                               row(gdn_dt_bias[i]), row(gdn_norm_g[i]), w_out_f, str(i))
            dres = send([("gdn_w_in", gr["w_in"]), ("gdn_w_out", gr["w_out"].reshape(N_DEV, D // N_DEV, D))],
                        i, f"gdn_{i}", dres)
            for n in ("conv_w", "A_log", "dt_bias", "norm_g"):
                gsmall["gdn_" + n][i] = gr[n]
        gsmall["mix_norm"][l] = gr["norm"]
        dhn, dwin, dwout = ffn_bwd(rec["hn1"], rec["a1"], rec["b1"], dres, *rec["w1"], l, "1")
        dhn = send([("ffn1_w_in", dwin.reshape(N_DEV, D, FFN_SHARD)),
                    ("ffn1_w_out", dwout.reshape(N_DEV, FFN_SHARD // 2, D))], l, f"ffn1_{l}", dhn)
        dres, dg = norm_bwd(rec["h1"], row(ffn1_norm[l]), dhn, dres, f"ffn1_norm_bwd_{l}")
        gsmall["ffn1_norm"][l] = dg
    grad_x = dres[None]

    n_repl_rows = 136
    small_rows = 576
    repl_flat = jnp.concatenate([jnp.concatenate([a.reshape(-1) for a in gsmall[n]]) for n in REPL[:-1]]
                                + [d_final.reshape(-1), loss_acc[0, 0:1]])
    loss_at = repl_flat.shape[0] - 1
    repl_pack = jnp.pad(repl_flat, (0, n_repl_rows * 128 - repl_flat.shape[0]))
    small_pack = jnp.concatenate([repl_pack] + [a.reshape(-1) for a in gsmall["hyb_dw_w"]]
                                 + [a.reshape(-1) for a in gsmall["gdn_conv_w"]]).reshape(small_rows, 128)

    own = {n: {} for n in big}

    def wait_for(pending, names, after, name):
        groups = [[("scatter", layer, s, names.index(n)) for n, layer, s in named] for named, _ in pending]
        sems = [s for _, pair in pending for s in pair]
        srcs_out, lands_out = push_wait(groups, [ge_land[n] for n in names], sems, after, name)
        flat_named = [it for named, _ in pending for it in named]
        for (n, layer, _), s in zip(flat_named, srcs_out):
            own[n][layer] = lax.dynamic_index_in_dim(s, me, 0, keepdims=False)
        for n, land in zip(names, lands_out):
            ge_land[n] = land

    def with_own(n, land):
        mine = jnp.stack([own[n][k] for k in range(len(own[n]))])
        return lax.dynamic_update_slice(land, mine[None], (me,) + (0,) * mine.ndim)

    out = {}
    last = ["ffn1_w_in", "ffn1_w_out"]
    wait_for(ge_pending[:-1], big, dres, "grad_wait_a")
    for n in big:
        if n not in last:
            out[n] = _adamw_nd(with_own(n, ge_land[n]), w[n], mom[n], var[n], f"adamw_{n}")
    pin = sum(out[n][1].reshape(-1)[0] for n in big if n not in last) * 0.0
    small_all, = all_gather([small_pack + pin], "small_grads_all_gather")
    g_small = sum_slabs(small_all, "small_grads_sum")
    loss = g_small.reshape(-1)[loss_at]
    wait_for(ge_pending[-1:], last, g_small, "grad_wait_b")
    for n in last:
        out[n] = _adamw_nd(with_own(n, ge_land[n]), w[n], mom[n], var[n], f"adamw_{n}")

    pk = lambda d: _pack([d[n] for n in REPL], n_repl_rows)
    res = adamw(g_small[:n_repl_rows][None], pk(w), pk(mom), pk(var), "adamw_replicated")
    off = 0
    for n in REPL:
        sz = w[n].size
        out[n] = [r.reshape(-1)[off:off + sz].reshape(w[n].shape) for r in res]
        off += sz
    g_dw = g_small[n_repl_rows:n_repl_rows + 248].reshape(2, CONV_K, CONV_C)
    g_dw = lax.dynamic_slice_in_dim(g_dw, me * (CONV_C // N_DEV), CONV_C // N_DEV, axis=2)
    out["hyb_dw_w"] = _adamw_nd(g_dw[None], w["hyb_dw_w"], mom["hyb_dw_w"], var["hyb_dw_w"], "adamw_hyb_dw_w")
    g_cw = g_small[n_repl_rows + 248:].reshape(2, G_CONV, G_QKV)
    g_cw = lax.dynamic_slice_in_dim(g_cw, me * (G_QKV // N_DEV), G_QKV // N_DEV, axis=2)
    out["gdn_conv_w"] = _adamw_nd(g_cw[None], w["gdn_conv_w"], mom["gdn_conv_w"], var["gdn_conv_w"], "adamw_gdn_conv_w")

    return (loss, grad_x, *[out[n][0] for n in WEIGHTS], *[out[n][1] for n in WEIGHTS],
            *[out[n][2] for n in WEIGHTS], *[out[n][3] for n in WEIGHTS])
```

```python
import functools

import jax
import jax.numpy as jnp
from jax import lax
from jax.experimental import pallas as pl
from jax.experimental.pallas import tpu as pltpu

F32 = jnp.float32
BF16 = jnp.bfloat16

N_DEV = 8
T = 4096
D = 1024
DEPTH = 4
FFN = 2816
FFN_SHARD = 2 * FFN // N_DEV
FFN_TILES = FFN // FFN_SHARD
EPS = 1e-6

A_HEADS = 8
A_HD = 64
A_W = 512
CONV_C = 512
CONV_K = 31
HYB_IN = 2560
ROPE_THETA = 500000.0
ROT = 16
DILATIONS = (1, 4, 16)
BLK = 128
KPAD = 2048

G_HEADS = 8
G_DK = 128
G_QKV = 3072
G_IN = 4112
G_CONV = 4
CH = 64
GRP = 512
CPG = GRP // CH
N_GRP = T // GRP

ADAM_LR = 0.001
ADAM_B1 = 0.9
ADAM_B2 = 0.999
ADAM_EPS = 1e-08
ADAM_WD = 0.01
ADAM_STEP = 10

VMEM_LIMIT = 56 * 1024 * 1024


def _cp(*sem):
    return pltpu.CompilerParams(dimension_semantics=sem, vmem_limit_bytes=VMEM_LIMIT)


def _dot(a, b):
    return jnp.dot(a, b, preferred_element_type=F32)


def _dot_nt(a, b):
    return lax.dot_general(a, b, (((1,), (1,)), ((), ())), preferred_element_type=F32)


def _dot_tn(a, b):
    return lax.dot_general(a, b, (((0,), (0,)), ((), ())), preferred_element_type=F32)


def _sigmoid(x):
    return 1.0 / (1.0 + jnp.exp(-x))


def _dsilu(x, sig):
    return sig * (1.0 + x * (1.0 - sig))


def _rms(x, g):
    rstd = lax.rsqrt(jnp.mean(x * x, axis=-1, keepdims=True) + EPS)
    return x * rstd * g


FFN_TT = 512


def ffn_fwd(h, g_row, w_in, w_out, layer, tag=""):
    def body(h_ref, g_ref, win_ref, wout_ref, hnew_ref, hn_ref, a_ref, b_ref):
        x = h_ref[...]
        hn = _rms(x, g_ref[...]).astype(BF16)
        hn_ref[...] = hn
        acc = None
        for j in range(FFN_TILES):
            a = _dot_nt(hn, win_ref[0, j])
            b = _dot_nt(hn, win_ref[1, j])
            act = a * _sigmoid(a) * b
            a_ref[j] = a.astype(BF16)
            b_ref[j] = b.astype(BF16)
            part = _dot(act.astype(BF16), wout_ref[2 * j:2 * j + 2].reshape(FFN_SHARD, D))
            acc = part if acc is None else acc + part
        hnew_ref[...] = x + 0.5 * acc

    tt = FFN_TT
    resident = pl.Buffered(1)
    return pl.pallas_call(
        body,
        grid=(T // tt,),
        in_specs=[
            pl.BlockSpec((tt, D), lambda i: (i, 0)),
            pl.BlockSpec((1, D), lambda i: (0, 0)),
            pl.BlockSpec((2, FFN_TILES, FFN_SHARD, D), lambda i: (0, 0, 0, 0), pipeline_mode=resident),
            pl.BlockSpec((N_DEV, FFN_SHARD // 2, D), lambda i: (0, 0, 0), pipeline_mode=resident),
        ],
        out_specs=[
            pl.BlockSpec((tt, D), lambda i: (i, 0)),
            pl.BlockSpec((tt, D), lambda i: (i, 0)),
            pl.BlockSpec((FFN_TILES, tt, FFN_SHARD), lambda i: (0, i, 0)),
            pl.BlockSpec((FFN_TILES, tt, FFN_SHARD), lambda i: (0, i, 0)),
        ],
        out_shape=[
            jax.ShapeDtypeStruct((T, D), F32),
            jax.ShapeDtypeStruct((T, D), BF16),
            jax.ShapeDtypeStruct((FFN_TILES, T, FFN_SHARD), BF16),
            jax.ShapeDtypeStruct((FFN_TILES, T, FFN_SHARD), BF16),
        ],
        compiler_params=_cp("parallel"),
        name=f"ffn{tag}_fwd_{layer}",
    )(h, g_row, w_in, w_out)


def ffn_bwd(hn, a, b, dres, w_in, w_out, layer, tag=""):
    tt = FFN_TT
    nt = T // tt

    def body(hn_ref, a_ref, b_ref, dres_ref, win_ref, wout_ref, dhn_ref, dwin_ref, dwout_ref, gin_ref, gout_ref,
             do_s, act_s, da_s, db_s):
        i = pl.program_id(1)
        wo = wout_ref[...].reshape(FFN_SHARD, D)
        half = tt // 2
        for r0 in (0, half):
            rows = slice(r0, r0 + half)
            do_h = (0.5 * dres_ref[rows, :]).astype(BF16)
            do_s[rows, :] = do_h
            dact = _dot_nt(do_h, wo)
            a = a_ref[rows, :].astype(F32)
            b = b_ref[rows, :].astype(F32)
            sig = _sigmoid(a)
            s = a * sig
            da_h = (dact * b * _dsilu(a, sig)).astype(BF16)
            db_h = (dact * s).astype(BF16)
            act_s[rows, :] = (s * b).astype(BF16)
            da_s[rows, :] = da_h
            db_s[rows, :] = db_h
            dhn_ref[rows, :] = (_dot(da_h, win_ref[0]) + _dot(db_h, win_ref[1])).astype(BF16)
        do, act, da, db = do_s[...], act_s[...], da_s[...], db_s[...]
        hn = hn_ref[...]
        gwo = _dot_tn(act, do)
        gwg = _dot_tn(da, hn)
        gwu = _dot_tn(db, hn)

        @pl.when(i == 0)
        def _():
            gout_ref[...] = gwo
            gin_ref[0] = gwg
            gin_ref[1] = gwu

        @pl.when(i > 0)
        def _():
            gout_ref[...] += gwo
            gin_ref[0] += gwg
            gin_ref[1] += gwu

        @pl.when(i == nt - 1)
        def _():
            dwin_ref[...] = gin_ref[...].astype(BF16)
            dwout_ref[...] = gout_ref[...].astype(BF16)

    return pl.pallas_call(
        body,
        grid=(FFN_TILES, nt),
        in_specs=[
            pl.BlockSpec((tt, D), lambda j, i: (i, 0)),
            pl.BlockSpec((None, tt, FFN_SHARD), lambda j, i: (j, i, 0)),
            pl.BlockSpec((None, tt, FFN_SHARD), lambda j, i: (j, i, 0)),
            pl.BlockSpec((tt, D), lambda j, i: (i, 0)),
            pl.BlockSpec((2, None, FFN_SHARD, D), lambda j, i: (0, j, 0, 0)),
            pl.BlockSpec((2, FFN_SHARD // 2, D), lambda j, i: (j, 0, 0)),
        ],
        out_specs=[
            pl.BlockSpec((None, tt, D), lambda j, i: (j, i, 0)),
            pl.BlockSpec((2, None, FFN_SHARD, D), lambda j, i: (0, j, 0, 0)),
            pl.BlockSpec((None, FFN_SHARD, D), lambda j, i: (j, 0, 0)),
        ],
        out_shape=[
            jax.ShapeDtypeStruct((FFN_TILES, T, D), BF16),
            jax.ShapeDtypeStruct((2, FFN_TILES, FFN_SHARD, D), BF16),
            jax.ShapeDtypeStruct((FFN_TILES, FFN_SHARD, D), BF16),
        ],
        scratch_shapes=[pltpu.VMEM((2, FFN_SHARD, D), F32), pltpu.VMEM((FFN_SHARD, D), F32),
                        pltpu.VMEM((tt, D), BF16),
                        pltpu.VMEM((tt, FFN_SHARD), BF16), pltpu.VMEM((tt, FFN_SHARD), BF16),
                        pltpu.VMEM((tt, FFN_SHARD), BF16)],
        compiler_params=_cp("parallel", "arbitrary"),
        name=f"ffn{tag}_bwd_{layer}",
    )(hn, a, b, dres, w_in, w_out)


def _rms_bwd(x, g, dy):
    rstd = lax.rsqrt(jnp.mean(x * x, axis=-1, keepdims=True) + EPS)
    xh = x * rstd
    u = dy * g
    dx = rstd * (u - xh * jnp.mean(u * xh, axis=-1, keepdims=True))
    return dx, jnp.sum(dy * xh, axis=0, keepdims=True)


def norm_bwd(x, g_row, dy_parts, dres, name):
    p = dy_parts.shape[0]
    tt = 512

    def body(x_ref, g_ref, dy_ref, dres_ref, out_ref, dg_ref):
        i = pl.program_id(0)
        dy = dy_ref[0].astype(F32)
        for q in range(1, p):
            dy = dy + dy_ref[q].astype(F32)
        dx, dg = _rms_bwd(x_ref[...], g_ref[...], dy)
        out_ref[...] = dres_ref[...] + dx

        @pl.when(i == 0)
        def _():
            dg_ref[...] = dg

        @pl.when(i > 0)
        def _():
            dg_ref[...] += dg

    return pl.pallas_call(
        body,
        grid=(T // tt,),
        in_specs=[
            pl.BlockSpec((tt, D), lambda i: (i, 0)),
            pl.BlockSpec((1, D), lambda i: (0, 0)),
            pl.BlockSpec((p, tt, D), lambda i: (0, i, 0)),
            pl.BlockSpec((tt, D), lambda i: (i, 0)),
        ],
        out_specs=[pl.BlockSpec((tt, D), lambda i: (i, 0)), pl.BlockSpec((1, D), lambda i: (0, 0))],
        out_shape=[jax.ShapeDtypeStruct((T, D), F32), jax.ShapeDtypeStruct((1, D), F32)],
        compiler_params=_cp("arbitrary"),
        name=name,
    )(x, g_row, dy_parts, dres)


def final_loss(h, g_row, target):
    tt = 512

    def body(h_ref, g_ref, t_ref, dres_ref, dg_ref, loss_ref):
        i = pl.program_id(0)
        x = h_ref[...]
        g = g_ref[...]
        err = _rms(x, g) - t_ref[...]
        part = 0.5 * jnp.sum(jnp.mean(err * err, axis=-1, keepdims=True), axis=0, keepdims=True)
        dx, dg = _rms_bwd(x, g, err * (1.0 / D))
        dres_ref[...] = dx
        part = jnp.broadcast_to(part, loss_ref.shape)

        @pl.when(i == 0)
        def _():
            dg_ref[...] = dg
            loss_ref[...] = part

        @pl.when(i > 0)
        def _():
            dg_ref[...] += dg
            loss_ref[...] += part

    return pl.pallas_call(
        body,
        grid=(T // tt,),
        in_specs=[
            pl.BlockSpec((tt, D), lambda i: (i, 0)),
            pl.BlockSpec((1, D), lambda i: (0, 0)),
            pl.BlockSpec((tt, D), lambda i: (i, 0)),
        ],
        out_specs=[
            pl.BlockSpec((tt, D), lambda i: (i, 0)),
            pl.BlockSpec((1, D), lambda i: (0, 0)),
            pl.BlockSpec((8, 128), lambda i: (0, 0)),
        ],
        out_shape=[
            jax.ShapeDtypeStruct((T, D), F32),
            jax.ShapeDtypeStruct((1, D), F32),
            jax.ShapeDtypeStruct((8, 128), F32),
        ],
        compiler_params=_cp("arbitrary"),
        name="final_loss",
    )(h, g_row, target)


PROJ_TT = 256


def rope_tables(pos_col, invf_row):
    tt = 512

    def body(p_ref, f_ref, c_ref, sm_ref, sp_ref):
        ang = p_ref[...].astype(F32) * f_ref[...]
        lane = lax.broadcasted_iota(jnp.int32, ang.shape, 1) % A_HD
        cs = jnp.cos(ang)
        sn = jnp.sin(ang)
        c_ref[...] = jnp.where(lane < ROT, cs, 1.0)
        sm_ref[...] = jnp.where(lane < ROT // 2, -sn, 0.0)
        sp_ref[...] = jnp.where((lane >= ROT // 2) & (lane < ROT), sn, 0.0)

    spec = pl.BlockSpec((tt, 128), lambda i: (i, 0))
    return pl.pallas_call(
        body,
        grid=(T // tt,),
        in_specs=[pl.BlockSpec((tt, 1), lambda i: (i, 0)), pl.BlockSpec((1, 128), lambda i: (0, 0))],
        out_specs=[spec, spec, spec],
        out_shape=[jax.ShapeDtypeStruct((T, 128), F32)] * 3,
        compiler_params=_cp("parallel"),
        name="rope_tables",
    )(pos_col, invf_row)


def make_rope(positions):
    inv_freq = jnp.power(jnp.float32(ROPE_THETA), -jnp.arange(0, ROT, 2, dtype=F32) / ROT)
    per_head = jnp.concatenate([inv_freq, inv_freq, jnp.zeros((A_HD - ROT,), F32)])
    invf_row = jnp.tile(per_head, 2)[None, :]
    return tuple(rope_tables(positions.reshape(T, 1), invf_row))


def _rope(x, c, sm, sp):
    return x * c + pltpu.roll(x, 128 - ROT // 2, 1) * sm + pltpu.roll(x, ROT // 2, 1) * sp


def _rope_t(dy, c, sm, sp):
    return dy * c + pltpu.roll(dy * sm, ROT // 2, 1) + pltpu.roll(dy * sp, 128 - ROT // 2, 1)


def proj_fwd(h, g_row, w, splits, name, rope=None):
    tt = PROJ_TT
    n = w.shape[1]
    n_rope = 0 if rope is None else 3

    def body(h_ref, g_ref, w_ref, *rest):
        tabs = rest[:n_rope]
        hn_ref = rest[n_rope]
        outs = rest[n_rope + 1:]
        hn = _rms(h_ref[...], g_ref[...]).astype(BF16)
        hn_ref[...] = hn
        for k, ((st, wd), o_ref) in enumerate(zip(splits, outs)):
            if rope is not None and k == 0:
                c, sm, sp = (t[...] for t in tabs)
                for gi in range(wd // 128):
                    r = _dot(hn, w_ref[:, st + 128 * gi:st + 128 * (gi + 1)])
                    if gi < 2 * A_W // 128:
                        r = _rope(r, c, sm, sp)
                    o_ref[:, 128 * gi:128 * (gi + 1)] = r
            else:
                o_ref[...] = _dot(hn, w_ref[:, st:st + wd])

    tab_specs = [pl.BlockSpec((tt, 128), lambda i: (i, 0))] * n_rope
    return pl.pallas_call(
        body,
        grid=(T // tt,),
        in_specs=[
            pl.BlockSpec((tt, D), lambda i: (i, 0)),
            pl.BlockSpec((1, D), lambda i: (0, 0)),
            pl.BlockSpec((D, n), lambda i: (0, 0)),
        ] + tab_specs,
        out_specs=[pl.BlockSpec((tt, D), lambda i: (i, 0))]
        + [pl.BlockSpec((tt, wd), lambda i: (i, 0)) for _, wd in splits],
        out_shape=[jax.ShapeDtypeStruct((T, D), BF16)]
        + [jax.ShapeDtypeStruct((T, wd), F32) for _, wd in splits],
        compiler_params=_cp("parallel"),
        name=name,
    )(h, g_row, w, *(rope or ()))


def proj_bwd_data(x, g_row, w, dparts, splits, dres, name, rope=None, n_rot=0):
    tt = PROJ_TT
    n = w.shape[1]
    n_rope = 0 if rope is None else 3
    k_parts = len(dparts)
    n_main = sum(wd for _, wd in splits if wd % 128 == 0)

    def body(x_ref, g_ref, w_ref, dres_ref, *rest):
        d_refs = rest[:k_parts]
        tabs = rest[k_parts:k_parts + n_rope]
        out_ref, dg_ref, dproj_ref = rest[k_parts + n_rope:k_parts + n_rope + 3]
        i = pl.program_id(0)
        dhn = jnp.zeros((tt, D), F32)
        for k, ((st, wd), d_ref) in enumerate(zip(splits, d_refs)):
            if k < n_rot:
                c, sm, sp = (t[...] for t in tabs)
                for gi in range(wd // 128):
                    cols = slice(st + 128 * gi, st + 128 * (gi + 1))
                    d = _rope_t(d_ref[:, 128 * gi:128 * (gi + 1)], c, sm, sp).astype(BF16)
                    dproj_ref[:, cols] = d
                    dhn = dhn + _dot_nt(d, w_ref[:, cols])
            else:
                d = d_ref[...].astype(BF16)
                if wd % 128 == 0:
                    dproj_ref[:, st:st + wd] = d
                dhn = dhn + _dot_nt(d, w_ref[:, st:st + wd])
        dx, dg = _rms_bwd(x_ref[...], g_ref[...], dhn)
        out_ref[...] = dres_ref[...] + dx

        @pl.when(i == 0)
        def _():
            dg_ref[...] = dg

        @pl.when(i > 0)
        def _():
            dg_ref[...] += dg

    tab_specs = [pl.BlockSpec((tt, 128), lambda i: (i, 0))] * n_rope
    out_specs = [pl.BlockSpec((tt, D), lambda i: (i, 0)), pl.BlockSpec((1, D), lambda i: (0, 0))]
    out_shape = [jax.ShapeDtypeStruct((T, D), F32), jax.ShapeDtypeStruct((1, D), F32)]
    out_specs.append(pl.BlockSpec((tt, n_main), lambda i: (i, 0)))
    out_shape.append(jax.ShapeDtypeStruct((T, n_main), BF16))
    return pl.pallas_call(
        body,
        grid=(T // tt,),
        in_specs=[
            pl.BlockSpec((tt, D), lambda i: (i, 0)),
            pl.BlockSpec((1, D), lambda i: (0, 0)),
            pl.BlockSpec((D, n), lambda i: (0, 0)),
            pl.BlockSpec((tt, D), lambda i: (i, 0)),
        ] + [pl.BlockSpec((tt, wd), lambda i: (i, 0)) for _, wd in splits] + tab_specs,
        out_specs=out_specs,
        out_shape=out_shape,
        compiler_params=_cp("arbitrary"),
        name=name,
    )(x, g_row, w, dres, *dparts, *(rope or ()))


def mm_tn_full(x, d, out_dtype, name):
    k = x.shape[1]
    n = d.shape[1]
    wn = 512

    def body(x_ref, d_ref, o_ref):
        o_ref[...] = _dot_tn(x_ref[...], d_ref[...].astype(BF16)).astype(out_dtype)

    return pl.pallas_call(
        body,
        grid=(n // wn,),
        in_specs=[pl.BlockSpec((T, k), lambda j: (0, 0), pipeline_mode=pl.Buffered(1)),
                  pl.BlockSpec((T, wn), lambda j: (0, j))],
        out_specs=pl.BlockSpec((k, wn), lambda j: (0, j)),
        out_shape=jax.ShapeDtypeStruct((k, n), out_dtype),
        compiler_params=_cp("parallel"),
        name=name,
    )(x, d)


def cols_to_slabs(main, tail, name):
    nm = main.shape[1]
    n = nm + (0 if tail is None else tail.shape[1])
    w = n // N_DEV
    tr = 256

    def body(*refs):
        m_ref, o_ref = refs[0], refs[-1]
        for s in range(N_DEV):
            a, b = w * s, w * (s + 1)
            if b <= nm:
                o_ref[s] = m_ref[:, a:b].astype(BF16)
            else:
                o_ref[s, :, 0:nm - a] = m_ref[:, a:nm].astype(BF16)
                o_ref[s, :, nm - a:w] = refs[1][:, 0:b - nm].astype(BF16)

    arrays = [main] + ([] if tail is None else [tail])
    return pl.pallas_call(
        body,
        grid=(D // tr,),
        in_specs=[pl.BlockSpec((tr, a.shape[1]), lambda i: (i, 0)) for a in arrays],
        out_specs=pl.BlockSpec((N_DEV, tr, w), lambda i: (0, i, 0)),
        out_shape=jax.ShapeDtypeStruct((N_DEV, D, w), BF16),
        compiler_params=_cp("parallel"),
        name=name,
    )(*arrays)


def mm_tn(x, d, name):
    k = x.shape[1]
    n = d.shape[1]
    wn = n if n <= 512 else 512
    tt = 512

    def body(x_ref, d_ref, o_ref):
        i = pl.program_id(1)
        r = _dot_tn(x_ref[...], d_ref[...].astype(BF16))

        @pl.when(i == 0)
        def _():
            o_ref[...] = r

        @pl.when(i > 0)
        def _():
            o_ref[...] += r

    return pl.pallas_call(
        body,
        grid=(n // wn, T // tt),
        in_specs=[pl.BlockSpec((tt, k), lambda j, i: (i, 0)), pl.BlockSpec((tt, wn), lambda j, i: (i, j))],
        out_specs=pl.BlockSpec((k, wn), lambda j, i: (0, j)),
        out_shape=jax.ShapeDtypeStruct((k, n), F32),
        compiler_params=_cp("parallel", "arbitrary"),
        name=name,
    )(x, d)


CONV_RC = 128
CONV_PAD = 32


def hyb_conv_fwd(u, dw_w, dw_b, name):
    def body(ua_ref, ug_ref, w_ref, b_ref, o_ref, xpad):
        xpad[0:CONV_PAD, :] = jnp.zeros((CONV_PAD, 128), F32)
        xpad[CONV_PAD:, :] = ua_ref[...] * _sigmoid(ug_ref[...])
        for r in range(T // CONV_RC):
            acc = jnp.broadcast_to(b_ref[...], (CONV_RC, 128))
            for j in range(CONV_K):
                acc = acc + w_ref[pl.ds(j, 1), :] * xpad[pl.ds(r * CONV_RC + CONV_PAD - (CONV_K - 1) + j, CONV_RC), :]
            o_ref[r * CONV_RC:(r + 1) * CONV_RC, :] = acc

    nb = CONV_C // 128
    return pl.pallas_call(
        body,
        grid=(nb,),
        in_specs=[
            pl.BlockSpec((T, 128), lambda c: (0, c)),
            pl.BlockSpec((T, 128), lambda c: (0, nb + c)),
            pl.BlockSpec((32, 128), lambda c: (0, c)),
            pl.BlockSpec((1, 128), lambda c: (0, c)),
        ],
        out_specs=pl.BlockSpec((T, 128), lambda c: (0, c)),
        out_shape=jax.ShapeDtypeStruct((T, CONV_C), F32),
        scratch_shapes=[pltpu.VMEM((T + CONV_PAD, 128), F32)],
        compiler_params=_cp("parallel"),
        name=name,
    )(u, u, dw_w, dw_b)


def hyb_conv_bwd(dc, u, dw_w, name):
    def body(dc_ref, ua_ref, ug_ref, w_ref, da_ref, dgate_ref, dw_ref, db_ref, xpad, dcpad, dwacc):
        ua = ua_ref[...]
        sig = _sigmoid(ug_ref[...])
        xpad[0:CONV_PAD, :] = jnp.zeros((CONV_PAD, 128), F32)
        xpad[CONV_PAD:, :] = ua * sig
        dcpad[0:T, :] = dc_ref[...]
        dcpad[T:, :] = jnp.zeros((CONV_PAD, 128), F32)
        dwacc[...] = jnp.zeros_like(dwacc)
        dbacc = jnp.zeros((8, 128), F32)
        for r in range(T // CONV_RC):
            r0 = r * CONV_RC
            dcr = dc_ref[r0:r0 + CONV_RC, :]
            dbacc = dbacc + dcr.reshape(CONV_RC // 8, 8, 128).sum(axis=0)
            dglu = jnp.zeros((CONV_RC, 128), F32)
            for j in range(CONV_K):
                dglu = dglu + w_ref[pl.ds(j, 1), :] * dcpad[pl.ds(r0 + (CONV_K - 1) - j, CONV_RC), :]
                prod = dcr * xpad[pl.ds(r0 + CONV_PAD - (CONV_K - 1) + j, CONV_RC), :]
                dwacc[8 * j:8 * j + 8, :] += prod.reshape(CONV_RC // 8, 8, 128).sum(axis=0)
            sg = sig[r0:r0 + CONV_RC, :]
            da_ref[r0:r0 + CONV_RC, :] = dglu * sg
            dgate_ref[r0:r0 + CONV_RC, :] = dglu * ua[r0:r0 + CONV_RC, :] * sg * (1.0 - sg)
        for j in range(CONV_K):
            dw_ref[pl.ds(j, 1), :] = jnp.sum(dwacc[8 * j:8 * j + 8, :], axis=0, keepdims=True)
        dw_ref[pl.ds(CONV_K, 1), :] = jnp.zeros((1, 128), F32)
        db_ref[...] = jnp.sum(dbacc, axis=0, keepdims=True)

    nb = CONV_C // 128
    col = pl.BlockSpec((T, 128), lambda c: (0, c))
    return pl.pallas_call(
        body,
        grid=(nb,),
        in_specs=[col, col, pl.BlockSpec((T, 128), lambda c: (0, nb + c)), pl.BlockSpec((32, 128), lambda c: (0, c))],
        out_specs=[col, col, pl.BlockSpec((32, 128), lambda c: (0, c)), pl.BlockSpec((1, 128), lambda c: (0, c))],
        out_shape=[
            jax.ShapeDtypeStruct((T, CONV_C), F32),
            jax.ShapeDtypeStruct((T, CONV_C), F32),
            jax.ShapeDtypeStruct((32, CONV_C), F32),
            jax.ShapeDtypeStruct((1, CONV_C), F32),
        ],
        scratch_shapes=[
            pltpu.VMEM((T + CONV_PAD, 128), F32),
            pltpu.VMEM((T + CONV_PAD, 128), F32),
            pltpu.VMEM((8 * 32, 128), F32),
        ],
        compiler_params=_cp("parallel"),
        name=name,
    )(dc, u, u, dw_w)


ATT_SCALE = A_HD ** -0.5
N_BLK = T // BLK


def _att_masks():
    i = lax.broadcasted_iota(jnp.int32, (BLK, 2 * BLK), 0)
    j = lax.broadcasted_iota(jnp.int32, (BLK, 2 * BLK), 1)
    band = (j >= i) & (j <= i + BLK)
    i1 = lax.broadcasted_iota(jnp.int32, (BLK, BLK), 0)
    j1 = lax.broadcasted_iota(jnp.int32, (BLK, BLK), 1)
    return band, j1 <= i1


def _att_rows(d, t, first):
    if first:
        base = t
        return pl.ds(base, BLK, stride=d), pl.ds(base, BLK, stride=d)
    c = t % d
    n = t // d + 1
    base = c + (BLK * d) * n
    return pl.ds(base, BLK, stride=d), pl.ds(base - BLK * d, 2 * BLK, stride=d)


def _loop_pairs(n, block, per=2):
    def several(i, carry):
        for k in range(per):
            block(per * i + k, carry)
        return carry

    if n >= per:
        lax.fori_loop(0, n // per, several, 0)
    for t in range(n - n % per, n):
        block(t, 0)


def attn_fwd(qkv, name):
    def body(q_ref, k_ref, v_ref, o_ref, lse_ref, og, lg):
        band, tri = _att_masks()
        head0 = lax.broadcasted_iota(jnp.int32, (BLK, 128), 1) < A_HD
        for g, d in enumerate(DILATIONS):
            def block(t, carry, first, g=g, d=d):
                rq, rk = _att_rows(d, t, first)
                q2 = q_ref[rq, :]
                k2 = k_ref[rk, :].astype(BF16)
                v2 = v_ref[rk, :].astype(BF16)
                o_e, l_e = [], []
                for e in range(2):
                    qe = jnp.where(head0 if e == 0 else ~head0, q2, 0.0).astype(BF16)
                    s = _dot_nt(qe, k2) * ATT_SCALE
                    s = jnp.where(tri if first else band, s, -jnp.inf)
                    m = jnp.max(s, axis=-1, keepdims=True)
                    p = jnp.exp(s - m)
                    den = jnp.sum(p, axis=-1, keepdims=True)
                    o_e.append(_dot(p.astype(BF16), v2) / den)
                    l_e.append(m + jnp.log(den))
                og[g, rq, :] = jnp.where(head0, o_e[0], o_e[1])
                lg[g, rq, :] = jnp.where(head0, l_e[0], l_e[1])
                return carry

            _loop_pairs(d, functools.partial(block, first=True), per=4)
            _loop_pairs(N_BLK - d, functools.partial(block, first=False), per=4)
        rc = 256
        for r in range(T // rc):
            rows = pl.ds(r * rc, rc)
            l0, l1, l2 = lg[0, rows, :], lg[1, rows, :], lg[2, rows, :]
            m = jnp.maximum(jnp.maximum(l0, l1), l2)
            e0, e1, e2 = jnp.exp(l0 - m), jnp.exp(l1 - m), jnp.exp(l2 - m)
            z = e0 + e1 + e2
            o_ref[rows, :] = (e0 / z) * og[0, rows, :] + (e1 / z) * og[1, rows, :] + (e2 / z) * og[2, rows, :]
            lse_ref[rows, :] = m + jnp.log(z)

    npair = A_HEADS // 2
    col = lambda off: pl.BlockSpec((T, 128), lambda p: (0, off + p))
    return pl.pallas_call(
        body,
        grid=(npair,),
        in_specs=[col(0), col(npair), col(2 * npair)],
        out_specs=[col(0), col(0)],
        out_shape=[jax.ShapeDtypeStruct((T, A_W), F32), jax.ShapeDtypeStruct((T, A_W), F32)],
        scratch_shapes=[pltpu.VMEM((3, T, 128), F32), pltpu.VMEM((3, T, 128), F32)],
        compiler_params=_cp("parallel"),
        name=name,
    )(qkv, qkv, qkv)


def attn_bwd(qkv, o, lse, do, name):
    def body(q_ref, k_ref, v_ref, o_ref, lse_ref, do_ref, dq_ref, dk_ref, dv_ref):
        band, tri = _att_masks()
        head0 = lax.broadcasted_iota(jnp.int32, (BLK, 128), 1) < A_HD
        head0k = lax.broadcasted_iota(jnp.int32, (2 * BLK, 128), 1) < A_HD
        dq_ref[...] = jnp.zeros_like(dq_ref)
        dk_ref[...] = jnp.zeros_like(dk_ref)
        dv_ref[...] = jnp.zeros_like(dv_ref)
        for d in DILATIONS:
            def block(t, carry, first, d=d):
                rq, rk = _att_rows(d, t, first)
                q2 = q_ref[rq, :]
                k2 = k_ref[rk, :].astype(BF16)
                v2 = v_ref[rk, :].astype(BF16)
                do2 = do_ref[rq, :]
                l2 = lse_ref[rq, :]
                prod = do2 * o_ref[rq, :]
                q2b = q2.astype(BF16)
                do2b = do2.astype(BF16)
                dq_e, dk_e, dv_e = [], [], []
                for e in range(2):
                    he = head0 if e == 0 else ~head0
                    qe = jnp.where(he, q2, 0.0).astype(BF16)
                    doe = jnp.where(he, do2, 0.0).astype(BF16)
                    l = l2[:, A_HD * e:A_HD * e + 1]
                    dd = jnp.sum(jnp.where(he, prod, 0.0), axis=-1, keepdims=True)
                    s = _dot_nt(qe, k2) * ATT_SCALE
                    p = jnp.where(tri if first else band, jnp.exp(s - l), 0.0)
                    dp = _dot_nt(doe, v2)
                    ds = (p * (dp - dd) * ATT_SCALE).astype(BF16)
                    dq_e.append(_dot(ds, k2))
                    dk_e.append(_dot_tn(ds, q2b))
                    dv_e.append(_dot_tn(p.astype(BF16), do2b))
                hk = head0 if first else head0k
                dq_ref[rq, :] += jnp.where(head0, dq_e[0], dq_e[1])
                dk_ref[rk, :] += jnp.where(hk, dk_e[0], dk_e[1])
                dv_ref[rk, :] += jnp.where(hk, dv_e[0], dv_e[1])
                return carry

            _loop_pairs(d, functools.partial(block, first=True), per=4)
            _loop_pairs(N_BLK - d, functools.partial(block, first=False), per=4)

    npair = A_HEADS // 2
    col = lambda off: pl.BlockSpec((T, 128), lambda p: (0, off + p))
    return pl.pallas_call(
        body,
        grid=(npair,),
        in_specs=[col(0), col(npair), col(2 * npair), col(0), col(0), col(0)],
        out_specs=[col(0), col(0), col(0)],
        out_shape=[jax.ShapeDtypeStruct((T, A_W), F32)] * 3,
        compiler_params=_cp("parallel"),
        name=name,
    )(qkv, qkv, qkv, o, lse, do)


def _ln_silu(x, g, b):
    mu = jnp.mean(x, axis=-1, keepdims=True)
    xc = x - mu
    rstd = lax.rsqrt(jnp.mean(xc * xc, axis=-1, keepdims=True) + EPS)
    xh = xc * rstd
    y = xh * g + b
    sig = _sigmoid(y)
    return y * sig, (xh, rstd, y, sig)


def hyb_out_fwd(h, attn, cpre, ln_g, ln_b, w_out, name):
    tt = 512

    def body(h_ref, a_ref, c_ref, g_ref, b_ref, w_ref, hnew_ref, cat_ref):
        cn, _ = _ln_silu(c_ref[...], g_ref[...], b_ref[...])
        ab = a_ref[...].astype(BF16)
        cb = cn.astype(BF16)
        cat_ref[:, 0:A_W] = ab
        cat_ref[:, A_W:D] = cb
        hnew_ref[...] = h_ref[...] + _dot(ab, w_ref[0:A_W, :]) + _dot(cb, w_ref[A_W:D, :])

    half = pl.BlockSpec((tt, A_W), lambda i: (i, 0))
    vec = pl.BlockSpec((1, CONV_C), lambda i: (0, 0))
    full = pl.BlockSpec((tt, D), lambda i: (i, 0))
    return pl.pallas_call(
        body,
        grid=(T // tt,),
        in_specs=[full, half, half, vec, vec, pl.BlockSpec((D, D), lambda i: (0, 0))],
        out_specs=[full, full],
        out_shape=[jax.ShapeDtypeStruct((T, D), F32), jax.ShapeDtypeStruct((T, D), BF16)],
        compiler_params=_cp("parallel"),
        name=name,
    )(h, attn, cpre, ln_g, ln_b, w_out)


def hyb_out_bwd(dres, cpre, ln_g, ln_b, w_out, name):
    tt = 512

    def body(d_ref, c_ref, g_ref, b_ref, w_ref, da_ref, dc_ref, dg_ref, db_ref):
        i = pl.program_id(0)
        db16 = d_ref[...].astype(BF16)
        da_ref[...] = _dot_nt(db16, w_ref[0:A_W, :])
        dcn = _dot_nt(db16, w_ref[A_W:D, :])
        g = g_ref[...]
        _, (xh, rstd, y, sig) = _ln_silu(c_ref[...], g, b_ref[...])
        dy = dcn * _dsilu(y, sig)
        dxh = dy * g
        dc_ref[...] = rstd * (dxh - jnp.mean(dxh, axis=-1, keepdims=True)
                              - xh * jnp.mean(dxh * xh, axis=-1, keepdims=True))
        dg = jnp.sum(dy * xh, axis=0, keepdims=True)
        db = jnp.sum(dy, axis=0, keepdims=True)

        @pl.when(i == 0)
        def _():
            dg_ref[...] = dg
            db_ref[...] = db

        @pl.when(i > 0)
        def _():
            dg_ref[...] += dg
            db_ref[...] += db

    half = pl.BlockSpec((tt, A_W), lambda i: (i, 0))
    vec = pl.BlockSpec((1, CONV_C), lambda i: (0, 0))
    return pl.pallas_call(
        body,
        grid=(T // tt,),
        in_specs=[pl.BlockSpec((tt, D), lambda i: (i, 0)), half, vec, vec, pl.BlockSpec((D, D), lambda i: (0, 0))],
        out_specs=[half, half, vec, vec],
        out_shape=[
            jax.ShapeDtypeStruct((T, A_W), F32),
            jax.ShapeDtypeStruct((T, CONV_C), F32),
            jax.ShapeDtypeStruct((1, CONV_C), F32),
            jax.ShapeDtypeStruct((1, CONV_C), F32),
        ],
        compiler_params=_cp("arbitrary"),
        name=name,
    )(dres, cpre, ln_g, ln_b, w_out)


def hybrid_fwd(h, g_row, w_in, dw_w, dw_b, ln_g, ln_b, w_out, rope, tag):
    hn, qkv, u = proj_fwd(h, g_row, w_in, [(0, 3 * A_W), (3 * A_W, 2 * CONV_C)], f"hyb_proj_{tag}", rope=rope)
    cpre = hyb_conv_fwd(u, dw_w, dw_b, f"hyb_conv_{tag}")
    attn, lse = attn_fwd(qkv, f"attn_fwd_{tag}")
    hnew, cat = hyb_out_fwd(h, attn, cpre, ln_g, ln_b, w_out, f"hyb_out_{tag}")
    return hnew, (h, hn, qkv, u, cpre, attn, lse, cat)


def hybrid_bwd(dres, saved, g_row, w_in, dw_w, ln_g, ln_b, w_out, rope, tag):
    h, hn, qkv, u, cpre, attn, lse, cat = saved
    d_attn, d_cpre, d_lng, d_lnb = hyb_out_bwd(dres, cpre, ln_g, ln_b, w_out, f"hyb_out_bwd_{tag}")
    d_wout = mm_tn_full(cat, dres, BF16, f"hyb_wout_grad_{tag}")
    d_a, d_gate, d_dw, d_db = hyb_conv_bwd(d_cpre, u, dw_w, f"hyb_conv_bwd_{tag}")
    dq, dk, dv = attn_bwd(qkv, attn, lse, d_attn, f"attn_bwd_{tag}")
    splits = [(0, A_W), (A_W, A_W), (2 * A_W, A_W), (3 * A_W, CONV_C), (3 * A_W + CONV_C, CONV_C)]
    dres_new, d_norm, dproj = proj_bwd_data(
        h, g_row, w_in, [dq, dk, dv, d_a, d_gate], splits, dres, f"hyb_proj_bwd_{tag}", rope=rope, n_rot=2)
    d_win = cols_to_slabs(mm_tn_full(hn, dproj, F32, f"hyb_win_grad_{tag}"), None, f"hyb_win_slabs_{tag}")
    return dres_new, dict(norm=d_norm, w_in=d_win, dw_w=d_dw[:CONV_K], dw_b=d_db, ln_g=d_lng, ln_b=d_lnb, w_out=d_wout)


G_SCALE = G_DK ** -0.5
GP_RC = 256
GP_PAD = 8


def gdn_prep_fwd(x, conv_w, name):
    def body(x_ref, w_ref, o_ref, xpad):
        cb = pl.program_id(0)
        xpad[0:GP_PAD, :] = jnp.zeros((GP_PAD, 128), F32)
        xpad[GP_PAD:, :] = x_ref[...]
        for r in range(T // GP_RC):
            r0 = r * GP_RC
            y = jnp.zeros((GP_RC, 128), F32)
            for j in range(G_CONV):
                y = y + w_ref[pl.ds(j, 1), :] * xpad[pl.ds(r0 + GP_PAD - (G_CONV - 1) + j, GP_RC), :]
            s = y * _sigmoid(y)
            n = lax.rsqrt(jnp.sum(s * s, axis=-1, keepdims=True) + EPS)
            o_ref[r0:r0 + GP_RC, :] = s * jnp.where(cb < 2 * G_HEADS, n, 1.0)

    nb = G_QKV // 128
    return pl.pallas_call(
        body,
        grid=(nb,),
        in_specs=[pl.BlockSpec((T, 128), lambda c: (0, c)), pl.BlockSpec((G_CONV, 128), lambda c: (0, c))],
        out_specs=pl.BlockSpec((T, 128), lambda c: (0, c)),
        out_shape=jax.ShapeDtypeStruct((T, G_QKV), F32),
        scratch_shapes=[pltpu.VMEM((T + GP_PAD, 128), F32)],
        compiler_params=_cp("parallel"),
        name=name,
    )(x, conv_w)


def gdn_prep_bwd(dout, x, conv_w, part, l2, name):
    def body(d_ref, x_ref, w_ref, dx_ref, dw_ref, xpad, dypad, dwacc):
        xpad[0:GP_PAD, :] = jnp.zeros((GP_PAD, 128), F32)
        xpad[GP_PAD:, :] = x_ref[...]
        dypad[T:, :] = jnp.zeros((GP_PAD, 128), F32)
        dwacc[...] = jnp.zeros_like(dwacc)
        for r in range(T // GP_RC):
            r0 = r * GP_RC
            y = jnp.zeros((GP_RC, 128), F32)
            xs = []
            for j in range(G_CONV):
                xj = xpad[pl.ds(r0 + GP_PAD - (G_CONV - 1) + j, GP_RC), :]
                xs.append(xj)
                y = y + w_ref[pl.ds(j, 1), :] * xj
            sig = _sigmoid(y)
            s = y * sig
            d = d_ref[r0:r0 + GP_RC, :]
            if l2:
                n = lax.rsqrt(jnp.sum(s * s, axis=-1, keepdims=True) + EPS)
                out = s * n
                d = n * (d - out * jnp.sum(d * out, axis=-1, keepdims=True))
            dy = d * _dsilu(y, sig)
            dypad[r0:r0 + GP_RC, :] = dy
            for j in range(G_CONV):
                dwacc[8 * j:8 * j + 8, :] += (dy * xs[j]).reshape(GP_RC // 8, 8, 128).sum(axis=0)
        for r in range(T // GP_RC):
            r0 = r * GP_RC
            dx = jnp.zeros((GP_RC, 128), F32)
            for j in range(G_CONV):
                dx = dx + w_ref[pl.ds(j, 1), :] * dypad[pl.ds(r0 + (G_CONV - 1) - j, GP_RC), :]
            dx_ref[r0:r0 + GP_RC, :] = dx
        for j in range(G_CONV):
            dw_ref[pl.ds(j, 1), :] = jnp.sum(dwacc[8 * j:8 * j + 8, :], axis=0, keepdims=True)

    nb = G_HEADS
    off = part * nb
    col = pl.BlockSpec((T, 128), lambda c: (0, c))
    return pl.pallas_call(
        body,
        grid=(nb,),
        in_specs=[col, pl.BlockSpec((T, 128), lambda c: (0, off + c)), pl.BlockSpec((G_CONV, 128), lambda c: (0, off + c))],
        out_specs=[col, pl.BlockSpec((G_CONV, 128), lambda c: (0, c))],
        out_shape=[jax.ShapeDtypeStruct((T, G_HEADS * G_DK), F32), jax.ShapeDtypeStruct((G_CONV, G_HEADS * G_DK), F32)],
        scratch_shapes=[
            pltpu.VMEM((T + GP_PAD, 128), F32),
            pltpu.VMEM((T + GP_PAD, 128), F32),
            pltpu.VMEM((8 * G_CONV, 128), F32),
        ],
        compiler_params=_cp("parallel"),
        name=name,
    )(dout, x, conv_w)


def _seg_cumsum(x, reverse=False):
    row = lax.broadcasted_iota(jnp.int32, x.shape, 0) % CH
    s = 1
    while s < CH:
        if reverse:
            x = x + jnp.where(row < CH - s, pltpu.roll(x, x.shape[0] - s, 0), 0.0)
        else:
            x = x + jnp.where(row >= s, pltpu.roll(x, s, 0), 0.0)
        s *= 2
    return x


def _gdn_gates(ba_ref, alog_ref, dt_ref, h):
    ba = ba_ref[...]
    lane = lax.broadcasted_iota(jnp.int32, ba.shape, 1)
    b_col = jnp.sum(jnp.where(lane == h, ba, 0.0), axis=1, keepdims=True)
    a_col = jnp.sum(jnp.where(lane == G_HEADS + h, ba, 0.0), axis=1, keepdims=True)
    lane8 = lax.broadcasted_iota(jnp.int32, (1, G_HEADS), 1)
    alog = jnp.sum(jnp.where(lane8 == h, alog_ref[...], 0.0), axis=1, keepdims=True)
    dt = jnp.sum(jnp.where(lane8 == h, dt_ref[...], 0.0), axis=1, keepdims=True)
    beta = _sigmoid(b_col)
    xa = a_col + dt
    softplus = jnp.maximum(xa, 0.0) + jnp.log(1.0 + jnp.exp(-jnp.abs(xa)))
    ea = jnp.exp(alog)
    return beta, -ea * softplus, xa, ea


def _chunk_masks():
    i = lax.broadcasted_iota(jnp.int32, (CH, CH), 0)
    j = lax.broadcasted_iota(jnp.int32, (CH, CH), 1)
    return i >= j, i > j, i, j


def _decay(gcc, causal):
    gm = gcc[:, 0:CH]
    return jnp.where(causal, jnp.exp(jnp.minimum(gm - gm.T, 0.0)), 0.0)


def _split(a):
    hi = a.astype(BF16)
    return hi, (a - hi.astype(F32)).astype(BF16)


def _dot3(a, b):
    ah, al = _split(a)
    bh, bl = _split(b)
    return _dot(ah, bh) + (_dot(ah, bl) + _dot(al, bh))


def _unit_lower_inverse(lms, i, j):
    eye = jnp.where(i == j, 1.0, 0.0)
    ms = [None] * len(lms)
    b = 1
    while b < CH:
        pair = ((i // (2 * b)) == (j // (2 * b))) & ((i // b) % 2 == 1) & ((j // b) % 2 == 0)
        lbs = [jnp.where(pair, lm, 0.0) for lm in lms]
        if b == 1:
            ms = [eye - lb for lb in lbs]
        else:
            ts = [_dot3(m, lb) for m, lb in zip(ms, lbs)]
            ms = [m - _dot3(t, m) for m, t in zip(ms, ts)]
        b *= 2
    return ms


def gdn_local_fwd(qkv, ba, alog, dtb, name):
    def body(q_ref, k_ref, v_ref, ba_ref, al_ref, dt_ref, u_ref, w_ref, qd_ref, kd_ref, at_ref, el_ref, ti_ref, gcs):
        h = pl.program_id(1)
        beta, g, _, _ = _gdn_gates(ba_ref, al_ref, dt_ref, h)
        gc = _seg_cumsum(jnp.broadcast_to(g, (GRP, 128)))
        gcs[...] = gc
        causal, strict, i, j = _chunk_masks()
        lms = []
        for c in range(CPG):
            r = slice(c * CH, (c + 1) * CH)
            q, k = q_ref[r, :], k_ref[r, :]
            gcc = gc[r, :]
            ec = jnp.exp(gcc)
            gl = gcs[pl.ds(c * CH + CH - 1, 1), :]
            dm = _decay(gcc, causal)
            kbf = k.astype(BF16)
            a1 = _dot_nt((k * beta[r, :]).astype(BF16), kbf)
            lms.append(jnp.where(strict, a1 * dm, 0.0))
            qs = q * G_SCALE
            qd_ref[r, :] = (qs * ec).astype(BF16)
            kd_ref[r, :] = (k * jnp.exp(gl - gcc)).astype(BF16)
            at_ref[r, :] = (_dot_nt(qs.astype(BF16), kbf) * dm).astype(BF16)
            el_ref[pl.ds(c, 1), :] = jnp.exp(gl)
        tinvs = _unit_lower_inverse(lms, i, j)
        for c in range(CPG):
            r = slice(c * CH, (c + 1) * CH)
            bt = beta[r, :]
            tb = tinvs[c].astype(BF16)
            u_ref[r, :] = _dot(tb, (v_ref[r, :] * bt).astype(BF16))
            w_ref[r, :] = _dot(tb, (k_ref[r, :] * bt * jnp.exp(gc[r, :])).astype(BF16)).astype(BF16)
            ti_ref[r, :] = tinvs[c]

    hd = lambda off: pl.BlockSpec((GRP, 128), lambda i, h: (i, off + h))
    vec = pl.BlockSpec((1, G_HEADS), lambda i, h: (0, 0))
    sq = pl.BlockSpec((None, GRP, CH), lambda i, h: (h, i, 0))
    return pl.pallas_call(
        body,
        grid=(N_GRP, G_HEADS),
        in_specs=[hd(0), hd(G_HEADS), hd(2 * G_HEADS), pl.BlockSpec((GRP, 2 * G_HEADS), lambda i, h: (i, 0)), vec, vec],
        out_specs=[hd(0), hd(0), hd(0), hd(0), sq, pl.BlockSpec((None, CPG, 128), lambda i, h: (h, i, 0)), sq],
        out_shape=[
            jax.ShapeDtypeStruct((T, D), F32),
            jax.ShapeDtypeStruct((T, D), BF16),
            jax.ShapeDtypeStruct((T, D), BF16),
            jax.ShapeDtypeStruct((T, D), BF16),
            jax.ShapeDtypeStruct((G_HEADS, T, CH), BF16),
            jax.ShapeDtypeStruct((G_HEADS, T // CH, 128), F32),
            jax.ShapeDtypeStruct((G_HEADS, T, CH), F32),
        ],
        scratch_shapes=[pltpu.VMEM((GRP, 128), F32)],
        compiler_params=_cp("parallel", "parallel"),
        name=name,
    )(qkv, qkv, qkv, ba, alog, dtb)


def gdn_rec_fwd(u, w, qd, kd, at, el, name):
    def body(u_ref, w_ref, qd_ref, kd_ref, at_ref, el_ref, o_ref, vn_ref, st_ref, s_scr):
        @pl.when(pl.program_id(0) == 0)
        def _():
            s_scr[...] = jnp.zeros_like(s_scr)

        states = [s_scr[h] for h in range(G_HEADS)]
        for c in range(CPG):
            r = slice(c * CH, (c + 1) * CH)
            for h in range(G_HEADS):
                ln = slice(h * 128, (h + 1) * 128)
                s = states[h]
                st_ref[h, c] = s
                sb = s.astype(BF16)
                vn = (u_ref[r, ln] - _dot(w_ref[r, ln], sb)).astype(BF16)
                o_ref[r, ln] = _dot(qd_ref[r, ln], sb) + _dot(at_ref[h, r, :], vn)
                states[h] = s * el_ref[h, pl.ds(c, 1), :] + _dot_tn(kd_ref[r, ln], vn)
                vn_ref[r, ln] = vn
        for h in range(G_HEADS):
            s_scr[h] = states[h]

    row = pl.BlockSpec((GRP, D), lambda i: (i, 0))
    return pl.pallas_call(
        body,
        grid=(N_GRP,),
        in_specs=[row, row, row, row, pl.BlockSpec((G_HEADS, GRP, CH), lambda i: (0, i, 0)),
                  pl.BlockSpec((G_HEADS, CPG, 128), lambda i: (0, i, 0))],
        out_specs=[row, row, pl.BlockSpec((G_HEADS, CPG, 128, 128), lambda i: (0, i, 0, 0))],
        out_shape=[
            jax.ShapeDtypeStruct((T, D), F32),
            jax.ShapeDtypeStruct((T, D), BF16),
            jax.ShapeDtypeStruct((G_HEADS, T // CH, 128, 128), F32),
        ],
        scratch_shapes=[pltpu.VMEM((G_HEADS, 128, 128), F32)],
        compiler_params=_cp("arbitrary"),
        name=name,
    )(u, w, qd, kd, at, el)


def gdn_rec_bwd(do, w, qd, kd, at, el, vn, st, name):
    def body(do_ref, w_ref, qd_ref, kd_ref, at_ref, el_ref, vn_ref, st_ref,
             du_ref, dw_ref, dqd_ref, dkd_ref, dat_ref, del_ref, ds_scr):
        @pl.when(pl.program_id(0) == 0)
        def _():
            ds_scr[...] = jnp.zeros_like(ds_scr)

        dstates = [ds_scr[h] for h in range(G_HEADS)]
        for c in reversed(range(CPG)):
            r = slice(c * CH, (c + 1) * CH)
            for h in range(G_HEADS):
                ln = slice(h * 128, (h + 1) * 128)
                ds = dstates[h]
                dsb = ds.astype(BF16)
                sn = st_ref[h, c]
                snb = sn.astype(BF16)
                dob = do_ref[r, ln].astype(BF16)
                vnb = vn_ref[r, ln]
                dvn = (_dot(kd_ref[r, ln], dsb) + _dot_tn(at_ref[h, r, :], dob)).astype(BF16)
                du_ref[r, ln] = dvn
                dkd_ref[r, ln] = _dot_nt(vnb, dsb)
                tot = jnp.sum(jnp.sum(ds * sn, axis=1, keepdims=True), axis=0, keepdims=True)
                del_ref[h, pl.ds(c, 1), :] = jnp.broadcast_to(tot, (1, 128))
                dqd_ref[r, ln] = _dot_nt(dob, snb)
                dat_ref[h, r, :] = _dot_nt(dob, vnb)
                dw_ref[r, ln] = (-_dot_nt(dvn, snb)).astype(BF16)
                dstates[h] = ds * el_ref[h, pl.ds(c, 1), :] + _dot_tn(qd_ref[r, ln], dob) - _dot_tn(w_ref[r, ln], dvn)
        for h in range(G_HEADS):
            ds_scr[h] = dstates[h]

    last = N_GRP - 1
    row = pl.BlockSpec((GRP, D), lambda i: (last - i, 0))
    sq = pl.BlockSpec((G_HEADS, GRP, CH), lambda i: (0, last - i, 0))
    sc = pl.BlockSpec((G_HEADS, CPG, 128), lambda i: (0, last - i, 0))
    return pl.pallas_call(
        body,
        grid=(N_GRP,),
        in_specs=[row, row, row, row, sq, sc, row, pl.BlockSpec((G_HEADS, CPG, 128, 128), lambda i: (0, last - i, 0, 0))],
        out_specs=[row, row, row, row, sq, sc],
        out_shape=[
            jax.ShapeDtypeStruct((T, D), BF16),
            jax.ShapeDtypeStruct((T, D), BF16),
            jax.ShapeDtypeStruct((T, D), F32),
            jax.ShapeDtypeStruct((T, D), F32),
            jax.ShapeDtypeStruct((G_HEADS, T, CH), F32),
            jax.ShapeDtypeStruct((G_HEADS, T // CH, 128), F32),
        ],
        scratch_shapes=[pltpu.VMEM((G_HEADS, 128, 128), F32)],
        compiler_params=_cp("arbitrary"),
        name=name,
    )(do, w, qd, kd, at, el, vn, st)


def gdn_local_bwd(qkv, ba, alog, dtb, tinv, du, dw, dqd, dkd, dat, dl, name):
    def body(q_ref, k_ref, v_ref, ba_ref, al_ref, dt_ref, ti_ref, du_ref, dw_ref, dqd_ref, dkd_ref, dat_ref, dl_ref,
             dq_ref, dk_ref, dv_ref, dba_ref, dal_ref, ddt_ref, gcs):
        gi = pl.program_id(0)
        h = pl.program_id(1)
        beta, g, xa, ea = _gdn_gates(ba_ref, al_ref, dt_ref, h)
        gc = _seg_cumsum(jnp.broadcast_to(g, (GRP, 128)))
        gcs[...] = gc
        causal, strict, _, _ = _chunk_masks()
        dgc_l, dgl_l, dbeta_l, state = [], [], [], []
        for c in range(CPG):
            r = slice(c * CH, (c + 1) * CH)
            q, k, v = q_ref[r, :], k_ref[r, :], v_ref[r, :]
            bt = beta[r, :]
            gcc = gc[r, :]
            ec = jnp.exp(gcc)
            gl = gcs[pl.ds(c * CH + CH - 1, 1), :]
            f2 = jnp.exp(gl - gcc)
            elc = jnp.exp(gl)
            dm = _decay(gcc, causal)
            qs = q * G_SCALE
            kb = k * bt
            vb = v * bt
            kbe = kb * ec
            kbf, kbb, qsb = k.astype(BF16), kb.astype(BF16), qs.astype(BF16)
            a1 = _dot_nt(kbb, kbf)
            qk = _dot_nt(qsb, kbf)
            ti = ti_ref[r, :]
            tb = ti.astype(BF16)
            du_c, dw_c = du_ref[r, :], dw_ref[r, :]
            dqd_c, dkd_c, dat_c = dqd_ref[r, :], dkd_ref[r, :], dat_ref[r, :]

            dqs = dqd_c * ec
            d_e = jnp.sum(dqd_c * qs, axis=1, keepdims=True)
            dk = dkd_c * f2
            tcol = jnp.sum(dkd_c * k, axis=1, keepdims=True) * f2[:, 0:1]
            dgl = jnp.sum(tcol, axis=0, keepdims=True) + dl_ref[pl.ds(c, 1), 0:1] * elc[:, 0:1]
            dgc = -tcol
            dqk = (dat_c * dm).astype(BF16)
            d_d = dat_c * qk
            dqs = dqs + _dot(dqk, kbf)
            dk = dk + _dot_tn(dqk, qsb)
            dtinv = _dot_nt(du_c, vb.astype(BF16)) + _dot_nt(dw_c, kbe.astype(BF16))
            dvb = _dot_tn(tb, du_c)
            dkbe = _dot_tn(tb, dw_c)
            dq_ref[r, :] = dqs * G_SCALE
            state.append((ti.T, dtinv, dm, a1, dkbe, dvb, d_d, dk, d_e, dgc, dgl))

        xs = [_dot3(st[0], st[1]) for st in state]
        dlms = [jnp.where(strict, -_dot3(x, st[0]), 0.0) for x, st in zip(xs, state)]

        for c in range(CPG):
            r = slice(c * CH, (c + 1) * CH)
            _, _, dm, a1, dkbe, dvb, d_d, dk, d_e, dgc, dgl = state[c]
            dlm = dlms[c]
            k, v = k_ref[r, :], v_ref[r, :]
            bt = beta[r, :]
            ec = jnp.exp(gc[r, :])
            kb = k * bt
            kbf, kbb = k.astype(BF16), kb.astype(BF16)
            da1 = (dlm * dm).astype(BF16)
            d_d = d_d + dlm * a1
            dkb = _dot(da1, kbf) + dkbe * ec
            dk = dk + _dot_tn(da1, kbb)
            d_e = d_e + jnp.sum(dkbe * kb, axis=1, keepdims=True)
            dk = dk + dkb * bt
            dbeta_l.append(jnp.sum(dkb * k, axis=1, keepdims=True) + jnp.sum(dvb * v, axis=1, keepdims=True))
            ddiff = d_d * dm
            dgc = dgc + jnp.sum(ddiff, axis=1, keepdims=True) - jnp.sum(ddiff.T, axis=1, keepdims=True)
            dgc = dgc + d_e * ec[:, 0:1]
            dgc_l.append(dgc)
            dgl_l.append(jnp.broadcast_to(dgl, (CH, 1)))
            dk_ref[r, :] = dk
            dv_ref[r, :] = dvb * bt

        dgc_all = jnp.broadcast_to(jnp.concatenate(dgc_l, axis=0), (GRP, 128))
        dg = _seg_cumsum(dgc_all, reverse=True)[:, 0:1] + jnp.concatenate(dgl_l, axis=0)
        dbeta = jnp.concatenate(dbeta_l, axis=0)
        da = dg * (-ea) * _sigmoid(xa)
        db = dbeta * beta * (1.0 - beta)
        lane = lax.broadcasted_iota(jnp.int32, (GRP, 2 * G_HEADS), 1)
        dba = jnp.where(lane == h, db, 0.0) + jnp.where(lane == G_HEADS + h, da, 0.0)
        lane8 = lax.broadcasted_iota(jnp.int32, (1, G_HEADS), 1)
        dal = jnp.where(lane8 == h, jnp.sum(dg * g, axis=0, keepdims=True), 0.0)
        ddt = jnp.where(lane8 == h, jnp.sum(da, axis=0, keepdims=True), 0.0)

        @pl.when(h == 0)
        def _():
            dba_ref[...] = dba

        @pl.when(h > 0)
        def _():
            dba_ref[...] += dba

        @pl.when((h == 0) & (gi == 0))
        def _():
            dal_ref[...] = dal
            ddt_ref[...] = ddt

        @pl.when((h > 0) | (gi > 0))
        def _():
            dal_ref[...] += dal
            ddt_ref[...] += ddt

    hd = lambda off: pl.BlockSpec((GRP, 128), lambda i, h: (i, off + h))
    vec = pl.BlockSpec((1, G_HEADS), lambda i, h: (0, 0))
    sq = pl.BlockSpec((None, GRP, CH), lambda i, h: (h, i, 0))
    gates = pl.BlockSpec((GRP, 2 * G_HEADS), lambda i, h: (i, 0))
    return pl.pallas_call(
        body,
        grid=(N_GRP, G_HEADS),
        in_specs=[hd(0), hd(G_HEADS), hd(2 * G_HEADS), gates, vec, vec, sq, hd(0), hd(0), hd(0), hd(0), sq,
                  pl.BlockSpec((None, CPG, 128), lambda i, h: (h, i, 0))],
        out_specs=[hd(0), hd(0), hd(0), gates, vec, vec],
        out_shape=[
            jax.ShapeDtypeStruct((T, D), F32),
            jax.ShapeDtypeStruct((T, D), F32),
            jax.ShapeDtypeStruct((T, D), F32),
            jax.ShapeDtypeStruct((T, 2 * G_HEADS), F32),
            jax.ShapeDtypeStruct((1, G_HEADS), F32),
            jax.ShapeDtypeStruct((1, G_HEADS), F32),
        ],
        scratch_shapes=[pltpu.VMEM((GRP, 128), F32)],
        compiler_params=_cp("arbitrary", "arbitrary"),
        name=name,
    )(qkv, qkv, qkv, ba, alog, dtb, tinv, du, dw, dqd, dkd, dat, dl)


def _gated_norm(o, z, g):
    rstd = lax.rsqrt(jnp.mean(o * o, axis=-1, keepdims=True) + EPS)
    oh = o * rstd
    sig = _sigmoid(z)
    return oh, rstd, sig


def gdn_out_fwd(h, o, z, norm_g, w_out, name):
    tt = 512

    def body(h_ref, o_ref, z_ref, g_ref, w_ref, hnew_ref, cat_ref):
        g = g_ref[...]
        for hh in range(G_HEADS):
            ln = slice(hh * 128, (hh + 1) * 128)
            zz = z_ref[:, ln]
            oh, _, sig = _gated_norm(o_ref[:, ln], zz, g)
            cat_ref[:, ln] = (oh * g * (zz * sig)).astype(BF16)
        hnew_ref[...] = h_ref[...] + _dot(cat_ref[...], w_ref[...])

    full = pl.BlockSpec((tt, D), lambda i: (i, 0))
    return pl.pallas_call(
        body,
        grid=(T // tt,),
        in_specs=[full, full, full, pl.BlockSpec((1, 128), lambda i: (0, 0)), pl.BlockSpec((D, D), lambda i: (0, 0))],
        out_specs=[full, full],
        out_shape=[jax.ShapeDtypeStruct((T, D), F32), jax.ShapeDtypeStruct((T, D), BF16)],
        compiler_params=_cp("parallel"),
        name=name,
    )(h, o, z, norm_g, w_out)


def gdn_out_bwd(dres, o, z, norm_g, w_out, name):
    tt = 512

    def body(d_ref, o_ref, z_ref, g_ref, w_ref, do_ref, dz_ref, dg_ref, dcat):
        i = pl.program_id(0)
        g = g_ref[...]
        dcat[...] = _dot_nt(d_ref[...].astype(BF16), w_ref[...])
        dg = jnp.zeros((1, 128), F32)
        for hh in range(G_HEADS):
            ln = slice(hh * 128, (hh + 1) * 128)
            zz = z_ref[:, ln]
            oh, rstd, sig = _gated_norm(o_ref[:, ln], zz, g)
            dout = dcat[:, ln]
            dy = dout * (zz * sig)
            dz_ref[:, ln] = dout * (oh * g) * _dsilu(zz, sig)
            dg = dg + jnp.sum(dy * oh, axis=0, keepdims=True)
            doh = dy * g
            do_ref[:, ln] = rstd * (doh - oh * jnp.mean(doh * oh, axis=-1, keepdims=True))

        @pl.when(i == 0)
        def _():
            dg_ref[...] = dg

        @pl.when(i > 0)
        def _():
            dg_ref[...] += dg

    full = pl.BlockSpec((tt, D), lambda i: (i, 0))
    vec = pl.BlockSpec((1, 128), lambda i: (0, 0))
    return pl.pallas_call(
        body,
        grid=(T // tt,),
        in_specs=[full, full, full, vec, pl.BlockSpec((D, D), lambda i: (0, 0))],
        out_specs=[full, full, vec],
        out_shape=[jax.ShapeDtypeStruct((T, D), F32), jax.ShapeDtypeStruct((T, D), F32), jax.ShapeDtypeStruct((1, 128), F32)],
        scratch_shapes=[pltpu.VMEM((tt, D), F32)],
        compiler_params=_cp("arbitrary"),
        name=name,
    )(dres, o, z, norm_g, w_out)


GDN_SPLITS = [(0, 1024), (1024, 1024), (2048, 1024), (3072, 1024), (4096, 2 * G_HEADS)]


def gdn_fwd(h, g_row, w_in, conv_w, alog, dtb, norm_g, w_out, tag):
    hn, qkv_pre, z, ba = proj_fwd(h, g_row, w_in, [(0, G_QKV), (G_QKV, 1024), (4096, 2 * G_HEADS)], f"gdn_proj_{tag}")
    qkv = gdn_prep_fwd(qkv_pre, conv_w, f"gdn_prep_{tag}")
    u, w, qd, kd, at, el, tinv = gdn_local_fwd(qkv, ba, alog, dtb, f"gdn_local_{tag}")
    o, vn, st = gdn_rec_fwd(u, w, qd, kd, at, el, f"gdn_rec_{tag}")
    hnew, cat = gdn_out_fwd(h, o, z, norm_g, w_out, f"gdn_out_{tag}")
    return hnew, (h, hn, qkv_pre, z, ba, qkv, w, qd, kd, at, el, tinv, o, vn, st, cat)


def gdn_bwd(dres, saved, g_row, w_in, conv_w, alog, dtb, norm_g, w_out, tag):
    h, hn, qkv_pre, z, ba, qkv, w, qd, kd, at, el, tinv, o, vn, st, cat = saved
    d_o, d_z, d_ng = gdn_out_bwd(dres, o, z, norm_g, w_out, f"gdn_out_bwd_{tag}")
    d_wout = mm_tn_full(cat, dres, BF16, f"gdn_wout_grad_{tag}")
    du, dw, dqd, dkd, dat, dl = gdn_rec_bwd(d_o, w, qd, kd, at, el, vn, st, f"gdn_rec_bwd_{tag}")
    dq, dk, dv, dba, dal, ddt = gdn_local_bwd(qkv, ba, alog, dtb, tinv, du, dw, dqd, dkd, dat, dl, f"gdn_local_bwd_{tag}")
    dpre, dcw = [], []
    for part, d in enumerate((dq, dk, dv)):
        dx, dwc = gdn_prep_bwd(d, qkv_pre, conv_w, part, part < 2, f"gdn_prep_bwd_{tag}_{part}")
        dpre.append(dx)
        dcw.append(dwc)
    parts = dpre + [d_z, dba]
    dres_new, d_norm, dproj = proj_bwd_data(h, g_row, w_in, parts, GDN_SPLITS, dres, f"gdn_proj_bwd_{tag}")
    d_win = cols_to_slabs(mm_tn_full(hn, dproj, F32, f"gdn_win_grad_{tag}"), mm_tn(hn, dba, f"gdn_win_grad_ba_{tag}"),
                          f"gdn_win_slabs_{tag}")
    return dres_new, dict(norm=d_norm, w_in=d_win, conv_w=jnp.concatenate(dcw, axis=1), A_log=dal, dt_bias=ddt,
                          norm_g=d_ng, w_out=d_wout)


MESH = pl.DeviceIdType.MESH
ANY = pl.BlockSpec(memory_space=pl.ANY)


def _coords():
    return lax.axis_index("x"), lax.axis_index("y"), lax.axis_index("c")


def _slot(p):
    return 4 * p[0] + 2 * p[1] + p[2]


def all_gather(shards, name):
    k_n = len(shards)

    def body(*refs):
        srcs, dsts = refs[:k_n], refs[k_n:2 * k_n]
        send_sems, recv_sems, local_sems = refs[2 * k_n:]
        x, y, c = _coords()
        me, sibling = (x, y, c), (x, y, 1 - c)
        chips = [(1 - x, y), (x, 1 - y), (1 - x, 1 - y)]

        def copy(k, s, block, to, from_src=False):
            rows = dsts[k].at[_slot(block)]
            return pltpu.make_async_remote_copy(
                src_ref=srcs[k] if from_src else rows, dst_ref=rows,
                send_sem=send_sems.at[k, s], recv_sem=recv_sems.at[k, s], device_id=to, device_id_type=MESH)

        local = [pltpu.make_async_copy(srcs[k], dsts[k].at[_slot(me)], local_sems.at[k]) for k in range(k_n)]
        for cp in local:
            cp.start()
        first = []
        for k in range(k_n):
            first.append(copy(k, 0, me, sibling, True))
            first += [copy(k, 1 + j, me, (*chip, c), True) for j, chip in enumerate(chips)]
        for cp in first:
            cp.start()
        passed = []
        for j, chip in enumerate(chips):
            for k in range(k_n):
                copy(k, 1 + j, (*chip, c), me).wait_recv()
                fw = copy(k, 4 + j, (*chip, c), sibling)
                fw.start()
                passed.append(fw)
        for k in range(k_n):
            copy(k, 0, sibling, me).wait_recv()
            for j, chip in enumerate(chips):
                copy(k, 4 + j, (*chip, 1 - c), me).wait_recv()
        for cp in first + passed:
            cp.wait_send()
        for cp in local:
            cp.wait()

    return pl.pallas_call(
        body,
        in_specs=[ANY] * k_n,
        out_specs=[ANY] * k_n,
        out_shape=[jax.ShapeDtypeStruct((N_DEV,) + s.shape, s.dtype) for s in shards],
        scratch_shapes=[pltpu.SemaphoreType.DMA((k_n, 7)), pltpu.SemaphoreType.DMA((k_n, 7)),
                        pltpu.SemaphoreType.DMA((k_n,))],
        name=name,
    )(*shards)


HBM = pl.BlockSpec(memory_space=pltpu.HBM)
SEM = pl.BlockSpec(memory_space=pltpu.SEMAPHORE)
EFFECT = pltpu.SideEffectType.DATAFLOW_SIDE_EFFECTING


def _hbm(a):
    return pltpu.with_memory_space_constraint(a, pltpu.HBM)


def _peer_list(x, y, c):
    peers = []
    for j in range(1, N_DEV):
        jx, jy, jc = (j >> 2) & 1, (j >> 1) & 1, j & 1
        peers.append((x if jx == 0 else 1 - x, y if jy == 0 else 1 - y, c if jc == 0 else 1 - c))
    return peers


def _push_views(kind, layer, src_ref, land_ref, me, peer_slot):
    if kind == "gather":
        return src_ref, land_ref.at[me], land_ref.at[peer_slot]
    if layer is None:
        return src_ref.at[peer_slot], land_ref.at[me], land_ref.at[peer_slot]
    return src_ref.at[peer_slot], land_ref.at[me, layer], land_ref.at[peer_slot, layer]


def _push_copies(groups, srcs, lands, sems):
    x, y, c = _coords()
    me = _slot((x, y, c))
    peers = _peer_list(x, y, c)
    t = 0
    for gi, group in enumerate(groups):
        for ti, (kind, layer, _, li) in enumerate(group):
            for j, peer in enumerate(peers):
                out, there, here = _push_views(kind, layer, srcs[t], lands[li], me, _slot(peer))
                k = ti * (N_DEV - 1) + j
                yield out, there, here, sems[2 * gi].at[k], sems[2 * gi + 1].at[k], peer
            t += 1


def push_start(groups, lands, name, carry=()):
    flat = [it for g in groups for it in g]
    n, n_l, n_g, n_c = len(flat), len(lands), len(groups), len(carry)
    n_in = n + n_l + n_c

    def body(*refs):
        srcs, land_refs, sems = refs[:n], refs[n:n + n_l], refs[n_in:n_in + 2 * n_g]
        for out, there, _, s_sem, r_sem, peer in _push_copies(groups, srcs, land_refs, sems):
            pltpu.make_async_remote_copy(src_ref=out, dst_ref=there, send_sem=s_sem, recv_sem=r_sem,
                                         device_id=peer, device_id_type=MESH).start()

    arrays = [it[2] for it in flat] + list(lands) + list(carry)
    sem_shapes = []
    for g in groups:
        sem_shapes += [pltpu.SemaphoreType.DMA((len(g) * (N_DEV - 1),))] * 2
    outs = pl.pallas_call(
        body,
        name=name,
        in_specs=[HBM] * n_in,
        out_specs=[SEM] * (2 * n_g) + [HBM] * n_in,
        out_shape=sem_shapes + [pltpu.HBM(a.shape, a.dtype) for a in arrays],
        input_output_aliases={i: 2 * n_g + i for i in range(n_in)},
        compiler_params=pltpu.CompilerParams(has_side_effects=EFFECT),
    )(*[_hbm(a) for a in arrays])
    sems, thru = list(outs[:2 * n_g]), list(outs[2 * n_g:])
    return sems, thru[:n], thru[n:n + n_l], thru[n + n_l:]


def push_wait(groups, lands, sems, after, name):
    flat = [it for g in groups for it in g]
    n, n_l, n_g = len(flat), len(lands), len(groups)

    def body(*refs):
        srcs, land_refs, sem_refs = refs[:n], refs[n:n + n_l], refs[n + n_l:n + n_l + 2 * n_g]
        for out, _, here, s_sem, r_sem, peer in _push_copies(groups, srcs, land_refs, sem_refs):
            cp = pltpu.make_async_remote_copy(src_ref=out, dst_ref=here, send_sem=s_sem, recv_sem=r_sem,
                                              device_id=peer, device_id_type=MESH)
            cp.wait_send()
            cp.wait_recv()

    arrays = [it[2] for it in flat] + list(lands)
    outs = pl.pallas_call(
        body,
        name=name,
        in_specs=[HBM] * (n + n_l) + [SEM] * (2 * n_g) + [ANY],
        out_specs=[HBM] * (n + n_l),
        out_shape=[pltpu.HBM(a.shape, a.dtype) for a in arrays],
        input_output_aliases={i: i for i in range(n + n_l)},
        compiler_params=pltpu.CompilerParams(has_side_effects=EFFECT),
    )(*arrays, *sems, after)
    return list(outs[:n]), list(outs[n:])


def sum_slabs(parts, name):
    n, rows, cols = parts.shape

    def body(p_ref, o_ref):
        g = p_ref[0]
        for s in range(1, n):
            g = g + p_ref[s]
        o_ref[...] = g

    return pl.pallas_call(body, out_shape=jax.ShapeDtypeStruct((rows, cols), F32), name=name)(parts)


def _row_tile(rows, cols):
    if rows * cols * 4 <= (1 << 20) or rows % 8:
        return rows
    tr = rows
    while tr % 2 == 0 and (tr // 2) % 8 == 0 and tr * cols * 4 > (1 << 20):
        tr //= 2
    return tr


def adamw(parts, w, m, v, name):
    p_n = parts.shape[0]
    rows, cols = w.shape
    tr = _row_tile(rows, cols)

    def body(p_ref, w_ref, m_ref, v_ref, g_ref, d_ref, nm_ref, nv_ref):
        g = p_ref[0].astype(F32)
        for s in range(1, p_n):
            g = g + p_ref[s].astype(F32)
        g_ref[...] = g
        d_ref[...], nm_ref[...], nv_ref[...] = _adam_update(g, w_ref[...], m_ref[...], v_ref[...])

    blk = pl.BlockSpec((tr, cols), lambda i: (i, 0))
    return pl.pallas_call(
        body,
        grid=(rows // tr,),
        in_specs=[pl.BlockSpec((p_n, tr, cols), lambda i: (0, i, 0)), blk, blk, blk],
        out_specs=[blk] * 4,
        out_shape=[jax.ShapeDtypeStruct((rows, cols), F32)] * 4,
        compiler_params=_cp("parallel"),
        name=name,
    )(parts, w, m, v)


def _adam_update(g, w, m, v):
    m_new = ADAM_B1 * m + (1.0 - ADAM_B1) * g
    v_new = ADAM_B2 * v + (1.0 - ADAM_B2) * (g * g)
    m_hat = m_new / (1.0 - ADAM_B1 ** ADAM_STEP)
    v_hat = v_new / (1.0 - ADAM_B2 ** ADAM_STEP)
    return -ADAM_LR * (m_hat / (jnp.sqrt(v_hat) + ADAM_EPS) + ADAM_WD * w), m_new, v_new


def _adamw_nd(parts, w, m, v, name):
    p_n = parts.shape[0]
    n_l, rows, cols = w.shape
    tr = _row_tile(rows, cols)

    def body(p_ref, w_ref, m_ref, v_ref, g_ref, d_ref, nm_ref, nv_ref):
        g = p_ref[0].astype(F32)
        for s in range(1, p_n):
            g = g + p_ref[s].astype(F32)
        g_ref[...] = g
        d_ref[...], nm_ref[...], nv_ref[...] = _adam_update(g, w_ref[...], m_ref[...], v_ref[...])

    blk = pl.BlockSpec((None, tr, cols), lambda l, i: (l, i, 0))
    return pl.pallas_call(
        body,
        grid=(n_l, rows // tr),
        in_specs=[pl.BlockSpec((p_n, None, tr, cols), lambda l, i: (0, l, i, 0)), blk, blk, blk],
        out_specs=[blk] * 4,
        out_shape=[jax.ShapeDtypeStruct(w.shape, F32)] * 4,
        compiler_params=_cp("parallel", "parallel"),
        name=name,
    )(parts, w, m, v)


def slabs_to_cols(slabs, name):
    n, r, w = slabs.shape
    tr = 256 if r % 256 == 0 else r

    def body(s_ref, o_ref):
        for s in range(n):
            o_ref[:, w * s:w * (s + 1)] = s_ref[s]

    return pl.pallas_call(
        body,
        grid=(r // tr,),
        in_specs=[pl.BlockSpec((n, tr, w), lambda i: (0, i, 0))],
        out_specs=pl.BlockSpec((tr, n * w), lambda i: (i, 0)),
        out_shape=jax.ShapeDtypeStruct((r, n * w), slabs.dtype),
        compiler_params=_cp("parallel"),
        name=name,
    )(slabs)


FFN_IN = ("ffn1_w_in", "ffn2_w_in")
REPL = ["ffn1_norm", "mix_norm", "ffn2_norm", "hyb_dw_b", "hyb_ln_g", "hyb_ln_b", "gdn_A_log", "gdn_dt_bias",
        "gdn_norm_g", "final_norm"]
WEIGHTS = ["ffn1_norm", "ffn1_w_in", "ffn1_w_out", "mix_norm", "ffn2_norm", "ffn2_w_in", "ffn2_w_out", "hyb_w_in",
           "hyb_dw_w", "hyb_dw_b", "hyb_ln_g", "hyb_ln_b", "hyb_w_out", "gdn_w_in", "gdn_conv_w", "gdn_A_log",
           "gdn_dt_bias", "gdn_norm_g", "gdn_w_out", "final_norm"]


def _pack(arrs, rows):
    flat = jnp.concatenate([a.reshape(-1) for a in arrs])
    return jnp.pad(flat, (0, rows * 128 - flat.shape[0])).reshape(rows, 128)


def kernel(x, positions, ffn1_norm, ffn1_w_in, ffn1_w_out, mix_norm, ffn2_norm, ffn2_w_in, ffn2_w_out, hyb_w_in, hyb_dw_w, hyb_dw_b, hyb_ln_g, hyb_ln_b, hyb_w_out, gdn_w_in, gdn_conv_w, gdn_A_log, gdn_dt_bias, gdn_norm_g, gdn_w_out, final_norm, loss_target, m_ffn1_norm, m_ffn1_w_in, m_ffn1_w_out, m_mix_norm, m_ffn2_norm, m_ffn2_w_in, m_ffn2_w_out, m_hyb_w_in, m_hyb_dw_w, m_hyb_dw_b, m_hyb_ln_g, m_hyb_ln_b, m_hyb_w_out, m_gdn_w_in, m_gdn_conv_w, m_gdn_A_log, m_gdn_dt_bias, m_gdn_norm_g, m_gdn_w_out, m_final_norm, v_ffn1_norm, v_ffn1_w_in, v_ffn1_w_out, v_mix_norm, v_ffn2_norm, v_ffn2_w_in, v_ffn2_w_out, v_hyb_w_in, v_hyb_dw_w, v_hyb_dw_b, v_hyb_ln_g, v_hyb_ln_b, v_hyb_w_out, v_gdn_w_in, v_gdn_conv_w, v_gdn_A_log, v_gdn_dt_bias, v_gdn_norm_g, v_gdn_w_out, v_final_norm):
    w = dict(ffn1_norm=ffn1_norm, ffn1_w_in=ffn1_w_in, ffn1_w_out=ffn1_w_out, mix_norm=mix_norm, ffn2_norm=ffn2_norm,
             ffn2_w_in=ffn2_w_in, ffn2_w_out=ffn2_w_out, hyb_w_in=hyb_w_in, hyb_dw_w=hyb_dw_w, hyb_dw_b=hyb_dw_b,
             hyb_ln_g=hyb_ln_g, hyb_ln_b=hyb_ln_b, hyb_w_out=hyb_w_out, gdn_w_in=gdn_w_in, gdn_conv_w=gdn_conv_w,
             gdn_A_log=gdn_A_log, gdn_dt_bias=gdn_dt_bias, gdn_norm_g=gdn_norm_g, gdn_w_out=gdn_w_out,
             final_norm=final_norm)
    mom = dict(ffn1_norm=m_ffn1_norm, ffn1_w_in=m_ffn1_w_in, ffn1_w_out=m_ffn1_w_out, mix_norm=m_mix_norm,
               ffn2_norm=m_ffn2_norm, ffn2_w_in=m_ffn2_w_in, ffn2_w_out=m_ffn2_w_out, hyb_w_in=m_hyb_w_in,
               hyb_dw_w=m_hyb_dw_w, hyb_dw_b=m_hyb_dw_b, hyb_ln_g=m_hyb_ln_g, hyb_ln_b=m_hyb_ln_b,
               hyb_w_out=m_hyb_w_out, gdn_w_in=m_gdn_w_in, gdn_conv_w=m_gdn_conv_w, gdn_A_log=m_gdn_A_log,
               gdn_dt_bias=m_gdn_dt_bias, gdn_norm_g=m_gdn_norm_g, gdn_w_out=m_gdn_w_out, final_norm=m_final_norm)
    var = dict(ffn1_norm=v_ffn1_norm, ffn1_w_in=v_ffn1_w_in, ffn1_w_out=v_ffn1_w_out, mix_norm=v_mix_norm,
               ffn2_norm=v_ffn2_norm, ffn2_w_in=v_ffn2_w_in, ffn2_w_out=v_ffn2_w_out, hyb_w_in=v_hyb_w_in,
               hyb_dw_w=v_hyb_dw_w, hyb_dw_b=v_hyb_dw_b, hyb_ln_g=v_hyb_ln_g, hyb_ln_b=v_hyb_ln_b,
               hyb_w_out=v_hyb_w_out, gdn_w_in=v_gdn_w_in, gdn_conv_w=v_gdn_conv_w, gdn_A_log=v_gdn_A_log,
               gdn_dt_bias=v_gdn_dt_bias, gdn_norm_g=v_gdn_norm_g, gdn_w_out=v_gdn_w_out, final_norm=v_final_norm)
    xi, yi, ci = _coords()
    me = 4 * xi + 2 * yi + ci
    for group in (w, mom, var):
        for n in FFN_IN:
            group[n] = jnp.swapaxes(group[n], 1, 2)

    big = ["ffn1_w_in", "ffn1_w_out", "ffn2_w_in", "ffn2_w_out", "hyb_w_in", "hyb_w_out", "gdn_w_in", "gdn_w_out"]
    ag_groups, ag_lands = [], []

    def add_group(shards):
        group = []
        for s in shards:
            land = lax.dynamic_update_slice(lax.empty((N_DEV,) + s.shape, s.dtype), s[None], (me,) + (0,) * s.ndim)
            group.append(("gather", None, s, len(ag_lands)))
            ag_lands.append(land)
        ag_groups.append(group)

    first = all_gather([w["ffn1_w_in"][0].astype(BF16), ffn1_w_out[0].astype(BF16)], "weights_gather_first")
    for l in range(DEPTH):
        i = l // 2
        if l == 0:
            ag_groups.append([])
        else:
            add_group([w["ffn1_w_in"][l].astype(BF16), ffn1_w_out[l].astype(BF16)])
        if l % 2 == 0:
            add_group([hyb_w_in[i].astype(BF16), hyb_w_out[i].astype(BF16), hyb_dw_w[i]])
        else:
            add_group([gdn_w_in[i].astype(BF16), gdn_w_out[i].astype(BF16), gdn_conv_w[i]])
        add_group([w["ffn2_w_in"][l].astype(BF16), ffn2_w_out[l].astype(BF16)])
    ag_sems, ag_srcs, ag_lands, first = push_start(ag_groups[1:], ag_lands, "weights_gather_start", carry=first)
    ag_sems = [None, None] + ag_sems

    def fetch(gi, after):
        if gi == 0:
            return first
        group = ag_groups[gi]
        base = sum(len(g) for g in ag_groups[:gi])
        items = [(kind, layer, ag_srcs[base + t], t) for t, (kind, layer, _, _) in enumerate(group)]
        lands = [ag_lands[li] for _, _, _, li in group]
        return push_wait([items], lands, ag_sems[2 * gi:2 * gi + 2], after, f"weights_gather_wait_{gi}")[1]

    row = lambda a: a.reshape(1, -1)

    rope = make_rope(positions)
    h = x[0]
    saved = []
    for l in range(DEPTH):
        i = l // 2
        rec = {"h1": h}
        wi, wo = fetch(3 * l, h)
        rec["w1"] = (wi.reshape(2, FFN_TILES, FFN_SHARD, D), wo)
        h, rec["hn1"], rec["a1"], rec["b1"] = ffn_fwd(h, row(ffn1_norm[l]), *rec["w1"], l, "1")
        mi, mo, mc = fetch(3 * l + 1, h)
        if l % 2 == 0:
            rec["wm"] = (slabs_to_cols(mi, f"hyb_w_in_cols_{i}"),
                         jnp.pad(slabs_to_cols(mc, f"hyb_dw_w_cols_{i}"), ((0, 1), (0, 0))), mo.reshape(D, D))
            w_in_f, dw_f, w_out_f = rec["wm"]
            h, rec["mix"] = hybrid_fwd(h, row(mix_norm[l]), w_in_f, dw_f, row(hyb_dw_b[i]), row(hyb_ln_g[i]),
                                       row(hyb_ln_b[i]), w_out_f, rope, str(i))
        else:
            rec["wm"] = (slabs_to_cols(mi, f"gdn_w_in_cols_{i}"), slabs_to_cols(mc, f"gdn_conv_w_cols_{i}"),
                         mo.reshape(D, D))
            w_in_f, cw_f, w_out_f = rec["wm"]
            h, rec["mix"] = gdn_fwd(h, row(mix_norm[l]), w_in_f, cw_f, row(gdn_A_log[i]), row(gdn_dt_bias[i]),
                                    row(gdn_norm_g[i]), w_out_f, str(i))
        rec["h2"] = h
        wi, wo = fetch(3 * l + 2, h)
        rec["w2"] = (wi.reshape(2, FFN_TILES, FFN_SHARD, D), wo)
        h, rec["hn2"], rec["a2"], rec["b2"] = ffn_fwd(h, row(ffn2_norm[l]), *rec["w2"], l, "2")
        saved.append(rec)
    dres, d_final, loss_acc = final_loss(h, row(final_norm), loss_target[0])

    ge_land = {n: lax.empty((N_DEV,) + w[n].shape, BF16) for n in big}
    ge_pending = []

    def send(named, layer, tag, carry):
        lands = [ge_land[n] for n, _ in named]
        group = [("scatter", layer, s, t) for t, (_, s) in enumerate(named)]
        sems, srcs, lands_out, carried = push_start([group], lands, f"grad_send_{tag}", carry=[carry])
        for (n, _), land in zip(named, lands_out):
            ge_land[n] = land
        ge_pending.append(([(n, layer, s) for (n, _), s in zip(named, srcs)], sems))
        return carried[0]

    gsmall = {n: [None] * (DEPTH if n in ("ffn1_norm", "mix_norm", "ffn2_norm") else 2) for n in REPL[:-1]}
    gsmall["hyb_dw_w"] = [None, None]
    gsmall["gdn_conv_w"] = [None, None]
    for l in reversed(range(DEPTH)):
        i = l // 2
        rec = saved[l]
        dhn, dwin, dwout = ffn_bwd(rec["hn2"], rec["a2"], rec["b2"], dres, *rec["w2"], l, "2")
        dhn = send([("ffn2_w_in", dwin.reshape(N_DEV, FFN_SHARD, D)),
                    ("ffn2_w_out", dwout.reshape(N_DEV, FFN_SHARD // 2, D))], l, f"ffn2_{l}", dhn)
        dres, dg = norm_bwd(rec["h2"], row(ffn2_norm[l]), dhn, dres, f"ffn2_norm_bwd_{l}")
        gsmall["ffn2_norm"][l] = dg
        if l % 2 == 0:
            w_in_f, dw_f, w_out_f = rec["wm"]
            dres, gr = hybrid_bwd(dres, rec["mix"], row(mix_norm[l]), w_in_f, dw_f, row(hyb_ln_g[i]),
                                  row(hyb_ln_b[i]), w_out_f, rope, str(i))
            dres = send([("hyb_w_in", gr["w_in"]), ("hyb_w_out", gr["w_out"].reshape(N_DEV, D // N_DEV, D))],
                        i, f"hyb_{i}", dres)
            for n in ("dw_w", "dw_b", "ln_g", "ln_b"):
                gsmall["hyb_" + n][i] = gr[n]
        else:
            w_in_f, cw_f, w_out_f = rec["wm"]
            dres, gr = gdn_bwd(dres, rec["mix"], row(mix_norm[l]), w_in_f, cw_f, row(gdn_A_log[i]),
                               row(gdn_dt_bias[i]), row(gdn_norm_g[i]), w_out_f, str(i))
            dres = send([("gdn_w_in", gr["w_in"]), ("gdn_w_out", gr["w_out"].reshape(N_DEV, D // N_DEV, D))],
                        i, f"gdn_{i}", dres)
            for n in ("conv_w", "A_log", "dt_bias", "norm_g"):
                gsmall["gdn_" + n][i] = gr[n]
        gsmall["mix_norm"][l] = gr["norm"]
        dhn, dwin, dwout = ffn_bwd(rec["hn1"], rec["a1"], rec["b1"], dres, *rec["w1"], l, "1")
        dhn = send([("ffn1_w_in", dwin.reshape(N_DEV, FFN_SHARD, D)),
                    ("ffn1_w_out", dwout.reshape(N_DEV, FFN_SHARD // 2, D))], l, f"ffn1_{l}", dhn)
        dres, dg = norm_bwd(rec["h1"], row(ffn1_norm[l]), dhn, dres, f"ffn1_norm_bwd_{l}")
        gsmall["ffn1_norm"][l] = dg
    grad_x = dres[None]

    n_repl_rows = 136
    small_rows = 576
    repl_flat = jnp.concatenate([jnp.concatenate([a.reshape(-1) for a in gsmall[n]]) for n in REPL[:-1]]
                                + [d_final.reshape(-1), loss_acc[0, 0:1]])
    loss_at = repl_flat.shape[0] - 1
    repl_pack = jnp.pad(repl_flat, (0, n_repl_rows * 128 - repl_flat.shape[0]))
    small_pack = jnp.concatenate([repl_pack] + [a.reshape(-1) for a in gsmall["hyb_dw_w"]]
                                 + [a.reshape(-1) for a in gsmall["gdn_conv_w"]]).reshape(small_rows, 128)

    own = {n: {} for n in big}

    def wait_for(pending, names, after, name):
        groups = [[("scatter", layer, s, names.index(n)) for n, layer, s in named] for named, _ in pending]
        sems = [s for _, pair in pending for s in pair]
        srcs_out, lands_out = push_wait(groups, [ge_land[n] for n in names], sems, after, name)
        flat_named = [it for named, _ in pending for it in named]
        for (n, layer, _), s in zip(flat_named, srcs_out):
            own[n][layer] = lax.dynamic_index_in_dim(s, me, 0, keepdims=False)
        for n, land in zip(names, lands_out):
            ge_land[n] = land

    def with_own(n, land):
        mine = jnp.stack([own[n][k] for k in range(len(own[n]))])
        return lax.dynamic_update_slice(land, mine[None], (me,) + (0,) * mine.ndim)

    out = {}
    last = ["ffn1_w_in", "ffn1_w_out"]
    wait_for(ge_pending[:-1], big, dres, "grad_wait_a")
    for n in big:
        if n not in last:
            out[n] = _adamw_nd(with_own(n, ge_land[n]), w[n], mom[n], var[n], f"adamw_{n}")
    pin = sum(out[n][1].reshape(-1)[0] for n in big if n not in last) * 0.0
    small_all, = all_gather([small_pack + pin], "small_grads_all_gather")
    g_small = sum_slabs(small_all, "small_grads_sum")
    loss = g_small.reshape(-1)[loss_at]
    wait_for(ge_pending[-1:], last, g_small, "grad_wait_b")
    for n in last:
        out[n] = _adamw_nd(with_own(n, ge_land[n]), w[n], mom[n], var[n], f"adamw_{n}")

    pk = lambda d: _pack([d[n] for n in REPL], n_repl_rows)
    res = adamw(g_small[:n_repl_rows][None], pk(w), pk(mom), pk(var), "adamw_replicated")
    off = 0
    for n in REPL:
        sz = w[n].size
        out[n] = [r.reshape(-1)[off:off + sz].reshape(w[n].shape) for r in res]
        off += sz
    g_dw = g_small[n_repl_rows:n_repl_rows + 248].reshape(2, CONV_K, CONV_C)
    g_dw = lax.dynamic_slice_in_dim(g_dw, me * (CONV_C // N_DEV), CONV_C // N_DEV, axis=2)
    out["hyb_dw_w"] = _adamw_nd(g_dw[None], w["hyb_dw_w"], mom["hyb_dw_w"], var["hyb_dw_w"], "adamw_hyb_dw_w")
    g_cw = g_small[n_repl_rows + 248:].reshape(2, G_CONV, G_QKV)
    g_cw = lax.dynamic_slice_in_dim(g_cw, me * (G_QKV // N_DEV), G_QKV // N_DEV, axis=2)
    out["gdn_conv_w"] = _adamw_nd(g_cw[None], w["gdn_conv_w"], mom["gdn_conv_w"], var["gdn_conv_w"], "adamw_gdn_conv_w")

    for n in FFN_IN:
        out[n] = [jnp.swapaxes(o, 1, 2) for o in out[n]]
    return (loss, grad_x, *[out[n][0] for n in WEIGHTS], *[out[n][1] for n in WEIGHTS],
            *[out[n][2] for n in WEIGHTS], *[out[n][3] for n in WEIGHTS])
```

```python
import functools

import jax
import jax.numpy as jnp
from jax import lax
from jax.experimental import pallas as pl
from jax.experimental.pallas import tpu as pltpu

F32 = jnp.float32
BF16 = jnp.bfloat16

N_DEV = 8
T = 4096
D = 1024
DEPTH = 4
FFN = 2816
FFN_SHARD = 2 * FFN // N_DEV
FFN_TILES = FFN // FFN_SHARD
EPS = 1e-6

A_HEADS = 8
A_HD = 64
A_W = 512
CONV_C = 512
CONV_K = 31
HYB_IN = 2560
ROPE_THETA = 500000.0
ROT = 16
DILATIONS = (1, 4, 16)
BLK = 128
KPAD = 2048

G_HEADS = 8
G_DK = 128
G_QKV = 3072
G_IN = 4112
G_CONV = 4
CH = 64
GRP = 512
CPG = GRP // CH
N_GRP = T // GRP

ADAM_LR = 0.001
ADAM_B1 = 0.9
ADAM_B2 = 0.999
ADAM_EPS = 1e-08
ADAM_WD = 0.01
ADAM_STEP = 10

VMEM_LIMIT = 56 * 1024 * 1024


def _cp(*sem):
    return pltpu.CompilerParams(dimension_semantics=sem, vmem_limit_bytes=VMEM_LIMIT)


def _dot(a, b):
    return jnp.dot(a, b, preferred_element_type=F32)


def _dot_nt(a, b):
    return lax.dot_general(a, b, (((1,), (1,)), ((), ())), preferred_element_type=F32)


def _dot_tn(a, b):
    return lax.dot_general(a, b, (((0,), (0,)), ((), ())), preferred_element_type=F32)


def _sigmoid(x):
    return 1.0 / (1.0 + jnp.exp(-x))


def _dsilu(x, sig):
    return sig * (1.0 + x * (1.0 - sig))


def _rms(x, g):
    rstd = lax.rsqrt(jnp.mean(x * x, axis=-1, keepdims=True) + EPS)
    return x * rstd * g


FFN_TT = 512


def ffn_fwd(h, g_row, w_in, w_out, layer, tag=""):
    def body(h_ref, g_ref, win_ref, wout_ref, hnew_ref, hn_ref, a_ref, b_ref):
        x = h_ref[...]
        hn = _rms(x, g_ref[...]).astype(BF16)
        hn_ref[...] = hn
        acc = None
        for j in range(FFN_TILES):
            a = _dot_nt(hn, win_ref[0, j])
            b = _dot_nt(hn, win_ref[1, j])
            act = a * _sigmoid(a) * b
            a_ref[j] = a.astype(BF16)
            b_ref[j] = b.astype(BF16)
            part = _dot(act.astype(BF16), wout_ref[2 * j:2 * j + 2].reshape(FFN_SHARD, D))
            acc = part if acc is None else acc + part
        hnew_ref[...] = x + 0.5 * acc

    tt = FFN_TT
    resident = pl.Buffered(1)
    return pl.pallas_call(
        body,
        grid=(T // tt,),
        in_specs=[
            pl.BlockSpec((tt, D), lambda i: (i, 0)),
            pl.BlockSpec((1, D), lambda i: (0, 0)),
            pl.BlockSpec((2, FFN_TILES, FFN_SHARD, D), lambda i: (0, 0, 0, 0), pipeline_mode=resident),
            pl.BlockSpec((N_DEV, FFN_SHARD // 2, D), lambda i: (0, 0, 0), pipeline_mode=resident),
        ],
        out_specs=[
            pl.BlockSpec((tt, D), lambda i: (i, 0)),
            pl.BlockSpec((tt, D), lambda i: (i, 0)),
            pl.BlockSpec((FFN_TILES, tt, FFN_SHARD), lambda i: (0, i, 0)),
            pl.BlockSpec((FFN_TILES, tt, FFN_SHARD), lambda i: (0, i, 0)),
        ],
        out_shape=[
            jax.ShapeDtypeStruct((T, D), F32),
            jax.ShapeDtypeStruct((T, D), BF16),
            jax.ShapeDtypeStruct((FFN_TILES, T, FFN_SHARD), BF16),
            jax.ShapeDtypeStruct((FFN_TILES, T, FFN_SHARD), BF16),
        ],
        compiler_params=_cp("parallel"),
        name=f"ffn{tag}_fwd_{layer}",
    )(h, g_row, w_in, w_out)


def ffn_bwd(hn, a, b, dres, w_in, w_out, layer, tag=""):
    tt = FFN_TT
    nt = T // tt

    def body(hn_ref, a_ref, b_ref, dres_ref, win_ref, wout_ref, dhn_ref, dwin_ref, dwout_ref, gin_ref, gout_ref,
             do_s, act_s, da_s, db_s):
        i = pl.program_id(1)
        wo = wout_ref[...].reshape(FFN_SHARD, D)
        half = tt // 2
        for r0 in (0, half):
            rows = slice(r0, r0 + half)
            do_h = (0.5 * dres_ref[rows, :]).astype(BF16)
            do_s[rows, :] = do_h
            dact = _dot_nt(do_h, wo)
            a = a_ref[rows, :].astype(F32)
            b = b_ref[rows, :].astype(F32)
            sig = _sigmoid(a)
            s = a * sig
            da_h = (dact * b * _dsilu(a, sig)).astype(BF16)
            db_h = (dact * s).astype(BF16)
            act_s[rows, :] = (s * b).astype(BF16)
            da_s[rows, :] = da_h
            db_s[rows, :] = db_h
            dhn_ref[rows, :] = (_dot(da_h, win_ref[0]) + _dot(db_h, win_ref[1])).astype(BF16)
        do, act, da, db = do_s[...], act_s[...], da_s[...], db_s[...]
        hn = hn_ref[...]
        gwo = _dot_tn(act, do)
        gwg = _dot_tn(da, hn)
        gwu = _dot_tn(db, hn)

        @pl.when(i == 0)
        def _():
            gout_ref[...] = gwo
            gin_ref[0] = gwg
            gin_ref[1] = gwu

        @pl.when(i > 0)
        def _():
            gout_ref[...] += gwo
            gin_ref[0] += gwg
            gin_ref[1] += gwu

        @pl.when(i == nt - 1)
        def _():
            dwin_ref[...] = gin_ref[...].astype(BF16)
            dwout_ref[...] = gout_ref[...].astype(BF16)

    return pl.pallas_call(
        body,
        grid=(FFN_TILES, nt),
        in_specs=[
            pl.BlockSpec((tt, D), lambda j, i: (i, 0)),
            pl.BlockSpec((None, tt, FFN_SHARD), lambda j, i: (j, i, 0)),
            pl.BlockSpec((None, tt, FFN_SHARD), lambda j, i: (j, i, 0)),
            pl.BlockSpec((tt, D), lambda j, i: (i, 0)),
            pl.BlockSpec((2, None, FFN_SHARD, D), lambda j, i: (0, j, 0, 0)),
            pl.BlockSpec((2, FFN_SHARD // 2, D), lambda j, i: (j, 0, 0)),
        ],
        out_specs=[
            pl.BlockSpec((None, tt, D), lambda j, i: (j, i, 0)),
            pl.BlockSpec((2, None, FFN_SHARD, D), lambda j, i: (0, j, 0, 0)),
            pl.BlockSpec((None, FFN_SHARD, D), lambda j, i: (j, 0, 0)),
        ],
        out_shape=[
            jax.ShapeDtypeStruct((FFN_TILES, T, D), BF16),
            jax.ShapeDtypeStruct((2, FFN_TILES, FFN_SHARD, D), BF16),
            jax.ShapeDtypeStruct((FFN_TILES, FFN_SHARD, D), BF16),
        ],
        scratch_shapes=[pltpu.VMEM((2, FFN_SHARD, D), F32), pltpu.VMEM((FFN_SHARD, D), F32),
                        pltpu.VMEM((tt, D), BF16),
                        pltpu.VMEM((tt, FFN_SHARD), BF16), pltpu.VMEM((tt, FFN_SHARD), BF16),
                        pltpu.VMEM((tt, FFN_SHARD), BF16)],
        compiler_params=_cp("parallel", "arbitrary"),
        name=f"ffn{tag}_bwd_{layer}",
    )(hn, a, b, dres, w_in, w_out)


def _rms_bwd(x, g, dy):
    rstd = lax.rsqrt(jnp.mean(x * x, axis=-1, keepdims=True) + EPS)
    xh = x * rstd
    u = dy * g
    dx = rstd * (u - xh * jnp.mean(u * xh, axis=-1, keepdims=True))
    return dx, jnp.sum(dy * xh, axis=0, keepdims=True)


def norm_bwd(x, g_row, dy_parts, dres, name):
    p = dy_parts.shape[0]
    tt = 512

    def body(x_ref, g_ref, dy_ref, dres_ref, out_ref, dg_ref):
        i = pl.program_id(0)
        dy = dy_ref[0].astype(F32)
        for q in range(1, p):
            dy = dy + dy_ref[q].astype(F32)
        dx, dg = _rms_bwd(x_ref[...], g_ref[...], dy)
        out_ref[...] = dres_ref[...] + dx

        @pl.when(i == 0)
        def _():
            dg_ref[...] = dg

        @pl.when(i > 0)
        def _():
            dg_ref[...] += dg

    return pl.pallas_call(
        body,
        grid=(T // tt,),
        in_specs=[
            pl.BlockSpec((tt, D), lambda i: (i, 0)),
            pl.BlockSpec((1, D), lambda i: (0, 0)),
            pl.BlockSpec((p, tt, D), lambda i: (0, i, 0)),
            pl.BlockSpec((tt, D), lambda i: (i, 0)),
        ],
        out_specs=[pl.BlockSpec((tt, D), lambda i: (i, 0)), pl.BlockSpec((1, D), lambda i: (0, 0))],
        out_shape=[jax.ShapeDtypeStruct((T, D), F32), jax.ShapeDtypeStruct((1, D), F32)],
        compiler_params=_cp("arbitrary"),
        name=name,
    )(x, g_row, dy_parts, dres)


def final_loss(h, g_row, target):
    tt = 512

    def body(h_ref, g_ref, t_ref, dres_ref, dg_ref, loss_ref):
        i = pl.program_id(0)
        x = h_ref[...]
        g = g_ref[...]
        err = _rms(x, g) - t_ref[...]
        part = 0.5 * jnp.sum(jnp.mean(err * err, axis=-1, keepdims=True), axis=0, keepdims=True)
        dx, dg = _rms_bwd(x, g, err * (1.0 / D))
        dres_ref[...] = dx
        part = jnp.broadcast_to(part, loss_ref.shape)

        @pl.when(i == 0)
        def _():
            dg_ref[...] = dg
            loss_ref[...] = part

        @pl.when(i > 0)
        def _():
            dg_ref[...] += dg
            loss_ref[...] += part

    return pl.pallas_call(
        body,
        grid=(T // tt,),
        in_specs=[
            pl.BlockSpec((tt, D), lambda i: (i, 0)),
            pl.BlockSpec((1, D), lambda i: (0, 0)),
            pl.BlockSpec((tt, D), lambda i: (i, 0)),
        ],
        out_specs=[
            pl.BlockSpec((tt, D), lambda i: (i, 0)),
            pl.BlockSpec((1, D), lambda i: (0, 0)),
            pl.BlockSpec((8, 128), lambda i: (0, 0)),
        ],
        out_shape=[
            jax.ShapeDtypeStruct((T, D), F32),
            jax.ShapeDtypeStruct((1, D), F32),
            jax.ShapeDtypeStruct((8, 128), F32),
        ],
        compiler_params=_cp("arbitrary"),
        name="final_loss",
    )(h, g_row, target)


PROJ_TT = 256


def rope_tables(pos_col, invf_row):
    tt = 512

    def body(p_ref, f_ref, c_ref, sm_ref, sp_ref):
        ang = p_ref[...].astype(F32) * f_ref[...]
        lane = lax.broadcasted_iota(jnp.int32, ang.shape, 1) % A_HD
        cs = jnp.cos(ang)
        sn = jnp.sin(ang)
        c_ref[...] = jnp.where(lane < ROT, cs, 1.0)
        sm_ref[...] = jnp.where(lane < ROT // 2, -sn, 0.0)
        sp_ref[...] = jnp.where((lane >= ROT // 2) & (lane < ROT), sn, 0.0)

    spec = pl.BlockSpec((tt, 128), lambda i: (i, 0))
    return pl.pallas_call(
        body,
        grid=(T // tt,),
        in_specs=[pl.BlockSpec((tt, 1), lambda i: (i, 0)), pl.BlockSpec((1, 128), lambda i: (0, 0))],
        out_specs=[spec, spec, spec],
        out_shape=[jax.ShapeDtypeStruct((T, 128), F32)] * 3,
        compiler_params=_cp("parallel"),
        name="rope_tables",
    )(pos_col, invf_row)


def make_rope(positions):
    inv_freq = jnp.power(jnp.float32(ROPE_THETA), -jnp.arange(0, ROT, 2, dtype=F32) / ROT)
    per_head = jnp.concatenate([inv_freq, inv_freq, jnp.zeros((A_HD - ROT,), F32)])
    invf_row = jnp.tile(per_head, 2)[None, :]
    return tuple(rope_tables(positions.reshape(T, 1), invf_row))


def _rope(x, c, sm, sp):
    return x * c + pltpu.roll(x, 128 - ROT // 2, 1) * sm + pltpu.roll(x, ROT // 2, 1) * sp


def _rope_t(dy, c, sm, sp):
    return dy * c + pltpu.roll(dy * sm, ROT // 2, 1) + pltpu.roll(dy * sp, 128 - ROT // 2, 1)


def proj_fwd(h, g_row, w, splits, name, rope=None):
    tt = PROJ_TT
    n = w.shape[1]
    n_rope = 0 if rope is None else 3

    def body(h_ref, g_ref, w_ref, *rest):
        tabs = rest[:n_rope]
        hn_ref = rest[n_rope]
        outs = rest[n_rope + 1:]
        hn = _rms(h_ref[...], g_ref[...]).astype(BF16)
        hn_ref[...] = hn
        for k, ((st, wd), o_ref) in enumerate(zip(splits, outs)):
            if rope is not None and k == 0:
                c, sm, sp = (t[...] for t in tabs)
                for gi in range(wd // 128):
                    r = _dot(hn, w_ref[:, st + 128 * gi:st + 128 * (gi + 1)])
                    if gi < 2 * A_W // 128:
                        r = _rope(r, c, sm, sp)
                    o_ref[:, 128 * gi:128 * (gi + 1)] = r
            else:
                o_ref[...] = _dot(hn, w_ref[:, st:st + wd])

    tab_specs = [pl.BlockSpec((tt, 128), lambda i: (i, 0))] * n_rope
    return pl.pallas_call(
        body,
        grid=(T // tt,),
        in_specs=[
            pl.BlockSpec((tt, D), lambda i: (i, 0)),
            pl.BlockSpec((1, D), lambda i: (0, 0)),
            pl.BlockSpec((D, n), lambda i: (0, 0)),
        ] + tab_specs,
        out_specs=[pl.BlockSpec((tt, D), lambda i: (i, 0))]
        + [pl.BlockSpec((tt, wd), lambda i: (i, 0)) for _, wd in splits],
        out_shape=[jax.ShapeDtypeStruct((T, D), BF16)]
        + [jax.ShapeDtypeStruct((T, wd), F32) for _, wd in splits],
        compiler_params=_cp("parallel"),
        name=name,
    )(h, g_row, w, *(rope or ()))


def proj_bwd_data(x, g_row, w, dparts, splits, dres, name, rope=None, n_rot=0):
    tt = PROJ_TT
    n = w.shape[1]
    n_rope = 0 if rope is None else 3
    k_parts = len(dparts)
    n_main = sum(wd for _, wd in splits if wd % 128 == 0)

    def body(x_ref, g_ref, w_ref, dres_ref, *rest):
        d_refs = rest[:k_parts]
        tabs = rest[k_parts:k_parts + n_rope]
        out_ref, dg_ref, dproj_ref = rest[k_parts + n_rope:k_parts + n_rope + 3]
        i = pl.program_id(0)
        dhn = jnp.zeros((tt, D), F32)
        for k, ((st, wd), d_ref) in enumerate(zip(splits, d_refs)):
            if k < n_rot:
                c, sm, sp = (t[...] for t in tabs)
                for gi in range(wd // 128):
                    cols = slice(st + 128 * gi, st + 128 * (gi + 1))
                    d = _rope_t(d_ref[:, 128 * gi:128 * (gi + 1)], c, sm, sp).astype(BF16)
                    dproj_ref[:, cols] = d
                    dhn = dhn + _dot_nt(d, w_ref[:, cols])
            else:
                d = d_ref[...].astype(BF16)
                if wd % 128 == 0:
                    dproj_ref[:, st:st + wd] = d
                dhn = dhn + _dot_nt(d, w_ref[:, st:st + wd])
        dx, dg = _rms_bwd(x_ref[...], g_ref[...], dhn)
        out_ref[...] = dres_ref[...] + dx

        @pl.when(i == 0)
        def _():
            dg_ref[...] = dg

        @pl.when(i > 0)
        def _():
            dg_ref[...] += dg

    tab_specs = [pl.BlockSpec((tt, 128), lambda i: (i, 0))] * n_rope
    out_specs = [pl.BlockSpec((tt, D), lambda i: (i, 0)), pl.BlockSpec((1, D), lambda i: (0, 0))]
    out_shape = [jax.ShapeDtypeStruct((T, D), F32), jax.ShapeDtypeStruct((1, D), F32)]
    out_specs.append(pl.BlockSpec((tt, n_main), lambda i: (i, 0)))
    out_shape.append(jax.ShapeDtypeStruct((T, n_main), BF16))
    return pl.pallas_call(
        body,
        grid=(T // tt,),
        in_specs=[
            pl.BlockSpec((tt, D), lambda i: (i, 0)),
            pl.BlockSpec((1, D), lambda i: (0, 0)),
            pl.BlockSpec((D, n), lambda i: (0, 0)),
            pl.BlockSpec((tt, D), lambda i: (i, 0)),
        ] + [pl.BlockSpec((tt, wd), lambda i: (i, 0)) for _, wd in splits] + tab_specs,
        out_specs=out_specs,
        out_shape=out_shape,
        compiler_params=_cp("arbitrary"),
        name=name,
    )(x, g_row, w, dres, *dparts, *(rope or ()))


def mm_tn_full(x, d, out_dtype, name):
    k = x.shape[1]
    n = d.shape[1]
    wn = 512

    def body(x_ref, d_ref, o_ref):
        o_ref[...] = _dot_tn(x_ref[...], d_ref[...].astype(BF16)).astype(out_dtype)

    return pl.pallas_call(
        body,
        grid=(n // wn,),
        in_specs=[pl.BlockSpec((T, k), lambda j: (0, 0), pipeline_mode=pl.Buffered(1)),
                  pl.BlockSpec((T, wn), lambda j: (0, j))],
        out_specs=pl.BlockSpec((k, wn), lambda j: (0, j)),
        out_shape=jax.ShapeDtypeStruct((k, n), out_dtype),
        compiler_params=_cp("parallel"),
        name=name,
    )(x, d)


def cols_to_slabs(main, tail, name):
    nm = main.shape[1]
    n = nm + (0 if tail is None else tail.shape[1])
    w = n // N_DEV
    tr = 256

    def body(*refs):
        m_ref, o_ref = refs[0], refs[-1]
        for s in range(N_DEV):
            a, b = w * s, w * (s + 1)
            if b <= nm:
                o_ref[s] = m_ref[:, a:b].astype(BF16)
            else:
                o_ref[s, :, 0:nm - a] = m_ref[:, a:nm].astype(BF16)
                o_ref[s, :, nm - a:w] = refs[1][:, 0:b - nm].astype(BF16)

    arrays = [main] + ([] if tail is None else [tail])
    return pl.pallas_call(
        body,
        grid=(D // tr,),
        in_specs=[pl.BlockSpec((tr, a.shape[1]), lambda i: (i, 0)) for a in arrays],
        out_specs=pl.BlockSpec((N_DEV, tr, w), lambda i: (0, i, 0)),
        out_shape=jax.ShapeDtypeStruct((N_DEV, D, w), BF16),
        compiler_params=_cp("parallel"),
        name=name,
    )(*arrays)


def mm_tn(x, d, name):
    k = x.shape[1]
    n = d.shape[1]
    wn = n if n <= 512 else 512
    tt = 512

    def body(x_ref, d_ref, o_ref):
        i = pl.program_id(1)
        r = _dot_tn(x_ref[...], d_ref[...].astype(BF16))

        @pl.when(i == 0)
        def _():
            o_ref[...] = r

        @pl.when(i > 0)
        def _():
            o_ref[...] += r

    return pl.pallas_call(
        body,
        grid=(n // wn, T // tt),
        in_specs=[pl.BlockSpec((tt, k), lambda j, i: (i, 0)), pl.BlockSpec((tt, wn), lambda j, i: (i, j))],
        out_specs=pl.BlockSpec((k, wn), lambda j, i: (0, j)),
        out_shape=jax.ShapeDtypeStruct((k, n), F32),
        compiler_params=_cp("parallel", "arbitrary"),
        name=name,
    )(x, d)


CONV_RC = 128
CONV_PAD = 32


def hyb_conv_fwd(u, dw_w, dw_b, name):
    def body(ua_ref, ug_ref, w_ref, b_ref, o_ref, xpad):
        xpad[0:CONV_PAD, :] = jnp.zeros((CONV_PAD, 128), F32)
        xpad[CONV_PAD:, :] = ua_ref[...] * _sigmoid(ug_ref[...])
        for r in range(T // CONV_RC):
            acc = jnp.broadcast_to(b_ref[...], (CONV_RC, 128))
            for j in range(CONV_K):
                acc = acc + w_ref[pl.ds(j, 1), :] * xpad[pl.ds(r * CONV_RC + CONV_PAD - (CONV_K - 1) + j, CONV_RC), :]
            o_ref[r * CONV_RC:(r + 1) * CONV_RC, :] = acc

    nb = CONV_C // 128
    return pl.pallas_call(
        body,
        grid=(nb,),
        in_specs=[
            pl.BlockSpec((T, 128), lambda c: (0, c)),
            pl.BlockSpec((T, 128), lambda c: (0, nb + c)),
            pl.BlockSpec((32, 128), lambda c: (0, c)),
            pl.BlockSpec((1, 128), lambda c: (0, c)),
        ],
        out_specs=pl.BlockSpec((T, 128), lambda c: (0, c)),
        out_shape=jax.ShapeDtypeStruct((T, CONV_C), F32),
        scratch_shapes=[pltpu.VMEM((T + CONV_PAD, 128), F32)],
        compiler_params=_cp("parallel"),
        name=name,
    )(u, u, dw_w, dw_b)


def hyb_conv_bwd(dc, u, dw_w, name):
    def body(dc_ref, ua_ref, ug_ref, w_ref, da_ref, dgate_ref, dw_ref, db_ref, xpad, dcpad, dwacc):
        ua = ua_ref[...]
        sig = _sigmoid(ug_ref[...])
        xpad[0:CONV_PAD, :] = jnp.zeros((CONV_PAD, 128), F32)
        xpad[CONV_PAD:, :] = ua * sig
        dcpad[0:T, :] = dc_ref[...]
        dcpad[T:, :] = jnp.zeros((CONV_PAD, 128), F32)
        dwacc[...] = jnp.zeros_like(dwacc)
        dbacc = jnp.zeros((8, 128), F32)
        for r in range(T // CONV_RC):
            r0 = r * CONV_RC
            dcr = dc_ref[r0:r0 + CONV_RC, :]
            dbacc = dbacc + dcr.reshape(CONV_RC // 8, 8, 128).sum(axis=0)
            dglu = jnp.zeros((CONV_RC, 128), F32)
            for j in range(CONV_K):
                dglu = dglu + w_ref[pl.ds(j, 1), :] * dcpad[pl.ds(r0 + (CONV_K - 1) - j, CONV_RC), :]
                prod = dcr * xpad[pl.ds(r0 + CONV_PAD - (CONV_K - 1) + j, CONV_RC), :]
                dwacc[8 * j:8 * j + 8, :] += prod.reshape(CONV_RC // 8, 8, 128).sum(axis=0)
            sg = sig[r0:r0 + CONV_RC, :]
            da_ref[r0:r0 + CONV_RC, :] = dglu * sg
            dgate_ref[r0:r0 + CONV_RC, :] = dglu * ua[r0:r0 + CONV_RC, :] * sg * (1.0 - sg)
        for j in range(CONV_K):
            dw_ref[pl.ds(j, 1), :] = jnp.sum(dwacc[8 * j:8 * j + 8, :], axis=0, keepdims=True)
        dw_ref[pl.ds(CONV_K, 1), :] = jnp.zeros((1, 128), F32)
        db_ref[...] = jnp.sum(dbacc, axis=0, keepdims=True)

    nb = CONV_C // 128
    col = pl.BlockSpec((T, 128), lambda c: (0, c))
    return pl.pallas_call(
        body,
        grid=(nb,),
        in_specs=[col, col, pl.BlockSpec((T, 128), lambda c: (0, nb + c)), pl.BlockSpec((32, 128), lambda c: (0, c))],
        out_specs=[col, col, pl.BlockSpec((32, 128), lambda c: (0, c)), pl.BlockSpec((1, 128), lambda c: (0, c))],
        out_shape=[
            jax.ShapeDtypeStruct((T, CONV_C), F32),
            jax.ShapeDtypeStruct((T, CONV_C), F32),
            jax.ShapeDtypeStruct((32, CONV_C), F32),
            jax.ShapeDtypeStruct((1, CONV_C), F32),
        ],
        scratch_shapes=[
            pltpu.VMEM((T + CONV_PAD, 128), F32),
            pltpu.VMEM((T + CONV_PAD, 128), F32),
            pltpu.VMEM((8 * 32, 128), F32),
        ],
        compiler_params=_cp("parallel"),
        name=name,
    )(dc, u, u, dw_w)


ATT_SCALE = A_HD ** -0.5
N_BLK = T // BLK


def _att_masks():
    i = lax.broadcasted_iota(jnp.int32, (BLK, 2 * BLK), 0)
    j = lax.broadcasted_iota(jnp.int32, (BLK, 2 * BLK), 1)
    band = (j >= i) & (j <= i + BLK)
    i1 = lax.broadcasted_iota(jnp.int32, (BLK, BLK), 0)
    j1 = lax.broadcasted_iota(jnp.int32, (BLK, BLK), 1)
    return band, j1 <= i1


def _att_rows(d, t, first):
    if first:
        base = t
        return pl.ds(base, BLK, stride=d), pl.ds(base, BLK, stride=d)
    c = t % d
    n = t // d + 1
    base = c + (BLK * d) * n
    return pl.ds(base, BLK, stride=d), pl.ds(base - BLK * d, 2 * BLK, stride=d)


def _stack_heads(x, head0):
    return jnp.concatenate([jnp.where(head0, x, 0.0), jnp.where(head0, 0.0, x)], axis=0)


def _loop_pairs(n, block, per=2):
    def several(i, carry):
        for k in range(per):
            block(per * i + k, carry)
        return carry

    if n >= per:
        lax.fori_loop(0, n // per, several, 0)
    for t in range(n - n % per, n):
        block(t, 0)


def attn_fwd(qkv, name):
    def body(q_ref, k_ref, v_ref, o_ref, lse_ref, og, lg):
        band, tri = _att_masks()
        band, tri = jnp.concatenate([band, band], axis=0), jnp.concatenate([tri, tri], axis=0)
        head0 = lax.broadcasted_iota(jnp.int32, (BLK, 128), 1) < A_HD
        for g, d in enumerate(DILATIONS):
            def block(t, carry, first, g=g, d=d):
                rq, rk = _att_rows(d, t, first)
                q2 = q_ref[rq, :]
                k2 = k_ref[rk, :].astype(BF16)
                v2 = v_ref[rk, :].astype(BF16)
                qs = _stack_heads(q2, head0).astype(BF16)
                s = _dot_nt(qs, k2) * ATT_SCALE
                s = jnp.where(tri if first else band, s, -jnp.inf)
                m = jnp.max(s, axis=-1, keepdims=True)
                p = jnp.exp(s - m)
                den = jnp.sum(p, axis=-1, keepdims=True)
                o = _dot(p.astype(BF16), v2) / den
                l = m + jnp.log(den)
                og[g, rq, :] = jnp.where(head0, o[0:BLK], o[BLK:2 * BLK])
                lg[g, rq, :] = jnp.where(head0, l[0:BLK], l[BLK:2 * BLK])
                return carry

            _loop_pairs(d, functools.partial(block, first=True), per=4)
            _loop_pairs(N_BLK - d, functools.partial(block, first=False), per=4)
        rc = 256
        for r in range(T // rc):
            rows = pl.ds(r * rc, rc)
            l0, l1, l2 = lg[0, rows, :], lg[1, rows, :], lg[2, rows, :]
            m = jnp.maximum(jnp.maximum(l0, l1), l2)
            e0, e1, e2 = jnp.exp(l0 - m), jnp.exp(l1 - m), jnp.exp(l2 - m)
            z = e0 + e1 + e2
            o_ref[rows, :] = (e0 / z) * og[0, rows, :] + (e1 / z) * og[1, rows, :] + (e2 / z) * og[2, rows, :]
            lse_ref[rows, :] = m + jnp.log(z)

    npair = A_HEADS // 2
    col = lambda off: pl.BlockSpec((T, 128), lambda p: (0, off + p))
    return pl.pallas_call(
        body,
        grid=(npair,),
        in_specs=[col(0), col(npair), col(2 * npair)],
        out_specs=[col(0), col(0)],
        out_shape=[jax.ShapeDtypeStruct((T, A_W), F32), jax.ShapeDtypeStruct((T, A_W), F32)],
        scratch_shapes=[pltpu.VMEM((3, T, 128), F32), pltpu.VMEM((3, T, 128), F32)],
        compiler_params=_cp("parallel"),
        name=name,
    )(qkv, qkv, qkv)


def attn_bwd(qkv, o, lse, do, name):
    def body(q_ref, k_ref, v_ref, o_ref, lse_ref, do_ref, dq_ref, dk_ref, dv_ref):
        band, tri = _att_masks()
        band, tri = jnp.concatenate([band, band], axis=0), jnp.concatenate([tri, tri], axis=0)
        head0 = lax.broadcasted_iota(jnp.int32, (BLK, 128), 1) < A_HD
        dq_ref[...] = jnp.zeros_like(dq_ref)
        dk_ref[...] = jnp.zeros_like(dk_ref)
        dv_ref[...] = jnp.zeros_like(dv_ref)
        for d in DILATIONS:
            def block(t, carry, first, d=d):
                rq, rk = _att_rows(d, t, first)
                k2 = k_ref[rk, :].astype(BF16)
                v2 = v_ref[rk, :].astype(BF16)
                do2 = do_ref[rq, :]
                l2 = lse_ref[rq, :]
                qs = _stack_heads(q_ref[rq, :], head0).astype(BF16)
                dos = _stack_heads(do2, head0).astype(BF16)
                l = jnp.concatenate([l2[:, 0:1], l2[:, A_HD:A_HD + 1]], axis=0)
                dd = jnp.sum(_stack_heads(do2 * o_ref[rq, :], head0), axis=-1, keepdims=True)
                s = _dot_nt(qs, k2) * ATT_SCALE
                p = jnp.where(tri if first else band, jnp.exp(s - l), 0.0)
                dp = _dot_nt(dos, v2)
                ds = (p * (dp - dd) * ATT_SCALE).astype(BF16)
                dq = _dot(ds, k2)
                dq_ref[rq, :] += jnp.where(head0, dq[0:BLK], dq[BLK:2 * BLK])
                dk_ref[rk, :] += _dot_tn(ds, qs)
                dv_ref[rk, :] += _dot_tn(p.astype(BF16), dos)
                return carry

            _loop_pairs(d, functools.partial(block, first=True), per=4)
            _loop_pairs(N_BLK - d, functools.partial(block, first=False), per=4)

    npair = A_HEADS // 2
    col = lambda off: pl.BlockSpec((T, 128), lambda p: (0, off + p))
    return pl.pallas_call(
        body,
        grid=(npair,),
        in_specs=[col(0), col(npair), col(2 * npair), col(0), col(0), col(0)],
        out_specs=[col(0), col(0), col(0)],
        out_shape=[jax.ShapeDtypeStruct((T, A_W), F32)] * 3,
        compiler_params=_cp("parallel"),
        name=name,
    )(qkv, qkv, qkv, o, lse, do)


def _ln_silu(x, g, b):
    mu = jnp.mean(x, axis=-1, keepdims=True)
    xc = x - mu
    rstd = lax.rsqrt(jnp.mean(xc * xc, axis=-1, keepdims=True) + EPS)
    xh = xc * rstd
    y = xh * g + b
    sig = _sigmoid(y)
    return y * sig, (xh, rstd, y, sig)


def hyb_out_fwd(h, attn, cpre, ln_g, ln_b, w_out, name):
    tt = 512

    def body(h_ref, a_ref, c_ref, g_ref, b_ref, w_ref, hnew_ref, cat_ref):
        cn, _ = _ln_silu(c_ref[...], g_ref[...], b_ref[...])
        ab = a_ref[...].astype(BF16)
        cb = cn.astype(BF16)
        cat_ref[:, 0:A_W] = ab
        cat_ref[:, A_W:D] = cb
        hnew_ref[...] = h_ref[...] + _dot(ab, w_ref[0:A_W, :]) + _dot(cb, w_ref[A_W:D, :])

    half = pl.BlockSpec((tt, A_W), lambda i: (i, 0))
    vec = pl.BlockSpec((1, CONV_C), lambda i: (0, 0))
    full = pl.BlockSpec((tt, D), lambda i: (i, 0))
    return pl.pallas_call(
        body,
        grid=(T // tt,),
        in_specs=[full, half, half, vec, vec, pl.BlockSpec((D, D), lambda i: (0, 0))],
        out_specs=[full, full],
        out_shape=[jax.ShapeDtypeStruct((T, D), F32), jax.ShapeDtypeStruct((T, D), BF16)],
        compiler_params=_cp("parallel"),
        name=name,
    )(h, attn, cpre, ln_g, ln_b, w_out)


def hyb_out_bwd(dres, cpre, ln_g, ln_b, w_out, name):
    tt = 512

    def body(d_ref, c_ref, g_ref, b_ref, w_ref, da_ref, dc_ref, dg_ref, db_ref):
        i = pl.program_id(0)
        db16 = d_ref[...].astype(BF16)
        da_ref[...] = _dot_nt(db16, w_ref[0:A_W, :])
        dcn = _dot_nt(db16, w_ref[A_W:D, :])
        g = g_ref[...]
        _, (xh, rstd, y, sig) = _ln_silu(c_ref[...], g, b_ref[...])
        dy = dcn * _dsilu(y, sig)
        dxh = dy * g
        dc_ref[...] = rstd * (dxh - jnp.mean(dxh, axis=-1, keepdims=True)
                              - xh * jnp.mean(dxh * xh, axis=-1, keepdims=True))
        dg = jnp.sum(dy * xh, axis=0, keepdims=True)
        db = jnp.sum(dy, axis=0, keepdims=True)

        @pl.when(i == 0)
        def _():
            dg_ref[...] = dg
            db_ref[...] = db

        @pl.when(i > 0)
        def _():
            dg_ref[...] += dg
            db_ref[...] += db

    half = pl.BlockSpec((tt, A_W), lambda i: (i, 0))
    vec = pl.BlockSpec((1, CONV_C), lambda i: (0, 0))
    return pl.pallas_call(
        body,
        grid=(T // tt,),
        in_specs=[pl.BlockSpec((tt, D), lambda i: (i, 0)), half, vec, vec, pl.BlockSpec((D, D), lambda i: (0, 0))],
        out_specs=[half, half, vec, vec],
        out_shape=[
            jax.ShapeDtypeStruct((T, A_W), F32),
            jax.ShapeDtypeStruct((T, CONV_C), F32),
            jax.ShapeDtypeStruct((1, CONV_C), F32),
            jax.ShapeDtypeStruct((1, CONV_C), F32),
        ],
        compiler_params=_cp("arbitrary"),
        name=name,
    )(dres, cpre, ln_g, ln_b, w_out)


def hybrid_fwd(h, g_row, w_in, dw_w, dw_b, ln_g, ln_b, w_out, rope, tag):
    hn, qkv, u = proj_fwd(h, g_row, w_in, [(0, 3 * A_W), (3 * A_W, 2 * CONV_C)], f"hyb_proj_{tag}", rope=rope)
    cpre = hyb_conv_fwd(u, dw_w, dw_b, f"hyb_conv_{tag}")
    attn, lse = attn_fwd(qkv, f"attn_fwd_{tag}")
    hnew, cat = hyb_out_fwd(h, attn, cpre, ln_g, ln_b, w_out, f"hyb_out_{tag}")
    return hnew, (h, hn, qkv, u, cpre, attn, lse, cat)


def hybrid_bwd(dres, saved, g_row, w_in, dw_w, ln_g, ln_b, w_out, rope, tag):
    h, hn, qkv, u, cpre, attn, lse, cat = saved
    d_attn, d_cpre, d_lng, d_lnb = hyb_out_bwd(dres, cpre, ln_g, ln_b, w_out, f"hyb_out_bwd_{tag}")
    d_wout = mm_tn_full(cat, dres, BF16, f"hyb_wout_grad_{tag}")
    d_a, d_gate, d_dw, d_db = hyb_conv_bwd(d_cpre, u, dw_w, f"hyb_conv_bwd_{tag}")
    dq, dk, dv = attn_bwd(qkv, attn, lse, d_attn, f"attn_bwd_{tag}")
    splits = [(0, A_W), (A_W, A_W), (2 * A_W, A_W), (3 * A_W, CONV_C), (3 * A_W + CONV_C, CONV_C)]
    dres_new, d_norm, dproj = proj_bwd_data(
        h, g_row, w_in, [dq, dk, dv, d_a, d_gate], splits, dres, f"hyb_proj_bwd_{tag}", rope=rope, n_rot=2)
    d_win = cols_to_slabs(mm_tn_full(hn, dproj, F32, f"hyb_win_grad_{tag}"), None, f"hyb_win_slabs_{tag}")
    return dres_new, dict(norm=d_norm, w_in=d_win, dw_w=d_dw[:CONV_K], dw_b=d_db, ln_g=d_lng, ln_b=d_lnb, w_out=d_wout)


G_SCALE = G_DK ** -0.5
GP_RC = 256
GP_PAD = 8


def gdn_prep_fwd(x, conv_w, name):
    def body(x_ref, w_ref, o_ref, xpad):
        cb = pl.program_id(0)
        xpad[0:GP_PAD, :] = jnp.zeros((GP_PAD, 128), F32)
        xpad[GP_PAD:, :] = x_ref[...]
        for r in range(T // GP_RC):
            r0 = r * GP_RC
            y = jnp.zeros((GP_RC, 128), F32)
            for j in range(G_CONV):
                y = y + w_ref[pl.ds(j, 1), :] * xpad[pl.ds(r0 + GP_PAD - (G_CONV - 1) + j, GP_RC), :]
            s = y * _sigmoid(y)
            n = lax.rsqrt(jnp.sum(s * s, axis=-1, keepdims=True) + EPS)
            o_ref[r0:r0 + GP_RC, :] = s * jnp.where(cb < 2 * G_HEADS, n, 1.0)

    nb = G_QKV // 128
    return pl.pallas_call(
        body,
        grid=(nb,),
        in_specs=[pl.BlockSpec((T, 128), lambda c: (0, c)), pl.BlockSpec((G_CONV, 128), lambda c: (0, c))],
        out_specs=pl.BlockSpec((T, 128), lambda c: (0, c)),
        out_shape=jax.ShapeDtypeStruct((T, G_QKV), F32),
        scratch_shapes=[pltpu.VMEM((T + GP_PAD, 128), F32)],
        compiler_params=_cp("parallel"),
        name=name,
    )(x, conv_w)


def gdn_prep_bwd(dout, x, conv_w, part, l2, name):
    def body(d_ref, x_ref, w_ref, dx_ref, dw_ref, xpad, dypad, dwacc):
        xpad[0:GP_PAD, :] = jnp.zeros((GP_PAD, 128), F32)
        xpad[GP_PAD:, :] = x_ref[...]
        dypad[T:, :] = jnp.zeros((GP_PAD, 128), F32)
        dwacc[...] = jnp.zeros_like(dwacc)
        for r in range(T // GP_RC):
            r0 = r * GP_RC
            y = jnp.zeros((GP_RC, 128), F32)
            xs = []
            for j in range(G_CONV):
                xj = xpad[pl.ds(r0 + GP_PAD - (G_CONV - 1) + j, GP_RC), :]
                xs.append(xj)
                y = y + w_ref[pl.ds(j, 1), :] * xj
            sig = _sigmoid(y)
            s = y * sig
            d = d_ref[r0:r0 + GP_RC, :]
            if l2:
                n = lax.rsqrt(jnp.sum(s * s, axis=-1, keepdims=True) + EPS)
                out = s * n
                d = n * (d - out * jnp.sum(d * out, axis=-1, keepdims=True))
            dy = d * _dsilu(y, sig)
            dypad[r0:r0 + GP_RC, :] = dy
            for j in range(G_CONV):
                dwacc[8 * j:8 * j + 8, :] += (dy * xs[j]).reshape(GP_RC // 8, 8, 128).sum(axis=0)
        for r in range(T // GP_RC):
            r0 = r * GP_RC
            dx = jnp.zeros((GP_RC, 128), F32)
            for j in range(G_CONV):
                dx = dx + w_ref[pl.ds(j, 1), :] * dypad[pl.ds(r0 + (G_CONV - 1) - j, GP_RC), :]
            dx_ref[r0:r0 + GP_RC, :] = dx
        for j in range(G_CONV):
            dw_ref[pl.ds(j, 1), :] = jnp.sum(dwacc[8 * j:8 * j + 8, :], axis=0, keepdims=True)

    nb = G_HEADS
    off = part * nb
    col = pl.BlockSpec((T, 128), lambda c: (0, c))
    return pl.pallas_call(
        body,
        grid=(nb,),
        in_specs=[col, pl.BlockSpec((T, 128), lambda c: (0, off + c)), pl.BlockSpec((G_CONV, 128), lambda c: (0, off + c))],
        out_specs=[col, pl.BlockSpec((G_CONV, 128), lambda c: (0, c))],
        out_shape=[jax.ShapeDtypeStruct((T, G_HEADS * G_DK), F32), jax.ShapeDtypeStruct((G_CONV, G_HEADS * G_DK), F32)],
        scratch_shapes=[
            pltpu.VMEM((T + GP_PAD, 128), F32),
            pltpu.VMEM((T + GP_PAD, 128), F32),
            pltpu.VMEM((8 * G_CONV, 128), F32),
        ],
        compiler_params=_cp("parallel"),
        name=name,
    )(dout, x, conv_w)


def _seg_cumsum(x, reverse=False):
    row = lax.broadcasted_iota(jnp.int32, x.shape, 0) % CH
    s = 1
    while s < CH:
        if reverse:
            x = x + jnp.where(row < CH - s, pltpu.roll(x, x.shape[0] - s, 0), 0.0)
        else:
            x = x + jnp.where(row >= s, pltpu.roll(x, s, 0), 0.0)
        s *= 2
    return x


def _gdn_gates(ba_ref, alog_ref, dt_ref, h):
    ba = ba_ref[...]
    lane = lax.broadcasted_iota(jnp.int32, ba.shape, 1)
    b_col = jnp.sum(jnp.where(lane == h, ba, 0.0), axis=1, keepdims=True)
    a_col = jnp.sum(jnp.where(lane == G_HEADS + h, ba, 0.0), axis=1, keepdims=True)
    lane8 = lax.broadcasted_iota(jnp.int32, (1, G_HEADS), 1)
    alog = jnp.sum(jnp.where(lane8 == h, alog_ref[...], 0.0), axis=1, keepdims=True)
    dt = jnp.sum(jnp.where(lane8 == h, dt_ref[...], 0.0), axis=1, keepdims=True)
    beta = _sigmoid(b_col)
    xa = a_col + dt
    softplus = jnp.maximum(xa, 0.0) + jnp.log(1.0 + jnp.exp(-jnp.abs(xa)))
    ea = jnp.exp(alog)
    return beta, -ea * softplus, xa, ea


def _chunk_masks():
    i = lax.broadcasted_iota(jnp.int32, (CH, CH), 0)
    j = lax.broadcasted_iota(jnp.int32, (CH, CH), 1)
    return i >= j, i > j, i, j


def _decay(gcc, causal):
    gm = gcc[:, 0:CH]
    return jnp.where(causal, jnp.exp(jnp.minimum(gm - gm.T, 0.0)), 0.0)


def _split(a):
    hi = a.astype(BF16)
    return hi, (a - hi.astype(F32)).astype(BF16)


def _dot3(a, b):
    ah, al = _split(a)
    bh, bl = _split(b)
    return _dot(ah, bh) + (_dot(ah, bl) + _dot(al, bh))


def _unit_lower_inverse(lms, i, j):
    eye = jnp.where(i == j, 1.0, 0.0)
    ms = [None] * len(lms)
    b = 1
    while b < CH:
        pair = ((i // (2 * b)) == (j // (2 * b))) & ((i // b) % 2 == 1) & ((j // b) % 2 == 0)
        lbs = [jnp.where(pair, lm, 0.0) for lm in lms]
        if b == 1:
            ms = [eye - lb for lb in lbs]
        else:
            ts = [_dot3(m, lb) for m, lb in zip(ms, lbs)]
            ms = [m - _dot3(t, m) for m, t in zip(ms, ts)]
        b *= 2
    return ms


def gdn_local_fwd(qkv, ba, alog, dtb, name):
    def body(q_ref, k_ref, v_ref, ba_ref, al_ref, dt_ref, u_ref, w_ref, qd_ref, kd_ref, at_ref, el_ref, ti_ref, gcs):
        h = pl.program_id(1)
        beta, g, _, _ = _gdn_gates(ba_ref, al_ref, dt_ref, h)
        gc = _seg_cumsum(jnp.broadcast_to(g, (GRP, 128)))
        gcs[...] = gc
        causal, strict, i, j = _chunk_masks()
        lms = []
        for c in range(CPG):
            r = slice(c * CH, (c + 1) * CH)
            q, k = q_ref[r, :], k_ref[r, :]
            gcc = gc[r, :]
            ec = jnp.exp(gcc)
            gl = gcs[pl.ds(c * CH + CH - 1, 1), :]
            dm = _decay(gcc, causal)
            kbf = k.astype(BF16)
            a1 = _dot_nt((k * beta[r, :]).astype(BF16), kbf)
            lms.append(jnp.where(strict, a1 * dm, 0.0))
            qs = q * G_SCALE
            qd_ref[r, :] = (qs * ec).astype(BF16)
            kd_ref[r, :] = (k * jnp.exp(gl - gcc)).astype(BF16)
            at_ref[r, :] = (_dot_nt(qs.astype(BF16), kbf) * dm).astype(BF16)
            el_ref[pl.ds(c, 1), :] = jnp.exp(gl)
        tinvs = _unit_lower_inverse(lms, i, j)
        for c in range(CPG):
            r = slice(c * CH, (c + 1) * CH)
            bt = beta[r, :]
            tb = tinvs[c].astype(BF16)
            u_ref[r, :] = _dot(tb, (v_ref[r, :] * bt).astype(BF16))
            w_ref[r, :] = _dot(tb, (k_ref[r, :] * bt * jnp.exp(gc[r, :])).astype(BF16)).astype(BF16)
            ti_ref[r, :] = tinvs[c]

    hd = lambda off: pl.BlockSpec((GRP, 128), lambda i, h: (i, off + h))
    vec = pl.BlockSpec((1, G_HEADS), lambda i, h: (0, 0))
    sq = pl.BlockSpec((None, GRP, CH), lambda i, h: (h, i, 0))
    return pl.pallas_call(
        body,
        grid=(N_GRP, G_HEADS),
        in_specs=[hd(0), hd(G_HEADS), hd(2 * G_HEADS), pl.BlockSpec((GRP, 2 * G_HEADS), lambda i, h: (i, 0)), vec, vec],
        out_specs=[hd(0), hd(0), hd(0), hd(0), sq, pl.BlockSpec((None, CPG, 128), lambda i, h: (h, i, 0)), sq],
        out_shape=[
            jax.ShapeDtypeStruct((T, D), F32),
            jax.ShapeDtypeStruct((T, D), BF16),
            jax.ShapeDtypeStruct((T, D), BF16),
            jax.ShapeDtypeStruct((T, D), BF16),
            jax.ShapeDtypeStruct((G_HEADS, T, CH), BF16),
            jax.ShapeDtypeStruct((G_HEADS, T // CH, 128), F32),
            jax.ShapeDtypeStruct((G_HEADS, T, CH), F32),
        ],
        scratch_shapes=[pltpu.VMEM((GRP, 128), F32)],
        compiler_params=_cp("parallel", "parallel"),
        name=name,
    )(qkv, qkv, qkv, ba, alog, dtb)


def gdn_rec_fwd(u, w, qd, kd, at, el, name):
    def body(u_ref, w_ref, qd_ref, kd_ref, at_ref, el_ref, o_ref, vn_ref, st_ref, s_scr):
        @pl.when(pl.program_id(0) == 0)
        def _():
            s_scr[...] = jnp.zeros_like(s_scr)

        states = [s_scr[h] for h in range(G_HEADS)]
        for c in range(CPG):
            r = slice(c * CH, (c + 1) * CH)
            for h in range(G_HEADS):
                ln = slice(h * 128, (h + 1) * 128)
                s = states[h]
                st_ref[h, c] = s
                sb = s.astype(BF16)
                vn = (u_ref[r, ln] - _dot(w_ref[r, ln], sb)).astype(BF16)
                o_ref[r, ln] = _dot(qd_ref[r, ln], sb) + _dot(at_ref[h, r, :], vn)
                states[h] = s * el_ref[h, pl.ds(c, 1), :] + _dot_tn(kd_ref[r, ln], vn)
                vn_ref[r, ln] = vn
        for h in range(G_HEADS):
            s_scr[h] = states[h]

    row = pl.BlockSpec((GRP, D), lambda i: (i, 0))
    return pl.pallas_call(
        body,
        grid=(N_GRP,),
        in_specs=[row, row, row, row, pl.BlockSpec((G_HEADS, GRP, CH), lambda i: (0, i, 0)),
                  pl.BlockSpec((G_HEADS, CPG, 128), lambda i: (0, i, 0))],
        out_specs=[row, row, pl.BlockSpec((G_HEADS, CPG, 128, 128), lambda i: (0, i, 0, 0))],
        out_shape=[
            jax.ShapeDtypeStruct((T, D), F32),
            jax.ShapeDtypeStruct((T, D), BF16),
            jax.ShapeDtypeStruct((G_HEADS, T // CH, 128, 128), F32),
        ],
        scratch_shapes=[pltpu.VMEM((G_HEADS, 128, 128), F32)],
        compiler_params=_cp("arbitrary"),
        name=name,
    )(u, w, qd, kd, at, el)


def gdn_rec_bwd(do, w, qd, kd, at, el, vn, st, name):
    def body(do_ref, w_ref, qd_ref, kd_ref, at_ref, el_ref, vn_ref, st_ref,
             du_ref, dw_ref, dqd_ref, dkd_ref, dat_ref, del_ref, ds_scr):
        @pl.when(pl.program_id(0) == 0)
        def _():
            ds_scr[...] = jnp.zeros_like(ds_scr)

        dstates = [ds_scr[h] for h in range(G_HEADS)]
        for c in reversed(range(CPG)):
            r = slice(c * CH, (c + 1) * CH)
            for h in range(G_HEADS):
                ln = slice(h * 128, (h + 1) * 128)
                ds = dstates[h]
                dsb = ds.astype(BF16)
                sn = st_ref[h, c]
                snb = sn.astype(BF16)
                dob = do_ref[r, ln].astype(BF16)
                vnb = vn_ref[r, ln]
                dvn = (_dot(kd_ref[r, ln], dsb) + _dot_tn(at_ref[h, r, :], dob)).astype(BF16)
                du_ref[r, ln] = dvn
                dkd_ref[r, ln] = _dot_nt(vnb, dsb)
                tot = jnp.sum(jnp.sum(ds * sn, axis=1, keepdims=True), axis=0, keepdims=True)
                del_ref[h, pl.ds(c, 1), :] = jnp.broadcast_to(tot, (1, 128))
                dqd_ref[r, ln] = _dot_nt(dob, snb)
                dat_ref[h, r, :] = _dot_nt(dob, vnb)
                dw_ref[r, ln] = (-_dot_nt(dvn, snb)).astype(BF16)
                dstates[h] = ds * el_ref[h, pl.ds(c, 1), :] + _dot_tn(qd_ref[r, ln], dob) - _dot_tn(w_ref[r, ln], dvn)
        for h in range(G_HEADS):
            ds_scr[h] = dstates[h]

    last = N_GRP - 1
    row = pl.BlockSpec((GRP, D), lambda i: (last - i, 0))
    sq = pl.BlockSpec((G_HEADS, GRP, CH), lambda i: (0, last - i, 0))
    sc = pl.BlockSpec((G_HEADS, CPG, 128), lambda i: (0, last - i, 0))
    return pl.pallas_call(
        body,
        grid=(N_GRP,),
        in_specs=[row, row, row, row, sq, sc, row, pl.BlockSpec((G_HEADS, CPG, 128, 128), lambda i: (0, last - i, 0, 0))],
        out_specs=[row, row, row, row, sq, sc],
        out_shape=[
            jax.ShapeDtypeStruct((T, D), BF16),
            jax.ShapeDtypeStruct((T, D), BF16),
            jax.ShapeDtypeStruct((T, D), F32),
            jax.ShapeDtypeStruct((T, D), F32),
            jax.ShapeDtypeStruct((G_HEADS, T, CH), F32),
            jax.ShapeDtypeStruct((G_HEADS, T // CH, 128), F32),
        ],
        scratch_shapes=[pltpu.VMEM((G_HEADS, 128, 128), F32)],
        compiler_params=_cp("arbitrary"),
        name=name,
    )(do, w, qd, kd, at, el, vn, st)


def gdn_local_bwd(qkv, ba, alog, dtb, tinv, du, dw, dqd, dkd, dat, dl, name):
    def body(q_ref, k_ref, v_ref, ba_ref, al_ref, dt_ref, ti_ref, du_ref, dw_ref, dqd_ref, dkd_ref, dat_ref, dl_ref,
             dq_ref, dk_ref, dv_ref, dba_ref, dal_ref, ddt_ref, gcs):
        gi = pl.program_id(0)
        h = pl.program_id(1)
        beta, g, xa, ea = _gdn_gates(ba_ref, al_ref, dt_ref, h)
        gc = _seg_cumsum(jnp.broadcast_to(g, (GRP, 128)))
        gcs[...] = gc
        causal, strict, _, _ = _chunk_masks()
        dgc_l, dgl_l, dbeta_l, state = [], [], [], []
        for c in range(CPG):
            r = slice(c * CH, (c + 1) * CH)
            q, k, v = q_ref[r, :], k_ref[r, :], v_ref[r, :]
            bt = beta[r, :]
            gcc = gc[r, :]
            ec = jnp.exp(gcc)
            gl = gcs[pl.ds(c * CH + CH - 1, 1), :]
            f2 = jnp.exp(gl - gcc)
            elc = jnp.exp(gl)
            dm = _decay(gcc, causal)
            qs = q * G_SCALE
            kb = k * bt
            vb = v * bt
            kbe = kb * ec
            kbf, kbb, qsb = k.astype(BF16), kb.astype(BF16), qs.astype(BF16)
            a1 = _dot_nt(kbb, kbf)
            qk = _dot_nt(qsb, kbf)
            ti = ti_ref[r, :]
            tb = ti.astype(BF16)
            du_c, dw_c = du_ref[r, :], dw_ref[r, :]
            dqd_c, dkd_c, dat_c = dqd_ref[r, :], dkd_ref[r, :], dat_ref[r, :]

            dqs = dqd_c * ec
            d_e = jnp.sum(dqd_c * qs, axis=1, keepdims=True)
            dk = dkd_c * f2
            tcol = jnp.sum(dkd_c * k, axis=1, keepdims=True) * f2[:, 0:1]
            dgl = jnp.sum(tcol, axis=0, keepdims=True) + dl_ref[pl.ds(c, 1), 0:1] * elc[:, 0:1]
            dgc = -tcol
            dqk = (dat_c * dm).astype(BF16)
            d_d = dat_c * qk
            dqs = dqs + _dot(dqk, kbf)
            dk = dk + _dot_tn(dqk, qsb)
            dtinv = _dot_nt(du_c, vb.astype(BF16)) + _dot_nt(dw_c, kbe.astype(BF16))
            dvb = _dot_tn(tb, du_c)
            dkbe = _dot_tn(tb, dw_c)
            dq_ref[r, :] = dqs * G_SCALE
            state.append((ti.T, dtinv, dm, a1, dkbe, dvb, d_d, dk, d_e, dgc, dgl))

        xs = [_dot3(st[0], st[1]) for st in state]
        dlms = [jnp.where(strict, -_dot3(x, st[0]), 0.0) for x, st in zip(xs, state)]

        for c in range(CPG):
            r = slice(c * CH, (c + 1) * CH)
            _, _, dm, a1, dkbe, dvb, d_d, dk, d_e, dgc, dgl = state[c]
            dlm = dlms[c]
            k, v = k_ref[r, :], v_ref[r, :]
            bt = beta[r, :]
            ec = jnp.exp(gc[r, :])
            kb = k * bt
            kbf, kbb = k.astype(BF16), kb.astype(BF16)
            da1 = (dlm * dm).astype(BF16)
            d_d = d_d + dlm * a1
            dkb = _dot(da1, kbf) + dkbe * ec
            dk = dk + _dot_tn(da1, kbb)
            d_e = d_e + jnp.sum(dkbe * kb, axis=1, keepdims=True)
            dk = dk + dkb * bt
            dbeta_l.append(jnp.sum(dkb * k, axis=1, keepdims=True) + jnp.sum(dvb * v, axis=1, keepdims=True))
            ddiff = d_d * dm
            dgc = dgc + jnp.sum(ddiff, axis=1, keepdims=True) - jnp.sum(ddiff.T, axis=1, keepdims=True)
            dgc = dgc + d_e * ec[:, 0:1]
            dgc_l.append(dgc)
            dgl_l.append(jnp.broadcast_to(dgl, (CH, 1)))
            dk_ref[r, :] = dk
            dv_ref[r, :] = dvb * bt

        dgc_all = jnp.broadcast_to(jnp.concatenate(dgc_l, axis=0), (GRP, 128))
        dg = _seg_cumsum(dgc_all, reverse=True)[:, 0:1] + jnp.concatenate(dgl_l, axis=0)
        dbeta = jnp.concatenate(dbeta_l, axis=0)
        da = dg * (-ea) * _sigmoid(xa)
        db = dbeta * beta * (1.0 - beta)
        lane = lax.broadcasted_iota(jnp.int32, (GRP, 2 * G_HEADS), 1)
        dba = jnp.where(lane == h, db, 0.0) + jnp.where(lane == G_HEADS + h, da, 0.0)
        lane8 = lax.broadcasted_iota(jnp.int32, (1, G_HEADS), 1)
        dal = jnp.where(lane8 == h, jnp.sum(dg * g, axis=0, keepdims=True), 0.0)
        ddt = jnp.where(lane8 == h, jnp.sum(da, axis=0, keepdims=True), 0.0)

        @pl.when(h == 0)
        def _():
            dba_ref[...] = dba

        @pl.when(h > 0)
        def _():
            dba_ref[...] += dba

        @pl.when((h == 0) & (gi == 0))
        def _():
            dal_ref[...] = dal
            ddt_ref[...] = ddt

        @pl.when((h > 0) | (gi > 0))
        def _():
            dal_ref[...] += dal
            ddt_ref[...] += ddt

    hd = lambda off: pl.BlockSpec((GRP, 128), lambda i, h: (i, off + h))
    vec = pl.BlockSpec((1, G_HEADS), lambda i, h: (0, 0))
    sq = pl.BlockSpec((None, GRP, CH), lambda i, h: (h, i, 0))
    gates = pl.BlockSpec((GRP, 2 * G_HEADS), lambda i, h: (i, 0))
    return pl.pallas_call(
        body,
        grid=(N_GRP, G_HEADS),
        in_specs=[hd(0), hd(G_HEADS), hd(2 * G_HEADS), gates, vec, vec, sq, hd(0), hd(0), hd(0), hd(0), sq,
                  pl.BlockSpec((None, CPG, 128), lambda i, h: (h, i, 0))],
        out_specs=[hd(0), hd(0), hd(0), gates, vec, vec],
        out_shape=[
            jax.ShapeDtypeStruct((T, D), F32),
            jax.ShapeDtypeStruct((T, D), F32),
            jax.ShapeDtypeStruct((T, D), F32),
            jax.ShapeDtypeStruct((T, 2 * G_HEADS), F32),
            jax.ShapeDtypeStruct((1, G_HEADS), F32),
            jax.ShapeDtypeStruct((1, G_HEADS), F32),
        ],
        scratch_shapes=[pltpu.VMEM((GRP, 128), F32)],
        compiler_params=_cp("arbitrary", "arbitrary"),
        name=name,
    )(qkv, qkv, qkv, ba, alog, dtb, tinv, du, dw, dqd, dkd, dat, dl)


def _gated_norm(o, z, g):
    rstd = lax.rsqrt(jnp.mean(o * o, axis=-1, keepdims=True) + EPS)
    oh = o * rstd
    sig = _sigmoid(z)
    return oh, rstd, sig


def gdn_out_fwd(h, o, z, norm_g, w_out, name):
    tt = 512

    def body(h_ref, o_ref, z_ref, g_ref, w_ref, hnew_ref, cat_ref):
        g = g_ref[...]
        for hh in range(G_HEADS):
            ln = slice(hh * 128, (hh + 1) * 128)
            zz = z_ref[:, ln]
            oh, _, sig = _gated_norm(o_ref[:, ln], zz, g)
            cat_ref[:, ln] = (oh * g * (zz * sig)).astype(BF16)
        hnew_ref[...] = h_ref[...] + _dot(cat_ref[...], w_ref[...])

    full = pl.BlockSpec((tt, D), lambda i: (i, 0))
    return pl.pallas_call(
        body,
        grid=(T // tt,),
        in_specs=[full, full, full, pl.BlockSpec((1, 128), lambda i: (0, 0)), pl.BlockSpec((D, D), lambda i: (0, 0))],
        out_specs=[full, full],
        out_shape=[jax.ShapeDtypeStruct((T, D), F32), jax.ShapeDtypeStruct((T, D), BF16)],
        compiler_params=_cp("parallel"),
        name=name,
    )(h, o, z, norm_g, w_out)


def gdn_out_bwd(dres, o, z, norm_g, w_out, name):
    tt = 512

    def body(d_ref, o_ref, z_ref, g_ref, w_ref, do_ref, dz_ref, dg_ref, dcat):
        i = pl.program_id(0)
        g = g_ref[...]
        dcat[...] = _dot_nt(d_ref[...].astype(BF16), w_ref[...])
        dg = jnp.zeros((1, 128), F32)
        for hh in range(G_HEADS):
            ln = slice(hh * 128, (hh + 1) * 128)
            zz = z_ref[:, ln]
            oh, rstd, sig = _gated_norm(o_ref[:, ln], zz, g)
            dout = dcat[:, ln]
            dy = dout * (zz * sig)
            dz_ref[:, ln] = dout * (oh * g) * _dsilu(zz, sig)
            dg = dg + jnp.sum(dy * oh, axis=0, keepdims=True)
            doh = dy * g
            do_ref[:, ln] = rstd * (doh - oh * jnp.mean(doh * oh, axis=-1, keepdims=True))

        @pl.when(i == 0)
        def _():
            dg_ref[...] = dg

        @pl.when(i > 0)
        def _():
            dg_ref[...] += dg

    full = pl.BlockSpec((tt, D), lambda i: (i, 0))
    vec = pl.BlockSpec((1, 128), lambda i: (0, 0))
    return pl.pallas_call(
        body,
        grid=(T // tt,),
        in_specs=[full, full, full, vec, pl.BlockSpec((D, D), lambda i: (0, 0))],
        out_specs=[full, full, vec],
        out_shape=[jax.ShapeDtypeStruct((T, D), F32), jax.ShapeDtypeStruct((T, D), F32), jax.ShapeDtypeStruct((1, 128), F32)],
        scratch_shapes=[pltpu.VMEM((tt, D), F32)],
        compiler_params=_cp("arbitrary"),
        name=name,
    )(dres, o, z, norm_g, w_out)


GDN_SPLITS = [(0, 1024), (1024, 1024), (2048, 1024), (3072, 1024), (4096, 2 * G_HEADS)]


def gdn_fwd(h, g_row, w_in, conv_w, alog, dtb, norm_g, w_out, tag):
    hn, qkv_pre, z, ba = proj_fwd(h, g_row, w_in, [(0, G_QKV), (G_QKV, 1024), (4096, 2 * G_HEADS)], f"gdn_proj_{tag}")
    qkv = gdn_prep_fwd(qkv_pre, conv_w, f"gdn_prep_{tag}")
    u, w, qd, kd, at, el, tinv = gdn_local_fwd(qkv, ba, alog, dtb, f"gdn_local_{tag}")
    o, vn, st = gdn_rec_fwd(u, w, qd, kd, at, el, f"gdn_rec_{tag}")
    hnew, cat = gdn_out_fwd(h, o, z, norm_g, w_out, f"gdn_out_{tag}")
    return hnew, (h, hn, qkv_pre, z, ba, qkv, w, qd, kd, at, el, tinv, o, vn, st, cat)


def gdn_bwd(dres, saved, g_row, w_in, conv_w, alog, dtb, norm_g, w_out, tag):
    h, hn, qkv_pre, z, ba, qkv, w, qd, kd, at, el, tinv, o, vn, st, cat = saved
    d_o, d_z, d_ng = gdn_out_bwd(dres, o, z, norm_g, w_out, f"gdn_out_bwd_{tag}")
    d_wout = mm_tn_full(cat, dres, BF16, f"gdn_wout_grad_{tag}")
    du, dw, dqd, dkd, dat, dl = gdn_rec_bwd(d_o, w, qd, kd, at, el, vn, st, f"gdn_rec_bwd_{tag}")
    dq, dk, dv, dba, dal, ddt = gdn_local_bwd(qkv, ba, alog, dtb, tinv, du, dw, dqd, dkd, dat, dl, f"gdn_local_bwd_{tag}")
    dpre, dcw = [], []
    for part, d in enumerate((dq, dk, dv)):
        dx, dwc = gdn_prep_bwd(d, qkv_pre, conv_w, part, part < 2, f"gdn_prep_bwd_{tag}_{part}")
        dpre.append(dx)
        dcw.append(dwc)
    parts = dpre + [d_z, dba]
    dres_new, d_norm, dproj = proj_bwd_data(h, g_row, w_in, parts, GDN_SPLITS, dres, f"gdn_proj_bwd_{tag}")
    d_win = cols_to_slabs(mm_tn_full(hn, dproj, F32, f"gdn_win_grad_{tag}"), mm_tn(hn, dba, f"gdn_win_grad_ba_{tag}"),
                          f"gdn_win_slabs_{tag}")
    return dres_new, dict(norm=d_norm, w_in=d_win, conv_w=jnp.concatenate(dcw, axis=1), A_log=dal, dt_bias=ddt,
                          norm_g=d_ng, w_out=d_wout)


MESH = pl.DeviceIdType.MESH
ANY = pl.BlockSpec(memory_space=pl.ANY)


def _coords():
    return lax.axis_index("x"), lax.axis_index("y"), lax.axis_index("c")


def _slot(p):
    return 4 * p[0] + 2 * p[1] + p[2]


def all_gather(shards, name):
    k_n = len(shards)

    def body(*refs):
        srcs, dsts = refs[:k_n], refs[k_n:2 * k_n]
        send_sems, recv_sems, local_sems = refs[2 * k_n:]
        x, y, c = _coords()
        me, sibling = (x, y, c), (x, y, 1 - c)
        chips = [(1 - x, y), (x, 1 - y), (1 - x, 1 - y)]

        def copy(k, s, block, to, from_src=False):
            rows = dsts[k].at[_slot(block)]
            return pltpu.make_async_remote_copy(
                src_ref=srcs[k] if from_src else rows, dst_ref=rows,
                send_sem=send_sems.at[k, s], recv_sem=recv_sems.at[k, s], device_id=to, device_id_type=MESH)

        local = [pltpu.make_async_copy(srcs[k], dsts[k].at[_slot(me)], local_sems.at[k]) for k in range(k_n)]
        for cp in local:
            cp.start()
        first = []
        for k in range(k_n):
            first.append(copy(k, 0, me, sibling, True))
            first += [copy(k, 1 + j, me, (*chip, c), True) for j, chip in enumerate(chips)]
        for cp in first:
            cp.start()
        passed = []
        for j, chip in enumerate(chips):
            for k in range(k_n):
                copy(k, 1 + j, (*chip, c), me).wait_recv()
                fw = copy(k, 4 + j, (*chip, c), sibling)
                fw.start()
                passed.append(fw)
        for k in range(k_n):
            copy(k, 0, sibling, me).wait_recv()
            for j, chip in enumerate(chips):
                copy(k, 4 + j, (*chip, 1 - c), me).wait_recv()
        for cp in first + passed:
            cp.wait_send()
        for cp in local:
            cp.wait()

    return pl.pallas_call(
        body,
        in_specs=[ANY] * k_n,
        out_specs=[ANY] * k_n,
        out_shape=[jax.ShapeDtypeStruct((N_DEV,) + s.shape, s.dtype) for s in shards],
        scratch_shapes=[pltpu.SemaphoreType.DMA((k_n, 7)), pltpu.SemaphoreType.DMA((k_n, 7)),
                        pltpu.SemaphoreType.DMA((k_n,))],
        name=name,
    )(*shards)


HBM = pl.BlockSpec(memory_space=pltpu.HBM)
SEM = pl.BlockSpec(memory_space=pltpu.SEMAPHORE)
EFFECT = pltpu.SideEffectType.DATAFLOW_SIDE_EFFECTING


def _hbm(a):
    return pltpu.with_memory_space_constraint(a, pltpu.HBM)


def _peer_list(x, y, c):
    peers = []
    for j in range(1, N_DEV):
        jx, jy, jc = (j >> 2) & 1, (j >> 1) & 1, j & 1
        peers.append((x if jx == 0 else 1 - x, y if jy == 0 else 1 - y, c if jc == 0 else 1 - c))
    return peers


def _push_views(kind, layer, src_ref, land_ref, me, peer_slot):
    if kind == "gather":
        return src_ref, land_ref.at[me], land_ref.at[peer_slot]
    if layer is None:
        return src_ref.at[peer_slot], land_ref.at[me], land_ref.at[peer_slot]
    return src_ref.at[peer_slot], land_ref.at[me, layer], land_ref.at[peer_slot, layer]


def _push_copies(groups, srcs, lands, sems):
    x, y, c = _coords()
    me = _slot((x, y, c))
    peers = _peer_list(x, y, c)
    t = 0
    for gi, group in enumerate(groups):
        for ti, (kind, layer, _, li) in enumerate(group):
            for j, peer in enumerate(peers):
                out, there, here = _push_views(kind, layer, srcs[t], lands[li], me, _slot(peer))
                k = ti * (N_DEV - 1) + j
                yield out, there, here, sems[2 * gi].at[k], sems[2 * gi + 1].at[k], peer
            t += 1


def push_start(groups, lands, name, carry=()):
    flat = [it for g in groups for it in g]
    n, n_l, n_g, n_c = len(flat), len(lands), len(groups), len(carry)
    n_in = n + n_l + n_c

    def body(*refs):
        srcs, land_refs, sems = refs[:n], refs[n:n + n_l], refs[n_in:n_in + 2 * n_g]
        for out, there, _, s_sem, r_sem, peer in _push_copies(groups, srcs, land_refs, sems):
            pltpu.make_async_remote_copy(src_ref=out, dst_ref=there, send_sem=s_sem, recv_sem=r_sem,
                                         device_id=peer, device_id_type=MESH).start()

    arrays = [it[2] for it in flat] + list(lands) + list(carry)
    sem_shapes = []
    for g in groups:
        sem_shapes += [pltpu.SemaphoreType.DMA((len(g) * (N_DEV - 1),))] * 2
    outs = pl.pallas_call(
        body,
        name=name,
        in_specs=[HBM] * n_in,
        out_specs=[SEM] * (2 * n_g) + [HBM] * n_in,
        out_shape=sem_shapes + [pltpu.HBM(a.shape, a.dtype) for a in arrays],
        input_output_aliases={i: 2 * n_g + i for i in range(n_in)},
        compiler_params=pltpu.CompilerParams(has_side_effects=EFFECT),
    )(*[_hbm(a) for a in arrays])
    sems, thru = list(outs[:2 * n_g]), list(outs[2 * n_g:])
    return sems, thru[:n], thru[n:n + n_l], thru[n + n_l:]


def push_wait(groups, lands, sems, after, name):
    flat = [it for g in groups for it in g]
    n, n_l, n_g = len(flat), len(lands), len(groups)

    def body(*refs):
        srcs, land_refs, sem_refs = refs[:n], refs[n:n + n_l], refs[n + n_l:n + n_l + 2 * n_g]
        for out, _, here, s_sem, r_sem, peer in _push_copies(groups, srcs, land_refs, sem_refs):
            cp = pltpu.make_async_remote_copy(src_ref=out, dst_ref=here, send_sem=s_sem, recv_sem=r_sem,
                                              device_id=peer, device_id_type=MESH)
            cp.wait_send()
            cp.wait_recv()

    arrays = [it[2] for it in flat] + list(lands)
    outs = pl.pallas_call(
        body,
        name=name,
        in_specs=[HBM] * (n + n_l) + [SEM] * (2 * n_g) + [ANY],
        out_specs=[HBM] * (n + n_l),
        out_shape=[pltpu.HBM(a.shape, a.dtype) for a in arrays],
        input_output_aliases={i: i for i in range(n + n_l)},
        compiler_params=pltpu.CompilerParams(has_side_effects=EFFECT),
    )(*arrays, *sems, after)
    return list(outs[:n]), list(outs[n:])


def sum_slabs(parts, name):
    n, rows, cols = parts.shape

    def body(p_ref, o_ref):
        g = p_ref[0]
        for s in range(1, n):
            g = g + p_ref[s]
        o_ref[...] = g

    return pl.pallas_call(body, out_shape=jax.ShapeDtypeStruct((rows, cols), F32), name=name)(parts)


def _row_tile(rows, cols):
    if rows * cols * 4 <= (1 << 20) or rows % 8:
        return rows
    tr = rows
    while tr % 2 == 0 and (tr // 2) % 8 == 0 and tr * cols * 4 > (1 << 20):
        tr //= 2
    return tr


def adamw(parts, w, m, v, name):
    p_n = parts.shape[0]
    rows, cols = w.shape
    tr = _row_tile(rows, cols)

    def body(p_ref, w_ref, m_ref, v_ref, g_ref, d_ref, nm_ref, nv_ref):
        g = p_ref[0].astype(F32)
        for s in range(1, p_n):
            g = g + p_ref[s].astype(F32)
        g_ref[...] = g
        d_ref[...], nm_ref[...], nv_ref[...] = _adam_update(g, w_ref[...], m_ref[...], v_ref[...])

    blk = pl.BlockSpec((tr, cols), lambda i: (i, 0))
    return pl.pallas_call(
        body,
        grid=(rows // tr,),
        in_specs=[pl.BlockSpec((p_n, tr, cols), lambda i: (0, i, 0)), blk, blk, blk],
        out_specs=[blk] * 4,
        out_shape=[jax.ShapeDtypeStruct((rows, cols), F32)] * 4,
        compiler_params=_cp("parallel"),
        name=name,
    )(parts, w, m, v)


def _adam_update(g, w, m, v):
    m_new = ADAM_B1 * m + (1.0 - ADAM_B1) * g
    v_new = ADAM_B2 * v + (1.0 - ADAM_B2) * (g * g)
    m_hat = m_new / (1.0 - ADAM_B1 ** ADAM_STEP)
    v_hat = v_new / (1.0 - ADAM_B2 ** ADAM_STEP)
    return -ADAM_LR * (m_hat / (jnp.sqrt(v_hat) + ADAM_EPS) + ADAM_WD * w), m_new, v_new


def _adamw_nd(parts, w, m, v, name):
    p_n = parts.shape[0]
    n_l, rows, cols = w.shape
    tr = _row_tile(rows, cols)

    def body(p_ref, w_ref, m_ref, v_ref, g_ref, d_ref, nm_ref, nv_ref):
        g = p_ref[0].astype(F32)
        for s in range(1, p_n):
            g = g + p_ref[s].astype(F32)
        g_ref[...] = g
        d_ref[...], nm_ref[...], nv_ref[...] = _adam_update(g, w_ref[...], m_ref[...], v_ref[...])

    blk = pl.BlockSpec((None, tr, cols), lambda l, i: (l, i, 0))
    return pl.pallas_call(
        body,
        grid=(n_l, rows // tr),
        in_specs=[pl.BlockSpec((p_n, None, tr, cols), lambda l, i: (0, l, i, 0)), blk, blk, blk],
        out_specs=[blk] * 4,
        out_shape=[jax.ShapeDtypeStruct(w.shape, F32)] * 4,
        compiler_params=_cp("parallel", "parallel"),
        name=name,
    )(parts, w, m, v)


def slabs_to_cols(slabs, name):
    n, r, w = slabs.shape
    tr = 256 if r % 256 == 0 else r

    def body(s_ref, o_ref):
        for s in range(n):
            o_ref[:, w * s:w * (s + 1)] = s_ref[s]

    return pl.pallas_call(
        body,
        grid=(r // tr,),
        in_specs=[pl.BlockSpec((n, tr, w), lambda i: (0, i, 0))],
        out_specs=pl.BlockSpec((tr, n * w), lambda i: (i, 0)),
        out_shape=jax.ShapeDtypeStruct((r, n * w), slabs.dtype),
        compiler_params=_cp("parallel"),
        name=name,
    )(slabs)


FFN_IN = ("ffn1_w_in", "ffn2_w_in")
REPL = ["ffn1_norm", "mix_norm", "ffn2_norm", "hyb_dw_b", "hyb_ln_g", "hyb_ln_b", "gdn_A_log", "gdn_dt_bias",
        "gdn_norm_g", "final_norm"]
WEIGHTS = ["ffn1_norm", "ffn1_w_in", "ffn1_w_out", "mix_norm", "ffn2_norm", "ffn2_w_in", "ffn2_w_out", "hyb_w_in",
           "hyb_dw_w", "hyb_dw_b", "hyb_ln_g", "hyb_ln_b", "hyb_w_out", "gdn_w_in", "gdn_conv_w", "gdn_A_log",
           "gdn_dt_bias", "gdn_norm_g", "gdn_w_out", "final_norm"]


def _pack(arrs, rows):
    flat = jnp.concatenate([a.reshape(-1) for a in arrs])
    return jnp.pad(flat, (0, rows * 128 - flat.shape[0])).reshape(rows, 128)


def kernel(x, positions, ffn1_norm, ffn1_w_in, ffn1_w_out, mix_norm, ffn2_norm, ffn2_w_in, ffn2_w_out, hyb_w_in, hyb_dw_w, hyb_dw_b, hyb_ln_g, hyb_ln_b, hyb_w_out, gdn_w_in, gdn_conv_w, gdn_A_log, gdn_dt_bias, gdn_norm_g, gdn_w_out, final_norm, loss_target, m_ffn1_norm, m_ffn1_w_in, m_ffn1_w_out, m_mix_norm, m_ffn2_norm, m_ffn2_w_in, m_ffn2_w_out, m_hyb_w_in, m_hyb_dw_w, m_hyb_dw_b, m_hyb_ln_g, m_hyb_ln_b, m_hyb_w_out, m_gdn_w_in, m_gdn_conv_w, m_gdn_A_log, m_gdn_dt_bias, m_gdn_norm_g, m_gdn_w_out, m_final_norm, v_ffn1_norm, v_ffn1_w_in, v_ffn1_w_out, v_mix_norm, v_ffn2_norm, v_ffn2_w_in, v_ffn2_w_out, v_hyb_w_in, v_hyb_dw_w, v_hyb_dw_b, v_hyb_ln_g, v_hyb_ln_b, v_hyb_w_out, v_gdn_w_in, v_gdn_conv_w, v_gdn_A_log, v_gdn_dt_bias, v_gdn_norm_g, v_gdn_w_out, v_final_norm):
    w = dict(ffn1_norm=ffn1_norm, ffn1_w_in=ffn1_w_in, ffn1_w_out=ffn1_w_out, mix_norm=mix_norm, ffn2_norm=ffn2_norm,
             ffn2_w_in=ffn2_w_in, ffn2_w_out=ffn2_w_out, hyb_w_in=hyb_w_in, hyb_dw_w=hyb_dw_w, hyb_dw_b=hyb_dw_b,
             hyb_ln_g=hyb_ln_g, hyb_ln_b=hyb_ln_b, hyb_w_out=hyb_w_out, gdn_w_in=gdn_w_in, gdn_conv_w=gdn_conv_w,
             gdn_A_log=gdn_A_log, gdn_dt_bias=gdn_dt_bias, gdn_norm_g=gdn_norm_g, gdn_w_out=gdn_w_out,
             final_norm=final_norm)
    mom = dict(ffn1_norm=m_ffn1_norm, ffn1_w_in=m_ffn1_w_in, ffn1_w_out=m_ffn1_w_out, mix_norm=m_mix_norm,
               ffn2_norm=m_ffn2_norm, ffn2_w_in=m_ffn2_w_in, ffn2_w_out=m_ffn2_w_out, hyb_w_in=m_hyb_w_in,
               hyb_dw_w=m_hyb_dw_w, hyb_dw_b=m_hyb_dw_b, hyb_ln_g=m_hyb_ln_g, hyb_ln_b=m_hyb_ln_b,
               hyb_w_out=m_hyb_w_out, gdn_w_in=m_gdn_w_in, gdn_conv_w=m_gdn_conv_w, gdn_A_log=m_gdn_A_log,
               gdn_dt_bias=m_gdn_dt_bias, gdn_norm_g=m_gdn_norm_g, gdn_w_out=m_gdn_w_out, final_norm=m_final_norm)
    var = dict(ffn1_norm=v_ffn1_norm, ffn1_w_in=v_ffn1_w_in, ffn1_w_out=v_ffn1_w_out, mix_norm=v_mix_norm,
               ffn2_norm=v_ffn2_norm, ffn2_w_in=v_ffn2_w_in, ffn2_w_out=v_ffn2_w_out, hyb_w_in=v_hyb_w_in,
               hyb_dw_w=v_hyb_dw_w, hyb_dw_b=v_hyb_dw_b, hyb_ln_g=v_hyb_ln_g, hyb_ln_b=v_hyb_ln_b,
               hyb_w_out=v_hyb_w_out, gdn_w_in=v_gdn_w_in, gdn_conv_w=v_gdn_conv_w, gdn_A_log=v_gdn_A_log,
               gdn_dt_bias=v_gdn_dt_bias, gdn_norm_g=v_gdn_norm_g, gdn_w_out=v_gdn_w_out, final_norm=v_final_norm)
    xi, yi, ci = _coords()
    me = 4 * xi + 2 * yi + ci
    for group in (w, mom, var):
        for n in FFN_IN:
            group[n] = jnp.swapaxes(group[n], 1, 2)

    big = ["ffn1_w_in", "ffn1_w_out", "ffn2_w_in", "ffn2_w_out", "hyb_w_in", "hyb_w_out", "gdn_w_in", "gdn_w_out"]
    ag_groups, ag_lands = [], []

    def add_group(shards):
        group = []
        for s in shards:
            land = lax.dynamic_update_slice(lax.empty((N_DEV,) + s.shape, s.dtype), s[None], (me,) + (0,) * s.ndim)
            group.append(("gather", None, s, len(ag_lands)))
            ag_lands.append(land)
        ag_groups.append(group)

    first = all_gather([w["ffn1_w_in"][0].astype(BF16), ffn1_w_out[0].astype(BF16)], "weights_gather_first")
    for l in range(DEPTH):
        i = l // 2
        if l == 0:
            ag_groups.append([])
        else:
            add_group([w["ffn1_w_in"][l].astype(BF16), ffn1_w_out[l].astype(BF16)])
        if l % 2 == 0:
            add_group([hyb_w_in[i].astype(BF16), hyb_w_out[i].astype(BF16), hyb_dw_w[i]])
        else:
            add_group([gdn_w_in[i].astype(BF16), gdn_w_out[i].astype(BF16), gdn_conv_w[i]])
        add_group([w["ffn2_w_in"][l].astype(BF16), ffn2_w_out[l].astype(BF16)])
    ag_sems, ag_srcs, ag_lands, first = push_start(ag_groups[1:], ag_lands, "weights_gather_start", carry=first)
    ag_sems = [None, None] + ag_sems

    def fetch(gi, after):
        if gi == 0:
            return first
        group = ag_groups[gi]
        base = sum(len(g) for g in ag_groups[:gi])
        items = [(kind, layer, ag_srcs[base + t], t) for t, (kind, layer, _, _) in enumerate(group)]
        lands = [ag_lands[li] for _, _, _, li in group]
        return push_wait([items], lands, ag_sems[2 * gi:2 * gi + 2], after, f"weights_gather_wait_{gi}")[1]

    row = lambda a: a.reshape(1, -1)

    rope = make_rope(positions)
    h = x[0]
    saved = []
    for l in range(DEPTH):
        i = l // 2
        rec = {"h1": h}
        wi, wo = fetch(3 * l, h)
        rec["w1"] = (wi.reshape(2, FFN_TILES, FFN_SHARD, D), wo)
        h, rec["hn1"], rec["a1"], rec["b1"] = ffn_fwd(h, row(ffn1_norm[l]), *rec["w1"], l, "1")
        mi, mo, mc = fetch(3 * l + 1, h)
        if l % 2 == 0:
            rec["wm"] = (slabs_to_cols(mi, f"hyb_w_in_cols_{i}"),
                         jnp.pad(slabs_to_cols(mc, f"hyb_dw_w_cols_{i}"), ((0, 1), (0, 0))), mo.reshape(D, D))
            w_in_f, dw_f, w_out_f = rec["wm"]
            h, rec["mix"] = hybrid_fwd(h, row(mix_norm[l]), w_in_f, dw_f, row(hyb_dw_b[i]), row(hyb_ln_g[i]),
                                       row(hyb_ln_b[i]), w_out_f, rope, str(i))
        else:
            rec["wm"] = (slabs_to_cols(mi, f"gdn_w_in_cols_{i}"), slabs_to_cols(mc, f"gdn_conv_w_cols_{i}"),
                         mo.reshape(D, D))
            w_in_f, cw_f, w_out_f = rec["wm"]
            h, rec["mix"] = gdn_fwd(h, row(mix_norm[l]), w_in_f, cw_f, row(gdn_A_log[i]), row(gdn_dt_bias[i]),
                                    row(gdn_norm_g[i]), w_out_f, str(i))
        rec["h2"] = h
        wi, wo = fetch(3 * l + 2, h)
        rec["w2"] = (wi.reshape(2, FFN_TILES, FFN_SHARD, D), wo)
        h, rec["hn2"], rec["a2"], rec["b2"] = ffn_fwd(h, row(ffn2_norm[l]), *rec["w2"], l, "2")
        saved.append(rec)
    dres, d_final, loss_acc = final_loss(h, row(final_norm), loss_target[0])

    ge_land = {n: lax.empty((N_DEV,) + w[n].shape, BF16) for n in big}
    ge_pending = []

    def send(named, layer, tag, carry):
        lands = [ge_land[n] for n, _ in named]
        group = [("scatter", layer, s, t) for t, (_, s) in enumerate(named)]
        sems, srcs, lands_out, carried = push_start([group], lands, f"grad_send_{tag}", carry=[carry])
        for (n, _), land in zip(named, lands_out):
            ge_land[n] = land
        ge_pending.append(([(n, layer, s) for (n, _), s in zip(named, srcs)], sems))
        return carried[0]

    gsmall = {n: [None] * (DEPTH if n in ("ffn1_norm", "mix_norm", "ffn2_norm") else 2) for n in REPL[:-1]}
    gsmall["hyb_dw_w"] = [None, None]
    gsmall["gdn_conv_w"] = [None, None]
    for l in reversed(range(DEPTH)):
        i = l // 2
        rec = saved[l]
        dhn, dwin, dwout = ffn_bwd(rec["hn2"], rec["a2"], rec["b2"], dres, *rec["w2"], l, "2")
        dhn = send([("ffn2_w_in", dwin.reshape(N_DEV, FFN_SHARD, D)),
                    ("ffn2_w_out", dwout.reshape(N_DEV, FFN_SHARD // 2, D))], l, f"ffn2_{l}", dhn)
        dres, dg = norm_bwd(rec["h2"], row(ffn2_norm[l]), dhn, dres, f"ffn2_norm_bwd_{l}")
        gsmall["ffn2_norm"][l] = dg
        if l % 2 == 0:
            w_in_f, dw_f, w_out_f = rec["wm"]
            dres, gr = hybrid_bwd(dres, rec["mix"], row(mix_norm[l]), w_in_f, dw_f, row(hyb_ln_g[i]),
                                  row(hyb_ln_b[i]), w_out_f, rope, str(i))
            dres = send([("hyb_w_in", gr["w_in"]), ("hyb_w_out", gr["w_out"].reshape(N_DEV, D // N_DEV, D))],
                        i, f"hyb_{i}", dres)
            for n in ("dw_w", "dw_b", "ln_g", "ln_b"):
                gsmall["hyb_" + n][i] = gr[n]
        else:
            w_in_f, cw_f, w_out_f = rec["wm"]
            dres, gr = gdn_bwd(dres, rec["mix"], row(mix_norm[l]), w_in_f, cw_f, row(gdn_A_log[i]),
                               row(gdn_dt_bias[i]), row(gdn_norm_g[i]), w_out_f, str(i))
            dres = send([("gdn_w_in", gr["w_in"]), ("gdn_w_out", gr["w_out"].reshape(N_DEV, D // N_DEV, D))],
                        i, f"gdn_{i}", dres)
            for n in ("conv_w", "A_log", "dt_bias", "norm_g"):
                gsmall["gdn_" + n][i] = gr[n]
        gsmall["mix_norm"][l] = gr["norm"]
        dhn, dwin, dwout = ffn_bwd(rec["hn1"], rec["a1"], rec["b1"], dres, *rec["w1"], l, "1")
        dhn = send([("ffn1_w_in", dwin.reshape(N_DEV, FFN_SHARD, D)),
                    ("ffn1_w_out", dwout.reshape(N_DEV, FFN_SHARD // 2, D))], l, f"ffn1_{l}", dhn)
        dres, dg = norm_bwd(rec["h1"], row(ffn1_norm[l]), dhn, dres, f"ffn1_norm_bwd_{l}")
        gsmall["ffn1_norm"][l] = dg
    grad_x = dres[None]

    n_repl_rows = 136
    small_rows = 576
    repl_flat = jnp.concatenate([jnp.concatenate([a.reshape(-1) for a in gsmall[n]]) for n in REPL[:-1]]
                                + [d_final.reshape(-1), loss_acc[0, 0:1]])
    loss_at = repl_flat.shape[0] - 1
    repl_pack = jnp.pad(repl_flat, (0, n_repl_rows * 128 - repl_flat.shape[0]))
    small_pack = jnp.concatenate([repl_pack] + [a.reshape(-1) for a in gsmall["hyb_dw_w"]]
                                 + [a.reshape(-1) for a in gsmall["gdn_conv_w"]]).reshape(small_rows, 128)

    own = {n: {} for n in big}

    def wait_for(pending, names, after, name):
        groups = [[("scatter", layer, s, names.index(n)) for n, layer, s in named] for named, _ in pending]
        sems = [s for _, pair in pending for s in pair]
        srcs_out, lands_out = push_wait(groups, [ge_land[n] for n in names], sems, after, name)
        flat_named = [it for named, _ in pending for it in named]
        for (n, layer, _), s in zip(flat_named, srcs_out):
            own[n][layer] = lax.dynamic_index_in_dim(s, me, 0, keepdims=False)
        for n, land in zip(names, lands_out):
            ge_land[n] = land

    def with_own(n, land):
        mine = jnp.stack([own[n][k] for k in range(len(own[n]))])
        return lax.dynamic_update_slice(land, mine[None], (me,) + (0,) * mine.ndim)

    out = {}
    last = ["ffn1_w_in", "ffn1_w_out"]
    wait_for(ge_pending[:-1], big, dres, "grad_wait_a")
    for n in big:
        if n not in last:
            out[n] = _adamw_nd(with_own(n, ge_land[n]), w[n], mom[n], var[n], f"adamw_{n}")
    pin = sum(out[n][1].reshape(-1)[0] for n in big if n not in last) * 0.0
    small_all, = all_gather([small_pack + pin], "small_grads_all_gather")
    g_small = sum_slabs(small_all, "small_grads_sum")
    loss = g_small.reshape(-1)[loss_at]
    wait_for(ge_pending[-1:], last, g_small, "grad_wait_b")
    for n in last:
        out[n] = _adamw_nd(with_own(n, ge_land[n]), w[n], mom[n], var[n], f"adamw_{n}")

    pk = lambda d: _pack([d[n] for n in REPL], n_repl_rows)
    res = adamw(g_small[:n_repl_rows][None], pk(w), pk(mom), pk(var), "adamw_replicated")
    off = 0
    for n in REPL:
        sz = w[n].size
        out[n] = [r.reshape(-1)[off:off + sz].reshape(w[n].shape) for r in res]
        off += sz
    g_dw = g_small[n_repl_rows:n_repl_rows + 248].reshape(2, CONV_K, CONV_C)
    g_dw = lax.dynamic_slice_in_dim(g_dw, me * (CONV_C // N_DEV), CONV_C // N_DEV, axis=2)
    out["hyb_dw_w"] = _adamw_nd(g_dw[None], w["hyb_dw_w"], mom["hyb_dw_w"], var["hyb_dw_w"], "adamw_hyb_dw_w")
    g_cw = g_small[n_repl_rows + 248:].reshape(2, G_CONV, G_QKV)
    g_cw = lax.dynamic_slice_in_dim(g_cw, me * (G_QKV // N_DEV), G_QKV // N_DEV, axis=2)
    out["gdn_conv_w"] = _adamw_nd(g_cw[None], w["gdn_conv_w"], mom["gdn_conv_w"], var["gdn_conv_w"], "adamw_gdn_conv_w")

    for n in FFN_IN:
        out[n] = [jnp.swapaxes(o, 1, 2) for o in out[n]]
    return (loss, grad_x, *[out[n][0] for n in WEIGHTS], *[out[n][1] for n in WEIGHTS],
            *[out[n][2] for n in WEIGHTS], *[out[n][3] for n in WEIGHTS])
```

```python
import functools

import jax
import jax.numpy as jnp
from jax import lax
from jax.experimental import pallas as pl
from jax.experimental.pallas import tpu as pltpu

F32 = jnp.float32
BF16 = jnp.bfloat16

N_DEV = 8
T = 4096
D = 1024
DEPTH = 4
FFN = 2816
FFN_SHARD = 2 * FFN // N_DEV
FFN_TILES = FFN // FFN_SHARD
EPS = 1e-6

A_HEADS = 8
A_HD = 64
A_W = 512
CONV_C = 512
CONV_K = 31
HYB_IN = 2560
ROPE_THETA = 500000.0
ROT = 16
DILATIONS = (1, 4, 16)
BLK = 128
KPAD = 2048

G_HEADS = 8
G_DK = 128
G_QKV = 3072
G_IN = 4112
G_CONV = 4
CH = 64
GRP = 512
CPG = GRP // CH
N_GRP = T // GRP

ADAM_LR = 0.001
ADAM_B1 = 0.9
ADAM_B2 = 0.999
ADAM_EPS = 1e-08
ADAM_WD = 0.01
ADAM_STEP = 10

VMEM_LIMIT = 56 * 1024 * 1024


def _cp(*sem):
    return pltpu.CompilerParams(dimension_semantics=sem, vmem_limit_bytes=VMEM_LIMIT)


def _dot(a, b):
    return jnp.dot(a, b, preferred_element_type=F32)


def _dot_nt(a, b):
    return lax.dot_general(a, b, (((1,), (1,)), ((), ())), preferred_element_type=F32)


def _dot_tn(a, b):
    return lax.dot_general(a, b, (((0,), (0,)), ((), ())), preferred_element_type=F32)


def _sigmoid(x):
    return 1.0 / (1.0 + jnp.exp(-x))


def _dsilu(x, sig):
    return sig * (1.0 + x * (1.0 - sig))


def _rms(x, g):
    rstd = lax.rsqrt(jnp.mean(x * x, axis=-1, keepdims=True) + EPS)
    return x * rstd * g


FFN_TT = 512


def ffn_fwd(h, g_row, w_in, w_out, layer, tag=""):
    def body(h_ref, g_ref, win_ref, wout_ref, hnew_ref, hn_ref, a_ref, b_ref):
        x = h_ref[...]
        hn = _rms(x, g_ref[...]).astype(BF16)
        hn_ref[...] = hn
        acc = None
        for j in range(FFN_TILES):
            a = _dot_nt(hn, win_ref[0, j])
            b = _dot_nt(hn, win_ref[1, j])
            act = a * _sigmoid(a) * b
            a_ref[j] = a.astype(BF16)
            b_ref[j] = b.astype(BF16)
            part = _dot(act.astype(BF16), wout_ref[2 * j:2 * j + 2].reshape(FFN_SHARD, D))
            acc = part if acc is None else acc + part
        hnew_ref[...] = x + 0.5 * acc

    tt = FFN_TT
    resident = pl.Buffered(1)
    return pl.pallas_call(
        body,
        grid=(T // tt,),
        in_specs=[
            pl.BlockSpec((tt, D), lambda i: (i, 0)),
            pl.BlockSpec((1, D), lambda i: (0, 0)),
            pl.BlockSpec((2, FFN_TILES, FFN_SHARD, D), lambda i: (0, 0, 0, 0), pipeline_mode=resident),
            pl.BlockSpec((N_DEV, FFN_SHARD // 2, D), lambda i: (0, 0, 0), pipeline_mode=resident),
        ],
        out_specs=[
            pl.BlockSpec((tt, D), lambda i: (i, 0)),
            pl.BlockSpec((tt, D), lambda i: (i, 0)),
            pl.BlockSpec((FFN_TILES, tt, FFN_SHARD), lambda i: (0, i, 0)),
            pl.BlockSpec((FFN_TILES, tt, FFN_SHARD), lambda i: (0, i, 0)),
        ],
        out_shape=[
            jax.ShapeDtypeStruct((T, D), F32),
            jax.ShapeDtypeStruct((T, D), BF16),
            jax.ShapeDtypeStruct((FFN_TILES, T, FFN_SHARD), BF16),
            jax.ShapeDtypeStruct((FFN_TILES, T, FFN_SHARD), BF16),
        ],
        compiler_params=_cp("parallel"),
        name=f"ffn{tag}_fwd_{layer}",
    )(h, g_row, w_in, w_out)


def ffn_bwd(hn, a, b, dres, w_in, w_out, layer, tag=""):
    tt = FFN_TT
    nt = T // tt

    def body(hn_ref, a_ref, b_ref, dres_ref, win_ref, wout_ref, dhn_ref, dwin_ref, dwout_ref, gin_ref, gout_ref,
             do_s, act_s, da_s, db_s):
        i = pl.program_id(1)
        wo = wout_ref[...].reshape(FFN_SHARD, D)
        half = tt // 2
        for r0 in (0, half):
            rows = slice(r0, r0 + half)
            do_h = (0.5 * dres_ref[rows, :]).astype(BF16)
            do_s[rows, :] = do_h
            dact = _dot_nt(do_h, wo)
            a = a_ref[rows, :].astype(F32)
            b = b_ref[rows, :].astype(F32)
            sig = _sigmoid(a)
            s = a * sig
            da_h = (dact * b * _dsilu(a, sig)).astype(BF16)
            db_h = (dact * s).astype(BF16)
            act_s[rows, :] = (s * b).astype(BF16)
            da_s[rows, :] = da_h
            db_s[rows, :] = db_h
            dhn_ref[rows, :] = (_dot(da_h, win_ref[0]) + _dot(db_h, win_ref[1])).astype(BF16)
        do, act, da, db = do_s[...], act_s[...], da_s[...], db_s[...]
        hn = hn_ref[...]
        gwo = _dot_tn(act, do)
        gwg = _dot_tn(da, hn)
        gwu = _dot_tn(db, hn)

        @pl.when(i == 0)
        def _():
            gout_ref[...] = gwo
            gin_ref[0] = gwg
            gin_ref[1] = gwu

        @pl.when(i > 0)
        def _():
            gout_ref[...] += gwo
            gin_ref[0] += gwg
            gin_ref[1] += gwu

        @pl.when(i == nt - 1)
        def _():
            dwin_ref[...] = gin_ref[...].astype(BF16)
            dwout_ref[...] = gout_ref[...].astype(BF16)

    return pl.pallas_call(
        body,
        grid=(FFN_TILES, nt),
        in_specs=[
            pl.BlockSpec((tt, D), lambda j, i: (i, 0)),
            pl.BlockSpec((None, tt, FFN_SHARD), lambda j, i: (j, i, 0)),
            pl.BlockSpec((None, tt, FFN_SHARD), lambda j, i: (j, i, 0)),
            pl.BlockSpec((tt, D), lambda j, i: (i, 0)),
            pl.BlockSpec((2, None, FFN_SHARD, D), lambda j, i: (0, j, 0, 0)),
            pl.BlockSpec((2, FFN_SHARD // 2, D), lambda j, i: (j, 0, 0)),
        ],
        out_specs=[
            pl.BlockSpec((None, tt, D), lambda j, i: (j, i, 0)),
            pl.BlockSpec((2, None, FFN_SHARD, D), lambda j, i: (0, j, 0, 0)),
            pl.BlockSpec((None, FFN_SHARD, D), lambda j, i: (j, 0, 0)),
        ],
        out_shape=[
            jax.ShapeDtypeStruct((FFN_TILES, T, D), BF16),
            jax.ShapeDtypeStruct((2, FFN_TILES, FFN_SHARD, D), BF16),
            jax.ShapeDtypeStruct((FFN_TILES, FFN_SHARD, D), BF16),
        ],
        scratch_shapes=[pltpu.VMEM((2, FFN_SHARD, D), F32), pltpu.VMEM((FFN_SHARD, D), F32),
                        pltpu.VMEM((tt, D), BF16),
                        pltpu.VMEM((tt, FFN_SHARD), BF16), pltpu.VMEM((tt, FFN_SHARD), BF16),
                        pltpu.VMEM((tt, FFN_SHARD), BF16)],
        compiler_params=_cp("parallel", "arbitrary"),
        name=f"ffn{tag}_bwd_{layer}",
    )(hn, a, b, dres, w_in, w_out)


def _rms_bwd(x, g, dy):
    rstd = lax.rsqrt(jnp.mean(x * x, axis=-1, keepdims=True) + EPS)
    xh = x * rstd
    u = dy * g
    dx = rstd * (u - xh * jnp.mean(u * xh, axis=-1, keepdims=True))
    return dx, jnp.sum(dy * xh, axis=0, keepdims=True)


def norm_bwd(x, g_row, dy_parts, dres, name):
    p = dy_parts.shape[0]
    tt = 512

    def body(x_ref, g_ref, dy_ref, dres_ref, out_ref, dg_ref):
        i = pl.program_id(0)
        dy = dy_ref[0].astype(F32)
        for q in range(1, p):
            dy = dy + dy_ref[q].astype(F32)
        dx, dg = _rms_bwd(x_ref[...], g_ref[...], dy)
        out_ref[...] = dres_ref[...] + dx

        @pl.when(i == 0)
        def _():
            dg_ref[...] = dg

        @pl.when(i > 0)
        def _():
            dg_ref[...] += dg

    return pl.pallas_call(
        body,
        grid=(T // tt,),
        in_specs=[
            pl.BlockSpec((tt, D), lambda i: (i, 0)),
            pl.BlockSpec((1, D), lambda i: (0, 0)),
            pl.BlockSpec((p, tt, D), lambda i: (0, i, 0)),
            pl.BlockSpec((tt, D), lambda i: (i, 0)),
        ],
        out_specs=[pl.BlockSpec((tt, D), lambda i: (i, 0)), pl.BlockSpec((1, D), lambda i: (0, 0))],
        out_shape=[jax.ShapeDtypeStruct((T, D), F32), jax.ShapeDtypeStruct((1, D), F32)],
        compiler_params=_cp("arbitrary"),
        name=name,
    )(x, g_row, dy_parts, dres)


def final_loss(h, g_row, target):
    tt = 512

    def body(h_ref, g_ref, t_ref, dres_ref, dg_ref, loss_ref):
        i = pl.program_id(0)
        x = h_ref[...]
        g = g_ref[...]
        err = _rms(x, g) - t_ref[...]
        part = 0.5 * jnp.sum(jnp.mean(err * err, axis=-1, keepdims=True), axis=0, keepdims=True)
        dx, dg = _rms_bwd(x, g, err * (1.0 / D))
        dres_ref[...] = dx
        part = jnp.broadcast_to(part, loss_ref.shape)

        @pl.when(i == 0)
        def _():
            dg_ref[...] = dg
            loss_ref[...] = part

        @pl.when(i > 0)
        def _():
            dg_ref[...] += dg
            loss_ref[...] += part

    return pl.pallas_call(
        body,
        grid=(T // tt,),
        in_specs=[
            pl.BlockSpec((tt, D), lambda i: (i, 0)),
            pl.BlockSpec((1, D), lambda i: (0, 0)),
            pl.BlockSpec((tt, D), lambda i: (i, 0)),
        ],
        out_specs=[
            pl.BlockSpec((tt, D), lambda i: (i, 0)),
            pl.BlockSpec((1, D), lambda i: (0, 0)),
            pl.BlockSpec((8, 128), lambda i: (0, 0)),
        ],
        out_shape=[
            jax.ShapeDtypeStruct((T, D), F32),
            jax.ShapeDtypeStruct((1, D), F32),
            jax.ShapeDtypeStruct((8, 128), F32),
        ],
        compiler_params=_cp("arbitrary"),
        name="final_loss",
    )(h, g_row, target)


PROJ_TT = 256


def rope_tables(pos_col, invf_row):
    tt = 512

    def body(p_ref, f_ref, c_ref, sm_ref, sp_ref):
        ang = p_ref[...].astype(F32) * f_ref[...]
        lane = lax.broadcasted_iota(jnp.int32, ang.shape, 1) % A_HD
        cs = jnp.cos(ang)
        sn = jnp.sin(ang)
        c_ref[...] = jnp.where(lane < ROT, cs, 1.0)
        sm_ref[...] = jnp.where(lane < ROT // 2, -sn, 0.0)
        sp_ref[...] = jnp.where((lane >= ROT // 2) & (lane < ROT), sn, 0.0)

    spec = pl.BlockSpec((tt, 128), lambda i: (i, 0))
    return pl.pallas_call(
        body,
        grid=(T // tt,),
        in_specs=[pl.BlockSpec((tt, 1), lambda i: (i, 0)), pl.BlockSpec((1, 128), lambda i: (0, 0))],
        out_specs=[spec, spec, spec],
        out_shape=[jax.ShapeDtypeStruct((T, 128), F32)] * 3,
        compiler_params=_cp("parallel"),
        name="rope_tables",
    )(pos_col, invf_row)


def make_rope(positions):
    inv_freq = jnp.power(jnp.float32(ROPE_THETA), -jnp.arange(0, ROT, 2, dtype=F32) / ROT)
    per_head = jnp.concatenate([inv_freq, inv_freq, jnp.zeros((A_HD - ROT,), F32)])
    invf_row = jnp.tile(per_head, 2)[None, :]
    return tuple(rope_tables(positions.reshape(T, 1), invf_row))


def _rope(x, c, sm, sp):
    return x * c + pltpu.roll(x, 128 - ROT // 2, 1) * sm + pltpu.roll(x, ROT // 2, 1) * sp


def _rope_t(dy, c, sm, sp):
    return dy * c + pltpu.roll(dy * sm, ROT // 2, 1) + pltpu.roll(dy * sp, 128 - ROT // 2, 1)


def proj_fwd(h, g_row, w, splits, name, rope=None):
    tt = PROJ_TT
    n = w.shape[1]
    n_rope = 0 if rope is None else 3

    def body(h_ref, g_ref, w_ref, *rest):
        tabs = rest[:n_rope]
        hn_ref = rest[n_rope]
        outs = rest[n_rope + 1:]
        hn = _rms(h_ref[...], g_ref[...]).astype(BF16)
        hn_ref[...] = hn
        for k, ((st, wd), o_ref) in enumerate(zip(splits, outs)):
            if rope is not None and k == 0:
                c, sm, sp = (t[...] for t in tabs)
                for gi in range(wd // 128):
                    r = _dot(hn, w_ref[:, st + 128 * gi:st + 128 * (gi + 1)])
                    if gi < 2 * A_W // 128:
                        r = _rope(r, c, sm, sp)
                    o_ref[:, 128 * gi:128 * (gi + 1)] = r
            else:
                o_ref[...] = _dot(hn, w_ref[:, st:st + wd])

    tab_specs = [pl.BlockSpec((tt, 128), lambda i: (i, 0))] * n_rope
    return pl.pallas_call(
        body,
        grid=(T // tt,),
        in_specs=[
            pl.BlockSpec((tt, D), lambda i: (i, 0)),
            pl.BlockSpec((1, D), lambda i: (0, 0)),
            pl.BlockSpec((D, n), lambda i: (0, 0)),
        ] + tab_specs,
        out_specs=[pl.BlockSpec((tt, D), lambda i: (i, 0))]
        + [pl.BlockSpec((tt, wd), lambda i: (i, 0)) for _, wd in splits],
        out_shape=[jax.ShapeDtypeStruct((T, D), BF16)]
        + [jax.ShapeDtypeStruct((T, wd), F32) for _, wd in splits],
        compiler_params=_cp("parallel"),
        name=name,
    )(h, g_row, w, *(rope or ()))


def proj_bwd_data(x, g_row, w, dparts, splits, dres, name, rope=None, n_rot=0):
    tt = PROJ_TT
    n = w.shape[1]
    n_rope = 0 if rope is None else 3
    k_parts = len(dparts)
    n_main = sum(wd for _, wd in splits if wd % 128 == 0)

    def body(x_ref, g_ref, w_ref, dres_ref, *rest):
        d_refs = rest[:k_parts]
        tabs = rest[k_parts:k_parts + n_rope]
        out_ref, dg_ref, dproj_ref = rest[k_parts + n_rope:k_parts + n_rope + 3]
        i = pl.program_id(0)
        dhn = jnp.zeros((tt, D), F32)
        for k, ((st, wd), d_ref) in enumerate(zip(splits, d_refs)):
            if k < n_rot:
                c, sm, sp = (t[...] for t in tabs)
                for gi in range(wd // 128):
                    cols = slice(st + 128 * gi, st + 128 * (gi + 1))
                    d = _rope_t(d_ref[:, 128 * gi:128 * (gi + 1)], c, sm, sp).astype(BF16)
                    dproj_ref[:, cols] = d
                    dhn = dhn + _dot_nt(d, w_ref[:, cols])
            else:
                d = d_ref[...].astype(BF16)
                if wd % 128 == 0:
                    dproj_ref[:, st:st + wd] = d
                dhn = dhn + _dot_nt(d, w_ref[:, st:st + wd])
        dx, dg = _rms_bwd(x_ref[...], g_ref[...], dhn)
        out_ref[...] = dres_ref[...] + dx

        @pl.when(i == 0)
        def _():
            dg_ref[...] = dg

        @pl.when(i > 0)
        def _():
            dg_ref[...] += dg

    tab_specs = [pl.BlockSpec((tt, 128), lambda i: (i, 0))] * n_rope
    out_specs = [pl.BlockSpec((tt, D), lambda i: (i, 0)), pl.BlockSpec((1, D), lambda i: (0, 0))]
    out_shape = [jax.ShapeDtypeStruct((T, D), F32), jax.ShapeDtypeStruct((1, D), F32)]
    out_specs.append(pl.BlockSpec((tt, n_main), lambda i: (i, 0)))
    out_shape.append(jax.ShapeDtypeStruct((T, n_main), BF16))
    return pl.pallas_call(
        body,
        grid=(T // tt,),
        in_specs=[
            pl.BlockSpec((tt, D), lambda i: (i, 0)),
            pl.BlockSpec((1, D), lambda i: (0, 0)),
            pl.BlockSpec((D, n), lambda i: (0, 0)),
            pl.BlockSpec((tt, D), lambda i: (i, 0)),
        ] + [pl.BlockSpec((tt, wd), lambda i: (i, 0)) for _, wd in splits] + tab_specs,
        out_specs=out_specs,
        out_shape=out_shape,
        compiler_params=_cp("arbitrary"),
        name=name,
    )(x, g_row, w, dres, *dparts, *(rope or ()))


def mm_tn_full(x, d, out_dtype, name):
    k = x.shape[1]
    n = d.shape[1]
    wn = 512

    def body(x_ref, d_ref, o_ref):
        o_ref[...] = _dot_tn(x_ref[...], d_ref[...].astype(BF16)).astype(out_dtype)

    return pl.pallas_call(
        body,
        grid=(n // wn,),
        in_specs=[pl.BlockSpec((T, k), lambda j: (0, 0), pipeline_mode=pl.Buffered(1)),
                  pl.BlockSpec((T, wn), lambda j: (0, j))],
        out_specs=pl.BlockSpec((k, wn), lambda j: (0, j)),
        out_shape=jax.ShapeDtypeStruct((k, n), out_dtype),
        compiler_params=_cp("parallel"),
        name=name,
    )(x, d)


def cols_to_slabs(main, tail, name):
    nm = main.shape[1]
    n = nm + (0 if tail is None else tail.shape[1])
    w = n // N_DEV
    tr = 256

    def body(*refs):
        m_ref, o_ref = refs[0], refs[-1]
        for s in range(N_DEV):
            a, b = w * s, w * (s + 1)
            if b <= nm:
                o_ref[s] = m_ref[:, a:b].astype(BF16)
            else:
                o_ref[s, :, 0:nm - a] = m_ref[:, a:nm].astype(BF16)
                o_ref[s, :, nm - a:w] = refs[1][:, 0:b - nm].astype(BF16)

    arrays = [main] + ([] if tail is None else [tail])
    return pl.pallas_call(
        body,
        grid=(D // tr,),
        in_specs=[pl.BlockSpec((tr, a.shape[1]), lambda i: (i, 0)) for a in arrays],
        out_specs=pl.BlockSpec((N_DEV, tr, w), lambda i: (0, i, 0)),
        out_shape=jax.ShapeDtypeStruct((N_DEV, D, w), BF16),
        compiler_params=_cp("parallel"),
        name=name,
    )(*arrays)


def mm_tn(x, d, name):
    k = x.shape[1]
    n = d.shape[1]
    wn = n if n <= 512 else 512
    tt = 512

    def body(x_ref, d_ref, o_ref):
        i = pl.program_id(1)
        r = _dot_tn(x_ref[...], d_ref[...].astype(BF16))

        @pl.when(i == 0)
        def _():
            o_ref[...] = r

        @pl.when(i > 0)
        def _():
            o_ref[...] += r

    return pl.pallas_call(
        body,
        grid=(n // wn, T // tt),
        in_specs=[pl.BlockSpec((tt, k), lambda j, i: (i, 0)), pl.BlockSpec((tt, wn), lambda j, i: (i, j))],
        out_specs=pl.BlockSpec((k, wn), lambda j, i: (0, j)),
        out_shape=jax.ShapeDtypeStruct((k, n), F32),
        compiler_params=_cp("parallel", "arbitrary"),
        name=name,
    )(x, d)


CONV_RC = 128
CONV_PAD = 32


def hyb_conv_fwd(u, dw_w, dw_b, name):
    def body(ua_ref, ug_ref, w_ref, b_ref, o_ref, xpad):
        xpad[0:CONV_PAD, :] = jnp.zeros((CONV_PAD, 128), F32)
        xpad[CONV_PAD:, :] = ua_ref[...] * _sigmoid(ug_ref[...])
        for r in range(T // CONV_RC):
            acc = jnp.broadcast_to(b_ref[...], (CONV_RC, 128))
            for j in range(CONV_K):
                acc = acc + w_ref[pl.ds(j, 1), :] * xpad[pl.ds(r * CONV_RC + CONV_PAD - (CONV_K - 1) + j, CONV_RC), :]
            o_ref[r * CONV_RC:(r + 1) * CONV_RC, :] = acc

    nb = CONV_C // 128
    return pl.pallas_call(
        body,
        grid=(nb,),
        in_specs=[
            pl.BlockSpec((T, 128), lambda c: (0, c)),
            pl.BlockSpec((T, 128), lambda c: (0, nb + c)),
            pl.BlockSpec((32, 128), lambda c: (0, c)),
            pl.BlockSpec((1, 128), lambda c: (0, c)),
        ],
        out_specs=pl.BlockSpec((T, 128), lambda c: (0, c)),
        out_shape=jax.ShapeDtypeStruct((T, CONV_C), F32),
        scratch_shapes=[pltpu.VMEM((T + CONV_PAD, 128), F32)],
        compiler_params=_cp("parallel"),
        name=name,
    )(u, u, dw_w, dw_b)


def hyb_conv_bwd(dc, u, dw_w, name):
    def body(dc_ref, ua_ref, ug_ref, w_ref, da_ref, dgate_ref, dw_ref, db_ref, xpad, dcpad, dwacc):
        ua = ua_ref[...]
        sig = _sigmoid(ug_ref[...])
        xpad[0:CONV_PAD, :] = jnp.zeros((CONV_PAD, 128), F32)
        xpad[CONV_PAD:, :] = ua * sig
        dcpad[0:T, :] = dc_ref[...]
        dcpad[T:, :] = jnp.zeros((CONV_PAD, 128), F32)
        dwacc[...] = jnp.zeros_like(dwacc)
        dbacc = jnp.zeros((8, 128), F32)
        for r in range(T // CONV_RC):
            r0 = r * CONV_RC
            dcr = dc_ref[r0:r0 + CONV_RC, :]
            dbacc = dbacc + dcr.reshape(CONV_RC // 8, 8, 128).sum(axis=0)
            dglu = jnp.zeros((CONV_RC, 128), F32)
            for j in range(CONV_K):
                dglu = dglu + w_ref[pl.ds(j, 1), :] * dcpad[pl.ds(r0 + (CONV_K - 1) - j, CONV_RC), :]
                prod = dcr * xpad[pl.ds(r0 + CONV_PAD - (CONV_K - 1) + j, CONV_RC), :]
                dwacc[8 * j:8 * j + 8, :] += prod.reshape(CONV_RC // 8, 8, 128).sum(axis=0)
            sg = sig[r0:r0 + CONV_RC, :]
            da_ref[r0:r0 + CONV_RC, :] = dglu * sg
            dgate_ref[r0:r0 + CONV_RC, :] = dglu * ua[r0:r0 + CONV_RC, :] * sg * (1.0 - sg)
        for j in range(CONV_K):
            dw_ref[pl.ds(j, 1), :] = jnp.sum(dwacc[8 * j:8 * j + 8, :], axis=0, keepdims=True)
        dw_ref[pl.ds(CONV_K, 1), :] = jnp.zeros((1, 128), F32)
        db_ref[...] = jnp.sum(dbacc, axis=0, keepdims=True)

    nb = CONV_C // 128
    col = pl.BlockSpec((T, 128), lambda c: (0, c))
    return pl.pallas_call(
        body,
        grid=(nb,),
        in_specs=[col, col, pl.BlockSpec((T, 128), lambda c: (0, nb + c)), pl.BlockSpec((32, 128), lambda c: (0, c))],
        out_specs=[col, col, pl.BlockSpec((32, 128), lambda c: (0, c)), pl.BlockSpec((1, 128), lambda c: (0, c))],
        out_shape=[
            jax.ShapeDtypeStruct((T, CONV_C), F32),
            jax.ShapeDtypeStruct((T, CONV_C), F32),
            jax.ShapeDtypeStruct((32, CONV_C), F32),
            jax.ShapeDtypeStruct((1, CONV_C), F32),
        ],
        scratch_shapes=[
            pltpu.VMEM((T + CONV_PAD, 128), F32),
            pltpu.VMEM((T + CONV_PAD, 128), F32),
            pltpu.VMEM((8 * 32, 128), F32),
        ],
        compiler_params=_cp("parallel"),
        name=name,
    )(dc, u, u, dw_w)


ATT_SCALE = A_HD ** -0.5
N_BLK = T // BLK


def _att_masks():
    i = lax.broadcasted_iota(jnp.int32, (BLK, 2 * BLK), 0)
    j = lax.broadcasted_iota(jnp.int32, (BLK, 2 * BLK), 1)
    band = (j >= i) & (j <= i + BLK)
    i1 = lax.broadcasted_iota(jnp.int32, (BLK, BLK), 0)
    j1 = lax.broadcasted_iota(jnp.int32, (BLK, BLK), 1)
    return band, j1 <= i1


def _att_rows(d, t, first):
    if first:
        base = t
        return pl.ds(base, BLK, stride=d), pl.ds(base, BLK, stride=d)
    c = t % d
    n = t // d + 1
    base = c + (BLK * d) * n
    return pl.ds(base, BLK, stride=d), pl.ds(base - BLK * d, 2 * BLK, stride=d)


def _stack_heads(x, head0):
    return jnp.concatenate([jnp.where(head0, x, 0.0), jnp.where(head0, 0.0, x)], axis=0)


def _loop_pairs(n, block, per=2):
    def several(i, carry):
        for k in range(per):
            block(per * i + k, carry)
        return carry

    if n >= per:
        lax.fori_loop(0, n // per, several, 0)
    for t in range(n - n % per, n):
        block(t, 0)


def attn_fwd(qkv, name):
    def body(q_ref, k_ref, v_ref, o_ref, lse_ref, og, lg):
        band, tri = _att_masks()
        band, tri = jnp.concatenate([band, band], axis=0), jnp.concatenate([tri, tri], axis=0)
        head0 = lax.broadcasted_iota(jnp.int32, (BLK, 128), 1) < A_HD
        for g, d in enumerate(DILATIONS):
            def block(t, carry, first, g=g, d=d):
                rq, rk = _att_rows(d, t, first)
                q2 = q_ref[rq, :]
                k2 = k_ref[rk, :].astype(BF16)
                v2 = v_ref[rk, :].astype(BF16)
                qs = _stack_heads(q2, head0).astype(BF16)
                s = _dot_nt(qs, k2) * ATT_SCALE
                s = jnp.where(tri if first else band, s, -jnp.inf)
                m = jnp.max(s, axis=-1, keepdims=True)
                p = jnp.exp(s - m)
                den = jnp.sum(p, axis=-1, keepdims=True)
                o = _dot(p.astype(BF16), v2) / den
                l = m + jnp.log(den)
                og[g, rq, :] = jnp.where(head0, o[0:BLK], o[BLK:2 * BLK])
                lg[g, rq, :] = jnp.where(head0, l[0:BLK], l[BLK:2 * BLK])
                return carry

            _loop_pairs(d, functools.partial(block, first=True), per=4)
            _loop_pairs(N_BLK - d, functools.partial(block, first=False), per=4)
        rc = 256
        for r in range(T // rc):
            rows = pl.ds(r * rc, rc)
            l0, l1, l2 = lg[0, rows, :], lg[1, rows, :], lg[2, rows, :]
            m = jnp.maximum(jnp.maximum(l0, l1), l2)
            e0, e1, e2 = jnp.exp(l0 - m), jnp.exp(l1 - m), jnp.exp(l2 - m)
            z = e0 + e1 + e2
            o_ref[rows, :] = (e0 / z) * og[0, rows, :] + (e1 / z) * og[1, rows, :] + (e2 / z) * og[2, rows, :]
            lse_ref[rows, :] = m + jnp.log(z)

    npair = A_HEADS // 2
    col = lambda off: pl.BlockSpec((T, 128), lambda p: (0, off + p))
    return pl.pallas_call(
        body,
        grid=(npair,),
        in_specs=[col(0), col(npair), col(2 * npair)],
        out_specs=[col(0), col(0)],
        out_shape=[jax.ShapeDtypeStruct((T, A_W), F32), jax.ShapeDtypeStruct((T, A_W), F32)],
        scratch_shapes=[pltpu.VMEM((3, T, 128), F32), pltpu.VMEM((3, T, 128), F32)],
        compiler_params=_cp("parallel"),
        name=name,
    )(qkv, qkv, qkv)


def attn_bwd(qkv, o, lse, do, name):
    def body(q_ref, k_ref, v_ref, o_ref, lse_ref, do_ref, dq_ref, dk_ref, dv_ref):
        band, tri = _att_masks()
        band, tri = jnp.concatenate([band, band], axis=0), jnp.concatenate([tri, tri], axis=0)
        head0 = lax.broadcasted_iota(jnp.int32, (BLK, 128), 1) < A_HD
        dq_ref[...] = jnp.zeros_like(dq_ref)
        dk_ref[...] = jnp.zeros_like(dk_ref)
        dv_ref[...] = jnp.zeros_like(dv_ref)
        for d in DILATIONS:
            def block(t, carry, first, d=d):
                rq, rk = _att_rows(d, t, first)
                k2 = k_ref[rk, :].astype(BF16)
                v2 = v_ref[rk, :].astype(BF16)
                do2 = do_ref[rq, :]
                l2 = lse_ref[rq, :]
                qs = _stack_heads(q_ref[rq, :], head0).astype(BF16)
                dos = _stack_heads(do2, head0).astype(BF16)
                l = jnp.concatenate([l2[:, 0:1], l2[:, A_HD:A_HD + 1]], axis=0)
                dd = jnp.sum(_stack_heads(do2 * o_ref[rq, :], head0), axis=-1, keepdims=True)
                s = _dot_nt(qs, k2) * ATT_SCALE
                p = jnp.where(tri if first else band, jnp.exp(s - l), 0.0)
                dp = _dot_nt(dos, v2)
                ds = (p * (dp - dd) * ATT_SCALE).astype(BF16)
                dq = _dot(ds, k2)
                dq_ref[rq, :] += jnp.where(head0, dq[0:BLK], dq[BLK:2 * BLK])
                dk_ref[rk, :] += _dot_tn(ds, qs)
                dv_ref[rk, :] += _dot_tn(p.astype(BF16), dos)
                return carry

            _loop_pairs(d, functools.partial(block, first=True), per=4)
            _loop_pairs(N_BLK - d, functools.partial(block, first=False), per=4)

    npair = A_HEADS // 2
    col = lambda off: pl.BlockSpec((T, 128), lambda p: (0, off + p))
    return pl.pallas_call(
        body,
        grid=(npair,),
        in_specs=[col(0), col(npair), col(2 * npair), col(0), col(0), col(0)],
        out_specs=[col(0), col(0), col(0)],
        out_shape=[jax.ShapeDtypeStruct((T, A_W), F32)] * 3,
        compiler_params=_cp("parallel"),
        name=name,
    )(qkv, qkv, qkv, o, lse, do)


def _ln_silu(x, g, b):
    mu = jnp.mean(x, axis=-1, keepdims=True)
    xc = x - mu
    rstd = lax.rsqrt(jnp.mean(xc * xc, axis=-1, keepdims=True) + EPS)
    xh = xc * rstd
    y = xh * g + b
    sig = _sigmoid(y)
    return y * sig, (xh, rstd, y, sig)


def hyb_out_fwd(h, attn, cpre, ln_g, ln_b, w_out, name):
    tt = 512

    def body(h_ref, a_ref, c_ref, g_ref, b_ref, w_ref, hnew_ref, cat_ref):
        cn, _ = _ln_silu(c_ref[...], g_ref[...], b_ref[...])
        ab = a_ref[...].astype(BF16)
        cb = cn.astype(BF16)
        cat_ref[:, 0:A_W] = ab
        cat_ref[:, A_W:D] = cb
        hnew_ref[...] = h_ref[...] + _dot(ab, w_ref[0:A_W, :]) + _dot(cb, w_ref[A_W:D, :])

    half = pl.BlockSpec((tt, A_W), lambda i: (i, 0))
    vec = pl.BlockSpec((1, CONV_C), lambda i: (0, 0))
    full = pl.BlockSpec((tt, D), lambda i: (i, 0))
    return pl.pallas_call(
        body,
        grid=(T // tt,),
        in_specs=[full, half, half, vec, vec, pl.BlockSpec((D, D), lambda i: (0, 0))],
        out_specs=[full, full],
        out_shape=[jax.ShapeDtypeStruct((T, D), F32), jax.ShapeDtypeStruct((T, D), BF16)],
        compiler_params=_cp("parallel"),
        name=name,
    )(h, attn, cpre, ln_g, ln_b, w_out)


def hyb_out_bwd(dres, cpre, ln_g, ln_b, w_out, name):
    tt = 512

    def body(d_ref, c_ref, g_ref, b_ref, w_ref, da_ref, dc_ref, dg_ref, db_ref):
        i = pl.program_id(0)
        db16 = d_ref[...].astype(BF16)
        da_ref[...] = _dot_nt(db16, w_ref[0:A_W, :])
        dcn = _dot_nt(db16, w_ref[A_W:D, :])
        g = g_ref[...]
        _, (xh, rstd, y, sig) = _ln_silu(c_ref[...], g, b_ref[...])
        dy = dcn * _dsilu(y, sig)
        dxh = dy * g
        dc_ref[...] = rstd * (dxh - jnp.mean(dxh, axis=-1, keepdims=True)
                              - xh * jnp.mean(dxh * xh, axis=-1, keepdims=True))
        dg = jnp.sum(dy * xh, axis=0, keepdims=True)
        db = jnp.sum(dy, axis=0, keepdims=True)

        @pl.when(i == 0)
        def _():
            dg_ref[...] = dg
            db_ref[...] = db

        @pl.when(i > 0)
        def _():
            dg_ref[...] += dg
            db_ref[...] += db

    half = pl.BlockSpec((tt, A_W), lambda i: (i, 0))
    vec = pl.BlockSpec((1, CONV_C), lambda i: (0, 0))
    return pl.pallas_call(
        body,
        grid=(T // tt,),
        in_specs=[pl.BlockSpec((tt, D), lambda i: (i, 0)), half, vec, vec, pl.BlockSpec((D, D), lambda i: (0, 0))],
        out_specs=[half, half, vec, vec],
        out_shape=[
            jax.ShapeDtypeStruct((T, A_W), F32),
            jax.ShapeDtypeStruct((T, CONV_C), F32),
            jax.ShapeDtypeStruct((1, CONV_C), F32),
            jax.ShapeDtypeStruct((1, CONV_C), F32),
        ],
        compiler_params=_cp("arbitrary"),
        name=name,
    )(dres, cpre, ln_g, ln_b, w_out)


def hybrid_fwd(h, g_row, w_in, dw_w, dw_b, ln_g, ln_b, w_out, rope, tag):
    hn, qkv, u = proj_fwd(h, g_row, w_in, [(0, 3 * A_W), (3 * A_W, 2 * CONV_C)], f"hyb_proj_{tag}", rope=rope)
    cpre = hyb_conv_fwd(u, dw_w, dw_b, f"hyb_conv_{tag}")
    attn, lse = attn_fwd(qkv, f"attn_fwd_{tag}")
    hnew, cat = hyb_out_fwd(h, attn, cpre, ln_g, ln_b, w_out, f"hyb_out_{tag}")
    return hnew, (h, hn, qkv, u, cpre, attn, lse, cat)


def hybrid_bwd(dres, saved, g_row, w_in, dw_w, ln_g, ln_b, w_out, rope, tag):
    h, hn, qkv, u, cpre, attn, lse, cat = saved
    d_attn, d_cpre, d_lng, d_lnb = hyb_out_bwd(dres, cpre, ln_g, ln_b, w_out, f"hyb_out_bwd_{tag}")
    d_wout = mm_tn_full(cat, dres, BF16, f"hyb_wout_grad_{tag}")
    d_a, d_gate, d_dw, d_db = hyb_conv_bwd(d_cpre, u, dw_w, f"hyb_conv_bwd_{tag}")
    dq, dk, dv = attn_bwd(qkv, attn, lse, d_attn, f"attn_bwd_{tag}")
    splits = [(0, A_W), (A_W, A_W), (2 * A_W, A_W), (3 * A_W, CONV_C), (3 * A_W + CONV_C, CONV_C)]
    dres_new, d_norm, dproj = proj_bwd_data(
        h, g_row, w_in, [dq, dk, dv, d_a, d_gate], splits, dres, f"hyb_proj_bwd_{tag}", rope=rope, n_rot=2)
    d_win = cols_to_slabs(mm_tn_full(hn, dproj, F32, f"hyb_win_grad_{tag}"), None, f"hyb_win_slabs_{tag}")
    return dres_new, dict(norm=d_norm, w_in=d_win, dw_w=d_dw[:CONV_K], dw_b=d_db, ln_g=d_lng, ln_b=d_lnb, w_out=d_wout)


G_SCALE = G_DK ** -0.5
GP_RC = 256
GP_PAD = 8


def gdn_prep_fwd(x, conv_w, name):
    def body(x_ref, w_ref, o_ref, xpad):
        cb = pl.program_id(0)
        xpad[0:GP_PAD, :] = jnp.zeros((GP_PAD, 128), F32)
        xpad[GP_PAD:, :] = x_ref[...]
        for r in range(T // GP_RC):
            r0 = r * GP_RC
            y = jnp.zeros((GP_RC, 128), F32)
            for j in range(G_CONV):
                y = y + w_ref[pl.ds(j, 1), :] * xpad[pl.ds(r0 + GP_PAD - (G_CONV - 1) + j, GP_RC), :]
            s = y * _sigmoid(y)
            n = lax.rsqrt(jnp.sum(s * s, axis=-1, keepdims=True) + EPS)
            o_ref[r0:r0 + GP_RC, :] = s * jnp.where(cb < 2 * G_HEADS, n, 1.0)

    nb = G_QKV // 128
    return pl.pallas_call(
        body,
        grid=(nb,),
        in_specs=[pl.BlockSpec((T, 128), lambda c: (0, c)), pl.BlockSpec((G_CONV, 128), lambda c: (0, c))],
        out_specs=pl.BlockSpec((T, 128), lambda c: (0, c)),
        out_shape=jax.ShapeDtypeStruct((T, G_QKV), F32),
        scratch_shapes=[pltpu.VMEM((T + GP_PAD, 128), F32)],
        compiler_params=_cp("parallel"),
        name=name,
    )(x, conv_w)


def gdn_prep_bwd(dout, x, conv_w, part, l2, name):
    def body(d_ref, x_ref, w_ref, dx_ref, dw_ref, xpad, dypad, dwacc):
        xpad[0:GP_PAD, :] = jnp.zeros((GP_PAD, 128), F32)
        xpad[GP_PAD:, :] = x_ref[...]
        dypad[T:, :] = jnp.zeros((GP_PAD, 128), F32)
        dwacc[...] = jnp.zeros_like(dwacc)
        for r in range(T // GP_RC):
            r0 = r * GP_RC
            y = jnp.zeros((GP_RC, 128), F32)
            xs = []
            for j in range(G_CONV):
                xj = xpad[pl.ds(r0 + GP_PAD - (G_CONV - 1) + j, GP_RC), :]
                xs.append(xj)
                y = y + w_ref[pl.ds(j, 1), :] * xj
            sig = _sigmoid(y)
            s = y * sig
            d = d_ref[r0:r0 + GP_RC, :]
            if l2:
                n = lax.rsqrt(jnp.sum(s * s, axis=-1, keepdims=True) + EPS)
                out = s * n
                d = n * (d - out * jnp.sum(d * out, axis=-1, keepdims=True))
            dy = d * _dsilu(y, sig)
            dypad[r0:r0 + GP_RC, :] = dy
            for j in range(G_CONV):
                dwacc[8 * j:8 * j + 8, :] += (dy * xs[j]).reshape(GP_RC // 8, 8, 128).sum(axis=0)
        for r in range(T // GP_RC):
            r0 = r * GP_RC
            dx = jnp.zeros((GP_RC, 128), F32)
            for j in range(G_CONV):
                dx = dx + w_ref[pl.ds(j, 1), :] * dypad[pl.ds(r0 + (G_CONV - 1) - j, GP_RC), :]
            dx_ref[r0:r0 + GP_RC, :] = dx
        for j in range(G_CONV):
            dw_ref[pl.ds(j, 1), :] = jnp.sum(dwacc[8 * j:8 * j + 8, :], axis=0, keepdims=True)

    nb = G_HEADS
    off = part * nb
    col = pl.BlockSpec((T, 128), lambda c: (0, c))
    return pl.pallas_call(
        body,
        grid=(nb,),
        in_specs=[col, pl.BlockSpec((T, 128), lambda c: (0, off + c)), pl.BlockSpec((G_CONV, 128), lambda c: (0, off + c))],
        out_specs=[col, pl.BlockSpec((G_CONV, 128), lambda c: (0, c))],
        out_shape=[jax.ShapeDtypeStruct((T, G_HEADS * G_DK), F32), jax.ShapeDtypeStruct((G_CONV, G_HEADS * G_DK), F32)],
        scratch_shapes=[
            pltpu.VMEM((T + GP_PAD, 128), F32),
            pltpu.VMEM((T + GP_PAD, 128), F32),
            pltpu.VMEM((8 * G_CONV, 128), F32),
        ],
        compiler_params=_cp("parallel"),
        name=name,
    )(dout, x, conv_w)


def _seg_cumsum(x, reverse=False):
    row = lax.broadcasted_iota(jnp.int32, x.shape, 0) % CH
    s = 1
    while s < CH:
        if reverse:
            x = x + jnp.where(row < CH - s, pltpu.roll(x, x.shape[0] - s, 0), 0.0)
        else:
            x = x + jnp.where(row >= s, pltpu.roll(x, s, 0), 0.0)
        s *= 2
    return x


def _gdn_gates(ba_ref, alog_ref, dt_ref, h):
    ba = ba_ref[...]
    lane = lax.broadcasted_iota(jnp.int32, ba.shape, 1)
    b_col = jnp.sum(jnp.where(lane == h, ba, 0.0), axis=1, keepdims=True)
    a_col = jnp.sum(jnp.where(lane == G_HEADS + h, ba, 0.0), axis=1, keepdims=True)
    lane8 = lax.broadcasted_iota(jnp.int32, (1, G_HEADS), 1)
    alog = jnp.sum(jnp.where(lane8 == h, alog_ref[...], 0.0), axis=1, keepdims=True)
    dt = jnp.sum(jnp.where(lane8 == h, dt_ref[...], 0.0), axis=1, keepdims=True)
    beta = _sigmoid(b_col)
    xa = a_col + dt
    softplus = jnp.maximum(xa, 0.0) + jnp.log(1.0 + jnp.exp(-jnp.abs(xa)))
    ea = jnp.exp(alog)
    return beta, -ea * softplus, xa, ea


def _chunk_masks():
    i = lax.broadcasted_iota(jnp.int32, (CH, CH), 0)
    j = lax.broadcasted_iota(jnp.int32, (CH, CH), 1)
    return i >= j, i > j, i, j


def _decay(gcc, causal):
    gm = gcc[:, 0:CH]
    return jnp.where(causal, jnp.exp(jnp.minimum(gm - gm.T, 0.0)), 0.0)


def _split(a):
    hi = a.astype(BF16)
    return hi, (a - hi.astype(F32)).astype(BF16)


def _dot3(a, b):
    ah, al = _split(a)
    bh, bl = _split(b)
    return _dot(ah, bh) + (_dot(ah, bl) + _dot(al, bh))


def _unit_lower_inverse(lms, i, j):
    eye = jnp.where(i == j, 1.0, 0.0)
    ms = [None] * len(lms)
    b = 1
    while b < CH:
        pair = ((i // (2 * b)) == (j // (2 * b))) & ((i // b) % 2 == 1) & ((j // b) % 2 == 0)
        lbs = [jnp.where(pair, lm, 0.0) for lm in lms]
        if b == 1:
            ms = [eye - lb for lb in lbs]
        else:
            ts = [_dot3(m, lb) for m, lb in zip(ms, lbs)]
            ms = [m - _dot3(t, m) for m, t in zip(ms, ts)]
        b *= 2
    return ms


def gdn_local_fwd(qkv, ba, alog, dtb, name):
    def body(q_ref, k_ref, v_ref, ba_ref, al_ref, dt_ref, u_ref, w_ref, qd_ref, kd_ref, at_ref, el_ref, ti_ref, gcs):
        h = pl.program_id(1)
        beta, g, _, _ = _gdn_gates(ba_ref, al_ref, dt_ref, h)
        gc = _seg_cumsum(jnp.broadcast_to(g, (GRP, 128)))
        gcs[...] = gc
        causal, strict, i, j = _chunk_masks()
        lms = []
        for c in range(CPG):
            r = slice(c * CH, (c + 1) * CH)
            q, k = q_ref[r, :], k_ref[r, :]
            gcc = gc[r, :]
            ec = jnp.exp(gcc)
            gl = gcs[pl.ds(c * CH + CH - 1, 1), :]
            dm = _decay(gcc, causal)
            kbf = k.astype(BF16)
            a1 = _dot_nt((k * beta[r, :]).astype(BF16), kbf)
            lms.append(jnp.where(strict, a1 * dm, 0.0))
            qs = q * G_SCALE
            qd_ref[r, :] = (qs * ec).astype(BF16)
            kd_ref[r, :] = (k * jnp.exp(gl - gcc)).astype(BF16)
            at_ref[r, :] = (_dot_nt(qs.astype(BF16), kbf) * dm).astype(BF16)
            el_ref[pl.ds(c, 1), :] = jnp.exp(gl)
        tinvs = _unit_lower_inverse(lms, i, j)
        for c in range(CPG):
            r = slice(c * CH, (c + 1) * CH)
            bt = beta[r, :]
            tb = tinvs[c].astype(BF16)
            u_ref[r, :] = _dot(tb, (v_ref[r, :] * bt).astype(BF16))
            w_ref[r, :] = _dot(tb, (k_ref[r, :] * bt * jnp.exp(gc[r, :])).astype(BF16)).astype(BF16)
            ti_ref[r, :] = tinvs[c]

    hd = lambda off: pl.BlockSpec((GRP, 128), lambda i, h: (i, off + h))
    vec = pl.BlockSpec((1, G_HEADS), lambda i, h: (0, 0))
    sq = pl.BlockSpec((None, GRP, CH), lambda i, h: (h, i, 0))
    return pl.pallas_call(
        body,
        grid=(N_GRP, G_HEADS),
        in_specs=[hd(0), hd(G_HEADS), hd(2 * G_HEADS), pl.BlockSpec((GRP, 2 * G_HEADS), lambda i, h: (i, 0)), vec, vec],
        out_specs=[hd(0), hd(0), hd(0), hd(0), sq, pl.BlockSpec((None, CPG, 128), lambda i, h: (h, i, 0)), sq],
        out_shape=[
            jax.ShapeDtypeStruct((T, D), F32),
            jax.ShapeDtypeStruct((T, D), BF16),
            jax.ShapeDtypeStruct((T, D), BF16),
            jax.ShapeDtypeStruct((T, D), BF16),
            jax.ShapeDtypeStruct((G_HEADS, T, CH), BF16),
            jax.ShapeDtypeStruct((G_HEADS, T // CH, 128), F32),
            jax.ShapeDtypeStruct((G_HEADS, T, CH), F32),
        ],
        scratch_shapes=[pltpu.VMEM((GRP, 128), F32)],
        compiler_params=_cp("parallel", "parallel"),
        name=name,
    )(qkv, qkv, qkv, ba, alog, dtb)


def gdn_rec_fwd(u, w, qd, kd, at, el, name):
    def body(u_ref, w_ref, qd_ref, kd_ref, at_ref, el_ref, o_ref, vn_ref, st_ref, s_scr):
        @pl.when(pl.program_id(0) == 0)
        def _():
            s_scr[...] = jnp.zeros_like(s_scr)

        states = [s_scr[h] for h in range(G_HEADS)]
        for c in range(CPG):
            r = slice(c * CH, (c + 1) * CH)
            for h in range(G_HEADS):
                ln = slice(h * 128, (h + 1) * 128)
                s = states[h]
                st_ref[h, c] = s
                sb = s.astype(BF16)
                ws = _dot(jnp.concatenate([w_ref[r, ln], qd_ref[r, ln]], axis=0), sb)
                vn = (u_ref[r, ln] - ws[0:CH]).astype(BF16)
                o_ref[r, ln] = ws[CH:2 * CH] + _dot(at_ref[h, r, :], vn)
                states[h] = s * el_ref[h, pl.ds(c, 1), :] + _dot_tn(kd_ref[r, ln], vn)
                vn_ref[r, ln] = vn
        for h in range(G_HEADS):
            s_scr[h] = states[h]

    row = pl.BlockSpec((GRP, D), lambda i: (i, 0))
    return pl.pallas_call(
        body,
        grid=(N_GRP,),
        in_specs=[row, row, row, row, pl.BlockSpec((G_HEADS, GRP, CH), lambda i: (0, i, 0)),
                  pl.BlockSpec((G_HEADS, CPG, 128), lambda i: (0, i, 0))],
        out_specs=[row, row, pl.BlockSpec((G_HEADS, CPG, 128, 128), lambda i: (0, i, 0, 0))],
        out_shape=[
            jax.ShapeDtypeStruct((T, D), F32),
            jax.ShapeDtypeStruct((T, D), BF16),
            jax.ShapeDtypeStruct((G_HEADS, T // CH, 128, 128), F32),
        ],
        scratch_shapes=[pltpu.VMEM((G_HEADS, 128, 128), F32)],
        compiler_params=_cp("arbitrary"),
        name=name,
    )(u, w, qd, kd, at, el)


def gdn_rec_bwd(do, w, qd, kd, at, el, vn, st, name):
    def body(do_ref, w_ref, qd_ref, kd_ref, at_ref, el_ref, vn_ref, st_ref,
             du_ref, dw_ref, dqd_ref, dkd_ref, dat_ref, del_ref, ds_scr):
        @pl.when(pl.program_id(0) == 0)
        def _():
            ds_scr[...] = jnp.zeros_like(ds_scr)

        dstates = [ds_scr[h] for h in range(G_HEADS)]
        for c in reversed(range(CPG)):
            r = slice(c * CH, (c + 1) * CH)
            for h in range(G_HEADS):
                ln = slice(h * 128, (h + 1) * 128)
                ds = dstates[h]
                dsb = ds.astype(BF16)
                sn = st_ref[h, c]
                snb = sn.astype(BF16)
                dob = do_ref[r, ln].astype(BF16)
                vnb = vn_ref[r, ln]
                dvn = (_dot(kd_ref[r, ln], dsb) + _dot_tn(at_ref[h, r, :], dob)).astype(BF16)
                du_ref[r, ln] = dvn
                dkd_ref[r, ln] = _dot_nt(vnb, dsb)
                tot = jnp.sum(jnp.sum(ds * sn, axis=1, keepdims=True), axis=0, keepdims=True)
                del_ref[h, pl.ds(c, 1), :] = jnp.broadcast_to(tot, (1, 128))
                both = _dot_nt(dob, jnp.concatenate([snb, vnb], axis=0))
                dqd_ref[r, ln] = both[:, 0:128]
                dat_ref[h, r, :] = both[:, 128:128 + CH]
                dw_ref[r, ln] = (-_dot_nt(dvn, snb)).astype(BF16)
                dstates[h] = ds * el_ref[h, pl.ds(c, 1), :] + _dot_tn(qd_ref[r, ln], dob) - _dot_tn(w_ref[r, ln], dvn)
        for h in range(G_HEADS):
            ds_scr[h] = dstates[h]

    last = N_GRP - 1
    row = pl.BlockSpec((GRP, D), lambda i: (last - i, 0))
    sq = pl.BlockSpec((G_HEADS, GRP, CH), lambda i: (0, last - i, 0))
    sc = pl.BlockSpec((G_HEADS, CPG, 128), lambda i: (0, last - i, 0))
    return pl.pallas_call(
        body,
        grid=(N_GRP,),
        in_specs=[row, row, row, row, sq, sc, row, pl.BlockSpec((G_HEADS, CPG, 128, 128), lambda i: (0, last - i, 0, 0))],
        out_specs=[row, row, row, row, sq, sc],
        out_shape=[
            jax.ShapeDtypeStruct((T, D), BF16),
            jax.ShapeDtypeStruct((T, D), BF16),
            jax.ShapeDtypeStruct((T, D), F32),
            jax.ShapeDtypeStruct((T, D), F32),
            jax.ShapeDtypeStruct((G_HEADS, T, CH), F32),
            jax.ShapeDtypeStruct((G_HEADS, T // CH, 128), F32),
        ],
        scratch_shapes=[pltpu.VMEM((G_HEADS, 128, 128), F32)],
        compiler_params=_cp("arbitrary"),
        name=name,
    )(do, w, qd, kd, at, el, vn, st)


def gdn_local_bwd(qkv, ba, alog, dtb, tinv, du, dw, dqd, dkd, dat, dl, name):
    def body(q_ref, k_ref, v_ref, ba_ref, al_ref, dt_ref, ti_ref, du_ref, dw_ref, dqd_ref, dkd_ref, dat_ref, dl_ref,
             dq_ref, dk_ref, dv_ref, dba_ref, dal_ref, ddt_ref, gcs):
        gi = pl.program_id(0)
        h = pl.program_id(1)
        beta, g, xa, ea = _gdn_gates(ba_ref, al_ref, dt_ref, h)
        gc = _seg_cumsum(jnp.broadcast_to(g, (GRP, 128)))
        gcs[...] = gc
        causal, strict, _, _ = _chunk_masks()
        dgc_l, dgl_l, dbeta_l, state = [], [], [], []
        for c in range(CPG):
            r = slice(c * CH, (c + 1) * CH)
            q, k, v = q_ref[r, :], k_ref[r, :], v_ref[r, :]
            bt = beta[r, :]
            gcc = gc[r, :]
            ec = jnp.exp(gcc)
            gl = gcs[pl.ds(c * CH + CH - 1, 1), :]
            f2 = jnp.exp(gl - gcc)
            elc = jnp.exp(gl)
            dm = _decay(gcc, causal)
            qs = q * G_SCALE
            kb = k * bt
            vb = v * bt
            kbe = kb * ec
            kbf, kbb, qsb = k.astype(BF16), kb.astype(BF16), qs.astype(BF16)
            a1 = _dot_nt(kbb, kbf)
            qk = _dot_nt(qsb, kbf)
            ti = ti_ref[r, :]
            tb = ti.astype(BF16)
            du_c, dw_c = du_ref[r, :], dw_ref[r, :]
            dqd_c, dkd_c, dat_c = dqd_ref[r, :], dkd_ref[r, :], dat_ref[r, :]

            dqs = dqd_c * ec
            d_e = jnp.sum(dqd_c * qs, axis=1, keepdims=True)
            dk = dkd_c * f2
            tcol = jnp.sum(dkd_c * k, axis=1, keepdims=True) * f2[:, 0:1]
            dgl = jnp.sum(tcol, axis=0, keepdims=True) + dl_ref[pl.ds(c, 1), 0:1] * elc[:, 0:1]
            dgc = -tcol
            dqk = (dat_c * dm).astype(BF16)
            d_d = dat_c * qk
            dqs = dqs + _dot(dqk, kbf)
            dk = dk + _dot_tn(dqk, qsb)
            dtinv = _dot_nt(du_c, vb.astype(BF16)) + _dot_nt(dw_c, kbe.astype(BF16))
            dvb = _dot_tn(tb, du_c)
            dkbe = _dot_tn(tb, dw_c)
            dq_ref[r, :] = dqs * G_SCALE
            state.append((ti.T, dtinv, dm, a1, dkbe, dvb, d_d, dk, d_e, dgc, dgl))

        xs = [_dot3(st[0], st[1]) for st in state]
        dlms = [jnp.where(strict, -_dot3(x, st[0]), 0.0) for x, st in zip(xs, state)]

        for c in range(CPG):
            r = slice(c * CH, (c + 1) * CH)
            _, _, dm, a1, dkbe, dvb, d_d, dk, d_e, dgc, dgl = state[c]
            dlm = dlms[c]
            k, v = k_ref[r, :], v_ref[r, :]
            bt = beta[r, :]
            ec = jnp.exp(gc[r, :])
            kb = k * bt
            kbf, kbb = k.astype(BF16), kb.astype(BF16)
            da1 = (dlm * dm).astype(BF16)
            d_d = d_d + dlm * a1
            dkb = _dot(da1, kbf) + dkbe * ec
            dk = dk + _dot_tn(da1, kbb)
            d_e = d_e + jnp.sum(dkbe * kb, axis=1, keepdims=True)
            dk = dk + dkb * bt
            dbeta_l.append(jnp.sum(dkb * k, axis=1, keepdims=True) + jnp.sum(dvb * v, axis=1, keepdims=True))
            ddiff = d_d * dm
            dgc = dgc + jnp.sum(ddiff, axis=1, keepdims=True) - jnp.sum(ddiff.T, axis=1, keepdims=True)
            dgc = dgc + d_e * ec[:, 0:1]
            dgc_l.append(dgc)
            dgl_l.append(jnp.broadcast_to(dgl, (CH, 1)))
            dk_ref[r, :] = dk
            dv_ref[r, :] = dvb * bt

        dgc_all = jnp.broadcast_to(jnp.concatenate(dgc_l, axis=0), (GRP, 128))
        dg = _seg_cumsum(dgc_all, reverse=True)[:, 0:1] + jnp.concatenate(dgl_l, axis=0)
        dbeta = jnp.concatenate(dbeta_l, axis=0)
        da = dg * (-ea) * _sigmoid(xa)
        db = dbeta * beta * (1.0 - beta)
        lane = lax.broadcasted_iota(jnp.int32, (GRP, 2 * G_HEADS), 1)
        dba = jnp.where(lane == h, db, 0.0) + jnp.where(lane == G_HEADS + h, da, 0.0)
        lane8 = lax.broadcasted_iota(jnp.int32, (1, G_HEADS), 1)
        dal = jnp.where(lane8 == h, jnp.sum(dg * g, axis=0, keepdims=True), 0.0)
        ddt = jnp.where(lane8 == h, jnp.sum(da, axis=0, keepdims=True), 0.0)

        @pl.when(h == 0)
        def _():
            dba_ref[...] = dba

        @pl.when(h > 0)
        def _():
            dba_ref[...] += dba

        @pl.when((h == 0) & (gi == 0))
        def _():
            dal_ref[...] = dal
            ddt_ref[...] = ddt

        @pl.when((h > 0) | (gi > 0))
        def _():
            dal_ref[...] += dal
            ddt_ref[...] += ddt

    hd = lambda off: pl.BlockSpec((GRP, 128), lambda i, h: (i, off + h))
    vec = pl.BlockSpec((1, G_HEADS), lambda i, h: (0, 0))
    sq = pl.BlockSpec((None, GRP, CH), lambda i, h: (h, i, 0))
    gates = pl.BlockSpec((GRP, 2 * G_HEADS), lambda i, h: (i, 0))
    return pl.pallas_call(
        body,
        grid=(N_GRP, G_HEADS),
        in_specs=[hd(0), hd(G_HEADS), hd(2 * G_HEADS), gates, vec, vec, sq, hd(0), hd(0), hd(0), hd(0), sq,
                  pl.BlockSpec((None, CPG, 128), lambda i, h: (h, i, 0))],
        out_specs=[hd(0), hd(0), hd(0), gates, vec, vec],
        out_shape=[
            jax.ShapeDtypeStruct((T, D), F32),
            jax.ShapeDtypeStruct((T, D), F32),
            jax.ShapeDtypeStruct((T, D), F32),
            jax.ShapeDtypeStruct((T, 2 * G_HEADS), F32),
            jax.ShapeDtypeStruct((1, G_HEADS), F32),
            jax.ShapeDtypeStruct((1, G_HEADS), F32),
        ],
        scratch_shapes=[pltpu.VMEM((GRP, 128), F32)],
        compiler_params=_cp("arbitrary", "arbitrary"),
        name=name,
    )(qkv, qkv, qkv, ba, alog, dtb, tinv, du, dw, dqd, dkd, dat, dl)


def _gated_norm(o, z, g):
    rstd = lax.rsqrt(jnp.mean(o * o, axis=-1, keepdims=True) + EPS)
    oh = o * rstd
    sig = _sigmoid(z)
    return oh, rstd, sig


def gdn_out_fwd(h, o, z, norm_g, w_out, name):
    tt = 512

    def body(h_ref, o_ref, z_ref, g_ref, w_ref, hnew_ref, cat_ref):
        g = g_ref[...]
        for hh in range(G_HEADS):
            ln = slice(hh * 128, (hh + 1) * 128)
            zz = z_ref[:, ln]
            oh, _, sig = _gated_norm(o_ref[:, ln], zz, g)
            cat_ref[:, ln] = (oh * g * (zz * sig)).astype(BF16)
        hnew_ref[...] = h_ref[...] + _dot(cat_ref[...], w_ref[...])

    full = pl.BlockSpec((tt, D), lambda i: (i, 0))
    return pl.pallas_call(
        body,
        grid=(T // tt,),
        in_specs=[full, full, full, pl.BlockSpec((1, 128), lambda i: (0, 0)), pl.BlockSpec((D, D), lambda i: (0, 0))],
        out_specs=[full, full],
        out_shape=[jax.ShapeDtypeStruct((T, D), F32), jax.ShapeDtypeStruct((T, D), BF16)],
        compiler_params=_cp("parallel"),
        name=name,
    )(h, o, z, norm_g, w_out)


def gdn_out_bwd(dres, o, z, norm_g, w_out, name):
    tt = 512

    def body(d_ref, o_ref, z_ref, g_ref, w_ref, do_ref, dz_ref, dg_ref, dcat):
        i = pl.program_id(0)
        g = g_ref[...]
        dcat[...] = _dot_nt(d_ref[...].astype(BF16), w_ref[...])
        dg = jnp.zeros((1, 128), F32)
        for hh in range(G_HEADS):
            ln = slice(hh * 128, (hh + 1) * 128)
            zz = z_ref[:, ln]
            oh, rstd, sig = _gated_norm(o_ref[:, ln], zz, g)
            dout = dcat[:, ln]
            dy = dout * (zz * sig)
            dz_ref[:, ln] = dout * (oh * g) * _dsilu(zz, sig)
            dg = dg + jnp.sum(dy * oh, axis=0, keepdims=True)
            doh = dy * g
            do_ref[:, ln] = rstd * (doh - oh * jnp.mean(doh * oh, axis=-1, keepdims=True))

        @pl.when(i == 0)
        def _():
            dg_ref[...] = dg

        @pl.when(i > 0)
        def _():
            dg_ref[...] += dg

    full = pl.BlockSpec((tt, D), lambda i: (i, 0))
    vec = pl.BlockSpec((1, 128), lambda i: (0, 0))
    return pl.pallas_call(
        body,
        grid=(T // tt,),
        in_specs=[full, full, full, vec, pl.BlockSpec((D, D), lambda i: (0, 0))],
        out_specs=[full, full, vec],
        out_shape=[jax.ShapeDtypeStruct((T, D), F32), jax.ShapeDtypeStruct((T, D), F32), jax.ShapeDtypeStruct((1, 128), F32)],
        scratch_shapes=[pltpu.VMEM((tt, D), F32)],
        compiler_params=_cp("arbitrary"),
        name=name,
    )(dres, o, z, norm_g, w_out)


GDN_SPLITS = [(0, 1024), (1024, 1024), (2048, 1024), (3072, 1024), (4096, 2 * G_HEADS)]


def gdn_fwd(h, g_row, w_in, conv_w, alog, dtb, norm_g, w_out, tag):
    hn, qkv_pre, z, ba = proj_fwd(h, g_row, w_in, [(0, G_QKV), (G_QKV, 1024), (4096, 2 * G_HEADS)], f"gdn_proj_{tag}")
    qkv = gdn_prep_fwd(qkv_pre, conv_w, f"gdn_prep_{tag}")
    u, w, qd, kd, at, el, tinv = gdn_local_fwd(qkv, ba, alog, dtb, f"gdn_local_{tag}")
    o, vn, st = gdn_rec_fwd(u, w, qd, kd, at, el, f"gdn_rec_{tag}")
    hnew, cat = gdn_out_fwd(h, o, z, norm_g, w_out, f"gdn_out_{tag}")
    return hnew, (h, hn, qkv_pre, z, ba, qkv, w, qd, kd, at, el, tinv, o, vn, st, cat)


def gdn_bwd(dres, saved, g_row, w_in, conv_w, alog, dtb, norm_g, w_out, tag):
    h, hn, qkv_pre, z, ba, qkv, w, qd, kd, at, el, tinv, o, vn, st, cat = saved
    d_o, d_z, d_ng = gdn_out_bwd(dres, o, z, norm_g, w_out, f"gdn_out_bwd_{tag}")
    d_wout = mm_tn_full(cat, dres, BF16, f"gdn_wout_grad_{tag}")
    du, dw, dqd, dkd, dat, dl = gdn_rec_bwd(d_o, w, qd, kd, at, el, vn, st, f"gdn_rec_bwd_{tag}")
    dq, dk, dv, dba, dal, ddt = gdn_local_bwd(qkv, ba, alog, dtb, tinv, du, dw, dqd, dkd, dat, dl, f"gdn_local_bwd_{tag}")
    dpre, dcw = [], []
    for part, d in enumerate((dq, dk, dv)):
        dx, dwc = gdn_prep_bwd(d, qkv_pre, conv_w, part, part < 2, f"gdn_prep_bwd_{tag}_{part}")
        dpre.append(dx)
        dcw.append(dwc)
    parts = dpre + [d_z, dba]
    dres_new, d_norm, dproj = proj_bwd_data(h, g_row, w_in, parts, GDN_SPLITS, dres, f"gdn_proj_bwd_{tag}")
    d_win = cols_to_slabs(mm_tn_full(hn, dproj, F32, f"gdn_win_grad_{tag}"), mm_tn(hn, dba, f"gdn_win_grad_ba_{tag}"),
                          f"gdn_win_slabs_{tag}")
    return dres_new, dict(norm=d_norm, w_in=d_win, conv_w=jnp.concatenate(dcw, axis=1), A_log=dal, dt_bias=ddt,
                          norm_g=d_ng, w_out=d_wout)


MESH = pl.DeviceIdType.MESH
ANY = pl.BlockSpec(memory_space=pl.ANY)


def _coords():
    return lax.axis_index("x"), lax.axis_index("y"), lax.axis_index("c")


def _slot(p):
    return 4 * p[0] + 2 * p[1] + p[2]


def all_gather(shards, name):
    k_n = len(shards)

    def body(*refs):
        srcs, dsts = refs[:k_n], refs[k_n:2 * k_n]
        send_sems, recv_sems, local_sems = refs[2 * k_n:]
        x, y, c = _coords()
        me, sibling = (x, y, c), (x, y, 1 - c)
        chips = [(1 - x, y), (x, 1 - y), (1 - x, 1 - y)]

        def copy(k, s, block, to, from_src=False):
            rows = dsts[k].at[_slot(block)]
            return pltpu.make_async_remote_copy(
                src_ref=srcs[k] if from_src else rows, dst_ref=rows,
                send_sem=send_sems.at[k, s], recv_sem=recv_sems.at[k, s], device_id=to, device_id_type=MESH)

        local = [pltpu.make_async_copy(srcs[k], dsts[k].at[_slot(me)], local_sems.at[k]) for k in range(k_n)]
        for cp in local:
            cp.start()
        first = []
        for k in range(k_n):
            first.append(copy(k, 0, me, sibling, True))
            first += [copy(k, 1 + j, me, (*chip, c), True) for j, chip in enumerate(chips)]
        for cp in first:
            cp.start()
        passed = []
        for j, chip in enumerate(chips):
            for k in range(k_n):
                copy(k, 1 + j, (*chip, c), me).wait_recv()
                fw = copy(k, 4 + j, (*chip, c), sibling)
                fw.start()
                passed.append(fw)
        for k in range(k_n):
            copy(k, 0, sibling, me).wait_recv()
            for j, chip in enumerate(chips):
                copy(k, 4 + j, (*chip, 1 - c), me).wait_recv()
        for cp in first + passed:
            cp.wait_send()
        for cp in local:
            cp.wait()

    return pl.pallas_call(
        body,
        in_specs=[ANY] * k_n,
        out_specs=[ANY] * k_n,
        out_shape=[jax.ShapeDtypeStruct((N_DEV,) + s.shape, s.dtype) for s in shards],
        scratch_shapes=[pltpu.SemaphoreType.DMA((k_n, 7)), pltpu.SemaphoreType.DMA((k_n, 7)),
                        pltpu.SemaphoreType.DMA((k_n,))],
        name=name,
    )(*shards)


HBM = pl.BlockSpec(memory_space=pltpu.HBM)
SEM = pl.BlockSpec(memory_space=pltpu.SEMAPHORE)
EFFECT = pltpu.SideEffectType.DATAFLOW_SIDE_EFFECTING


def _hbm(a):
    return pltpu.with_memory_space_constraint(a, pltpu.HBM)


def _peer_list(x, y, c):
    peers = []
    for j in range(1, N_DEV):
        jx, jy, jc = (j >> 2) & 1, (j >> 1) & 1, j & 1
        peers.append((x if jx == 0 else 1 - x, y if jy == 0 else 1 - y, c if jc == 0 else 1 - c))
    return peers


def _push_views(kind, layer, src_ref, land_ref, me, peer_slot):
    if kind == "gather":
        return src_ref, land_ref.at[me], land_ref.at[peer_slot]
    if layer is None:
        return src_ref.at[peer_slot], land_ref.at[me], land_ref.at[peer_slot]
    return src_ref.at[peer_slot], land_ref.at[me, layer], land_ref.at[peer_slot, layer]


def _push_copies(groups, srcs, lands, sems):
    x, y, c = _coords()
    me = _slot((x, y, c))
    peers = _peer_list(x, y, c)
    t = 0
    for gi, group in enumerate(groups):
        for ti, (kind, layer, _, li) in enumerate(group):
            for j, peer in enumerate(peers):
                out, there, here = _push_views(kind, layer, srcs[t], lands[li], me, _slot(peer))
                k = ti * (N_DEV - 1) + j
                yield out, there, here, sems[2 * gi].at[k], sems[2 * gi + 1].at[k], peer
            t += 1


def push_start(groups, lands, name, carry=()):
    flat = [it for g in groups for it in g]
    n, n_l, n_g, n_c = len(flat), len(lands), len(groups), len(carry)
    n_in = n + n_l + n_c

    def body(*refs):
        srcs, land_refs, sems = refs[:n], refs[n:n + n_l], refs[n_in:n_in + 2 * n_g]
        for out, there, _, s_sem, r_sem, peer in _push_copies(groups, srcs, land_refs, sems):
            pltpu.make_async_remote_copy(src_ref=out, dst_ref=there, send_sem=s_sem, recv_sem=r_sem,
                                         device_id=peer, device_id_type=MESH).start()

    arrays = [it[2] for it in flat] + list(lands) + list(carry)
    sem_shapes = []
    for g in groups:
        sem_shapes += [pltpu.SemaphoreType.DMA((len(g) * (N_DEV - 1),))] * 2
    outs = pl.pallas_call(
        body,
        name=name,
        in_specs=[HBM] * n_in,
        out_specs=[SEM] * (2 * n_g) + [HBM] * n_in,
        out_shape=sem_shapes + [pltpu.HBM(a.shape, a.dtype) for a in arrays],
        input_output_aliases={i: 2 * n_g + i for i in range(n_in)},
        compiler_params=pltpu.CompilerParams(has_side_effects=EFFECT),
    )(*[_hbm(a) for a in arrays])
    sems, thru = list(outs[:2 * n_g]), list(outs[2 * n_g:])
    return sems, thru[:n], thru[n:n + n_l], thru[n + n_l:]


def push_wait(groups, lands, sems, after, name):
    flat = [it for g in groups for it in g]
    n, n_l, n_g = len(flat), len(lands), len(groups)

    def body(*refs):
        srcs, land_refs, sem_refs = refs[:n], refs[n:n + n_l], refs[n + n_l:n + n_l + 2 * n_g]
        for out, _, here, s_sem, r_sem, peer in _push_copies(groups, srcs, land_refs, sem_refs):
            cp = pltpu.make_async_remote_copy(src_ref=out, dst_ref=here, send_sem=s_sem, recv_sem=r_sem,
                                              device_id=peer, device_id_type=MESH)
            cp.wait_send()
            cp.wait_recv()

    arrays = [it[2] for it in flat] + list(lands)
    outs = pl.pallas_call(
        body,
        name=name,
        in_specs=[HBM] * (n + n_l) + [SEM] * (2 * n_g) + [ANY],
        out_specs=[HBM] * (n + n_l),
        out_shape=[pltpu.HBM(a.shape, a.dtype) for a in arrays],
        input_output_aliases={i: i for i in range(n + n_l)},
        compiler_params=pltpu.CompilerParams(has_side_effects=EFFECT),
    )(*arrays, *sems, after)
    return list(outs[:n]), list(outs[n:])


def place_own(srcs, lands, where, name):
    n, n_l = len(srcs), len(lands)

    def body(*refs):
        src_refs, out_refs, sems = refs[:n], refs[n + n_l:n + 2 * n_l], refs[-1]
        me = _slot(_coords())
        copies = [pltpu.make_async_copy(src_refs[t].at[me], out_refs[k].at[me, layer], sems.at[t])
                  for t, (k, layer) in enumerate(where)]
        for cp in copies:
            cp.start()
        for cp in copies:
            cp.wait()

    return pl.pallas_call(
        body,
        in_specs=[ANY] * (n + n_l),
        out_specs=[ANY] * n_l,
        out_shape=[jax.ShapeDtypeStruct(a.shape, a.dtype) for a in lands],
        input_output_aliases={n + i: i for i in range(n_l)},
        scratch_shapes=[pltpu.SemaphoreType.DMA((n,))],
        name=name,
    )(*srcs, *lands)


def sum_slabs(parts, name):
    n, rows, cols = parts.shape

    def body(p_ref, o_ref):
        g = p_ref[0]
        for s in range(1, n):
            g = g + p_ref[s]
        o_ref[...] = g

    return pl.pallas_call(body, out_shape=jax.ShapeDtypeStruct((rows, cols), F32), name=name)(parts)


def _row_tile(rows, cols):
    if rows * cols * 4 <= (1 << 20) or rows % 8:
        return rows
    tr = rows
    while tr % 2 == 0 and (tr // 2) % 8 == 0 and tr * cols * 4 > (1 << 20):
        tr //= 2
    return tr


def adamw(parts, w, m, v, name):
    p_n = parts.shape[0]
    rows, cols = w.shape
    tr = _row_tile(rows, cols)

    def body(p_ref, w_ref, m_ref, v_ref, g_ref, d_ref, nm_ref, nv_ref):
        g = p_ref[0].astype(F32)
        for s in range(1, p_n):
            g = g + p_ref[s].astype(F32)
        g_ref[...] = g
        d_ref[...], nm_ref[...], nv_ref[...] = _adam_update(g, w_ref[...], m_ref[...], v_ref[...])

    blk = pl.BlockSpec((tr, cols), lambda i: (i, 0))
    return pl.pallas_call(
        body,
        grid=(rows // tr,),
        in_specs=[pl.BlockSpec((p_n, tr, cols), lambda i: (0, i, 0)), blk, blk, blk],
        out_specs=[blk] * 4,
        out_shape=[jax.ShapeDtypeStruct((rows, cols), F32)] * 4,
        compiler_params=_cp("parallel"),
        name=name,
    )(parts, w, m, v)


def _adam_update(g, w, m, v):
    m_new = ADAM_B1 * m + (1.0 - ADAM_B1) * g
    v_new = ADAM_B2 * v + (1.0 - ADAM_B2) * (g * g)
    m_hat = m_new / (1.0 - ADAM_B1 ** ADAM_STEP)
    v_hat = v_new / (1.0 - ADAM_B2 ** ADAM_STEP)
    return -ADAM_LR * (m_hat / (jnp.sqrt(v_hat) + ADAM_EPS) + ADAM_WD * w), m_new, v_new


def _adamw_nd(parts, w, m, v, name):
    p_n = parts.shape[0]
    n_l, rows, cols = w.shape
    tr = _row_tile(rows, cols)

    def body(p_ref, w_ref, m_ref, v_ref, g_ref, d_ref, nm_ref, nv_ref):
        g = p_ref[0].astype(F32)
        for s in range(1, p_n):
            g = g + p_ref[s].astype(F32)
        g_ref[...] = g
        d_ref[...], nm_ref[...], nv_ref[...] = _adam_update(g, w_ref[...], m_ref[...], v_ref[...])

    blk = pl.BlockSpec((None, tr, cols), lambda l, i: (l, i, 0))
    return pl.pallas_call(
        body,
        grid=(n_l, rows // tr),
        in_specs=[pl.BlockSpec((p_n, None, tr, cols), lambda l, i: (0, l, i, 0)), blk, blk, blk],
        out_specs=[blk] * 4,
        out_shape=[jax.ShapeDtypeStruct(w.shape, F32)] * 4,
        compiler_params=_cp("parallel", "parallel"),
        name=name,
    )(parts, w, m, v)


def slabs_to_cols(slabs, name):
    n, r, w = slabs.shape
    tr = 256 if r % 256 == 0 else r

    def body(s_ref, o_ref):
        for s in range(n):
            o_ref[:, w * s:w * (s + 1)] = s_ref[s]

    return pl.pallas_call(
        body,
        grid=(r // tr,),
        in_specs=[pl.BlockSpec((n, tr, w), lambda i: (0, i, 0))],
        out_specs=pl.BlockSpec((tr, n * w), lambda i: (i, 0)),
        out_shape=jax.ShapeDtypeStruct((r, n * w), slabs.dtype),
        compiler_params=_cp("parallel"),
        name=name,
    )(slabs)


FFN_IN = ("ffn1_w_in", "ffn2_w_in")
REPL = ["ffn1_norm", "mix_norm", "ffn2_norm", "hyb_dw_b", "hyb_ln_g", "hyb_ln_b", "gdn_A_log", "gdn_dt_bias",
        "gdn_norm_g", "final_norm"]
WEIGHTS = ["ffn1_norm", "ffn1_w_in", "ffn1_w_out", "mix_norm", "ffn2_norm", "ffn2_w_in", "ffn2_w_out", "hyb_w_in",
           "hyb_dw_w", "hyb_dw_b", "hyb_ln_g", "hyb_ln_b", "hyb_w_out", "gdn_w_in", "gdn_conv_w", "gdn_A_log",
           "gdn_dt_bias", "gdn_norm_g", "gdn_w_out", "final_norm"]


def _pack(arrs, rows):
    flat = jnp.concatenate([a.reshape(-1) for a in arrs])
    return jnp.pad(flat, (0, rows * 128 - flat.shape[0])).reshape(rows, 128)


def kernel(x, positions, ffn1_norm, ffn1_w_in, ffn1_w_out, mix_norm, ffn2_norm, ffn2_w_in, ffn2_w_out, hyb_w_in, hyb_dw_w, hyb_dw_b, hyb_ln_g, hyb_ln_b, hyb_w_out, gdn_w_in, gdn_conv_w, gdn_A_log, gdn_dt_bias, gdn_norm_g, gdn_w_out, final_norm, loss_target, m_ffn1_norm, m_ffn1_w_in, m_ffn1_w_out, m_mix_norm, m_ffn2_norm, m_ffn2_w_in, m_ffn2_w_out, m_hyb_w_in, m_hyb_dw_w, m_hyb_dw_b, m_hyb_ln_g, m_hyb_ln_b, m_hyb_w_out, m_gdn_w_in, m_gdn_conv_w, m_gdn_A_log, m_gdn_dt_bias, m_gdn_norm_g, m_gdn_w_out, m_final_norm, v_ffn1_norm, v_ffn1_w_in, v_ffn1_w_out, v_mix_norm, v_ffn2_norm, v_ffn2_w_in, v_ffn2_w_out, v_hyb_w_in, v_hyb_dw_w, v_hyb_dw_b, v_hyb_ln_g, v_hyb_ln_b, v_hyb_w_out, v_gdn_w_in, v_gdn_conv_w, v_gdn_A_log, v_gdn_dt_bias, v_gdn_norm_g, v_gdn_w_out, v_final_norm):
    w = dict(ffn1_norm=ffn1_norm, ffn1_w_in=ffn1_w_in, ffn1_w_out=ffn1_w_out, mix_norm=mix_norm, ffn2_norm=ffn2_norm,
             ffn2_w_in=ffn2_w_in, ffn2_w_out=ffn2_w_out, hyb_w_in=hyb_w_in, hyb_dw_w=hyb_dw_w, hyb_dw_b=hyb_dw_b,
             hyb_ln_g=hyb_ln_g, hyb_ln_b=hyb_ln_b, hyb_w_out=hyb_w_out, gdn_w_in=gdn_w_in, gdn_conv_w=gdn_conv_w,
             gdn_A_log=gdn_A_log, gdn_dt_bias=gdn_dt_bias, gdn_norm_g=gdn_norm_g, gdn_w_out=gdn_w_out,
             final_norm=final_norm)
    mom = dict(ffn1_norm=m_ffn1_norm, ffn1_w_in=m_ffn1_w_in, ffn1_w_out=m_ffn1_w_out, mix_norm=m_mix_norm,
               ffn2_norm=m_ffn2_norm, ffn2_w_in=m_ffn2_w_in, ffn2_w_out=m_ffn2_w_out, hyb_w_in=m_hyb_w_in,
               hyb_dw_w=m_hyb_dw_w, hyb_dw_b=m_hyb_dw_b, hyb_ln_g=m_hyb_ln_g, hyb_ln_b=m_hyb_ln_b,
               hyb_w_out=m_hyb_w_out, gdn_w_in=m_gdn_w_in, gdn_conv_w=m_gdn_conv_w, gdn_A_log=m_gdn_A_log,
               gdn_dt_bias=m_gdn_dt_bias, gdn_norm_g=m_gdn_norm_g, gdn_w_out=m_gdn_w_out, final_norm=m_final_norm)
    var = dict(ffn1_norm=v_ffn1_norm, ffn1_w_in=v_ffn1_w_in, ffn1_w_out=v_ffn1_w_out, mix_norm=v_mix_norm,
               ffn2_norm=v_ffn2_norm, ffn2_w_in=v_ffn2_w_in, ffn2_w_out=v_ffn2_w_out, hyb_w_in=v_hyb_w_in,
               hyb_dw_w=v_hyb_dw_w, hyb_dw_b=v_hyb_dw_b, hyb_ln_g=v_hyb_ln_g, hyb_ln_b=v_hyb_ln_b,
               hyb_w_out=v_hyb_w_out, gdn_w_in=v_gdn_w_in, gdn_conv_w=v_gdn_conv_w, gdn_A_log=v_gdn_A_log,
               gdn_dt_bias=v_gdn_dt_bias, gdn_norm_g=v_gdn_norm_g, gdn_w_out=v_gdn_w_out, final_norm=v_final_norm)
    xi, yi, ci = _coords()
    me = 4 * xi + 2 * yi + ci
    for group in (w, mom, var):
        for n in FFN_IN:
            group[n] = jnp.swapaxes(group[n], 1, 2)

    big = ["ffn1_w_in", "ffn1_w_out", "ffn2_w_in", "ffn2_w_out", "hyb_w_in", "hyb_w_out", "gdn_w_in", "gdn_w_out"]
    ag_groups, ag_lands = [], []

    def add_group(shards):
        group = []
        for s in shards:
            land = lax.dynamic_update_slice(lax.empty((N_DEV,) + s.shape, s.dtype), s[None], (me,) + (0,) * s.ndim)
            group.append(("gather", None, s, len(ag_lands)))
            ag_lands.append(land)
        ag_groups.append(group)

    first = all_gather([w["ffn1_w_in"][0].astype(BF16), ffn1_w_out[0].astype(BF16)], "weights_gather_first")
    for l in range(DEPTH):
        i = l // 2
        if l == 0:
            ag_groups.append([])
        else:
            add_group([w["ffn1_w_in"][l].astype(BF16), ffn1_w_out[l].astype(BF16)])
        if l % 2 == 0:
            add_group([hyb_w_in[i].astype(BF16), hyb_w_out[i].astype(BF16), hyb_dw_w[i]])
        else:
            add_group([gdn_w_in[i].astype(BF16), gdn_w_out[i].astype(BF16), gdn_conv_w[i]])
        add_group([w["ffn2_w_in"][l].astype(BF16), ffn2_w_out[l].astype(BF16)])
    ag_sems, ag_srcs, ag_lands, first = push_start(ag_groups[1:], ag_lands, "weights_gather_start", carry=first)
    ag_sems = [None, None] + ag_sems

    def fetch(gi, after):
        if gi == 0:
            return first
        group = ag_groups[gi]
        base = sum(len(g) for g in ag_groups[:gi])
        items = [(kind, layer, ag_srcs[base + t], t) for t, (kind, layer, _, _) in enumerate(group)]
        lands = [ag_lands[li] for _, _, _, li in group]
        return push_wait([items], lands, ag_sems[2 * gi:2 * gi + 2], after, f"weights_gather_wait_{gi}")[1]

    row = lambda a: a.reshape(1, -1)

    rope = make_rope(positions)
    h = x[0]
    saved = []
    for l in range(DEPTH):
        i = l // 2
        rec = {"h1": h}
        wi, wo = fetch(3 * l, h)
        rec["w1"] = (wi.reshape(2, FFN_TILES, FFN_SHARD, D), wo)
        h, rec["hn1"], rec["a1"], rec["b1"] = ffn_fwd(h, row(ffn1_norm[l]), *rec["w1"], l, "1")
        mi, mo, mc = fetch(3 * l + 1, h)
        if l % 2 == 0:
            rec["wm"] = (slabs_to_cols(mi, f"hyb_w_in_cols_{i}"),
                         jnp.pad(slabs_to_cols(mc, f"hyb_dw_w_cols_{i}"), ((0, 1), (0, 0))), mo.reshape(D, D))
            w_in_f, dw_f, w_out_f = rec["wm"]
            h, rec["mix"] = hybrid_fwd(h, row(mix_norm[l]), w_in_f, dw_f, row(hyb_dw_b[i]), row(hyb_ln_g[i]),
                                       row(hyb_ln_b[i]), w_out_f, rope, str(i))
        else:
            rec["wm"] = (slabs_to_cols(mi, f"gdn_w_in_cols_{i}"), slabs_to_cols(mc, f"gdn_conv_w_cols_{i}"),
                         mo.reshape(D, D))
            w_in_f, cw_f, w_out_f = rec["wm"]
            h, rec["mix"] = gdn_fwd(h, row(mix_norm[l]), w_in_f, cw_f, row(gdn_A_log[i]), row(gdn_dt_bias[i]),
                                    row(gdn_norm_g[i]), w_out_f, str(i))
        rec["h2"] = h
        wi, wo = fetch(3 * l + 2, h)
        rec["w2"] = (wi.reshape(2, FFN_TILES, FFN_SHARD, D), wo)
        h, rec["hn2"], rec["a2"], rec["b2"] = ffn_fwd(h, row(ffn2_norm[l]), *rec["w2"], l, "2")
        saved.append(rec)
    dres, d_final, loss_acc = final_loss(h, row(final_norm), loss_target[0])

    ge_land = {n: lax.empty((N_DEV,) + w[n].shape, BF16) for n in big}
    ge_pending = []

    def send(named, layer, tag, carry):
        lands = [ge_land[n] for n, _ in named]
        group = [("scatter", layer, s, t) for t, (_, s) in enumerate(named)]
        sems, srcs, lands_out, carried = push_start([group], lands, f"grad_send_{tag}", carry=[carry])
        for (n, _), land in zip(named, lands_out):
            ge_land[n] = land
        ge_pending.append(([(n, layer, s) for (n, _), s in zip(named, srcs)], sems))
        return carried[0]

    gsmall = {n: [None] * (DEPTH if n in ("ffn1_norm", "mix_norm", "ffn2_norm") else 2) for n in REPL[:-1]}
    gsmall["hyb_dw_w"] = [None, None]
    gsmall["gdn_conv_w"] = [None, None]
    for l in reversed(range(DEPTH)):
        i = l // 2
        rec = saved[l]
        dhn, dwin, dwout = ffn_bwd(rec["hn2"], rec["a2"], rec["b2"], dres, *rec["w2"], l, "2")
        dhn = send([("ffn2_w_in", dwin.reshape(N_DEV, FFN_SHARD, D)),
                    ("ffn2_w_out", dwout.reshape(N_DEV, FFN_SHARD // 2, D))], l, f"ffn2_{l}", dhn)
        dres, dg = norm_bwd(rec["h2"], row(ffn2_norm[l]), dhn, dres, f"ffn2_norm_bwd_{l}")
        gsmall["ffn2_norm"][l] = dg
        if l % 2 == 0:
            w_in_f, dw_f, w_out_f = rec["wm"]
            dres, gr = hybrid_bwd(dres, rec["mix"], row(mix_norm[l]), w_in_f, dw_f, row(hyb_ln_g[i]),
                                  row(hyb_ln_b[i]), w_out_f, rope, str(i))
            dres = send([("hyb_w_in", gr["w_in"]), ("hyb_w_out", gr["w_out"].reshape(N_DEV, D // N_DEV, D))],
                        i, f"hyb_{i}", dres)
            for n in ("dw_w", "dw_b", "ln_g", "ln_b"):
                gsmall["hyb_" + n][i] = gr[n]
        else:
            w_in_f, cw_f, w_out_f = rec["wm"]
            dres, gr = gdn_bwd(dres, rec["mix"], row(mix_norm[l]), w_in_f, cw_f, row(gdn_A_log[i]),
                               row(gdn_dt_bias[i]), row(gdn_norm_g[i]), w_out_f, str(i))
            dres = send([("gdn_w_in", gr["w_in"]), ("gdn_w_out", gr["w_out"].reshape(N_DEV, D // N_DEV, D))],
                        i, f"gdn_{i}", dres)
            for n in ("conv_w", "A_log", "dt_bias", "norm_g"):
                gsmall["gdn_" + n][i] = gr[n]
        gsmall["mix_norm"][l] = gr["norm"]
        dhn, dwin, dwout = ffn_bwd(rec["hn1"], rec["a1"], rec["b1"], dres, *rec["w1"], l, "1")
        dhn = send([("ffn1_w_in", dwin.reshape(N_DEV, FFN_SHARD, D)),
                    ("ffn1_w_out", dwout.reshape(N_DEV, FFN_SHARD // 2, D))], l, f"ffn1_{l}", dhn)
        dres, dg = norm_bwd(rec["h1"], row(ffn1_norm[l]), dhn, dres, f"ffn1_norm_bwd_{l}")
        gsmall["ffn1_norm"][l] = dg
    grad_x = dres[None]

    n_repl_rows = 136
    small_rows = 576
    repl_flat = jnp.concatenate([jnp.concatenate([a.reshape(-1) for a in gsmall[n]]) for n in REPL[:-1]]
                                + [d_final.reshape(-1), loss_acc[0, 0:1]])
    loss_at = repl_flat.shape[0] - 1
    repl_pack = jnp.pad(repl_flat, (0, n_repl_rows * 128 - repl_flat.shape[0]))
    small_pack = jnp.concatenate([repl_pack] + [a.reshape(-1) for a in gsmall["hyb_dw_w"]]
                                 + [a.reshape(-1) for a in gsmall["gdn_conv_w"]]).reshape(small_rows, 128)

    sent = []

    def wait_for(pending, names, after, name):
        groups = [[("scatter", layer, s, names.index(n)) for n, layer, s in named] for named, _ in pending]
        sems = [s for _, pair in pending for s in pair]
        srcs_out, lands_out = push_wait(groups, [ge_land[n] for n in names], sems, after, name)
        flat_named = [it for named, _ in pending for it in named]
        sent.extend((n, layer, s) for (n, layer, _), s in zip(flat_named, srcs_out))
        for n, land in zip(names, lands_out):
            ge_land[n] = land

    def with_own(names, name):
        mine = [(n, layer, s) for n, layer, s in sent if n in names]
        lands = place_own([s for _, _, s in mine], [ge_land[n] for n in names],
                          [(names.index(n), layer) for n, layer, _ in mine], name)
        return dict(zip(names, lands))

    out = {}
    last = ["ffn1_w_in", "ffn1_w_out"]
    early = [n for n in big if n not in last]
    wait_for(ge_pending[:-1], big, dres, "grad_wait_a")
    for n, parts in with_own(early, "own_slabs_a").items():
        out[n] = _adamw_nd(parts, w[n], mom[n], var[n], f"adamw_{n}")
    pin = sum(out[n][1].reshape(-1)[0] for n in early) * 0.0
    small_all, = all_gather([small_pack + pin], "small_grads_all_gather")
    g_small = sum_slabs(small_all, "small_grads_sum")
    loss = g_small.reshape(-1)[loss_at]
    wait_for(ge_pending[-1:], last, g_small, "grad_wait_b")
    for n, parts in with_own(last, "own_slabs_b").items():
        out[n] = _adamw_nd(parts, w[n], mom[n], var[n], f"adamw_{n}")

    pk = lambda d: _pack([d[n] for n in REPL], n_repl_rows)
    res = adamw(g_small[:n_repl_rows][None], pk(w), pk(mom), pk(var), "adamw_replicated")
    off = 0
    for n in REPL:
        sz = w[n].size
        out[n] = [r.reshape(-1)[off:off + sz].reshape(w[n].shape) for r in res]
        off += sz
    g_dw = g_small[n_repl_rows:n_repl_rows + 248].reshape(2, CONV_K, CONV_C)
    g_dw = lax.dynamic_slice_in_dim(g_dw, me * (CONV_C // N_DEV), CONV_C // N_DEV, axis=2)
    out["hyb_dw_w"] = _adamw_nd(g_dw[None], w["hyb_dw_w"], mom["hyb_dw_w"], var["hyb_dw_w"], "adamw_hyb_dw_w")
    g_cw = g_small[n_repl_rows + 248:].reshape(2, G_CONV, G_QKV)
    g_cw = lax.dynamic_slice_in_dim(g_cw, me * (G_QKV // N_DEV), G_QKV // N_DEV, axis=2)
    out["gdn_conv_w"] = _adamw_nd(g_cw[None], w["gdn_conv_w"], mom["gdn_conv_w"], var["gdn_conv_w"], "adamw_gdn_conv_w")

    for n in FFN_IN:
        out[n] = [jnp.swapaxes(o, 1, 2) for o in out[n]]
    return (loss, grad_x, *[out[n][0] for n in WEIGHTS], *[out[n][1] for n in WEIGHTS],
            *[out[n][2] for n in WEIGHTS], *[out[n][3] for n in WEIGHTS])
```

```python
import functools

import jax
import jax.numpy as jnp
from jax import lax
from jax.experimental import pallas as pl
from jax.experimental.pallas import tpu as pltpu

F32 = jnp.float32
BF16 = jnp.bfloat16

N_DEV = 8
T = 4096
D = 1024
DEPTH = 4
FFN = 2816
FFN_SHARD = 2 * FFN // N_DEV
FFN_TILES = FFN // FFN_SHARD
EPS = 1e-6

A_HEADS = 8
A_HD = 64
A_W = 512
CONV_C = 512
CONV_K = 31
HYB_IN = 2560
ROPE_THETA = 500000.0
ROT = 16
DILATIONS = (1, 4, 16)
BLK = 128
KPAD = 2048

G_HEADS = 8
G_DK = 128
G_QKV = 3072
G_IN = 4112
G_CONV = 4
CH = 64
GRP = 512
CPG = GRP // CH
N_GRP = T // GRP

ADAM_LR = 0.001
ADAM_B1 = 0.9
ADAM_B2 = 0.999
ADAM_EPS = 1e-08
ADAM_WD = 0.01
ADAM_STEP = 10

VMEM_LIMIT = 56 * 1024 * 1024


def _cp(*sem):
    return pltpu.CompilerParams(dimension_semantics=sem, vmem_limit_bytes=VMEM_LIMIT)


def _dot(a, b):
    return jnp.dot(a, b, preferred_element_type=F32)


def _dot_nt(a, b):
    return lax.dot_general(a, b, (((1,), (1,)), ((), ())), preferred_element_type=F32)


def _dot_tn(a, b):
    return lax.dot_general(a, b, (((0,), (0,)), ((), ())), preferred_element_type=F32)


def _sigmoid(x):
    return 1.0 / (1.0 + jnp.exp(-x))


def _dsilu(x, sig):
    return sig * (1.0 + x * (1.0 - sig))


def _rms(x, g):
    rstd = lax.rsqrt(jnp.mean(x * x, axis=-1, keepdims=True) + EPS)
    return x * rstd * g


FFN_TT = 512


def ffn_fwd(h, g_row, w_in, w_out, layer, tag=""):
    def body(h_ref, g_ref, win_ref, wout_ref, hnew_ref, hn_ref, a_ref, b_ref):
        x = h_ref[...]
        hn = _rms(x, g_ref[...]).astype(BF16)
        hn_ref[...] = hn
        acc = None
        for j in range(FFN_TILES):
            a = _dot_nt(hn, win_ref[0, j])
            b = _dot_nt(hn, win_ref[1, j])
            act = a * _sigmoid(a) * b
            a_ref[j] = a.astype(BF16)
            b_ref[j] = b.astype(BF16)
            part = _dot(act.astype(BF16), wout_ref[2 * j:2 * j + 2].reshape(FFN_SHARD, D))
            acc = part if acc is None else acc + part
        hnew_ref[...] = x + 0.5 * acc

    tt = FFN_TT
    resident = pl.Buffered(1)
    return pl.pallas_call(
        body,
        grid=(T // tt,),
        in_specs=[
            pl.BlockSpec((tt, D), lambda i: (i, 0)),
            pl.BlockSpec((1, D), lambda i: (0, 0)),
            pl.BlockSpec((2, FFN_TILES, FFN_SHARD, D), lambda i: (0, 0, 0, 0), pipeline_mode=resident),
            pl.BlockSpec((N_DEV, FFN_SHARD // 2, D), lambda i: (0, 0, 0), pipeline_mode=resident),
        ],
        out_specs=[
            pl.BlockSpec((tt, D), lambda i: (i, 0)),
            pl.BlockSpec((tt, D), lambda i: (i, 0)),
            pl.BlockSpec((FFN_TILES, tt, FFN_SHARD), lambda i: (0, i, 0)),
            pl.BlockSpec((FFN_TILES, tt, FFN_SHARD), lambda i: (0, i, 0)),
        ],
        out_shape=[
            jax.ShapeDtypeStruct((T, D), F32),
            jax.ShapeDtypeStruct((T, D), BF16),
            jax.ShapeDtypeStruct((FFN_TILES, T, FFN_SHARD), BF16),
            jax.ShapeDtypeStruct((FFN_TILES, T, FFN_SHARD), BF16),
        ],
        compiler_params=_cp("parallel"),
        name=f"ffn{tag}_fwd_{layer}",
    )(h, g_row, w_in, w_out)


def ffn_bwd(hn, a, b, dres, w_in, w_out, layer, tag=""):
    tt = FFN_TT
    nt = T // tt

    def body(hn_ref, a_ref, b_ref, dres_ref, win_ref, wout_ref, dhn_ref, dwin_ref, dwout_ref, gin_ref, gout_ref,
             do_s, act_s, da_s, db_s):
        i = pl.program_id(1)
        wo = wout_ref[...].reshape(FFN_SHARD, D)
        half = tt // 2
        for r0 in (0, half):
            rows = slice(r0, r0 + half)
            do_h = (0.5 * dres_ref[rows, :]).astype(BF16)
            do_s[rows, :] = do_h
            dact = _dot_nt(do_h, wo)
            a = a_ref[rows, :].astype(F32)
            b = b_ref[rows, :].astype(F32)
            sig = _sigmoid(a)
            s = a * sig
            da_h = (dact * b * _dsilu(a, sig)).astype(BF16)
            db_h = (dact * s).astype(BF16)
            act_s[rows, :] = (s * b).astype(BF16)
            da_s[rows, :] = da_h
            db_s[rows, :] = db_h
            dhn_ref[rows, :] = (_dot(da_h, win_ref[0]) + _dot(db_h, win_ref[1])).astype(BF16)
        do, act, da, db = do_s[...], act_s[...], da_s[...], db_s[...]
        hn = hn_ref[...]
        gwo = _dot_tn(act, do)
        gwg = _dot_tn(da, hn)
        gwu = _dot_tn(db, hn)

        @pl.when(i == 0)
        def _():
            gout_ref[...] = gwo
            gin_ref[0] = gwg
            gin_ref[1] = gwu

        @pl.when(i > 0)
        def _():
            gout_ref[...] += gwo
            gin_ref[0] += gwg
            gin_ref[1] += gwu

        @pl.when(i == nt - 1)
        def _():
            dwin_ref[...] = gin_ref[...].astype(BF16)
            dwout_ref[...] = gout_ref[...].astype(BF16)

    return pl.pallas_call(
        body,
        grid=(FFN_TILES, nt),
        in_specs=[
            pl.BlockSpec((tt, D), lambda j, i: (i, 0)),
            pl.BlockSpec((None, tt, FFN_SHARD), lambda j, i: (j, i, 0)),
            pl.BlockSpec((None, tt, FFN_SHARD), lambda j, i: (j, i, 0)),
            pl.BlockSpec((tt, D), lambda j, i: (i, 0)),
            pl.BlockSpec((2, None, FFN_SHARD, D), lambda j, i: (0, j, 0, 0)),
            pl.BlockSpec((2, FFN_SHARD // 2, D), lambda j, i: (j, 0, 0)),
        ],
        out_specs=[
            pl.BlockSpec((None, tt, D), lambda j, i: (j, i, 0)),
            pl.BlockSpec((2, None, FFN_SHARD, D), lambda j, i: (0, j, 0, 0)),
            pl.BlockSpec((None, FFN_SHARD, D), lambda j, i: (j, 0, 0)),
        ],
        out_shape=[
            jax.ShapeDtypeStruct((FFN_TILES, T, D), BF16),
            jax.ShapeDtypeStruct((2, FFN_TILES, FFN_SHARD, D), BF16),
            jax.ShapeDtypeStruct((FFN_TILES, FFN_SHARD, D), BF16),
        ],
        scratch_shapes=[pltpu.VMEM((2, FFN_SHARD, D), F32), pltpu.VMEM((FFN_SHARD, D), F32),
                        pltpu.VMEM((tt, D), BF16),
                        pltpu.VMEM((tt, FFN_SHARD), BF16), pltpu.VMEM((tt, FFN_SHARD), BF16),
                        pltpu.VMEM((tt, FFN_SHARD), BF16)],
        compiler_params=_cp("parallel", "arbitrary"),
        name=f"ffn{tag}_bwd_{layer}",
    )(hn, a, b, dres, w_in, w_out)


def _rms_bwd(x, g, dy):
    rstd = lax.rsqrt(jnp.mean(x * x, axis=-1, keepdims=True) + EPS)
    xh = x * rstd
    u = dy * g
    dx = rstd * (u - xh * jnp.mean(u * xh, axis=-1, keepdims=True))
    return dx, jnp.sum(dy * xh, axis=0, keepdims=True)


def norm_bwd(x, g_row, dy_parts, dres, name):
    p = dy_parts.shape[0]
    tt = 512

    def body(x_ref, g_ref, dy_ref, dres_ref, out_ref, dg_ref):
        i = pl.program_id(0)
        dy = dy_ref[0].astype(F32)
        for q in range(1, p):
            dy = dy + dy_ref[q].astype(F32)
        dx, dg = _rms_bwd(x_ref[...], g_ref[...], dy)
        out_ref[...] = dres_ref[...] + dx

        @pl.when(i == 0)
        def _():
            dg_ref[...] = dg

        @pl.when(i > 0)
        def _():
            dg_ref[...] += dg

    return pl.pallas_call(
        body,
        grid=(T // tt,),
        in_specs=[
            pl.BlockSpec((tt, D), lambda i: (i, 0)),
            pl.BlockSpec((1, D), lambda i: (0, 0)),
            pl.BlockSpec((p, tt, D), lambda i: (0, i, 0)),
            pl.BlockSpec((tt, D), lambda i: (i, 0)),
        ],
        out_specs=[pl.BlockSpec((tt, D), lambda i: (i, 0)), pl.BlockSpec((1, D), lambda i: (0, 0))],
        out_shape=[jax.ShapeDtypeStruct((T, D), F32), jax.ShapeDtypeStruct((1, D), F32)],
        compiler_params=_cp("arbitrary"),
        name=name,
    )(x, g_row, dy_parts, dres)


def final_loss(h, g_row, target):
    tt = 512

    def body(h_ref, g_ref, t_ref, dres_ref, dg_ref, loss_ref):
        i = pl.program_id(0)
        x = h_ref[...]
        g = g_ref[...]
        err = _rms(x, g) - t_ref[...]
        part = 0.5 * jnp.sum(jnp.mean(err * err, axis=-1, keepdims=True), axis=0, keepdims=True)
        dx, dg = _rms_bwd(x, g, err * (1.0 / D))
        dres_ref[...] = dx
        part = jnp.broadcast_to(part, loss_ref.shape)

        @pl.when(i == 0)
        def _():
            dg_ref[...] = dg
            loss_ref[...] = part

        @pl.when(i > 0)
        def _():
            dg_ref[...] += dg
            loss_ref[...] += part

    return pl.pallas_call(
        body,
        grid=(T // tt,),
        in_specs=[
            pl.BlockSpec((tt, D), lambda i: (i, 0)),
            pl.BlockSpec((1, D), lambda i: (0, 0)),
            pl.BlockSpec((tt, D), lambda i: (i, 0)),
        ],
        out_specs=[
            pl.BlockSpec((tt, D), lambda i: (i, 0)),
            pl.BlockSpec((1, D), lambda i: (0, 0)),
            pl.BlockSpec((8, 128), lambda i: (0, 0)),
        ],
        out_shape=[
            jax.ShapeDtypeStruct((T, D), F32),
            jax.ShapeDtypeStruct((1, D), F32),
            jax.ShapeDtypeStruct((8, 128), F32),
        ],
        compiler_params=_cp("arbitrary"),
        name="final_loss",
    )(h, g_row, target)


PROJ_TT = 256


def rope_tables(pos_col, invf_row):
    tt = 512

    def body(p_ref, f_ref, c_ref, sm_ref, sp_ref):
        ang = p_ref[...].astype(F32) * f_ref[...]
        lane = lax.broadcasted_iota(jnp.int32, ang.shape, 1) % A_HD
        cs = jnp.cos(ang)
        sn = jnp.sin(ang)
        c_ref[...] = jnp.where(lane < ROT, cs, 1.0)
        sm_ref[...] = jnp.where(lane < ROT // 2, -sn, 0.0)
        sp_ref[...] = jnp.where((lane >= ROT // 2) & (lane < ROT), sn, 0.0)

    spec = pl.BlockSpec((tt, 128), lambda i: (i, 0))
    return pl.pallas_call(
        body,
        grid=(T // tt,),
        in_specs=[pl.BlockSpec((tt, 1), lambda i: (i, 0)), pl.BlockSpec((1, 128), lambda i: (0, 0))],
        out_specs=[spec, spec, spec],
        out_shape=[jax.ShapeDtypeStruct((T, 128), F32)] * 3,
        compiler_params=_cp("parallel"),
        name="rope_tables",
    )(pos_col, invf_row)


def make_rope(positions):
    inv_freq = jnp.power(jnp.float32(ROPE_THETA), -jnp.arange(0, ROT, 2, dtype=F32) / ROT)
    per_head = jnp.concatenate([inv_freq, inv_freq, jnp.zeros((A_HD - ROT,), F32)])
    invf_row = jnp.tile(per_head, 2)[None, :]
    return tuple(rope_tables(positions.reshape(T, 1), invf_row))


def _rope(x, c, sm, sp):
    return x * c + pltpu.roll(x, 128 - ROT // 2, 1) * sm + pltpu.roll(x, ROT // 2, 1) * sp


def _rope_t(dy, c, sm, sp):
    return dy * c + pltpu.roll(dy * sm, ROT // 2, 1) + pltpu.roll(dy * sp, 128 - ROT // 2, 1)


def proj_fwd(h, g_row, w, splits, name, rope=None):
    tt = PROJ_TT
    n = w.shape[1]
    n_rope = 0 if rope is None else 3

    def body(h_ref, g_ref, w_ref, *rest):
        tabs = rest[:n_rope]
        hn_ref = rest[n_rope]
        outs = rest[n_rope + 1:]
        hn = _rms(h_ref[...], g_ref[...]).astype(BF16)
        hn_ref[...] = hn
        for k, ((st, wd), o_ref) in enumerate(zip(splits, outs)):
            if rope is not None and k == 0:
                c, sm, sp = (t[...] for t in tabs)
                for gi in range(wd // 128):
                    r = _dot(hn, w_ref[:, st + 128 * gi:st + 128 * (gi + 1)])
                    if gi < 2 * A_W // 128:
                        r = _rope(r, c, sm, sp)
                    o_ref[:, 128 * gi:128 * (gi + 1)] = r
            else:
                o_ref[...] = _dot(hn, w_ref[:, st:st + wd])

    tab_specs = [pl.BlockSpec((tt, 128), lambda i: (i, 0))] * n_rope
    return pl.pallas_call(
        body,
        grid=(T // tt,),
        in_specs=[
            pl.BlockSpec((tt, D), lambda i: (i, 0)),
            pl.BlockSpec((1, D), lambda i: (0, 0)),
            pl.BlockSpec((D, n), lambda i: (0, 0)),
        ] + tab_specs,
        out_specs=[pl.BlockSpec((tt, D), lambda i: (i, 0))]
        + [pl.BlockSpec((tt, wd), lambda i: (i, 0)) for _, wd in splits],
        out_shape=[jax.ShapeDtypeStruct((T, D), BF16)]
        + [jax.ShapeDtypeStruct((T, wd), F32) for _, wd in splits],
        compiler_params=_cp("parallel"),
        name=name,
    )(h, g_row, w, *(rope or ()))


def proj_bwd_data(x, g_row, w, dparts, splits, dres, name, rope=None, n_rot=0):
    tt = PROJ_TT
    n = w.shape[1]
    n_rope = 0 if rope is None else 3
    k_parts = len(dparts)
    n_main = sum(wd for _, wd in splits if wd % 128 == 0)

    def body(x_ref, g_ref, w_ref, dres_ref, *rest):
        d_refs = rest[:k_parts]
        tabs = rest[k_parts:k_parts + n_rope]
        out_ref, dg_ref, dproj_ref = rest[k_parts + n_rope:k_parts + n_rope + 3]
        i = pl.program_id(0)
        dhn = jnp.zeros((tt, D), F32)
        for k, ((st, wd), d_ref) in enumerate(zip(splits, d_refs)):
            if k < n_rot:
                c, sm, sp = (t[...] for t in tabs)
                for gi in range(wd // 128):
                    cols = slice(st + 128 * gi, st + 128 * (gi + 1))
                    d = _rope_t(d_ref[:, 128 * gi:128 * (gi + 1)], c, sm, sp).astype(BF16)
                    dproj_ref[:, cols] = d
                    dhn = dhn + _dot_nt(d, w_ref[:, cols])
            else:
                d = d_ref[...].astype(BF16)
                if wd % 128 == 0:
                    dproj_ref[:, st:st + wd] = d
                dhn = dhn + _dot_nt(d, w_ref[:, st:st + wd])
        dx, dg = _rms_bwd(x_ref[...], g_ref[...], dhn)
        out_ref[...] = dres_ref[...] + dx

        @pl.when(i == 0)
        def _():
            dg_ref[...] = dg

        @pl.when(i > 0)
        def _():
            dg_ref[...] += dg

    tab_specs = [pl.BlockSpec((tt, 128), lambda i: (i, 0))] * n_rope
    out_specs = [pl.BlockSpec((tt, D), lambda i: (i, 0)), pl.BlockSpec((1, D), lambda i: (0, 0))]
    out_shape = [jax.ShapeDtypeStruct((T, D), F32), jax.ShapeDtypeStruct((1, D), F32)]
    out_specs.append(pl.BlockSpec((tt, n_main), lambda i: (i, 0)))
    out_shape.append(jax.ShapeDtypeStruct((T, n_main), BF16))
    return pl.pallas_call(
        body,
        grid=(T // tt,),
        in_specs=[
            pl.BlockSpec((tt, D), lambda i: (i, 0)),
            pl.BlockSpec((1, D), lambda i: (0, 0)),
            pl.BlockSpec((D, n), lambda i: (0, 0)),
            pl.BlockSpec((tt, D), lambda i: (i, 0)),
        ] + [pl.BlockSpec((tt, wd), lambda i: (i, 0)) for _, wd in splits] + tab_specs,
        out_specs=out_specs,
        out_shape=out_shape,
        compiler_params=_cp("arbitrary"),
        name=name,
    )(x, g_row, w, dres, *dparts, *(rope or ()))


def mm_tn_full(x, d, out_dtype, name):
    k = x.shape[1]
    n = d.shape[1]
    wn = 512

    def body(x_ref, d_ref, o_ref):
        o_ref[...] = _dot_tn(x_ref[...], d_ref[...].astype(BF16)).astype(out_dtype)

    return pl.pallas_call(
        body,
        grid=(n // wn,),
        in_specs=[pl.BlockSpec((T, k), lambda j: (0, 0), pipeline_mode=pl.Buffered(1)),
                  pl.BlockSpec((T, wn), lambda j: (0, j))],
        out_specs=pl.BlockSpec((k, wn), lambda j: (0, j)),
        out_shape=jax.ShapeDtypeStruct((k, n), out_dtype),
        compiler_params=_cp("parallel"),
        name=name,
    )(x, d)


def cols_to_slabs(main, tail, name):
    nm = main.shape[1]
    n = nm + (0 if tail is None else tail.shape[1])
    w = n // N_DEV
    tr = 256

    def body(*refs):
        m_ref, o_ref = refs[0], refs[-1]
        for s in range(N_DEV):
            a, b = w * s, w * (s + 1)
            if b <= nm:
                o_ref[s] = m_ref[:, a:b].astype(BF16)
            else:
                o_ref[s, :, 0:nm - a] = m_ref[:, a:nm].astype(BF16)
                o_ref[s, :, nm - a:w] = refs[1][:, 0:b - nm].astype(BF16)

    arrays = [main] + ([] if tail is None else [tail])
    return pl.pallas_call(
        body,
        grid=(D // tr,),
        in_specs=[pl.BlockSpec((tr, a.shape[1]), lambda i: (i, 0)) for a in arrays],
        out_specs=pl.BlockSpec((N_DEV, tr, w), lambda i: (0, i, 0)),
        out_shape=jax.ShapeDtypeStruct((N_DEV, D, w), BF16),
        compiler_params=_cp("parallel"),
        name=name,
    )(*arrays)


def mm_tn(x, d, name):
    k = x.shape[1]
    n = d.shape[1]
    wn = n if n <= 512 else 512
    tt = 512

    def body(x_ref, d_ref, o_ref):
        i = pl.program_id(1)
        r = _dot_tn(x_ref[...], d_ref[...].astype(BF16))

        @pl.when(i == 0)
        def _():
            o_ref[...] = r

        @pl.when(i > 0)
        def _():
            o_ref[...] += r

    return pl.pallas_call(
        body,
        grid=(n // wn, T // tt),
        in_specs=[pl.BlockSpec((tt, k), lambda j, i: (i, 0)), pl.BlockSpec((tt, wn), lambda j, i: (i, j))],
        out_specs=pl.BlockSpec((k, wn), lambda j, i: (0, j)),
        out_shape=jax.ShapeDtypeStruct((k, n), F32),
        compiler_params=_cp("parallel", "arbitrary"),
        name=name,
    )(x, d)


CONV_RC = 128
CONV_PAD = 32


def hyb_conv_fwd(u, dw_w, dw_b, name):
    def body(ua_ref, ug_ref, w_ref, b_ref, o_ref, xpad):
        xpad[0:CONV_PAD, :] = jnp.zeros((CONV_PAD, 128), F32)
        xpad[CONV_PAD:, :] = ua_ref[...] * _sigmoid(ug_ref[...])
        for r in range(T // CONV_RC):
            acc = jnp.broadcast_to(b_ref[...], (CONV_RC, 128))
            for j in range(CONV_K):
                acc = acc + w_ref[pl.ds(j, 1), :] * xpad[pl.ds(r * CONV_RC + CONV_PAD - (CONV_K - 1) + j, CONV_RC), :]
            o_ref[r * CONV_RC:(r + 1) * CONV_RC, :] = acc

    nb = CONV_C // 128
    return pl.pallas_call(
        body,
        grid=(nb,),
        in_specs=[
            pl.BlockSpec((T, 128), lambda c: (0, c)),
            pl.BlockSpec((T, 128), lambda c: (0, nb + c)),
            pl.BlockSpec((32, 128), lambda c: (0, c)),
            pl.BlockSpec((1, 128), lambda c: (0, c)),
        ],
        out_specs=pl.BlockSpec((T, 128), lambda c: (0, c)),
        out_shape=jax.ShapeDtypeStruct((T, CONV_C), F32),
        scratch_shapes=[pltpu.VMEM((T + CONV_PAD, 128), F32)],
        compiler_params=_cp("parallel"),
        name=name,
    )(u, u, dw_w, dw_b)


def hyb_conv_bwd(dc, u, dw_w, name):
    def body(dc_ref, ua_ref, ug_ref, w_ref, da_ref, dgate_ref, dw_ref, db_ref, xpad, dcpad, dwacc):
        ua = ua_ref[...]
        sig = _sigmoid(ug_ref[...])
        xpad[0:CONV_PAD, :] = jnp.zeros((CONV_PAD, 128), F32)
        xpad[CONV_PAD:, :] = ua * sig
        dcpad[0:T, :] = dc_ref[...]
        dcpad[T:, :] = jnp.zeros((CONV_PAD, 128), F32)
        dwacc[...] = jnp.zeros_like(dwacc)
        dbacc = jnp.zeros((8, 128), F32)
        for r in range(T // CONV_RC):
            r0 = r * CONV_RC
            dcr = dc_ref[r0:r0 + CONV_RC, :]
            dbacc = dbacc + dcr.reshape(CONV_RC // 8, 8, 128).sum(axis=0)
            dglu = jnp.zeros((CONV_RC, 128), F32)
            for j in range(CONV_K):
                dglu = dglu + w_ref[pl.ds(j, 1), :] * dcpad[pl.ds(r0 + (CONV_K - 1) - j, CONV_RC), :]
                prod = dcr * xpad[pl.ds(r0 + CONV_PAD - (CONV_K - 1) + j, CONV_RC), :]
                dwacc[8 * j:8 * j + 8, :] += prod.reshape(CONV_RC // 8, 8, 128).sum(axis=0)
            sg = sig[r0:r0 + CONV_RC, :]
            da_ref[r0:r0 + CONV_RC, :] = dglu * sg
            dgate_ref[r0:r0 + CONV_RC, :] = dglu * ua[r0:r0 + CONV_RC, :] * sg * (1.0 - sg)
        for j in range(CONV_K):
            dw_ref[pl.ds(j, 1), :] = jnp.sum(dwacc[8 * j:8 * j + 8, :], axis=0, keepdims=True)
        dw_ref[pl.ds(CONV_K, 1), :] = jnp.zeros((1, 128), F32)
        db_ref[...] = jnp.sum(dbacc, axis=0, keepdims=True)

    nb = CONV_C // 128
    col = pl.BlockSpec((T, 128), lambda c: (0, c))
    return pl.pallas_call(
        body,
        grid=(nb,),
        in_specs=[col, col, pl.BlockSpec((T, 128), lambda c: (0, nb + c)), pl.BlockSpec((32, 128), lambda c: (0, c))],
        out_specs=[col, col, pl.BlockSpec((32, 128), lambda c: (0, c)), pl.BlockSpec((1, 128), lambda c: (0, c))],
        out_shape=[
            jax.ShapeDtypeStruct((T, CONV_C), F32),
            jax.ShapeDtypeStruct((T, CONV_C), F32),
            jax.ShapeDtypeStruct((32, CONV_C), F32),
            jax.ShapeDtypeStruct((1, CONV_C), F32),
        ],
        scratch_shapes=[
            pltpu.VMEM((T + CONV_PAD, 128), F32),
            pltpu.VMEM((T + CONV_PAD, 128), F32),
            pltpu.VMEM((8 * 32, 128), F32),
        ],
        compiler_params=_cp("parallel"),
        name=name,
    )(dc, u, u, dw_w)


ATT_SCALE = A_HD ** -0.5
N_BLK = T // BLK


def _att_masks():
    i = lax.broadcasted_iota(jnp.int32, (BLK, 2 * BLK), 0)
    j = lax.broadcasted_iota(jnp.int32, (BLK, 2 * BLK), 1)
    band = (j >= i) & (j <= i + BLK)
    i1 = lax.broadcasted_iota(jnp.int32, (BLK, BLK), 0)
    j1 = lax.broadcasted_iota(jnp.int32, (BLK, BLK), 1)
    return band, j1 <= i1


def _att_rows(d, t, first):
    if first:
        base = t
        return pl.ds(base, BLK, stride=d), pl.ds(base, BLK, stride=d)
    c = t % d
    n = t // d + 1
    base = c + (BLK * d) * n
    return pl.ds(base, BLK, stride=d), pl.ds(base - BLK * d, 2 * BLK, stride=d)


def _stack_heads(x, head0):
    return jnp.concatenate([jnp.where(head0, x, 0.0), jnp.where(head0, 0.0, x)], axis=0)


def _loop_pairs(n, block, per=2):
    def several(i, carry):
        for k in range(per):
            block(per * i + k, carry)
        return carry

    if n >= per:
        lax.fori_loop(0, n // per, several, 0)
    for t in range(n - n % per, n):
        block(t, 0)


def attn_fwd(qkv, name):
    def body(q_ref, k_ref, v_ref, o_ref, lse_ref, og, lg):
        band, tri = _att_masks()
        band, tri = jnp.concatenate([band, band], axis=0), jnp.concatenate([tri, tri], axis=0)
        head0 = lax.broadcasted_iota(jnp.int32, (BLK, 128), 1) < A_HD
        for g, d in enumerate(DILATIONS):
            def block(t, carry, first, g=g, d=d):
                rq, rk = _att_rows(d, t, first)
                q2 = q_ref[rq, :]
                k2 = k_ref[rk, :].astype(BF16)
                v2 = v_ref[rk, :].astype(BF16)
                qs = _stack_heads(q2, head0).astype(BF16)
                s = _dot_nt(qs, k2) * ATT_SCALE
                s = jnp.where(tri if first else band, s, -jnp.inf)
                m = jnp.max(s, axis=-1, keepdims=True)
                p = jnp.exp(s - m)
                den = jnp.sum(p, axis=-1, keepdims=True)
                o = _dot(p.astype(BF16), v2) / den
                l = m + jnp.log(den)
                og[g, rq, :] = jnp.where(head0, o[0:BLK], o[BLK:2 * BLK])
                lg[g, rq, :] = jnp.where(head0, l[0:BLK], l[BLK:2 * BLK])
                return carry

            _loop_pairs(d, functools.partial(block, first=True), per=4)
            _loop_pairs(N_BLK - d, functools.partial(block, first=False), per=4)
        rc = 256
        for r in range(T // rc):
            rows = pl.ds(r * rc, rc)
            l0, l1, l2 = lg[0, rows, :], lg[1, rows, :], lg[2, rows, :]
            m = jnp.maximum(jnp.maximum(l0, l1), l2)
            e0, e1, e2 = jnp.exp(l0 - m), jnp.exp(l1 - m), jnp.exp(l2 - m)
            z = e0 + e1 + e2
            o_ref[rows, :] = (e0 / z) * og[0, rows, :] + (e1 / z) * og[1, rows, :] + (e2 / z) * og[2, rows, :]
            lse_ref[rows, :] = m + jnp.log(z)

    npair = A_HEADS // 2
    col = lambda off: pl.BlockSpec((T, 128), lambda p: (0, off + p))
    return pl.pallas_call(
        body,
        grid=(npair,),
        in_specs=[col(0), col(npair), col(2 * npair)],
        out_specs=[col(0), col(0)],
        out_shape=[jax.ShapeDtypeStruct((T, A_W), F32), jax.ShapeDtypeStruct((T, A_W), F32)],
        scratch_shapes=[pltpu.VMEM((3, T, 128), F32), pltpu.VMEM((3, T, 128), F32)],
        compiler_params=_cp("parallel"),
        name=name,
    )(qkv, qkv, qkv)


def attn_bwd(qkv, o, lse, do, name):
    def body(q_ref, k_ref, v_ref, o_ref, lse_ref, do_ref, dq_ref, dk_ref, dv_ref):
        band, tri = _att_masks()
        band, tri = jnp.concatenate([band, band], axis=0), jnp.concatenate([tri, tri], axis=0)
        head0 = lax.broadcasted_iota(jnp.int32, (BLK, 128), 1) < A_HD
        dq_ref[...] = jnp.zeros_like(dq_ref)
        dk_ref[...] = jnp.zeros_like(dk_ref)
        dv_ref[...] = jnp.zeros_like(dv_ref)
        for d in DILATIONS:
            def block(t, carry, first, d=d):
                rq, rk = _att_rows(d, t, first)
                k2 = k_ref[rk, :].astype(BF16)
                v2 = v_ref[rk, :].astype(BF16)
                do2 = do_ref[rq, :]
                l2 = lse_ref[rq, :]
                qs = _stack_heads(q_ref[rq, :], head0).astype(BF16)
                dos = _stack_heads(do2, head0).astype(BF16)
                l = jnp.concatenate([l2[:, 0:1], l2[:, A_HD:A_HD + 1]], axis=0)
                dd = jnp.sum(_stack_heads(do2 * o_ref[rq, :], head0), axis=-1, keepdims=True)
                s = _dot_nt(qs, k2) * ATT_SCALE
                p = jnp.where(tri if first else band, jnp.exp(s - l), 0.0)
                dp = _dot_nt(dos, v2)
                ds = (p * (dp - dd) * ATT_SCALE).astype(BF16)
                dq = _dot(ds, k2)
                dq_ref[rq, :] += jnp.where(head0, dq[0:BLK], dq[BLK:2 * BLK])
                dk_ref[rk, :] += _dot_tn(ds, qs)
                dv_ref[rk, :] += _dot_tn(p.astype(BF16), dos)
                return carry

            _loop_pairs(d, functools.partial(block, first=True), per=4)
            _loop_pairs(N_BLK - d, functools.partial(block, first=False), per=4)

    npair = A_HEADS // 2
    col = lambda off: pl.BlockSpec((T, 128), lambda p: (0, off + p))
    return pl.pallas_call(
        body,
        grid=(npair,),
        in_specs=[col(0), col(npair), col(2 * npair), col(0), col(0), col(0)],
        out_specs=[col(0), col(0), col(0)],
        out_shape=[jax.ShapeDtypeStruct((T, A_W), F32)] * 3,
        compiler_params=_cp("parallel"),
        name=name,
    )(qkv, qkv, qkv, o, lse, do)


def _ln_silu(x, g, b):
    mu = jnp.mean(x, axis=-1, keepdims=True)
    xc = x - mu
    rstd = lax.rsqrt(jnp.mean(xc * xc, axis=-1, keepdims=True) + EPS)
    xh = xc * rstd
    y = xh * g + b
    sig = _sigmoid(y)
    return y * sig, (xh, rstd, y, sig)


def hyb_out_fwd(h, attn, cpre, ln_g, ln_b, w_out, name):
    tt = 512

    def body(h_ref, a_ref, c_ref, g_ref, b_ref, w_ref, hnew_ref, cat_ref):
        cn, _ = _ln_silu(c_ref[...], g_ref[...], b_ref[...])
        ab = a_ref[...].astype(BF16)
        cb = cn.astype(BF16)
        cat_ref[:, 0:A_W] = ab
        cat_ref[:, A_W:D] = cb
        hnew_ref[...] = h_ref[...] + _dot(ab, w_ref[0:A_W, :]) + _dot(cb, w_ref[A_W:D, :])

    half = pl.BlockSpec((tt, A_W), lambda i: (i, 0))
    vec = pl.BlockSpec((1, CONV_C), lambda i: (0, 0))
    full = pl.BlockSpec((tt, D), lambda i: (i, 0))
    return pl.pallas_call(
        body,
        grid=(T // tt,),
        in_specs=[full, half, half, vec, vec, pl.BlockSpec((D, D), lambda i: (0, 0))],
        out_specs=[full, full],
        out_shape=[jax.ShapeDtypeStruct((T, D), F32), jax.ShapeDtypeStruct((T, D), BF16)],
        compiler_params=_cp("parallel"),
        name=name,
    )(h, attn, cpre, ln_g, ln_b, w_out)


def hyb_out_bwd(dres, cpre, ln_g, ln_b, w_out, name):
    tt = 512

    def body(d_ref, c_ref, g_ref, b_ref, w_ref, da_ref, dc_ref, dg_ref, db_ref):
        i = pl.program_id(0)
        db16 = d_ref[...].astype(BF16)
        da_ref[...] = _dot_nt(db16, w_ref[0:A_W, :])
        dcn = _dot_nt(db16, w_ref[A_W:D, :])
        g = g_ref[...]
        _, (xh, rstd, y, sig) = _ln_silu(c_ref[...], g, b_ref[...])
        dy = dcn * _dsilu(y, sig)
        dxh = dy * g
        dc_ref[...] = rstd * (dxh - jnp.mean(dxh, axis=-1, keepdims=True)
                              - xh * jnp.mean(dxh * xh, axis=-1, keepdims=True))
        dg = jnp.sum(dy * xh, axis=0, keepdims=True)
        db = jnp.sum(dy, axis=0, keepdims=True)

        @pl.when(i == 0)
        def _():
            dg_ref[...] = dg
            db_ref[...] = db

        @pl.when(i > 0)
        def _():
            dg_ref[...] += dg
            db_ref[...] += db

    half = pl.BlockSpec((tt, A_W), lambda i: (i, 0))
    vec = pl.BlockSpec((1, CONV_C), lambda i: (0, 0))
    return pl.pallas_call(
        body,
        grid=(T // tt,),
        in_specs=[pl.BlockSpec((tt, D), lambda i: (i, 0)), half, vec, vec, pl.BlockSpec((D, D), lambda i: (0, 0))],
        out_specs=[half, half, vec, vec],
        out_shape=[
            jax.ShapeDtypeStruct((T, A_W), F32),
            jax.ShapeDtypeStruct((T, CONV_C), F32),
            jax.ShapeDtypeStruct((1, CONV_C), F32),
            jax.ShapeDtypeStruct((1, CONV_C), F32),
        ],
        compiler_params=_cp("arbitrary"),
        name=name,
    )(dres, cpre, ln_g, ln_b, w_out)


def hybrid_fwd(h, g_row, w_in, dw_w, dw_b, ln_g, ln_b, w_out, rope, tag):
    hn, qkv, u = proj_fwd(h, g_row, w_in, [(0, 3 * A_W), (3 * A_W, 2 * CONV_C)], f"hyb_proj_{tag}", rope=rope)
    cpre = hyb_conv_fwd(u, dw_w, dw_b, f"hyb_conv_{tag}")
    attn, lse = attn_fwd(qkv, f"attn_fwd_{tag}")
    hnew, cat = hyb_out_fwd(h, attn, cpre, ln_g, ln_b, w_out, f"hyb_out_{tag}")
    return hnew, (h, hn, qkv, u, cpre, attn, lse, cat)


def hybrid_bwd(dres, saved, g_row, w_in, dw_w, ln_g, ln_b, w_out, rope, tag):
    h, hn, qkv, u, cpre, attn, lse, cat = saved
    d_attn, d_cpre, d_lng, d_lnb = hyb_out_bwd(dres, cpre, ln_g, ln_b, w_out, f"hyb_out_bwd_{tag}")
    d_wout = mm_tn_full(cat, dres, BF16, f"hyb_wout_grad_{tag}")
    d_a, d_gate, d_dw, d_db = hyb_conv_bwd(d_cpre, u, dw_w, f"hyb_conv_bwd_{tag}")
    dq, dk, dv = attn_bwd(qkv, attn, lse, d_attn, f"attn_bwd_{tag}")
    splits = [(0, A_W), (A_W, A_W), (2 * A_W, A_W), (3 * A_W, CONV_C), (3 * A_W + CONV_C, CONV_C)]
    dres_new, d_norm, dproj = proj_bwd_data(
        h, g_row, w_in, [dq, dk, dv, d_a, d_gate], splits, dres, f"hyb_proj_bwd_{tag}", rope=rope, n_rot=2)
    d_win = cols_to_slabs(mm_tn_full(hn, dproj, F32, f"hyb_win_grad_{tag}"), None, f"hyb_win_slabs_{tag}")
    return dres_new, dict(norm=d_norm, w_in=d_win, dw_w=d_dw[:CONV_K], dw_b=d_db, ln_g=d_lng, ln_b=d_lnb, w_out=d_wout)


G_SCALE = G_DK ** -0.5
GP_RC = 256
GP_PAD = 8


def gdn_prep_fwd(x, conv_w, name):
    def body(x_ref, w_ref, o_ref, xpad):
        cb = pl.program_id(0)
        xpad[0:GP_PAD, :] = jnp.zeros((GP_PAD, 128), F32)
        xpad[GP_PAD:, :] = x_ref[...]
        for r in range(T // GP_RC):
            r0 = r * GP_RC
            y = jnp.zeros((GP_RC, 128), F32)
            for j in range(G_CONV):
                y = y + w_ref[pl.ds(j, 1), :] * xpad[pl.ds(r0 + GP_PAD - (G_CONV - 1) + j, GP_RC), :]
            s = y * _sigmoid(y)
            n = lax.rsqrt(jnp.sum(s * s, axis=-1, keepdims=True) + EPS)
            o_ref[r0:r0 + GP_RC, :] = s * jnp.where(cb < 2 * G_HEADS, n, 1.0)

    nb = G_QKV // 128
    return pl.pallas_call(
        body,
        grid=(nb,),
        in_specs=[pl.BlockSpec((T, 128), lambda c: (0, c)), pl.BlockSpec((G_CONV, 128), lambda c: (0, c))],
        out_specs=pl.BlockSpec((T, 128), lambda c: (0, c)),
        out_shape=jax.ShapeDtypeStruct((T, G_QKV), F32),
        scratch_shapes=[pltpu.VMEM((T + GP_PAD, 128), F32)],
        compiler_params=_cp("parallel"),
        name=name,
    )(x, conv_w)


def gdn_prep_bwd(dout, x, conv_w, part, l2, name):
    def body(d_ref, x_ref, w_ref, dx_ref, dw_ref, xpad, dypad, dwacc):
        xpad[0:GP_PAD, :] = jnp.zeros((GP_PAD, 128), F32)
        xpad[GP_PAD:, :] = x_ref[...]
        dypad[T:, :] = jnp.zeros((GP_PAD, 128), F32)
        dwacc[...] = jnp.zeros_like(dwacc)
        for r in range(T // GP_RC):
            r0 = r * GP_RC
            y = jnp.zeros((GP_RC, 128), F32)
            xs = []
            for j in range(G_CONV):
                xj = xpad[pl.ds(r0 + GP_PAD - (G_CONV - 1) + j, GP_RC), :]
                xs.append(xj)
                y = y + w_ref[pl.ds(j, 1), :] * xj
            sig = _sigmoid(y)
            s = y * sig
            d = d_ref[r0:r0 + GP_RC, :]
            if l2:
                n = lax.rsqrt(jnp.sum(s * s, axis=-1, keepdims=True) + EPS)
                out = s * n
                d = n * (d - out * jnp.sum(d * out, axis=-1, keepdims=True))
            dy = d * _dsilu(y, sig)
            dypad[r0:r0 + GP_RC, :] = dy
            for j in range(G_CONV):
                dwacc[8 * j:8 * j + 8, :] += (dy * xs[j]).reshape(GP_RC // 8, 8, 128).sum(axis=0)
        for r in range(T // GP_RC):
            r0 = r * GP_RC
            dx = jnp.zeros((GP_RC, 128), F32)
            for j in range(G_CONV):
                dx = dx + w_ref[pl.ds(j, 1), :] * dypad[pl.ds(r0 + (G_CONV - 1) - j, GP_RC), :]
            dx_ref[r0:r0 + GP_RC, :] = dx
        for j in range(G_CONV):
            dw_ref[pl.ds(j, 1), :] = jnp.sum(dwacc[8 * j:8 * j + 8, :], axis=0, keepdims=True)

    nb = G_HEADS
    off = part * nb
    col = pl.BlockSpec((T, 128), lambda c: (0, c))
    return pl.pallas_call(
        body,
        grid=(nb,),
        in_specs=[col, pl.BlockSpec((T, 128), lambda c: (0, off + c)), pl.BlockSpec((G_CONV, 128), lambda c: (0, off + c))],
        out_specs=[col, pl.BlockSpec((G_CONV, 128), lambda c: (0, c))],
        out_shape=[jax.ShapeDtypeStruct((T, G_HEADS * G_DK), F32), jax.ShapeDtypeStruct((G_CONV, G_HEADS * G_DK), F32)],
        scratch_shapes=[
            pltpu.VMEM((T + GP_PAD, 128), F32),
            pltpu.VMEM((T + GP_PAD, 128), F32),
            pltpu.VMEM((8 * G_CONV, 128), F32),
        ],
        compiler_params=_cp("parallel"),
        name=name,
    )(dout, x, conv_w)


def _seg_cumsum(x, reverse=False):
    row = lax.broadcasted_iota(jnp.int32, x.shape, 0) % CH
    s = 1
    while s < CH:
        if reverse:
            x = x + jnp.where(row < CH - s, pltpu.roll(x, x.shape[0] - s, 0), 0.0)
        else:
            x = x + jnp.where(row >= s, pltpu.roll(x, s, 0), 0.0)
        s *= 2
    return x


def _gdn_gates(ba_ref, alog_ref, dt_ref, h):
    ba = ba_ref[...]
    lane = lax.broadcasted_iota(jnp.int32, ba.shape, 1)
    b_col = jnp.sum(jnp.where(lane == h, ba, 0.0), axis=1, keepdims=True)
    a_col = jnp.sum(jnp.where(lane == G_HEADS + h, ba, 0.0), axis=1, keepdims=True)
    lane8 = lax.broadcasted_iota(jnp.int32, (1, G_HEADS), 1)
    alog = jnp.sum(jnp.where(lane8 == h, alog_ref[...], 0.0), axis=1, keepdims=True)
    dt = jnp.sum(jnp.where(lane8 == h, dt_ref[...], 0.0), axis=1, keepdims=True)
    beta = _sigmoid(b_col)
    xa = a_col + dt
    softplus = jnp.maximum(xa, 0.0) + jnp.log(1.0 + jnp.exp(-jnp.abs(xa)))
    ea = jnp.exp(alog)
    return beta, -ea * softplus, xa, ea


def _chunk_masks():
    i = lax.broadcasted_iota(jnp.int32, (CH, CH), 0)
    j = lax.broadcasted_iota(jnp.int32, (CH, CH), 1)
    return i >= j, i > j, i, j


def _decay(gcc, causal):
    gm = gcc[:, 0:CH]
    return jnp.where(causal, jnp.exp(jnp.minimum(gm - gm.T, 0.0)), 0.0)


def _split(a):
    hi = a.astype(BF16)
    return hi, (a - hi.astype(F32)).astype(BF16)


def _dot3(a, b):
    ah, al = _split(a)
    bh, bl = _split(b)
    return _dot(ah, bh) + (_dot(ah, bl) + _dot(al, bh))


def _unit_lower_inverse(lms, i, j):
    eye = jnp.where(i == j, 1.0, 0.0)
    ms = [None] * len(lms)
    b = 1
    while b < CH:
        pair = ((i // (2 * b)) == (j // (2 * b))) & ((i // b) % 2 == 1) & ((j // b) % 2 == 0)
        lbs = [jnp.where(pair, lm, 0.0) for lm in lms]
        if b == 1:
            ms = [eye - lb for lb in lbs]
        else:
            ts = [_dot3(m, lb) for m, lb in zip(ms, lbs)]
            ms = [m - _dot3(t, m) for m, t in zip(ms, ts)]
        b *= 2
    return ms


def gdn_local_fwd(qkv, ba, alog, dtb, name):
    def body(q_ref, k_ref, v_ref, ba_ref, al_ref, dt_ref, u_ref, w_ref, qd_ref, kd_ref, at_ref, el_ref, ti_ref, gcs):
        h = pl.program_id(1)
        beta, g, _, _ = _gdn_gates(ba_ref, al_ref, dt_ref, h)
        gc = _seg_cumsum(jnp.broadcast_to(g, (GRP, 128)))
        gcs[...] = gc
        causal, strict, i, j = _chunk_masks()
        lms = []
        for c in range(CPG):
            r = slice(c * CH, (c + 1) * CH)
            q, k = q_ref[r, :], k_ref[r, :]
            gcc = gc[r, :]
            ec = jnp.exp(gcc)
            gl = gcs[pl.ds(c * CH + CH - 1, 1), :]
            dm = _decay(gcc, causal)
            kbf = k.astype(BF16)
            a1 = _dot_nt((k * beta[r, :]).astype(BF16), kbf)
            lms.append(jnp.where(strict, a1 * dm, 0.0))
            qs = q * G_SCALE
            qd_ref[r, :] = (qs * ec).astype(BF16)
            kd_ref[r, :] = (k * jnp.exp(gl - gcc)).astype(BF16)
            at_ref[r, :] = (_dot_nt(qs.astype(BF16), kbf) * dm).astype(BF16)
            el_ref[pl.ds(c, 1), :] = jnp.exp(gl)
        tinvs = _unit_lower_inverse(lms, i, j)
        for c in range(CPG):
            r = slice(c * CH, (c + 1) * CH)
            bt = beta[r, :]
            tb = tinvs[c].astype(BF16)
            u_ref[r, :] = _dot(tb, (v_ref[r, :] * bt).astype(BF16))
            w_ref[r, :] = _dot(tb, (k_ref[r, :] * bt * jnp.exp(gc[r, :])).astype(BF16)).astype(BF16)
            ti_ref[r, :] = tinvs[c]

    hd = lambda off: pl.BlockSpec((GRP, 128), lambda i, h: (i, off + h))
    vec = pl.BlockSpec((1, G_HEADS), lambda i, h: (0, 0))
    sq = pl.BlockSpec((None, GRP, CH), lambda i, h: (h, i, 0))
    return pl.pallas_call(
        body,
        grid=(N_GRP, G_HEADS),
        in_specs=[hd(0), hd(G_HEADS), hd(2 * G_HEADS), pl.BlockSpec((GRP, 2 * G_HEADS), lambda i, h: (i, 0)), vec, vec],
        out_specs=[hd(0), hd(0), hd(0), hd(0), sq, pl.BlockSpec((None, CPG, 128), lambda i, h: (h, i, 0)), sq],
        out_shape=[
            jax.ShapeDtypeStruct((T, D), F32),
            jax.ShapeDtypeStruct((T, D), BF16),
            jax.ShapeDtypeStruct((T, D), BF16),
            jax.ShapeDtypeStruct((T, D), BF16),
            jax.ShapeDtypeStruct((G_HEADS, T, CH), BF16),
            jax.ShapeDtypeStruct((G_HEADS, T // CH, 128), F32),
            jax.ShapeDtypeStruct((G_HEADS, T, CH), F32),
        ],
        scratch_shapes=[pltpu.VMEM((GRP, 128), F32)],
        compiler_params=_cp("parallel", "parallel"),
        name=name,
    )(qkv, qkv, qkv, ba, alog, dtb)


def gdn_rec_fwd(u, w, qd, kd, at, el, name):
    def body(u_ref, w_ref, qd_ref, kd_ref, at_ref, el_ref, o_ref, vn_ref, st_ref, s_scr):
        @pl.when(pl.program_id(0) == 0)
        def _():
            s_scr[...] = jnp.zeros_like(s_scr)

        states = [s_scr[h] for h in range(G_HEADS)]
        heads = range(G_HEADS)
        lns = [slice(h * 128, (h + 1) * 128) for h in heads]
        for c in range(CPG):
            r = slice(c * CH, (c + 1) * CH)
            for h in heads:
                st_ref[h, c] = states[h]
            sbs = [states[h].astype(BF16) for h in heads]
            ws = [_dot(w_ref[r, lns[h]], sbs[h]) for h in heads]
            qs = [_dot(qd_ref[r, lns[h]], sbs[h]) for h in heads]
            vns = [(u_ref[r, lns[h]] - ws[h]).astype(BF16) for h in heads]
            avs = [_dot(at_ref[h, r, :], vns[h]) for h in heads]
            kvs = [_dot_tn(kd_ref[r, lns[h]], vns[h]) for h in heads]
            for h in heads:
                o_ref[r, lns[h]] = qs[h] + avs[h]
                vn_ref[r, lns[h]] = vns[h]
                states[h] = states[h] * el_ref[h, pl.ds(c, 1), :] + kvs[h]
        for h in heads:
            s_scr[h] = states[h]

    row = pl.BlockSpec((GRP, D), lambda i: (i, 0))
    return pl.pallas_call(
        body,
        grid=(N_GRP,),
        in_specs=[row, row, row, row, pl.BlockSpec((G_HEADS, GRP, CH), lambda i: (0, i, 0)),
                  pl.BlockSpec((G_HEADS, CPG, 128), lambda i: (0, i, 0))],
        out_specs=[row, row, pl.BlockSpec((G_HEADS, CPG, 128, 128), lambda i: (0, i, 0, 0))],
        out_shape=[
            jax.ShapeDtypeStruct((T, D), F32),
            jax.ShapeDtypeStruct((T, D), BF16),
            jax.ShapeDtypeStruct((G_HEADS, T // CH, 128, 128), F32),
        ],
        scratch_shapes=[pltpu.VMEM((G_HEADS, 128, 128), F32)],
        compiler_params=_cp("arbitrary"),
        name=name,
    )(u, w, qd, kd, at, el)


def gdn_rec_bwd(do, w, qd, kd, at, el, vn, st, name):
    def body(do_ref, w_ref, qd_ref, kd_ref, at_ref, el_ref, vn_ref, st_ref,
             du_ref, dw_ref, dqd_ref, dkd_ref, dat_ref, del_ref, ds_scr):
        @pl.when(pl.program_id(0) == 0)
        def _():
            ds_scr[...] = jnp.zeros_like(ds_scr)

        dstates = [ds_scr[h] for h in range(G_HEADS)]
        heads = range(G_HEADS)
        lns = [slice(h * 128, (h + 1) * 128) for h in heads]
        for c in reversed(range(CPG)):
            r = slice(c * CH, (c + 1) * CH)
            dsbs = [dstates[h].astype(BF16) for h in heads]
            sns = [st_ref[h, c] for h in heads]
            snbs = [sns[h].astype(BF16) for h in heads]
            dobs = [do_ref[r, lns[h]].astype(BF16) for h in heads]
            vnbs = [vn_ref[r, lns[h]] for h in heads]
            dvns = [(_dot(kd_ref[r, lns[h]], dsbs[h]) + _dot_tn(at_ref[h, r, :], dobs[h])).astype(BF16) for h in heads]
            dkds = [_dot_nt(vnbs[h], dsbs[h]) for h in heads]
            dqds = [_dot_nt(dobs[h], snbs[h]) for h in heads]
            dats = [_dot_nt(dobs[h], vnbs[h]) for h in heads]
            dws = [_dot_nt(dvns[h], snbs[h]) for h in heads]
            ups = [_dot_tn(qd_ref[r, lns[h]], dobs[h]) - _dot_tn(w_ref[r, lns[h]], dvns[h]) for h in heads]
            for h in heads:
                du_ref[r, lns[h]] = dvns[h]
                dkd_ref[r, lns[h]] = dkds[h]
                tot = jnp.sum(jnp.sum(dstates[h] * sns[h], axis=1, keepdims=True), axis=0, keepdims=True)
                del_ref[h, pl.ds(c, 1), :] = jnp.broadcast_to(tot, (1, 128))
                dqd_ref[r, lns[h]] = dqds[h]
                dat_ref[h, r, :] = dats[h]
                dw_ref[r, lns[h]] = (-dws[h]).astype(BF16)
                dstates[h] = dstates[h] * el_ref[h, pl.ds(c, 1), :] + ups[h]
        for h in heads:
            ds_scr[h] = dstates[h]

    last = N_GRP - 1
    row = pl.BlockSpec((GRP, D), lambda i: (last - i, 0))
    sq = pl.BlockSpec((G_HEADS, GRP, CH), lambda i: (0, last - i, 0))
    sc = pl.BlockSpec((G_HEADS, CPG, 128), lambda i: (0, last - i, 0))
    return pl.pallas_call(
        body,
        grid=(N_GRP,),
        in_specs=[row, row, row, row, sq, sc, row, pl.BlockSpec((G_HEADS, CPG, 128, 128), lambda i: (0, last - i, 0, 0))],
        out_specs=[row, row, row, row, sq, sc],
        out_shape=[
            jax.ShapeDtypeStruct((T, D), BF16),
            jax.ShapeDtypeStruct((T, D), BF16),
            jax.ShapeDtypeStruct((T, D), F32),
            jax.ShapeDtypeStruct((T, D), F32),
            jax.ShapeDtypeStruct((G_HEADS, T, CH), F32),
            jax.ShapeDtypeStruct((G_HEADS, T // CH, 128), F32),
        ],
        scratch_shapes=[pltpu.VMEM((G_HEADS, 128, 128), F32)],
        compiler_params=_cp("arbitrary"),
        name=name,
    )(do, w, qd, kd, at, el, vn, st)


def gdn_local_bwd(qkv, ba, alog, dtb, tinv, du, dw, dqd, dkd, dat, dl, name):
    def body(q_ref, k_ref, v_ref, ba_ref, al_ref, dt_ref, ti_ref, du_ref, dw_ref, dqd_ref, dkd_ref, dat_ref, dl_ref,
             dq_ref, dk_ref, dv_ref, dba_ref, dal_ref, ddt_ref, gcs):
        gi = pl.program_id(0)
        h = pl.program_id(1)
        beta, g, xa, ea = _gdn_gates(ba_ref, al_ref, dt_ref, h)
        gc = _seg_cumsum(jnp.broadcast_to(g, (GRP, 128)))
        gcs[...] = gc
        causal, strict, _, _ = _chunk_masks()
        dgc_l, dgl_l, dbeta_l, state = [], [], [], []
        for c in range(CPG):
            r = slice(c * CH, (c + 1) * CH)
            q, k, v = q_ref[r, :], k_ref[r, :], v_ref[r, :]
            bt = beta[r, :]
            gcc = gc[r, :]
            ec = jnp.exp(gcc)
            gl = gcs[pl.ds(c * CH + CH - 1, 1), :]
            f2 = jnp.exp(gl - gcc)
            elc = jnp.exp(gl)
            dm = _decay(gcc, causal)
            qs = q * G_SCALE
            kb = k * bt
            vb = v * bt
            kbe = kb * ec
            kbf, kbb, qsb = k.astype(BF16), kb.astype(BF16), qs.astype(BF16)
            a1 = _dot_nt(kbb, kbf)
            qk = _dot_nt(qsb, kbf)
            ti = ti_ref[r, :]
            tb = ti.astype(BF16)
            du_c, dw_c = du_ref[r, :], dw_ref[r, :]
            dqd_c, dkd_c, dat_c = dqd_ref[r, :], dkd_ref[r, :], dat_ref[r, :]

            dqs = dqd_c * ec
            d_e = jnp.sum(dqd_c * qs, axis=1, keepdims=True)
            dk = dkd_c * f2
            tcol = jnp.sum(dkd_c * k, axis=1, keepdims=True) * f2[:, 0:1]
            dgl = jnp.sum(tcol, axis=0, keepdims=True) + dl_ref[pl.ds(c, 1), 0:1] * elc[:, 0:1]
            dgc = -tcol
            dqk = (dat_c * dm).astype(BF16)
            d_d = dat_c * qk
            dqs = dqs + _dot(dqk, kbf)
            dk = dk + _dot_tn(dqk, qsb)
            dtinv = _dot_nt(du_c, vb.astype(BF16)) + _dot_nt(dw_c, kbe.astype(BF16))
            dvb = _dot_tn(tb, du_c)
            dkbe = _dot_tn(tb, dw_c)
            dq_ref[r, :] = dqs * G_SCALE
            state.append((ti.T, dtinv, dm, a1, dkbe, dvb, d_d, dk, d_e, dgc, dgl))

        xs = [_dot3(st[0], st[1]) for st in state]
        dlms = [jnp.where(strict, -_dot3(x, st[0]), 0.0) for x, st in zip(xs, state)]

        for c in range(CPG):
            r = slice(c * CH, (c + 1) * CH)
            _, _, dm, a1, dkbe, dvb, d_d, dk, d_e, dgc, dgl = state[c]
            dlm = dlms[c]
            k, v = k_ref[r, :], v_ref[r, :]
            bt = beta[r, :]
            ec = jnp.exp(gc[r, :])
            kb = k * bt
            kbf, kbb = k.astype(BF16), kb.astype(BF16)
            da1 = (dlm * dm).astype(BF16)
            d_d = d_d + dlm * a1
            dkb = _dot(da1, kbf) + dkbe * ec
            dk = dk + _dot_tn(da1, kbb)
            d_e = d_e + jnp.sum(dkbe * kb, axis=1, keepdims=True)
            dk = dk + dkb * bt
            dbeta_l.append(jnp.sum(dkb * k, axis=1, keepdims=True) + jnp.sum(dvb * v, axis=1, keepdims=True))
            ddiff = d_d * dm
            dgc = dgc + jnp.sum(ddiff, axis=1, keepdims=True) - jnp.sum(ddiff.T, axis=1, keepdims=True)
            dgc = dgc + d_e * ec[:, 0:1]
            dgc_l.append(dgc)
            dgl_l.append(jnp.broadcast_to(dgl, (CH, 1)))
            dk_ref[r, :] = dk
            dv_ref[r, :] = dvb * bt

        dgc_all = jnp.broadcast_to(jnp.concatenate(dgc_l, axis=0), (GRP, 128))
        dg = _seg_cumsum(dgc_all, reverse=True)[:, 0:1] + jnp.concatenate(dgl_l, axis=0)
        dbeta = jnp.concatenate(dbeta_l, axis=0)
        da = dg * (-ea) * _sigmoid(xa)
        db = dbeta * beta * (1.0 - beta)
        lane = lax.broadcasted_iota(jnp.int32, (GRP, 2 * G_HEADS), 1)
        dba = jnp.where(lane == h, db, 0.0) + jnp.where(lane == G_HEADS + h, da, 0.0)
        lane8 = lax.broadcasted_iota(jnp.int32, (1, G_HEADS), 1)
        dal = jnp.where(lane8 == h, jnp.sum(dg * g, axis=0, keepdims=True), 0.0)
        ddt = jnp.where(lane8 == h, jnp.sum(da, axis=0, keepdims=True), 0.0)

        @pl.when(h == 0)
        def _():
            dba_ref[...] = dba

        @pl.when(h > 0)
        def _():
            dba_ref[...] += dba

        @pl.when((h == 0) & (gi == 0))
        def _():
            dal_ref[...] = dal
            ddt_ref[...] = ddt

        @pl.when((h > 0) | (gi > 0))
        def _():
            dal_ref[...] += dal
            ddt_ref[...] += ddt

    hd = lambda off: pl.BlockSpec((GRP, 128), lambda i, h: (i, off + h))
    vec = pl.BlockSpec((1, G_HEADS), lambda i, h: (0, 0))
    sq = pl.BlockSpec((None, GRP, CH), lambda i, h: (h, i, 0))
    gates = pl.BlockSpec((GRP, 2 * G_HEADS), lambda i, h: (i, 0))
    return pl.pallas_call(
        body,
        grid=(N_GRP, G_HEADS),
        in_specs=[hd(0), hd(G_HEADS), hd(2 * G_HEADS), gates, vec, vec, sq, hd(0), hd(0), hd(0), hd(0), sq,
                  pl.BlockSpec((None, CPG, 128), lambda i, h: (h, i, 0))],
        out_specs=[hd(0), hd(0), hd(0), gates, vec, vec],
        out_shape=[
            jax.ShapeDtypeStruct((T, D), F32),
            jax.ShapeDtypeStruct((T, D), F32),
            jax.ShapeDtypeStruct((T, D), F32),
            jax.ShapeDtypeStruct((T, 2 * G_HEADS), F32),
            jax.ShapeDtypeStruct((1, G_HEADS), F32),
            jax.ShapeDtypeStruct((1, G_HEADS), F32),
        ],
        scratch_shapes=[pltpu.VMEM((GRP, 128), F32)],
        compiler_params=_cp("arbitrary", "arbitrary"),
        name=name,
    )(qkv, qkv, qkv, ba, alog, dtb, tinv, du, dw, dqd, dkd, dat, dl)


def _gated_norm(o, z, g):
    rstd = lax.rsqrt(jnp.mean(o * o, axis=-1, keepdims=True) + EPS)
    oh = o * rstd
    sig = _sigmoid(z)
    return oh, rstd, sig


def gdn_out_fwd(h, o, z, norm_g, w_out, name):
    tt = 512

    def body(h_ref, o_ref, z_ref, g_ref, w_ref, hnew_ref, cat_ref):
        g = g_ref[...]
        for hh in range(G_HEADS):
            ln = slice(hh * 128, (hh + 1) * 128)
            zz = z_ref[:, ln]
            oh, _, sig = _gated_norm(o_ref[:, ln], zz, g)
            cat_ref[:, ln] = (oh * g * (zz * sig)).astype(BF16)
        hnew_ref[...] = h_ref[...] + _dot(cat_ref[...], w_ref[...])

    full = pl.BlockSpec((tt, D), lambda i: (i, 0))
    return pl.pallas_call(
        body,
        grid=(T // tt,),
        in_specs=[full, full, full, pl.BlockSpec((1, 128), lambda i: (0, 0)), pl.BlockSpec((D, D), lambda i: (0, 0))],
        out_specs=[full, full],
        out_shape=[jax.ShapeDtypeStruct((T, D), F32), jax.ShapeDtypeStruct((T, D), BF16)],
        compiler_params=_cp("parallel"),
        name=name,
    )(h, o, z, norm_g, w_out)


def gdn_out_bwd(dres, o, z, norm_g, w_out, name):
    tt = 512

    def body(d_ref, o_ref, z_ref, g_ref, w_ref, do_ref, dz_ref, dg_ref, dcat):
        i = pl.program_id(0)
        g = g_ref[...]
        dcat[...] = _dot_nt(d_ref[...].astype(BF16), w_ref[...])
        dg = jnp.zeros((1, 128), F32)
        for hh in range(G_HEADS):
            ln = slice(hh * 128, (hh + 1) * 128)
            zz = z_ref[:, ln]
            oh, rstd, sig = _gated_norm(o_ref[:, ln], zz, g)
            dout = dcat[:, ln]
            dy = dout * (zz * sig)
            dz_ref[:, ln] = dout * (oh * g) * _dsilu(zz, sig)
            dg = dg + jnp.sum(dy * oh, axis=0, keepdims=True)
            doh = dy * g
            do_ref[:, ln] = rstd * (doh - oh * jnp.mean(doh * oh, axis=-1, keepdims=True))

        @pl.when(i == 0)
        def _():
            dg_ref[...] = dg

        @pl.when(i > 0)
        def _():
            dg_ref[...] += dg

    full = pl.BlockSpec((tt, D), lambda i: (i, 0))
    vec = pl.BlockSpec((1, 128), lambda i: (0, 0))
    return pl.pallas_call(
        body,
        grid=(T // tt,),
        in_specs=[full, full, full, vec, pl.BlockSpec((D, D), lambda i: (0, 0))],
        out_specs=[full, full, vec],
        out_shape=[jax.ShapeDtypeStruct((T, D), F32), jax.ShapeDtypeStruct((T, D), F32), jax.ShapeDtypeStruct((1, 128), F32)],
        scratch_shapes=[pltpu.VMEM((tt, D), F32)],
        compiler_params=_cp("arbitrary"),
        name=name,
    )(dres, o, z, norm_g, w_out)


GDN_SPLITS = [(0, 1024), (1024, 1024), (2048, 1024), (3072, 1024), (4096, 2 * G_HEADS)]


def gdn_fwd(h, g_row, w_in, conv_w, alog, dtb, norm_g, w_out, tag):
    hn, qkv_pre, z, ba = proj_fwd(h, g_row, w_in, [(0, G_QKV), (G_QKV, 1024), (4096, 2 * G_HEADS)], f"gdn_proj_{tag}")
    qkv = gdn_prep_fwd(qkv_pre, conv_w, f"gdn_prep_{tag}")
    u, w, qd, kd, at, el, tinv = gdn_local_fwd(qkv, ba, alog, dtb, f"gdn_local_{tag}")
    o, vn, st = gdn_rec_fwd(u, w, qd, kd, at, el, f"gdn_rec_{tag}")
    hnew, cat = gdn_out_fwd(h, o, z, norm_g, w_out, f"gdn_out_{tag}")
    return hnew, (h, hn, qkv_pre, z, ba, qkv, w, qd, kd, at, el, tinv, o, vn, st, cat)


def gdn_bwd(dres, saved, g_row, w_in, conv_w, alog, dtb, norm_g, w_out, tag):
    h, hn, qkv_pre, z, ba, qkv, w, qd, kd, at, el, tinv, o, vn, st, cat = saved
    d_o, d_z, d_ng = gdn_out_bwd(dres, o, z, norm_g, w_out, f"gdn_out_bwd_{tag}")
    d_wout = mm_tn_full(cat, dres, BF16, f"gdn_wout_grad_{tag}")
    du, dw, dqd, dkd, dat, dl = gdn_rec_bwd(d_o, w, qd, kd, at, el, vn, st, f"gdn_rec_bwd_{tag}")
    dq, dk, dv, dba, dal, ddt = gdn_local_bwd(qkv, ba, alog, dtb, tinv, du, dw, dqd, dkd, dat, dl, f"gdn_local_bwd_{tag}")
    dpre, dcw = [], []
    for part, d in enumerate((dq, dk, dv)):
        dx, dwc = gdn_prep_bwd(d, qkv_pre, conv_w, part, part < 2, f"gdn_prep_bwd_{tag}_{part}")
        dpre.append(dx)
        dcw.append(dwc)
    parts = dpre + [d_z, dba]
    dres_new, d_norm, dproj = proj_bwd_data(h, g_row, w_in, parts, GDN_SPLITS, dres, f"gdn_proj_bwd_{tag}")
    d_win = cols_to_slabs(mm_tn_full(hn, dproj, F32, f"gdn_win_grad_{tag}"), mm_tn(hn, dba, f"gdn_win_grad_ba_{tag}"),
                          f"gdn_win_slabs_{tag}")
    return dres_new, dict(norm=d_norm, w_in=d_win, conv_w=jnp.concatenate(dcw, axis=1), A_log=dal, dt_bias=ddt,
                          norm_g=d_ng, w_out=d_wout)


MESH = pl.DeviceIdType.MESH
ANY = pl.BlockSpec(memory_space=pl.ANY)


def _coords():
    return lax.axis_index("x"), lax.axis_index("y"), lax.axis_index("c")


def _slot(p):
    return 4 * p[0] + 2 * p[1] + p[2]


def all_gather(shards, name):
    k_n = len(shards)

    def body(*refs):
        srcs, dsts = refs[:k_n], refs[k_n:2 * k_n]
        send_sems, recv_sems, local_sems = refs[2 * k_n:]
        x, y, c = _coords()
        me, sibling = (x, y, c), (x, y, 1 - c)
        chips = [(1 - x, y), (x, 1 - y), (1 - x, 1 - y)]

        def copy(k, s, block, to, from_src=False):
            rows = dsts[k].at[_slot(block)]
            return pltpu.make_async_remote_copy(
                src_ref=srcs[k] if from_src else rows, dst_ref=rows,
                send_sem=send_sems.at[k, s], recv_sem=recv_sems.at[k, s], device_id=to, device_id_type=MESH)

        local = [pltpu.make_async_copy(srcs[k], dsts[k].at[_slot(me)], local_sems.at[k]) for k in range(k_n)]
        for cp in local:
            cp.start()
        first = []
        for k in range(k_n):
            first.append(copy(k, 0, me, sibling, True))
            first += [copy(k, 1 + j, me, (*chip, c), True) for j, chip in enumerate(chips)]
        for cp in first:
            cp.start()
        passed = []
        for j, chip in enumerate(chips):
            for k in range(k_n):
                copy(k, 1 + j, (*chip, c), me).wait_recv()
                fw = copy(k, 4 + j, (*chip, c), sibling)
                fw.start()
                passed.append(fw)
        for k in range(k_n):
            copy(k, 0, sibling, me).wait_recv()
            for j, chip in enumerate(chips):
                copy(k, 4 + j, (*chip, 1 - c), me).wait_recv()
        for cp in first + passed:
            cp.wait_send()
        for cp in local:
            cp.wait()

    return pl.pallas_call(
        body,
        in_specs=[ANY] * k_n,
        out_specs=[ANY] * k_n,
        out_shape=[jax.ShapeDtypeStruct((N_DEV,) + s.shape, s.dtype) for s in shards],
        scratch_shapes=[pltpu.SemaphoreType.DMA((k_n, 7)), pltpu.SemaphoreType.DMA((k_n, 7)),
                        pltpu.SemaphoreType.DMA((k_n,))],
        name=name,
    )(*shards)


HBM = pl.BlockSpec(memory_space=pltpu.HBM)
SEM = pl.BlockSpec(memory_space=pltpu.SEMAPHORE)
EFFECT = pltpu.SideEffectType.DATAFLOW_SIDE_EFFECTING


def _hbm(a):
    return pltpu.with_memory_space_constraint(a, pltpu.HBM)


def _peer_list(x, y, c):
    peers = []
    for j in range(1, N_DEV):
        jx, jy, jc = (j >> 2) & 1, (j >> 1) & 1, j & 1
        peers.append((x if jx == 0 else 1 - x, y if jy == 0 else 1 - y, c if jc == 0 else 1 - c))
    return peers


def _push_views(kind, layer, src_ref, land_ref, me, peer_slot):
    if kind == "gather":
        return src_ref, land_ref.at[me], land_ref.at[peer_slot]
    if layer is None:
        return src_ref.at[peer_slot], land_ref.at[me], land_ref.at[peer_slot]
    return src_ref.at[peer_slot], land_ref.at[me, layer], land_ref.at[peer_slot, layer]


def _push_copies(groups, srcs, lands, sems):
    x, y, c = _coords()
    me = _slot((x, y, c))
    peers = _peer_list(x, y, c)
    t = 0
    for gi, group in enumerate(groups):
        for ti, (kind, layer, _, li) in enumerate(group):
            for j, peer in enumerate(peers):
                out, there, here = _push_views(kind, layer, srcs[t], lands[li], me, _slot(peer))
                k = ti * (N_DEV - 1) + j
                yield out, there, here, sems[2 * gi].at[k], sems[2 * gi + 1].at[k], peer
            t += 1


def push_start(groups, lands, name, carry=()):
    flat = [it for g in groups for it in g]
    n, n_l, n_g, n_c = len(flat), len(lands), len(groups), len(carry)
    n_in = n + n_l + n_c

    def body(*refs):
        srcs, land_refs, sems = refs[:n], refs[n:n + n_l], refs[n_in:n_in + 2 * n_g]
        for out, there, _, s_sem, r_sem, peer in _push_copies(groups, srcs, land_refs, sems):
            pltpu.make_async_remote_copy(src_ref=out, dst_ref=there, send_sem=s_sem, recv_sem=r_sem,
                                         device_id=peer, device_id_type=MESH).start()

    arrays = [it[2] for it in flat] + list(lands) + list(carry)
    sem_shapes = []
    for g in groups:
        sem_shapes += [pltpu.SemaphoreType.DMA((len(g) * (N_DEV - 1),))] * 2
    outs = pl.pallas_call(
        body,
        name=name,
        in_specs=[HBM] * n_in,
        out_specs=[SEM] * (2 * n_g) + [HBM] * n_in,
        out_shape=sem_shapes + [pltpu.HBM(a.shape, a.dtype) for a in arrays],
        input_output_aliases={i: 2 * n_g + i for i in range(n_in)},
        compiler_params=pltpu.CompilerParams(has_side_effects=EFFECT),
    )(*[_hbm(a) for a in arrays])
    sems, thru = list(outs[:2 * n_g]), list(outs[2 * n_g:])
    return sems, thru[:n], thru[n:n + n_l], thru[n + n_l:]


def push_wait(groups, lands, sems, after, name):
    flat = [it for g in groups for it in g]
    n, n_l, n_g = len(flat), len(lands), len(groups)

    def body(*refs):
        srcs, land_refs, sem_refs = refs[:n], refs[n:n + n_l], refs[n + n_l:n + n_l + 2 * n_g]
        for out, _, here, s_sem, r_sem, peer in _push_copies(groups, srcs, land_refs, sem_refs):
            cp = pltpu.make_async_remote_copy(src_ref=out, dst_ref=here, send_sem=s_sem, recv_sem=r_sem,
                                              device_id=peer, device_id_type=MESH)
            cp.wait_send()
            cp.wait_recv()

    arrays = [it[2] for it in flat] + list(lands)
    outs = pl.pallas_call(
        body,
        name=name,
        in_specs=[HBM] * (n + n_l) + [SEM] * (2 * n_g) + [ANY],
        out_specs=[HBM] * (n + n_l),
        out_shape=[pltpu.HBM(a.shape, a.dtype) for a in arrays],
        input_output_aliases={i: i for i in range(n + n_l)},
        compiler_params=pltpu.CompilerParams(has_side_effects=EFFECT),
    )(*arrays, *sems, after)
    return list(outs[:n]), list(outs[n:])


def sum_slabs(parts, name):
    n, rows, cols = parts.shape

    def body(p_ref, o_ref):
        g = p_ref[0]
        for s in range(1, n):
            g = g + p_ref[s]
        o_ref[...] = g

    return pl.pallas_call(body, out_shape=jax.ShapeDtypeStruct((rows, cols), F32), name=name)(parts)


def _row_tile(rows, cols):
    if rows * cols * 4 <= (1 << 20) or rows % 8:
        return rows
    tr = rows
    while tr % 2 == 0 and (tr // 2) % 8 == 0 and tr * cols * 4 > (1 << 20):
        tr //= 2
    return tr


def adamw(parts, w, m, v, name):
    p_n = parts.shape[0]
    rows, cols = w.shape
    tr = _row_tile(rows, cols)

    def body(p_ref, w_ref, m_ref, v_ref, g_ref, d_ref, nm_ref, nv_ref):
        g = p_ref[0].astype(F32)
        for s in range(1, p_n):
            g = g + p_ref[s].astype(F32)
        g_ref[...] = g
        d_ref[...], nm_ref[...], nv_ref[...] = _adam_update(g, w_ref[...], m_ref[...], v_ref[...])

    blk = pl.BlockSpec((tr, cols), lambda i: (i, 0))
    return pl.pallas_call(
        body,
        grid=(rows // tr,),
        in_specs=[pl.BlockSpec((p_n, tr, cols), lambda i: (0, i, 0)), blk, blk, blk],
        out_specs=[blk] * 4,
        out_shape=[jax.ShapeDtypeStruct((rows, cols), F32)] * 4,
        compiler_params=_cp("parallel"),
        name=name,
    )(parts, w, m, v)


def _adam_update(g, w, m, v):
    m_new = ADAM_B1 * m + (1.0 - ADAM_B1) * g
    v_new = ADAM_B2 * v + (1.0 - ADAM_B2) * (g * g)
    m_hat = m_new / (1.0 - ADAM_B1 ** ADAM_STEP)
    v_hat = v_new / (1.0 - ADAM_B2 ** ADAM_STEP)
    return -ADAM_LR * (m_hat / (jnp.sqrt(v_hat) + ADAM_EPS) + ADAM_WD * w), m_new, v_new


def _adamw_nd(parts, w, m, v, name):
    p_n = parts.shape[0]
    n_l, rows, cols = w.shape
    tr = _row_tile(rows, cols)

    def body(p_ref, w_ref, m_ref, v_ref, g_ref, d_ref, nm_ref, nv_ref):
        g = p_ref[0].astype(F32)
        for s in range(1, p_n):
            g = g + p_ref[s].astype(F32)
        g_ref[...] = g
        d_ref[...], nm_ref[...], nv_ref[...] = _adam_update(g, w_ref[...], m_ref[...], v_ref[...])

    blk = pl.BlockSpec((None, tr, cols), lambda l, i: (l, i, 0))
    return pl.pallas_call(
        body,
        grid=(n_l, rows // tr),
        in_specs=[pl.BlockSpec((p_n, None, tr, cols), lambda l, i: (0, l, i, 0)), blk, blk, blk],
        out_specs=[blk] * 4,
        out_shape=[jax.ShapeDtypeStruct(w.shape, F32)] * 4,
        compiler_params=_cp("parallel", "parallel"),
        name=name,
    )(parts, w, m, v)


def slabs_to_cols(slabs, name):
    n, r, w = slabs.shape
    tr = 256 if r % 256 == 0 else r

    def body(s_ref, o_ref):
        for s in range(n):
            o_ref[:, w * s:w * (s + 1)] = s_ref[s]

    return pl.pallas_call(
        body,
        grid=(r // tr,),
        in_specs=[pl.BlockSpec((n, tr, w), lambda i: (0, i, 0))],
        out_specs=pl.BlockSpec((tr, n * w), lambda i: (i, 0)),
        out_shape=jax.ShapeDtypeStruct((r, n * w), slabs.dtype),
        compiler_params=_cp("parallel"),
        name=name,
    )(slabs)


FFN_IN = ("ffn1_w_in", "ffn2_w_in")
REPL = ["ffn1_norm", "mix_norm", "ffn2_norm", "hyb_dw_b", "hyb_ln_g", "hyb_ln_b", "gdn_A_log", "gdn_dt_bias",
        "gdn_norm_g", "final_norm"]
WEIGHTS = ["ffn1_norm", "ffn1_w_in", "ffn1_w_out", "mix_norm", "ffn2_norm", "ffn2_w_in", "ffn2_w_out", "hyb_w_in",
           "hyb_dw_w", "hyb_dw_b", "hyb_ln_g", "hyb_ln_b", "hyb_w_out", "gdn_w_in", "gdn_conv_w", "gdn_A_log",
           "gdn_dt_bias", "gdn_norm_g", "gdn_w_out", "final_norm"]


def _pack(arrs, rows):
    flat = jnp.concatenate([a.reshape(-1) for a in arrs])
    return jnp.pad(flat, (0, rows * 128 - flat.shape[0])).reshape(rows, 128)


def kernel(x, positions, ffn1_norm, ffn1_w_in, ffn1_w_out, mix_norm, ffn2_norm, ffn2_w_in, ffn2_w_out, hyb_w_in, hyb_dw_w, hyb_dw_b, hyb_ln_g, hyb_ln_b, hyb_w_out, gdn_w_in, gdn_conv_w, gdn_A_log, gdn_dt_bias, gdn_norm_g, gdn_w_out, final_norm, loss_target, m_ffn1_norm, m_ffn1_w_in, m_ffn1_w_out, m_mix_norm, m_ffn2_norm, m_ffn2_w_in, m_ffn2_w_out, m_hyb_w_in, m_hyb_dw_w, m_hyb_dw_b, m_hyb_ln_g, m_hyb_ln_b, m_hyb_w_out, m_gdn_w_in, m_gdn_conv_w, m_gdn_A_log, m_gdn_dt_bias, m_gdn_norm_g, m_gdn_w_out, m_final_norm, v_ffn1_norm, v_ffn1_w_in, v_ffn1_w_out, v_mix_norm, v_ffn2_norm, v_ffn2_w_in, v_ffn2_w_out, v_hyb_w_in, v_hyb_dw_w, v_hyb_dw_b, v_hyb_ln_g, v_hyb_ln_b, v_hyb_w_out, v_gdn_w_in, v_gdn_conv_w, v_gdn_A_log, v_gdn_dt_bias, v_gdn_norm_g, v_gdn_w_out, v_final_norm):
    w = dict(ffn1_norm=ffn1_norm, ffn1_w_in=ffn1_w_in, ffn1_w_out=ffn1_w_out, mix_norm=mix_norm, ffn2_norm=ffn2_norm,
             ffn2_w_in=ffn2_w_in, ffn2_w_out=ffn2_w_out, hyb_w_in=hyb_w_in, hyb_dw_w=hyb_dw_w, hyb_dw_b=hyb_dw_b,
             hyb_ln_g=hyb_ln_g, hyb_ln_b=hyb_ln_b, hyb_w_out=hyb_w_out, gdn_w_in=gdn_w_in, gdn_conv_w=gdn_conv_w,
             gdn_A_log=gdn_A_log, gdn_dt_bias=gdn_dt_bias, gdn_norm_g=gdn_norm_g, gdn_w_out=gdn_w_out,
             final_norm=final_norm)
    mom = dict(ffn1_norm=m_ffn1_norm, ffn1_w_in=m_ffn1_w_in, ffn1_w_out=m_ffn1_w_out, mix_norm=m_mix_norm,
               ffn2_norm=m_ffn2_norm, ffn2_w_in=m_ffn2_w_in, ffn2_w_out=m_ffn2_w_out, hyb_w_in=m_hyb_w_in,
               hyb_dw_w=m_hyb_dw_w, hyb_dw_b=m_hyb_dw_b, hyb_ln_g=m_hyb_ln_g, hyb_ln_b=m_hyb_ln_b,
               hyb_w_out=m_hyb_w_out, gdn_w_in=m_gdn_w_in, gdn_conv_w=m_gdn_conv_w, gdn_A_log=m_gdn_A_log,
               gdn_dt_bias=m_gdn_dt_bias, gdn_norm_g=m_gdn_norm_g, gdn_w_out=m_gdn_w_out, final_norm=m_final_norm)
    var = dict(ffn1_norm=v_ffn1_norm, ffn1_w_in=v_ffn1_w_in, ffn1_w_out=v_ffn1_w_out, mix_norm=v_mix_norm,
               ffn2_norm=v_ffn2_norm, ffn2_w_in=v_ffn2_w_in, ffn2_w_out=v_ffn2_w_out, hyb_w_in=v_hyb_w_in,
               hyb_dw_w=v_hyb_dw_w, hyb_dw_b=v_hyb_dw_b, hyb_ln_g=v_hyb_ln_g, hyb_ln_b=v_hyb_ln_b,
               hyb_w_out=v_hyb_w_out, gdn_w_in=v_gdn_w_in, gdn_conv_w=v_gdn_conv_w, gdn_A_log=v_gdn_A_log,
               gdn_dt_bias=v_gdn_dt_bias, gdn_norm_g=v_gdn_norm_g, gdn_w_out=v_gdn_w_out, final_norm=v_final_norm)
    xi, yi, ci = _coords()
    me = 4 * xi + 2 * yi + ci
    for group in (w, mom, var):
        for n in FFN_IN:
            group[n] = jnp.swapaxes(group[n], 1, 2)

    big = ["ffn1_w_in", "ffn1_w_out", "ffn2_w_in", "ffn2_w_out", "hyb_w_in", "hyb_w_out", "gdn_w_in", "gdn_w_out"]
    ag_groups, ag_lands = [], []

    def add_group(shards):
        group = []
        for s in shards:
            land = lax.dynamic_update_slice(lax.empty((N_DEV,) + s.shape, s.dtype), s[None], (me,) + (0,) * s.ndim)
            group.append(("gather", None, s, len(ag_lands)))
            ag_lands.append(land)
        ag_groups.append(group)

    first = all_gather([w["ffn1_w_in"][0].astype(BF16), ffn1_w_out[0].astype(BF16)], "weights_gather_first")
    for l in range(DEPTH):
        i = l // 2
        if l == 0:
            ag_groups.append([])
        else:
            add_group([w["ffn1_w_in"][l].astype(BF16), ffn1_w_out[l].astype(BF16)])
        if l % 2 == 0:
            add_group([hyb_w_in[i].astype(BF16), hyb_w_out[i].astype(BF16), hyb_dw_w[i]])
        else:
            add_group([gdn_w_in[i].astype(BF16), gdn_w_out[i].astype(BF16), gdn_conv_w[i]])
        add_group([w["ffn2_w_in"][l].astype(BF16), ffn2_w_out[l].astype(BF16)])
    ag_sems, ag_srcs, ag_lands, first = push_start(ag_groups[1:], ag_lands, "weights_gather_start", carry=first)
    ag_sems = [None, None] + ag_sems

    def fetch(gi, after):
        if gi == 0:
            return first
        group = ag_groups[gi]
        base = sum(len(g) for g in ag_groups[:gi])
        items = [(kind, layer, ag_srcs[base + t], t) for t, (kind, layer, _, _) in enumerate(group)]
        lands = [ag_lands[li] for _, _, _, li in group]
        return push_wait([items], lands, ag_sems[2 * gi:2 * gi + 2], after, f"weights_gather_wait_{gi}")[1]

    row = lambda a: a.reshape(1, -1)

    rope = make_rope(positions)
    h = x[0]
    saved = []
    for l in range(DEPTH):
        i = l // 2
        rec = {"h1": h}
        wi, wo = fetch(3 * l, h)
        rec["w1"] = (wi.reshape(2, FFN_TILES, FFN_SHARD, D), wo)
        h, rec["hn1"], rec["a1"], rec["b1"] = ffn_fwd(h, row(ffn1_norm[l]), *rec["w1"], l, "1")
        mi, mo, mc = fetch(3 * l + 1, h)
        if l % 2 == 0:
            rec["wm"] = (slabs_to_cols(mi, f"hyb_w_in_cols_{i}"),
                         jnp.pad(slabs_to_cols(mc, f"hyb_dw_w_cols_{i}"), ((0, 1), (0, 0))), mo.reshape(D, D))
            w_in_f, dw_f, w_out_f = rec["wm"]
            h, rec["mix"] = hybrid_fwd(h, row(mix_norm[l]), w_in_f, dw_f, row(hyb_dw_b[i]), row(hyb_ln_g[i]),
                                       row(hyb_ln_b[i]), w_out_f, rope, str(i))
        else:
            rec["wm"] = (slabs_to_cols(mi, f"gdn_w_in_cols_{i}"), slabs_to_cols(mc, f"gdn_conv_w_cols_{i}"),
                         mo.reshape(D, D))
            w_in_f, cw_f, w_out_f = rec["wm"]
            h, rec["mix"] = gdn_fwd(h, row(mix_norm[l]), w_in_f, cw_f, row(gdn_A_log[i]), row(gdn_dt_bias[i]),
                                    row(gdn_norm_g[i]), w_out_f, str(i))
        rec["h2"] = h
        wi, wo = fetch(3 * l + 2, h)
        rec["w2"] = (wi.reshape(2, FFN_TILES, FFN_SHARD, D), wo)
        h, rec["hn2"], rec["a2"], rec["b2"] = ffn_fwd(h, row(ffn2_norm[l]), *rec["w2"], l, "2")
        saved.append(rec)
    dres, d_final, loss_acc = final_loss(h, row(final_norm), loss_target[0])

    ge_land = {n: lax.empty((N_DEV,) + w[n].shape, BF16) for n in big}
    ge_pending = []

    def send(named, layer, tag, carry):
        lands = [ge_land[n] for n, _ in named]
        group = [("scatter", layer, s, t) for t, (_, s) in enumerate(named)]
        sems, srcs, lands_out, carried = push_start([group], lands, f"grad_send_{tag}", carry=[carry])
        for (n, _), land in zip(named, lands_out):
            ge_land[n] = land
        ge_pending.append(([(n, layer, s) for (n, _), s in zip(named, srcs)], sems))
        return carried[0]

    gsmall = {n: [None] * (DEPTH if n in ("ffn1_norm", "mix_norm", "ffn2_norm") else 2) for n in REPL[:-1]}
    gsmall["hyb_dw_w"] = [None, None]
    gsmall["gdn_conv_w"] = [None, None]
    for l in reversed(range(DEPTH)):
        i = l // 2
        rec = saved[l]
        dhn, dwin, dwout = ffn_bwd(rec["hn2"], rec["a2"], rec["b2"], dres, *rec["w2"], l, "2")
        dhn = send([("ffn2_w_in", dwin.reshape(N_DEV, FFN_SHARD, D)),
                    ("ffn2_w_out", dwout.reshape(N_DEV, FFN_SHARD // 2, D))], l, f"ffn2_{l}", dhn)
        dres, dg = norm_bwd(rec["h2"], row(ffn2_norm[l]), dhn, dres, f"ffn2_norm_bwd_{l}")
        gsmall["ffn2_norm"][l] = dg
        if l % 2 == 0:
            w_in_f, dw_f, w_out_f = rec["wm"]
            dres, gr = hybrid_bwd(dres, rec["mix"], row(mix_norm[l]), w_in_f, dw_f, row(hyb_ln_g[i]),
                                  row(hyb_ln_b[i]), w_out_f, rope, str(i))
            dres = send([("hyb_w_in", gr["w_in"]), ("hyb_w_out", gr["w_out"].reshape(N_DEV, D // N_DEV, D))],
                        i, f"hyb_{i}", dres)
            for n in ("dw_w", "dw_b", "ln_g", "ln_b"):
                gsmall["hyb_" + n][i] = gr[n]
        else:
            w_in_f, cw_f, w_out_f = rec["wm"]
            dres, gr = gdn_bwd(dres, rec["mix"], row(mix_norm[l]), w_in_f, cw_f, row(gdn_A_log[i]),
                               row(gdn_dt_bias[i]), row(gdn_norm_g[i]), w_out_f, str(i))
            dres = send([("gdn_w_in", gr["w_in"]), ("gdn_w_out", gr["w_out"].reshape(N_DEV, D // N_DEV, D))],
                        i, f"gdn_{i}", dres)
            for n in ("conv_w", "A_log", "dt_bias", "norm_g"):
                gsmall["gdn_" + n][i] = gr[n]
        gsmall["mix_norm"][l] = gr["norm"]
        dhn, dwin, dwout = ffn_bwd(rec["hn1"], rec["a1"], rec["b1"], dres, *rec["w1"], l, "1")
        dhn = send([("ffn1_w_in", dwin.reshape(N_DEV, FFN_SHARD, D)),
                    ("ffn1_w_out", dwout.reshape(N_DEV, FFN_SHARD // 2, D))], l, f"ffn1_{l}", dhn)
        dres, dg = norm_bwd(rec["h1"], row(ffn1_norm[l]), dhn, dres, f"ffn1_norm_bwd_{l}")
        gsmall["ffn1_norm"][l] = dg
    grad_x = dres[None]

    n_repl_rows = 136
    small_rows = 576
    repl_flat = jnp.concatenate([jnp.concatenate([a.reshape(-1) for a in gsmall[n]]) for n in REPL[:-1]]
                                + [d_final.reshape(-1), loss_acc[0, 0:1]])
    loss_at = repl_flat.shape[0] - 1
    repl_pack = jnp.pad(repl_flat, (0, n_repl_rows * 128 - repl_flat.shape[0]))
    small_pack = jnp.concatenate([repl_pack] + [a.reshape(-1) for a in gsmall["hyb_dw_w"]]
                                 + [a.reshape(-1) for a in gsmall["gdn_conv_w"]]).reshape(small_rows, 128)

    own = {n: {} for n in big}

    def wait_for(pending, names, after, name):
        groups = [[("scatter", layer, s, names.index(n)) for n, layer, s in named] for named, _ in pending]
        sems = [s for _, pair in pending for s in pair]
        srcs_out, lands_out = push_wait(groups, [ge_land[n] for n in names], sems, after, name)
        flat_named = [it for named, _ in pending for it in named]
        for (n, layer, _), s in zip(flat_named, srcs_out):
            own[n][layer] = lax.dynamic_index_in_dim(s, me, 0, keepdims=False)
        for n, land in zip(names, lands_out):
            ge_land[n] = land

    def with_own(n, land):
        mine = jnp.stack([own[n][k] for k in range(len(own[n]))])
        return lax.dynamic_update_slice(land, mine[None], (me,) + (0,) * mine.ndim)

    out = {}
    last = ["ffn1_w_in", "ffn1_w_out"]
    wait_for(ge_pending[:-1], big, dres, "grad_wait_a")
    for n in big:
        if n not in last:
            out[n] = _adamw_nd(with_own(n, ge_land[n]), w[n], mom[n], var[n], f"adamw_{n}")
    pin = sum(out[n][1].reshape(-1)[0] for n in big if n not in last) * 0.0
    small_all, = all_gather([small_pack + pin], "small_grads_all_gather")
    g_small = sum_slabs(small_all, "small_grads_sum")
    loss = g_small.reshape(-1)[loss_at]
    wait_for(ge_pending[-1:], last, g_small, "grad_wait_b")
    for n in last:
        out[n] = _adamw_nd(with_own(n, ge_land[n]), w[n], mom[n], var[n], f"adamw_{n}")

    pk = lambda d: _pack([d[n] for n in REPL], n_repl_rows)
    res = adamw(g_small[:n_repl_rows][None], pk(w), pk(mom), pk(var), "adamw_replicated")
    off = 0
    for n in REPL:
        sz = w[n].size
        out[n] = [r.reshape(-1)[off:off + sz].reshape(w[n].shape) for r in res]
        off += sz
    g_dw = g_small[n_repl_rows:n_repl_rows + 248].reshape(2, CONV_K, CONV_C)
    g_dw = lax.dynamic_slice_in_dim(g_dw, me * (CONV_C // N_DEV), CONV_C // N_DEV, axis=2)
    out["hyb_dw_w"] = _adamw_nd(g_dw[None], w["hyb_dw_w"], mom["hyb_dw_w"], var["hyb_dw_w"], "adamw_hyb_dw_w")
    g_cw = g_small[n_repl_rows + 248:].reshape(2, G_CONV, G_QKV)
    g_cw = lax.dynamic_slice_in_dim(g_cw, me * (G_QKV // N_DEV), G_QKV // N_DEV, axis=2)
    out["gdn_conv_w"] = _adamw_nd(g_cw[None], w["gdn_conv_w"], mom["gdn_conv_w"], var["gdn_conv_w"], "adamw_gdn_conv_w")

    for n in FFN_IN:
        out[n] = [jnp.swapaxes(o, 1, 2) for o in out[n]]
    return (loss, grad_x, *[out[n][0] for n in WEIGHTS], *[out[n][1] for n in WEIGHTS],
            *[out[n][2] for n in WEIGHTS], *[out[n][3] for n in WEIGHTS])
```

```python
import functools

import jax
import jax.numpy as jnp
from jax import lax
from jax.experimental import pallas as pl
from jax.experimental.pallas import tpu as pltpu

F32 = jnp.float32
BF16 = jnp.bfloat16

N_DEV = 8
T = 4096
D = 1024
DEPTH = 4
FFN = 2816
FFN_SHARD = 2 * FFN // N_DEV
FFN_TILES = FFN // FFN_SHARD
EPS = 1e-6

A_HEADS = 8
A_HD = 64
A_W = 512
CONV_C = 512
CONV_K = 31
HYB_IN = 2560
ROPE_THETA = 500000.0
ROT = 16
DILATIONS = (1, 4, 16)
BLK = 128
KPAD = 2048

G_HEADS = 8
G_DK = 128
G_QKV = 3072
G_IN = 4112
G_CONV = 4
CH = 64
GRP = 512
CPG = GRP // CH
N_GRP = T // GRP

ADAM_LR = 0.001
ADAM_B1 = 0.9
ADAM_B2 = 0.999
ADAM_EPS = 1e-08
ADAM_WD = 0.01
ADAM_STEP = 10

VMEM_LIMIT = 56 * 1024 * 1024


def _cp(*sem):
    return pltpu.CompilerParams(dimension_semantics=sem, vmem_limit_bytes=VMEM_LIMIT)


def _dot(a, b):
    return jnp.dot(a, b, preferred_element_type=F32)


def _dot_nt(a, b):
    return lax.dot_general(a, b, (((1,), (1,)), ((), ())), preferred_element_type=F32)


def _dot_tn(a, b):
    return lax.dot_general(a, b, (((0,), (0,)), ((), ())), preferred_element_type=F32)


def _sigmoid(x):
    return 1.0 / (1.0 + jnp.exp(-x))


def _dsilu(x, sig):
    return sig * (1.0 + x * (1.0 - sig))


def _rms(x, g):
    rstd = lax.rsqrt(jnp.mean(x * x, axis=-1, keepdims=True) + EPS)
    return x * rstd * g


FFN_TT = 512


def ffn_fwd(h, g_row, w_in, w_out, layer, tag=""):
    def body(h_ref, g_ref, win_ref, wout_ref, hnew_ref, hn_ref, a_ref, b_ref):
        x = h_ref[...]
        hn = _rms(x, g_ref[...]).astype(BF16)
        hn_ref[...] = hn
        acc = None
        for j in range(FFN_TILES):
            a = _dot_nt(hn, win_ref[0, j])
            b = _dot_nt(hn, win_ref[1, j])
            act = a * _sigmoid(a) * b
            a_ref[j] = a.astype(BF16)
            b_ref[j] = b.astype(BF16)
            part = _dot(act.astype(BF16), wout_ref[2 * j:2 * j + 2].reshape(FFN_SHARD, D))
            acc = part if acc is None else acc + part
        hnew_ref[...] = x + 0.5 * acc

    tt = FFN_TT
    resident = pl.Buffered(1)
    return pl.pallas_call(
        body,
        grid=(T // tt,),
        in_specs=[
            pl.BlockSpec((tt, D), lambda i: (i, 0)),
            pl.BlockSpec((1, D), lambda i: (0, 0)),
            pl.BlockSpec((2, FFN_TILES, FFN_SHARD, D), lambda i: (0, 0, 0, 0), pipeline_mode=resident),
            pl.BlockSpec((N_DEV, FFN_SHARD // 2, D), lambda i: (0, 0, 0), pipeline_mode=resident),
        ],
        out_specs=[
            pl.BlockSpec((tt, D), lambda i: (i, 0)),
            pl.BlockSpec((tt, D), lambda i: (i, 0)),
            pl.BlockSpec((FFN_TILES, tt, FFN_SHARD), lambda i: (0, i, 0)),
            pl.BlockSpec((FFN_TILES, tt, FFN_SHARD), lambda i: (0, i, 0)),
        ],
        out_shape=[
            jax.ShapeDtypeStruct((T, D), F32),
            jax.ShapeDtypeStruct((T, D), BF16),
            jax.ShapeDtypeStruct((FFN_TILES, T, FFN_SHARD), BF16),
            jax.ShapeDtypeStruct((FFN_TILES, T, FFN_SHARD), BF16),
        ],
        compiler_params=_cp("parallel"),
        name=f"ffn{tag}_fwd_{layer}",
    )(h, g_row, w_in, w_out)


def ffn_bwd(hn, a, b, dres, w_in, w_out, layer, tag=""):
    tt = FFN_TT
    nt = T // tt

    def body(hn_ref, a_ref, b_ref, dres_ref, win_ref, wout_ref, dhn_ref, dwin_ref, dwout_ref, gin_ref, gout_ref,
             do_s, act_s, da_s, db_s):
        i = pl.program_id(1)
        wo = wout_ref[...].reshape(FFN_SHARD, D)
        half = tt // 2
        for r0 in (0, half):
            rows = slice(r0, r0 + half)
            do_h = (0.5 * dres_ref[rows, :]).astype(BF16)
            do_s[rows, :] = do_h
            dact = _dot_nt(do_h, wo)
            a = a_ref[rows, :].astype(F32)
            b = b_ref[rows, :].astype(F32)
            sig = _sigmoid(a)
            s = a * sig
            da_h = (dact * b * _dsilu(a, sig)).astype(BF16)
            db_h = (dact * s).astype(BF16)
            act_s[rows, :] = (s * b).astype(BF16)
            da_s[rows, :] = da_h
            db_s[rows, :] = db_h
            dhn_ref[rows, :] = (_dot(da_h, win_ref[0]) + _dot(db_h, win_ref[1])).astype(BF16)
        do, act, da, db = do_s[...], act_s[...], da_s[...], db_s[...]
        hn = hn_ref[...]
        gwo = _dot_tn(act, do)
        gwg = _dot_tn(da, hn)
        gwu = _dot_tn(db, hn)

        @pl.when(i == 0)
        def _():
            gout_ref[...] = gwo
            gin_ref[0] = gwg
            gin_ref[1] = gwu

        @pl.when(i > 0)
        def _():
            gout_ref[...] += gwo
            gin_ref[0] += gwg
            gin_ref[1] += gwu

        @pl.when(i == nt - 1)
        def _():
            dwin_ref[...] = gin_ref[...].astype(BF16)
            dwout_ref[...] = gout_ref[...].astype(BF16)

    return pl.pallas_call(
        body,
        grid=(FFN_TILES, nt),
        in_specs=[
            pl.BlockSpec((tt, D), lambda j, i: (i, 0)),
            pl.BlockSpec((None, tt, FFN_SHARD), lambda j, i: (j, i, 0)),
            pl.BlockSpec((None, tt, FFN_SHARD), lambda j, i: (j, i, 0)),
            pl.BlockSpec((tt, D), lambda j, i: (i, 0)),
            pl.BlockSpec((2, None, FFN_SHARD, D), lambda j, i: (0, j, 0, 0)),
            pl.BlockSpec((2, FFN_SHARD // 2, D), lambda j, i: (j, 0, 0)),
        ],
        out_specs=[
            pl.BlockSpec((None, tt, D), lambda j, i: (j, i, 0)),
            pl.BlockSpec((2, None, FFN_SHARD, D), lambda j, i: (0, j, 0, 0)),
            pl.BlockSpec((None, FFN_SHARD, D), lambda j, i: (j, 0, 0)),
        ],
        out_shape=[
            jax.ShapeDtypeStruct((FFN_TILES, T, D), BF16),
            jax.ShapeDtypeStruct((2, FFN_TILES, FFN_SHARD, D), BF16),
            jax.ShapeDtypeStruct((FFN_TILES, FFN_SHARD, D), BF16),
        ],
        scratch_shapes=[pltpu.VMEM((2, FFN_SHARD, D), F32), pltpu.VMEM((FFN_SHARD, D), F32),
                        pltpu.VMEM((tt, D), BF16),
                        pltpu.VMEM((tt, FFN_SHARD), BF16), pltpu.VMEM((tt, FFN_SHARD), BF16),
                        pltpu.VMEM((tt, FFN_SHARD), BF16)],
        compiler_params=_cp("parallel", "arbitrary"),
        name=f"ffn{tag}_bwd_{layer}",
    )(hn, a, b, dres, w_in, w_out)


def _rms_bwd(x, g, dy):
    rstd = lax.rsqrt(jnp.mean(x * x, axis=-1, keepdims=True) + EPS)
    xh = x * rstd
    u = dy * g
    dx = rstd * (u - xh * jnp.mean(u * xh, axis=-1, keepdims=True))
    return dx, jnp.sum(dy * xh, axis=0, keepdims=True)


def norm_bwd(x, g_row, dy_parts, dres, name):
    p = dy_parts.shape[0]
    tt = 512

    def body(x_ref, g_ref, dy_ref, dres_ref, out_ref, dg_ref):
        i = pl.program_id(0)
        dy = dy_ref[0].astype(F32)
        for q in range(1, p):
            dy = dy + dy_ref[q].astype(F32)
        dx, dg = _rms_bwd(x_ref[...], g_ref[...], dy)
        out_ref[...] = dres_ref[...] + dx

        @pl.when(i == 0)
        def _():
            dg_ref[...] = dg

        @pl.when(i > 0)
        def _():
            dg_ref[...] += dg

    return pl.pallas_call(
        body,
        grid=(T // tt,),
        in_specs=[
            pl.BlockSpec((tt, D), lambda i: (i, 0)),
            pl.BlockSpec((1, D), lambda i: (0, 0)),
            pl.BlockSpec((p, tt, D), lambda i: (0, i, 0)),
            pl.BlockSpec((tt, D), lambda i: (i, 0)),
        ],
        out_specs=[pl.BlockSpec((tt, D), lambda i: (i, 0)), pl.BlockSpec((1, D), lambda i: (0, 0))],
        out_shape=[jax.ShapeDtypeStruct((T, D), F32), jax.ShapeDtypeStruct((1, D), F32)],
        compiler_params=_cp("arbitrary"),
        name=name,
    )(x, g_row, dy_parts, dres)


def final_loss(h, g_row, target):
    tt = 512

    def body(h_ref, g_ref, t_ref, dres_ref, dg_ref, loss_ref):
        i = pl.program_id(0)
        x = h_ref[...]
        g = g_ref[...]
        err = _rms(x, g) - t_ref[...]
        part = 0.5 * jnp.sum(jnp.mean(err * err, axis=-1, keepdims=True), axis=0, keepdims=True)
        dx, dg = _rms_bwd(x, g, err * (1.0 / D))
        dres_ref[...] = dx
        part = jnp.broadcast_to(part, loss_ref.shape)

        @pl.when(i == 0)
        def _():
            dg_ref[...] = dg
            loss_ref[...] = part

        @pl.when(i > 0)
        def _():
            dg_ref[...] += dg
            loss_ref[...] += part

    return pl.pallas_call(
        body,
        grid=(T // tt,),
        in_specs=[
            pl.BlockSpec((tt, D), lambda i: (i, 0)),
            pl.BlockSpec((1, D), lambda i: (0, 0)),
            pl.BlockSpec((tt, D), lambda i: (i, 0)),
        ],
        out_specs=[
            pl.BlockSpec((tt, D), lambda i: (i, 0)),
            pl.BlockSpec((1, D), lambda i: (0, 0)),
            pl.BlockSpec((8, 128), lambda i: (0, 0)),
        ],
        out_shape=[
            jax.ShapeDtypeStruct((T, D), F32),
            jax.ShapeDtypeStruct((1, D), F32),
            jax.ShapeDtypeStruct((8, 128), F32),
        ],
        compiler_params=_cp("arbitrary"),
        name="final_loss",
    )(h, g_row, target)


PROJ_TT = 256


def rope_tables(pos_col, invf_row):
    tt = 512

    def body(p_ref, f_ref, c_ref, sm_ref, sp_ref):
        ang = p_ref[...].astype(F32) * f_ref[...]
        lane = lax.broadcasted_iota(jnp.int32, ang.shape, 1) % A_HD
        cs = jnp.cos(ang)
        sn = jnp.sin(ang)
        c_ref[...] = jnp.where(lane < ROT, cs, 1.0)
        sm_ref[...] = jnp.where(lane < ROT // 2, -sn, 0.0)
        sp_ref[...] = jnp.where((lane >= ROT // 2) & (lane < ROT), sn, 0.0)

    spec = pl.BlockSpec((tt, 128), lambda i: (i, 0))
    return pl.pallas_call(
        body,
        grid=(T // tt,),
        in_specs=[pl.BlockSpec((tt, 1), lambda i: (i, 0)), pl.BlockSpec((1, 128), lambda i: (0, 0))],
        out_specs=[spec, spec, spec],
        out_shape=[jax.ShapeDtypeStruct((T, 128), F32)] * 3,
        compiler_params=_cp("parallel"),
        name="rope_tables",
    )(pos_col, invf_row)


def make_rope(positions):
    inv_freq = jnp.power(jnp.float32(ROPE_THETA), -jnp.arange(0, ROT, 2, dtype=F32) / ROT)
    per_head = jnp.concatenate([inv_freq, inv_freq, jnp.zeros((A_HD - ROT,), F32)])
    invf_row = jnp.tile(per_head, 2)[None, :]
    return tuple(rope_tables(positions.reshape(T, 1), invf_row))


def _rope(x, c, sm, sp):
    return x * c + pltpu.roll(x, 128 - ROT // 2, 1) * sm + pltpu.roll(x, ROT // 2, 1) * sp


def _rope_t(dy, c, sm, sp):
    return dy * c + pltpu.roll(dy * sm, ROT // 2, 1) + pltpu.roll(dy * sp, 128 - ROT // 2, 1)


def proj_fwd(h, g_row, w, splits, name, rope=None):
    tt = PROJ_TT
    n = w.shape[1]
    n_rope = 0 if rope is None else 3

    def body(h_ref, g_ref, w_ref, *rest):
        tabs = rest[:n_rope]
        hn_ref = rest[n_rope]
        outs = rest[n_rope + 1:]
        hn = _rms(h_ref[...], g_ref[...]).astype(BF16)
        hn_ref[...] = hn
        for k, ((st, wd), o_ref) in enumerate(zip(splits, outs)):
            if rope is not None and k == 0:
                c, sm, sp = (t[...] for t in tabs)
                for gi in range(wd // 128):
                    r = _dot(hn, w_ref[:, st + 128 * gi:st + 128 * (gi + 1)])
                    if gi < 2 * A_W // 128:
                        r = _rope(r, c, sm, sp)
                    o_ref[:, 128 * gi:128 * (gi + 1)] = r
            else:
                o_ref[...] = _dot(hn, w_ref[:, st:st + wd])

    tab_specs = [pl.BlockSpec((tt, 128), lambda i: (i, 0))] * n_rope
    return pl.pallas_call(
        body,
        grid=(T // tt,),
        in_specs=[
            pl.BlockSpec((tt, D), lambda i: (i, 0)),
            pl.BlockSpec((1, D), lambda i: (0, 0)),
            pl.BlockSpec((D, n), lambda i: (0, 0)),
        ] + tab_specs,
        out_specs=[pl.BlockSpec((tt, D), lambda i: (i, 0))]
        + [pl.BlockSpec((tt, wd), lambda i: (i, 0)) for _, wd in splits],
        out_shape=[jax.ShapeDtypeStruct((T, D), BF16)]
        + [jax.ShapeDtypeStruct((T, wd), F32) for _, wd in splits],
        compiler_params=_cp("parallel"),
        name=name,
    )(h, g_row, w, *(rope or ()))


def proj_bwd_data(x, g_row, w, dparts, splits, dres, name, rope=None, n_rot=0):
    tt = PROJ_TT
    n = w.shape[1]
    n_rope = 0 if rope is None else 3
    k_parts = len(dparts)
    n_main = sum(wd for _, wd in splits if wd % 128 == 0)

    def body(x_ref, g_ref, w_ref, dres_ref, *rest):
        d_refs = rest[:k_parts]
        tabs = rest[k_parts:k_parts + n_rope]
        out_ref, dg_ref, dproj_ref = rest[k_parts + n_rope:k_parts + n_rope + 3]
        i = pl.program_id(0)
        dhn = jnp.zeros((tt, D), F32)
        for k, ((st, wd), d_ref) in enumerate(zip(splits, d_refs)):
            if k < n_rot:
                c, sm, sp = (t[...] for t in tabs)
                for gi in range(wd // 128):
                    cols = slice(st + 128 * gi, st + 128 * (gi + 1))
                    d = _rope_t(d_ref[:, 128 * gi:128 * (gi + 1)], c, sm, sp).astype(BF16)
                    dproj_ref[:, cols] = d
                    dhn = dhn + _dot_nt(d, w_ref[:, cols])
            else:
                d = d_ref[...].astype(BF16)
                if wd % 128 == 0:
                    dproj_ref[:, st:st + wd] = d
                dhn = dhn + _dot_nt(d, w_ref[:, st:st + wd])
        dx, dg = _rms_bwd(x_ref[...], g_ref[...], dhn)
        out_ref[...] = dres_ref[...] + dx

        @pl.when(i == 0)
        def _():
            dg_ref[...] = dg

        @pl.when(i > 0)
        def _():
            dg_ref[...] += dg

    tab_specs = [pl.BlockSpec((tt, 128), lambda i: (i, 0))] * n_rope
    out_specs = [pl.BlockSpec((tt, D), lambda i: (i, 0)), pl.BlockSpec((1, D), lambda i: (0, 0))]
    out_shape = [jax.ShapeDtypeStruct((T, D), F32), jax.ShapeDtypeStruct((1, D), F32)]
    out_specs.append(pl.BlockSpec((tt, n_main), lambda i: (i, 0)))
    out_shape.append(jax.ShapeDtypeStruct((T, n_main), BF16))
    return pl.pallas_call(
        body,
        grid=(T // tt,),
        in_specs=[
            pl.BlockSpec((tt, D), lambda i: (i, 0)),
            pl.BlockSpec((1, D), lambda i: (0, 0)),
            pl.BlockSpec((D, n), lambda i: (0, 0)),
            pl.BlockSpec((tt, D), lambda i: (i, 0)),
        ] + [pl.BlockSpec((tt, wd), lambda i: (i, 0)) for _, wd in splits] + tab_specs,
        out_specs=out_specs,
        out_shape=out_shape,
        compiler_params=_cp("arbitrary"),
        name=name,
    )(x, g_row, w, dres, *dparts, *(rope or ()))


def mm_tn_full(x, d, out_dtype, name):
    k = x.shape[1]
    n = d.shape[1]
    wn = 512

    def body(x_ref, d_ref, o_ref):
        o_ref[...] = _dot_tn(x_ref[...], d_ref[...].astype(BF16)).astype(out_dtype)

    return pl.pallas_call(
        body,
        grid=(n // wn,),
        in_specs=[pl.BlockSpec((T, k), lambda j: (0, 0), pipeline_mode=pl.Buffered(1)),
                  pl.BlockSpec((T, wn), lambda j: (0, j))],
        out_specs=pl.BlockSpec((k, wn), lambda j: (0, j)),
        out_shape=jax.ShapeDtypeStruct((k, n), out_dtype),
        compiler_params=_cp("parallel"),
        name=name,
    )(x, d)


def cols_to_slabs(main, tail, name):
    nm = main.shape[1]
    n = nm + (0 if tail is None else tail.shape[1])
    w = n // N_DEV
    tr = 256

    def body(*refs):
        m_ref, o_ref = refs[0], refs[-1]
        for s in range(N_DEV):
            a, b = w * s, w * (s + 1)
            if b <= nm:
                o_ref[s] = m_ref[:, a:b].astype(BF16)
            else:
                o_ref[s, :, 0:nm - a] = m_ref[:, a:nm].astype(BF16)
                o_ref[s, :, nm - a:w] = refs[1][:, 0:b - nm].astype(BF16)

    arrays = [main] + ([] if tail is None else [tail])
    return pl.pallas_call(
        body,
        grid=(D // tr,),
        in_specs=[pl.BlockSpec((tr, a.shape[1]), lambda i: (i, 0)) for a in arrays],
        out_specs=pl.BlockSpec((N_DEV, tr, w), lambda i: (0, i, 0)),
        out_shape=jax.ShapeDtypeStruct((N_DEV, D, w), BF16),
        compiler_params=_cp("parallel"),
        name=name,
    )(*arrays)


def mm_tn(x, d, name):
    k = x.shape[1]
    n = d.shape[1]
    wn = n if n <= 512 else 512
    tt = 512

    def body(x_ref, d_ref, o_ref):
        i = pl.program_id(1)
        r = _dot_tn(x_ref[...], d_ref[...].astype(BF16))

        @pl.when(i == 0)
        def _():
            o_ref[...] = r

        @pl.when(i > 0)
        def _():
            o_ref[...] += r

    return pl.pallas_call(
        body,
        grid=(n // wn, T // tt),
        in_specs=[pl.BlockSpec((tt, k), lambda j, i: (i, 0)), pl.BlockSpec((tt, wn), lambda j, i: (i, j))],
        out_specs=pl.BlockSpec((k, wn), lambda j, i: (0, j)),
        out_shape=jax.ShapeDtypeStruct((k, n), F32),
        compiler_params=_cp("parallel", "arbitrary"),
        name=name,
    )(x, d)


CONV_RC = 128
CONV_PAD = 32


def hyb_conv_fwd(u, dw_w, dw_b, name):
    def body(ua_ref, ug_ref, w_ref, b_ref, o_ref, xpad):
        xpad[0:CONV_PAD, :] = jnp.zeros((CONV_PAD, 128), F32)
        xpad[CONV_PAD:, :] = ua_ref[...] * _sigmoid(ug_ref[...])
        for r in range(T // CONV_RC):
            acc = jnp.broadcast_to(b_ref[...], (CONV_RC, 128))
            for j in range(CONV_K):
                acc = acc + w_ref[pl.ds(j, 1), :] * xpad[pl.ds(r * CONV_RC + CONV_PAD - (CONV_K - 1) + j, CONV_RC), :]
            o_ref[r * CONV_RC:(r + 1) * CONV_RC, :] = acc

    nb = CONV_C // 128
    return pl.pallas_call(
        body,
        grid=(nb,),
        in_specs=[
            pl.BlockSpec((T, 128), lambda c: (0, c)),
            pl.BlockSpec((T, 128), lambda c: (0, nb + c)),
            pl.BlockSpec((32, 128), lambda c: (0, c)),
            pl.BlockSpec((1, 128), lambda c: (0, c)),
        ],
        out_specs=pl.BlockSpec((T, 128), lambda c: (0, c)),
        out_shape=jax.ShapeDtypeStruct((T, CONV_C), F32),
        scratch_shapes=[pltpu.VMEM((T + CONV_PAD, 128), F32)],
        compiler_params=_cp("parallel"),
        name=name,
    )(u, u, dw_w, dw_b)


def hyb_conv_bwd(dc, u, dw_w, name):
    def body(dc_ref, ua_ref, ug_ref, w_ref, da_ref, dgate_ref, dw_ref, db_ref, xpad, dcpad, dwacc):
        ua = ua_ref[...]
        sig = _sigmoid(ug_ref[...])
        xpad[0:CONV_PAD, :] = jnp.zeros((CONV_PAD, 128), F32)
        xpad[CONV_PAD:, :] = ua * sig
        dcpad[0:T, :] = dc_ref[...]
        dcpad[T:, :] = jnp.zeros((CONV_PAD, 128), F32)
        dwacc[...] = jnp.zeros_like(dwacc)
        dbacc = jnp.zeros((8, 128), F32)
        for r in range(T // CONV_RC):
            r0 = r * CONV_RC
            dcr = dc_ref[r0:r0 + CONV_RC, :]
            dbacc = dbacc + dcr.reshape(CONV_RC // 8, 8, 128).sum(axis=0)
            dglu = jnp.zeros((CONV_RC, 128), F32)
            for j in range(CONV_K):
                dglu = dglu + w_ref[pl.ds(j, 1), :] * dcpad[pl.ds(r0 + (CONV_K - 1) - j, CONV_RC), :]
                prod = dcr * xpad[pl.ds(r0 + CONV_PAD - (CONV_K - 1) + j, CONV_RC), :]
                dwacc[8 * j:8 * j + 8, :] += prod.reshape(CONV_RC // 8, 8, 128).sum(axis=0)
            sg = sig[r0:r0 + CONV_RC, :]
            da_ref[r0:r0 + CONV_RC, :] = dglu * sg
            dgate_ref[r0:r0 + CONV_RC, :] = dglu * ua[r0:r0 + CONV_RC, :] * sg * (1.0 - sg)
        for j in range(CONV_K):
            dw_ref[pl.ds(j, 1), :] = jnp.sum(dwacc[8 * j:8 * j + 8, :], axis=0, keepdims=True)
        dw_ref[pl.ds(CONV_K, 1), :] = jnp.zeros((1, 128), F32)
        db_ref[...] = jnp.sum(dbacc, axis=0, keepdims=True)

    nb = CONV_C // 128
    col = pl.BlockSpec((T, 128), lambda c: (0, c))
    return pl.pallas_call(
        body,
        grid=(nb,),
        in_specs=[col, col, pl.BlockSpec((T, 128), lambda c: (0, nb + c)), pl.BlockSpec((32, 128), lambda c: (0, c))],
        out_specs=[col, col, pl.BlockSpec((32, 128), lambda c: (0, c)), pl.BlockSpec((1, 128), lambda c: (0, c))],
        out_shape=[
            jax.ShapeDtypeStruct((T, CONV_C), F32),
            jax.ShapeDtypeStruct((T, CONV_C), F32),
            jax.ShapeDtypeStruct((32, CONV_C), F32),
            jax.ShapeDtypeStruct((1, CONV_C), F32),
        ],
        scratch_shapes=[
            pltpu.VMEM((T + CONV_PAD, 128), F32),
            pltpu.VMEM((T + CONV_PAD, 128), F32),
            pltpu.VMEM((8 * 32, 128), F32),
        ],
        compiler_params=_cp("parallel"),
        name=name,
    )(dc, u, u, dw_w)


ATT_SCALE = A_HD ** -0.5
N_BLK = T // BLK


def _att_masks():
    i = lax.broadcasted_iota(jnp.int32, (BLK, 2 * BLK), 0)
    j = lax.broadcasted_iota(jnp.int32, (BLK, 2 * BLK), 1)
    band = (j >= i) & (j <= i + BLK)
    i1 = lax.broadcasted_iota(jnp.int32, (BLK, BLK), 0)
    j1 = lax.broadcasted_iota(jnp.int32, (BLK, BLK), 1)
    return band, j1 <= i1


def _att_rows(d, t, first):
    if first:
        base = t
        return pl.ds(base, BLK, stride=d), pl.ds(base, BLK, stride=d)
    c = t % d
    n = t // d + 1
    base = c + (BLK * d) * n
    return pl.ds(base, BLK, stride=d), pl.ds(base - BLK * d, 2 * BLK, stride=d)


def _stack_heads(x, head0):
    return jnp.concatenate([jnp.where(head0, x, 0.0), jnp.where(head0, 0.0, x)], axis=0)


def _loop_pairs(n, block, per=2):
    def several(i, carry):
        for k in range(per):
            block(per * i + k, carry)
        return carry

    if n >= per:
        lax.fori_loop(0, n // per, several, 0)
    for t in range(n - n % per, n):
        block(t, 0)


def attn_fwd(qkv, name):
    def body(q_ref, k_ref, v_ref, o_ref, lse_ref, og, lg):
        band, tri = _att_masks()
        band, tri = jnp.concatenate([band, band], axis=0), jnp.concatenate([tri, tri], axis=0)
        head0 = lax.broadcasted_iota(jnp.int32, (BLK, 128), 1) < A_HD
        for g, d in enumerate(DILATIONS):
            def block(t, carry, first, g=g, d=d):
                rq, rk = _att_rows(d, t, first)
                q2 = q_ref[rq, :]
                k2 = k_ref[rk, :].astype(BF16)
                v2 = v_ref[rk, :].astype(BF16)
                qs = _stack_heads(q2, head0).astype(BF16)
                s = _dot_nt(qs, k2) * ATT_SCALE
                s = jnp.where(tri if first else band, s, -jnp.inf)
                m = jnp.max(s, axis=-1, keepdims=True)
                p = jnp.exp(s - m)
                den = jnp.sum(p, axis=-1, keepdims=True)
                o = _dot(p.astype(BF16), v2) / den
                l = m + jnp.log(den)
                og[g, rq, :] = jnp.where(head0, o[0:BLK], o[BLK:2 * BLK])
                lg[g, rq, :] = jnp.where(head0, l[0:BLK], l[BLK:2 * BLK])
                return carry

            _loop_pairs(d, functools.partial(block, first=True), per=8)
            _loop_pairs(N_BLK - d, functools.partial(block, first=False), per=8)
        rc = 256
        for r in range(T // rc):
            rows = pl.ds(r * rc, rc)
            l0, l1, l2 = lg[0, rows, :], lg[1, rows, :], lg[2, rows, :]
            m = jnp.maximum(jnp.maximum(l0, l1), l2)
            e0, e1, e2 = jnp.exp(l0 - m), jnp.exp(l1 - m), jnp.exp(l2 - m)
            z = e0 + e1 + e2
            o_ref[rows, :] = (e0 / z) * og[0, rows, :] + (e1 / z) * og[1, rows, :] + (e2 / z) * og[2, rows, :]
            lse_ref[rows, :] = m + jnp.log(z)

    npair = A_HEADS // 2
    col = lambda off: pl.BlockSpec((T, 128), lambda p: (0, off + p))
    return pl.pallas_call(
        body,
        grid=(npair,),
        in_specs=[col(0), col(npair), col(2 * npair)],
        out_specs=[col(0), col(0)],
        out_shape=[jax.ShapeDtypeStruct((T, A_W), F32), jax.ShapeDtypeStruct((T, A_W), F32)],
        scratch_shapes=[pltpu.VMEM((3, T, 128), F32), pltpu.VMEM((3, T, 128), F32)],
        compiler_params=_cp("parallel"),
        name=name,
    )(qkv, qkv, qkv)


def attn_bwd(qkv, o, lse, do, name):
    def body(q_ref, k_ref, v_ref, o_ref, lse_ref, do_ref, dq_ref, dk_ref, dv_ref):
        band, tri = _att_masks()
        band, tri = jnp.concatenate([band, band], axis=0), jnp.concatenate([tri, tri], axis=0)
        head0 = lax.broadcasted_iota(jnp.int32, (BLK, 128), 1) < A_HD
        dq_ref[...] = jnp.zeros_like(dq_ref)
        dk_ref[...] = jnp.zeros_like(dk_ref)
        dv_ref[...] = jnp.zeros_like(dv_ref)
        for d in DILATIONS:
            def block(t, carry, first, d=d):
                rq, rk = _att_rows(d, t, first)
                k2 = k_ref[rk, :].astype(BF16)
                v2 = v_ref[rk, :].astype(BF16)
                do2 = do_ref[rq, :]
                l2 = lse_ref[rq, :]
                qs = _stack_heads(q_ref[rq, :], head0).astype(BF16)
                dos = _stack_heads(do2, head0).astype(BF16)
                l = jnp.concatenate([l2[:, 0:1], l2[:, A_HD:A_HD + 1]], axis=0)
                dd = jnp.sum(_stack_heads(do2 * o_ref[rq, :], head0), axis=-1, keepdims=True)
                s = _dot_nt(qs, k2) * ATT_SCALE
                p = jnp.where(tri if first else band, jnp.exp(s - l), 0.0)
                dp = _dot_nt(dos, v2)
                ds = (p * (dp - dd) * ATT_SCALE).astype(BF16)
                dq = _dot(ds, k2)
                dq_ref[rq, :] += jnp.where(head0, dq[0:BLK], dq[BLK:2 * BLK])
                dk_ref[rk, :] += _dot_tn(ds, qs)
                dv_ref[rk, :] += _dot_tn(p.astype(BF16), dos)
                return carry

            _loop_pairs(d, functools.partial(block, first=True), per=4)
            _loop_pairs(N_BLK - d, functools.partial(block, first=False), per=4)

    npair = A_HEADS // 2
    col = lambda off: pl.BlockSpec((T, 128), lambda p: (0, off + p))
    return pl.pallas_call(
        body,
        grid=(npair,),
        in_specs=[col(0), col(npair), col(2 * npair), col(0), col(0), col(0)],
        out_specs=[col(0), col(0), col(0)],
        out_shape=[jax.ShapeDtypeStruct((T, A_W), F32)] * 3,
        compiler_params=_cp("parallel"),
        name=name,
    )(qkv, qkv, qkv, o, lse, do)


def _ln_silu(x, g, b):
    mu = jnp.mean(x, axis=-1, keepdims=True)
    xc = x - mu
    rstd = lax.rsqrt(jnp.mean(xc * xc, axis=-1, keepdims=True) + EPS)
    xh = xc * rstd
    y = xh * g + b
    sig = _sigmoid(y)
    return y * sig, (xh, rstd, y, sig)


def hyb_out_fwd(h, attn, cpre, ln_g, ln_b, w_out, name):
    tt = 512

    def body(h_ref, a_ref, c_ref, g_ref, b_ref, w_ref, hnew_ref, cat_ref):
        cn, _ = _ln_silu(c_ref[...], g_ref[...], b_ref[...])
        ab = a_ref[...].astype(BF16)
        cb = cn.astype(BF16)
        cat_ref[:, 0:A_W] = ab
        cat_ref[:, A_W:D] = cb
        hnew_ref[...] = h_ref[...] + _dot(ab, w_ref[0:A_W, :]) + _dot(cb, w_ref[A_W:D, :])

    half = pl.BlockSpec((tt, A_W), lambda i: (i, 0))
    vec = pl.BlockSpec((1, CONV_C), lambda i: (0, 0))
    full = pl.BlockSpec((tt, D), lambda i: (i, 0))
    return pl.pallas_call(
        body,
        grid=(T // tt,),
        in_specs=[full, half, half, vec, vec, pl.BlockSpec((D, D), lambda i: (0, 0))],
        out_specs=[full, full],
        out_shape=[jax.ShapeDtypeStruct((T, D), F32), jax.ShapeDtypeStruct((T, D), BF16)],
        compiler_params=_cp("parallel"),
        name=name,
    )(h, attn, cpre, ln_g, ln_b, w_out)


def hyb_out_bwd(dres, cpre, ln_g, ln_b, w_out, name):
    tt = 512

    def body(d_ref, c_ref, g_ref, b_ref, w_ref, da_ref, dc_ref, dg_ref, db_ref):
        i = pl.program_id(0)
        db16 = d_ref[...].astype(BF16)
        da_ref[...] = _dot_nt(db16, w_ref[0:A_W, :])
        dcn = _dot_nt(db16, w_ref[A_W:D, :])
        g = g_ref[...]
        _, (xh, rstd, y, sig) = _ln_silu(c_ref[...], g, b_ref[...])
        dy = dcn * _dsilu(y, sig)
        dxh = dy * g
        dc_ref[...] = rstd * (dxh - jnp.mean(dxh, axis=-1, keepdims=True)
                              - xh * jnp.mean(dxh * xh, axis=-1, keepdims=True))
        dg = jnp.sum(dy * xh, axis=0, keepdims=True)
        db = jnp.sum(dy, axis=0, keepdims=True)

        @pl.when(i == 0)
        def _():
            dg_ref[...] = dg
            db_ref[...] = db

        @pl.when(i > 0)
        def _():
            dg_ref[...] += dg
            db_ref[...] += db

    half = pl.BlockSpec((tt, A_W), lambda i: (i, 0))
    vec = pl.BlockSpec((1, CONV_C), lambda i: (0, 0))
    return pl.pallas_call(
        body,
        grid=(T // tt,),
        in_specs=[pl.BlockSpec((tt, D), lambda i: (i, 0)), half, vec, vec, pl.BlockSpec((D, D), lambda i: (0, 0))],
        out_specs=[half, half, vec, vec],
        out_shape=[
            jax.ShapeDtypeStruct((T, A_W), F32),
            jax.ShapeDtypeStruct((T, CONV_C), F32),
            jax.ShapeDtypeStruct((1, CONV_C), F32),
            jax.ShapeDtypeStruct((1, CONV_C), F32),
        ],
        compiler_params=_cp("arbitrary"),
        name=name,
    )(dres, cpre, ln_g, ln_b, w_out)


def hybrid_fwd(h, g_row, w_in, dw_w, dw_b, ln_g, ln_b, w_out, rope, tag):
    hn, qkv, u = proj_fwd(h, g_row, w_in, [(0, 3 * A_W), (3 * A_W, 2 * CONV_C)], f"hyb_proj_{tag}", rope=rope)
    cpre = hyb_conv_fwd(u, dw_w, dw_b, f"hyb_conv_{tag}")
    attn, lse = attn_fwd(qkv, f"attn_fwd_{tag}")
    hnew, cat = hyb_out_fwd(h, attn, cpre, ln_g, ln_b, w_out, f"hyb_out_{tag}")
    return hnew, (h, hn, qkv, u, cpre, attn, lse, cat)


def hybrid_bwd(dres, saved, g_row, w_in, dw_w, ln_g, ln_b, w_out, rope, tag):
    h, hn, qkv, u, cpre, attn, lse, cat = saved
    d_attn, d_cpre, d_lng, d_lnb = hyb_out_bwd(dres, cpre, ln_g, ln_b, w_out, f"hyb_out_bwd_{tag}")
    d_wout = mm_tn_full(cat, dres, BF16, f"hyb_wout_grad_{tag}")
    d_a, d_gate, d_dw, d_db = hyb_conv_bwd(d_cpre, u, dw_w, f"hyb_conv_bwd_{tag}")
    dq, dk, dv = attn_bwd(qkv, attn, lse, d_attn, f"attn_bwd_{tag}")
    splits = [(0, A_W), (A_W, A_W), (2 * A_W, A_W), (3 * A_W, CONV_C), (3 * A_W + CONV_C, CONV_C)]
    dres_new, d_norm, dproj = proj_bwd_data(
        h, g_row, w_in, [dq, dk, dv, d_a, d_gate], splits, dres, f"hyb_proj_bwd_{tag}", rope=rope, n_rot=2)
    d_win = cols_to_slabs(mm_tn_full(hn, dproj, F32, f"hyb_win_grad_{tag}"), None, f"hyb_win_slabs_{tag}")
    return dres_new, dict(norm=d_norm, w_in=d_win, dw_w=d_dw[:CONV_K], dw_b=d_db, ln_g=d_lng, ln_b=d_lnb, w_out=d_wout)


G_SCALE = G_DK ** -0.5
GP_RC = 256
GP_PAD = 8


def gdn_prep_fwd(x, conv_w, name):
    def body(x_ref, w_ref, o_ref, xpad):
        cb = pl.program_id(0)
        xpad[0:GP_PAD, :] = jnp.zeros((GP_PAD, 128), F32)
        xpad[GP_PAD:, :] = x_ref[...]
        for r in range(T // GP_RC):
            r0 = r * GP_RC
            y = jnp.zeros((GP_RC, 128), F32)
            for j in range(G_CONV):
                y = y + w_ref[pl.ds(j, 1), :] * xpad[pl.ds(r0 + GP_PAD - (G_CONV - 1) + j, GP_RC), :]
            s = y * _sigmoid(y)
            n = lax.rsqrt(jnp.sum(s * s, axis=-1, keepdims=True) + EPS)
            o_ref[r0:r0 + GP_RC, :] = s * jnp.where(cb < 2 * G_HEADS, n, 1.0)

    nb = G_QKV // 128
    return pl.pallas_call(
        body,
        grid=(nb,),
        in_specs=[pl.BlockSpec((T, 128), lambda c: (0, c)), pl.BlockSpec((G_CONV, 128), lambda c: (0, c))],
        out_specs=pl.BlockSpec((T, 128), lambda c: (0, c)),
        out_shape=jax.ShapeDtypeStruct((T, G_QKV), F32),
        scratch_shapes=[pltpu.VMEM((T + GP_PAD, 128), F32)],
        compiler_params=_cp("parallel"),
        name=name,
    )(x, conv_w)


def gdn_prep_bwd(dout, x, conv_w, part, l2, name):
    def body(d_ref, x_ref, w_ref, dx_ref, dw_ref, xpad, dypad, dwacc):
        xpad[0:GP_PAD, :] = jnp.zeros((GP_PAD, 128), F32)
        xpad[GP_PAD:, :] = x_ref[...]
        dypad[T:, :] = jnp.zeros((GP_PAD, 128), F32)
        dwacc[...] = jnp.zeros_like(dwacc)
        for r in range(T // GP_RC):
            r0 = r * GP_RC
            y = jnp.zeros((GP_RC, 128), F32)
            xs = []
            for j in range(G_CONV):
                xj = xpad[pl.ds(r0 + GP_PAD - (G_CONV - 1) + j, GP_RC), :]
                xs.append(xj)
                y = y + w_ref[pl.ds(j, 1), :] * xj
            sig = _sigmoid(y)
            s = y * sig
            d = d_ref[r0:r0 + GP_RC, :]
            if l2:
                n = lax.rsqrt(jnp.sum(s * s, axis=-1, keepdims=True) + EPS)
                out = s * n
                d = n * (d - out * jnp.sum(d * out, axis=-1, keepdims=True))
            dy = d * _dsilu(y, sig)
            dypad[r0:r0 + GP_RC, :] = dy
            for j in range(G_CONV):
                dwacc[8 * j:8 * j + 8, :] += (dy * xs[j]).reshape(GP_RC // 8, 8, 128).sum(axis=0)
        for r in range(T // GP_RC):
            r0 = r * GP_RC
            dx = jnp.zeros((GP_RC, 128), F32)
            for j in range(G_CONV):
                dx = dx + w_ref[pl.ds(j, 1), :] * dypad[pl.ds(r0 + (G_CONV - 1) - j, GP_RC), :]
            dx_ref[r0:r0 + GP_RC, :] = dx
        for j in range(G_CONV):
            dw_ref[pl.ds(j, 1), :] = jnp.sum(dwacc[8 * j:8 * j + 8, :], axis=0, keepdims=True)

    nb = G_HEADS
    off = part * nb
    col = pl.BlockSpec((T, 128), lambda c: (0, c))
    return pl.pallas_call(
        body,
        grid=(nb,),
        in_specs=[col, pl.BlockSpec((T, 128), lambda c: (0, off + c)), pl.BlockSpec((G_CONV, 128), lambda c: (0, off + c))],
        out_specs=[col, pl.BlockSpec((G_CONV, 128), lambda c: (0, c))],
        out_shape=[jax.ShapeDtypeStruct((T, G_HEADS * G_DK), F32), jax.ShapeDtypeStruct((G_CONV, G_HEADS * G_DK), F32)],
        scratch_shapes=[
            pltpu.VMEM((T + GP_PAD, 128), F32),
            pltpu.VMEM((T + GP_PAD, 128), F32),
            pltpu.VMEM((8 * G_CONV, 128), F32),
        ],
        compiler_params=_cp("parallel"),
        name=name,
    )(dout, x, conv_w)


def _seg_cumsum(x, reverse=False):
    row = lax.broadcasted_iota(jnp.int32, x.shape, 0) % CH
    s = 1
    while s < CH:
        if reverse:
            x = x + jnp.where(row < CH - s, pltpu.roll(x, x.shape[0] - s, 0), 0.0)
        else:
            x = x + jnp.where(row >= s, pltpu.roll(x, s, 0), 0.0)
        s *= 2
    return x


def _gdn_gates(ba_ref, alog_ref, dt_ref, h):
    ba = ba_ref[...]
    lane = lax.broadcasted_iota(jnp.int32, ba.shape, 1)
    b_col = jnp.sum(jnp.where(lane == h, ba, 0.0), axis=1, keepdims=True)
    a_col = jnp.sum(jnp.where(lane == G_HEADS + h, ba, 0.0), axis=1, keepdims=True)
    lane8 = lax.broadcasted_iota(jnp.int32, (1, G_HEADS), 1)
    alog = jnp.sum(jnp.where(lane8 == h, alog_ref[...], 0.0), axis=1, keepdims=True)
    dt = jnp.sum(jnp.where(lane8 == h, dt_ref[...], 0.0), axis=1, keepdims=True)
    beta = _sigmoid(b_col)
    xa = a_col + dt
    softplus = jnp.maximum(xa, 0.0) + jnp.log(1.0 + jnp.exp(-jnp.abs(xa)))
    ea = jnp.exp(alog)
    return beta, -ea * softplus, xa, ea


def _chunk_masks():
    i = lax.broadcasted_iota(jnp.int32, (CH, CH), 0)
    j = lax.broadcasted_iota(jnp.int32, (CH, CH), 1)
    return i >= j, i > j, i, j


def _decay(gcc, causal):
    gm = gcc[:, 0:CH]
    return jnp.where(causal, jnp.exp(jnp.minimum(gm - gm.T, 0.0)), 0.0)


def _split(a):
    hi = a.astype(BF16)
    return hi, (a - hi.astype(F32)).astype(BF16)


def _dot3(a, b):
    ah, al = _split(a)
    bh, bl = _split(b)
    return _dot(ah, bh) + (_dot(ah, bl) + _dot(al, bh))


def _unit_lower_inverse(lms, i, j):
    eye = jnp.where(i == j, 1.0, 0.0)
    ms = [None] * len(lms)
    b = 1
    while b < CH:
        pair = ((i // (2 * b)) == (j // (2 * b))) & ((i // b) % 2 == 1) & ((j // b) % 2 == 0)
        lbs = [jnp.where(pair, lm, 0.0) for lm in lms]
        if b == 1:
            ms = [eye - lb for lb in lbs]
        else:
            ts = [_dot3(m, lb) for m, lb in zip(ms, lbs)]
            ms = [m - _dot3(t, m) for m, t in zip(ms, ts)]
        b *= 2
    return ms


def gdn_local_fwd(qkv, ba, alog, dtb, name):
    def body(q_ref, k_ref, v_ref, ba_ref, al_ref, dt_ref, u_ref, w_ref, qd_ref, kd_ref, at_ref, el_ref, ti_ref, gcs):
        h = pl.program_id(1)
        beta, g, _, _ = _gdn_gates(ba_ref, al_ref, dt_ref, h)
        gc = _seg_cumsum(jnp.broadcast_to(g, (GRP, 128)))
        gcs[...] = gc
        causal, strict, i, j = _chunk_masks()
        lms = []
        for c in range(CPG):
            r = slice(c * CH, (c + 1) * CH)
            q, k = q_ref[r, :], k_ref[r, :]
            gcc = gc[r, :]
            ec = jnp.exp(gcc)
            gl = gcs[pl.ds(c * CH + CH - 1, 1), :]
            dm = _decay(gcc, causal)
            kbf = k.astype(BF16)
            a1 = _dot_nt((k * beta[r, :]).astype(BF16), kbf)
            lms.append(jnp.where(strict, a1 * dm, 0.0))
            qs = q * G_SCALE
            qd_ref[r, :] = (qs * ec).astype(BF16)
            kd_ref[r, :] = (k * jnp.exp(gl - gcc)).astype(BF16)
            at_ref[r, :] = (_dot_nt(qs.astype(BF16), kbf) * dm).astype(BF16)
            el_ref[pl.ds(c, 1), :] = jnp.exp(gl)
        tinvs = _unit_lower_inverse(lms, i, j)
        for c in range(CPG):
            r = slice(c * CH, (c + 1) * CH)
            bt = beta[r, :]
            tb = tinvs[c].astype(BF16)
            u_ref[r, :] = _dot(tb, (v_ref[r, :] * bt).astype(BF16))
            w_ref[r, :] = _dot(tb, (k_ref[r, :] * bt * jnp.exp(gc[r, :])).astype(BF16)).astype(BF16)
            ti_ref[r, :] = tinvs[c]

    hd = lambda off: pl.BlockSpec((GRP, 128), lambda i, h: (i, off + h))
    vec = pl.BlockSpec((1, G_HEADS), lambda i, h: (0, 0))
    sq = pl.BlockSpec((None, GRP, CH), lambda i, h: (h, i, 0))
    return pl.pallas_call(
        body,
        grid=(N_GRP, G_HEADS),
        in_specs=[hd(0), hd(G_HEADS), hd(2 * G_HEADS), pl.BlockSpec((GRP, 2 * G_HEADS), lambda i, h: (i, 0)), vec, vec],
        out_specs=[hd(0), hd(0), hd(0), hd(0), sq, pl.BlockSpec((None, CPG, 128), lambda i, h: (h, i, 0)), sq],
        out_shape=[
            jax.ShapeDtypeStruct((T, D), F32),
            jax.ShapeDtypeStruct((T, D), BF16),
            jax.ShapeDtypeStruct((T, D), BF16),
            jax.ShapeDtypeStruct((T, D), BF16),
            jax.ShapeDtypeStruct((G_HEADS, T, CH), BF16),
            jax.ShapeDtypeStruct((G_HEADS, T // CH, 128), F32),
            jax.ShapeDtypeStruct((G_HEADS, T, CH), F32),
        ],
        scratch_shapes=[pltpu.VMEM((GRP, 128), F32)],
        compiler_params=_cp("parallel", "parallel"),
        name=name,
    )(qkv, qkv, qkv, ba, alog, dtb)


def gdn_rec_fwd(u, w, qd, kd, at, el, name):
    def body(u_ref, w_ref, qd_ref, kd_ref, at_ref, el_ref, o_ref, vn_ref, st_ref, s_scr):
        @pl.when(pl.program_id(0) == 0)
        def _():
            s_scr[...] = jnp.zeros_like(s_scr)

        states = [s_scr[h] for h in range(G_HEADS)]
        heads = range(G_HEADS)
        lns = [slice(h * 128, (h + 1) * 128) for h in heads]
        for c in range(CPG):
            r = slice(c * CH, (c + 1) * CH)
            for h in heads:
                st_ref[h, c] = states[h]
            sbs = [states[h].astype(BF16) for h in heads]
            ws = [_dot(w_ref[r, lns[h]], sbs[h]) for h in heads]
            qs = [_dot(qd_ref[r, lns[h]], sbs[h]) for h in heads]
            vns = [(u_ref[r, lns[h]] - ws[h]).astype(BF16) for h in heads]
            avs = [_dot(at_ref[h, r, :], vns[h]) for h in heads]
            kvs = [_dot_tn(kd_ref[r, lns[h]], vns[h]) for h in heads]
            for h in heads:
                o_ref[r, lns[h]] = qs[h] + avs[h]
                vn_ref[r, lns[h]] = vns[h]
                states[h] = states[h] * el_ref[h, pl.ds(c, 1), :] + kvs[h]
        for h in heads:
            s_scr[h] = states[h]

    row = pl.BlockSpec((GRP, D), lambda i: (i, 0))
    return pl.pallas_call(
        body,
        grid=(N_GRP,),
        in_specs=[row, row, row, row, pl.BlockSpec((G_HEADS, GRP, CH), lambda i: (0, i, 0)),
                  pl.BlockSpec((G_HEADS, CPG, 128), lambda i: (0, i, 0))],
        out_specs=[row, row, pl.BlockSpec((G_HEADS, CPG, 128, 128), lambda i: (0, i, 0, 0))],
        out_shape=[
            jax.ShapeDtypeStruct((T, D), F32),
            jax.ShapeDtypeStruct((T, D), BF16),
            jax.ShapeDtypeStruct((G_HEADS, T // CH, 128, 128), F32),
        ],
        scratch_shapes=[pltpu.VMEM((G_HEADS, 128, 128), F32)],
        compiler_params=_cp("arbitrary"),
        name=name,
    )(u, w, qd, kd, at, el)


def gdn_rec_bwd(do, w, qd, kd, at, el, vn, st, name):
    def body(do_ref, w_ref, qd_ref, kd_ref, at_ref, el_ref, vn_ref, st_ref,
             du_ref, dw_ref, dqd_ref, dkd_ref, dat_ref, del_ref, ds_scr):
        @pl.when(pl.program_id(0) == 0)
        def _():
            ds_scr[...] = jnp.zeros_like(ds_scr)

        dstates = [ds_scr[h] for h in range(G_HEADS)]
        heads = range(G_HEADS)
        lns = [slice(h * 128, (h + 1) * 128) for h in heads]
        for c in reversed(range(CPG)):
            r = slice(c * CH, (c + 1) * CH)
            dsbs = [dstates[h].astype(BF16) for h in heads]
            sns = [st_ref[h, c] for h in heads]
            snbs = [sns[h].astype(BF16) for h in heads]
            dobs = [do_ref[r, lns[h]].astype(BF16) for h in heads]
            vnbs = [vn_ref[r, lns[h]] for h in heads]
            dvns = [(_dot(kd_ref[r, lns[h]], dsbs[h]) + _dot_tn(at_ref[h, r, :], dobs[h])).astype(BF16) for h in heads]
            dkds = [_dot_nt(vnbs[h], dsbs[h]) for h in heads]
            dqds = [_dot_nt(dobs[h], snbs[h]) for h in heads]
            dats = [_dot_nt(dobs[h], vnbs[h]) for h in heads]
            dws = [_dot_nt(dvns[h], snbs[h]) for h in heads]
            ups = [_dot_tn(qd_ref[r, lns[h]], dobs[h]) - _dot_tn(w_ref[r, lns[h]], dvns[h]) for h in heads]
            for h in heads:
                du_ref[r, lns[h]] = dvns[h]
                dkd_ref[r, lns[h]] = dkds[h]
                tot = jnp.sum(jnp.sum(dstates[h] * sns[h], axis=1, keepdims=True), axis=0, keepdims=True)
                del_ref[h, pl.ds(c, 1), :] = jnp.broadcast_to(tot, (1, 128))
                dqd_ref[r, lns[h]] = dqds[h]
                dat_ref[h, r, :] = dats[h]
                dw_ref[r, lns[h]] = (-dws[h]).astype(BF16)
                dstates[h] = dstates[h] * el_ref[h, pl.ds(c, 1), :] + ups[h]
        for h in heads:
            ds_scr[h] = dstates[h]

    last = N_GRP - 1
    row = pl.BlockSpec((GRP, D), lambda i: (last - i, 0))
    sq = pl.BlockSpec((G_HEADS, GRP, CH), lambda i: (0, last - i, 0))
    sc = pl.BlockSpec((G_HEADS, CPG, 128), lambda i: (0, last - i, 0))
    return pl.pallas_call(
        body,
        grid=(N_GRP,),
        in_specs=[row, row, row, row, sq, sc, row, pl.BlockSpec((G_HEADS, CPG, 128, 128), lambda i: (0, last - i, 0, 0))],
        out_specs=[row, row, row, row, sq, sc],
        out_shape=[
            jax.ShapeDtypeStruct((T, D), BF16),
            jax.ShapeDtypeStruct((T, D), BF16),
            jax.ShapeDtypeStruct((T, D), F32),
            jax.ShapeDtypeStruct((T, D), F32),
            jax.ShapeDtypeStruct((G_HEADS, T, CH), F32),
            jax.ShapeDtypeStruct((G_HEADS, T // CH, 128), F32),
        ],
        scratch_shapes=[pltpu.VMEM((G_HEADS, 128, 128), F32)],
        compiler_params=_cp("arbitrary"),
        name=name,
    )(do, w, qd, kd, at, el, vn, st)


def gdn_local_bwd(qkv, ba, alog, dtb, tinv, du, dw, dqd, dkd, dat, dl, name):
    def body(q_ref, k_ref, v_ref, ba_ref, al_ref, dt_ref, ti_ref, du_ref, dw_ref, dqd_ref, dkd_ref, dat_ref, dl_ref,
             dq_ref, dk_ref, dv_ref, dba_ref, dal_ref, ddt_ref, gcs):
        gi = pl.program_id(0)
        h = pl.program_id(1)
        beta, g, xa, ea = _gdn_gates(ba_ref, al_ref, dt_ref, h)
        gc = _seg_cumsum(jnp.broadcast_to(g, (GRP, 128)))
        gcs[...] = gc
        causal, strict, _, _ = _chunk_masks()
        dgc_l, dgl_l, dbeta_l, state = [], [], [], []
        for c in range(CPG):
            r = slice(c * CH, (c + 1) * CH)
            q, k, v = q_ref[r, :], k_ref[r, :], v_ref[r, :]
            bt = beta[r, :]
            gcc = gc[r, :]
            ec = jnp.exp(gcc)
            gl = gcs[pl.ds(c * CH + CH - 1, 1), :]
            f2 = jnp.exp(gl - gcc)
            elc = jnp.exp(gl)
            dm = _decay(gcc, causal)
            qs = q * G_SCALE
            kb = k * bt
            vb = v * bt
            kbe = kb * ec
            kbf, kbb, qsb = k.astype(BF16), kb.astype(BF16), qs.astype(BF16)
            a1 = _dot_nt(kbb, kbf)
            qk = _dot_nt(qsb, kbf)
            ti = ti_ref[r, :]
            tb = ti.astype(BF16)
            du_c, dw_c = du_ref[r, :], dw_ref[r, :]
            dqd_c, dkd_c, dat_c = dqd_ref[r, :], dkd_ref[r, :], dat_ref[r, :]

            dqs = dqd_c * ec
            d_e = jnp.sum(dqd_c * qs, axis=1, keepdims=True)
            dk = dkd_c * f2
            tcol = jnp.sum(dkd_c * k, axis=1, keepdims=True) * f2[:, 0:1]
            dgl = jnp.sum(tcol, axis=0, keepdims=True) + dl_ref[pl.ds(c, 1), 0:1] * elc[:, 0:1]
            dgc = -tcol
            dqk = (dat_c * dm).astype(BF16)
            d_d = dat_c * qk
            dqs = dqs + _dot(dqk, kbf)
            dk = dk + _dot_tn(dqk, qsb)
            dtinv = _dot_nt(du_c, vb.astype(BF16)) + _dot_nt(dw_c, kbe.astype(BF16))
            dvb = _dot_tn(tb, du_c)
            dkbe = _dot_tn(tb, dw_c)
            dq_ref[r, :] = dqs * G_SCALE
            state.append((ti.T, dtinv, dm, a1, dkbe, dvb, d_d, dk, d_e, dgc, dgl))

        xs = [_dot3(st[0], st[1]) for st in state]
        dlms = [jnp.where(strict, -_dot3(x, st[0]), 0.0) for x, st in zip(xs, state)]

        for c in range(CPG):
            r = slice(c * CH, (c + 1) * CH)
            _, _, dm, a1, dkbe, dvb, d_d, dk, d_e, dgc, dgl = state[c]
            dlm = dlms[c]
            k, v = k_ref[r, :], v_ref[r, :]
            bt = beta[r, :]
            ec = jnp.exp(gc[r, :])
            kb = k * bt
            kbf, kbb = k.astype(BF16), kb.astype(BF16)
            da1 = (dlm * dm).astype(BF16)
            d_d = d_d + dlm * a1
            dkb = _dot(da1, kbf) + dkbe * ec
            dk = dk + _dot_tn(da1, kbb)
            d_e = d_e + jnp.sum(dkbe * kb, axis=1, keepdims=True)
            dk = dk + dkb * bt
            dbeta_l.append(jnp.sum(dkb * k, axis=1, keepdims=True) + jnp.sum(dvb * v, axis=1, keepdims=True))
            ddiff = d_d * dm
            dgc = dgc + jnp.sum(ddiff, axis=1, keepdims=True) - jnp.sum(ddiff.T, axis=1, keepdims=True)
            dgc = dgc + d_e * ec[:, 0:1]
            dgc_l.append(dgc)
            dgl_l.append(jnp.broadcast_to(dgl, (CH, 1)))
            dk_ref[r, :] = dk
            dv_ref[r, :] = dvb * bt

        dgc_all = jnp.broadcast_to(jnp.concatenate(dgc_l, axis=0), (GRP, 128))
        dg = _seg_cumsum(dgc_all, reverse=True)[:, 0:1] + jnp.concatenate(dgl_l, axis=0)
        dbeta = jnp.concatenate(dbeta_l, axis=0)
        da = dg * (-ea) * _sigmoid(xa)
        db = dbeta * beta * (1.0 - beta)
        lane = lax.broadcasted_iota(jnp.int32, (GRP, 2 * G_HEADS), 1)
        dba = jnp.where(lane == h, db, 0.0) + jnp.where(lane == G_HEADS + h, da, 0.0)
        lane8 = lax.broadcasted_iota(jnp.int32, (1, G_HEADS), 1)
        dal = jnp.where(lane8 == h, jnp.sum(dg * g, axis=0, keepdims=True), 0.0)
        ddt = jnp.where(lane8 == h, jnp.sum(da, axis=0, keepdims=True), 0.0)

        @pl.when(h == 0)
        def _():
            dba_ref[...] = dba

        @pl.when(h > 0)
        def _():
            dba_ref[...] += dba

        @pl.when((h == 0) & (gi == 0))
        def _():
            dal_ref[...] = dal
            ddt_ref[...] = ddt

        @pl.when((h > 0) | (gi > 0))
        def _():
            dal_ref[...] += dal
            ddt_ref[...] += ddt

    hd = lambda off: pl.BlockSpec((GRP, 128), lambda i, h: (i, off + h))
    vec = pl.BlockSpec((1, G_HEADS), lambda i, h: (0, 0))
    sq = pl.BlockSpec((None, GRP, CH), lambda i, h: (h, i, 0))
    gates = pl.BlockSpec((GRP, 2 * G_HEADS), lambda i, h: (i, 0))
    return pl.pallas_call(
        body,
        grid=(N_GRP, G_HEADS),
        in_specs=[hd(0), hd(G_HEADS), hd(2 * G_HEADS), gates, vec, vec, sq, hd(0), hd(0), hd(0), hd(0), sq,
                  pl.BlockSpec((None, CPG, 128), lambda i, h: (h, i, 0))],
        out_specs=[hd(0), hd(0), hd(0), gates, vec, vec],
        out_shape=[
            jax.ShapeDtypeStruct((T, D), F32),
            jax.ShapeDtypeStruct((T, D), F32),
            jax.ShapeDtypeStruct((T, D), F32),
            jax.ShapeDtypeStruct((T, 2 * G_HEADS), F32),
            jax.ShapeDtypeStruct((1, G_HEADS), F32),
            jax.ShapeDtypeStruct((1, G_HEADS), F32),
        ],
        scratch_shapes=[pltpu.VMEM((GRP, 128), F32)],
        compiler_params=_cp("arbitrary", "arbitrary"),
        name=name,
    )(qkv, qkv, qkv, ba, alog, dtb, tinv, du, dw, dqd, dkd, dat, dl)


def _gated_norm(o, z, g):
    rstd = lax.rsqrt(jnp.mean(o * o, axis=-1, keepdims=True) + EPS)
    oh = o * rstd
    sig = _sigmoid(z)
    return oh, rstd, sig


def gdn_out_fwd(h, o, z, norm_g, w_out, name):
    tt = 512

    def body(h_ref, o_ref, z_ref, g_ref, w_ref, hnew_ref, cat_ref):
        g = g_ref[...]
        for hh in range(G_HEADS):
            ln = slice(hh * 128, (hh + 1) * 128)
            zz = z_ref[:, ln]
            oh, _, sig = _gated_norm(o_ref[:, ln], zz, g)
            cat_ref[:, ln] = (oh * g * (zz * sig)).astype(BF16)
        hnew_ref[...] = h_ref[...] + _dot(cat_ref[...], w_ref[...])

    full = pl.BlockSpec((tt, D), lambda i: (i, 0))
    return pl.pallas_call(
        body,
        grid=(T // tt,),
        in_specs=[full, full, full, pl.BlockSpec((1, 128), lambda i: (0, 0)), pl.BlockSpec((D, D), lambda i: (0, 0))],
        out_specs=[full, full],
        out_shape=[jax.ShapeDtypeStruct((T, D), F32), jax.ShapeDtypeStruct((T, D), BF16)],
        compiler_params=_cp("parallel"),
        name=name,
    )(h, o, z, norm_g, w_out)


def gdn_out_bwd(dres, o, z, norm_g, w_out, name):
    tt = 512

    def body(d_ref, o_ref, z_ref, g_ref, w_ref, do_ref, dz_ref, dg_ref, dcat):
        i = pl.program_id(0)
        g = g_ref[...]
        dcat[...] = _dot_nt(d_ref[...].astype(BF16), w_ref[...])
        dg = jnp.zeros((1, 128), F32)
        for hh in range(G_HEADS):
            ln = slice(hh * 128, (hh + 1) * 128)
            zz = z_ref[:, ln]
            oh, rstd, sig = _gated_norm(o_ref[:, ln], zz, g)
            dout = dcat[:, ln]
            dy = dout * (zz * sig)
            dz_ref[:, ln] = dout * (oh * g) * _dsilu(zz, sig)
            dg = dg + jnp.sum(dy * oh, axis=0, keepdims=True)
            doh = dy * g
            do_ref[:, ln] = rstd * (doh - oh * jnp.mean(doh * oh, axis=-1, keepdims=True))

        @pl.when(i == 0)
        def _():
            dg_ref[...] = dg

        @pl.when(i > 0)
        def _():
            dg_ref[...] += dg

    full = pl.BlockSpec((tt, D), lambda i: (i, 0))
    vec = pl.BlockSpec((1, 128), lambda i: (0, 0))
    return pl.pallas_call(
        body,
        grid=(T // tt,),
        in_specs=[full, full, full, vec, pl.BlockSpec((D, D), lambda i: (0, 0))],
        out_specs=[full, full, vec],
        out_shape=[jax.ShapeDtypeStruct((T, D), F32), jax.ShapeDtypeStruct((T, D), F32), jax.ShapeDtypeStruct((1, 128), F32)],
        scratch_shapes=[pltpu.VMEM((tt, D), F32)],
        compiler_params=_cp("arbitrary"),
        name=name,
    )(dres, o, z, norm_g, w_out)


GDN_SPLITS = [(0, 1024), (1024, 1024), (2048, 1024), (3072, 1024), (4096, 2 * G_HEADS)]


def gdn_fwd(h, g_row, w_in, conv_w, alog, dtb, norm_g, w_out, tag):
    hn, qkv_pre, z, ba = proj_fwd(h, g_row, w_in, [(0, G_QKV), (G_QKV, 1024), (4096, 2 * G_HEADS)], f"gdn_proj_{tag}")
    qkv = gdn_prep_fwd(qkv_pre, conv_w, f"gdn_prep_{tag}")
    u, w, qd, kd, at, el, tinv = gdn_local_fwd(qkv, ba, alog, dtb, f"gdn_local_{tag}")
    o, vn, st = gdn_rec_fwd(u, w, qd, kd, at, el, f"gdn_rec_{tag}")
    hnew, cat = gdn_out_fwd(h, o, z, norm_g, w_out, f"gdn_out_{tag}")
    return hnew, (h, hn, qkv_pre, z, ba, qkv, w, qd, kd, at, el, tinv, o, vn, st, cat)


def gdn_bwd(dres, saved, g_row, w_in, conv_w, alog, dtb, norm_g, w_out, tag):
    h, hn, qkv_pre, z, ba, qkv, w, qd, kd, at, el, tinv, o, vn, st, cat = saved
    d_o, d_z, d_ng = gdn_out_bwd(dres, o, z, norm_g, w_out, f"gdn_out_bwd_{tag}")
    d_wout = mm_tn_full(cat, dres, BF16, f"gdn_wout_grad_{tag}")
    du, dw, dqd, dkd, dat, dl = gdn_rec_bwd(d_o, w, qd, kd, at, el, vn, st, f"gdn_rec_bwd_{tag}")
    dq, dk, dv, dba, dal, ddt = gdn_local_bwd(qkv, ba, alog, dtb, tinv, du, dw, dqd, dkd, dat, dl, f"gdn_local_bwd_{tag}")
    dpre, dcw = [], []
    for part, d in enumerate((dq, dk, dv)):
        dx, dwc = gdn_prep_bwd(d, qkv_pre, conv_w, part, part < 2, f"gdn_prep_bwd_{tag}_{part}")
        dpre.append(dx)
        dcw.append(dwc)
    parts = dpre + [d_z, dba]
    dres_new, d_norm, dproj = proj_bwd_data(h, g_row, w_in, parts, GDN_SPLITS, dres, f"gdn_proj_bwd_{tag}")
    d_win = cols_to_slabs(mm_tn_full(hn, dproj, F32, f"gdn_win_grad_{tag}"), mm_tn(hn, dba, f"gdn_win_grad_ba_{tag}"),
                          f"gdn_win_slabs_{tag}")
    return dres_new, dict(norm=d_norm, w_in=d_win, conv_w=jnp.concatenate(dcw, axis=1), A_log=dal, dt_bias=ddt,
                          norm_g=d_ng, w_out=d_wout)


MESH = pl.DeviceIdType.MESH
ANY = pl.BlockSpec(memory_space=pl.ANY)


def _coords():
    return lax.axis_index("x"), lax.axis_index("y"), lax.axis_index("c")


def _slot(p):
    return 4 * p[0] + 2 * p[1] + p[2]


def all_gather(shards, name):
    k_n = len(shards)

    def body(*refs):
        srcs, dsts = refs[:k_n], refs[k_n:2 * k_n]
        send_sems, recv_sems, local_sems = refs[2 * k_n:]
        x, y, c = _coords()
        me, sibling = (x, y, c), (x, y, 1 - c)
        chips = [(1 - x, y), (x, 1 - y), (1 - x, 1 - y)]

        def copy(k, s, block, to, from_src=False):
            rows = dsts[k].at[_slot(block)]
            return pltpu.make_async_remote_copy(
                src_ref=srcs[k] if from_src else rows, dst_ref=rows,
                send_sem=send_sems.at[k, s], recv_sem=recv_sems.at[k, s], device_id=to, device_id_type=MESH)

        local = [pltpu.make_async_copy(srcs[k], dsts[k].at[_slot(me)], local_sems.at[k]) for k in range(k_n)]
        for cp in local:
            cp.start()
        first = []
        for k in range(k_n):
            first.append(copy(k, 0, me, sibling, True))
            first += [copy(k, 1 + j, me, (*chip, c), True) for j, chip in enumerate(chips)]
        for cp in first:
            cp.start()
        passed = []
        for j, chip in enumerate(chips):
            for k in range(k_n):
                copy(k, 1 + j, (*chip, c), me).wait_recv()
                fw = copy(k, 4 + j, (*chip, c), sibling)
                fw.start()
                passed.append(fw)
        for k in range(k_n):
            copy(k, 0, sibling, me).wait_recv()
            for j, chip in enumerate(chips):
                copy(k, 4 + j, (*chip, 1 - c), me).wait_recv()
        for cp in first + passed:
            cp.wait_send()
        for cp in local:
            cp.wait()

    return pl.pallas_call(
        body,
        in_specs=[ANY] * k_n,
        out_specs=[ANY] * k_n,
        out_shape=[jax.ShapeDtypeStruct((N_DEV,) + s.shape, s.dtype) for s in shards],
        scratch_shapes=[pltpu.SemaphoreType.DMA((k_n, 7)), pltpu.SemaphoreType.DMA((k_n, 7)),
                        pltpu.SemaphoreType.DMA((k_n,))],
        name=name,
    )(*shards)


HBM = pl.BlockSpec(memory_space=pltpu.HBM)
SEM = pl.BlockSpec(memory_space=pltpu.SEMAPHORE)
EFFECT = pltpu.SideEffectType.DATAFLOW_SIDE_EFFECTING


def _hbm(a):
    return pltpu.with_memory_space_constraint(a, pltpu.HBM)


def _peer_list(x, y, c):
    peers = []
    for j in range(1, N_DEV):
        jx, jy, jc = (j >> 2) & 1, (j >> 1) & 1, j & 1
        peers.append((x if jx == 0 else 1 - x, y if jy == 0 else 1 - y, c if jc == 0 else 1 - c))
    return peers


def _push_views(kind, layer, src_ref, land_ref, me, peer_slot):
    if kind == "gather":
        return src_ref, land_ref.at[me], land_ref.at[peer_slot]
    if layer is None:
        return src_ref.at[peer_slot], land_ref.at[me], land_ref.at[peer_slot]
    return src_ref.at[peer_slot], land_ref.at[me, layer], land_ref.at[peer_slot, layer]


def _push_copies(groups, srcs, lands, sems):
    x, y, c = _coords()
    me = _slot((x, y, c))
    peers = _peer_list(x, y, c)
    t = 0
    for gi, group in enumerate(groups):
        for ti, (kind, layer, _, li) in enumerate(group):
            for j, peer in enumerate(peers):
                out, there, here = _push_views(kind, layer, srcs[t], lands[li], me, _slot(peer))
                k = ti * (N_DEV - 1) + j
                yield out, there, here, sems[2 * gi].at[k], sems[2 * gi + 1].at[k], peer
            t += 1


def push_start(groups, lands, name, carry=()):
    flat = [it for g in groups for it in g]
    n, n_l, n_g, n_c = len(flat), len(lands), len(groups), len(carry)
    n_in = n + n_l + n_c

    def body(*refs):
        srcs, land_refs, sems = refs[:n], refs[n:n + n_l], refs[n_in:n_in + 2 * n_g]
        for out, there, _, s_sem, r_sem, peer in _push_copies(groups, srcs, land_refs, sems):
            pltpu.make_async_remote_copy(src_ref=out, dst_ref=there, send_sem=s_sem, recv_sem=r_sem,
                                         device_id=peer, device_id_type=MESH).start()

    arrays = [it[2] for it in flat] + list(lands) + list(carry)
    sem_shapes = []
    for g in groups:
        sem_shapes += [pltpu.SemaphoreType.DMA((len(g) * (N_DEV - 1),))] * 2
    outs = pl.pallas_call(
        body,
        name=name,
        in_specs=[HBM] * n_in,
        out_specs=[SEM] * (2 * n_g) + [HBM] * n_in,
        out_shape=sem_shapes + [pltpu.HBM(a.shape, a.dtype) for a in arrays],
        input_output_aliases={i: 2 * n_g + i for i in range(n_in)},
        compiler_params=pltpu.CompilerParams(has_side_effects=EFFECT),
    )(*[_hbm(a) for a in arrays])
    sems, thru = list(outs[:2 * n_g]), list(outs[2 * n_g:])
    return sems, thru[:n], thru[n:n + n_l], thru[n + n_l:]


def push_wait(groups, lands, sems, after, name):
    flat = [it for g in groups for it in g]
    n, n_l, n_g = len(flat), len(lands), len(groups)

    def body(*refs):
        srcs, land_refs, sem_refs = refs[:n], refs[n:n + n_l], refs[n + n_l:n + n_l + 2 * n_g]
        for out, _, here, s_sem, r_sem, peer in _push_copies(groups, srcs, land_refs, sem_refs):
            cp = pltpu.make_async_remote_copy(src_ref=out, dst_ref=here, send_sem=s_sem, recv_sem=r_sem,
                                              device_id=peer, device_id_type=MESH)
            cp.wait_send()
            cp.wait_recv()

    arrays = [it[2] for it in flat] + list(lands)
    outs = pl.pallas_call(
        body,
        name=name,
        in_specs=[HBM] * (n + n_l) + [SEM] * (2 * n_g) + [ANY],
        out_specs=[HBM] * (n + n_l),
        out_shape=[pltpu.HBM(a.shape, a.dtype) for a in arrays],
        input_output_aliases={i: i for i in range(n + n_l)},
        compiler_params=pltpu.CompilerParams(has_side_effects=EFFECT),
    )(*arrays, *sems, after)
    return list(outs[:n]), list(outs[n:])


def sum_slabs(parts, name):
    n, rows, cols = parts.shape

    def body(p_ref, o_ref):
        g = p_ref[0]
        for s in range(1, n):
            g = g + p_ref[s]
        o_ref[...] = g

    return pl.pallas_call(body, out_shape=jax.ShapeDtypeStruct((rows, cols), F32), name=name)(parts)


def _row_tile(rows, cols):
    if rows * cols * 4 <= (1 << 20) or rows % 8:
        return rows
    tr = rows
    while tr % 2 == 0 and (tr // 2) % 8 == 0 and tr * cols * 4 > (1 << 20):
        tr //= 2
    return tr


def adamw(parts, w, m, v, name):
    p_n = parts.shape[0]
    rows, cols = w.shape
    tr = _row_tile(rows, cols)

    def body(p_ref, w_ref, m_ref, v_ref, g_ref, d_ref, nm_ref, nv_ref):
        g = p_ref[0].astype(F32)
        for s in range(1, p_n):
            g = g + p_ref[s].astype(F32)
        g_ref[...] = g
        d_ref[...], nm_ref[...], nv_ref[...] = _adam_update(g, w_ref[...], m_ref[...], v_ref[...])

    blk = pl.BlockSpec((tr, cols), lambda i: (i, 0))
    return pl.pallas_call(
        body,
        grid=(rows // tr,),
        in_specs=[pl.BlockSpec((p_n, tr, cols), lambda i: (0, i, 0)), blk, blk, blk],
        out_specs=[blk] * 4,
        out_shape=[jax.ShapeDtypeStruct((rows, cols), F32)] * 4,
        compiler_params=_cp("parallel"),
        name=name,
    )(parts, w, m, v)


def _adam_update(g, w, m, v):
    m_new = ADAM_B1 * m + (1.0 - ADAM_B1) * g
    v_new = ADAM_B2 * v + (1.0 - ADAM_B2) * (g * g)
    m_hat = m_new / (1.0 - ADAM_B1 ** ADAM_STEP)
    v_hat = v_new / (1.0 - ADAM_B2 ** ADAM_STEP)
    return -ADAM_LR * (m_hat / (jnp.sqrt(v_hat) + ADAM_EPS) + ADAM_WD * w), m_new, v_new


def _adamw_nd(parts, w, m, v, name):
    p_n = parts.shape[0]
    n_l, rows, cols = w.shape
    tr = _row_tile(rows, cols)

    def body(p_ref, w_ref, m_ref, v_ref, g_ref, d_ref, nm_ref, nv_ref):
        g = p_ref[0].astype(F32)
        for s in range(1, p_n):
            g = g + p_ref[s].astype(F32)
        g_ref[...] = g
        d_ref[...], nm_ref[...], nv_ref[...] = _adam_update(g, w_ref[...], m_ref[...], v_ref[...])

    blk = pl.BlockSpec((None, tr, cols), lambda l, i: (l, i, 0))
    return pl.pallas_call(
        body,
        grid=(n_l, rows // tr),
        in_specs=[pl.BlockSpec((p_n, None, tr, cols), lambda l, i: (0, l, i, 0)), blk, blk, blk],
        out_specs=[blk] * 4,
        out_shape=[jax.ShapeDtypeStruct(w.shape, F32)] * 4,
        compiler_params=_cp("parallel", "parallel"),
        name=name,
    )(parts, w, m, v)


def slabs_to_cols(slabs, name):
    n, r, w = slabs.shape
    tr = 256 if r % 256 == 0 else r

    def body(s_ref, o_ref):
        for s in range(n):
            o_ref[:, w * s:w * (s + 1)] = s_ref[s]

    return pl.pallas_call(
        body,
        grid=(r // tr,),
        in_specs=[pl.BlockSpec((n, tr, w), lambda i: (0, i, 0))],
        out_specs=pl.BlockSpec((tr, n * w), lambda i: (i, 0)),
        out_shape=jax.ShapeDtypeStruct((r, n * w), slabs.dtype),
        compiler_params=_cp("parallel"),
        name=name,
    )(slabs)


FFN_IN = ("ffn1_w_in", "ffn2_w_in")
REPL = ["ffn1_norm", "mix_norm", "ffn2_norm", "hyb_dw_b", "hyb_ln_g", "hyb_ln_b", "gdn_A_log", "gdn_dt_bias",
        "gdn_norm_g", "final_norm"]
WEIGHTS = ["ffn1_norm", "ffn1_w_in", "ffn1_w_out", "mix_norm", "ffn2_norm", "ffn2_w_in", "ffn2_w_out", "hyb_w_in",
           "hyb_dw_w", "hyb_dw_b", "hyb_ln_g", "hyb_ln_b", "hyb_w_out", "gdn_w_in", "gdn_conv_w", "gdn_A_log",
           "gdn_dt_bias", "gdn_norm_g", "gdn_w_out", "final_norm"]


def _pack(arrs, rows):
    flat = jnp.concatenate([a.reshape(-1) for a in arrs])
    return jnp.pad(flat, (0, rows * 128 - flat.shape[0])).reshape(rows, 128)


def kernel(x, positions, ffn1_norm, ffn1_w_in, ffn1_w_out, mix_norm, ffn2_norm, ffn2_w_in, ffn2_w_out, hyb_w_in, hyb_dw_w, hyb_dw_b, hyb_ln_g, hyb_ln_b, hyb_w_out, gdn_w_in, gdn_conv_w, gdn_A_log, gdn_dt_bias, gdn_norm_g, gdn_w_out, final_norm, loss_target, m_ffn1_norm, m_ffn1_w_in, m_ffn1_w_out, m_mix_norm, m_ffn2_norm, m_ffn2_w_in, m_ffn2_w_out, m_hyb_w_in, m_hyb_dw_w, m_hyb_dw_b, m_hyb_ln_g, m_hyb_ln_b, m_hyb_w_out, m_gdn_w_in, m_gdn_conv_w, m_gdn_A_log, m_gdn_dt_bias, m_gdn_norm_g, m_gdn_w_out, m_final_norm, v_ffn1_norm, v_ffn1_w_in, v_ffn1_w_out, v_mix_norm, v_ffn2_norm, v_ffn2_w_in, v_ffn2_w_out, v_hyb_w_in, v_hyb_dw_w, v_hyb_dw_b, v_hyb_ln_g, v_hyb_ln_b, v_hyb_w_out, v_gdn_w_in, v_gdn_conv_w, v_gdn_A_log, v_gdn_dt_bias, v_gdn_norm_g, v_gdn_w_out, v_final_norm):
    w = dict(ffn1_norm=ffn1_norm, ffn1_w_in=ffn1_w_in, ffn1_w_out=ffn1_w_out, mix_norm=mix_norm, ffn2_norm=ffn2_norm,
             ffn2_w_in=ffn2_w_in, ffn2_w_out=ffn2_w_out, hyb_w_in=hyb_w_in, hyb_dw_w=hyb_dw_w, hyb_dw_b=hyb_dw_b,
             hyb_ln_g=hyb_ln_g, hyb_ln_b=hyb_ln_b, hyb_w_out=hyb_w_out, gdn_w_in=gdn_w_in, gdn_conv_w=gdn_conv_w,
             gdn_A_log=gdn_A_log, gdn_dt_bias=gdn_dt_bias, gdn_norm_g=gdn_norm_g, gdn_w_out=gdn_w_out,
             final_norm=final_norm)
    mom = dict(ffn1_norm=m_ffn1_norm, ffn1_w_in=m_ffn1_w_in, ffn1_w_out=m_ffn1_w_out, mix_norm=m_mix_norm,
               ffn2_norm=m_ffn2_norm, ffn2_w_in=m_ffn2_w_in, ffn2_w_out=m_ffn2_w_out, hyb_w_in=m_hyb_w_in,
               hyb_dw_w=m_hyb_dw_w, hyb_dw_b=m_hyb_dw_b, hyb_ln_g=m_hyb_ln_g, hyb_ln_b=m_hyb_ln_b,
               hyb_w_out=m_hyb_w_out, gdn_w_in=m_gdn_w_in, gdn_conv_w=m_gdn_conv_w, gdn_A_log=m_gdn_A_log,
               gdn_dt_bias=m_gdn_dt_bias, gdn_norm_g=m_gdn_norm_g, gdn_w_out=m_gdn_w_out, final_norm=m_final_norm)
    var = dict(ffn1_norm=v_ffn1_norm, ffn1_w_in=v_ffn1_w_in, ffn1_w_out=v_ffn1_w_out, mix_norm=v_mix_norm,
               ffn2_norm=v_ffn2_norm, ffn2_w_in=v_ffn2_w_in, ffn2_w_out=v_ffn2_w_out, hyb_w_in=v_hyb_w_in,
               hyb_dw_w=v_hyb_dw_w, hyb_dw_b=v_hyb_dw_b, hyb_ln_g=v_hyb_ln_g, hyb_ln_b=v_hyb_ln_b,
               hyb_w_out=v_hyb_w_out, gdn_w_in=v_gdn_w_in, gdn_conv_w=v_gdn_conv_w, gdn_A_log=v_gdn_A_log,
               gdn_dt_bias=v_gdn_dt_bias, gdn_norm_g=v_gdn_norm_g, gdn_w_out=v_gdn_w_out, final_norm=v_final_norm)
    xi, yi, ci = _coords()
    me = 4 * xi + 2 * yi + ci
    for group in (w, mom, var):
        for n in FFN_IN:
            group[n] = jnp.swapaxes(group[n], 1, 2)

    big = ["ffn1_w_in", "ffn1_w_out", "ffn2_w_in", "ffn2_w_out", "hyb_w_in", "hyb_w_out", "gdn_w_in", "gdn_w_out"]
    ag_groups, ag_lands = [], []

    def add_group(shards):
        group = []
        for s in shards:
            land = lax.dynamic_update_slice(lax.empty((N_DEV,) + s.shape, s.dtype), s[None], (me,) + (0,) * s.ndim)
            group.append(("gather", None, s, len(ag_lands)))
            ag_lands.append(land)
        ag_groups.append(group)

    first = all_gather([w["ffn1_w_in"][0].astype(BF16), ffn1_w_out[0].astype(BF16)], "weights_gather_first")
    for l in range(DEPTH):
        i = l // 2
        if l == 0:
            ag_groups.append([])
        else:
            add_group([w["ffn1_w_in"][l].astype(BF16), ffn1_w_out[l].astype(BF16)])
        if l % 2 == 0:
            add_group([hyb_w_in[i].astype(BF16), hyb_w_out[i].astype(BF16), hyb_dw_w[i]])
        else:
            add_group([gdn_w_in[i].astype(BF16), gdn_w_out[i].astype(BF16), gdn_conv_w[i]])
        add_group([w["ffn2_w_in"][l].astype(BF16), ffn2_w_out[l].astype(BF16)])
    ag_sems, ag_srcs, ag_lands, first = push_start(ag_groups[1:], ag_lands, "weights_gather_start", carry=first)
    ag_sems = [None, None] + ag_sems

    def fetch(gi, after):
        if gi == 0:
            return first
        group = ag_groups[gi]
        base = sum(len(g) for g in ag_groups[:gi])
        items = [(kind, layer, ag_srcs[base + t], t) for t, (kind, layer, _, _) in enumerate(group)]
        lands = [ag_lands[li] for _, _, _, li in group]
        return push_wait([items], lands, ag_sems[2 * gi:2 * gi + 2], after, f"weights_gather_wait_{gi}")[1]

    row = lambda a: a.reshape(1, -1)

    rope = make_rope(positions)
    h = x[0]
    saved = []
    for l in range(DEPTH):
        i = l // 2
        rec = {"h1": h}
        wi, wo = fetch(3 * l, h)
        rec["w1"] = (wi.reshape(2, FFN_TILES, FFN_SHARD, D), wo)
        h, rec["hn1"], rec["a1"], rec["b1"] = ffn_fwd(h, row(ffn1_norm[l]), *rec["w1"], l, "1")
        mi, mo, mc = fetch(3 * l + 1, h)
        if l % 2 == 0:
            rec["wm"] = (slabs_to_cols(mi, f"hyb_w_in_cols_{i}"),
                         jnp.pad(slabs_to_cols(mc, f"hyb_dw_w_cols_{i}"), ((0, 1), (0, 0))), mo.reshape(D, D))
            w_in_f, dw_f, w_out_f = rec["wm"]
            h, rec["mix"] = hybrid_fwd(h, row(mix_norm[l]), w_in_f, dw_f, row(hyb_dw_b[i]), row(hyb_ln_g[i]),
                                       row(hyb_ln_b[i]), w_out_f, rope, str(i))
        else:
            rec["wm"] = (slabs_to_cols(mi, f"gdn_w_in_cols_{i}"), slabs_to_cols(mc, f"gdn_conv_w_cols_{i}"),
                         mo.reshape(D, D))
            w_in_f, cw_f, w_out_f = rec["wm"]
            h, rec["mix"] = gdn_fwd(h, row(mix_norm[l]), w_in_f, cw_f, row(gdn_A_log[i]), row(gdn_dt_bias[i]),
                                    row(gdn_norm_g[i]), w_out_f, str(i))
        rec["h2"] = h
        wi, wo = fetch(3 * l + 2, h)
        rec["w2"] = (wi.reshape(2, FFN_TILES, FFN_SHARD, D), wo)
        h, rec["hn2"], rec["a2"], rec["b2"] = ffn_fwd(h, row(ffn2_norm[l]), *rec["w2"], l, "2")
        saved.append(rec)
    dres, d_final, loss_acc = final_loss(h, row(final_norm), loss_target[0])

    ge_land = {n: lax.empty((N_DEV,) + w[n].shape, BF16) for n in big}
    ge_pending = []

    def send(named, layer, tag, carry):
        lands = [ge_land[n] for n, _ in named]
        group = [("scatter", layer, s, t) for t, (_, s) in enumerate(named)]
        sems, srcs, lands_out, carried = push_start([group], lands, f"grad_send_{tag}", carry=[carry])
        for (n, _), land in zip(named, lands_out):
            ge_land[n] = land
        ge_pending.append(([(n, layer, s) for (n, _), s in zip(named, srcs)], sems))
        return carried[0]

    gsmall = {n: [None] * (DEPTH if n in ("ffn1_norm", "mix_norm", "ffn2_norm") else 2) for n in REPL[:-1]}
    gsmall["hyb_dw_w"] = [None, None]
    gsmall["gdn_conv_w"] = [None, None]
    for l in reversed(range(DEPTH)):
        i = l // 2
        rec = saved[l]
        dhn, dwin, dwout = ffn_bwd(rec["hn2"], rec["a2"], rec["b2"], dres, *rec["w2"], l, "2")
        dhn = send([("ffn2_w_in", dwin.reshape(N_DEV, FFN_SHARD, D)),
                    ("ffn2_w_out", dwout.reshape(N_DEV, FFN_SHARD // 2, D))], l, f"ffn2_{l}", dhn)
        dres, dg = norm_bwd(rec["h2"], row(ffn2_norm[l]), dhn, dres, f"ffn2_norm_bwd_{l}")
        gsmall["ffn2_norm"][l] = dg
        if l % 2 == 0:
            w_in_f, dw_f, w_out_f = rec["wm"]
            dres, gr = hybrid_bwd(dres, rec["mix"], row(mix_norm[l]), w_in_f, dw_f, row(hyb_ln_g[i]),
                                  row(hyb_ln_b[i]), w_out_f, rope, str(i))
            dres = send([("hyb_w_in", gr["w_in"]), ("hyb_w_out", gr["w_out"].reshape(N_DEV, D // N_DEV, D))],
                        i, f"hyb_{i}", dres)
            for n in ("dw_w", "dw_b", "ln_g", "ln_b"):
                gsmall["hyb_" + n][i] = gr[n]
        else:
            w_in_f, cw_f, w_out_f = rec["wm"]
            dres, gr = gdn_bwd(dres, rec["mix"], row(mix_norm[l]), w_in_f, cw_f, row(gdn_A_log[i]),
                               row(gdn_dt_bias[i]), row(gdn_norm_g[i]), w_out_f, str(i))
            dres = send([("gdn_w_in", gr["w_in"]), ("gdn_w_out", gr["w_out"].reshape(N_DEV, D // N_DEV, D))],
                        i, f"gdn_{i}", dres)
            for n in ("conv_w", "A_log", "dt_bias", "norm_g"):
                gsmall["gdn_" + n][i] = gr[n]
        gsmall["mix_norm"][l] = gr["norm"]
        dhn, dwin, dwout = ffn_bwd(rec["hn1"], rec["a1"], rec["b1"], dres, *rec["w1"], l, "1")
        dhn = send([("ffn1_w_in", dwin.reshape(N_DEV, FFN_SHARD, D)),
                    ("ffn1_w_out", dwout.reshape(N_DEV, FFN_SHARD // 2, D))], l, f"ffn1_{l}", dhn)
        dres, dg = norm_bwd(rec["h1"], row(ffn1_norm[l]), dhn, dres, f"ffn1_norm_bwd_{l}")
        gsmall["ffn1_norm"][l] = dg
    grad_x = dres[None]

    n_repl_rows = 136
    small_rows = 576
    repl_flat = jnp.concatenate([jnp.concatenate([a.reshape(-1) for a in gsmall[n]]) for n in REPL[:-1]]
                                + [d_final.reshape(-1), loss_acc[0, 0:1]])
    loss_at = repl_flat.shape[0] - 1
    repl_pack = jnp.pad(repl_flat, (0, n_repl_rows * 128 - repl_flat.shape[0]))
    small_pack = jnp.concatenate([repl_pack] + [a.reshape(-1) for a in gsmall["hyb_dw_w"]]
                                 + [a.reshape(-1) for a in gsmall["gdn_conv_w"]]).reshape(small_rows, 128)

    own = {n: {} for n in big}

    def wait_for(pending, names, after, name):
        groups = [[("scatter", layer, s, names.index(n)) for n, layer, s in named] for named, _ in pending]
        sems = [s for _, pair in pending for s in pair]
        srcs_out, lands_out = push_wait(groups, [ge_land[n] for n in names], sems, after, name)
        flat_named = [it for named, _ in pending for it in named]
        for (n, layer, _), s in zip(flat_named, srcs_out):
            own[n][layer] = lax.dynamic_index_in_dim(s, me, 0, keepdims=False)
        for n, land in zip(names, lands_out):
            ge_land[n] = land

    def with_own(n, land):
        mine = jnp.stack([own[n][k] for k in range(len(own[n]))])
        return lax.dynamic_update_slice(land, mine[None], (me,) + (0,) * mine.ndim)

    out = {}
    last = ["ffn1_w_in", "ffn1_w_out"]
    wait_for(ge_pending[:-1], big, dres, "grad_wait_a")
    for n in big:
        if n not in last:
            out[n] = _adamw_nd(with_own(n, ge_land[n]), w[n], mom[n], var[n], f"adamw_{n}")
    pin = sum(out[n][1].reshape(-1)[0] for n in big if n not in last) * 0.0
    small_all, = all_gather([small_pack + pin], "small_grads_all_gather")
    g_small = sum_slabs(small_all, "small_grads_sum")
    loss = g_small.reshape(-1)[loss_at]
    wait_for(ge_pending[-1:], last, g_small, "grad_wait_b")
    for n in last:
        out[n] = _adamw_nd(with_own(n, ge_land[n]), w[n], mom[n], var[n], f"adamw_{n}")

    pk = lambda d: _pack([d[n] for n in REPL], n_repl_rows)
    res = adamw(g_small[:n_repl_rows][None], pk(w), pk(mom), pk(var), "adamw_replicated")
    off = 0
    for n in REPL:
        sz = w[n].size
        out[n] = [r.reshape(-1)[off:off + sz].reshape(w[n].shape) for r in res]
        off += sz
    g_dw = g_small[n_repl_rows:n_repl_rows + 248].reshape(2, CONV_K, CONV_C)
    g_dw = lax.dynamic_slice_in_dim(g_dw, me * (CONV_C // N_DEV), CONV_C // N_DEV, axis=2)
    out["hyb_dw_w"] = _adamw_nd(g_dw[None], w["hyb_dw_w"], mom["hyb_dw_w"], var["hyb_dw_w"], "adamw_hyb_dw_w")
    g_cw = g_small[n_repl_rows + 248:].reshape(2, G_CONV, G_QKV)
    g_cw = lax.dynamic_slice_in_dim(g_cw, me * (G_QKV // N_DEV), G_QKV // N_DEV, axis=2)
    out["gdn_conv_w"] = _adamw_nd(g_cw[None], w["gdn_conv_w"], mom["gdn_conv_w"], var["gdn_conv_w"], "adamw_gdn_conv_w")

    for n in FFN_IN:
        out[n] = [jnp.swapaxes(o, 1, 2) for o in out[n]]
    return (loss, grad_x, *[out[n][0] for n in WEIGHTS], *[out[n][1] for n in WEIGHTS],
            *[out[n][2] for n in WEIGHTS], *[out[n][3] for n in WEIGHTS])
```
